```python
import jax, jax.numpy as jnp
from jax import lax
import numpy as np

D_MODEL = 2048
BATCH = 8
SEQ = 4096
DEPTH = 1

MIX_WIDTH = D_MODEL
HEAD_DIM = 128
DSWA_HEADS = (MIX_WIDTH // 2) // HEAD_DIM
DSWA_CONFIGS = ((128, 1), (512, 4), (2048, 16))
ROT_DIM = HEAD_DIM // 4
ROPE_THETA = 500000.0
MLA_HEADS = (MIX_WIDTH // 2) // HEAD_DIM
Q_LORA_RANK = 512
KV_LORA_RANK = 512
QK_NOPE_DIM = 128
QK_ROPE_DIM = 64
V_HEAD_DIM = 128
D_FF = 4 * D_MODEL
Q_BLOCK = 128
NORM_EPS = 1e-6
NEG_INF = -1e30

A_WIDTH = DSWA_HEADS * HEAD_DIM
IN_SPLITS = (A_WIDTH, 2 * A_WIDTH, 3 * A_WIDTH,
             3 * A_WIDTH + Q_LORA_RANK,
             3 * A_WIDTH + Q_LORA_RANK + KV_LORA_RANK)
IN_COLS = 3 * A_WIDTH + Q_LORA_RANK + KV_LORA_RANK + QK_ROPE_DIM
OUT_ROWS = DSWA_HEADS * HEAD_DIM + MLA_HEADS * V_HEAD_DIM

kernel_name = "hybrid_dilated_swa_mla_sandwich"


def _rmsnorm(x, gain):
    xf = x.astype(jnp.float32)
    xf = xf * lax.rsqrt(jnp.mean(xf * xf, axis=-1, keepdims=True) + NORM_EPS)
    return xf.astype(x.dtype) * gain


def _rope(x, positions, rot_dim):
    inv_freq = ROPE_THETA ** (-jnp.arange(0, rot_dim, 2, dtype=jnp.float32) / rot_dim)
    ang = positions.astype(jnp.float32)[..., None] * inv_freq
    cos = jnp.cos(ang)[:, :, None, :]
    sin = jnp.sin(ang)[:, :, None, :]
    xr = x[..., :rot_dim].astype(jnp.float32)
    x1, x2 = xr[..., : rot_dim // 2], xr[..., rot_dim // 2:]
    rot = jnp.concatenate([x1 * cos - x2 * sin, x2 * cos + x1 * sin], axis=-1)
    return jnp.concatenate([rot.astype(x.dtype), x[..., rot_dim:]], axis=-1)


def _dilated_window_attention(q, k, v, window, dilation):
    B, S, H, D = q.shape
    steps = window // dilation
    span = dilation * Q_BLOCK
    s_pad = -(-S // span) * span
    pad = ((0, 0), (0, s_pad - S), (0, 0), (0, 0))
    q, k, v = jnp.pad(q, pad), jnp.pad(k, pad), jnp.pad(v, pad)
    nb = s_pad // span
    qb = q.reshape(B, nb, Q_BLOCK, dilation, H, D)
    kb = k.reshape(B, nb, Q_BLOCK, dilation, H, D)
    vb = v.reshape(B, nb, Q_BLOCK, dilation, H, D)

    def with_prev(t):
        prev = jnp.pad(t[:, :-1], ((0, 0), (1, 0), (0, 0), (0, 0), (0, 0), (0, 0)))
        return jnp.concatenate([prev, t], axis=2)

    kc, vc = with_prev(kb), with_prev(vb)
    s = jnp.einsum('bniphd,bnjphd->bnphij', qb, kc).astype(jnp.float32) * (D ** -0.5)
    i = jnp.arange(Q_BLOCK)[:, None]
    j = jnp.arange(2 * Q_BLOCK)[None, :]
    dist = i + Q_BLOCK - j
    band = (dist >= 0) & (dist <= steps)
    first = (jnp.arange(nb) == 0)[:, None, None] & (j < Q_BLOCK)[None]
    mask = band[None] & ~first
    s = jnp.where(mask[None, :, None, None], s, NEG_INF)
    m = jnp.max(s, axis=-1, keepdims=True)
    p = jnp.exp(s - m)
    denom = jnp.sum(p, axis=-1, keepdims=True)
    o = jnp.einsum('bnphij,bnjphd->bniphd', (p / denom).astype(v.dtype), vc)
    lse = (m + jnp.log(denom))[..., 0]
    o = o.reshape(B, s_pad, H, D)[:, :S]
    lse = jnp.transpose(lse, (0, 1, 4, 2, 3)).reshape(B, s_pad, H)[:, :S]
    return o, lse


def _causal_block_attention(q, k, v, scale):
    B, S, H, Dk = q.shape
    nb = S // Q_BLOCK
    qb = q.reshape(B, nb, Q_BLOCK, H, Dk).transpose(1, 0, 2, 3, 4)
    key_pos = jnp.arange(S)

    def one_block(args):
        n, qn = args
        s = jnp.einsum('bqhd,bkhd->bhqk', qn, k).astype(jnp.float32) * scale
        q_pos = n * Q_BLOCK + jnp.arange(Q_BLOCK)
        s = jnp.where((key_pos[None, :] <= q_pos[:, None])[None, None], s, NEG_INF)
        p = jax.nn.softmax(s, axis=-1)
        return jnp.einsum('bhqk,bkhd->bqhd', p.astype(v.dtype), v)

    o = lax.map(one_block, (jnp.arange(nb), qb))
    return o.transpose(1, 0, 2, 3, 4).reshape(B, S, H, v.shape[-1])


def _fwd_setup_inputs(seed: int = 0) -> dict:
    key = jax.random.key(seed)
    ks = jax.random.split(key, 16)
    f32 = jnp.float32

    def w(k, shape, fan_in):
        return jax.random.normal(k, shape, f32) * (fan_in ** -0.5)

    def gain(k, n):
        return 1.0 + 0.05 * jax.random.normal(k, (DEPTH, n), f32)

    x = jax.random.normal(ks[0], (BATCH, SEQ, D_MODEL), f32)
    offset = jax.random.randint(ks[1], (BATCH, 1), 0, 2048, dtype=jnp.int32)
    positions = offset + jnp.arange(SEQ, dtype=jnp.int32)[None, :]
    return {
        "x": x,
        "positions": positions,
        "norm_attn_pre": gain(ks[2], D_MODEL),
        "norm_attn_post": gain(ks[3], D_MODEL),
        "w_in": w(ks[4], (DEPTH, D_MODEL, IN_COLS), D_MODEL),
        "q_latent_norm": gain(ks[5], Q_LORA_RANK),
        "kv_latent_norm": gain(ks[6], KV_LORA_RANK),
        "w_uq": w(ks[7], (DEPTH, Q_LORA_RANK, MLA_HEADS * (QK_NOPE_DIM + QK_ROPE_DIM)), Q_LORA_RANK),
        "w_ukv": w(ks[8], (DEPTH, KV_LORA_RANK, MLA_HEADS * (QK_NOPE_DIM + V_HEAD_DIM)), KV_LORA_RANK),
        "w_out": w(ks[9], (DEPTH, OUT_ROWS, D_MODEL), OUT_ROWS),
        "norm_mlp_pre": gain(ks[10], D_MODEL),
        "norm_mlp_post": gain(ks[11], D_MODEL),
        "w_up": w(ks[12], (DEPTH, D_MODEL, D_FF), D_MODEL),
        "w_down": w(ks[13], (DEPTH, D_FF, D_MODEL), D_FF),
    }


def _fwd_reference(x, positions, norm_attn_pre, norm_attn_post, w_in, q_latent_norm,
              kv_latent_norm, w_uq, w_ukv, w_out, norm_mlp_pre, norm_mlp_post,
              w_up, w_down):
    B, S, _ = x.shape
    for layer in range(DEPTH):
        h = _rmsnorm(x, norm_attn_pre[layer])
        proj = h @ w_in[layer]
        a_q, a_k, a_v, c_q, c_kv, k_r = jnp.split(proj, IN_SPLITS, axis=-1)

        a_q = _rope(a_q.reshape(B, S, DSWA_HEADS, HEAD_DIM), positions, ROT_DIM)
        a_k = _rope(a_k.reshape(B, S, DSWA_HEADS, HEAD_DIM), positions, ROT_DIM)
        a_v = a_v.reshape(B, S, DSWA_HEADS, HEAD_DIM)
        outs, lses = [], []
        for window, dilation in DSWA_CONFIGS:
            o, lse = _dilated_window_attention(a_q, a_k, a_v, window, dilation)
            outs.append(o)
            lses.append(lse)
        alpha = jax.nn.softmax(jnp.stack(lses, axis=0), axis=0)
        a_out = jnp.sum(alpha[..., None].astype(a_v.dtype) * jnp.stack(outs, axis=0), axis=0)

        c_q = _rmsnorm(c_q, q_latent_norm[layer])
        q_b = (c_q @ w_uq[layer]).reshape(B, S, MLA_HEADS, QK_NOPE_DIM + QK_ROPE_DIM)
        q_nope, q_rope = q_b[..., :QK_NOPE_DIM], q_b[..., QK_NOPE_DIM:]
        q_rope = _rope(q_rope, positions, QK_ROPE_DIM)
        c_kv = _rmsnorm(c_kv, kv_latent_norm[layer])
        kv = (c_kv @ w_ukv[layer]).reshape(B, S, MLA_HEADS, QK_NOPE_DIM + V_HEAD_DIM)
        k_nope, v_b = kv[..., :QK_NOPE_DIM], kv[..., QK_NOPE_DIM:]
        k_rope = _rope(k_r[:, :, None, :], positions, QK_ROPE_DIM)
        q_full = jnp.concatenate([q_nope, q_rope], axis=-1)
        k_full = jnp.concatenate(
            [k_nope, jnp.broadcast_to(k_rope, (B, S, MLA_HEADS, QK_ROPE_DIM))], axis=-1)
        b_out = _causal_block_attention(q_full, k_full, v_b,
                                        (QK_NOPE_DIM + QK_ROPE_DIM) ** -0.5)

        mixed = jnp.concatenate([a_out.reshape(B, S, DSWA_HEADS * HEAD_DIM),
                                 b_out.reshape(B, S, MLA_HEADS * V_HEAD_DIM)], axis=-1)
        x = x + _rmsnorm(mixed @ w_out[layer], norm_attn_post[layer])

        h = _rmsnorm(x, norm_mlp_pre[layer])
        u = jnp.square(jax.nn.relu(h @ w_up[layer]))
        x = x + _rmsnorm(u @ w_down[layer], norm_mlp_post[layer])
    return x


import jax as _jax
import jax.numpy as _jnp

TWIN_FORMAT = 'train_step'
FWD_PARAMS = ['x', 'positions', 'norm_attn_pre', 'norm_attn_post', 'w_in', 'q_latent_norm', 'kv_latent_norm', 'w_uq', 'w_ukv', 'w_out', 'norm_mlp_pre', 'norm_mlp_post', 'w_up', 'w_down']
TWIN_WEIGHTS = ['norm_attn_pre', 'norm_attn_post', 'w_in', 'q_latent_norm', 'kv_latent_norm', 'w_uq', 'w_ukv', 'w_out', 'norm_mlp_pre', 'norm_mlp_post', 'w_up', 'w_down']
TWIN_DIFF_INPUT = 'x'
TWIN_INPUTS = ['x', 'positions', 'norm_attn_pre', 'norm_attn_post', 'w_in', 'q_latent_norm', 'kv_latent_norm', 'w_uq', 'w_ukv', 'w_out', 'norm_mlp_pre', 'norm_mlp_post', 'w_up', 'w_down', 'loss_target', 'm_norm_attn_pre', 'm_norm_attn_post', 'm_w_in', 'm_q_latent_norm', 'm_kv_latent_norm', 'm_w_uq', 'm_w_ukv', 'm_w_out', 'm_norm_mlp_pre', 'm_norm_mlp_post', 'm_w_up', 'm_w_down', 'v_norm_attn_pre', 'v_norm_attn_post', 'v_w_in', 'v_q_latent_norm', 'v_kv_latent_norm', 'v_w_uq', 'v_w_ukv', 'v_w_out', 'v_norm_mlp_pre', 'v_norm_mlp_post', 'v_w_up', 'v_w_down']
TWIN_OUTPUTS = ['loss', 'grad_x', 'grad_norm_attn_pre', 'grad_norm_attn_post', 'grad_w_in', 'grad_q_latent_norm', 'grad_kv_latent_norm', 'grad_w_uq', 'grad_w_ukv', 'grad_w_out', 'grad_norm_mlp_pre', 'grad_norm_mlp_post', 'grad_w_up', 'grad_w_down', 'delta_norm_attn_pre', 'delta_norm_attn_post', 'delta_w_in', 'delta_q_latent_norm', 'delta_kv_latent_norm', 'delta_w_uq', 'delta_w_ukv', 'delta_w_out', 'delta_norm_mlp_pre', 'delta_norm_mlp_post', 'delta_w_up', 'delta_w_down', 'new_m_norm_attn_pre', 'new_m_norm_attn_post', 'new_m_w_in', 'new_m_q_latent_norm', 'new_m_kv_latent_norm', 'new_m_w_uq', 'new_m_w_ukv', 'new_m_w_out', 'new_m_norm_mlp_pre', 'new_m_norm_mlp_post', 'new_m_w_up', 'new_m_w_down', 'new_v_norm_attn_pre', 'new_v_norm_attn_post', 'new_v_w_in', 'new_v_q_latent_norm', 'new_v_kv_latent_norm', 'new_v_w_uq', 'new_v_w_ukv', 'new_v_w_out', 'new_v_norm_mlp_pre', 'new_v_norm_mlp_post', 'new_v_w_up', 'new_v_w_down']
TWIN_LEAF_KINDS = {'loss': 'loss', 'grad_x': 'grad_x', 'grad_norm_attn_pre': 'grad_w', 'grad_norm_attn_post': 'grad_w', 'grad_w_in': 'grad_w', 'grad_q_latent_norm': 'grad_w', 'grad_kv_latent_norm': 'grad_w', 'grad_w_uq': 'grad_w', 'grad_w_ukv': 'grad_w', 'grad_w_out': 'grad_w', 'grad_norm_mlp_pre': 'grad_w', 'grad_norm_mlp_post': 'grad_w', 'grad_w_up': 'grad_w', 'grad_w_down': 'grad_w', 'delta_norm_attn_pre': 'delta_w', 'delta_norm_attn_post': 'delta_w', 'delta_w_in': 'delta_w', 'delta_q_latent_norm': 'delta_w', 'delta_kv_latent_norm': 'delta_w', 'delta_w_uq': 'delta_w', 'delta_w_ukv': 'delta_w', 'delta_w_out': 'delta_w', 'delta_norm_mlp_pre': 'delta_w', 'delta_norm_mlp_post': 'delta_w', 'delta_w_up': 'delta_w', 'delta_w_down': 'delta_w', 'new_m_norm_attn_pre': 'new_m', 'new_m_norm_attn_post': 'new_m', 'new_m_w_in': 'new_m', 'new_m_q_latent_norm': 'new_m', 'new_m_kv_latent_norm': 'new_m', 'new_m_w_uq': 'new_m', 'new_m_w_ukv': 'new_m', 'new_m_w_out': 'new_m', 'new_m_norm_mlp_pre': 'new_m', 'new_m_norm_mlp_post': 'new_m', 'new_m_w_up': 'new_m', 'new_m_w_down': 'new_m', 'new_v_norm_attn_pre': 'new_v', 'new_v_norm_attn_post': 'new_v', 'new_v_w_in': 'new_v', 'new_v_q_latent_norm': 'new_v', 'new_v_kv_latent_norm': 'new_v', 'new_v_w_uq': 'new_v', 'new_v_w_ukv': 'new_v', 'new_v_w_out': 'new_v', 'new_v_norm_mlp_pre': 'new_v', 'new_v_norm_mlp_post': 'new_v', 'new_v_w_up': 'new_v', 'new_v_w_down': 'new_v'}


def _forward(args):
    return _fwd_reference(*[args[k] for k in FWD_PARAMS])


def _output_shape():
    def fwd():
        inp = _fwd_setup_inputs(0)
        return _fwd_reference(*[inp[k] for k in FWD_PARAMS])
    out = _jax.eval_shape(fwd)
    return out.shape, out.dtype

N_MICROBATCH = 1
ADAM_LR = 0.001
ADAM_B1 = 0.9
ADAM_B2 = 0.999
ADAM_EPS = 1e-08
ADAM_WD = 0.01
ADAM_STEP = 10
PER_EXAMPLE_BATCH_AXIS = {'x': 0, 'positions': 0, 'loss_target': 0}
SHARED_INPUTS = []
_WEIGHT_DTYPES = {'norm_attn_pre': _jnp.float32, 'norm_attn_post': _jnp.float32, 'w_in': _jnp.float32, 'q_latent_norm': _jnp.float32, 'kv_latent_norm': _jnp.float32, 'w_uq': _jnp.float32, 'w_ukv': _jnp.float32, 'w_out': _jnp.float32, 'norm_mlp_pre': _jnp.float32, 'norm_mlp_post': _jnp.float32, 'w_up': _jnp.float32, 'w_down': _jnp.float32}
MOMENT_SCALE = {'norm_attn_pre': 7.162084e-01, 'norm_attn_post': 1.590477e+01, 'w_in': 5.014461e-01, 'q_latent_norm': 2.664927e-01, 'kv_latent_norm': 1.147987e+00, 'w_uq': 1.532170e-01, 'w_ukv': 4.952533e-01, 'w_out': 6.609950e-01, 'norm_mlp_pre': 3.887428e-01, 'norm_mlp_post': 1.636446e+01, 'w_up': 1.981675e-01, 'w_down': 7.134263e-01}


def _to_microbatches(a, axis):
    t = _jnp.moveaxis(a, axis, 0)
    t = t.reshape((N_MICROBATCH, t.shape[0] // N_MICROBATCH) + t.shape[1:])
    return _jnp.moveaxis(t, 1, axis + 1)


def setup_inputs(seed: int = 0) -> dict:
    inp = _fwd_setup_inputs(seed)
    key = _jax.random.fold_in(_jax.random.key(seed), 7919)
    shape, _ = _output_shape()
    out = dict(inp)
    out["loss_target"] = _jax.random.normal(_jax.random.fold_in(key, 0), shape, _jnp.float32)
    for i, name in enumerate(TWIN_WEIGHTS):
        w = inp[name].astype(_jnp.float32)
        if MOMENT_SCALE is None:
            s = _jnp.sqrt(_jnp.mean(_jnp.square(w)) + 1e-30)
        else:
            s = MOMENT_SCALE[name]
        km, kv = _jax.random.split(_jax.random.fold_in(key, i + 1))
        out[name] = w
        out["m_" + name] = s * _jax.random.normal(km, w.shape, _jnp.float32)
        out["v_" + name] = (s * s) * _jax.random.uniform(kv, w.shape, _jnp.float32, 0.5, 1.5)
    if N_MICROBATCH > 1:
        for name, axis in PER_EXAMPLE_BATCH_AXIS.items():
            out[name] = _to_microbatches(out[name], axis)
    return {'x': out['x'], 'positions': out['positions'], 'norm_attn_pre': out['norm_attn_pre'], 'norm_attn_post': out['norm_attn_post'], 'w_in': out['w_in'], 'q_latent_norm': out['q_latent_norm'], 'kv_latent_norm': out['kv_latent_norm'], 'w_uq': out['w_uq'], 'w_ukv': out['w_ukv'], 'w_out': out['w_out'], 'norm_mlp_pre': out['norm_mlp_pre'], 'norm_mlp_post': out['norm_mlp_post'], 'w_up': out['w_up'], 'w_down': out['w_down'], 'loss_target': out['loss_target'], 'm_norm_attn_pre': out['m_norm_attn_pre'], 'm_norm_attn_post': out['m_norm_attn_post'], 'm_w_in': out['m_w_in'], 'm_q_latent_norm': out['m_q_latent_norm'], 'm_kv_latent_norm': out['m_kv_latent_norm'], 'm_w_uq': out['m_w_uq'], 'm_w_ukv': out['m_w_ukv'], 'm_w_out': out['m_w_out'], 'm_norm_mlp_pre': out['m_norm_mlp_pre'], 'm_norm_mlp_post': out['m_norm_mlp_post'], 'm_w_up': out['m_w_up'], 'm_w_down': out['m_w_down'], 'v_norm_attn_pre': out['v_norm_attn_pre'], 'v_norm_attn_post': out['v_norm_attn_post'], 'v_w_in': out['v_w_in'], 'v_q_latent_norm': out['v_q_latent_norm'], 'v_kv_latent_norm': out['v_kv_latent_norm'], 'v_w_uq': out['v_w_uq'], 'v_w_ukv': out['v_w_ukv'], 'v_w_out': out['v_w_out'], 'v_norm_mlp_pre': out['v_norm_mlp_pre'], 'v_norm_mlp_post': out['v_norm_mlp_post'], 'v_w_up': out['v_w_up'], 'v_w_down': out['v_w_down']}


def _loss(weights, diff, rest, loss_target):
    with _jax.named_scope("forward"):
        args = {**rest, TWIN_DIFF_INPUT: diff, **{k: w.astype(_WEIGHT_DTYPES[k]) for k, w in weights.items()}}
        y = _forward(args)
    with _jax.named_scope("loss_head"):
        err = _jnp.square(y.astype(_jnp.float32) - loss_target)
        return 0.5 * _jnp.sum(_jnp.mean(err, axis=-1)) if err.ndim else 0.5 * err


def _adamw(w, g, m, v):
    m = ADAM_B1 * m + (1.0 - ADAM_B1) * g
    v = ADAM_B2 * v + (1.0 - ADAM_B2) * _jnp.square(g)
    m_hat = m / (1.0 - ADAM_B1 ** ADAM_STEP)
    v_hat = v / (1.0 - ADAM_B2 ** ADAM_STEP)
    delta = -ADAM_LR * (m_hat / (_jnp.sqrt(v_hat) + ADAM_EPS) + ADAM_WD * w)
    return delta, m, v


def reference(x, positions, norm_attn_pre, norm_attn_post, w_in, q_latent_norm, kv_latent_norm, w_uq, w_ukv, w_out, norm_mlp_pre, norm_mlp_post, w_up, w_down, loss_target, m_norm_attn_pre, m_norm_attn_post, m_w_in, m_q_latent_norm, m_kv_latent_norm, m_w_uq, m_w_ukv, m_w_out, m_norm_mlp_pre, m_norm_mlp_post, m_w_up, m_w_down, v_norm_attn_pre, v_norm_attn_post, v_w_in, v_q_latent_norm, v_kv_latent_norm, v_w_uq, v_w_ukv, v_w_out, v_norm_mlp_pre, v_norm_mlp_post, v_w_up, v_w_down):
    given = dict(x=x, positions=positions, norm_attn_pre=norm_attn_pre, norm_attn_post=norm_attn_post, w_in=w_in, q_latent_norm=q_latent_norm, kv_latent_norm=kv_latent_norm, w_uq=w_uq, w_ukv=w_ukv, w_out=w_out, norm_mlp_pre=norm_mlp_pre, norm_mlp_post=norm_mlp_post, w_up=w_up, w_down=w_down, loss_target=loss_target, m_norm_attn_pre=m_norm_attn_pre, m_norm_attn_post=m_norm_attn_post, m_w_in=m_w_in, m_q_latent_norm=m_q_latent_norm, m_kv_latent_norm=m_kv_latent_norm, m_w_uq=m_w_uq, m_w_ukv=m_w_ukv, m_w_out=m_w_out, m_norm_mlp_pre=m_norm_mlp_pre, m_norm_mlp_post=m_norm_mlp_post, m_w_up=m_w_up, m_w_down=m_w_down, v_norm_attn_pre=v_norm_attn_pre, v_norm_attn_post=v_norm_attn_post, v_w_in=v_w_in, v_q_latent_norm=v_q_latent_norm, v_kv_latent_norm=v_kv_latent_norm, v_w_uq=v_w_uq, v_w_ukv=v_w_ukv, v_w_out=v_w_out, v_norm_mlp_pre=v_norm_mlp_pre, v_norm_mlp_post=v_norm_mlp_post, v_w_up=v_w_up, v_w_down=v_w_down)
    weights = {n: given[n] for n in TWIN_WEIGHTS}
    shared = {n: given[n] for n in SHARED_INPUTS}
    per_example = {n: given[n] for n in ['x', 'positions']}
    grad_fn = _jax.value_and_grad(_loss, argnums=(0, 1))

    def one_microbatch(ex, loss_target):
        ex = dict(ex)
        diff = ex.pop(TWIN_DIFF_INPUT)
        return grad_fn(weights, diff, {**shared, **ex}, loss_target)

    if N_MICROBATCH == 1:
        loss, (grad_w, grad_x) = one_microbatch(per_example, given["loss_target"])
    else:
        def body(carry, xs):
            loss_sum, grad_sum = carry
            l_k, (gw_k, gx_k) = one_microbatch(xs[0], xs[1])
            with _jax.named_scope("update"):
                return (loss_sum + l_k, _jax.tree.map(_jnp.add, grad_sum, gw_k)), gx_k

        init = (_jnp.zeros((), _jnp.float32), _jax.tree.map(_jnp.zeros_like, weights))
        (loss, grad_w), grad_x = _jax.lax.scan(body, init, (per_example, given["loss_target"]))
    with _jax.named_scope("update"):
        delta_w, new_m, new_v = {}, {}, {}
        for n in TWIN_WEIGHTS:
            delta_w[n], new_m[n], new_v[n] = _adamw(weights[n], grad_w[n], given["m_" + n], given["v_" + n])
    return (loss, grad_x, *[grad_w[n] for n in TWIN_WEIGHTS], *[delta_w[n] for n in TWIN_WEIGHTS],
            *[new_m[n] for n in TWIN_WEIGHTS], *[new_v[n] for n in TWIN_WEIGHTS])
```

```python
import functools

import jax
import jax.numpy as jnp
from jax import lax
from jax.experimental import pallas as pl
from jax.experimental.pallas import tpu as pltpu

F32 = jnp.float32
BF16 = jnp.bfloat16
MXU_DTYPE = jnp.bfloat16
WIRE_DTYPE = jnp.bfloat16

D_MODEL = 2048
HEAD = 128
NH = 8
A_W = NH * HEAD
LORA = 512
ROPE_B = 64
QPAD = 256
MAIN_COLS = 3 * A_W + 2 * LORA
IN_COLS = MAIN_COLS + ROPE_B
D_FF = 4 * D_MODEL
DIL = (1, 4, 16)
ROT_A = 32
ROPE_THETA = 500000.0
EPS = 1e-6
NEG = -1e30
N_CHIPS = 4
N_DEV = 8

ADAM_LR = 0.001
ADAM_B1 = 0.9
ADAM_B2 = 0.999
ADAM_EPS = 1e-08
ADAM_WD = 0.01
ADAM_STEP = 10

MESH = pl.DeviceIdType.MESH
ANY = pl.BlockSpec(memory_space=pl.ANY)


def _pcall(body, **kw):
    return pl.pallas_call(body, **kw)


_DIMS = {
    "nn": (((1,), (0,)), ((), ())),
    "nt": (((1,), (1,)), ((), ())),
    "tn": (((0,), (0,)), ((), ())),
}


def _mm_body(*refs, dims, nk, epi, n_extra, n_out):
    a_ref, b_ref = refs[0], refs[1]
    extra = refs[2:2 + n_extra]
    outs = refs[2 + n_extra:2 + n_extra + n_out]
    part = lax.dot_general(a_ref[...], b_ref[...], _DIMS[dims], preferred_element_type=F32)

    def finish(acc):
        res = epi(acc, *[r[...] for r in extra]) if epi is not None else (acc,)
        for o_ref, o in zip(outs, res):
            o_ref[...] = o.astype(o_ref.dtype)

    if nk == 1:
        finish(part)
        return
    acc_ref = refs[-1]
    k = pl.program_id(2)

    @pl.when(k == 0)
    def _():
        acc_ref[...] = part

    @pl.when(k > 0)
    def _():
        acc_ref[...] += part

    @pl.when(k == nk - 1)
    def _():
        finish(acc_ref[...])


def _matmul(a, b, *, dims, out_dtypes, tm, tn, tk, name, epi=None, extras=()):
    if dims == "nn":
        (M, K), (K2, N) = a.shape, b.shape
    elif dims == "nt":
        (M, K), (N, K2) = a.shape, b.shape
    else:
        (K, M), (K2, N) = a.shape, b.shape
    assert K == K2, (a.shape, b.shape, dims)
    tm, tn, tk = min(tm, M), min(tn, N), min(tk, K)
    assert M % tm == 0 and N % tn == 0 and K % tk == 0, (name, M, N, K, tm, tn, tk)
    nk = K // tk
    a_spec = {"nn": pl.BlockSpec((tm, tk), lambda i, j, k: (i, k)),
              "nt": pl.BlockSpec((tm, tk), lambda i, j, k: (i, k)),
              "tn": pl.BlockSpec((tk, tm), lambda i, j, k: (k, i))}[dims]
    b_spec = {"nn": pl.BlockSpec((tk, tn), lambda i, j, k: (k, j)),
              "nt": pl.BlockSpec((tn, tk), lambda i, j, k: (j, k)),
              "tn": pl.BlockSpec((tk, tn), lambda i, j, k: (k, j))}[dims]
    o_spec = pl.BlockSpec((tm, tn), lambda i, j, k: (i, j))
    body = functools.partial(_mm_body, dims=dims, nk=nk, epi=epi,
                             n_extra=len(extras), n_out=len(out_dtypes))
    res = _pcall(
        body, name=name,
        grid=(M // tm, N // tn, nk),
        in_specs=[a_spec, b_spec] + [o_spec] * len(extras),
        out_specs=[o_spec] * len(out_dtypes),
        out_shape=[jax.ShapeDtypeStruct((M, N), dt) for dt in out_dtypes],
        scratch_shapes=[pltpu.VMEM((tm, tn), F32)] if nk > 1 else [],
        compiler_params=pltpu.CompilerParams(
            dimension_semantics=("parallel", "parallel", "arbitrary")),
    )(a, b, *extras)
    return list(res)


def _rowwise(body, row_ins, vec_ins, row_outs, acc_outs, *, tr, name):
    T = row_ins[0].shape[0]
    tr = min(tr, T)
    assert T % tr == 0
    in_specs = [pl.BlockSpec((tr, a.shape[1]), lambda i: (i, 0)) for a in row_ins]
    in_specs += [pl.BlockSpec(a.shape, lambda i: (0, 0)) for a in vec_ins]
    out_specs = [pl.BlockSpec((tr, w), lambda i: (i, 0)) for (w, _) in row_outs]
    out_specs += [pl.BlockSpec(s, lambda i: (0, 0)) for s in acc_outs]
    out_shape = [jax.ShapeDtypeStruct((T, w), dt) for (w, dt) in row_outs]
    out_shape += [jax.ShapeDtypeStruct(s, F32) for s in acc_outs]
    sem = "arbitrary" if acc_outs else "parallel"
    return list(_pcall(
        body, name=name, grid=(T // tr,), in_specs=in_specs, out_specs=out_specs,
        out_shape=out_shape,
        compiler_params=pltpu.CompilerParams(dimension_semantics=(sem,)),
    )(*row_ins, *vec_ins))


def _rstd(x):
    return lax.rsqrt(jnp.mean(x * x, axis=-1, keepdims=True) + EPS)


def _rms_bwd(x, rstd, dyg):
    xh = x * rstd
    return rstd * (dyg - xh * jnp.mean(dyg * xh, axis=-1, keepdims=True)), xh


def _fold8(v):
    r, w = v.shape
    return jnp.sum(v.reshape(r // 8, 8, w), axis=0)


def _acc(ref, val):
    first = pl.program_id(0) == 0

    @pl.when(first)
    def _():
        ref[...] = val

    @pl.when(jnp.logical_not(first))
    def _():
        ref[...] += val


def _rope(x, c, sa, sb, half):
    return x * c + pltpu.roll(x, HEAD - half, 1) * sa + pltpu.roll(x, half, 1) * sb


def _rope_t(dy, c, sa, sb, half):
    return dy * c - pltpu.roll(dy, HEAD - half, 1) * sa - pltpu.roll(dy, half, 1) * sb


def _rope_tab_body(pos_ref, inv_ref, ca, saa, sab, cb, sba, sbb):
    pos = pos_ref[...]
    lane = lax.broadcasted_iota(jnp.int32, (pos.shape[0], HEAD), 1)
    ang_a = pos * inv_ref[0:1, :]
    ang_b = pos * inv_ref[1:2, :]
    c, s = jnp.cos(ang_a), jnp.sin(ang_a)
    ha = ROT_A // 2
    ca[...] = jnp.where(lane < ROT_A, c, 1.0)
    saa[...] = jnp.where(lane < ha, -s, 0.0)
    sab[...] = jnp.where((lane >= ha) & (lane < ROT_A), s, 0.0)
    c, s = jnp.cos(ang_b), jnp.sin(ang_b)
    hb = ROPE_B // 2
    cb[...] = jnp.where(lane < ROPE_B, c, 1.0)
    sba[...] = jnp.where(lane < hb, -s, 0.0)
    sbb[...] = jnp.where((lane >= hb) & (lane < ROPE_B), s, 0.0)


def _rms_fwd_body(x_ref, g_ref, h_ref):
    x = x_ref[...]
    h_ref[...] = ((x * _rstd(x)) * g_ref[...]).astype(h_ref.dtype)


def _postproj_body(p_ref, kr_ref, ca, saa, sab, cb, sba, sbb, gq_ref, gkv_ref,
                   q_ref, k_ref, v_ref, cqn_ref, ckvn_ref, krope_ref):
    c, sa, sb = ca[...], saa[...], sab[...]
    for h in range(NH):
        lo = h * HEAD
        q_ref[:, lo:lo + HEAD] = _rope(p_ref[:, lo:lo + HEAD], c, sa, sb, ROT_A // 2).astype(q_ref.dtype)
        k_ref[:, lo:lo + HEAD] = _rope(p_ref[:, A_W + lo:A_W + lo + HEAD], c, sa, sb, ROT_A // 2).astype(k_ref.dtype)
    v_ref[...] = p_ref[:, 2 * A_W:3 * A_W].astype(v_ref.dtype)
    cq = p_ref[:, 3 * A_W:3 * A_W + LORA]
    cqn_ref[...] = ((cq * _rstd(cq)) * gq_ref[...]).astype(cqn_ref.dtype)
    ckv = p_ref[:, 3 * A_W + LORA:MAIN_COLS]
    ckvn_ref[...] = ((ckv * _rstd(ckv)) * gkv_ref[...]).astype(ckvn_ref.dtype)
    krope_ref[...] = _rope(kr_ref[...], cb[...], sba[...], sbb[...], ROPE_B // 2).astype(krope_ref.dtype)


def _merge_body(o1, o2, o3, l1, l2, l3, a_ref, lse_ref):
    la, lb, lc = l1[...], l2[...], l3[...]
    m = jnp.maximum(jnp.maximum(la, lb), lc)
    wa, wb, wc = jnp.exp(la - m), jnp.exp(lb - m), jnp.exp(lc - m)
    den = wa + wb + wc
    a = (wa / den) * o1[...] + (wb / den) * o2[...] + (wc / den) * o3[...]
    a_ref[...] = a.astype(a_ref.dtype)
    lse_ref[...] = m + jnp.log(den)


def _qrope_body(qp_ref, cb, sba, sbb, q_ref):
    c, sa, sb = cb[...], sba[...], sbb[...]
    for h in range(NH):
        lo = h * QPAD
        q_ref[:, lo:lo + HEAD] = qp_ref[:, lo:lo + HEAD].astype(q_ref.dtype)
        q_ref[:, lo + HEAD:lo + QPAD] = _rope(qp_ref[:, lo + HEAD:lo + QPAD], c, sa, sb, ROPE_B // 2).astype(q_ref.dtype)


def _qrope_t_body(dq_ref, cb, sba, sbb, o_ref):
    c, sa, sb = cb[...], sba[...], sbb[...]
    for h in range(NH):
        lo = h * QPAD
        o_ref[:, lo:lo + HEAD] = dq_ref[:, lo:lo + HEAD].astype(o_ref.dtype)
        o_ref[:, lo + HEAD:lo + QPAD] = _rope_t(dq_ref[:, lo + HEAD:lo + QPAD], c, sa, sb, ROPE_B // 2).astype(o_ref.dtype)


def _mid_body(x_ref, o_ref, g2_ref, g3_ref, x1_ref, h2_ref):
    o = o_ref[...]
    x1 = x_ref[...] + (o * _rstd(o)) * g2_ref[...]
    x1_ref[...] = x1
    h2_ref[...] = ((x1 * _rstd(x1)) * g3_ref[...]).astype(h2_ref.dtype)


def _loss_body(x1_ref, d_ref, t_ref, g4_ref, dy_ref, dd_ref, loss_ref, dg4_ref):
    d = d_ref[...]
    rstd = _rstd(d)
    y = x1_ref[...] + (d * rstd) * g4_ref[...]
    e = y - t_ref[...]
    dy = e * (1.0 / D_MODEL)
    dy_ref[...] = dy
    dd, dh = _rms_bwd(d, rstd, dy * g4_ref[...])
    dd_ref[...] = dd.astype(dd_ref.dtype)
    _acc(dg4_ref, _fold8(dy * dh))
    e8 = _fold8(e * e)
    l = e8[:, 0:HEAD]
    for j in range(1, D_MODEL // HEAD):
        l = l + e8[:, j * HEAD:(j + 1) * HEAD]
    _acc(loss_ref, l)


def _bmid_body(dy_ref, dh2_ref, x1_ref, o_ref, g2_ref, g3_ref, dx1_ref, do_ref, dg3_ref, dg2_ref):
    x1 = x1_ref[...]
    dh2 = dh2_ref[...]
    dn, x1h = _rms_bwd(x1, _rstd(x1), dh2 * g3_ref[...])
    dx1 = dy_ref[...] + dn
    dx1_ref[...] = dx1
    _acc(dg3_ref, _fold8(dh2 * x1h))
    o = o_ref[...]
    do, oh = _rms_bwd(o, _rstd(o), dx1 * g2_ref[...])
    do_ref[...] = do.astype(do_ref.dtype)
    _acc(dg2_ref, _fold8(dx1 * oh))


def _delta_body(dm_ref, mx_ref, dmb_ref, delta_ref):
    for h in range(2 * NH):
        lo = h * HEAD
        dm = dm_ref[:, lo:lo + HEAD]
        dmb_ref[:, lo:lo + HEAD] = dm.astype(dmb_ref.dtype)
        dl = jnp.sum(dm * mx_ref[:, lo:lo + HEAD].astype(F32), axis=1, keepdims=True)
        delta_ref[:, lo:lo + HEAD] = jnp.broadcast_to(dl, dm.shape)


def _dproj_body(dq1, dq2, dq3, dk1, dk2, dk3, dv1, dv2, dv3, dcq_ref, dckv_ref, p_ref, dkr_ref,
                ca, saa, sab, cb, sba, sbb, gq_ref, gkv_ref,
                dp_ref, dkrp_ref, dgq_ref, dgkv_ref):
    c, sa, sb = ca[...], saa[...], sab[...]
    for h in range(NH):
        lo = h * HEAD
        dq = dq1[:, lo:lo + HEAD] + dq2[:, lo:lo + HEAD] + dq3[:, lo:lo + HEAD]
        dp_ref[:, lo:lo + HEAD] = _rope_t(dq, c, sa, sb, ROT_A // 2).astype(dp_ref.dtype)
        dk = dk1[:, lo:lo + HEAD] + dk2[:, lo:lo + HEAD] + dk3[:, lo:lo + HEAD]
        dp_ref[:, A_W + lo:A_W + lo + HEAD] = _rope_t(dk, c, sa, sb, ROT_A // 2).astype(dp_ref.dtype)
    dp_ref[:, 2 * A_W:3 * A_W] = (dv1[...] + dv2[...] + dv3[...]).astype(dp_ref.dtype)
    cq = p_ref[:, 3 * A_W:3 * A_W + LORA]
    dcqn = dcq_ref[...]
    dcq, cqh = _rms_bwd(cq, _rstd(cq), dcqn * gq_ref[...])
    dp_ref[:, 3 * A_W:3 * A_W + LORA] = dcq.astype(dp_ref.dtype)
    _acc(dgq_ref, _fold8(dcqn * cqh))
    ckv = p_ref[:, 3 * A_W + LORA:MAIN_COLS]
    dckvn = dckv_ref[...]
    dckv, ckvh = _rms_bwd(ckv, _rstd(ckv), dckvn * gkv_ref[...])
    dp_ref[:, 3 * A_W + LORA:MAIN_COLS] = dckv.astype(dp_ref.dtype)
    _acc(dgkv_ref, _fold8(dckvn * ckvh))
    dkrp_ref[...] = _rope_t(dkr_ref[...], cb[...], sba[...], sbb[...], ROPE_B // 2).astype(dkrp_ref.dtype)


def _bin_body(dx1_ref, dha_ref, dhb_ref, x_ref, g1_ref, dx_ref, dg1_ref):
    x = x_ref[...]
    dh = dha_ref[...] + dhb_ref[...]
    dn, xh = _rms_bwd(x, _rstd(x), dh * g1_ref[...])
    dx_ref[...] = dx1_ref[...] + dn
    _acc(dg1_ref, _fold8(dh * xh))


def _dot_nt(a, b):
    return lax.dot_general(a, b, _DIMS["nt"], preferred_element_type=F32)


def _dot_tn(a, b):
    return lax.dot_general(a, b, _DIMS["tn"], preferred_element_type=F32)


def _dot_nn(a, b):
    return jnp.dot(a, b, preferred_element_type=F32)


def _band_masks():
    row = lax.broadcasted_iota(jnp.int32, (HEAD, HEAD), 0)
    col = lax.broadcasted_iota(jnp.int32, (HEAD, HEAD), 1)
    return col <= row, col >= row


def _dil_fwd_body(q_ref, k_ref, v_ref, o_ref, l_ref, *, nb, hb):
    m_cur, m_prev = _band_masks()
    scale = HEAD ** -0.5
    for h in range(hb):
        cs = slice(h * HEAD, (h + 1) * HEAD)

        def blk(n, carry, cs=cs):
            r0 = pl.multiple_of(n * HEAD, HEAD)
            rp = pl.multiple_of(jnp.maximum(n - 1, 0) * HEAD, HEAD)
            q = q_ref[pl.ds(r0, HEAD), cs]
            s_c = jnp.where(m_cur, _dot_nt(q, k_ref[pl.ds(r0, HEAD), cs]) * scale, NEG)
            s_p = jnp.where(jnp.logical_and(m_prev, n > 0),
                            _dot_nt(q, k_ref[pl.ds(rp, HEAD), cs]) * scale, NEG)
            m = jnp.maximum(jnp.max(s_c, axis=1, keepdims=True), jnp.max(s_p, axis=1, keepdims=True))
            p_c, p_p = jnp.exp(s_c - m), jnp.exp(s_p - m)
            den = jnp.sum(p_c, axis=1, keepdims=True) + jnp.sum(p_p, axis=1, keepdims=True)
            o = (_dot_nn((p_c / den).astype(MXU_DTYPE), v_ref[pl.ds(r0, HEAD), cs])
                 + _dot_nn((p_p / den).astype(MXU_DTYPE), v_ref[pl.ds(rp, HEAD), cs]))
            o_ref[pl.ds(r0, HEAD), cs] = o
            l_ref[pl.ds(r0, HEAD), cs] = jnp.broadcast_to(m + jnp.log(den), (HEAD, HEAD))
            return carry

        lax.fori_loop(0, nb, blk, 0)


def _dil_bwd_body(q_ref, k_ref, v_ref, do_ref, l_ref, dl_ref, dq_ref, dk_ref, dv_ref, *, nb, hb):
    m_cur, m_prev = _band_masks()
    scale = HEAD ** -0.5
    for h in range(hb):
        cs = slice(h * HEAD, (h + 1) * HEAD)

        def blk(n, carry, cs=cs):
            r0 = pl.multiple_of(n * HEAD, HEAD)
            rp = pl.multiple_of(jnp.maximum(n - 1, 0) * HEAD, HEAD)
            q = q_ref[pl.ds(r0, HEAD), cs]
            kc, kp = k_ref[pl.ds(r0, HEAD), cs], k_ref[pl.ds(rp, HEAD), cs]
            vc, vp = v_ref[pl.ds(r0, HEAD), cs], v_ref[pl.ds(rp, HEAD), cs]
            do = do_ref[pl.ds(r0, HEAD), cs]
            lse = l_ref[pl.ds(r0, HEAD), cs]
            dl = dl_ref[pl.ds(r0, HEAD), cs]
            p_c = jnp.where(m_cur, jnp.exp(_dot_nt(q, kc) * scale - lse), 0.0)
            p_p = jnp.where(jnp.logical_and(m_prev, n > 0), jnp.exp(_dot_nt(q, kp) * scale - lse), 0.0)
            ds_c = (p_c * (_dot_nt(do, vc) - dl)).astype(MXU_DTYPE)
            ds_p = (p_p * (_dot_nt(do, vp) - dl)).astype(MXU_DTYPE)
            dq_ref[pl.ds(r0, HEAD), cs] = (_dot_nn(ds_c, kc) + _dot_nn(ds_p, kp)) * scale
            dk_ref[pl.ds(r0, HEAD), cs] = _dot_tn(ds_c, q) * scale
            dv_ref[pl.ds(r0, HEAD), cs] = _dot_tn(p_c.astype(MXU_DTYPE), do)
            dk_ref[pl.ds(rp, HEAD), cs] += _dot_tn(ds_p, q) * scale
            dv_ref[pl.ds(rp, HEAD), cs] += _dot_tn(p_p.astype(MXU_DTYPE), do)
            return carry

        lax.fori_loop(0, nb, blk, 0)


def _dil_heads_per_block(rows):
    return max(1, min(NH, (4096 // rows)))


def _dil_call(body_fn, ins, n_out, d, name):
    T = ins[0].shape[0]
    rows = T // d
    hb = _dil_heads_per_block(rows)
    nb = rows // HEAD
    views = [a.reshape(rows, d * A_W) for a in ins]
    spec = pl.BlockSpec((rows, hb * HEAD), lambda g: (0, g))
    outs = _pcall(
        functools.partial(body_fn, nb=nb, hb=hb), name=name,
        grid=(d * NH // hb,),
        in_specs=[spec] * len(ins), out_specs=[spec] * n_out,
        out_shape=[jax.ShapeDtypeStruct((rows, d * A_W), F32)] * n_out,
        compiler_params=pltpu.CompilerParams(dimension_semantics=("parallel",)),
    )(*views)
    return [o.reshape(T, A_W) for o in outs]


MLA_SCALE = (HEAD + ROPE_B) ** -0.5


def _causal(qi, ki, tq, tk):
    row = lax.broadcasted_iota(jnp.int32, (tq, tk), 0) + qi * tq
    col = lax.broadcasted_iota(jnp.int32, (tq, tk), 1) + ki * tk
    return col <= row


def _mla_fwd_body(q_ref, kn_ref, kr_ref, v_ref, o_ref, lse_ref, m_sc, l_sc, acc_sc, *, tq, tk):
    qi, ki = pl.program_id(1), pl.program_id(2)

    @pl.when(ki == 0)
    def _():
        m_sc[...] = jnp.full(m_sc.shape, NEG, F32)
        l_sc[...] = jnp.zeros(l_sc.shape, F32)
        acc_sc[...] = jnp.zeros(acc_sc.shape, F32)

    @pl.when(ki * tk <= qi * tq + (tq - 1))
    def _():
        kcat = jnp.concatenate([kn_ref[...], kr_ref[...]], axis=1)
        s = _dot_nt(q_ref[...], kcat) * MLA_SCALE
        s = jnp.where(_causal(qi, ki, tq, tk), s, NEG)
        m_prev = m_sc[:, 0:1]
        m_new = jnp.maximum(m_prev, jnp.max(s, axis=1, keepdims=True))
        alpha = jnp.exp(m_prev - m_new)
        p = jnp.exp(s - m_new)
        l_sc[...] = jnp.broadcast_to(alpha * l_sc[:, 0:1] + jnp.sum(p, axis=1, keepdims=True), l_sc.shape)
        acc_sc[...] = alpha * acc_sc[...] + _dot_nn(p.astype(MXU_DTYPE), v_ref[...])
        m_sc[...] = jnp.broadcast_to(m_new, m_sc.shape)

    @pl.when(ki == pl.num_programs(2) - 1)
    def _():
        l = l_sc[...]
        o_ref[...] = (acc_sc[...] / l).astype(o_ref.dtype)
        lse_ref[...] = m_sc[...] + jnp.log(l)


def _mla_fwd(qf, kv, kr, *, tq, tk):
    T = qf.shape[0]
    tq, tk = min(tq, T), min(tk, T)
    nq, nk = T // tq, T // tk
    last = lambda i, j: jnp.minimum(j, (i * tq + tq - 1) // tk)
    return _pcall(
        functools.partial(_mla_fwd_body, tq=tq, tk=tk), name="mla_fwd",
        grid=(NH, nq, nk),
        in_specs=[pl.BlockSpec((tq, QPAD), lambda h, i, j: (i, h)),
                  pl.BlockSpec((tk, HEAD), lambda h, i, j: (last(i, j), h)),
                  pl.BlockSpec((tk, HEAD), lambda h, i, j: (last(i, j), 0)),
                  pl.BlockSpec((tk, HEAD), lambda h, i, j: (last(i, j), NH + h))],
        out_specs=[pl.BlockSpec((tq, HEAD), lambda h, i, j: (i, h)),
                   pl.BlockSpec((tq, HEAD), lambda h, i, j: (i, h))],
        out_shape=[jax.ShapeDtypeStruct((T, A_W), MXU_DTYPE), jax.ShapeDtypeStruct((T, A_W), F32)],
        scratch_shapes=[pltpu.VMEM((tq, HEAD), F32), pltpu.VMEM((tq, HEAD), F32), pltpu.VMEM((tq, HEAD), F32)],
        compiler_params=pltpu.CompilerParams(dimension_semantics=("parallel", "parallel", "arbitrary")),
    )(qf, kv, kr, kv)


def _mla_p_ds(q, kcat, v, do, lse, dl, qi, ki, tq, tk):
    s = _dot_nt(q, kcat) * MLA_SCALE
    p = jnp.where(_causal(qi, ki, tq, tk), jnp.exp(s - lse[:, 0:1]), 0.0)
    ds = p * (_dot_nt(do, v) - dl[:, 0:1])
    return p.astype(MXU_DTYPE), ds.astype(MXU_DTYPE)


def _mla_dq_body(q_ref, kn_ref, kr_ref, v_ref, do_ref, lse_ref, dl_ref, dq_ref, acc_sc, *, tq, tk):
    qi, ki = pl.program_id(1), pl.program_id(2)

    @pl.when(ki == 0)
    def _():
        acc_sc[...] = jnp.zeros(acc_sc.shape, F32)

    @pl.when(ki * tk <= qi * tq + (tq - 1))
    def _():
        kcat = jnp.concatenate([kn_ref[...], kr_ref[...]], axis=1)
        _, ds = _mla_p_ds(q_ref[...], kcat, v_ref[...], do_ref[...], lse_ref[...], dl_ref[...], qi, ki, tq, tk)
        acc_sc[...] += _dot_nn(ds, kcat)

    @pl.when(ki == pl.num_programs(2) - 1)
    def _():
        dq_ref[...] = acc_sc[...] * MLA_SCALE


def _mla_dq(qf, kv, kr, dob, lse, dl, *, tq, tk):
    T = qf.shape[0]
    tq, tk = min(tq, T), min(tk, T)
    nq, nk = T // tq, T // tk
    last = lambda i, j: jnp.minimum(j, (i * tq + tq - 1) // tk)
    qspec = pl.BlockSpec((tq, HEAD), lambda h, i, j: (i, h))
    return _pcall(
        functools.partial(_mla_dq_body, tq=tq, tk=tk), name="mla_dq",
        grid=(NH, nq, nk),
        in_specs=[pl.BlockSpec((tq, QPAD), lambda h, i, j: (i, h)),
                  pl.BlockSpec((tk, HEAD), lambda h, i, j: (last(i, j), h)),
                  pl.BlockSpec((tk, HEAD), lambda h, i, j: (last(i, j), 0)),
                  pl.BlockSpec((tk, HEAD), lambda h, i, j: (last(i, j), NH + h)),
                  pl.BlockSpec((tq, HEAD), lambda h, i, j: (i, NH + h)), qspec,
                  pl.BlockSpec((tq, HEAD), lambda h, i, j: (i, NH + h))],
        out_specs=pl.BlockSpec((tq, QPAD), lambda h, i, j: (i, h)),
        out_shape=jax.ShapeDtypeStruct((T, NH * QPAD), F32),
        scratch_shapes=[pltpu.VMEM((tq, QPAD), F32)],
        compiler_params=pltpu.CompilerParams(dimension_semantics=("parallel", "parallel", "arbitrary")),
    )(qf, kv, kr, kv, dob, lse, dl)


def _mla_dkv_body(q_ref, kn_ref, kr_ref, v_ref, do_ref, lse_ref, dl_ref, dkn_ref, dv_ref, dkr_ref,
                  dk_sc, dv_sc, dkr_sc, *, tq, tk):
    ki, h, qi = pl.program_id(0), pl.program_id(1), pl.program_id(2)
    nq = pl.num_programs(2)

    @pl.when(jnp.logical_and(h == 0, qi == 0))
    def _():
        dkr_sc[...] = jnp.zeros(dkr_sc.shape, F32)

    @pl.when(qi == 0)
    def _():
        dk_sc[...] = jnp.zeros(dk_sc.shape, F32)
        dv_sc[...] = jnp.zeros(dv_sc.shape, F32)

    @pl.when(ki * tk <= qi * tq + (tq - 1))
    def _():
        kcat = jnp.concatenate([kn_ref[...], kr_ref[...]], axis=1)
        q, do = q_ref[...], do_ref[...]
        p, ds = _mla_p_ds(q, kcat, v_ref[...], do, lse_ref[...], dl_ref[...], qi, ki, tq, tk)
        dk_sc[...] += _dot_tn(ds, q)
        dv_sc[...] += _dot_tn(p, do)

    @pl.when(qi == nq - 1)
    def _():
        dk = dk_sc[...] * MLA_SCALE
        dkn_ref[...] = dk[:, 0:HEAD].astype(dkn_ref.dtype)
        dv_ref[...] = dv_sc[...].astype(dv_ref.dtype)
        dkr_sc[...] += dk[:, HEAD:QPAD]

    @pl.when(jnp.logical_and(h == NH - 1, qi == nq - 1))
    def _():
        dkr_ref[...] = dkr_sc[...]


def _mla_dkv(qf, kv, kr, dob, lse, dl, *, tq, tk):
    T = qf.shape[0]
    tq, tk = min(tq, T), min(tk, T)
    nq, nk = T // tq, T // tk
    first = lambda j, i: jnp.maximum(i, (j * tk) // tq)
    kspec = pl.BlockSpec((tk, HEAD), lambda j, h, i: (j, h))
    outs = _pcall(
        functools.partial(_mla_dkv_body, tq=tq, tk=tk), name="mla_dkv",
        grid=(nk, NH, nq),
        in_specs=[pl.BlockSpec((tq, QPAD), lambda j, h, i: (first(j, i), h)),
                  kspec,
                  pl.BlockSpec((tk, HEAD), lambda j, h, i: (j, 0)),
                  pl.BlockSpec((tk, HEAD), lambda j, h, i: (j, NH + h)),
                  pl.BlockSpec((tq, HEAD), lambda j, h, i: (first(j, i), NH + h)),
                  pl.BlockSpec((tq, HEAD), lambda j, h, i: (first(j, i), h)),
                  pl.BlockSpec((tq, HEAD), lambda j, h, i: (first(j, i), NH + h))],
        out_specs=[kspec, kspec, pl.BlockSpec((tk, HEAD), lambda j, h, i: (j, 0))],
        out_shape=[jax.ShapeDtypeStruct((T, A_W), MXU_DTYPE), jax.ShapeDtypeStruct((T, A_W), MXU_DTYPE),
                   jax.ShapeDtypeStruct((T, HEAD), F32)],
        scratch_shapes=[pltpu.VMEM((tk, QPAD), F32), pltpu.VMEM((tk, HEAD), F32), pltpu.VMEM((tk, HEAD), F32)],
        compiler_params=pltpu.CompilerParams(dimension_semantics=("parallel", "arbitrary", "arbitrary")),
    )(qf, kv, kr, kv, dob, lse, dl)
    return outs


def _local_step(x, pos, target, g1, g2, gq, gkv, g3, g4, w_main, w_kr, w_uq_p, w_ukv_p, w_out, w_up, w_down):
    T = x.shape[0]
    TR = 256
    mm = functools.partial(_matmul, tm=512, tn=1024, tk=2048)

    inv_a = ROPE_THETA ** (-jnp.arange(0, ROT_A, 2, dtype=F32) / ROT_A)
    inv_b = ROPE_THETA ** (-jnp.arange(0, ROPE_B, 2, dtype=F32) / ROPE_B)
    inv = jnp.stack([jnp.concatenate([inv_a, inv_a, jnp.zeros((HEAD - ROT_A,), F32)]),
                     jnp.concatenate([inv_b, inv_b, jnp.zeros((HEAD - ROPE_B,), F32)])])
    inv = jnp.concatenate([inv, jnp.zeros((6, HEAD), F32)], axis=0)
    tabs = _rowwise(_rope_tab_body, [pos], [inv], [(HEAD, F32)] * 6, [], tr=512, name="rope_tables")

    (h,) = _rowwise(_rms_fwd_body, [x], [g1], [(D_MODEL, MXU_DTYPE)], [], tr=TR, name="rms_in")
    (proj,) = mm(h, w_main, dims="nn", out_dtypes=[F32], name="proj_main")
    (kr_raw,) = mm(h, w_kr, dims="nn", out_dtypes=[F32], name="proj_kr")
    q, k, v, cqn, ckvn, krope = _rowwise(
        _postproj_body, [proj, kr_raw] + tabs, [gq, gkv],
        [(A_W, MXU_DTYPE)] * 3 + [(LORA, MXU_DTYPE)] * 2 + [(HEAD, MXU_DTYPE)], [], tr=TR, name="post_proj")

    outs, lses = [], []
    for d in DIL:
        o_c, l_c = _dil_call(_dil_fwd_body, [q, k, v], 2, d, "dil_fwd_%d" % d)
        outs.append(o_c)
        lses.append(l_c)
    a_out, lse_a = _rowwise(_merge_body, outs + lses, [], [(A_W, MXU_DTYPE), (A_W, F32)], [], tr=TR, name="dil_merge")

    (q_pad,) = mm(cqn, w_uq_p, dims="nn", out_dtypes=[F32], name="q_up")
    (qf,) = _rowwise(_qrope_body, [q_pad] + tabs[3:], [], [(NH * QPAD, MXU_DTYPE)], [], tr=TR, name="q_rope")
    (kv,) = mm(ckvn, w_ukv_p, dims="nn", out_dtypes=[MXU_DTYPE], name="kv_up")
    b_out, lse_b = _mla_fwd(qf, kv, krope, tq=512, tk=512)

    mixed = jnp.concatenate([a_out, b_out], axis=1)
    (o,) = mm(mixed, w_out, dims="nn", out_dtypes=[F32], name="out_proj")
    x1, h2 = _rowwise(_mid_body, [x, o], [g2, g3], [(D_MODEL, F32), (D_MODEL, MXU_DTYPE)], [], tr=TR, name="mid_norm")

    def up_epi(acc):
        r = jnp.maximum(acc, 0.0)
        return r * r, r
    u, r = mm(h2, w_up, dims="nn", out_dtypes=[MXU_DTYPE, MXU_DTYPE], name="mlp_up", epi=up_epi)
    (dn,) = mm(u, w_down, dims="nn", out_dtypes=[F32], name="mlp_down")
    dy, dd, loss8, dg4 = _rowwise(_loss_body, [x1, dn, target], [g4], [(D_MODEL, F32), (D_MODEL, MXU_DTYPE)],
                                  [(8, HEAD), (8, D_MODEL)], tr=TR, name="loss_head")

    def dup_epi(acc, rr):
        return (acc * (2.0 * rr.astype(F32)),)
    (dup,) = mm(dd, w_down, dims="nt", out_dtypes=[MXU_DTYPE], name="d_up", epi=dup_epi, extras=(r,))
    (gw_down,) = mm(u, dd, dims="tn", out_dtypes=[WIRE_DTYPE], name="gw_down")
    (dh2,) = mm(dup, w_up, dims="nt", out_dtypes=[F32], name="d_h2")
    (gw_up,) = mm(h2, dup, dims="tn", out_dtypes=[WIRE_DTYPE], name="gw_up")
    dx1, do, dg3, dg2 = _rowwise(_bmid_body, [dy, dh2, x1, o], [g2, g3], [(D_MODEL, F32), (D_MODEL, MXU_DTYPE)],
                                 [(8, D_MODEL), (8, D_MODEL)], tr=TR, name="bwd_mid")
    (dmix,) = mm(do, w_out, dims="nt", out_dtypes=[F32], name="d_mixed")
    (gw_out,) = mm(mixed, do, dims="tn", out_dtypes=[WIRE_DTYPE], name="gw_out")
    dmb, delta = _rowwise(_delta_body, [dmix, mixed], [], [(2 * A_W, MXU_DTYPE), (2 * A_W, F32)], [], tr=TR, name="attn_delta")

    dqf = _mla_dq(qf, kv, krope, dmb, lse_b, delta, tq=512, tk=512)
    dkn, dvb, dkr = _mla_dkv(qf, kv, krope, dmb, lse_b, delta, tq=512, tk=512)
    (dq_pad,) = _rowwise(_qrope_t_body, [dqf] + tabs[3:], [], [(NH * QPAD, MXU_DTYPE)], [], tr=TR, name="q_rope_t")
    (dcqn,) = mm(dq_pad, w_uq_p, dims="nt", out_dtypes=[F32], name="d_cq")
    (gw_uq_p,) = mm(cqn, dq_pad, dims="tn", out_dtypes=[WIRE_DTYPE], name="gw_uq")
    dkv = jnp.concatenate([dkn, dvb], axis=1)
    (dckvn,) = mm(dkv, w_ukv_p, dims="nt", out_dtypes=[F32], name="d_ckv")
    (gw_ukv_p,) = mm(ckvn, dkv, dims="tn", out_dtypes=[WIRE_DTYPE], name="gw_ukv")

    da = dmb[:, 0:A_W]
    delta_a = delta[:, 0:A_W]
    dqs, dks, dvs = [], [], []
    for d in DIL:
        dq_c, dk_c, dv_c = _dil_call(_dil_bwd_body, [q, k, v, da, lse_a, delta_a], 3, d, "dil_bwd_%d" % d)
        dqs.append(dq_c)
        dks.append(dk_c)
        dvs.append(dv_c)
    dproj, dkrp, dgq, dgkv = _rowwise(
        _dproj_body, dqs + dks + dvs + [dcqn, dckvn, proj, dkr] + tabs, [gq, gkv],
        [(MAIN_COLS, MXU_DTYPE), (HEAD, MXU_DTYPE)], [(8, LORA), (8, LORA)], tr=TR, name="d_proj")
    (dha,) = mm(dproj, w_main, dims="nt", out_dtypes=[F32], name="d_h_main")
    (dhb,) = mm(dkrp, w_kr, dims="nt", out_dtypes=[F32], name="d_h_kr")
    (gw_main,) = mm(h, dproj, dims="tn", out_dtypes=[WIRE_DTYPE], name="gw_in_main")
    (gw_kr,) = mm(h, dkrp, dims="tn", out_dtypes=[WIRE_DTYPE], name="gw_in_kr")
    dx, dg1 = _rowwise(_bin_body, [dx1, dha, dhb, x], [g1], [(D_MODEL, F32)], [(8, D_MODEL)], tr=TR, name="bwd_in")

    small = jnp.concatenate([dg1, dg2, dgq, dgkv, dg3, dg4, loss8], axis=1)
    return dx, (gw_main, gw_kr, gw_uq_p, gw_ukv_p, gw_out, gw_up, gw_down), small


def _place():
    x, y, c = lax.axis_index("x"), lax.axis_index("y"), lax.axis_index("c")
    chips = [(1 - x, y), (x, 1 - y), (1 - x, 1 - y)]
    return x, y, c, chips


def _ag_body(*refs, n_w):
    ins, outs = refs[:n_w], refs[n_w:2 * n_w]
    send_sems, recv_sems, fsend_sems, frecv_sems, local_sems = refs[2 * n_w:]
    x, y, c, chips = _place()
    me = 2 * x + y
    sib = (x, y, 1 - c)

    def half_rows(w, which):
        half = ins[w].shape[0] // 2
        return pl.ds(pl.multiple_of(which * half, 16), half)

    locals_, sends, fwds = [], [], []
    for w in range(n_w):
        loc = pltpu.make_async_copy(ins[w], outs[w].at[me], local_sems.at[w])
        loc.start()
        locals_.append(loc)
        mine = half_rows(w, c)
        for j, (px, py) in enumerate(chips):
            cp = pltpu.make_async_remote_copy(
                src_ref=ins[w].at[mine], dst_ref=outs[w].at[me, mine],
                send_sem=send_sems.at[w * 3 + j], recv_sem=recv_sems.at[w * 3 + j],
                device_id=(px, py, c), device_id_type=MESH)
            cp.start()
            sends.append(cp)
    for w in range(n_w):
        mine = half_rows(w, c)
        for j, (px, py) in enumerate(chips):
            landed = outs[w].at[2 * px + py, mine]
            pltpu.make_async_remote_copy(
                src_ref=landed, dst_ref=landed,
                send_sem=send_sems.at[w * 3 + j], recv_sem=recv_sems.at[w * 3 + j],
                device_id=(px, py, c), device_id_type=MESH).wait_recv()
            fw = pltpu.make_async_remote_copy(
                src_ref=landed, dst_ref=landed,
                send_sem=fsend_sems.at[w * 3 + j], recv_sem=frecv_sems.at[w * 3 + j],
                device_id=sib, device_id_type=MESH)
            fw.start()
            fwds.append(fw)
    for w in range(n_w):
        theirs = half_rows(w, 1 - c)
        for j, (px, py) in enumerate(chips):
            passed = outs[w].at[2 * px + py, theirs]
            pltpu.make_async_remote_copy(
                src_ref=passed, dst_ref=passed,
                send_sem=fsend_sems.at[w * 3 + j], recv_sem=frecv_sems.at[w * 3 + j],
                device_id=sib, device_id_type=MESH).wait_recv()
    for cp in sends + fwds:
        cp.wait_send()
    for loc in locals_:
        loc.wait()


def _allgather_weights(shards):
    n_w = len(shards)
    return _pcall(
        functools.partial(_ag_body, n_w=n_w), name="weight_allgather",
        in_specs=[ANY] * n_w, out_specs=[ANY] * n_w,
        out_shape=[jax.ShapeDtypeStruct((N_CHIPS,) + s.shape, s.dtype) for s in shards],
        scratch_shapes=[pltpu.SemaphoreType.DMA((3 * n_w,))] * 4 + [pltpu.SemaphoreType.DMA((n_w,))],
    )(*shards)


def _pair_send_body(*refs, n_w):
    ins, outs = refs[:n_w], refs[n_w:2 * n_w]
    send_sems, recv_sems = refs[2 * n_w:]
    x, y, c, _ = _place()
    cps = []
    for w in range(n_w):
        cp = pltpu.make_async_remote_copy(
            src_ref=ins[w].at[:, 1 - c], dst_ref=outs[w],
            send_sem=send_sems.at[w], recv_sem=recv_sems.at[w],
            device_id=(x, y, 1 - c), device_id_type=MESH)
        cp.start()
        cps.append(cp)
    for cp in cps:
        cp.wait()


def _pair_send(grads4):
    n_w = len(grads4)
    return _pcall(
        functools.partial(_pair_send_body, n_w=n_w), name="grad_pair_exchange",
        in_specs=[ANY] * n_w, out_specs=[ANY] * n_w,
        out_shape=[jax.ShapeDtypeStruct((g.shape[0],) + g.shape[2:], g.dtype) for g in grads4],
        scratch_shapes=[pltpu.SemaphoreType.DMA((n_w,))] * 2,
    )(*grads4)


def _pair_add_body(c_ref, mine_ref, theirs_ref, o_ref):
    o_ref[...] = (mine_ref[...].astype(F32) + theirs_ref[...].astype(F32)).astype(o_ref.dtype)


def _pair_add(c_arr, g4, recv, name):
    _, _, hr, cols = g4.shape
    tr = min(hr, 256)
    grid_spec = pltpu.PrefetchScalarGridSpec(
        num_scalar_prefetch=1, grid=(N_CHIPS, hr // tr),
        in_specs=[pl.BlockSpec((None, None, tr, cols), lambda s, i, c: (s, c[0], i, 0)),
                  pl.BlockSpec((None, tr, cols), lambda s, i, c: (s, i, 0))],
        out_specs=pl.BlockSpec((None, tr, cols), lambda s, i, c: (s, i, 0)))
    return _pcall(
        _pair_add_body, name=name, grid_spec=grid_spec,
        out_shape=jax.ShapeDtypeStruct(recv.shape, recv.dtype),
        compiler_params=pltpu.CompilerParams(dimension_semantics=("parallel", "parallel")),
    )(c_arr, g4, recv)


def _scatter_body(*refs, n_w):
    ins, outs = refs[:n_w], refs[n_w:2 * n_w]
    send_sems, recv_sems, local_sems = refs[2 * n_w:]
    x, y, c, chips = _place()
    me = 2 * x + y
    todo = []
    for w in range(n_w):
        loc = pltpu.make_async_copy(ins[w].at[me], outs[w].at[me], local_sems.at[w])
        loc.start()
        todo.append(loc)
        for j, (px, py) in enumerate(chips):
            cp = pltpu.make_async_remote_copy(
                src_ref=ins[w].at[2 * px + py], dst_ref=outs[w].at[me],
                send_sem=send_sems.at[w * 3 + j], recv_sem=recv_sems.at[w * 3 + j],
                device_id=(px, py, c), device_id_type=MESH)
            cp.start()
            todo.append(cp)
    for t in todo:
        t.wait()


def _scatter(parts):
    n_w = len(parts)
    return _pcall(
        functools.partial(_scatter_body, n_w=n_w), name="grad_scatter",
        in_specs=[ANY] * n_w, out_specs=[ANY] * n_w,
        out_shape=[jax.ShapeDtypeStruct(p.shape, p.dtype) for p in parts],
        scratch_shapes=[pltpu.SemaphoreType.DMA((3 * n_w,))] * 2 + [pltpu.SemaphoreType.DMA((n_w,))],
    )(*parts)


def _sum4_body(p_ref, o_ref):
    o_ref[...] = ((p_ref[0].astype(F32) + p_ref[1].astype(F32)) + p_ref[2].astype(F32)) + p_ref[3].astype(F32)


def _sum4(p, name):
    _, hr, cols = p.shape
    tr = min(hr, 256)
    return _pcall(
        _sum4_body, name=name, grid=(hr // tr,),
        in_specs=[pl.BlockSpec((N_CHIPS, tr, cols), lambda i: (0, i, 0))],
        out_specs=pl.BlockSpec((tr, cols), lambda i: (i, 0)),
        out_shape=jax.ShapeDtypeStruct((hr, cols), F32),
        compiler_params=pltpu.CompilerParams(dimension_semantics=("parallel",)),
    )(p)


def _pair_join_body(*refs, n_w):
    ins, outs = refs[:n_w], refs[n_w:2 * n_w]
    send_sems, recv_sems, local_sems = refs[2 * n_w:]
    x, y, c, _ = _place()
    todo = []
    for w in range(n_w):
        loc = pltpu.make_async_copy(ins[w], outs[w].at[c], local_sems.at[w])
        loc.start()
        todo.append(loc)
        cp = pltpu.make_async_remote_copy(
            src_ref=ins[w], dst_ref=outs[w].at[c],
            send_sem=send_sems.at[w], recv_sem=recv_sems.at[w],
            device_id=(x, y, 1 - c), device_id_type=MESH)
        cp.start()
        todo.append(cp)
    for t in todo:
        t.wait()


def _pair_join(halves):
    n_w = len(halves)
    return _pcall(
        functools.partial(_pair_join_body, n_w=n_w), name="grad_pair_join",
        in_specs=[ANY] * n_w, out_specs=[ANY] * n_w,
        out_shape=[jax.ShapeDtypeStruct((2,) + h.shape, h.dtype) for h in halves],
        scratch_shapes=[pltpu.SemaphoreType.DMA((n_w,))] * 3,
    )(*halves)


def _small_gather_body(x_ref, out_ref, send_sems, recv_sems, local_sem):
    m_per = x_ref.shape[0]
    x, y, c, chips = _place()
    me, sibling = (x, y, c), (x, y, 1 - c)

    def rows(px, py, pc):
        return out_ref.at[pl.ds((4 * px + 2 * py + pc) * m_per, m_per), :]

    def copy(k, block, to, src=None):
        return pltpu.make_async_remote_copy(
            src_ref=rows(*block) if src is None else src, dst_ref=rows(*block),
            send_sem=send_sems.at[k], recv_sem=recv_sems.at[k], device_id=to, device_id_type=MESH)

    mine = pltpu.make_async_copy(x_ref, rows(*me), local_sem)
    mine.start()
    first = [copy(0, me, sibling, src=x_ref)]
    first += [copy(1 + j, me, (*chip, c), src=x_ref) for j, chip in enumerate(chips)]
    for cp in first:
        cp.start()
    passed = [copy(4 + j, (*chip, c), sibling) for j, chip in enumerate(chips)]
    for j, chip in enumerate(chips):
        copy(1 + j, (*chip, c), me).wait_recv()
        passed[j].start()
    copy(0, sibling, me).wait_recv()
    for j, chip in enumerate(chips):
        copy(4 + j, (*chip, 1 - c), me).wait_recv()
    for cp in first + passed:
        cp.wait_send()
    mine.wait()


def _small_gather(small):
    m_per, n = small.shape
    return _pcall(
        _small_gather_body, name="small_allgather",
        out_shape=jax.ShapeDtypeStruct((N_DEV * m_per, n), small.dtype),
        in_specs=[pl.BlockSpec(memory_space=pltpu.VMEM)],
        out_specs=pl.BlockSpec(memory_space=pltpu.VMEM),
        scratch_shapes=[pltpu.SemaphoreType.DMA((7,)), pltpu.SemaphoreType.DMA((7,)), pltpu.SemaphoreType.DMA],
    )(small)


def _adamw(w, g, m, v):
    m = ADAM_B1 * m + (1.0 - ADAM_B1) * g
    v = ADAM_B2 * v + (1.0 - ADAM_B2) * (g * g)
    m_hat = m / (1.0 - ADAM_B1 ** ADAM_STEP)
    v_hat = v / (1.0 - ADAM_B2 ** ADAM_STEP)
    delta = -ADAM_LR * (m_hat / (jnp.sqrt(v_hat) + ADAM_EPS) + ADAM_WD * w)
    return delta, m, v


def _adamw_body(w_ref, g_ref, m_ref, v_ref, d_ref, nm_ref, nv_ref):
    d, m, v = _adamw(w_ref[...], g_ref[...], m_ref[...], v_ref[...])
    d_ref[...] = d
    nm_ref[...] = m
    nv_ref[...] = v


def _adamw_call(w, g, m, v, name):
    rows, cols = w.shape
    tr = min(rows, 256)
    spec = pl.BlockSpec((tr, cols), lambda i: (i, 0))
    return _pcall(
        _adamw_body, name=name, grid=(rows // tr,),
        in_specs=[spec] * 4, out_specs=[spec] * 3,
        out_shape=[jax.ShapeDtypeStruct(w.shape, F32)] * 3,
        compiler_params=pltpu.CompilerParams(dimension_semantics=("parallel",)),
    )(w, g, m, v)


def _small_update_body(gath_ref, w_ref, m_ref, v_ref, g_ref, d_ref, nm_ref, nv_ref, loss_ref, *, n_gain):
    tot = gath_ref[0:1, :]
    for i in range(1, gath_ref.shape[0]):
        tot = tot + gath_ref[i:i + 1, :]
    g = tot[:, 0:n_gain]
    g_ref[...] = g
    d, m, v = _adamw(w_ref[...], g, m_ref[...], v_ref[...])
    d_ref[...] = d
    nm_ref[...] = m
    nv_ref[...] = v
    loss_ref[...] = (0.5 / D_MODEL) * jnp.sum(tot[:, n_gain:n_gain + HEAD], axis=1, keepdims=True) * jnp.ones((1, HEAD), F32)


def _small_update(gath, w, m, v):
    n_gain = w.shape[1]
    vm = pl.BlockSpec(memory_space=pltpu.VMEM)
    return _pcall(
        functools.partial(_small_update_body, n_gain=n_gain), name="gain_update",
        in_specs=[vm] * 4, out_specs=[vm] * 5,
        out_shape=[jax.ShapeDtypeStruct((1, n_gain), F32)] * 4 + [jax.ShapeDtypeStruct((1, HEAD), F32)],
    )(gath, w, m, v)


def kernel(x, positions, norm_attn_pre, norm_attn_post, w_in, q_latent_norm, kv_latent_norm, w_uq, w_ukv, w_out, norm_mlp_pre, norm_mlp_post, w_up, w_down, loss_target, m_norm_attn_pre, m_norm_attn_post, m_w_in, m_q_latent_norm, m_kv_latent_norm, m_w_uq, m_w_ukv, m_w_out, m_norm_mlp_pre, m_norm_mlp_post, m_w_up, m_w_down, v_norm_attn_pre, v_norm_attn_post, v_w_in, v_q_latent_norm, v_kv_latent_norm, v_w_uq, v_w_ukv, v_w_out, v_norm_mlp_pre, v_norm_mlp_post, v_w_up, v_w_down):
    T = x.shape[1]
    c_arr = lax.axis_index("c").astype(jnp.int32).reshape(1)

    mats = [w_in[0], w_uq[0], w_ukv[0], w_out[0], w_up[0], w_down[0]]
    shards = [w.astype(WIRE_DTYPE) for w in mats]
    win_g, wuq_g, wukv_g, wout_g, wup_g, wdown_g = _allgather_weights(shards)

    col_major = lambda g: jnp.transpose(g, (1, 0, 2)).reshape(g.shape[1], N_CHIPS * g.shape[2])
    win_full = col_major(win_g)
    w_main = win_full[:, :MAIN_COLS]
    w_kr = jnp.pad(win_full[:, MAIN_COLS:], ((0, 0), (0, HEAD - ROPE_B)))
    wuq_full = col_major(wuq_g).reshape(LORA, NH, HEAD + ROPE_B)
    w_uq_p = jnp.pad(wuq_full, ((0, 0), (0, 0), (0, QPAD - HEAD - ROPE_B))).reshape(LORA, NH * QPAD)
    w_ukv_p = col_major(wukv_g).reshape(LORA, NH, 2, HEAD).transpose(0, 2, 1, 3).reshape(LORA, 2 * A_W)
    w_out_f = wout_g.reshape(2 * A_W, D_MODEL)
    w_up_f = col_major(wup_g)
    w_down_f = wdown_g.reshape(D_FF, D_MODEL)
    cast = lambda a: a.astype(MXU_DTYPE)

    dx, gws, small = _local_step(
        x[0], positions[0].astype(F32).reshape(T, 1), loss_target[0],
        norm_attn_pre, norm_attn_post, q_latent_norm, kv_latent_norm, norm_mlp_pre, norm_mlp_post,
        cast(w_main), cast(w_kr), cast(w_uq_p), cast(w_ukv_p), cast(w_out_f), cast(w_up_f), cast(w_down_f))
    gw_main, gw_kr, gw_uq_p, gw_ukv_p, gw_out, gw_up, gw_down = gws

    to_shards = lambda g: jnp.transpose(g.reshape(g.shape[0], N_CHIPS, g.shape[1] // N_CHIPS), (1, 0, 2))
    gw_in = to_shards(jnp.concatenate([gw_main, gw_kr[:, :ROPE_B]], axis=1))
    gw_uq = to_shards(gw_uq_p.reshape(LORA, NH, QPAD)[:, :, :HEAD + ROPE_B].reshape(LORA, NH * (HEAD + ROPE_B)))
    gw_ukv = to_shards(gw_ukv_p.reshape(LORA, 2, NH, HEAD).transpose(0, 2, 1, 3).reshape(LORA, 2 * A_W))
    full = [gw_in, gw_uq, gw_ukv, gw_out.reshape(N_CHIPS, LORA, D_MODEL), to_shards(gw_up),
            gw_down.reshape(N_CHIPS, D_MODEL, D_MODEL)]
    names = ["w_in", "w_uq", "w_ukv", "w_out", "w_up", "w_down"]
    full4 = [g.reshape(N_CHIPS, 2, g.shape[1] // 2, g.shape[2]) for g in full]

    from_sib = _pair_send(full4)
    parts = [_pair_add(c_arr, g4, r, "pair_add_" + n) for g4, r, n in zip(full4, from_sib, names)]
    landed = _scatter(parts)
    halves = [_sum4(p, "chip_sum_" + n) for p, n in zip(landed, names)]
    joined = _pair_join(halves)
    grads = [j.reshape(2 * j.shape[1], j.shape[2]) for j in joined]

    ms = [m_w_in[0], m_w_uq[0], m_w_ukv[0], m_w_out[0], m_w_up[0], m_w_down[0]]
    vs = [v_w_in[0], v_w_uq[0], v_w_ukv[0], v_w_out[0], v_w_up[0], v_w_down[0]]
    upd = [_adamw_call(w, g, m, v, "adamw_" + n) for w, g, m, v, n in zip(mats, grads, ms, vs, names)]

    gath = _small_gather(small)
    gains = [norm_attn_pre, norm_attn_post, q_latent_norm, kv_latent_norm, norm_mlp_pre, norm_mlp_post]
    gm = [m_norm_attn_pre, m_norm_attn_post, m_q_latent_norm, m_kv_latent_norm, m_norm_mlp_pre, m_norm_mlp_post]
    gv = [v_norm_attn_pre, v_norm_attn_post, v_q_latent_norm, v_kv_latent_norm, v_norm_mlp_pre, v_norm_mlp_post]
    cat = lambda xs: jnp.concatenate(xs, axis=1)
    g_s, d_s, m_s, v_s, loss_v = _small_update(gath, cat(gains), cat(gm), cat(gv))
    widths = [a.shape[1] for a in gains]
    offs = [sum(widths[:i]) for i in range(len(widths))]
    split = lambda a: [a[:, o:o + w] for o, w in zip(offs, widths)]
    g_gain, d_gain, m_gain, v_gain = split(g_s), split(d_s), split(m_s), split(v_s)

    def ordered(gain_list, mat_list):
        gl, ml = gain_list, [a[None] for a in mat_list]
        return [gl[0], gl[1], ml[0], gl[2], gl[3], ml[1], ml[2], ml[3], gl[4], gl[5], ml[4], ml[5]]

    loss = loss_v[0, 0]
    return (loss, dx[None],
            *ordered(g_gain, grads),
            *ordered(d_gain, [u[0] for u in upd]),
            *ordered(m_gain, [u[1] for u in upd]),
            *ordered(v_gain, [u[2] for u in upd]))
```

```python
import functools

import jax
import jax.numpy as jnp
from jax import lax
from jax.experimental import pallas as pl
from jax.experimental.pallas import tpu as pltpu

F32 = jnp.float32
BF16 = jnp.bfloat16
MXU_DTYPE = jnp.bfloat16
WIRE_DTYPE = jnp.bfloat16

D_MODEL = 2048
HEAD = 128
NH = 8
A_W = NH * HEAD
LORA = 512
ROPE_B = 64
QPAD = 256
MAIN_COLS = 3 * A_W + 2 * LORA
IN_COLS = MAIN_COLS + ROPE_B
D_FF = 4 * D_MODEL
DIL = (1, 4, 16)
ROT_A = 32
ROPE_THETA = 500000.0
EPS = 1e-6
NEG = -1e30
N_CHIPS = 4
N_DEV = 8

ADAM_LR = 0.001
ADAM_B1 = 0.9
ADAM_B2 = 0.999
ADAM_EPS = 1e-08
ADAM_WD = 0.01
ADAM_STEP = 10

MESH = pl.DeviceIdType.MESH
ANY = pl.BlockSpec(memory_space=pl.ANY)


def _pcall(body, **kw):
    return pl.pallas_call(body, **kw)


_DIMS = {
    "nn": (((1,), (0,)), ((), ())),
    "nt": (((1,), (1,)), ((), ())),
    "tn": (((0,), (0,)), ((), ())),
}


def _mm_body(*refs, dims, nk, epi, n_extra, n_out):
    a_ref, b_ref = refs[0], refs[1]
    extra = refs[2:2 + n_extra]
    outs = refs[2 + n_extra:2 + n_extra + n_out]
    part = lax.dot_general(a_ref[...], b_ref[...], _DIMS[dims], preferred_element_type=F32)

    def finish(acc):
        res = epi(acc, *[r[...] for r in extra]) if epi is not None else (acc,)
        for o_ref, o in zip(outs, res):
            o_ref[...] = o.astype(o_ref.dtype)

    if nk == 1:
        finish(part)
        return
    acc_ref = refs[-1]
    k = pl.program_id(2)

    @pl.when(k == 0)
    def _():
        acc_ref[...] = part

    @pl.when(k > 0)
    def _():
        acc_ref[...] += part

    @pl.when(k == nk - 1)
    def _():
        finish(acc_ref[...])


def _matmul(a, b, *, dims, out_dtypes, tm, tn, tk, name, epi=None, extras=()):
    if dims == "nn":
        (M, K), (K2, N) = a.shape, b.shape
    elif dims == "nt":
        (M, K), (N, K2) = a.shape, b.shape
    else:
        (K, M), (K2, N) = a.shape, b.shape
    assert K == K2, (a.shape, b.shape, dims)
    tm, tn, tk = min(tm, M), min(tn, N), min(tk, K)
    assert M % tm == 0 and N % tn == 0 and K % tk == 0, (name, M, N, K, tm, tn, tk)
    nk = K // tk
    a_spec = {"nn": pl.BlockSpec((tm, tk), lambda i, j, k: (i, k)),
              "nt": pl.BlockSpec((tm, tk), lambda i, j, k: (i, k)),
              "tn": pl.BlockSpec((tk, tm), lambda i, j, k: (k, i))}[dims]
    b_spec = {"nn": pl.BlockSpec((tk, tn), lambda i, j, k: (k, j)),
              "nt": pl.BlockSpec((tn, tk), lambda i, j, k: (j, k)),
              "tn": pl.BlockSpec((tk, tn), lambda i, j, k: (k, j))}[dims]
    o_spec = pl.BlockSpec((tm, tn), lambda i, j, k: (i, j))
    body = functools.partial(_mm_body, dims=dims, nk=nk, epi=epi,
                             n_extra=len(extras), n_out=len(out_dtypes))
    res = _pcall(
        body, name=name,
        grid=(M // tm, N // tn, nk),
        in_specs=[a_spec, b_spec] + [o_spec] * len(extras),
        out_specs=[o_spec] * len(out_dtypes),
        out_shape=[jax.ShapeDtypeStruct((M, N), dt) for dt in out_dtypes],
        scratch_shapes=[pltpu.VMEM((tm, tn), F32)] if nk > 1 else [],
        compiler_params=pltpu.CompilerParams(
            dimension_semantics=("parallel", "parallel", "arbitrary")),
    )(a, b, *extras)
    return list(res)


def _rowwise(body, row_ins, vec_ins, row_outs, acc_outs, *, tr, name):
    T = row_ins[0].shape[0]
    tr = min(tr, T)
    assert T % tr == 0
    in_specs = [pl.BlockSpec((tr, a.shape[1]), lambda i: (i, 0)) for a in row_ins]
    in_specs += [pl.BlockSpec(a.shape, lambda i: (0, 0)) for a in vec_ins]
    out_specs = [pl.BlockSpec((tr, w), lambda i: (i, 0)) for (w, _) in row_outs]
    out_specs += [pl.BlockSpec(s, lambda i: (0, 0)) for s in acc_outs]
    out_shape = [jax.ShapeDtypeStruct((T, w), dt) for (w, dt) in row_outs]
    out_shape += [jax.ShapeDtypeStruct(s, F32) for s in acc_outs]
    sem = "arbitrary" if acc_outs else "parallel"
    return list(_pcall(
        body, name=name, grid=(T // tr,), in_specs=in_specs, out_specs=out_specs,
        out_shape=out_shape,
        compiler_params=pltpu.CompilerParams(dimension_semantics=(sem,)),
    )(*row_ins, *vec_ins))


def _rstd(x):
    return lax.rsqrt(jnp.mean(x * x, axis=-1, keepdims=True) + EPS)


def _rms_bwd(x, rstd, dyg):
    xh = x * rstd
    return rstd * (dyg - xh * jnp.mean(dyg * xh, axis=-1, keepdims=True)), xh


def _fold8(v):
    r, w = v.shape
    return jnp.sum(v.reshape(r // 8, 8, w), axis=0)


def _acc(ref, val):
    first = pl.program_id(0) == 0

    @pl.when(first)
    def _():
        ref[...] = val

    @pl.when(jnp.logical_not(first))
    def _():
        ref[...] += val


def _rope(x, c, sa, sb, half):
    return x * c + pltpu.roll(x, HEAD - half, 1) * sa + pltpu.roll(x, half, 1) * sb


def _rope_t(dy, c, sa, sb, half):
    return dy * c - pltpu.roll(dy, HEAD - half, 1) * sa - pltpu.roll(dy, half, 1) * sb


def _rope_tab_body(pos_ref, inv_ref, ca, saa, sab, cb, sba, sbb):
    pos = pos_ref[...]
    lane = lax.broadcasted_iota(jnp.int32, (pos.shape[0], HEAD), 1)
    ang_a = pos * inv_ref[0:1, :]
    ang_b = pos * inv_ref[1:2, :]
    c, s = jnp.cos(ang_a), jnp.sin(ang_a)
    ha = ROT_A // 2
    ca[...] = jnp.where(lane < ROT_A, c, 1.0)
    saa[...] = jnp.where(lane < ha, -s, 0.0)
    sab[...] = jnp.where((lane >= ha) & (lane < ROT_A), s, 0.0)
    c, s = jnp.cos(ang_b), jnp.sin(ang_b)
    hb = ROPE_B // 2
    cb[...] = jnp.where(lane < ROPE_B, c, 1.0)
    sba[...] = jnp.where(lane < hb, -s, 0.0)
    sbb[...] = jnp.where((lane >= hb) & (lane < ROPE_B), s, 0.0)


def _rms_fwd_body(x_ref, g_ref, h_ref):
    x = x_ref[...]
    h_ref[...] = ((x * _rstd(x)) * g_ref[...]).astype(h_ref.dtype)


def _postproj_body(p_ref, kr_ref, ca, saa, sab, cb, sba, sbb, gq_ref, gkv_ref,
                   q_ref, k_ref, v_ref, cqn_ref, ckvn_ref, krope_ref):
    c, sa, sb = ca[...], saa[...], sab[...]
    for h in range(NH):
        lo = h * HEAD
        q_ref[:, lo:lo + HEAD] = _rope(p_ref[:, lo:lo + HEAD], c, sa, sb, ROT_A // 2).astype(q_ref.dtype)
        k_ref[:, lo:lo + HEAD] = _rope(p_ref[:, A_W + lo:A_W + lo + HEAD], c, sa, sb, ROT_A // 2).astype(k_ref.dtype)
    v_ref[...] = p_ref[:, 2 * A_W:3 * A_W].astype(v_ref.dtype)
    cq = p_ref[:, 3 * A_W:3 * A_W + LORA]
    cqn_ref[...] = ((cq * _rstd(cq)) * gq_ref[...]).astype(cqn_ref.dtype)
    ckv = p_ref[:, 3 * A_W + LORA:MAIN_COLS]
    ckvn_ref[...] = ((ckv * _rstd(ckv)) * gkv_ref[...]).astype(ckvn_ref.dtype)
    krope_ref[...] = _rope(kr_ref[...], cb[...], sba[...], sbb[...], ROPE_B // 2).astype(krope_ref.dtype)


def _qrope_body(qp_ref, cb, sba, sbb, q_ref):
    c, sa, sb = cb[...], sba[...], sbb[...]
    for h in range(NH):
        lo = h * QPAD
        q_ref[:, lo:lo + HEAD] = qp_ref[:, lo:lo + HEAD].astype(q_ref.dtype)
        q_ref[:, lo + HEAD:lo + QPAD] = _rope(qp_ref[:, lo + HEAD:lo + QPAD], c, sa, sb, ROPE_B // 2).astype(q_ref.dtype)


def _qrope_t_body(dq_ref, cb, sba, sbb, o_ref):
    c, sa, sb = cb[...], sba[...], sbb[...]
    for h in range(NH):
        lo = h * QPAD
        o_ref[:, lo:lo + HEAD] = dq_ref[:, lo:lo + HEAD].astype(o_ref.dtype)
        o_ref[:, lo + HEAD:lo + QPAD] = _rope_t(dq_ref[:, lo + HEAD:lo + QPAD], c, sa, sb, ROPE_B // 2).astype(o_ref.dtype)


def _mid_body(x_ref, o_ref, g2_ref, g3_ref, x1_ref, h2_ref):
    o = o_ref[...]
    x1 = x_ref[...] + (o * _rstd(o)) * g2_ref[...]
    x1_ref[...] = x1
    h2_ref[...] = ((x1 * _rstd(x1)) * g3_ref[...]).astype(h2_ref.dtype)


def _loss_body(x1_ref, d_ref, t_ref, g4_ref, dy_ref, dd_ref, loss_ref, dg4_ref):
    d = d_ref[...]
    rstd = _rstd(d)
    y = x1_ref[...] + (d * rstd) * g4_ref[...]
    e = y - t_ref[...]
    dy = e * (1.0 / D_MODEL)
    dy_ref[...] = dy
    dd, dh = _rms_bwd(d, rstd, dy * g4_ref[...])
    dd_ref[...] = dd.astype(dd_ref.dtype)
    _acc(dg4_ref, _fold8(dy * dh))
    e8 = _fold8(e * e)
    l = e8[:, 0:HEAD]
    for j in range(1, D_MODEL // HEAD):
        l = l + e8[:, j * HEAD:(j + 1) * HEAD]
    _acc(loss_ref, l)


def _bmid_body(dy_ref, dh2_ref, x1_ref, o_ref, g2_ref, g3_ref, dx1_ref, do_ref, dg3_ref, dg2_ref):
    x1 = x1_ref[...]
    dh2 = dh2_ref[...]
    dn, x1h = _rms_bwd(x1, _rstd(x1), dh2 * g3_ref[...])
    dx1 = dy_ref[...] + dn
    dx1_ref[...] = dx1
    _acc(dg3_ref, _fold8(dh2 * x1h))
    o = o_ref[...]
    do, oh = _rms_bwd(o, _rstd(o), dx1 * g2_ref[...])
    do_ref[...] = do.astype(do_ref.dtype)
    _acc(dg2_ref, _fold8(dx1 * oh))


def _delta_body(dm_ref, mx_ref, dmb_ref, delta_ref):
    for h in range(2 * NH):
        lo = h * HEAD
        dm = dm_ref[:, lo:lo + HEAD]
        dmb_ref[:, lo:lo + HEAD] = dm.astype(dmb_ref.dtype)
        dl = jnp.sum(dm * mx_ref[:, lo:lo + HEAD].astype(F32), axis=1, keepdims=True)
        delta_ref[:, lo:lo + HEAD] = jnp.broadcast_to(dl, dm.shape)


def _dproj_body(dq_ref, dk_ref, dv_ref, dcq_ref, dckv_ref, p_ref, dkr_ref,
                ca, saa, sab, cb, sba, sbb, gq_ref, gkv_ref,
                dp_ref, dkrp_ref, dgq_ref, dgkv_ref):
    c, sa, sb = ca[...], saa[...], sab[...]
    for h in range(NH):
        lo = h * HEAD
        dp_ref[:, lo:lo + HEAD] = _rope_t(dq_ref[:, lo:lo + HEAD], c, sa, sb, ROT_A // 2).astype(dp_ref.dtype)
        dp_ref[:, A_W + lo:A_W + lo + HEAD] = _rope_t(dk_ref[:, lo:lo + HEAD], c, sa, sb, ROT_A // 2).astype(dp_ref.dtype)
    dp_ref[:, 2 * A_W:3 * A_W] = dv_ref[...].astype(dp_ref.dtype)
    cq = p_ref[:, 3 * A_W:3 * A_W + LORA]
    dcqn = dcq_ref[...]
    dcq, cqh = _rms_bwd(cq, _rstd(cq), dcqn * gq_ref[...])
    dp_ref[:, 3 * A_W:3 * A_W + LORA] = dcq.astype(dp_ref.dtype)
    _acc(dgq_ref, _fold8(dcqn * cqh))
    ckv = p_ref[:, 3 * A_W + LORA:MAIN_COLS]
    dckvn = dckv_ref[...]
    dckv, ckvh = _rms_bwd(ckv, _rstd(ckv), dckvn * gkv_ref[...])
    dp_ref[:, 3 * A_W + LORA:MAIN_COLS] = dckv.astype(dp_ref.dtype)
    _acc(dgkv_ref, _fold8(dckvn * ckvh))
    dkrp_ref[...] = _rope_t(dkr_ref[...], cb[...], sba[...], sbb[...], ROPE_B // 2).astype(dkrp_ref.dtype)


def _bin_body(dx1_ref, dha_ref, dhb_ref, x_ref, g1_ref, dx_ref, dg1_ref):
    x = x_ref[...]
    dh = dha_ref[...] + dhb_ref[...]
    dn, xh = _rms_bwd(x, _rstd(x), dh * g1_ref[...])
    dx_ref[...] = dx1_ref[...] + dn
    _acc(dg1_ref, _fold8(dh * xh))


def _dot_nt(a, b):
    return lax.dot_general(a, b, _DIMS["nt"], preferred_element_type=F32)


def _dot_tn(a, b):
    return lax.dot_general(a, b, _DIMS["tn"], preferred_element_type=F32)


def _dot_nn(a, b):
    return jnp.dot(a, b, preferred_element_type=F32)


DIL_SCALE = HEAD ** -0.5
DIL_CHUNK = 256


def _dil_rows(t, d):
    r = t & (d - 1)
    n = t >> (d.bit_length() - 1)
    start = r + n * (HEAD * d)
    has_prev = n > 0
    pstart = jnp.where(has_prev, start - HEAD * d, start)
    if d == 1:
        return pl.ds(pl.multiple_of(start, HEAD), HEAD), pl.ds(pl.multiple_of(pstart, HEAD), HEAD), has_prev
    return pl.ds(start, HEAD, stride=d), pl.ds(pstart, HEAD, stride=d), has_prev


def _dil_band():
    row = lax.broadcasted_iota(jnp.int32, (HEAD, 2 * HEAD), 0)
    col = lax.broadcasted_iota(jnp.int32, (HEAD, 2 * HEAD), 1)
    return (col >= row) & (col <= row + HEAD), col >= HEAD


def _dil_fwd_body(q_ref, k_ref, v_ref, a_ref, lse_ref, o1, o2, o3, l1, l2, l3, *, nt, unroll):
    band, is_cur = _dil_band()
    for d, o_sc, l_sc in zip(DIL, (o1, o2, o3), (l1, l2, l3)):

        def tile(t, carry, d=d, o_sc=o_sc, l_sc=l_sc):
            rows, prows, has_prev = _dil_rows(t, d)
            q = q_ref[rows, :].astype(MXU_DTYPE)
            kk = jnp.concatenate([k_ref[prows, :], k_ref[rows, :]], axis=0).astype(MXU_DTYPE)
            vv = jnp.concatenate([v_ref[prows, :], v_ref[rows, :]], axis=0).astype(MXU_DTYPE)
            ok = band & (is_cur | has_prev)
            s = jnp.where(ok, _dot_nt(q, kk) * DIL_SCALE, NEG)
            m = jnp.max(s, axis=1, keepdims=True)
            p = jnp.exp(s - m)
            den = jnp.sum(p, axis=1, keepdims=True)
            o_sc[rows, :] = _dot_nn((p / den).astype(MXU_DTYPE), vv)
            l_sc[rows, :] = jnp.broadcast_to(m + jnp.log(den), (HEAD, HEAD))
            return carry

        lax.fori_loop(0, nt, tile, 0, unroll=unroll)

    def merge(i, carry):
        rs = pl.ds(pl.multiple_of(i * DIL_CHUNK, DIL_CHUNK), DIL_CHUNK)
        la, lb, lc = l1[rs, :], l2[rs, :], l3[rs, :]
        m = jnp.maximum(jnp.maximum(la, lb), lc)
        wa, wb, wc = jnp.exp(la - m), jnp.exp(lb - m), jnp.exp(lc - m)
        den = wa + wb + wc
        a = (wa / den) * o1[rs, :] + (wb / den) * o2[rs, :] + (wc / den) * o3[rs, :]
        a_ref[rs, :] = a.astype(a_ref.dtype)
        lse_ref[rs, :] = m + jnp.log(den)
        return carry

    lax.fori_loop(0, q_ref.shape[0] // DIL_CHUNK, merge, 0)


def _dil_fwd(q, k, v):
    T = q.shape[0]
    spec = pl.BlockSpec((T, HEAD), lambda h: (0, h))
    return _pcall(
        functools.partial(_dil_fwd_body, nt=T // HEAD, unroll=4), name="dil_fwd",
        grid=(NH,), in_specs=[spec] * 3, out_specs=[spec] * 2,
        out_shape=[jax.ShapeDtypeStruct((T, A_W), MXU_DTYPE), jax.ShapeDtypeStruct((T, A_W), F32)],
        scratch_shapes=[pltpu.VMEM((T, HEAD), F32)] * 6,
        compiler_params=pltpu.CompilerParams(dimension_semantics=("parallel",)),
    )(q, k, v)


def _dil_bwd_body(q_ref, k_ref, v_ref, do_ref, a_ref, lse_ref, dq_ref, dk_ref, dv_ref, dl_sc, *, nt, unroll):
    band, is_cur = _dil_band()

    def prep(i, carry):
        rs = pl.ds(pl.multiple_of(i * DIL_CHUNK, DIL_CHUNK), DIL_CHUNK)
        dl = jnp.sum(do_ref[rs, :] * a_ref[rs, :].astype(F32), axis=1, keepdims=True)
        dl_sc[rs, :] = jnp.broadcast_to(dl, (DIL_CHUNK, HEAD))
        zero = jnp.zeros((DIL_CHUNK, HEAD), F32)
        dq_ref[rs, :] = zero
        dk_ref[rs, :] = zero
        dv_ref[rs, :] = zero
        return carry

    lax.fori_loop(0, q_ref.shape[0] // DIL_CHUNK, prep, 0)

    for d in DIL:

        def tile(t, carry, d=d):
            rows, prows, has_prev = _dil_rows(t, d)
            q = q_ref[rows, :].astype(MXU_DTYPE)
            kk = jnp.concatenate([k_ref[prows, :], k_ref[rows, :]], axis=0).astype(MXU_DTYPE)
            vv = jnp.concatenate([v_ref[prows, :], v_ref[rows, :]], axis=0).astype(MXU_DTYPE)
            do = do_ref[rows, :].astype(MXU_DTYPE)
            lse = lse_ref[rows, :]
            dl = dl_sc[rows, :]
            ok = band & (is_cur | has_prev)
            s = _dot_nt(q, kk) * DIL_SCALE
            p = jnp.where(ok, jnp.exp(s - jnp.concatenate([lse, lse], axis=1)), 0.0)
            ds = (p * (_dot_nt(do, vv) - jnp.concatenate([dl, dl], axis=1))).astype(MXU_DTYPE)
            dq_ref[rows, :] += _dot_nn(ds, kk) * DIL_SCALE
            dkk = _dot_tn(ds, q) * DIL_SCALE
            dvv = _dot_tn(p.astype(MXU_DTYPE), do)
            dk_ref[rows, :] += dkk[HEAD:, :]
            dv_ref[rows, :] += dvv[HEAD:, :]
            dk_ref[prows, :] += dkk[:HEAD, :]
            dv_ref[prows, :] += dvv[:HEAD, :]
            return carry

        lax.fori_loop(0, nt, tile, 0, unroll=unroll)


def _dil_bwd(q, k, v, dmix, mixed, lse):
    T = q.shape[0]
    spec = pl.BlockSpec((T, HEAD), lambda h: (0, h))
    return _pcall(
        functools.partial(_dil_bwd_body, nt=T // HEAD, unroll=2), name="dil_bwd",
        grid=(NH,), in_specs=[spec] * 6, out_specs=[spec] * 3,
        out_shape=[jax.ShapeDtypeStruct((T, A_W), F32)] * 3,
        scratch_shapes=[pltpu.VMEM((T, HEAD), F32)],
        compiler_params=pltpu.CompilerParams(dimension_semantics=("parallel",)),
    )(q, k, v, dmix, mixed, lse)


MLA_SCALE = (HEAD + ROPE_B) ** -0.5


def _causal(qi, ki, tq, tk):
    row = lax.broadcasted_iota(jnp.int32, (tq, tk), 0) + qi * tq
    col = lax.broadcasted_iota(jnp.int32, (tq, tk), 1) + ki * tk
    return col <= row


def _mla_fwd_body(q_ref, kn_ref, kr_ref, v_ref, o_ref, lse_ref, m_sc, l_sc, acc_sc, *, tq, tk):
    qi, ki = pl.program_id(1), pl.program_id(2)

    @pl.when(ki == 0)
    def _():
        m_sc[...] = jnp.full(m_sc.shape, NEG, F32)
        l_sc[...] = jnp.zeros(l_sc.shape, F32)
        acc_sc[...] = jnp.zeros(acc_sc.shape, F32)

    @pl.when(ki * tk <= qi * tq + (tq - 1))
    def _():
        kcat = jnp.concatenate([kn_ref[...], kr_ref[...]], axis=1)
        s = _dot_nt(q_ref[...], kcat) * MLA_SCALE
        s = jnp.where(_causal(qi, ki, tq, tk), s, NEG)
        m_prev = m_sc[:, 0:1]
        m_new = jnp.maximum(m_prev, jnp.max(s, axis=1, keepdims=True))
        alpha = jnp.exp(m_prev - m_new)
        p = jnp.exp(s - m_new)
        l_sc[...] = jnp.broadcast_to(alpha * l_sc[:, 0:1] + jnp.sum(p, axis=1, keepdims=True), l_sc.shape)
        acc_sc[...] = alpha * acc_sc[...] + _dot_nn(p.astype(MXU_DTYPE), v_ref[...])
        m_sc[...] = jnp.broadcast_to(m_new, m_sc.shape)

    @pl.when(ki == pl.num_programs(2) - 1)
    def _():
        l = l_sc[...]
        o_ref[...] = (acc_sc[...] / l).astype(o_ref.dtype)
        lse_ref[...] = m_sc[...] + jnp.log(l)


def _mla_fwd(qf, kv, kr, *, tq, tk):
    T = qf.shape[0]
    tq, tk = min(tq, T), min(tk, T)
    nq, nk = T // tq, T // tk
    last = lambda i, j: jnp.minimum(j, (i * tq + tq - 1) // tk)
    return _pcall(
        functools.partial(_mla_fwd_body, tq=tq, tk=tk), name="mla_fwd",
        grid=(NH, nq, nk),
        in_specs=[pl.BlockSpec((tq, QPAD), lambda h, i, j: (i, h)),
                  pl.BlockSpec((tk, HEAD), lambda h, i, j: (last(i, j), h)),
                  pl.BlockSpec((tk, HEAD), lambda h, i, j: (last(i, j), 0)),
                  pl.BlockSpec((tk, HEAD), lambda h, i, j: (last(i, j), NH + h))],
        out_specs=[pl.BlockSpec((tq, HEAD), lambda h, i, j: (i, h)),
                   pl.BlockSpec((tq, HEAD), lambda h, i, j: (i, h))],
        out_shape=[jax.ShapeDtypeStruct((T, A_W), MXU_DTYPE), jax.ShapeDtypeStruct((T, A_W), F32)],
        scratch_shapes=[pltpu.VMEM((tq, HEAD), F32), pltpu.VMEM((tq, HEAD), F32), pltpu.VMEM((tq, HEAD), F32)],
        compiler_params=pltpu.CompilerParams(dimension_semantics=("parallel", "parallel", "arbitrary")),
    )(qf, kv, kr, kv)


def _mla_p_ds(q, kcat, v, do, lse, dl, qi, ki, tq, tk):
    s = _dot_nt(q, kcat) * MLA_SCALE
    p = jnp.where(_causal(qi, ki, tq, tk), jnp.exp(s - lse[:, 0:1]), 0.0)
    ds = p * (_dot_nt(do, v) - dl[:, 0:1])
    return p.astype(MXU_DTYPE), ds.astype(MXU_DTYPE)


def _mla_dq_body(q_ref, kn_ref, kr_ref, v_ref, do_ref, lse_ref, dl_ref, dq_ref, acc_sc, *, tq, tk):
    qi, ki = pl.program_id(1), pl.program_id(2)

    @pl.when(ki == 0)
    def _():
        acc_sc[...] = jnp.zeros(acc_sc.shape, F32)

    @pl.when(ki * tk <= qi * tq + (tq - 1))
    def _():
        kcat = jnp.concatenate([kn_ref[...], kr_ref[...]], axis=1)
        _, ds = _mla_p_ds(q_ref[...], kcat, v_ref[...], do_ref[...], lse_ref[...], dl_ref[...], qi, ki, tq, tk)
        acc_sc[...] += _dot_nn(ds, kcat)

    @pl.when(ki == pl.num_programs(2) - 1)
    def _():
        dq_ref[...] = acc_sc[...] * MLA_SCALE


def _mla_dq(qf, kv, kr, dob, lse, dl, *, tq, tk):
    T = qf.shape[0]
    tq, tk = min(tq, T), min(tk, T)
    nq, nk = T // tq, T // tk
    last = lambda i, j: jnp.minimum(j, (i * tq + tq - 1) // tk)
    qspec = pl.BlockSpec((tq, HEAD), lambda h, i, j: (i, h))
    return _pcall(
        functools.partial(_mla_dq_body, tq=tq, tk=tk), name="mla_dq",
        grid=(NH, nq, nk),
        in_specs=[pl.BlockSpec((tq, QPAD), lambda h, i, j: (i, h)),
                  pl.BlockSpec((tk, HEAD), lambda h, i, j: (last(i, j), h)),
                  pl.BlockSpec((tk, HEAD), lambda h, i, j: (last(i, j), 0)),
                  pl.BlockSpec((tk, HEAD), lambda h, i, j: (last(i, j), NH + h)),
                  pl.BlockSpec((tq, HEAD), lambda h, i, j: (i, NH + h)), qspec,
                  pl.BlockSpec((tq, HEAD), lambda h, i, j: (i, NH + h))],
        out_specs=pl.BlockSpec((tq, QPAD), lambda h, i, j: (i, h)),
        out_shape=jax.ShapeDtypeStruct((T, NH * QPAD), F32),
        scratch_shapes=[pltpu.VMEM((tq, QPAD), F32)],
        compiler_params=pltpu.CompilerParams(dimension_semantics=("parallel", "parallel", "arbitrary")),
    )(qf, kv, kr, kv, dob, lse, dl)


def _mla_dkv_body(q_ref, kn_ref, kr_ref, v_ref, do_ref, lse_ref, dl_ref, dkn_ref, dv_ref, dkr_ref,
                  dk_sc, dv_sc, dkr_sc, *, tq, tk):
    ki, h, qi = pl.program_id(0), pl.program_id(1), pl.program_id(2)
    nq = pl.num_programs(2)

    @pl.when(jnp.logical_and(h == 0, qi == 0))
    def _():
        dkr_sc[...] = jnp.zeros(dkr_sc.shape, F32)

    @pl.when(qi == 0)
    def _():
        dk_sc[...] = jnp.zeros(dk_sc.shape, F32)
        dv_sc[...] = jnp.zeros(dv_sc.shape, F32)

    @pl.when(ki * tk <= qi * tq + (tq - 1))
    def _():
        kcat = jnp.concatenate([kn_ref[...], kr_ref[...]], axis=1)
        q, do = q_ref[...], do_ref[...]
        p, ds = _mla_p_ds(q, kcat, v_ref[...], do, lse_ref[...], dl_ref[...], qi, ki, tq, tk)
        dk_sc[...] += _dot_tn(ds, q)
        dv_sc[...] += _dot_tn(p, do)

    @pl.when(qi == nq - 1)
    def _():
        dk = dk_sc[...] * MLA_SCALE
        dkn_ref[...] = dk[:, 0:HEAD].astype(dkn_ref.dtype)
        dv_ref[...] = dv_sc[...].astype(dv_ref.dtype)
        dkr_sc[...] += dk[:, HEAD:QPAD]

    @pl.when(jnp.logical_and(h == NH - 1, qi == nq - 1))
    def _():
        dkr_ref[...] = dkr_sc[...]


def _mla_dkv(qf, kv, kr, dob, lse, dl, *, tq, tk):
    T = qf.shape[0]
    tq, tk = min(tq, T), min(tk, T)
    nq, nk = T // tq, T // tk
    first = lambda j, i: jnp.maximum(i, (j * tk) // tq)
    kspec = pl.BlockSpec((tk, HEAD), lambda j, h, i: (j, h))
    outs = _pcall(
        functools.partial(_mla_dkv_body, tq=tq, tk=tk), name="mla_dkv",
        grid=(nk, NH, nq),
        in_specs=[pl.BlockSpec((tq, QPAD), lambda j, h, i: (first(j, i), h)),
                  kspec,
                  pl.BlockSpec((tk, HEAD), lambda j, h, i: (j, 0)),
                  pl.BlockSpec((tk, HEAD), lambda j, h, i: (j, NH + h)),
                  pl.BlockSpec((tq, HEAD), lambda j, h, i: (first(j, i), NH + h)),
                  pl.BlockSpec((tq, HEAD), lambda j, h, i: (first(j, i), h)),
                  pl.BlockSpec((tq, HEAD), lambda j, h, i: (first(j, i), NH + h))],
        out_specs=[kspec, kspec, pl.BlockSpec((tk, HEAD), lambda j, h, i: (j, 0))],
        out_shape=[jax.ShapeDtypeStruct((T, A_W), MXU_DTYPE), jax.ShapeDtypeStruct((T, A_W), MXU_DTYPE),
                   jax.ShapeDtypeStruct((T, HEAD), F32)],
        scratch_shapes=[pltpu.VMEM((tk, QPAD), F32), pltpu.VMEM((tk, HEAD), F32), pltpu.VMEM((tk, HEAD), F32)],
        compiler_params=pltpu.CompilerParams(dimension_semantics=("parallel", "arbitrary", "arbitrary")),
    )(qf, kv, kr, kv, dob, lse, dl)
    return outs


def _local_step(x, pos, target, g1, g2, gq, gkv, g3, g4, w_main, w_kr, w_uq_p, w_ukv_p, w_out, w_up, w_down):
    T = x.shape[0]
    TR = 256
    mm = functools.partial(_matmul, tm=512, tn=1024, tk=2048)

    inv_a = ROPE_THETA ** (-jnp.arange(0, ROT_A, 2, dtype=F32) / ROT_A)
    inv_b = ROPE_THETA ** (-jnp.arange(0, ROPE_B, 2, dtype=F32) / ROPE_B)
    inv = jnp.stack([jnp.concatenate([inv_a, inv_a, jnp.zeros((HEAD - ROT_A,), F32)]),
                     jnp.concatenate([inv_b, inv_b, jnp.zeros((HEAD - ROPE_B,), F32)])])
    inv = jnp.concatenate([inv, jnp.zeros((6, HEAD), F32)], axis=0)
    tabs = _rowwise(_rope_tab_body, [pos], [inv], [(HEAD, F32)] * 6, [], tr=512, name="rope_tables")

    (h,) = _rowwise(_rms_fwd_body, [x], [g1], [(D_MODEL, MXU_DTYPE)], [], tr=TR, name="rms_in")
    (proj,) = mm(h, w_main, dims="nn", out_dtypes=[F32], name="proj_main")
    (kr_raw,) = mm(h, w_kr, dims="nn", out_dtypes=[F32], name="proj_kr")
    q, k, v, cqn, ckvn, krope = _rowwise(
        _postproj_body, [proj, kr_raw] + tabs, [gq, gkv],
        [(A_W, F32)] * 3 + [(LORA, MXU_DTYPE)] * 2 + [(HEAD, MXU_DTYPE)], [], tr=TR, name="post_proj")
    a_out, lse_a = _dil_fwd(q, k, v)

    (q_pad,) = mm(cqn, w_uq_p, dims="nn", out_dtypes=[F32], name="q_up")
    (qf,) = _rowwise(_qrope_body, [q_pad] + tabs[3:], [], [(NH * QPAD, MXU_DTYPE)], [], tr=TR, name="q_rope")
    (kv,) = mm(ckvn, w_ukv_p, dims="nn", out_dtypes=[MXU_DTYPE], name="kv_up")
    b_out, lse_b = _mla_fwd(qf, kv, krope, tq=512, tk=512)

    mixed = jnp.concatenate([a_out, b_out], axis=1)
    (o,) = mm(mixed, w_out, dims="nn", out_dtypes=[F32], name="out_proj")
    x1, h2 = _rowwise(_mid_body, [x, o], [g2, g3], [(D_MODEL, F32), (D_MODEL, MXU_DTYPE)], [], tr=TR, name="mid_norm")

    def up_epi(acc):
        r = jnp.maximum(acc, 0.0)
        return r * r, r
    u, r = mm(h2, w_up, dims="nn", out_dtypes=[MXU_DTYPE, MXU_DTYPE], name="mlp_up", epi=up_epi)
    (dn,) = mm(u, w_down, dims="nn", out_dtypes=[F32], name="mlp_down")
    dy, dd, loss8, dg4 = _rowwise(_loss_body, [x1, dn, target], [g4], [(D_MODEL, F32), (D_MODEL, MXU_DTYPE)],
                                  [(8, HEAD), (8, D_MODEL)], tr=TR, name="loss_head")

    def dup_epi(acc, rr):
        return (acc * (2.0 * rr.astype(F32)),)
    (dup,) = mm(dd, w_down, dims="nt", out_dtypes=[MXU_DTYPE], name="d_up", epi=dup_epi, extras=(r,))
    (gw_down,) = mm(u, dd, dims="tn", out_dtypes=[WIRE_DTYPE], name="gw_down")
    (dh2,) = mm(dup, w_up, dims="nt", out_dtypes=[F32], name="d_h2")
    (gw_up,) = mm(h2, dup, dims="tn", out_dtypes=[WIRE_DTYPE], name="gw_up")
    dx1, do, dg3, dg2 = _rowwise(_bmid_body, [dy, dh2, x1, o], [g2, g3], [(D_MODEL, F32), (D_MODEL, MXU_DTYPE)],
                                 [(8, D_MODEL), (8, D_MODEL)], tr=TR, name="bwd_mid")
    (dmix,) = mm(do, w_out, dims="nt", out_dtypes=[F32], name="d_mixed")
    (gw_out,) = mm(mixed, do, dims="tn", out_dtypes=[WIRE_DTYPE], name="gw_out")
    dmb, delta = _rowwise(_delta_body, [dmix, mixed], [], [(2 * A_W, MXU_DTYPE), (2 * A_W, F32)], [], tr=TR, name="attn_delta")

    dqf = _mla_dq(qf, kv, krope, dmb, lse_b, delta, tq=512, tk=512)
    dkn, dvb, dkr = _mla_dkv(qf, kv, krope, dmb, lse_b, delta, tq=512, tk=512)
    (dq_pad,) = _rowwise(_qrope_t_body, [dqf] + tabs[3:], [], [(NH * QPAD, MXU_DTYPE)], [], tr=TR, name="q_rope_t")
    (dcqn,) = mm(dq_pad, w_uq_p, dims="nt", out_dtypes=[F32], name="d_cq")
    (gw_uq_p,) = mm(cqn, dq_pad, dims="tn", out_dtypes=[WIRE_DTYPE], name="gw_uq")
    dkv = jnp.concatenate([dkn, dvb], axis=1)
    (dckvn,) = mm(dkv, w_ukv_p, dims="nt", out_dtypes=[F32], name="d_ckv")
    (gw_ukv_p,) = mm(ckvn, dkv, dims="tn", out_dtypes=[WIRE_DTYPE], name="gw_ukv")

    dq_a, dk_a, dv_a = _dil_bwd(q, k, v, dmix, mixed, lse_a)
    dproj, dkrp, dgq, dgkv = _rowwise(
        _dproj_body, [dq_a, dk_a, dv_a, dcqn, dckvn, proj, dkr] + tabs, [gq, gkv],
        [(MAIN_COLS, MXU_DTYPE), (HEAD, MXU_DTYPE)], [(8, LORA), (8, LORA)], tr=TR, name="d_proj")
    (dha,) = mm(dproj, w_main, dims="nt", out_dtypes=[F32], name="d_h_main")
    (dhb,) = mm(dkrp, w_kr, dims="nt", out_dtypes=[F32], name="d_h_kr")
    (gw_main,) = mm(h, dproj, dims="tn", out_dtypes=[WIRE_DTYPE], name="gw_in_main")
    (gw_kr,) = mm(h, dkrp, dims="tn", out_dtypes=[WIRE_DTYPE], name="gw_in_kr")
    dx, dg1 = _rowwise(_bin_body, [dx1, dha, dhb, x], [g1], [(D_MODEL, F32)], [(8, D_MODEL)], tr=TR, name="bwd_in")

    small = jnp.concatenate([dg1, dg2, dgq, dgkv, dg3, dg4, loss8], axis=1)
    return dx, (gw_main, gw_kr, gw_uq_p, gw_ukv_p, gw_out, gw_up, gw_down), small


def _place():
    x, y, c = lax.axis_index("x"), lax.axis_index("y"), lax.axis_index("c")
    chips = [(1 - x, y), (x, 1 - y), (1 - x, 1 - y)]
    return x, y, c, chips


def _cast_place_body(me_ref, w_ref, o_ref):
    o_ref[...] = w_ref[...].astype(o_ref.dtype)


def _cast_place(me_arr, w, name):
    rows, cols = w.shape
    tr = min(rows, 256)
    grid_spec = pltpu.PrefetchScalarGridSpec(
        num_scalar_prefetch=1, grid=(rows // tr,),
        in_specs=[pl.BlockSpec((tr, cols), lambda i, me: (i, 0))],
        out_specs=pl.BlockSpec((None, tr, cols), lambda i, me: (me[0], i, 0)))
    return _pcall(
        _cast_place_body, name=name, grid_spec=grid_spec,
        out_shape=jax.ShapeDtypeStruct((N_CHIPS, rows, cols), WIRE_DTYPE),
        compiler_params=pltpu.CompilerParams(dimension_semantics=("parallel",)),
    )(me_arr, w)


def _ag_body(*refs, n_w):
    bufs = refs[n_w:2 * n_w]
    send_sems, recv_sems, fsend_sems, frecv_sems = refs[2 * n_w:]
    x, y, c, chips = _place()
    me = 2 * x + y
    sib = (x, y, 1 - c)

    def half_rows(w, which):
        half = bufs[w].shape[1] // 2
        return pl.ds(pl.multiple_of(which * half, 16), half)

    sends, fwds = [], []
    for w in range(n_w):
        mine = bufs[w].at[me, half_rows(w, c)]
        for j, (px, py) in enumerate(chips):
            cp = pltpu.make_async_remote_copy(
                src_ref=mine, dst_ref=mine,
                send_sem=send_sems.at[w * 3 + j], recv_sem=recv_sems.at[w * 3 + j],
                device_id=(px, py, c), device_id_type=MESH)
            cp.start()
            sends.append(cp)
    for w in range(n_w):
        for j, (px, py) in enumerate(chips):
            landed = bufs[w].at[2 * px + py, half_rows(w, c)]
            pltpu.make_async_remote_copy(
                src_ref=landed, dst_ref=landed,
                send_sem=send_sems.at[w * 3 + j], recv_sem=recv_sems.at[w * 3 + j],
                device_id=(px, py, c), device_id_type=MESH).wait_recv()
            fw = pltpu.make_async_remote_copy(
                src_ref=landed, dst_ref=landed,
                send_sem=fsend_sems.at[w * 3 + j], recv_sem=frecv_sems.at[w * 3 + j],
                device_id=sib, device_id_type=MESH)
            fw.start()
            fwds.append(fw)
    for w in range(n_w):
        for j, (px, py) in enumerate(chips):
            passed = bufs[w].at[2 * px + py, half_rows(w, 1 - c)]
            pltpu.make_async_remote_copy(
                src_ref=passed, dst_ref=passed,
                send_sem=fsend_sems.at[w * 3 + j], recv_sem=frecv_sems.at[w * 3 + j],
                device_id=sib, device_id_type=MESH).wait_recv()
    for cp in sends + fwds:
        cp.wait_send()


def _allgather_weights(placed):
    n_w = len(placed)
    return _pcall(
        functools.partial(_ag_body, n_w=n_w), name="weight_allgather",
        in_specs=[ANY] * n_w, out_specs=[ANY] * n_w,
        out_shape=[jax.ShapeDtypeStruct(p.shape, p.dtype) for p in placed],
        input_output_aliases={w: w for w in range(n_w)},
        scratch_shapes=[pltpu.SemaphoreType.DMA((3 * n_w,))] * 4,
    )(*placed)


def _pair_send_body(*refs, n_w):
    ins, outs = refs[:n_w], refs[n_w:2 * n_w]
    send_sems, recv_sems = refs[2 * n_w:]
    x, y, c, _ = _place()
    cps = []
    for w in range(n_w):
        cp = pltpu.make_async_remote_copy(
            src_ref=ins[w].at[:, 1 - c], dst_ref=outs[w],
            send_sem=send_sems.at[w], recv_sem=recv_sems.at[w],
            device_id=(x, y, 1 - c), device_id_type=MESH)
        cp.start()
        cps.append(cp)
    for cp in cps:
        cp.wait()


def _pair_send(grads4):
    n_w = len(grads4)
    return _pcall(
        functools.partial(_pair_send_body, n_w=n_w), name="grad_pair_exchange",
        in_specs=[ANY] * n_w, out_specs=[ANY] * n_w,
        out_shape=[jax.ShapeDtypeStruct((g.shape[0],) + g.shape[2:], g.dtype) for g in grads4],
        scratch_shapes=[pltpu.SemaphoreType.DMA((n_w,))] * 2,
    )(*grads4)


def _pair_add_body(c_ref, mine_ref, theirs_ref, o_ref):
    o_ref[...] = (mine_ref[...].astype(F32) + theirs_ref[...].astype(F32)).astype(o_ref.dtype)


def _pair_add(c_arr, g4, recv, name):
    _, _, hr, cols = g4.shape
    tr = min(hr, 256)
    grid_spec = pltpu.PrefetchScalarGridSpec(
        num_scalar_prefetch=1, grid=(N_CHIPS, hr // tr),
        in_specs=[pl.BlockSpec((None, None, tr, cols), lambda s, i, c: (s, c[0], i, 0)),
                  pl.BlockSpec((None, tr, cols), lambda s, i, c: (s, i, 0))],
        out_specs=pl.BlockSpec((None, tr, cols), lambda s, i, c: (s, i, 0)))
    return _pcall(
        _pair_add_body, name=name, grid_spec=grid_spec,
        out_shape=jax.ShapeDtypeStruct(recv.shape, recv.dtype),
        compiler_params=pltpu.CompilerParams(dimension_semantics=("parallel", "parallel")),
    )(c_arr, g4, recv)


def _scatter_body(*refs, n_w):
    ins, outs = refs[:n_w], refs[n_w:2 * n_w]
    send_sems, recv_sems = refs[2 * n_w:]
    x, y, c, chips = _place()
    me = 2 * x + y
    todo = []
    for w in range(n_w):
        for j, (px, py) in enumerate(chips):
            cp = pltpu.make_async_remote_copy(
                src_ref=ins[w].at[2 * px + py], dst_ref=outs[w].at[me],
                send_sem=send_sems.at[w * 3 + j], recv_sem=recv_sems.at[w * 3 + j],
                device_id=(px, py, c), device_id_type=MESH)
            cp.start()
            todo.append(cp)
    for t in todo:
        t.wait()


def _scatter(parts):
    n_w = len(parts)
    return _pcall(
        functools.partial(_scatter_body, n_w=n_w), name="grad_scatter",
        in_specs=[ANY] * n_w, out_specs=[ANY] * n_w,
        out_shape=[jax.ShapeDtypeStruct(p.shape, p.dtype) for p in parts],
        scratch_shapes=[pltpu.SemaphoreType.DMA((3 * n_w,))] * 2,
    )(*parts)


def _sum4_body(me_ref, p_ref, l0, l1, l2, l3, o_ref):
    me = me_ref[0]
    t = [jnp.where(me == j, p_ref[...], l[...]).astype(F32) for j, l in enumerate((l0, l1, l2, l3))]
    o_ref[...] = ((t[0] + t[1]) + t[2]) + t[3]


def _sum4(me_arr, part, landed, name):
    _, hr, cols = part.shape
    tr = min(hr, 256)

    def slot(j):
        return lambda i, me: (jnp.where(me[0] == j, (j + 1) % N_CHIPS, j), i, 0)

    grid_spec = pltpu.PrefetchScalarGridSpec(
        num_scalar_prefetch=1, grid=(hr // tr,),
        in_specs=[pl.BlockSpec((None, tr, cols), lambda i, me: (me[0], i, 0))]
        + [pl.BlockSpec((None, tr, cols), slot(j)) for j in range(N_CHIPS)],
        out_specs=pl.BlockSpec((tr, cols), lambda i, me: (i, 0)))
    return _pcall(
        _sum4_body, name=name, grid_spec=grid_spec,
        out_shape=jax.ShapeDtypeStruct((hr, cols), F32),
        compiler_params=pltpu.CompilerParams(dimension_semantics=("parallel",)),
    )(me_arr, part, landed, landed, landed, landed)


def _pair_swap_body(*refs, n_w):
    ins, outs = refs[:n_w], refs[n_w:2 * n_w]
    send_sems, recv_sems = refs[2 * n_w:]
    x, y, c, _ = _place()
    todo = []
    for w in range(n_w):
        cp = pltpu.make_async_remote_copy(
            src_ref=ins[w], dst_ref=outs[w],
            send_sem=send_sems.at[w], recv_sem=recv_sems.at[w],
            device_id=(x, y, 1 - c), device_id_type=MESH)
        cp.start()
        todo.append(cp)
    for t in todo:
        t.wait()


def _pair_swap(halves):
    n_w = len(halves)
    return _pcall(
        functools.partial(_pair_swap_body, n_w=n_w), name="grad_pair_swap",
        in_specs=[ANY] * n_w, out_specs=[ANY] * n_w,
        out_shape=[jax.ShapeDtypeStruct(h.shape, h.dtype) for h in halves],
        scratch_shapes=[pltpu.SemaphoreType.DMA((n_w,))] * 2,
    )(*halves)


def _small_gather_body(x_ref, out_ref, send_sems, recv_sems, local_sem):
    m_per = x_ref.shape[0]
    x, y, c, chips = _place()
    me, sibling = (x, y, c), (x, y, 1 - c)

    def rows(px, py, pc):
        return out_ref.at[pl.ds((4 * px + 2 * py + pc) * m_per, m_per), :]

    def copy(k, block, to, src=None):
        return pltpu.make_async_remote_copy(
            src_ref=rows(*block) if src is None else src, dst_ref=rows(*block),
            send_sem=send_sems.at[k], recv_sem=recv_sems.at[k], device_id=to, device_id_type=MESH)

    mine = pltpu.make_async_copy(x_ref, rows(*me), local_sem)
    mine.start()
    first = [copy(0, me, sibling, src=x_ref)]
    first += [copy(1 + j, me, (*chip, c), src=x_ref) for j, chip in enumerate(chips)]
    for cp in first:
        cp.start()
    passed = [copy(4 + j, (*chip, c), sibling) for j, chip in enumerate(chips)]
    for j, chip in enumerate(chips):
        copy(1 + j, (*chip, c), me).wait_recv()
        passed[j].start()
    copy(0, sibling, me).wait_recv()
    for j, chip in enumerate(chips):
        copy(4 + j, (*chip, 1 - c), me).wait_recv()
    for cp in first + passed:
        cp.wait_send()
    mine.wait()


def _small_gather(small):
    m_per, n = small.shape
    return _pcall(
        _small_gather_body, name="small_allgather",
        out_shape=jax.ShapeDtypeStruct((N_DEV * m_per, n), small.dtype),
        in_specs=[pl.BlockSpec(memory_space=pltpu.VMEM)],
        out_specs=pl.BlockSpec(memory_space=pltpu.VMEM),
        scratch_shapes=[pltpu.SemaphoreType.DMA((7,)), pltpu.SemaphoreType.DMA((7,)), pltpu.SemaphoreType.DMA],
    )(small)


def _adamw(w, g, m, v):
    m = ADAM_B1 * m + (1.0 - ADAM_B1) * g
    v = ADAM_B2 * v + (1.0 - ADAM_B2) * (g * g)
    m_hat = m / (1.0 - ADAM_B1 ** ADAM_STEP)
    v_hat = v / (1.0 - ADAM_B2 ** ADAM_STEP)
    delta = -ADAM_LR * (m_hat / (jnp.sqrt(v_hat) + ADAM_EPS) + ADAM_WD * w)
    return delta, m, v


def _adamw_body(c_ref, w_ref, own_ref, sib_ref, m_ref, v_ref, g_ref, d_ref, nm_ref, nv_ref, *, nh):
    mine = (pl.program_id(0) // nh) == c_ref[0]
    g = jnp.where(mine, own_ref[...], sib_ref[...])
    g_ref[...] = g
    d, m, v = _adamw(w_ref[...], g, m_ref[...], v_ref[...])
    d_ref[...] = d
    nm_ref[...] = m
    nv_ref[...] = v


def _adamw_call(c_arr, w, own, sib, m, v, name):
    rows, cols = w.shape
    tr = min(rows // 2, 256)
    nh = (rows // 2) // tr
    full = pl.BlockSpec((tr, cols), lambda i, c: (i, 0))
    half = pl.BlockSpec((tr, cols), lambda i, c: (i % nh, 0))
    grid_spec = pltpu.PrefetchScalarGridSpec(
        num_scalar_prefetch=1, grid=(rows // tr,),
        in_specs=[full, half, half, full, full], out_specs=[full] * 4)
    return _pcall(
        functools.partial(_adamw_body, nh=nh), name=name, grid_spec=grid_spec,
        out_shape=[jax.ShapeDtypeStruct(w.shape, F32)] * 4,
        compiler_params=pltpu.CompilerParams(dimension_semantics=("parallel",)),
    )(c_arr, w, own, sib, m, v)


def _small_update_body(gath_ref, w_ref, m_ref, v_ref, g_ref, d_ref, nm_ref, nv_ref, loss_ref, *, n_gain):
    tot = gath_ref[0:1, :]
    for i in range(1, gath_ref.shape[0]):
        tot = tot + gath_ref[i:i + 1, :]
    g = tot[:, 0:n_gain]
    g_ref[...] = g
    d, m, v = _adamw(w_ref[...], g, m_ref[...], v_ref[...])
    d_ref[...] = d
    nm_ref[...] = m
    nv_ref[...] = v
    loss_ref[...] = (0.5 / D_MODEL) * jnp.sum(tot[:, n_gain:n_gain + HEAD], axis=1, keepdims=True) * jnp.ones((1, HEAD), F32)


def _small_update(gath, w, m, v):
    n_gain = w.shape[1]
    vm = pl.BlockSpec(memory_space=pltpu.VMEM)
    return _pcall(
        functools.partial(_small_update_body, n_gain=n_gain), name="gain_update",
        in_specs=[vm] * 4, out_specs=[vm] * 5,
        out_shape=[jax.ShapeDtypeStruct((1, n_gain), F32)] * 4 + [jax.ShapeDtypeStruct((1, HEAD), F32)],
    )(gath, w, m, v)


def kernel(x, positions, norm_attn_pre, norm_attn_post, w_in, q_latent_norm, kv_latent_norm, w_uq, w_ukv, w_out, norm_mlp_pre, norm_mlp_post, w_up, w_down, loss_target, m_norm_attn_pre, m_norm_attn_post, m_w_in, m_q_latent_norm, m_kv_latent_norm, m_w_uq, m_w_ukv, m_w_out, m_norm_mlp_pre, m_norm_mlp_post, m_w_up, m_w_down, v_norm_attn_pre, v_norm_attn_post, v_w_in, v_q_latent_norm, v_kv_latent_norm, v_w_uq, v_w_ukv, v_w_out, v_norm_mlp_pre, v_norm_mlp_post, v_w_up, v_w_down):
    T = x.shape[1]
    c_arr = lax.axis_index("c").astype(jnp.int32).reshape(1)
    me_arr = (2 * lax.axis_index("x") + lax.axis_index("y")).astype(jnp.int32).reshape(1)
    names = ["w_in", "w_uq", "w_ukv", "w_out", "w_up", "w_down"]

    mats = [w_in[0], w_uq[0], w_ukv[0], w_out[0], w_up[0], w_down[0]]
    placed = [_cast_place(me_arr, w, "cast_" + n) for w, n in zip(mats, names)]
    win_g, wuq_g, wukv_g, wout_g, wup_g, wdown_g = _allgather_weights(placed)

    col_major = lambda g: jnp.transpose(g, (1, 0, 2)).reshape(g.shape[1], N_CHIPS * g.shape[2])
    win_full = col_major(win_g)
    w_main = win_full[:, :MAIN_COLS]
    w_kr = jnp.pad(win_full[:, MAIN_COLS:], ((0, 0), (0, HEAD - ROPE_B)))
    wuq_full = col_major(wuq_g).reshape(LORA, NH, HEAD + ROPE_B)
    w_uq_p = jnp.pad(wuq_full, ((0, 0), (0, 0), (0, QPAD - HEAD - ROPE_B))).reshape(LORA, NH * QPAD)
    w_ukv_p = col_major(wukv_g).reshape(LORA, NH, 2, HEAD).transpose(0, 2, 1, 3).reshape(LORA, 2 * A_W)
    w_out_f = wout_g.reshape(2 * A_W, D_MODEL)
    w_up_f = col_major(wup_g)
    w_down_f = wdown_g.reshape(D_FF, D_MODEL)
    cast = lambda a: a.astype(MXU_DTYPE)

    dx, gws, small = _local_step(
        x[0], positions[0].astype(F32).reshape(T, 1), loss_target[0],
        norm_attn_pre, norm_attn_post, q_latent_norm, kv_latent_norm, norm_mlp_pre, norm_mlp_post,
        cast(w_main), cast(w_kr), cast(w_uq_p), cast(w_ukv_p), cast(w_out_f), cast(w_up_f), cast(w_down_f))
    gw_main, gw_kr, gw_uq_p, gw_ukv_p, gw_out, gw_up, gw_down = gws

    to_shards = lambda g: jnp.transpose(g.reshape(g.shape[0], N_CHIPS, g.shape[1] // N_CHIPS), (1, 0, 2))
    gw_in = to_shards(jnp.concatenate([gw_main, gw_kr[:, :ROPE_B]], axis=1))
    gw_uq = to_shards(gw_uq_p.reshape(LORA, NH, QPAD)[:, :, :HEAD + ROPE_B].reshape(LORA, NH * (HEAD + ROPE_B)))
    gw_ukv = to_shards(gw_ukv_p.reshape(LORA, 2, NH, HEAD).transpose(0, 2, 1, 3).reshape(LORA, 2 * A_W))
    full = [gw_in, gw_uq, gw_ukv, gw_out.reshape(N_CHIPS, LORA, D_MODEL), to_shards(gw_up),
            gw_down.reshape(N_CHIPS, D_MODEL, D_MODEL)]
    full4 = [g.reshape(N_CHIPS, 2, g.shape[1] // 2, g.shape[2]) for g in full]

    from_sib = _pair_send(full4)
    parts = [_pair_add(c_arr, g4, r, "pair_add_" + n) for g4, r, n in zip(full4, from_sib, names)]
    landed = _scatter(parts)
    halves = [_sum4(me_arr, p, l, "chip_sum_" + n) for p, l, n in zip(parts, landed, names)]
    from_sib2 = _pair_swap(halves)

    ms = [m_w_in[0], m_w_uq[0], m_w_ukv[0], m_w_out[0], m_w_up[0], m_w_down[0]]
    vs = [v_w_in[0], v_w_uq[0], v_w_ukv[0], v_w_out[0], v_w_up[0], v_w_down[0]]
    upd = [_adamw_call(c_arr, w, own, sib, m, v, "adamw_" + n)
           for w, own, sib, m, v, n in zip(mats, halves, from_sib2, ms, vs, names)]
    grads = [u[0] for u in upd]

    gath = _small_gather(small)
    gains = [norm_attn_pre, norm_attn_post, q_latent_norm, kv_latent_norm, norm_mlp_pre, norm_mlp_post]
    gm = [m_norm_attn_pre, m_norm_attn_post, m_q_latent_norm, m_kv_latent_norm, m_norm_mlp_pre, m_norm_mlp_post]
    gv = [v_norm_attn_pre, v_norm_attn_post, v_q_latent_norm, v_kv_latent_norm, v_norm_mlp_pre, v_norm_mlp_post]
    cat = lambda xs: jnp.concatenate(xs, axis=1)
    g_s, d_s, m_s, v_s, loss_v = _small_update(gath, cat(gains), cat(gm), cat(gv))
    widths = [a.shape[1] for a in gains]
    offs = [sum(widths[:i]) for i in range(len(widths))]
    split = lambda a: [a[:, o:o + w] for o, w in zip(offs, widths)]
    g_gain, d_gain, m_gain, v_gain = split(g_s), split(d_s), split(m_s), split(v_s)

    def ordered(gain_list, mat_list):
        gl, ml = gain_list, [a[None] for a in mat_list]
        return [gl[0], gl[1], ml[0], gl[2], gl[3], ml[1], ml[2], ml[3], gl[4], gl[5], ml[4], ml[5]]

    loss = loss_v[0, 0]
    return (loss, dx[None],
            *ordered(g_gain, grads),
            *ordered(d_gain, [u[1] for u in upd]),
            *ordered(m_gain, [u[2] for u in upd]),
            *ordered(v_gain, [u[3] for u in upd]))
```

```python
import functools

import jax
import jax.numpy as jnp
from jax import lax
from jax.experimental import pallas as pl
from jax.experimental.pallas import tpu as pltpu

F32 = jnp.float32
BF16 = jnp.bfloat16
MXU_DTYPE = jnp.bfloat16
WIRE_DTYPE = jnp.bfloat16

D_MODEL = 2048
HEAD = 128
NH = 8
A_W = NH * HEAD
LORA = 512
ROPE_B = 64
QPAD = 256
MAIN_COLS = 3 * A_W + 2 * LORA
IN_COLS = MAIN_COLS + ROPE_B
D_FF = 4 * D_MODEL
DIL = (1, 4, 16)
ROT_A = 32
ROPE_THETA = 500000.0
EPS = 1e-6
NEG = -1e30
N_CHIPS = 4
N_DEV = 8

ADAM_LR = 0.001
ADAM_B1 = 0.9
ADAM_B2 = 0.999
ADAM_EPS = 1e-08
ADAM_WD = 0.01
ADAM_STEP = 10

MESH = pl.DeviceIdType.MESH
ANY = pl.BlockSpec(memory_space=pl.ANY)


def _pcall(body, **kw):
    return pl.pallas_call(body, **kw)


_DIMS = {
    "nn": (((1,), (0,)), ((), ())),
    "nt": (((1,), (1,)), ((), ())),
    "tn": (((0,), (0,)), ((), ())),
}


def _mm_body(*refs, dims, nk, epi, n_extra, n_out):
    a_ref, b_ref = refs[0], refs[1]
    extra = refs[2:2 + n_extra]
    outs = refs[2 + n_extra:2 + n_extra + n_out]
    part = lax.dot_general(a_ref[...], b_ref[...], _DIMS[dims], preferred_element_type=F32)

    def finish(acc):
        res = epi(acc, *[r[...] for r in extra]) if epi is not None else (acc,)
        for o_ref, o in zip(outs, res):
            o_ref[...] = o.astype(o_ref.dtype)

    if nk == 1:
        finish(part)
        return
    acc_ref = refs[-1]
    k = pl.program_id(2)

    @pl.when(k == 0)
    def _():
        acc_ref[...] = part

    @pl.when(k > 0)
    def _():
        acc_ref[...] += part

    @pl.when(k == nk - 1)
    def _():
        finish(acc_ref[...])


def _matmul(a, b, *, dims, out_dtypes, tm, tn, tk, name, epi=None, extras=()):
    if dims == "nn":
        (M, K), (K2, N) = a.shape, b.shape
    elif dims == "nt":
        (M, K), (N, K2) = a.shape, b.shape
    else:
        (K, M), (K2, N) = a.shape, b.shape
    assert K == K2, (a.shape, b.shape, dims)
    tm, tn, tk = min(tm, M), min(tn, N), min(tk, K)
    assert M % tm == 0 and N % tn == 0 and K % tk == 0, (name, M, N, K, tm, tn, tk)
    nk = K // tk
    a_spec = {"nn": pl.BlockSpec((tm, tk), lambda i, j, k: (i, k)),
              "nt": pl.BlockSpec((tm, tk), lambda i, j, k: (i, k)),
              "tn": pl.BlockSpec((tk, tm), lambda i, j, k: (k, i))}[dims]
    b_spec = {"nn": pl.BlockSpec((tk, tn), lambda i, j, k: (k, j)),
              "nt": pl.BlockSpec((tn, tk), lambda i, j, k: (j, k)),
              "tn": pl.BlockSpec((tk, tn), lambda i, j, k: (k, j))}[dims]
    o_spec = pl.BlockSpec((tm, tn), lambda i, j, k: (i, j))
    body = functools.partial(_mm_body, dims=dims, nk=nk, epi=epi,
                             n_extra=len(extras), n_out=len(out_dtypes))
    res = _pcall(
        body, name=name,
        grid=(M // tm, N // tn, nk),
        in_specs=[a_spec, b_spec] + [o_spec] * len(extras),
        out_specs=[o_spec] * len(out_dtypes),
        out_shape=[jax.ShapeDtypeStruct((M, N), dt) for dt in out_dtypes],
        scratch_shapes=[pltpu.VMEM((tm, tn), F32)] if nk > 1 else [],
        compiler_params=pltpu.CompilerParams(
            dimension_semantics=("parallel", "parallel", "arbitrary")),
    )(a, b, *extras)
    return list(res)


def _rowwise(body, row_ins, vec_ins, row_outs, acc_outs, *, tr, name):
    T = row_ins[0].shape[0]
    tr = min(tr, T)
    assert T % tr == 0
    in_specs = [pl.BlockSpec((tr, a.shape[1]), lambda i: (i, 0)) for a in row_ins]
    in_specs += [pl.BlockSpec(a.shape, lambda i: (0, 0)) for a in vec_ins]
    out_specs = [pl.BlockSpec((tr, w), lambda i: (i, 0)) for (w, _) in row_outs]
    out_specs += [pl.BlockSpec(s, lambda i: (0, 0)) for s in acc_outs]
    out_shape = [jax.ShapeDtypeStruct((T, w), dt) for (w, dt) in row_outs]
    out_shape += [jax.ShapeDtypeStruct(s, F32) for s in acc_outs]
    sem = "arbitrary" if acc_outs else "parallel"
    return list(_pcall(
        body, name=name, grid=(T // tr,), in_specs=in_specs, out_specs=out_specs,
        out_shape=out_shape,
        compiler_params=pltpu.CompilerParams(dimension_semantics=(sem,)),
    )(*row_ins, *vec_ins))


def _rstd(x):
    return lax.rsqrt(jnp.mean(x * x, axis=-1, keepdims=True) + EPS)


def _rms_bwd(x, rstd, dyg):
    xh = x * rstd
    return rstd * (dyg - xh * jnp.mean(dyg * xh, axis=-1, keepdims=True)), xh


def _fold8(v):
    r, w = v.shape
    return jnp.sum(v.reshape(r // 8, 8, w), axis=0)


def _acc(ref, val):
    first = pl.program_id(0) == 0

    @pl.when(first)
    def _():
        ref[...] = val

    @pl.when(jnp.logical_not(first))
    def _():
        ref[...] += val


def _rope(x, c, sa, sb, half):
    return x * c + pltpu.roll(x, HEAD - half, 1) * sa + pltpu.roll(x, half, 1) * sb


def _rope_t(dy, c, sa, sb, half):
    return dy * c - pltpu.roll(dy, HEAD - half, 1) * sa - pltpu.roll(dy, half, 1) * sb


def _rope_tab_body(pos_ref, inv_ref, ca, saa, sab, cb, sba, sbb):
    pos = pos_ref[...]
    lane = lax.broadcasted_iota(jnp.int32, (pos.shape[0], HEAD), 1)
    ang_a = pos * inv_ref[0:1, :]
    ang_b = pos * inv_ref[1:2, :]
    c, s = jnp.cos(ang_a), jnp.sin(ang_a)
    ha = ROT_A // 2
    ca[...] = jnp.where(lane < ROT_A, c, 1.0)
    saa[...] = jnp.where(lane < ha, -s, 0.0)
    sab[...] = jnp.where((lane >= ha) & (lane < ROT_A), s, 0.0)
    c, s = jnp.cos(ang_b), jnp.sin(ang_b)
    hb = ROPE_B // 2
    cb[...] = jnp.where(lane < ROPE_B, c, 1.0)
    sba[...] = jnp.where(lane < hb, -s, 0.0)
    sbb[...] = jnp.where((lane >= hb) & (lane < ROPE_B), s, 0.0)


def _rms_fwd_body(x_ref, g_ref, h_ref):
    x = x_ref[...]
    h_ref[...] = ((x * _rstd(x)) * g_ref[...]).astype(h_ref.dtype)


def _postproj_body(p_ref, kr_ref, ca, saa, sab, cb, sba, sbb, gq_ref, gkv_ref,
                   q_ref, k_ref, v_ref, cqn_ref, ckvn_ref, krope_ref):
    c, sa, sb = ca[...], saa[...], sab[...]
    for h in range(NH):
        lo = h * HEAD
        q_ref[:, lo:lo + HEAD] = _rope(p_ref[:, lo:lo + HEAD], c, sa, sb, ROT_A // 2).astype(q_ref.dtype)
        k_ref[:, lo:lo + HEAD] = _rope(p_ref[:, A_W + lo:A_W + lo + HEAD], c, sa, sb, ROT_A // 2).astype(k_ref.dtype)
    v_ref[...] = p_ref[:, 2 * A_W:3 * A_W].astype(v_ref.dtype)
    cq = p_ref[:, 3 * A_W:3 * A_W + LORA]
    cqn_ref[...] = ((cq * _rstd(cq)) * gq_ref[...]).astype(cqn_ref.dtype)
    ckv = p_ref[:, 3 * A_W + LORA:MAIN_COLS]
    ckvn_ref[...] = ((ckv * _rstd(ckv)) * gkv_ref[...]).astype(ckvn_ref.dtype)
    krope_ref[...] = _rope(kr_ref[...], cb[...], sba[...], sbb[...], ROPE_B // 2).astype(krope_ref.dtype)


def _qrope_body(qp_ref, cb, sba, sbb, q_ref):
    c, sa, sb = cb[...], sba[...], sbb[...]
    for h in range(NH):
        lo = h * QPAD
        q_ref[:, lo:lo + HEAD] = qp_ref[:, lo:lo + HEAD].astype(q_ref.dtype)
        q_ref[:, lo + HEAD:lo + QPAD] = _rope(qp_ref[:, lo + HEAD:lo + QPAD], c, sa, sb, ROPE_B // 2).astype(q_ref.dtype)


def _qrope_t_body(dq_ref, cb, sba, sbb, o_ref):
    c, sa, sb = cb[...], sba[...], sbb[...]
    for h in range(NH):
        lo = h * QPAD
        o_ref[:, lo:lo + HEAD] = dq_ref[:, lo:lo + HEAD].astype(o_ref.dtype)
        o_ref[:, lo + HEAD:lo + QPAD] = _rope_t(dq_ref[:, lo + HEAD:lo + QPAD], c, sa, sb, ROPE_B // 2).astype(o_ref.dtype)


def _mid_body(x_ref, o_ref, g2_ref, g3_ref, x1_ref, h2_ref):
    o = o_ref[...]
    x1 = x_ref[...] + (o * _rstd(o)) * g2_ref[...]
    x1_ref[...] = x1
    h2_ref[...] = ((x1 * _rstd(x1)) * g3_ref[...]).astype(h2_ref.dtype)


def _loss_body(x1_ref, d_ref, t_ref, g4_ref, dy_ref, dd_ref, loss_ref, dg4_ref):
    d = d_ref[...]
    rstd = _rstd(d)
    y = x1_ref[...] + (d * rstd) * g4_ref[...]
    e = y - t_ref[...]
    dy = e * (1.0 / D_MODEL)
    dy_ref[...] = dy
    dd, dh = _rms_bwd(d, rstd, dy * g4_ref[...])
    dd_ref[...] = dd.astype(dd_ref.dtype)
    _acc(dg4_ref, _fold8(dy * dh))
    e8 = _fold8(e * e)
    l = e8[:, 0:HEAD]
    for j in range(1, D_MODEL // HEAD):
        l = l + e8[:, j * HEAD:(j + 1) * HEAD]
    _acc(loss_ref, l)


def _bmid_body(dy_ref, dh2_ref, x1_ref, o_ref, g2_ref, g3_ref, dx1_ref, do_ref, dg3_ref, dg2_ref):
    x1 = x1_ref[...]
    dh2 = dh2_ref[...]
    dn, x1h = _rms_bwd(x1, _rstd(x1), dh2 * g3_ref[...])
    dx1 = dy_ref[...] + dn
    dx1_ref[...] = dx1
    _acc(dg3_ref, _fold8(dh2 * x1h))
    o = o_ref[...]
    do, oh = _rms_bwd(o, _rstd(o), dx1 * g2_ref[...])
    do_ref[...] = do.astype(do_ref.dtype)
    _acc(dg2_ref, _fold8(dx1 * oh))


def _dproj_body(dq_ref, dk_ref, dv_ref, dcq_ref, dckv_ref, p_ref, dkr_ref,
                ca, saa, sab, cb, sba, sbb, gq_ref, gkv_ref,
                dp_ref, dkrp_ref, dgq_ref, dgkv_ref):
    c, sa, sb = ca[...], saa[...], sab[...]
    for h in range(NH):
        lo = h * HEAD
        dp_ref[:, lo:lo + HEAD] = _rope_t(dq_ref[:, lo:lo + HEAD], c, sa, sb, ROT_A // 2).astype(dp_ref.dtype)
        dp_ref[:, A_W + lo:A_W + lo + HEAD] = _rope_t(dk_ref[:, lo:lo + HEAD], c, sa, sb, ROT_A // 2).astype(dp_ref.dtype)
    dp_ref[:, 2 * A_W:3 * A_W] = dv_ref[...].astype(dp_ref.dtype)
    cq = p_ref[:, 3 * A_W:3 * A_W + LORA]
    dcqn = dcq_ref[...]
    dcq, cqh = _rms_bwd(cq, _rstd(cq), dcqn * gq_ref[...])
    dp_ref[:, 3 * A_W:3 * A_W + LORA] = dcq.astype(dp_ref.dtype)
    _acc(dgq_ref, _fold8(dcqn * cqh))
    ckv = p_ref[:, 3 * A_W + LORA:MAIN_COLS]
    dckvn = dckv_ref[...]
    dckv, ckvh = _rms_bwd(ckv, _rstd(ckv), dckvn * gkv_ref[...])
    dp_ref[:, 3 * A_W + LORA:MAIN_COLS] = dckv.astype(dp_ref.dtype)
    _acc(dgkv_ref, _fold8(dckvn * ckvh))
    dkr = dkr_ref[:, 0:HEAD]
    for h in range(1, NH):
        dkr = dkr + dkr_ref[:, h * HEAD:(h + 1) * HEAD]
    dkrp_ref[...] = _rope_t(dkr, cb[...], sba[...], sbb[...], ROPE_B // 2).astype(dkrp_ref.dtype)


def _bin_body(dx1_ref, dha_ref, dhb_ref, x_ref, g1_ref, dx_ref, dg1_ref):
    x = x_ref[...]
    dh = dha_ref[...] + dhb_ref[...]
    dn, xh = _rms_bwd(x, _rstd(x), dh * g1_ref[...])
    dx_ref[...] = dx1_ref[...] + dn
    _acc(dg1_ref, _fold8(dh * xh))


def _dot_nt(a, b):
    return lax.dot_general(a, b, _DIMS["nt"], preferred_element_type=F32)


def _dot_tn(a, b):
    return lax.dot_general(a, b, _DIMS["tn"], preferred_element_type=F32)


def _dot_nn(a, b):
    return jnp.dot(a, b, preferred_element_type=F32)


DIL_SCALE = HEAD ** -0.5
DIL_CHUNK = 256


def _dil_rows(t, d):
    r = t & (d - 1)
    n = t >> (d.bit_length() - 1)
    start = r + n * (HEAD * d)
    has_prev = n > 0
    pstart = jnp.where(has_prev, start - HEAD * d, start)
    if d == 1:
        return pl.ds(pl.multiple_of(start, HEAD), HEAD), pl.ds(pl.multiple_of(pstart, HEAD), HEAD), has_prev
    return pl.ds(start, HEAD, stride=d), pl.ds(pstart, HEAD, stride=d), has_prev


def _dil_band():
    row = lax.broadcasted_iota(jnp.int32, (HEAD, 2 * HEAD), 0)
    col = lax.broadcasted_iota(jnp.int32, (HEAD, 2 * HEAD), 1)
    return (col >= row) & (col <= row + HEAD), col >= HEAD


def _dil_fwd_body(q_ref, k_ref, v_ref, a_ref, lse_ref, o1, o2, o3, l1, l2, l3, *, nt, unroll):
    band, is_cur = _dil_band()
    for d, o_sc, l_sc in zip(DIL, (o1, o2, o3), (l1, l2, l3)):

        def tile(t, carry, d=d, o_sc=o_sc, l_sc=l_sc):
            rows, prows, has_prev = _dil_rows(t, d)
            q = q_ref[rows, :].astype(MXU_DTYPE)
            kk = jnp.concatenate([k_ref[prows, :], k_ref[rows, :]], axis=0).astype(MXU_DTYPE)
            vv = jnp.concatenate([v_ref[prows, :], v_ref[rows, :]], axis=0).astype(MXU_DTYPE)
            ok = band & (is_cur | has_prev)
            s = jnp.where(ok, _dot_nt(q, kk) * DIL_SCALE, NEG)
            m = jnp.max(s, axis=1, keepdims=True)
            p = jnp.exp(s - m)
            den = jnp.sum(p, axis=1, keepdims=True)
            o_sc[rows, :] = _dot_nn((p / den).astype(MXU_DTYPE), vv)
            l_sc[rows, :] = jnp.broadcast_to(m + jnp.log(den), (HEAD, HEAD))
            return carry

        lax.fori_loop(0, nt, tile, 0, unroll=unroll)

    def merge(i, carry):
        rs = pl.ds(pl.multiple_of(i * DIL_CHUNK, DIL_CHUNK), DIL_CHUNK)
        la, lb, lc = l1[rs, :], l2[rs, :], l3[rs, :]
        m = jnp.maximum(jnp.maximum(la, lb), lc)
        wa, wb, wc = jnp.exp(la - m), jnp.exp(lb - m), jnp.exp(lc - m)
        den = wa + wb + wc
        a = (wa / den) * o1[rs, :] + (wb / den) * o2[rs, :] + (wc / den) * o3[rs, :]
        a_ref[rs, :] = a.astype(a_ref.dtype)
        lse_ref[rs, :] = m + jnp.log(den)
        return carry

    lax.fori_loop(0, q_ref.shape[0] // DIL_CHUNK, merge, 0)


def _dil_fwd(q, k, v):
    T = q.shape[0]
    spec = pl.BlockSpec((T, HEAD), lambda h: (0, h))
    return _pcall(
        functools.partial(_dil_fwd_body, nt=T // HEAD, unroll=4), name="dil_fwd",
        grid=(NH,), in_specs=[spec] * 3, out_specs=[spec] * 2,
        out_shape=[jax.ShapeDtypeStruct((T, A_W), MXU_DTYPE), jax.ShapeDtypeStruct((T, A_W), F32)],
        scratch_shapes=[pltpu.VMEM((T, HEAD), F32)] * 6,
        compiler_params=pltpu.CompilerParams(dimension_semantics=("parallel",)),
    )(q, k, v)


def _dil_bwd_body(q_ref, k_ref, v_ref, do_ref, a_ref, lse_ref, dq_ref, dk_ref, dv_ref, dl_sc, *, nt, unroll):
    band, is_cur = _dil_band()

    def prep(i, carry):
        rs = pl.ds(pl.multiple_of(i * DIL_CHUNK, DIL_CHUNK), DIL_CHUNK)
        dl = jnp.sum(do_ref[rs, :] * a_ref[rs, :].astype(F32), axis=1, keepdims=True)
        dl_sc[rs, :] = jnp.broadcast_to(dl, (DIL_CHUNK, HEAD))
        zero = jnp.zeros((DIL_CHUNK, HEAD), F32)
        dq_ref[rs, :] = zero
        dk_ref[rs, :] = zero
        dv_ref[rs, :] = zero
        return carry

    lax.fori_loop(0, q_ref.shape[0] // DIL_CHUNK, prep, 0)

    for d in DIL:

        def tile(t, carry, d=d):
            rows, prows, has_prev = _dil_rows(t, d)
            q = q_ref[rows, :].astype(MXU_DTYPE)
            kk = jnp.concatenate([k_ref[prows, :], k_ref[rows, :]], axis=0).astype(MXU_DTYPE)
            vv = jnp.concatenate([v_ref[prows, :], v_ref[rows, :]], axis=0).astype(MXU_DTYPE)
            do = do_ref[rows, :].astype(MXU_DTYPE)
            lse = lse_ref[rows, :]
            dl = dl_sc[rows, :]
            ok = band & (is_cur | has_prev)
            s = _dot_nt(q, kk) * DIL_SCALE
            p = jnp.where(ok, jnp.exp(s - jnp.concatenate([lse, lse], axis=1)), 0.0)
            ds = (p * (_dot_nt(do, vv) - jnp.concatenate([dl, dl], axis=1))).astype(MXU_DTYPE)
            dq_ref[rows, :] += _dot_nn(ds, kk) * DIL_SCALE
            dkk = _dot_tn(ds, q) * DIL_SCALE
            dvv = _dot_tn(p.astype(MXU_DTYPE), do)
            dk_ref[rows, :] += dkk[HEAD:, :]
            dv_ref[rows, :] += dvv[HEAD:, :]
            dk_ref[prows, :] += dkk[:HEAD, :]
            dv_ref[prows, :] += dvv[:HEAD, :]
            return carry

        lax.fori_loop(0, nt, tile, 0, unroll=unroll)


def _dil_bwd(q, k, v, dmix, mixed, lse):
    T = q.shape[0]
    spec = pl.BlockSpec((T, HEAD), lambda h: (0, h))
    return _pcall(
        functools.partial(_dil_bwd_body, nt=T // HEAD, unroll=2), name="dil_bwd",
        grid=(NH,), in_specs=[spec] * 6, out_specs=[spec] * 3,
        out_shape=[jax.ShapeDtypeStruct((T, A_W), F32)] * 3,
        scratch_shapes=[pltpu.VMEM((T, HEAD), F32)],
        compiler_params=pltpu.CompilerParams(dimension_semantics=("parallel",)),
    )(q, k, v, dmix, mixed, lse)


MLA_SCALE = (HEAD + ROPE_B) ** -0.5
MLA_T = 512
MLA_HP = 2


def _tri(t):
    row = lax.broadcasted_iota(jnp.int32, (t, t), 0)
    col = lax.broadcasted_iota(jnp.int32, (t, t), 1)
    return col <= row


def _lanes(x, n):
    return jnp.tile(x, (1, n // HEAD))


def _mla_fwd_body(q_ref, kn_ref, kr_ref, v_ref, o_ref, lse_ref, m_sc, l_sc, acc_sc, *, t, hp):
    qi = pl.program_id(1)
    m_sc[...] = jnp.full(m_sc.shape, NEG, F32)
    l_sc[...] = jnp.zeros(l_sc.shape, F32)
    acc_sc[...] = jnp.zeros(acc_sc.shape, F32)

    def step(j, masked):
        ks = pl.ds(pl.multiple_of(j * t, t), t)
        kr = kr_ref[ks, :]
        for hh in range(hp):
            kcat = jnp.concatenate([kn_ref[ks, hh * HEAD:(hh + 1) * HEAD], kr], axis=1)
            s = _dot_nt(q_ref[:, hh * QPAD:(hh + 1) * QPAD], kcat) * MLA_SCALE
            if masked:
                s = jnp.where(_tri(t), s, NEG)
            m_prev = m_sc[hh]
            m_new = jnp.maximum(m_prev, jnp.max(s, axis=1, keepdims=True))
            alpha = jnp.exp(m_prev - m_new)
            p = jnp.exp(s - _lanes(m_new, t))
            l_sc[hh] = alpha * l_sc[hh] + jnp.sum(p, axis=1, keepdims=True)
            acc_sc[hh] = alpha * acc_sc[hh] + _dot_nn(p.astype(MXU_DTYPE), v_ref[ks, hh * HEAD:(hh + 1) * HEAD])
            m_sc[hh] = m_new

    def off_diag(j, carry):
        step(j, False)
        return carry

    lax.fori_loop(0, qi, off_diag, 0)
    step(qi, True)
    for hh in range(hp):
        l = l_sc[hh]
        o_ref[:, hh * HEAD:(hh + 1) * HEAD] = (acc_sc[hh] / l).astype(o_ref.dtype)
        lse_ref[:, hh * HEAD:(hh + 1) * HEAD] = m_sc[hh] + jnp.log(l)


def _mla_fwd(qf, kv, kr):
    T = qf.shape[0]
    t, hp = min(MLA_T, T), MLA_HP
    ng = NH // hp
    return _pcall(
        functools.partial(_mla_fwd_body, t=t, hp=hp), name="mla_fwd",
        grid=(ng, T // t),
        in_specs=[pl.BlockSpec((t, hp * QPAD), lambda g, i: (i, g)),
                  pl.BlockSpec((T, hp * HEAD), lambda g, i: (0, g)),
                  pl.BlockSpec((T, HEAD), lambda g, i: (0, 0)),
                  pl.BlockSpec((T, hp * HEAD), lambda g, i: (0, ng + g))],
        out_specs=[pl.BlockSpec((t, hp * HEAD), lambda g, i: (i, g))] * 2,
        out_shape=[jax.ShapeDtypeStruct((T, A_W), MXU_DTYPE), jax.ShapeDtypeStruct((T, A_W), F32)],
        scratch_shapes=[pltpu.VMEM((hp, t, HEAD), F32)] * 3,
        compiler_params=pltpu.CompilerParams(dimension_semantics=("parallel", "parallel")),
    )(qf, kv, kr, kv)


def _mla_bwd_body(q_ref, kn_ref, kr_ref, v_ref, do_ref, o_ref, lse_ref, dq_ref, dkn_ref, dv_ref, dkr_ref,
                  dl_sc, dk_sc, dv_sc, *, t):
    ki = pl.program_id(1)
    nq = q_ref.shape[0] // t

    @pl.when(ki == 0)
    def _():
        def prep(i, carry):
            rs = pl.ds(pl.multiple_of(i * t, t), t)
            dl = jnp.sum(do_ref[rs, :] * o_ref[rs, :].astype(F32), axis=1, keepdims=True)
            dl_sc[rs, :] = jnp.broadcast_to(dl, (t, HEAD))
            dq_ref[rs, :] = jnp.zeros((t, QPAD), F32)
            return carry
        lax.fori_loop(0, nq, prep, 0)

    kcat = jnp.concatenate([kn_ref[...], kr_ref[...]], axis=1)
    v = v_ref[...]
    dk_sc[...] = jnp.zeros(dk_sc.shape, F32)
    dv_sc[...] = jnp.zeros(dv_sc.shape, F32)

    def step(i, masked):
        qs = pl.ds(pl.multiple_of(i * t, t), t)
        q = q_ref[qs, :]
        do = do_ref[qs, :].astype(MXU_DTYPE)
        p = jnp.exp(_dot_nt(q, kcat) * MLA_SCALE - _lanes(lse_ref[qs, :], t))
        if masked:
            p = jnp.where(_tri(t), p, 0.0)
        ds = (p * (_dot_nt(do, v) - _lanes(dl_sc[qs, :], t))).astype(MXU_DTYPE)
        dv_sc[...] += _dot_tn(p.astype(MXU_DTYPE), do)
        dk_sc[...] += _dot_tn(ds, q)
        dq_ref[qs, :] += _dot_nn(ds, kcat) * MLA_SCALE

    step(ki, True)

    def off_diag(i, carry):
        step(i, False)
        return carry

    lax.fori_loop(ki + 1, nq, off_diag, 0)
    dk = dk_sc[...] * MLA_SCALE
    dkn_ref[...] = dk[:, 0:HEAD].astype(dkn_ref.dtype)
    dkr_ref[...] = dk[:, HEAD:QPAD]
    dv_ref[...] = dv_sc[...].astype(dv_ref.dtype)


def _mla_bwd(qf, kv, kr, dmix, mixed, lse):
    T = qf.shape[0]
    t = min(MLA_T, T)
    head = lambda h, j: (0, h)
    b_half = lambda h, j: (0, NH + h)
    kblk = pl.BlockSpec((t, HEAD), lambda h, j: (j, h))
    return _pcall(
        functools.partial(_mla_bwd_body, t=t), name="mla_bwd",
        grid=(NH, T // t),
        in_specs=[pl.BlockSpec((T, QPAD), head), kblk,
                  pl.BlockSpec((t, HEAD), lambda h, j: (j, 0)),
                  pl.BlockSpec((t, HEAD), lambda h, j: (j, NH + h)),
                  pl.BlockSpec((T, HEAD), b_half), pl.BlockSpec((T, HEAD), b_half),
                  pl.BlockSpec((T, HEAD), head)],
        out_specs=[pl.BlockSpec((T, QPAD), head), kblk, kblk, kblk],
        out_shape=[jax.ShapeDtypeStruct((T, NH * QPAD), F32), jax.ShapeDtypeStruct((T, A_W), MXU_DTYPE),
                   jax.ShapeDtypeStruct((T, A_W), MXU_DTYPE), jax.ShapeDtypeStruct((T, A_W), F32)],
        scratch_shapes=[pltpu.VMEM((T, HEAD), F32), pltpu.VMEM((t, QPAD), F32), pltpu.VMEM((t, HEAD), F32)],
        compiler_params=pltpu.CompilerParams(dimension_semantics=("parallel", "arbitrary")),
    )(qf, kv, kr, kv, dmix, mixed, lse)


def _local_step(x, pos, target, g1, g2, gq, gkv, g3, g4, w_main, w_kr, w_uq_p, w_ukv_p, w_out, w_up, w_down):
    T = x.shape[0]
    TR = 256
    mm = functools.partial(_matmul, tm=512, tn=1024, tk=2048)

    inv_a = ROPE_THETA ** (-jnp.arange(0, ROT_A, 2, dtype=F32) / ROT_A)
    inv_b = ROPE_THETA ** (-jnp.arange(0, ROPE_B, 2, dtype=F32) / ROPE_B)
    inv = jnp.stack([jnp.concatenate([inv_a, inv_a, jnp.zeros((HEAD - ROT_A,), F32)]),
                     jnp.concatenate([inv_b, inv_b, jnp.zeros((HEAD - ROPE_B,), F32)])])
    inv = jnp.concatenate([inv, jnp.zeros((6, HEAD), F32)], axis=0)
    tabs = _rowwise(_rope_tab_body, [pos], [inv], [(HEAD, F32)] * 6, [], tr=512, name="rope_tables")

    (h,) = _rowwise(_rms_fwd_body, [x], [g1], [(D_MODEL, MXU_DTYPE)], [], tr=TR, name="rms_in")
    (proj,) = mm(h, w_main, dims="nn", out_dtypes=[F32], name="proj_main")
    (kr_raw,) = mm(h, w_kr, dims="nn", out_dtypes=[F32], name="proj_kr")
    q, k, v, cqn, ckvn, krope = _rowwise(
        _postproj_body, [proj, kr_raw] + tabs, [gq, gkv],
        [(A_W, F32)] * 3 + [(LORA, MXU_DTYPE)] * 2 + [(HEAD, MXU_DTYPE)], [], tr=TR, name="post_proj")
    a_out, lse_a = _dil_fwd(q, k, v)

    (q_pad,) = mm(cqn, w_uq_p, dims="nn", out_dtypes=[F32], name="q_up")
    (qf,) = _rowwise(_qrope_body, [q_pad] + tabs[3:], [], [(NH * QPAD, MXU_DTYPE)], [], tr=TR, name="q_rope")
    (kv,) = mm(ckvn, w_ukv_p, dims="nn", out_dtypes=[MXU_DTYPE], name="kv_up")
    b_out, lse_b = _mla_fwd(qf, kv, krope)

    mixed = jnp.concatenate([a_out, b_out], axis=1)
    (o,) = mm(mixed, w_out, dims="nn", out_dtypes=[F32], name="out_proj")
    x1, h2 = _rowwise(_mid_body, [x, o], [g2, g3], [(D_MODEL, F32), (D_MODEL, MXU_DTYPE)], [], tr=TR, name="mid_norm")

    def up_epi(acc):
        r = jnp.maximum(acc, 0.0)
        return r * r, r
    u, r = mm(h2, w_up, dims="nn", out_dtypes=[MXU_DTYPE, MXU_DTYPE], name="mlp_up", epi=up_epi)
    (dn,) = mm(u, w_down, dims="nn", out_dtypes=[F32], name="mlp_down")
    dy, dd, loss8, dg4 = _rowwise(_loss_body, [x1, dn, target], [g4], [(D_MODEL, F32), (D_MODEL, MXU_DTYPE)],
                                  [(8, HEAD), (8, D_MODEL)], tr=TR, name="loss_head")

    def dup_epi(acc, rr):
        return (acc * (2.0 * rr.astype(F32)),)
    (dup,) = mm(dd, w_down, dims="nt", out_dtypes=[MXU_DTYPE], name="d_up", epi=dup_epi, extras=(r,))
    (gw_down,) = mm(u, dd, dims="tn", out_dtypes=[WIRE_DTYPE], name="gw_down")
    (dh2,) = mm(dup, w_up, dims="nt", out_dtypes=[F32], name="d_h2")
    (gw_up,) = mm(h2, dup, dims="tn", out_dtypes=[WIRE_DTYPE], name="gw_up")
    dx1, do, dg3, dg2 = _rowwise(_bmid_body, [dy, dh2, x1, o], [g2, g3], [(D_MODEL, F32), (D_MODEL, MXU_DTYPE)],
                                 [(8, D_MODEL), (8, D_MODEL)], tr=TR, name="bwd_mid")
    (dmix,) = mm(do, w_out, dims="nt", out_dtypes=[F32], name="d_mixed")
    (gw_out,) = mm(mixed, do, dims="tn", out_dtypes=[WIRE_DTYPE], name="gw_out")

    dqf, dkn, dvb, dkr = _mla_bwd(qf, kv, krope, dmix, mixed, lse_b)
    (dq_pad,) = _rowwise(_qrope_t_body, [dqf] + tabs[3:], [], [(NH * QPAD, MXU_DTYPE)], [], tr=TR, name="q_rope_t")
    (dcqn,) = mm(dq_pad, w_uq_p, dims="nt", out_dtypes=[F32], name="d_cq")
    (gw_uq_p,) = mm(cqn, dq_pad, dims="tn", out_dtypes=[WIRE_DTYPE], name="gw_uq")
    dkv = jnp.concatenate([dkn, dvb], axis=1)
    (dckvn,) = mm(dkv, w_ukv_p, dims="nt", out_dtypes=[F32], name="d_ckv")
    (gw_ukv_p,) = mm(ckvn, dkv, dims="tn", out_dtypes=[WIRE_DTYPE], name="gw_ukv")

    dq_a, dk_a, dv_a = _dil_bwd(q, k, v, dmix, mixed, lse_a)
    dproj, dkrp, dgq, dgkv = _rowwise(
        _dproj_body, [dq_a, dk_a, dv_a, dcqn, dckvn, proj, dkr] + tabs, [gq, gkv],
        [(MAIN_COLS, MXU_DTYPE), (HEAD, MXU_DTYPE)], [(8, LORA), (8, LORA)], tr=TR, name="d_proj")
    (dha,) = mm(dproj, w_main, dims="nt", out_dtypes=[F32], name="d_h_main")
    (dhb,) = mm(dkrp, w_kr, dims="nt", out_dtypes=[F32], name="d_h_kr")
    (gw_main,) = mm(h, dproj, dims="tn", out_dtypes=[WIRE_DTYPE], name="gw_in_main")
    (gw_kr,) = mm(h, dkrp, dims="tn", out_dtypes=[WIRE_DTYPE], name="gw_in_kr")
    dx, dg1 = _rowwise(_bin_body, [dx1, dha, dhb, x], [g1], [(D_MODEL, F32)], [(8, D_MODEL)], tr=TR, name="bwd_in")

    small = jnp.concatenate([dg1, dg2, dgq, dgkv, dg3, dg4, loss8], axis=1)
    return dx, (gw_main, gw_kr, gw_uq_p, gw_ukv_p, gw_out, gw_up, gw_down), small


def _place():
    x, y, c = lax.axis_index("x"), lax.axis_index("y"), lax.axis_index("c")
    chips = [(1 - x, y), (x, 1 - y), (1 - x, 1 - y)]
    return x, y, c, chips


def _cast_place_body(me_ref, w_ref, o_ref):
    o_ref[...] = w_ref[...].astype(o_ref.dtype)


def _cast_place(me_arr, w, name):
    rows, cols = w.shape
    tr = min(rows, 256)
    grid_spec = pltpu.PrefetchScalarGridSpec(
        num_scalar_prefetch=1, grid=(rows // tr,),
        in_specs=[pl.BlockSpec((tr, cols), lambda i, me: (i, 0))],
        out_specs=pl.BlockSpec((None, tr, cols), lambda i, me: (me[0], i, 0)))
    return _pcall(
        _cast_place_body, name=name, grid_spec=grid_spec,
        out_shape=jax.ShapeDtypeStruct((N_CHIPS, rows, cols), WIRE_DTYPE),
        compiler_params=pltpu.CompilerParams(dimension_semantics=("parallel",)),
    )(me_arr, w)


def _ag_body(*refs, n_w):
    bufs = refs[n_w:2 * n_w]
    send_sems, recv_sems, fsend_sems, frecv_sems = refs[2 * n_w:]
    x, y, c, chips = _place()
    me = 2 * x + y
    sib = (x, y, 1 - c)

    def half_rows(w, which):
        half = bufs[w].shape[1] // 2
        return pl.ds(pl.multiple_of(which * half, 16), half)

    sends, fwds = [], []
    for w in range(n_w):
        mine = bufs[w].at[me, half_rows(w, c)]
        for j, (px, py) in enumerate(chips):
            cp = pltpu.make_async_remote_copy(
                src_ref=mine, dst_ref=mine,
                send_sem=send_sems.at[w * 3 + j], recv_sem=recv_sems.at[w * 3 + j],
                device_id=(px, py, c), device_id_type=MESH)
            cp.start()
            sends.append(cp)
    for w in range(n_w):
        for j, (px, py) in enumerate(chips):
            landed = bufs[w].at[2 * px + py, half_rows(w, c)]
            pltpu.make_async_remote_copy(
                src_ref=landed, dst_ref=landed,
                send_sem=send_sems.at[w * 3 + j], recv_sem=recv_sems.at[w * 3 + j],
                device_id=(px, py, c), device_id_type=MESH).wait_recv()
            fw = pltpu.make_async_remote_copy(
                src_ref=landed, dst_ref=landed,
                send_sem=fsend_sems.at[w * 3 + j], recv_sem=frecv_sems.at[w * 3 + j],
                device_id=sib, device_id_type=MESH)
            fw.start()
            fwds.append(fw)
    for w in range(n_w):
        for j, (px, py) in enumerate(chips):
            passed = bufs[w].at[2 * px + py, half_rows(w, 1 - c)]
            pltpu.make_async_remote_copy(
                src_ref=passed, dst_ref=passed,
                send_sem=fsend_sems.at[w * 3 + j], recv_sem=frecv_sems.at[w * 3 + j],
                device_id=sib, device_id_type=MESH).wait_recv()
    for cp in sends + fwds:
        cp.wait_send()


def _allgather_weights(placed):
    n_w = len(placed)
    return _pcall(
        functools.partial(_ag_body, n_w=n_w), name="weight_allgather",
        in_specs=[ANY] * n_w, out_specs=[ANY] * n_w,
        out_shape=[jax.ShapeDtypeStruct(p.shape, p.dtype) for p in placed],
        input_output_aliases={w: w for w in range(n_w)},
        scratch_shapes=[pltpu.SemaphoreType.DMA((3 * n_w,))] * 4,
    )(*placed)


def _pair_send_body(*refs, n_w):
    ins, outs = refs[:n_w], refs[n_w:2 * n_w]
    send_sems, recv_sems = refs[2 * n_w:]
    x, y, c, _ = _place()
    cps = []
    for w in range(n_w):
        cp = pltpu.make_async_remote_copy(
            src_ref=ins[w].at[:, 1 - c], dst_ref=outs[w],
            send_sem=send_sems.at[w], recv_sem=recv_sems.at[w],
            device_id=(x, y, 1 - c), device_id_type=MESH)
        cp.start()
        cps.append(cp)
    for cp in cps:
        cp.wait()


def _pair_send(grads4):
    n_w = len(grads4)
    return _pcall(
        functools.partial(_pair_send_body, n_w=n_w), name="grad_pair_exchange",
        in_specs=[ANY] * n_w, out_specs=[ANY] * n_w,
        out_shape=[jax.ShapeDtypeStruct((g.shape[0],) + g.shape[2:], g.dtype) for g in grads4],
        scratch_shapes=[pltpu.SemaphoreType.DMA((n_w,))] * 2,
    )(*grads4)


def _pair_add_body(c_ref, mine_ref, theirs_ref, o_ref):
    o_ref[...] = (mine_ref[...].astype(F32) + theirs_ref[...].astype(F32)).astype(o_ref.dtype)


def _pair_add(c_arr, g4, recv, name):
    _, _, hr, cols = g4.shape
    tr = min(hr, 256)
    grid_spec = pltpu.PrefetchScalarGridSpec(
        num_scalar_prefetch=1, grid=(N_CHIPS, hr // tr),
        in_specs=[pl.BlockSpec((None, None, tr, cols), lambda s, i, c: (s, c[0], i, 0)),
                  pl.BlockSpec((None, tr, cols), lambda s, i, c: (s, i, 0))],
        out_specs=pl.BlockSpec((None, tr, cols), lambda s, i, c: (s, i, 0)))
    return _pcall(
        _pair_add_body, name=name, grid_spec=grid_spec,
        out_shape=jax.ShapeDtypeStruct(recv.shape, recv.dtype),
        compiler_params=pltpu.CompilerParams(dimension_semantics=("parallel", "parallel")),
    )(c_arr, g4, recv)


def _scatter_body(*refs, n_w):
    ins, outs = refs[:n_w], refs[n_w:2 * n_w]
    send_sems, recv_sems = refs[2 * n_w:]
    x, y, c, chips = _place()
    me = 2 * x + y
    todo = []
    for w in range(n_w):
        for j, (px, py) in enumerate(chips):
            cp = pltpu.make_async_remote_copy(
                src_ref=ins[w].at[2 * px + py], dst_ref=outs[w].at[me],
                send_sem=send_sems.at[w * 3 + j], recv_sem=recv_sems.at[w * 3 + j],
                device_id=(px, py, c), device_id_type=MESH)
            cp.start()
            todo.append(cp)
    for t in todo:
        t.wait()


def _scatter(parts):
    n_w = len(parts)
    return _pcall(
        functools.partial(_scatter_body, n_w=n_w), name="grad_scatter",
        in_specs=[ANY] * n_w, out_specs=[ANY] * n_w,
        out_shape=[jax.ShapeDtypeStruct(p.shape, p.dtype) for p in parts],
        scratch_shapes=[pltpu.SemaphoreType.DMA((3 * n_w,))] * 2,
    )(*parts)


def _sum4_body(me_ref, p_ref, l0, l1, l2, l3, o_ref):
    me = me_ref[0]
    t = [jnp.where(me == j, p_ref[...], l[...]).astype(F32) for j, l in enumerate((l0, l1, l2, l3))]
    o_ref[...] = ((t[0] + t[1]) + t[2]) + t[3]


def _sum4(me_arr, part, landed, name):
    _, hr, cols = part.shape
    tr = min(hr, 256)

    def slot(j):
        return lambda i, me: (jnp.where(me[0] == j, (j + 1) % N_CHIPS, j), i, 0)

    grid_spec = pltpu.PrefetchScalarGridSpec(
        num_scalar_prefetch=1, grid=(hr // tr,),
        in_specs=[pl.BlockSpec((None, tr, cols), lambda i, me: (me[0], i, 0))]
        + [pl.BlockSpec((None, tr, cols), slot(j)) for j in range(N_CHIPS)],
        out_specs=pl.BlockSpec((tr, cols), lambda i, me: (i, 0)))
    return _pcall(
        _sum4_body, name=name, grid_spec=grid_spec,
        out_shape=jax.ShapeDtypeStruct((hr, cols), F32),
        compiler_params=pltpu.CompilerParams(dimension_semantics=("parallel",)),
    )(me_arr, part, landed, landed, landed, landed)


def _pair_swap_body(*refs, n_w):
    ins, outs = refs[:n_w], refs[n_w:2 * n_w]
    send_sems, recv_sems = refs[2 * n_w:]
    x, y, c, _ = _place()
    todo = []
    for w in range(n_w):
        cp = pltpu.make_async_remote_copy(
            src_ref=ins[w], dst_ref=outs[w],
            send_sem=send_sems.at[w], recv_sem=recv_sems.at[w],
            device_id=(x, y, 1 - c), device_id_type=MESH)
        cp.start()
        todo.append(cp)
    for t in todo:
        t.wait()


def _pair_swap(halves):
    n_w = len(halves)
    return _pcall(
        functools.partial(_pair_swap_body, n_w=n_w), name="grad_pair_swap",
        in_specs=[ANY] * n_w, out_specs=[ANY] * n_w,
        out_shape=[jax.ShapeDtypeStruct(h.shape, h.dtype) for h in halves],
        scratch_shapes=[pltpu.SemaphoreType.DMA((n_w,))] * 2,
    )(*halves)


def _small_gather_body(x_ref, out_ref, send_sems, recv_sems, local_sem):
    m_per = x_ref.shape[0]
    x, y, c, chips = _place()
    me, sibling = (x, y, c), (x, y, 1 - c)

    def rows(px, py, pc):
        return out_ref.at[pl.ds((4 * px + 2 * py + pc) * m_per, m_per), :]

    def copy(k, block, to, src=None):
        return pltpu.make_async_remote_copy(
            src_ref=rows(*block) if src is None else src, dst_ref=rows(*block),
            send_sem=send_sems.at[k], recv_sem=recv_sems.at[k], device_id=to, device_id_type=MESH)

    mine = pltpu.make_async_copy(x_ref, rows(*me), local_sem)
    mine.start()
    first = [copy(0, me, sibling, src=x_ref)]
    first += [copy(1 + j, me, (*chip, c), src=x_ref) for j, chip in enumerate(chips)]
    for cp in first:
        cp.start()
    passed = [copy(4 + j, (*chip, c), sibling) for j, chip in enumerate(chips)]
    for j, chip in enumerate(chips):
        copy(1 + j, (*chip, c), me).wait_recv()
        passed[j].start()
    copy(0, sibling, me).wait_recv()
    for j, chip in enumerate(chips):
        copy(4 + j, (*chip, 1 - c), me).wait_recv()
    for cp in first + passed:
        cp.wait_send()
    mine.wait()


def _small_gather(small):
    m_per, n = small.shape
    return _pcall(
        _small_gather_body, name="small_allgather",
        out_shape=jax.ShapeDtypeStruct((N_DEV * m_per, n), small.dtype),
        in_specs=[pl.BlockSpec(memory_space=pltpu.VMEM)],
        out_specs=pl.BlockSpec(memory_space=pltpu.VMEM),
        scratch_shapes=[pltpu.SemaphoreType.DMA((7,)), pltpu.SemaphoreType.DMA((7,)), pltpu.SemaphoreType.DMA],
    )(small)


def _adamw(w, g, m, v):
    m = ADAM_B1 * m + (1.0 - ADAM_B1) * g
    v = ADAM_B2 * v + (1.0 - ADAM_B2) * (g * g)
    m_hat = m / (1.0 - ADAM_B1 ** ADAM_STEP)
    v_hat = v / (1.0 - ADAM_B2 ** ADAM_STEP)
    delta = -ADAM_LR * (m_hat / (jnp.sqrt(v_hat) + ADAM_EPS) + ADAM_WD * w)
    return delta, m, v


def _adamw_body(c_ref, w_ref, own_ref, sib_ref, m_ref, v_ref, g_ref, d_ref, nm_ref, nv_ref, *, nh):
    mine = (pl.program_id(0) // nh) == c_ref[0]
    g = jnp.where(mine, own_ref[...], sib_ref[...])
    g_ref[...] = g
    d, m, v = _adamw(w_ref[...], g, m_ref[...], v_ref[...])
    d_ref[...] = d
    nm_ref[...] = m
    nv_ref[...] = v


def _adamw_call(c_arr, w, own, sib, m, v, name):
    rows, cols = w.shape
    tr = min(rows // 2, 256)
    nh = (rows // 2) // tr
    full = pl.BlockSpec((tr, cols), lambda i, c: (i, 0))
    half = pl.BlockSpec((tr, cols), lambda i, c: (i % nh, 0))
    grid_spec = pltpu.PrefetchScalarGridSpec(
        num_scalar_prefetch=1, grid=(rows // tr,),
        in_specs=[full, half, half, full, full], out_specs=[full] * 4)
    return _pcall(
        functools.partial(_adamw_body, nh=nh), name=name, grid_spec=grid_spec,
        out_shape=[jax.ShapeDtypeStruct(w.shape, F32)] * 4,
        compiler_params=pltpu.CompilerParams(dimension_semantics=("parallel",)),
    )(c_arr, w, own, sib, m, v)


def _small_update_body(gath_ref, w_ref, m_ref, v_ref, g_ref, d_ref, nm_ref, nv_ref, loss_ref, *, n_gain):
    tot = gath_ref[0:1, :]
    for i in range(1, gath_ref.shape[0]):
        tot = tot + gath_ref[i:i + 1, :]
    g = tot[:, 0:n_gain]
    g_ref[...] = g
    d, m, v = _adamw(w_ref[...], g, m_ref[...], v_ref[...])
    d_ref[...] = d
    nm_ref[...] = m
    nv_ref[...] = v
    loss_ref[...] = (0.5 / D_MODEL) * jnp.sum(tot[:, n_gain:n_gain + HEAD], axis=1, keepdims=True) * jnp.ones((1, HEAD), F32)


def _small_update(gath, w, m, v):
    n_gain = w.shape[1]
    vm = pl.BlockSpec(memory_space=pltpu.VMEM)
    return _pcall(
        functools.partial(_small_update_body, n_gain=n_gain), name="gain_update",
        in_specs=[vm] * 4, out_specs=[vm] * 5,
        out_shape=[jax.ShapeDtypeStruct((1, n_gain), F32)] * 4 + [jax.ShapeDtypeStruct((1, HEAD), F32)],
    )(gath, w, m, v)


def kernel(x, positions, norm_attn_pre, norm_attn_post, w_in, q_latent_norm, kv_latent_norm, w_uq, w_ukv, w_out, norm_mlp_pre, norm_mlp_post, w_up, w_down, loss_target, m_norm_attn_pre, m_norm_attn_post, m_w_in, m_q_latent_norm, m_kv_latent_norm, m_w_uq, m_w_ukv, m_w_out, m_norm_mlp_pre, m_norm_mlp_post, m_w_up, m_w_down, v_norm_attn_pre, v_norm_attn_post, v_w_in, v_q_latent_norm, v_kv_latent_norm, v_w_uq, v_w_ukv, v_w_out, v_norm_mlp_pre, v_norm_mlp_post, v_w_up, v_w_down):
    T = x.shape[1]
    c_arr = lax.axis_index("c").astype(jnp.int32).reshape(1)
    me_arr = (2 * lax.axis_index("x") + lax.axis_index("y")).astype(jnp.int32).reshape(1)
    names = ["w_in", "w_uq", "w_ukv", "w_out", "w_up", "w_down"]

    mats = [w_in[0], w_uq[0], w_ukv[0], w_out[0], w_up[0], w_down[0]]
    placed = [_cast_place(me_arr, w, "cast_" + n) for w, n in zip(mats, names)]
    win_g, wuq_g, wukv_g, wout_g, wup_g, wdown_g = _allgather_weights(placed)

    col_major = lambda g: jnp.transpose(g, (1, 0, 2)).reshape(g.shape[1], N_CHIPS * g.shape[2])
    win_full = col_major(win_g)
    w_main = win_full[:, :MAIN_COLS]
    w_kr = jnp.pad(win_full[:, MAIN_COLS:], ((0, 0), (0, HEAD - ROPE_B)))
    wuq_full = col_major(wuq_g).reshape(LORA, NH, HEAD + ROPE_B)
    w_uq_p = jnp.pad(wuq_full, ((0, 0), (0, 0), (0, QPAD - HEAD - ROPE_B))).reshape(LORA, NH * QPAD)
    w_ukv_p = col_major(wukv_g).reshape(LORA, NH, 2, HEAD).transpose(0, 2, 1, 3).reshape(LORA, 2 * A_W)
    w_out_f = wout_g.reshape(2 * A_W, D_MODEL)
    w_up_f = col_major(wup_g)
    w_down_f = wdown_g.reshape(D_FF, D_MODEL)
    cast = lambda a: a.astype(MXU_DTYPE)

    dx, gws, small = _local_step(
        x[0], positions[0].astype(F32).reshape(T, 1), loss_target[0],
        norm_attn_pre, norm_attn_post, q_latent_norm, kv_latent_norm, norm_mlp_pre, norm_mlp_post,
        cast(w_main), cast(w_kr), cast(w_uq_p), cast(w_ukv_p), cast(w_out_f), cast(w_up_f), cast(w_down_f))
    gw_main, gw_kr, gw_uq_p, gw_ukv_p, gw_out, gw_up, gw_down = gws

    to_shards = lambda g: jnp.transpose(g.reshape(g.shape[0], N_CHIPS, g.shape[1] // N_CHIPS), (1, 0, 2))
    gw_in = to_shards(jnp.concatenate([gw_main, gw_kr[:, :ROPE_B]], axis=1))
    gw_uq = to_shards(gw_uq_p.reshape(LORA, NH, QPAD)[:, :, :HEAD + ROPE_B].reshape(LORA, NH * (HEAD + ROPE_B)))
    gw_ukv = to_shards(gw_ukv_p.reshape(LORA, 2, NH, HEAD).transpose(0, 2, 1, 3).reshape(LORA, 2 * A_W))
    full = [gw_in, gw_uq, gw_ukv, gw_out.reshape(N_CHIPS, LORA, D_MODEL), to_shards(gw_up),
            gw_down.reshape(N_CHIPS, D_MODEL, D_MODEL)]
    full4 = [g.reshape(N_CHIPS, 2, g.shape[1] // 2, g.shape[2]) for g in full]

    from_sib = _pair_send(full4)
    parts = [_pair_add(c_arr, g4, r, "pair_add_" + n) for g4, r, n in zip(full4, from_sib, names)]
    landed = _scatter(parts)
    halves = [_sum4(me_arr, p, l, "chip_sum_" + n) for p, l, n in zip(parts, landed, names)]
    from_sib2 = _pair_swap(halves)

    ms = [m_w_in[0], m_w_uq[0], m_w_ukv[0], m_w_out[0], m_w_up[0], m_w_down[0]]
    vs = [v_w_in[0], v_w_uq[0], v_w_ukv[0], v_w_out[0], v_w_up[0], v_w_down[0]]
    upd = [_adamw_call(c_arr, w, own, sib, m, v, "adamw_" + n)
           for w, own, sib, m, v, n in zip(mats, halves, from_sib2, ms, vs, names)]
    grads = [u[0] for u in upd]

    gath = _small_gather(small)
    gains = [norm_attn_pre, norm_attn_post, q_latent_norm, kv_latent_norm, norm_mlp_pre, norm_mlp_post]
    gm = [m_norm_attn_pre, m_norm_attn_post, m_q_latent_norm, m_kv_latent_norm, m_norm_mlp_pre, m_norm_mlp_post]
    gv = [v_norm_attn_pre, v_norm_attn_post, v_q_latent_norm, v_kv_latent_norm, v_norm_mlp_pre, v_norm_mlp_post]
    cat = lambda xs: jnp.concatenate(xs, axis=1)
    g_s, d_s, m_s, v_s, loss_v = _small_update(gath, cat(gains), cat(gm), cat(gv))
    widths = [a.shape[1] for a in gains]
    offs = [sum(widths[:i]) for i in range(len(widths))]
    split = lambda a: [a[:, o:o + w] for o, w in zip(offs, widths)]
    g_gain, d_gain, m_gain, v_gain = split(g_s), split(d_s), split(m_s), split(v_s)

    def ordered(gain_list, mat_list):
        gl, ml = gain_list, [a[None] for a in mat_list]
        return [gl[0], gl[1], ml[0], gl[2], gl[3], ml[1], ml[2], ml[3], gl[4], gl[5], ml[4], ml[5]]

    loss = loss_v[0, 0]
    return (loss, dx[None],
            *ordered(g_gain, grads),
            *ordered(d_gain, [u[1] for u in upd]),
            *ordered(m_gain, [u[2] for u in upd]),
            *ordered(v_gain, [u[3] for u in upd]))
```

```python
import functools

import jax
import jax.numpy as jnp
from jax import lax
from jax.experimental import pallas as pl
from jax.experimental.pallas import tpu as pltpu

F32 = jnp.float32
BF16 = jnp.bfloat16
MXU_DTYPE = jnp.bfloat16
WIRE_DTYPE = jnp.bfloat16

D_MODEL = 2048
HEAD = 128
NH = 8
A_W = NH * HEAD
LORA = 512
ROPE_B = 64
QPAD = 256
MAIN_COLS = 3 * A_W + 2 * LORA
IN_COLS = MAIN_COLS + ROPE_B
D_FF = 4 * D_MODEL
DIL = (1, 4, 16)
ROT_A = 32
ROPE_THETA = 500000.0
EPS = 1e-6
NEG = -1e30
N_CHIPS = 4
N_DEV = 8

ADAM_LR = 0.001
ADAM_B1 = 0.9
ADAM_B2 = 0.999
ADAM_EPS = 1e-08
ADAM_WD = 0.01
ADAM_STEP = 10

MESH = pl.DeviceIdType.MESH
ANY = pl.BlockSpec(memory_space=pl.ANY)


def _pcall(body, **kw):
    return pl.pallas_call(body, **kw)


_DIMS = {
    "nn": (((1,), (0,)), ((), ())),
    "nt": (((1,), (1,)), ((), ())),
    "tn": (((0,), (0,)), ((), ())),
}


def _mm_body(*refs, dims, nk, epi, n_extra, n_out):
    a_ref, b_ref = refs[0], refs[1]
    extra = refs[2:2 + n_extra]
    outs = refs[2 + n_extra:2 + n_extra + n_out]
    part = lax.dot_general(a_ref[...], b_ref[...], _DIMS[dims], preferred_element_type=F32)

    def finish(acc):
        res = epi(acc, *[r[...] for r in extra]) if epi is not None else (acc,)
        for o_ref, o in zip(outs, res):
            o_ref[...] = o.astype(o_ref.dtype)

    if nk == 1:
        finish(part)
        return
    acc_ref = refs[-1]
    k = pl.program_id(2)

    @pl.when(k == 0)
    def _():
        acc_ref[...] = part

    @pl.when(k > 0)
    def _():
        acc_ref[...] += part

    @pl.when(k == nk - 1)
    def _():
        finish(acc_ref[...])


def _matmul(a, b, *, dims, out_dtypes, tm, tn, tk, name, epi=None, extras=()):
    if dims == "nn":
        (M, K), (K2, N) = a.shape, b.shape
    elif dims == "nt":
        (M, K), (N, K2) = a.shape, b.shape
    else:
        (K, M), (K2, N) = a.shape, b.shape
    assert K == K2, (a.shape, b.shape, dims)
    tm, tn, tk = min(tm, M), min(tn, N), min(tk, K)
    assert M % tm == 0 and N % tn == 0 and K % tk == 0, (name, M, N, K, tm, tn, tk)
    nk = K // tk
    a_spec = {"nn": pl.BlockSpec((tm, tk), lambda i, j, k: (i, k)),
              "nt": pl.BlockSpec((tm, tk), lambda i, j, k: (i, k)),
              "tn": pl.BlockSpec((tk, tm), lambda i, j, k: (k, i))}[dims]
    b_spec = {"nn": pl.BlockSpec((tk, tn), lambda i, j, k: (k, j)),
              "nt": pl.BlockSpec((tn, tk), lambda i, j, k: (j, k)),
              "tn": pl.BlockSpec((tk, tn), lambda i, j, k: (k, j))}[dims]
    o_spec = pl.BlockSpec((tm, tn), lambda i, j, k: (i, j))
    body = functools.partial(_mm_body, dims=dims, nk=nk, epi=epi,
                             n_extra=len(extras), n_out=len(out_dtypes))
    res = _pcall(
        body, name=name,
        grid=(M // tm, N // tn, nk),
        in_specs=[a_spec, b_spec] + [o_spec] * len(extras),
        out_specs=[o_spec] * len(out_dtypes),
        out_shape=[jax.ShapeDtypeStruct((M, N), dt) for dt in out_dtypes],
        scratch_shapes=[pltpu.VMEM((tm, tn), F32)] if nk > 1 else [],
        compiler_params=pltpu.CompilerParams(
            dimension_semantics=("parallel", "parallel", "arbitrary")),
    )(a, b, *extras)
    return list(res)


def _rowwise(body, row_ins, vec_ins, row_outs, acc_outs, *, tr, name):
    T = row_ins[0].shape[0]
    tr = min(tr, T)
    assert T % tr == 0
    in_specs = [pl.BlockSpec((tr, a.shape[1]), lambda i: (i, 0)) for a in row_ins]
    in_specs += [pl.BlockSpec(a.shape, lambda i: (0, 0)) for a in vec_ins]
    out_specs = [pl.BlockSpec((tr, w), lambda i: (i, 0)) for (w, _) in row_outs]
    out_specs += [pl.BlockSpec(s, lambda i: (0, 0)) for s in acc_outs]
    out_shape = [jax.ShapeDtypeStruct((T, w), dt) for (w, dt) in row_outs]
    out_shape += [jax.ShapeDtypeStruct(s, F32) for s in acc_outs]
    sem = "arbitrary" if acc_outs else "parallel"
    return list(_pcall(
        body, name=name, grid=(T // tr,), in_specs=in_specs, out_specs=out_specs,
        out_shape=out_shape,
        compiler_params=pltpu.CompilerParams(dimension_semantics=(sem,)),
    )(*row_ins, *vec_ins))


def _rstd(x):
    return lax.rsqrt(jnp.mean(x * x, axis=-1, keepdims=True) + EPS)


def _rms_bwd(x, rstd, dyg):
    xh = x * rstd
    return rstd * (dyg - xh * jnp.mean(dyg * xh, axis=-1, keepdims=True)), xh


def _fold8(v):
    r, w = v.shape
    return jnp.sum(v.reshape(r // 8, 8, w), axis=0)


def _acc(ref, val):
    first = pl.program_id(0) == 0

    @pl.when(first)
    def _():
        ref[...] = val

    @pl.when(jnp.logical_not(first))
    def _():
        ref[...] += val


def _rope(x, c, sa, sb, half):
    return x * c + pltpu.roll(x, HEAD - half, 1) * sa + pltpu.roll(x, half, 1) * sb


def _rope_t(dy, c, sa, sb, half):
    return dy * c - pltpu.roll(dy, HEAD - half, 1) * sa - pltpu.roll(dy, half, 1) * sb


def _rope_tab_body(pos_ref, inv_ref, ca, saa, sab, cb, sba, sbb):
    pos = pos_ref[...]
    lane = lax.broadcasted_iota(jnp.int32, (pos.shape[0], HEAD), 1)
    ang_a = pos * inv_ref[0:1, :]
    ang_b = pos * inv_ref[1:2, :]
    c, s = jnp.cos(ang_a), jnp.sin(ang_a)
    ha = ROT_A // 2
    ca[...] = jnp.where(lane < ROT_A, c, 1.0)
    saa[...] = jnp.where(lane < ha, -s, 0.0)
    sab[...] = jnp.where((lane >= ha) & (lane < ROT_A), s, 0.0)
    c, s = jnp.cos(ang_b), jnp.sin(ang_b)
    hb = ROPE_B // 2
    cb[...] = jnp.where(lane < ROPE_B, c, 1.0)
    sba[...] = jnp.where(lane < hb, -s, 0.0)
    sbb[...] = jnp.where((lane >= hb) & (lane < ROPE_B), s, 0.0)


def _rms_fwd_body(x_ref, g_ref, h_ref):
    x = x_ref[...]
    h_ref[...] = ((x * _rstd(x)) * g_ref[...]).astype(h_ref.dtype)


def _postproj_body(p_ref, kr_ref, ca, saa, sab, cb, sba, sbb, gq_ref, gkv_ref,
                   q_ref, k_ref, v_ref, cqn_ref, ckvn_ref, krope_ref):
    c, sa, sb = ca[...], saa[...], sab[...]
    for h in range(NH):
        lo = h * HEAD
        q_ref[:, lo:lo + HEAD] = _rope(p_ref[:, lo:lo + HEAD], c, sa, sb, ROT_A // 2).astype(q_ref.dtype)
        k_ref[:, lo:lo + HEAD] = _rope(p_ref[:, A_W + lo:A_W + lo + HEAD], c, sa, sb, ROT_A // 2).astype(k_ref.dtype)
    v_ref[...] = p_ref[:, 2 * A_W:3 * A_W].astype(v_ref.dtype)
    cq = p_ref[:, 3 * A_W:3 * A_W + LORA]
    cqn_ref[...] = ((cq * _rstd(cq)) * gq_ref[...]).astype(cqn_ref.dtype)
    ckv = p_ref[:, 3 * A_W + LORA:MAIN_COLS]
    ckvn_ref[...] = ((ckv * _rstd(ckv)) * gkv_ref[...]).astype(ckvn_ref.dtype)
    krope_ref[...] = _rope(kr_ref[...], cb[...], sba[...], sbb[...], ROPE_B // 2).astype(krope_ref.dtype)


def _qrope_body(qp_ref, cb, sba, sbb, q_ref):
    c, sa, sb = cb[...], sba[...], sbb[...]
    for h in range(NH):
        lo = h * QPAD
        q_ref[:, lo:lo + HEAD] = qp_ref[:, lo:lo + HEAD].astype(q_ref.dtype)
        q_ref[:, lo + HEAD:lo + QPAD] = _rope(qp_ref[:, lo + HEAD:lo + QPAD], c, sa, sb, ROPE_B // 2).astype(q_ref.dtype)


def _qrope_t_body(dq_ref, cb, sba, sbb, o_ref):
    c, sa, sb = cb[...], sba[...], sbb[...]
    for h in range(NH):
        lo = h * QPAD
        o_ref[:, lo:lo + HEAD] = dq_ref[:, lo:lo + HEAD].astype(o_ref.dtype)
        o_ref[:, lo + HEAD:lo + QPAD] = _rope_t(dq_ref[:, lo + HEAD:lo + QPAD], c, sa, sb, ROPE_B // 2).astype(o_ref.dtype)


def _mid_body(x_ref, o_ref, g2_ref, g3_ref, x1_ref, h2_ref):
    o = o_ref[...]
    x1 = x_ref[...] + (o * _rstd(o)) * g2_ref[...]
    x1_ref[...] = x1
    h2_ref[...] = ((x1 * _rstd(x1)) * g3_ref[...]).astype(h2_ref.dtype)


def _loss_body(x1_ref, d_ref, t_ref, g4_ref, dy_ref, dd_ref, loss_ref, dg4_ref):
    d = d_ref[...]
    rstd = _rstd(d)
    y = x1_ref[...] + (d * rstd) * g4_ref[...]
    e = y - t_ref[...]
    dy = e * (1.0 / D_MODEL)
    dy_ref[...] = dy
    dd, dh = _rms_bwd(d, rstd, dy * g4_ref[...])
    dd_ref[...] = dd.astype(dd_ref.dtype)
    _acc(dg4_ref, _fold8(dy * dh))
    e8 = _fold8(e * e)
    l = e8[:, 0:HEAD]
    for j in range(1, D_MODEL // HEAD):
        l = l + e8[:, j * HEAD:(j + 1) * HEAD]
    _acc(loss_ref, l)


def _bmid_body(dy_ref, dh2_ref, x1_ref, o_ref, g2_ref, g3_ref, dx1_ref, do_ref, dg3_ref, dg2_ref):
    x1 = x1_ref[...]
    dh2 = dh2_ref[...]
    dn, x1h = _rms_bwd(x1, _rstd(x1), dh2 * g3_ref[...])
    dx1 = dy_ref[...] + dn
    dx1_ref[...] = dx1
    _acc(dg3_ref, _fold8(dh2 * x1h))
    o = o_ref[...]
    do, oh = _rms_bwd(o, _rstd(o), dx1 * g2_ref[...])
    do_ref[...] = do.astype(do_ref.dtype)
    _acc(dg2_ref, _fold8(dx1 * oh))


def _dproj_body(dq_ref, dk_ref, dv_ref, dcq_ref, dckv_ref, p_ref, dkr_ref,
                ca, saa, sab, cb, sba, sbb, gq_ref, gkv_ref,
                dp_ref, dkrp_ref, dgq_ref, dgkv_ref):
    c, sa, sb = ca[...], saa[...], sab[...]
    for h in range(NH):
        lo = h * HEAD
        dp_ref[:, lo:lo + HEAD] = _rope_t(dq_ref[:, lo:lo + HEAD], c, sa, sb, ROT_A // 2).astype(dp_ref.dtype)
        dp_ref[:, A_W + lo:A_W + lo + HEAD] = _rope_t(dk_ref[:, lo:lo + HEAD], c, sa, sb, ROT_A // 2).astype(dp_ref.dtype)
    dp_ref[:, 2 * A_W:3 * A_W] = dv_ref[...].astype(dp_ref.dtype)
    cq = p_ref[:, 3 * A_W:3 * A_W + LORA]
    dcqn = dcq_ref[...]
    dcq, cqh = _rms_bwd(cq, _rstd(cq), dcqn * gq_ref[...])
    dp_ref[:, 3 * A_W:3 * A_W + LORA] = dcq.astype(dp_ref.dtype)
    _acc(dgq_ref, _fold8(dcqn * cqh))
    ckv = p_ref[:, 3 * A_W + LORA:MAIN_COLS]
    dckvn = dckv_ref[...]
    dckv, ckvh = _rms_bwd(ckv, _rstd(ckv), dckvn * gkv_ref[...])
    dp_ref[:, 3 * A_W + LORA:MAIN_COLS] = dckv.astype(dp_ref.dtype)
    _acc(dgkv_ref, _fold8(dckvn * ckvh))
    dkr = dkr_ref[:, 0:HEAD]
    for h in range(1, NH):
        dkr = dkr + dkr_ref[:, h * HEAD:(h + 1) * HEAD]
    dkrp_ref[...] = _rope_t(dkr, cb[...], sba[...], sbb[...], ROPE_B // 2).astype(dkrp_ref.dtype)


def _bin_body(dx1_ref, dha_ref, dhb_ref, x_ref, g1_ref, dx_ref, dg1_ref):
    x = x_ref[...]
    dh = dha_ref[...] + dhb_ref[...]
    dn, xh = _rms_bwd(x, _rstd(x), dh * g1_ref[...])
    dx_ref[...] = dx1_ref[...] + dn
    _acc(dg1_ref, _fold8(dh * xh))


def _dot_nt(a, b):
    return lax.dot_general(a, b, _DIMS["nt"], preferred_element_type=F32)


def _dot_tn(a, b):
    return lax.dot_general(a, b, _DIMS["tn"], preferred_element_type=F32)


def _dot_nn(a, b):
    return jnp.dot(a, b, preferred_element_type=F32)


DIL_SCALE = HEAD ** -0.5
DIL_CHUNK = 256


def _dil_rows(t, d):
    r = t & (d - 1)
    n = t >> (d.bit_length() - 1)
    start = r + n * (HEAD * d)
    has_prev = n > 0
    pstart = jnp.where(has_prev, start - HEAD * d, start)
    if d == 1:
        return pl.ds(pl.multiple_of(start, HEAD), HEAD), pl.ds(pl.multiple_of(pstart, HEAD), HEAD), has_prev
    return pl.ds(start, HEAD, stride=d), pl.ds(pstart, HEAD, stride=d), has_prev


def _dil_band():
    row = lax.broadcasted_iota(jnp.int32, (HEAD, 2 * HEAD), 0)
    col = lax.broadcasted_iota(jnp.int32, (HEAD, 2 * HEAD), 1)
    return (col >= row) & (col <= row + HEAD), col >= HEAD


def _dil_fwd_body(q_ref, k_ref, v_ref, a_ref, lse_ref, o1, o2, o3, l1, l2, l3, *, nt, unroll):
    band, is_cur = _dil_band()
    for d, o_sc, l_sc in zip(DIL, (o1, o2, o3), (l1, l2, l3)):

        def tile(t, carry, d=d, o_sc=o_sc, l_sc=l_sc):
            rows, prows, has_prev = _dil_rows(t, d)
            q = q_ref[rows, :].astype(MXU_DTYPE)
            kk = jnp.concatenate([k_ref[prows, :], k_ref[rows, :]], axis=0).astype(MXU_DTYPE)
            vv = jnp.concatenate([v_ref[prows, :], v_ref[rows, :]], axis=0).astype(MXU_DTYPE)
            ok = band & (is_cur | has_prev)
            s = jnp.where(ok, _dot_nt(q, kk) * DIL_SCALE, NEG)
            m = jnp.max(s, axis=1, keepdims=True)
            p = jnp.exp(s - m)
            den = jnp.sum(p, axis=1, keepdims=True)
            o_sc[rows, :] = _dot_nn((p / den).astype(MXU_DTYPE), vv)
            l_sc[rows, :] = jnp.broadcast_to(m + jnp.log(den), (HEAD, HEAD))
            return carry

        lax.fori_loop(0, nt, tile, 0, unroll=unroll)

    def merge(i, carry):
        rs = pl.ds(pl.multiple_of(i * DIL_CHUNK, DIL_CHUNK), DIL_CHUNK)
        la, lb, lc = l1[rs, :], l2[rs, :], l3[rs, :]
        m = jnp.maximum(jnp.maximum(la, lb), lc)
        wa, wb, wc = jnp.exp(la - m), jnp.exp(lb - m), jnp.exp(lc - m)
        den = wa + wb + wc
        a = (wa / den) * o1[rs, :] + (wb / den) * o2[rs, :] + (wc / den) * o3[rs, :]
        a_ref[rs, :] = a.astype(a_ref.dtype)
        lse_ref[rs, :] = m + jnp.log(den)
        return carry

    lax.fori_loop(0, q_ref.shape[0] // DIL_CHUNK, merge, 0)


def _dil_fwd(q, k, v):
    T = q.shape[0]
    spec = pl.BlockSpec((T, HEAD), lambda h: (0, h))
    return _pcall(
        functools.partial(_dil_fwd_body, nt=T // HEAD, unroll=4), name="dil_fwd",
        grid=(NH,), in_specs=[spec] * 3, out_specs=[spec] * 2,
        out_shape=[jax.ShapeDtypeStruct((T, A_W), MXU_DTYPE), jax.ShapeDtypeStruct((T, A_W), F32)],
        scratch_shapes=[pltpu.VMEM((T, HEAD), F32)] * 6,
        compiler_params=pltpu.CompilerParams(dimension_semantics=("parallel",)),
    )(q, k, v)


def _dil_bwd_body(q_ref, k_ref, v_ref, do_ref, a_ref, lse_ref, dq_ref, dk_ref, dv_ref, dl_sc, *, nt, unroll):
    band, is_cur = _dil_band()

    def prep(i, carry):
        rs = pl.ds(pl.multiple_of(i * DIL_CHUNK, DIL_CHUNK), DIL_CHUNK)
        dl = jnp.sum(do_ref[rs, :] * a_ref[rs, :].astype(F32), axis=1, keepdims=True)
        dl_sc[rs, :] = jnp.broadcast_to(dl, (DIL_CHUNK, HEAD))
        zero = jnp.zeros((DIL_CHUNK, HEAD), F32)
        dq_ref[rs, :] = zero
        dk_ref[rs, :] = zero
        dv_ref[rs, :] = zero
        return carry

    lax.fori_loop(0, q_ref.shape[0] // DIL_CHUNK, prep, 0)

    for d in DIL:

        def tile(t, carry, d=d):
            rows, prows, has_prev = _dil_rows(t, d)
            q = q_ref[rows, :].astype(MXU_DTYPE)
            kk = jnp.concatenate([k_ref[prows, :], k_ref[rows, :]], axis=0).astype(MXU_DTYPE)
            vv = jnp.concatenate([v_ref[prows, :], v_ref[rows, :]], axis=0).astype(MXU_DTYPE)
            do = do_ref[rows, :].astype(MXU_DTYPE)
            lse = lse_ref[rows, :]
            dl = dl_sc[rows, :]
            ok = band & (is_cur | has_prev)
            s = _dot_nt(q, kk) * DIL_SCALE
            p = jnp.where(ok, jnp.exp(s - jnp.concatenate([lse, lse], axis=1)), 0.0)
            ds = (p * (_dot_nt(do, vv) - jnp.concatenate([dl, dl], axis=1))).astype(MXU_DTYPE)
            dq_ref[rows, :] += _dot_nn(ds, kk) * DIL_SCALE
            dkk = _dot_tn(ds, q) * DIL_SCALE
            dvv = _dot_tn(p.astype(MXU_DTYPE), do)
            dk_ref[rows, :] += dkk[HEAD:, :]
            dv_ref[rows, :] += dvv[HEAD:, :]
            dk_ref[prows, :] += dkk[:HEAD, :]
            dv_ref[prows, :] += dvv[:HEAD, :]
            return carry

        lax.fori_loop(0, nt, tile, 0, unroll=unroll)


def _dil_bwd(q, k, v, dmix, mixed, lse):
    T = q.shape[0]
    spec = pl.BlockSpec((T, HEAD), lambda h: (0, h))
    return _pcall(
        functools.partial(_dil_bwd_body, nt=T // HEAD, unroll=2), name="dil_bwd",
        grid=(NH,), in_specs=[spec] * 6, out_specs=[spec] * 3,
        out_shape=[jax.ShapeDtypeStruct((T, A_W), F32)] * 3,
        scratch_shapes=[pltpu.VMEM((T, HEAD), F32)],
        compiler_params=pltpu.CompilerParams(dimension_semantics=("parallel",)),
    )(q, k, v, dmix, mixed, lse)


MLA_SCALE = (HEAD + ROPE_B) ** -0.5
MLA_T = 512
MLA_HP = 2


def _tri(t):
    row = lax.broadcasted_iota(jnp.int32, (t, t), 0)
    col = lax.broadcasted_iota(jnp.int32, (t, t), 1)
    return col <= row


def _lanes(x, n):
    return jnp.tile(x, (1, n // HEAD))


def _mla_fwd_body(q_ref, kn_ref, kr_ref, v_ref, o_ref, lse_ref, m_sc, l_sc, acc_sc, *, t, hp):
    qi = pl.program_id(1)
    m_sc[...] = jnp.full(m_sc.shape, NEG, F32)
    l_sc[...] = jnp.zeros(l_sc.shape, F32)
    acc_sc[...] = jnp.zeros(acc_sc.shape, F32)

    def step(j, masked):
        ks = pl.ds(pl.multiple_of(j * t, t), t)
        kr = kr_ref[ks, :]
        for hh in range(hp):
            kcat = jnp.concatenate([kn_ref[ks, hh * HEAD:(hh + 1) * HEAD], kr], axis=1)
            s = _dot_nt(q_ref[:, hh * QPAD:(hh + 1) * QPAD], kcat) * MLA_SCALE
            if masked:
                s = jnp.where(_tri(t), s, NEG)
            m_prev = m_sc[hh]
            m_new = jnp.maximum(m_prev, jnp.max(s, axis=1, keepdims=True))
            alpha = jnp.exp(m_prev - m_new)
            p = jnp.exp(s - _lanes(m_new, t))
            l_sc[hh] = alpha * l_sc[hh] + jnp.sum(p, axis=1, keepdims=True)
            acc_sc[hh] = alpha * acc_sc[hh] + _dot_nn(p.astype(MXU_DTYPE), v_ref[ks, hh * HEAD:(hh + 1) * HEAD])
            m_sc[hh] = m_new

    def off_diag(j, carry):
        step(j, False)
        return carry

    lax.fori_loop(0, qi, off_diag, 0)
    step(qi, True)
    for hh in range(hp):
        l = l_sc[hh]
        o_ref[:, hh * HEAD:(hh + 1) * HEAD] = (acc_sc[hh] / l).astype(o_ref.dtype)
        lse_ref[:, hh * HEAD:(hh + 1) * HEAD] = m_sc[hh] + jnp.log(l)


def _mla_fwd(qf, kv, kr):
    T = qf.shape[0]
    t, hp = min(MLA_T, T), MLA_HP
    ng = NH // hp
    return _pcall(
        functools.partial(_mla_fwd_body, t=t, hp=hp), name="mla_fwd",
        grid=(ng, T // t),
        in_specs=[pl.BlockSpec((t, hp * QPAD), lambda g, i: (i, g)),
                  pl.BlockSpec((T, hp * HEAD), lambda g, i: (0, g)),
                  pl.BlockSpec((T, HEAD), lambda g, i: (0, 0)),
                  pl.BlockSpec((T, hp * HEAD), lambda g, i: (0, ng + g))],
        out_specs=[pl.BlockSpec((t, hp * HEAD), lambda g, i: (i, g))] * 2,
        out_shape=[jax.ShapeDtypeStruct((T, A_W), MXU_DTYPE), jax.ShapeDtypeStruct((T, A_W), F32)],
        scratch_shapes=[pltpu.VMEM((hp, t, HEAD), F32)] * 3,
        compiler_params=pltpu.CompilerParams(dimension_semantics=("parallel", "parallel")),
    )(qf, kv, kr, kv)


def _mla_bwd_body(q_ref, kn_ref, kr_ref, v_ref, do_ref, o_ref, lse_ref, dq_ref, dkn_ref, dv_ref, dkr_ref,
                  dl_sc, dk_sc, dv_sc, *, t):
    ki = pl.program_id(1)
    nq = q_ref.shape[0] // t

    @pl.when(ki == 0)
    def _():
        def prep(i, carry):
            rs = pl.ds(pl.multiple_of(i * t, t), t)
            dl = jnp.sum(do_ref[rs, :] * o_ref[rs, :].astype(F32), axis=1, keepdims=True)
            dl_sc[rs, :] = jnp.broadcast_to(dl, (t, HEAD))
            dq_ref[rs, :] = jnp.zeros((t, QPAD), F32)
            return carry
        lax.fori_loop(0, nq, prep, 0)

    kcat = jnp.concatenate([kn_ref[...], kr_ref[...]], axis=1)
    v = v_ref[...]
    dk_sc[...] = jnp.zeros(dk_sc.shape, F32)
    dv_sc[...] = jnp.zeros(dv_sc.shape, F32)

    def step(i, masked):
        qs = pl.ds(pl.multiple_of(i * t, t), t)
        q = q_ref[qs, :]
        do = do_ref[qs, :].astype(MXU_DTYPE)
        p = jnp.exp(_dot_nt(q, kcat) * MLA_SCALE - _lanes(lse_ref[qs, :], t))
        if masked:
            p = jnp.where(_tri(t), p, 0.0)
        ds = (p * (_dot_nt(do, v) - _lanes(dl_sc[qs, :], t))).astype(MXU_DTYPE)
        dv_sc[...] += _dot_tn(p.astype(MXU_DTYPE), do)
        dk_sc[...] += _dot_tn(ds, q)
        dq_ref[qs, :] += _dot_nn(ds, kcat) * MLA_SCALE

    step(ki, True)

    def off_diag(i, carry):
        step(i, False)
        return carry

    lax.fori_loop(ki + 1, nq, off_diag, 0)
    dk = dk_sc[...] * MLA_SCALE
    dkn_ref[...] = dk[:, 0:HEAD].astype(dkn_ref.dtype)
    dkr_ref[...] = dk[:, HEAD:QPAD]
    dv_ref[...] = dv_sc[...].astype(dv_ref.dtype)


def _mla_bwd(qf, kv, kr, dmix, mixed, lse):
    T = qf.shape[0]
    t = min(MLA_T, T)
    head = lambda h, j: (0, h)
    b_half = lambda h, j: (0, NH + h)
    kblk = pl.BlockSpec((t, HEAD), lambda h, j: (j, h))
    return _pcall(
        functools.partial(_mla_bwd_body, t=t), name="mla_bwd",
        grid=(NH, T // t),
        in_specs=[pl.BlockSpec((T, QPAD), head), kblk,
                  pl.BlockSpec((t, HEAD), lambda h, j: (j, 0)),
                  pl.BlockSpec((t, HEAD), lambda h, j: (j, NH + h)),
                  pl.BlockSpec((T, HEAD), b_half), pl.BlockSpec((T, HEAD), b_half),
                  pl.BlockSpec((T, HEAD), head)],
        out_specs=[pl.BlockSpec((T, QPAD), head), kblk, kblk, kblk],
        out_shape=[jax.ShapeDtypeStruct((T, NH * QPAD), F32), jax.ShapeDtypeStruct((T, A_W), MXU_DTYPE),
                   jax.ShapeDtypeStruct((T, A_W), MXU_DTYPE), jax.ShapeDtypeStruct((T, A_W), F32)],
        scratch_shapes=[pltpu.VMEM((T, HEAD), F32), pltpu.VMEM((t, QPAD), F32), pltpu.VMEM((t, HEAD), F32)],
        compiler_params=pltpu.CompilerParams(dimension_semantics=("parallel", "arbitrary")),
    )(qf, kv, kr, kv, dmix, mixed, lse)


def _local_step(x, pos, target, g1, g2, gq, gkv, g3, g4, w_main, w_kr, w_uq_p, w_ukv_p, w_out,
                mlp_weights, mlp_grads_ready):
    T = x.shape[0]
    TR = 256
    mm = functools.partial(_matmul, tm=512, tn=1024, tk=2048)

    inv_a = ROPE_THETA ** (-jnp.arange(0, ROT_A, 2, dtype=F32) / ROT_A)
    inv_b = ROPE_THETA ** (-jnp.arange(0, ROPE_B, 2, dtype=F32) / ROPE_B)
    inv = jnp.stack([jnp.concatenate([inv_a, inv_a, jnp.zeros((HEAD - ROT_A,), F32)]),
                     jnp.concatenate([inv_b, inv_b, jnp.zeros((HEAD - ROPE_B,), F32)])])
    inv = jnp.concatenate([inv, jnp.zeros((6, HEAD), F32)], axis=0)
    tabs = _rowwise(_rope_tab_body, [pos], [inv], [(HEAD, F32)] * 6, [], tr=512, name="rope_tables")

    (h,) = _rowwise(_rms_fwd_body, [x], [g1], [(D_MODEL, MXU_DTYPE)], [], tr=TR, name="rms_in")
    (proj,) = mm(h, w_main, dims="nn", out_dtypes=[F32], name="proj_main")
    (kr_raw,) = mm(h, w_kr, dims="nn", out_dtypes=[F32], name="proj_kr")
    q, k, v, cqn, ckvn, krope = _rowwise(
        _postproj_body, [proj, kr_raw] + tabs, [gq, gkv],
        [(A_W, F32)] * 3 + [(LORA, MXU_DTYPE)] * 2 + [(HEAD, MXU_DTYPE)], [], tr=TR, name="post_proj")
    a_out, lse_a = _dil_fwd(q, k, v)

    (q_pad,) = mm(cqn, w_uq_p, dims="nn", out_dtypes=[F32], name="q_up")
    (qf,) = _rowwise(_qrope_body, [q_pad] + tabs[3:], [], [(NH * QPAD, MXU_DTYPE)], [], tr=TR, name="q_rope")
    (kv,) = mm(ckvn, w_ukv_p, dims="nn", out_dtypes=[MXU_DTYPE], name="kv_up")
    b_out, lse_b = _mla_fwd(qf, kv, krope)

    mixed = jnp.concatenate([a_out, b_out], axis=1)
    (o,) = mm(mixed, w_out, dims="nn", out_dtypes=[F32], name="out_proj")
    x1, h2 = _rowwise(_mid_body, [x, o], [g2, g3], [(D_MODEL, F32), (D_MODEL, MXU_DTYPE)], [], tr=TR, name="mid_norm")

    w_up, w_down = mlp_weights(h2)

    def up_epi(acc):
        r = jnp.maximum(acc, 0.0)
        return r * r, r
    u, r = mm(h2, w_up, dims="nn", out_dtypes=[MXU_DTYPE, MXU_DTYPE], name="mlp_up", epi=up_epi)
    (dn,) = mm(u, w_down, dims="nn", out_dtypes=[F32], name="mlp_down")
    dy, dd, loss8, dg4 = _rowwise(_loss_body, [x1, dn, target], [g4], [(D_MODEL, F32), (D_MODEL, MXU_DTYPE)],
                                  [(8, HEAD), (8, D_MODEL)], tr=TR, name="loss_head")

    def dup_epi(acc, rr):
        return (acc * (2.0 * rr.astype(F32)),)
    (dup,) = mm(dd, w_down, dims="nt", out_dtypes=[MXU_DTYPE], name="d_up", epi=dup_epi, extras=(r,))
    (gw_down,) = mm(u, dd, dims="tn", out_dtypes=[WIRE_DTYPE], name="gw_down")
    (dh2,) = mm(dup, w_up, dims="nt", out_dtypes=[F32], name="d_h2")
    (gw_up,) = mm(h2, dup, dims="tn", out_dtypes=[WIRE_DTYPE], name="gw_up")
    g2 = g2 + mlp_grads_ready(gw_up, gw_down)
    dx1, do, dg3, dg2 = _rowwise(_bmid_body, [dy, dh2, x1, o], [g2, g3], [(D_MODEL, F32), (D_MODEL, MXU_DTYPE)],
                                 [(8, D_MODEL), (8, D_MODEL)], tr=TR, name="bwd_mid")
    (dmix,) = mm(do, w_out, dims="nt", out_dtypes=[F32], name="d_mixed")
    (gw_out,) = mm(mixed, do, dims="tn", out_dtypes=[WIRE_DTYPE], name="gw_out")

    dqf, dkn, dvb, dkr = _mla_bwd(qf, kv, krope, dmix, mixed, lse_b)
    (dq_pad,) = _rowwise(_qrope_t_body, [dqf] + tabs[3:], [], [(NH * QPAD, MXU_DTYPE)], [], tr=TR, name="q_rope_t")
    (dcqn,) = mm(dq_pad, w_uq_p, dims="nt", out_dtypes=[F32], name="d_cq")
    (gw_uq_p,) = mm(cqn, dq_pad, dims="tn", out_dtypes=[WIRE_DTYPE], name="gw_uq")
    dkv = jnp.concatenate([dkn, dvb], axis=1)
    (dckvn,) = mm(dkv, w_ukv_p, dims="nt", out_dtypes=[F32], name="d_ckv")
    (gw_ukv_p,) = mm(ckvn, dkv, dims="tn", out_dtypes=[WIRE_DTYPE], name="gw_ukv")

    dq_a, dk_a, dv_a = _dil_bwd(q, k, v, dmix, mixed, lse_a)
    dproj, dkrp, dgq, dgkv = _rowwise(
        _dproj_body, [dq_a, dk_a, dv_a, dcqn, dckvn, proj, dkr] + tabs, [gq, gkv],
        [(MAIN_COLS, MXU_DTYPE), (HEAD, MXU_DTYPE)], [(8, LORA), (8, LORA)], tr=TR, name="d_proj")
    (dha,) = mm(dproj, w_main, dims="nt", out_dtypes=[F32], name="d_h_main")
    (dhb,) = mm(dkrp, w_kr, dims="nt", out_dtypes=[F32], name="d_h_kr")
    (gw_main,) = mm(h, dproj, dims="tn", out_dtypes=[WIRE_DTYPE], name="gw_in_main")
    (gw_kr,) = mm(h, dkrp, dims="tn", out_dtypes=[WIRE_DTYPE], name="gw_in_kr")
    dx, dg1 = _rowwise(_bin_body, [dx1, dha, dhb, x], [g1], [(D_MODEL, F32)], [(8, D_MODEL)], tr=TR, name="bwd_in")

    small = jnp.concatenate([dg1, dg2, dgq, dgkv, dg3, dg4, loss8], axis=1)
    return dx, (gw_main, gw_kr, gw_uq_p, gw_ukv_p, gw_out), small


def _place():
    x, y, c = lax.axis_index("x"), lax.axis_index("y"), lax.axis_index("c")
    chips = [(1 - x, y), (x, 1 - y), (1 - x, 1 - y)]
    return x, y, c, chips


def _cast_place_body(me_ref, w_ref, o_ref):
    o_ref[...] = w_ref[...].astype(o_ref.dtype)


def _cast_place(me_arr, w, name):
    rows, cols = w.shape
    tr = min(rows, 256)
    grid_spec = pltpu.PrefetchScalarGridSpec(
        num_scalar_prefetch=1, grid=(rows // tr,),
        in_specs=[pl.BlockSpec((tr, cols), lambda i, me: (i, 0))],
        out_specs=pl.BlockSpec((None, tr, cols), lambda i, me: (me[0], i, 0)))
    return _pcall(
        _cast_place_body, name=name, grid_spec=grid_spec,
        out_shape=jax.ShapeDtypeStruct((N_CHIPS, rows, cols), WIRE_DTYPE),
        compiler_params=pltpu.CompilerParams(dimension_semantics=("parallel",)),
    )(me_arr, w)


def _ag_body(*refs, n_w):
    bufs = refs[n_w:2 * n_w]
    send_sems, recv_sems, fsend_sems, frecv_sems = refs[2 * n_w:]
    x, y, c, chips = _place()
    me = 2 * x + y
    sib = (x, y, 1 - c)

    def half_rows(w, which):
        half = bufs[w].shape[1] // 2
        return pl.ds(pl.multiple_of(which * half, 16), half)

    sends, fwds = [], []
    for w in range(n_w):
        mine = bufs[w].at[me, half_rows(w, c)]
        for j, (px, py) in enumerate(chips):
            cp = pltpu.make_async_remote_copy(
                src_ref=mine, dst_ref=mine,
                send_sem=send_sems.at[w * 3 + j], recv_sem=recv_sems.at[w * 3 + j],
                device_id=(px, py, c), device_id_type=MESH)
            cp.start()
            sends.append(cp)
    for w in range(n_w):
        for j, (px, py) in enumerate(chips):
            landed = bufs[w].at[2 * px + py, half_rows(w, c)]
            pltpu.make_async_remote_copy(
                src_ref=landed, dst_ref=landed,
                send_sem=send_sems.at[w * 3 + j], recv_sem=recv_sems.at[w * 3 + j],
                device_id=(px, py, c), device_id_type=MESH).wait_recv()
            fw = pltpu.make_async_remote_copy(
                src_ref=landed, dst_ref=landed,
                send_sem=fsend_sems.at[w * 3 + j], recv_sem=frecv_sems.at[w * 3 + j],
                device_id=sib, device_id_type=MESH)
            fw.start()
            fwds.append(fw)
    for w in range(n_w):
        for j, (px, py) in enumerate(chips):
            passed = bufs[w].at[2 * px + py, half_rows(w, 1 - c)]
            pltpu.make_async_remote_copy(
                src_ref=passed, dst_ref=passed,
                send_sem=fsend_sems.at[w * 3 + j], recv_sem=frecv_sems.at[w * 3 + j],
                device_id=sib, device_id_type=MESH).wait_recv()
    for cp in sends + fwds:
        cp.wait_send()


def _allgather_weights(placed):
    n_w = len(placed)
    return _pcall(
        functools.partial(_ag_body, n_w=n_w), name="weight_allgather",
        in_specs=[ANY] * n_w, out_specs=[ANY] * n_w,
        out_shape=[jax.ShapeDtypeStruct(p.shape, p.dtype) for p in placed],
        input_output_aliases={w: w for w in range(n_w)},
        scratch_shapes=[pltpu.SemaphoreType.DMA((3 * n_w,))] * 4,
    )(*placed)


HBM = pl.BlockSpec(memory_space=pltpu.HBM)
SEM = pl.BlockSpec(memory_space=pltpu.SEMAPHORE)
EFFECT = pltpu.SideEffectType.DATAFLOW_SIDE_EFFECTING


def _in_hbm(a):
    return pltpu.with_memory_space_constraint(a, pltpu.HBM)


def _ag_descs(bufs, send_sems, recv_sems):
    x, y, c, chips = _place()
    me = 2 * x + y
    out = []
    for w, buf in enumerate(bufs):
        half = buf.shape[1] // 2
        rows = pl.ds(pl.multiple_of(c * half, 16), half)
        mine = buf.at[me, rows]
        for j, (px, py) in enumerate(chips):
            landed = buf.at[2 * px + py, rows]
            mk = lambda ref, w=w, j=j, px=px, py=py: pltpu.make_async_remote_copy(
                src_ref=ref, dst_ref=ref, send_sem=send_sems.at[w * 3 + j], recv_sem=recv_sems.at[w * 3 + j],
                device_id=(px, py, c), device_id_type=MESH)
            out.append((mk(mine), mk(landed)))
    return out


def _ag_start_body(*refs, n_w):
    bufs = refs[:n_w]
    send_sems, recv_sems = refs[n_w + 1], refs[n_w + 2]
    token = refs[-1]
    for send, _ in _ag_descs(bufs, send_sems, recv_sems):
        send.start()
    token[...] = jnp.zeros_like(token)


def _ag_start(placed, after):
    n_w = len(placed)
    res = _pcall(
        functools.partial(_ag_start_body, n_w=n_w), name="weight_allgather_start",
        in_specs=[HBM] * n_w + [ANY],
        out_specs=[SEM, SEM] + [HBM] * n_w + [pl.BlockSpec(memory_space=pltpu.VMEM)],
        out_shape=[pltpu.SemaphoreType.DMA((3 * n_w,)), pltpu.SemaphoreType.DMA((3 * n_w,))]
        + [pltpu.HBM(p.shape, p.dtype) for p in placed] + [jax.ShapeDtypeStruct((8, HEAD), F32)],
        input_output_aliases={w: 2 + w for w in range(n_w)},
        compiler_params=pltpu.CompilerParams(has_side_effects=EFFECT),
    )(*[_in_hbm(p) for p in placed], after)
    return res[0], res[1], list(res[2:2 + n_w]), res[-1]


def _ag_wait_body(*refs, n_w):
    bufs = refs[:n_w]
    send_sems, recv_sems = refs[n_w], refs[n_w + 1]
    for send, recv in _ag_descs(bufs, send_sems, recv_sems):
        send.wait_send()
        recv.wait_recv()


def _ag_wait(send_sems, recv_sems, bufs, after):
    n_w = len(bufs)
    return list(_pcall(
        functools.partial(_ag_wait_body, n_w=n_w), name="weight_allgather_wait",
        in_specs=[HBM] * n_w + [SEM, SEM, ANY], out_specs=[HBM] * n_w,
        out_shape=[pltpu.HBM(b.shape, b.dtype) for b in bufs],
        input_output_aliases={w: w for w in range(n_w)},
        compiler_params=pltpu.CompilerParams(has_side_effects=EFFECT),
    )(*bufs, send_sems, recv_sems, after))


def _ag_forward_body(*refs, n_w):
    bufs = refs[n_w:2 * n_w]
    send_sems, recv_sems = refs[2 * n_w:]
    x, y, c, chips = _place()
    fwds = []
    for w, buf in enumerate(bufs):
        half = buf.shape[1] // 2
        for j, (px, py) in enumerate(chips):
            def piece(which, buf=buf, half=half, px=px, py=py):
                return buf.at[2 * px + py, pl.ds(pl.multiple_of(which * half, 16), half)]
            mk = lambda ref, w=w, j=j: pltpu.make_async_remote_copy(
                src_ref=ref, dst_ref=ref, send_sem=send_sems.at[w * 3 + j], recv_sem=recv_sems.at[w * 3 + j],
                device_id=(x, y, 1 - c), device_id_type=MESH)
            fw = mk(piece(c))
            fw.start()
            fwds.append((fw, mk(piece(1 - c))))
    for fw, back in fwds:
        back.wait_recv()
        fw.wait_send()


def _ag_forward(bufs):
    n_w = len(bufs)
    return list(_pcall(
        functools.partial(_ag_forward_body, n_w=n_w), name="weight_allgather_forward",
        in_specs=[ANY] * n_w, out_specs=[ANY] * n_w,
        out_shape=[jax.ShapeDtypeStruct(b.shape, b.dtype) for b in bufs],
        input_output_aliases={w: w for w in range(n_w)},
        scratch_shapes=[pltpu.SemaphoreType.DMA((3 * n_w,))] * 2,
    )(*bufs))


def _sc_descs(ins, outs, send_sems, recv_sems):
    x, y, c, chips = _place()
    me = 2 * x + y
    out = []
    for w in range(len(ins)):
        for j, (px, py) in enumerate(chips):
            out.append(pltpu.make_async_remote_copy(
                src_ref=ins[w].at[2 * px + py], dst_ref=outs[w].at[me],
                send_sem=send_sems.at[w * 3 + j], recv_sem=recv_sems.at[w * 3 + j],
                device_id=(px, py, c), device_id_type=MESH))
    return out


def _scatter_start_body(*refs, n_w):
    ins, lands = refs[:n_w], refs[n_w:2 * n_w]
    send_sems, recv_sems = refs[2 * n_w + 1], refs[2 * n_w + 2]
    token = refs[-1]
    for cp in _sc_descs(ins, lands, send_sems, recv_sems):
        cp.start()
    token[...] = jnp.zeros_like(token)


def _scatter_start(parts, after):
    n_w = len(parts)
    lands = [lax.empty(p.shape, p.dtype) for p in parts]
    res = _pcall(
        functools.partial(_scatter_start_body, n_w=n_w), name="grad_scatter_start",
        in_specs=[HBM] * (2 * n_w) + [ANY],
        out_specs=[SEM, SEM] + [HBM] * (2 * n_w) + [pl.BlockSpec(memory_space=pltpu.VMEM)],
        out_shape=[pltpu.SemaphoreType.DMA((3 * n_w,)), pltpu.SemaphoreType.DMA((3 * n_w,))]
        + [pltpu.HBM(p.shape, p.dtype) for p in parts] * 2 + [jax.ShapeDtypeStruct((8, HEAD), F32)],
        input_output_aliases={i: 2 + i for i in range(2 * n_w)},
        compiler_params=pltpu.CompilerParams(has_side_effects=EFFECT),
    )(*[_in_hbm(p) for p in parts], *[_in_hbm(l) for l in lands], after)
    return res[0], res[1], list(res[2:2 + n_w]), list(res[2 + n_w:2 + 2 * n_w]), res[-1]


def _scatter_wait_body(*refs, n_w):
    ins, lands = refs[:n_w], refs[n_w:2 * n_w]
    send_sems, recv_sems = refs[2 * n_w], refs[2 * n_w + 1]
    for cp in _sc_descs(ins, lands, send_sems, recv_sems):
        cp.wait_send()
        cp.wait_recv()


def _scatter_wait(send_sems, recv_sems, parts, lands, after):
    n_w = len(parts)
    res = _pcall(
        functools.partial(_scatter_wait_body, n_w=n_w), name="grad_scatter_wait",
        in_specs=[HBM] * (2 * n_w) + [SEM, SEM, ANY], out_specs=[HBM] * (2 * n_w),
        out_shape=[pltpu.HBM(p.shape, p.dtype) for p in parts] * 2,
        input_output_aliases={i: i for i in range(2 * n_w)},
        compiler_params=pltpu.CompilerParams(has_side_effects=EFFECT),
    )(*parts, *lands, send_sems, recv_sems, after)
    return list(res[:n_w]), list(res[n_w:])


def _pair_send_body(*refs, n_w):
    ins, outs = refs[:n_w], refs[n_w:2 * n_w]
    send_sems, recv_sems = refs[2 * n_w:]
    x, y, c, _ = _place()
    cps = []
    for w in range(n_w):
        cp = pltpu.make_async_remote_copy(
            src_ref=ins[w].at[:, 1 - c], dst_ref=outs[w],
            send_sem=send_sems.at[w], recv_sem=recv_sems.at[w],
            device_id=(x, y, 1 - c), device_id_type=MESH)
        cp.start()
        cps.append(cp)
    for cp in cps:
        cp.wait()


def _pair_send(grads4, tag):
    n_w = len(grads4)
    return _pcall(
        functools.partial(_pair_send_body, n_w=n_w), name="grad_pair_exchange_" + tag,
        in_specs=[ANY] * n_w, out_specs=[ANY] * n_w,
        out_shape=[jax.ShapeDtypeStruct((g.shape[0],) + g.shape[2:], g.dtype) for g in grads4],
        scratch_shapes=[pltpu.SemaphoreType.DMA((n_w,))] * 2,
    )(*grads4)


def _pair_add_body(c_ref, mine_ref, theirs_ref, o_ref):
    o_ref[...] = (mine_ref[...].astype(F32) + theirs_ref[...].astype(F32)).astype(o_ref.dtype)


def _pair_add(c_arr, g4, recv, name):
    _, _, hr, cols = g4.shape
    tr = min(hr, 256)
    grid_spec = pltpu.PrefetchScalarGridSpec(
        num_scalar_prefetch=1, grid=(N_CHIPS, hr // tr),
        in_specs=[pl.BlockSpec((None, None, tr, cols), lambda s, i, c: (s, c[0], i, 0)),
                  pl.BlockSpec((None, tr, cols), lambda s, i, c: (s, i, 0))],
        out_specs=pl.BlockSpec((None, tr, cols), lambda s, i, c: (s, i, 0)))
    return _pcall(
        _pair_add_body, name=name, grid_spec=grid_spec,
        out_shape=jax.ShapeDtypeStruct(recv.shape, recv.dtype),
        compiler_params=pltpu.CompilerParams(dimension_semantics=("parallel", "parallel")),
    )(c_arr, g4, recv)


def _scatter_body(*refs, n_w):
    ins, outs = refs[:n_w], refs[n_w:2 * n_w]
    send_sems, recv_sems = refs[2 * n_w:]
    x, y, c, chips = _place()
    me = 2 * x + y
    todo = []
    for w in range(n_w):
        for j, (px, py) in enumerate(chips):
            cp = pltpu.make_async_remote_copy(
                src_ref=ins[w].at[2 * px + py], dst_ref=outs[w].at[me],
                send_sem=send_sems.at[w * 3 + j], recv_sem=recv_sems.at[w * 3 + j],
                device_id=(px, py, c), device_id_type=MESH)
            cp.start()
            todo.append(cp)
    for t in todo:
        t.wait()


def _scatter(parts):
    n_w = len(parts)
    return _pcall(
        functools.partial(_scatter_body, n_w=n_w), name="grad_scatter",
        in_specs=[ANY] * n_w, out_specs=[ANY] * n_w,
        out_shape=[jax.ShapeDtypeStruct(p.shape, p.dtype) for p in parts],
        scratch_shapes=[pltpu.SemaphoreType.DMA((3 * n_w,))] * 2,
    )(*parts)


def _sum4_body(me_ref, p_ref, l0, l1, l2, l3, o_ref):
    me = me_ref[0]
    t = [jnp.where(me == j, p_ref[...], l[...]).astype(F32) for j, l in enumerate((l0, l1, l2, l3))]
    o_ref[...] = ((t[0] + t[1]) + t[2]) + t[3]


def _sum4(me_arr, part, landed, name):
    _, hr, cols = part.shape
    tr = min(hr, 256)

    def slot(j):
        return lambda i, me: (jnp.where(me[0] == j, (j + 1) % N_CHIPS, j), i, 0)

    grid_spec = pltpu.PrefetchScalarGridSpec(
        num_scalar_prefetch=1, grid=(hr // tr,),
        in_specs=[pl.BlockSpec((None, tr, cols), lambda i, me: (me[0], i, 0))]
        + [pl.BlockSpec((None, tr, cols), slot(j)) for j in range(N_CHIPS)],
        out_specs=pl.BlockSpec((tr, cols), lambda i, me: (i, 0)))
    return _pcall(
        _sum4_body, name=name, grid_spec=grid_spec,
        out_shape=jax.ShapeDtypeStruct((hr, cols), F32),
        compiler_params=pltpu.CompilerParams(dimension_semantics=("parallel",)),
    )(me_arr, part, landed, landed, landed, landed)


def _pair_swap_body(*refs, n_w):
    ins, outs = refs[:n_w], refs[n_w:2 * n_w]
    send_sems, recv_sems = refs[2 * n_w:]
    x, y, c, _ = _place()
    todo = []
    for w in range(n_w):
        cp = pltpu.make_async_remote_copy(
            src_ref=ins[w], dst_ref=outs[w],
            send_sem=send_sems.at[w], recv_sem=recv_sems.at[w],
            device_id=(x, y, 1 - c), device_id_type=MESH)
        cp.start()
        todo.append(cp)
    for t in todo:
        t.wait()


def _pair_swap(halves):
    n_w = len(halves)
    return _pcall(
        functools.partial(_pair_swap_body, n_w=n_w), name="grad_pair_swap",
        in_specs=[ANY] * n_w, out_specs=[ANY] * n_w,
        out_shape=[jax.ShapeDtypeStruct(h.shape, h.dtype) for h in halves],
        scratch_shapes=[pltpu.SemaphoreType.DMA((n_w,))] * 2,
    )(*halves)


def _small_gather_body(x_ref, out_ref, send_sems, recv_sems, local_sem):
    m_per = x_ref.shape[0]
    x, y, c, chips = _place()
    me, sibling = (x, y, c), (x, y, 1 - c)

    def rows(px, py, pc):
        return out_ref.at[pl.ds((4 * px + 2 * py + pc) * m_per, m_per), :]

    def copy(k, block, to, src=None):
        return pltpu.make_async_remote_copy(
            src_ref=rows(*block) if src is None else src, dst_ref=rows(*block),
            send_sem=send_sems.at[k], recv_sem=recv_sems.at[k], device_id=to, device_id_type=MESH)

    mine = pltpu.make_async_copy(x_ref, rows(*me), local_sem)
    mine.start()
    first = [copy(0, me, sibling, src=x_ref)]
    first += [copy(1 + j, me, (*chip, c), src=x_ref) for j, chip in enumerate(chips)]
    for cp in first:
        cp.start()
    passed = [copy(4 + j, (*chip, c), sibling) for j, chip in enumerate(chips)]
    for j, chip in enumerate(chips):
        copy(1 + j, (*chip, c), me).wait_recv()
        passed[j].start()
    copy(0, sibling, me).wait_recv()
    for j, chip in enumerate(chips):
        copy(4 + j, (*chip, 1 - c), me).wait_recv()
    for cp in first + passed:
        cp.wait_send()
    mine.wait()


def _small_gather(small):
    m_per, n = small.shape
    return _pcall(
        _small_gather_body, name="small_allgather",
        out_shape=jax.ShapeDtypeStruct((N_DEV * m_per, n), small.dtype),
        in_specs=[pl.BlockSpec(memory_space=pltpu.VMEM)],
        out_specs=pl.BlockSpec(memory_space=pltpu.VMEM),
        scratch_shapes=[pltpu.SemaphoreType.DMA((7,)), pltpu.SemaphoreType.DMA((7,)), pltpu.SemaphoreType.DMA],
    )(small)


def _adamw(w, g, m, v):
    m = ADAM_B1 * m + (1.0 - ADAM_B1) * g
    v = ADAM_B2 * v + (1.0 - ADAM_B2) * (g * g)
    m_hat = m / (1.0 - ADAM_B1 ** ADAM_STEP)
    v_hat = v / (1.0 - ADAM_B2 ** ADAM_STEP)
    delta = -ADAM_LR * (m_hat / (jnp.sqrt(v_hat) + ADAM_EPS) + ADAM_WD * w)
    return delta, m, v


def _adamw_body(c_ref, w_ref, own_ref, sib_ref, m_ref, v_ref, g_ref, d_ref, nm_ref, nv_ref, *, nh):
    mine = (pl.program_id(0) // nh) == c_ref[0]
    g = jnp.where(mine, own_ref[...], sib_ref[...])
    g_ref[...] = g
    d, m, v = _adamw(w_ref[...], g, m_ref[...], v_ref[...])
    d_ref[...] = d
    nm_ref[...] = m
    nv_ref[...] = v


def _adamw_call(c_arr, w, own, sib, m, v, name):
    rows, cols = w.shape
    tr = min(rows // 2, 256)
    nh = (rows // 2) // tr
    full = pl.BlockSpec((tr, cols), lambda i, c: (i, 0))
    half = pl.BlockSpec((tr, cols), lambda i, c: (i % nh, 0))
    grid_spec = pltpu.PrefetchScalarGridSpec(
        num_scalar_prefetch=1, grid=(rows // tr,),
        in_specs=[full, half, half, full, full], out_specs=[full] * 4)
    return _pcall(
        functools.partial(_adamw_body, nh=nh), name=name, grid_spec=grid_spec,
        out_shape=[jax.ShapeDtypeStruct(w.shape, F32)] * 4,
        compiler_params=pltpu.CompilerParams(dimension_semantics=("parallel",)),
    )(c_arr, w, own, sib, m, v)


def _small_update_body(gath_ref, w_ref, m_ref, v_ref, g_ref, d_ref, nm_ref, nv_ref, loss_ref, *, n_gain):
    tot = gath_ref[0:1, :]
    for i in range(1, gath_ref.shape[0]):
        tot = tot + gath_ref[i:i + 1, :]
    g = tot[:, 0:n_gain]
    g_ref[...] = g
    d, m, v = _adamw(w_ref[...], g, m_ref[...], v_ref[...])
    d_ref[...] = d
    nm_ref[...] = m
    nv_ref[...] = v
    loss_ref[...] = (0.5 / D_MODEL) * jnp.sum(tot[:, n_gain:n_gain + HEAD], axis=1, keepdims=True) * jnp.ones((1, HEAD), F32)


def _small_update(gath, w, m, v):
    n_gain = w.shape[1]
    vm = pl.BlockSpec(memory_space=pltpu.VMEM)
    return _pcall(
        functools.partial(_small_update_body, n_gain=n_gain), name="gain_update",
        in_specs=[vm] * 4, out_specs=[vm] * 5,
        out_shape=[jax.ShapeDtypeStruct((1, n_gain), F32)] * 4 + [jax.ShapeDtypeStruct((1, HEAD), F32)],
    )(gath, w, m, v)


def kernel(x, positions, norm_attn_pre, norm_attn_post, w_in, q_latent_norm, kv_latent_norm, w_uq, w_ukv, w_out, norm_mlp_pre, norm_mlp_post, w_up, w_down, loss_target, m_norm_attn_pre, m_norm_attn_post, m_w_in, m_q_latent_norm, m_kv_latent_norm, m_w_uq, m_w_ukv, m_w_out, m_norm_mlp_pre, m_norm_mlp_post, m_w_up, m_w_down, v_norm_attn_pre, v_norm_attn_post, v_w_in, v_q_latent_norm, v_kv_latent_norm, v_w_uq, v_w_ukv, v_w_out, v_norm_mlp_pre, v_norm_mlp_post, v_w_up, v_w_down):
    T = x.shape[1]
    c_arr = lax.axis_index("c").astype(jnp.int32).reshape(1)
    me_arr = (2 * lax.axis_index("x") + lax.axis_index("y")).astype(jnp.int32).reshape(1)
    names = ["w_in", "w_uq", "w_ukv", "w_out", "w_up", "w_down"]

    mats = [w_in[0], w_uq[0], w_ukv[0], w_out[0], w_up[0], w_down[0]]
    placed = [_cast_place(me_arr, w, "cast_" + n) for w, n in zip(mats, names)]
    win_g, wuq_g, wukv_g, wout_g = _allgather_weights(placed[:4])
    mlp_send, mlp_recv, mlp_bufs, started = _ag_start(placed[4:], win_g)

    col_major = lambda g: jnp.transpose(g, (1, 0, 2)).reshape(g.shape[1], N_CHIPS * g.shape[2])
    win_full = col_major(win_g)
    w_main = win_full[:, :MAIN_COLS]
    w_kr = jnp.pad(win_full[:, MAIN_COLS:], ((0, 0), (0, HEAD - ROPE_B)))
    wuq_full = col_major(wuq_g).reshape(LORA, NH, HEAD + ROPE_B)
    w_uq_p = jnp.pad(wuq_full, ((0, 0), (0, 0), (0, QPAD - HEAD - ROPE_B))).reshape(LORA, NH * QPAD)
    w_ukv_p = col_major(wukv_g).reshape(LORA, NH, 2, HEAD).transpose(0, 2, 1, 3).reshape(LORA, 2 * A_W)
    w_out_f = wout_g.reshape(2 * A_W, D_MODEL)
    cast = lambda a: a.astype(MXU_DTYPE)
    to_shards = lambda g: jnp.transpose(g.reshape(g.shape[0], N_CHIPS, g.shape[1] // N_CHIPS), (1, 0, 2))
    halved = lambda g: g.reshape(N_CHIPS, 2, g.shape[1] // 2, g.shape[2])

    def pair_sum(full4, ns):
        from_sib = _pair_send(full4, "_".join(ns))
        return [_pair_add(c_arr, g4, r, "pair_add_" + n) for g4, r, n in zip(full4, from_sib, ns)]

    def mlp_weights(after):
        wup_g, wdown_g = _ag_forward(_ag_wait(mlp_send, mlp_recv, mlp_bufs, after))
        return cast(col_major(wup_g)), cast(wdown_g.reshape(D_FF, D_MODEL))

    mlp_scatter = []

    def mlp_grads_ready(gw_up, gw_down):
        parts = pair_sum([halved(to_shards(gw_up)), halved(gw_down.reshape(N_CHIPS, D_MODEL, D_MODEL))], names[4:])
        mlp_scatter.extend(_scatter_start(parts, started))
        return mlp_scatter[-1][0:1, 0:1]

    dx, gws, small = _local_step(
        x[0], positions[0].astype(F32).reshape(T, 1), loss_target[0],
        norm_attn_pre + started[0:1, 0:1], norm_attn_post, q_latent_norm, kv_latent_norm, norm_mlp_pre, norm_mlp_post,
        cast(w_main), cast(w_kr), cast(w_uq_p), cast(w_ukv_p), cast(w_out_f), mlp_weights, mlp_grads_ready)
    gw_main, gw_kr, gw_uq_p, gw_ukv_p, gw_out = gws

    gw_in = to_shards(jnp.concatenate([gw_main, gw_kr[:, :ROPE_B]], axis=1))
    gw_uq = to_shards(gw_uq_p.reshape(LORA, NH, QPAD)[:, :, :HEAD + ROPE_B].reshape(LORA, NH * (HEAD + ROPE_B)))
    gw_ukv = to_shards(gw_ukv_p.reshape(LORA, 2, NH, HEAD).transpose(0, 2, 1, 3).reshape(LORA, 2 * A_W))
    full4 = [halved(g) for g in (gw_in, gw_uq, gw_ukv, gw_out.reshape(N_CHIPS, LORA, D_MODEL))]

    parts_a = pair_sum(full4, names[:4])
    landed_a = list(_scatter(parts_a))
    s_send, s_recv, parts_b, lands_b, _ = mlp_scatter
    parts_b, landed_b = _scatter_wait(s_send, s_recv, parts_b, lands_b, landed_a[0])
    halves = [_sum4(me_arr, p, l, "chip_sum_" + n)
              for p, l, n in zip(parts_a + parts_b, landed_a + landed_b, names)]
    from_sib2 = _pair_swap(halves)

    ms = [m_w_in[0], m_w_uq[0], m_w_ukv[0], m_w_out[0], m_w_up[0], m_w_down[0]]
    vs = [v_w_in[0], v_w_uq[0], v_w_ukv[0], v_w_out[0], v_w_up[0], v_w_down[0]]
    upd = [_adamw_call(c_arr, w, own, sib, m, v, "adamw_" + n)
           for w, own, sib, m, v, n in zip(mats, halves, from_sib2, ms, vs, names)]
    grads = [u[0] for u in upd]

    gath = _small_gather(small)
    gains = [norm_attn_pre, norm_attn_post, q_latent_norm, kv_latent_norm, norm_mlp_pre, norm_mlp_post]
    gm = [m_norm_attn_pre, m_norm_attn_post, m_q_latent_norm, m_kv_latent_norm, m_norm_mlp_pre, m_norm_mlp_post]
    gv = [v_norm_attn_pre, v_norm_attn_post, v_q_latent_norm, v_kv_latent_norm, v_norm_mlp_pre, v_norm_mlp_post]
    cat = lambda xs: jnp.concatenate(xs, axis=1)
    g_s, d_s, m_s, v_s, loss_v = _small_update(gath, cat(gains), cat(gm), cat(gv))
    widths = [a.shape[1] for a in gains]
    offs = [sum(widths[:i]) for i in range(len(widths))]
    split = lambda a: [a[:, o:o + w] for o, w in zip(offs, widths)]
    g_gain, d_gain, m_gain, v_gain = split(g_s), split(d_s), split(m_s), split(v_s)

    def ordered(gain_list, mat_list):
        gl, ml = gain_list, [a[None] for a in mat_list]
        return [gl[0], gl[1], ml[0], gl[2], gl[3], ml[1], ml[2], ml[3], gl[4], gl[5], ml[4], ml[5]]

    loss = loss_v[0, 0]
    return (loss, dx[None],
            *ordered(g_gain, grads),
            *ordered(d_gain, [u[1] for u in upd]),
            *ordered(m_gain, [u[2] for u in upd]),
            *ordered(v_gain, [u[3] for u in upd]))
```

```python
import functools

import jax
import jax.numpy as jnp
from jax import lax
from jax.experimental import pallas as pl
from jax.experimental.pallas import tpu as pltpu

F32 = jnp.float32
BF16 = jnp.bfloat16
MXU_DTYPE = jnp.bfloat16
WIRE_DTYPE = jnp.bfloat16

D_MODEL = 2048
HEAD = 128
NH = 8
A_W = NH * HEAD
LORA = 512
ROPE_B = 64
QPAD = 256
MAIN_COLS = 3 * A_W + 2 * LORA
IN_COLS = MAIN_COLS + ROPE_B
D_FF = 4 * D_MODEL
DIL = (1, 4, 16)
ROT_A = 32
ROPE_THETA = 500000.0
EPS = 1e-6
NEG = -1e30
N_CHIPS = 4
N_DEV = 8

ADAM_LR = 0.001
ADAM_B1 = 0.9
ADAM_B2 = 0.999
ADAM_EPS = 1e-08
ADAM_WD = 0.01
ADAM_STEP = 10

MESH = pl.DeviceIdType.MESH
ANY = pl.BlockSpec(memory_space=pl.ANY)


def _pcall(body, **kw):
    return pl.pallas_call(body, **kw)


_DIMS = {
    "nn": (((1,), (0,)), ((), ())),
    "nt": (((1,), (1,)), ((), ())),
    "tn": (((0,), (0,)), ((), ())),
}


def _mm_body(*refs, dims, nk, epi, n_extra, n_out):
    a_ref, b_ref = refs[0], refs[1]
    extra = refs[2:2 + n_extra]
    outs = refs[2 + n_extra:2 + n_extra + n_out]
    part = lax.dot_general(a_ref[...], b_ref[...], _DIMS[dims], preferred_element_type=F32)

    def finish(acc):
        res = epi(acc, *[r[...] for r in extra]) if epi is not None else (acc,)
        for o_ref, o in zip(outs, res):
            o_ref[...] = o.astype(o_ref.dtype)

    if nk == 1:
        finish(part)
        return
    acc_ref = refs[-1]
    k = pl.program_id(2)

    @pl.when(k == 0)
    def _():
        acc_ref[...] = part

    @pl.when(k > 0)
    def _():
        acc_ref[...] += part

    @pl.when(k == nk - 1)
    def _():
        finish(acc_ref[...])


def _matmul(a, b, *, dims, out_dtypes, tm, tn, tk, name, epi=None, extras=(), b_outer=False):
    if dims == "nn":
        (M, K), (K2, N) = a.shape, b.shape
    elif dims == "nt":
        (M, K), (N, K2) = a.shape, b.shape
    else:
        (K, M), (K2, N) = a.shape, b.shape
    assert K == K2, (a.shape, b.shape, dims)
    tm, tn, tk = min(tm, M), min(tn, N), min(tk, K)
    assert M % tm == 0 and N % tn == 0 and K % tk == 0, (name, M, N, K, tm, tn, tk)
    nk = K // tk

    def at(f):
        if b_outer:
            return lambda j, i, k: f(i, j, k)
        return f

    a_spec = {"nn": pl.BlockSpec((tm, tk), at(lambda i, j, k: (i, k))),
              "nt": pl.BlockSpec((tm, tk), at(lambda i, j, k: (i, k))),
              "tn": pl.BlockSpec((tk, tm), at(lambda i, j, k: (k, i)))}[dims]
    b_spec = {"nn": pl.BlockSpec((tk, tn), at(lambda i, j, k: (k, j))),
              "nt": pl.BlockSpec((tn, tk), at(lambda i, j, k: (j, k))),
              "tn": pl.BlockSpec((tk, tn), at(lambda i, j, k: (k, j)))}[dims]
    o_spec = pl.BlockSpec((tm, tn), at(lambda i, j, k: (i, j)))
    body = functools.partial(_mm_body, dims=dims, nk=nk, epi=epi,
                             n_extra=len(extras), n_out=len(out_dtypes))
    res = _pcall(
        body, name=name,
        grid=(N // tn, M // tm, nk) if b_outer else (M // tm, N // tn, nk),
        in_specs=[a_spec, b_spec] + [o_spec] * len(extras),
        out_specs=[o_spec] * len(out_dtypes),
        out_shape=[jax.ShapeDtypeStruct((M, N), dt) for dt in out_dtypes],
        scratch_shapes=[pltpu.VMEM((tm, tn), F32)] if nk > 1 else [],
        compiler_params=pltpu.CompilerParams(
            dimension_semantics=("parallel", "parallel", "arbitrary")),
    )(a, b, *extras)
    return list(res)


def _rowwise(body, row_ins, vec_ins, row_outs, acc_outs, *, tr, name):
    T = row_ins[0].shape[0]
    tr = min(tr, T)
    assert T % tr == 0
    in_specs = [pl.BlockSpec((tr, a.shape[1]), lambda i: (i, 0)) for a in row_ins]
    in_specs += [pl.BlockSpec(a.shape, lambda i: (0, 0)) for a in vec_ins]
    out_specs = [pl.BlockSpec((tr, w), lambda i: (i, 0)) for (w, _) in row_outs]
    out_specs += [pl.BlockSpec(s, lambda i: (0, 0)) for s in acc_outs]
    out_shape = [jax.ShapeDtypeStruct((T, w), dt) for (w, dt) in row_outs]
    out_shape += [jax.ShapeDtypeStruct(s, F32) for s in acc_outs]
    sem = "arbitrary" if acc_outs else "parallel"
    return list(_pcall(
        body, name=name, grid=(T // tr,), in_specs=in_specs, out_specs=out_specs,
        out_shape=out_shape,
        compiler_params=pltpu.CompilerParams(dimension_semantics=(sem,)),
    )(*row_ins, *vec_ins))


def _rstd(x):
    return lax.rsqrt(jnp.mean(x * x, axis=-1, keepdims=True) + EPS)


def _rms_bwd(x, rstd, dyg):
    xh = x * rstd
    return rstd * (dyg - xh * jnp.mean(dyg * xh, axis=-1, keepdims=True)), xh


def _fold8(v):
    r, w = v.shape
    return jnp.sum(v.reshape(r // 8, 8, w), axis=0)


def _acc(ref, val):
    first = pl.program_id(0) == 0

    @pl.when(first)
    def _():
        ref[...] = val

    @pl.when(jnp.logical_not(first))
    def _():
        ref[...] += val


def _rope(x, c, sa, sb, half):
    return x * c + pltpu.roll(x, HEAD - half, 1) * sa + pltpu.roll(x, half, 1) * sb


def _rope_t(dy, c, sa, sb, half):
    return dy * c - pltpu.roll(dy, HEAD - half, 1) * sa - pltpu.roll(dy, half, 1) * sb


def _rope_tab_body(pos_ref, inv_ref, ca, saa, sab, cb, sba, sbb):
    pos = pos_ref[...]
    lane = lax.broadcasted_iota(jnp.int32, (pos.shape[0], HEAD), 1)
    ang_a = pos * inv_ref[0:1, :]
    ang_b = pos * inv_ref[1:2, :]
    c, s = jnp.cos(ang_a), jnp.sin(ang_a)
    ha = ROT_A // 2
    ca[...] = jnp.where(lane < ROT_A, c, 1.0)
    saa[...] = jnp.where(lane < ha, -s, 0.0)
    sab[...] = jnp.where((lane >= ha) & (lane < ROT_A), s, 0.0)
    c, s = jnp.cos(ang_b), jnp.sin(ang_b)
    hb = ROPE_B // 2
    cb[...] = jnp.where(lane < ROPE_B, c, 1.0)
    sba[...] = jnp.where(lane < hb, -s, 0.0)
    sbb[...] = jnp.where((lane >= hb) & (lane < ROPE_B), s, 0.0)


def _rms_fwd_body(x_ref, g_ref, h_ref):
    x = x_ref[...]
    h_ref[...] = ((x * _rstd(x)) * g_ref[...]).astype(h_ref.dtype)


def _postproj_body(p_ref, kr_ref, ca, saa, sab, cb, sba, sbb, gq_ref, gkv_ref,
                   q_ref, k_ref, v_ref, cqn_ref, ckvn_ref, krope_ref):
    c, sa, sb = ca[...], saa[...], sab[...]
    for h in range(NH):
        lo = h * HEAD
        q_ref[:, lo:lo + HEAD] = _rope(p_ref[:, lo:lo + HEAD], c, sa, sb, ROT_A // 2).astype(q_ref.dtype)
        k_ref[:, lo:lo + HEAD] = _rope(p_ref[:, A_W + lo:A_W + lo + HEAD], c, sa, sb, ROT_A // 2).astype(k_ref.dtype)
    v_ref[...] = p_ref[:, 2 * A_W:3 * A_W].astype(v_ref.dtype)
    cq = p_ref[:, 3 * A_W:3 * A_W + LORA]
    cqn_ref[...] = ((cq * _rstd(cq)) * gq_ref[...]).astype(cqn_ref.dtype)
    ckv = p_ref[:, 3 * A_W + LORA:MAIN_COLS]
    ckvn_ref[...] = ((ckv * _rstd(ckv)) * gkv_ref[...]).astype(ckvn_ref.dtype)
    krope_ref[...] = _rope(kr_ref[...], cb[...], sba[...], sbb[...], ROPE_B // 2).astype(krope_ref.dtype)


def _qrope_body(qp_ref, cb, sba, sbb, q_ref):
    c, sa, sb = cb[...], sba[...], sbb[...]
    for h in range(NH):
        lo = h * QPAD
        q_ref[:, lo:lo + HEAD] = qp_ref[:, lo:lo + HEAD].astype(q_ref.dtype)
        q_ref[:, lo + HEAD:lo + QPAD] = _rope(qp_ref[:, lo + HEAD:lo + QPAD], c, sa, sb, ROPE_B // 2).astype(q_ref.dtype)


def _qrope_t_body(dq_ref, cb, sba, sbb, o_ref):
    c, sa, sb = cb[...], sba[...], sbb[...]
    for h in range(NH):
        lo = h * QPAD
        o_ref[:, lo:lo + HEAD] = dq_ref[:, lo:lo + HEAD].astype(o_ref.dtype)
        o_ref[:, lo + HEAD:lo + QPAD] = _rope_t(dq_ref[:, lo + HEAD:lo + QPAD], c, sa, sb, ROPE_B // 2).astype(o_ref.dtype)


def _mid_body(x_ref, o_ref, g2_ref, g3_ref, x1_ref, h2_ref):
    o = o_ref[...]
    x1 = x_ref[...] + (o * _rstd(o)) * g2_ref[...]
    x1_ref[...] = x1
    h2_ref[...] = ((x1 * _rstd(x1)) * g3_ref[...]).astype(h2_ref.dtype)


def _loss_body(x1_ref, d_ref, t_ref, g4_ref, dy_ref, dd_ref, loss_ref, dg4_ref):
    d = d_ref[...]
    rstd = _rstd(d)
    y = x1_ref[...] + (d * rstd) * g4_ref[...]
    e = y - t_ref[...]
    dy = e * (1.0 / D_MODEL)
    dy_ref[...] = dy
    dd, dh = _rms_bwd(d, rstd, dy * g4_ref[...])
    dd_ref[...] = dd.astype(dd_ref.dtype)
    _acc(dg4_ref, _fold8(dy * dh))
    e8 = _fold8(e * e)
    l = e8[:, 0:HEAD]
    for j in range(1, D_MODEL // HEAD):
        l = l + e8[:, j * HEAD:(j + 1) * HEAD]
    _acc(loss_ref, l)


def _bmid_body(dy_ref, dh2_ref, x1_ref, o_ref, g2_ref, g3_ref, dx1_ref, do_ref, dg3_ref, dg2_ref):
    x1 = x1_ref[...]
    dh2 = dh2_ref[...]
    dn, x1h = _rms_bwd(x1, _rstd(x1), dh2 * g3_ref[...])
    dx1 = dy_ref[...] + dn
    dx1_ref[...] = dx1
    _acc(dg3_ref, _fold8(dh2 * x1h))
    o = o_ref[...]
    do, oh = _rms_bwd(o, _rstd(o), dx1 * g2_ref[...])
    do_ref[...] = do.astype(do_ref.dtype)
    _acc(dg2_ref, _fold8(dx1 * oh))


def _dproj_body(dq_ref, dk_ref, dv_ref, dcq_ref, dckv_ref, p_ref, dkr_ref,
                ca, saa, sab, cb, sba, sbb, gq_ref, gkv_ref,
                dp_ref, dkrp_ref, dgq_ref, dgkv_ref):
    c, sa, sb = ca[...], saa[...], sab[...]
    for h in range(NH):
        lo = h * HEAD
        dp_ref[:, lo:lo + HEAD] = _rope_t(dq_ref[:, lo:lo + HEAD], c, sa, sb, ROT_A // 2).astype(dp_ref.dtype)
        dp_ref[:, A_W + lo:A_W + lo + HEAD] = _rope_t(dk_ref[:, lo:lo + HEAD], c, sa, sb, ROT_A // 2).astype(dp_ref.dtype)
    dp_ref[:, 2 * A_W:3 * A_W] = dv_ref[...].astype(dp_ref.dtype)
    cq = p_ref[:, 3 * A_W:3 * A_W + LORA]
    dcqn = dcq_ref[...]
    dcq, cqh = _rms_bwd(cq, _rstd(cq), dcqn * gq_ref[...])
    dp_ref[:, 3 * A_W:3 * A_W + LORA] = dcq.astype(dp_ref.dtype)
    _acc(dgq_ref, _fold8(dcqn * cqh))
    ckv = p_ref[:, 3 * A_W + LORA:MAIN_COLS]
    dckvn = dckv_ref[...]
    dckv, ckvh = _rms_bwd(ckv, _rstd(ckv), dckvn * gkv_ref[...])
    dp_ref[:, 3 * A_W + LORA:MAIN_COLS] = dckv.astype(dp_ref.dtype)
    _acc(dgkv_ref, _fold8(dckvn * ckvh))
    dkr = dkr_ref[:, 0:HEAD]
    for h in range(1, NH):
        dkr = dkr + dkr_ref[:, h * HEAD:(h + 1) * HEAD]
    dkrp_ref[...] = _rope_t(dkr, cb[...], sba[...], sbb[...], ROPE_B // 2).astype(dkrp_ref.dtype)


def _bin_body(dx1_ref, dha_ref, dhb_ref, x_ref, g1_ref, dx_ref, dg1_ref):
    x = x_ref[...]
    dh = dha_ref[...] + dhb_ref[...]
    dn, xh = _rms_bwd(x, _rstd(x), dh * g1_ref[...])
    dx_ref[...] = dx1_ref[...] + dn
    _acc(dg1_ref, _fold8(dh * xh))


def _dot_nt(a, b):
    return lax.dot_general(a, b, _DIMS["nt"], preferred_element_type=F32)


def _dot_tn(a, b):
    return lax.dot_general(a, b, _DIMS["tn"], preferred_element_type=F32)


def _dot_nn(a, b):
    return jnp.dot(a, b, preferred_element_type=F32)


DIL_SCALE = HEAD ** -0.5
DIL_CHUNK = 256


def _dil_rows(t, d):
    r = t & (d - 1)
    n = t >> (d.bit_length() - 1)
    start = r + n * (HEAD * d)
    has_prev = n > 0
    pstart = jnp.where(has_prev, start - HEAD * d, start)
    if d == 1:
        return pl.ds(pl.multiple_of(start, HEAD), HEAD), pl.ds(pl.multiple_of(pstart, HEAD), HEAD), has_prev
    return pl.ds(start, HEAD, stride=d), pl.ds(pstart, HEAD, stride=d), has_prev


def _dil_band():
    row = lax.broadcasted_iota(jnp.int32, (HEAD, 2 * HEAD), 0)
    col = lax.broadcasted_iota(jnp.int32, (HEAD, 2 * HEAD), 1)
    return (col >= row) & (col <= row + HEAD), col >= HEAD


def _dil_fwd_body(q_ref, k_ref, v_ref, a_ref, lse_ref, o1, o2, o3, l1, l2, l3, *, nt, unroll):
    band, is_cur = _dil_band()
    for d, o_sc, l_sc in zip(DIL, (o1, o2, o3), (l1, l2, l3)):

        def tile(t, carry, d=d, o_sc=o_sc, l_sc=l_sc):
            rows, prows, has_prev = _dil_rows(t, d)
            q = q_ref[rows, :].astype(MXU_DTYPE)
            kk = jnp.concatenate([k_ref[prows, :], k_ref[rows, :]], axis=0).astype(MXU_DTYPE)
            vv = jnp.concatenate([v_ref[prows, :], v_ref[rows, :]], axis=0).astype(MXU_DTYPE)
            ok = band & (is_cur | has_prev)
            s = jnp.where(ok, _dot_nt(q, kk) * DIL_SCALE, NEG)
            m = jnp.max(s, axis=1, keepdims=True)
            p = jnp.exp(s - m)
            den = jnp.sum(p, axis=1, keepdims=True)
            o_sc[rows, :] = _dot_nn((p / den).astype(MXU_DTYPE), vv)
            l_sc[rows, :] = jnp.broadcast_to(m + jnp.log(den), (HEAD, HEAD))
            return carry

        lax.fori_loop(0, nt, tile, 0, unroll=unroll)

    def merge(i, carry):
        rs = pl.ds(pl.multiple_of(i * DIL_CHUNK, DIL_CHUNK), DIL_CHUNK)
        la, lb, lc = l1[rs, :], l2[rs, :], l3[rs, :]
        m = jnp.maximum(jnp.maximum(la, lb), lc)
        wa, wb, wc = jnp.exp(la - m), jnp.exp(lb - m), jnp.exp(lc - m)
        den = wa + wb + wc
        a = (wa / den) * o1[rs, :] + (wb / den) * o2[rs, :] + (wc / den) * o3[rs, :]
        a_ref[rs, :] = a.astype(a_ref.dtype)
        lse_ref[rs, :] = m + jnp.log(den)
        return carry

    lax.fori_loop(0, q_ref.shape[0] // DIL_CHUNK, merge, 0)


def _dil_fwd(q, k, v):
    T = q.shape[0]
    spec = pl.BlockSpec((T, HEAD), lambda h: (0, h))
    return _pcall(
        functools.partial(_dil_fwd_body, nt=T // HEAD, unroll=4), name="dil_fwd",
        grid=(NH,), in_specs=[spec] * 3, out_specs=[spec] * 2,
        out_shape=[jax.ShapeDtypeStruct((T, A_W), MXU_DTYPE), jax.ShapeDtypeStruct((T, A_W), F32)],
        scratch_shapes=[pltpu.VMEM((T, HEAD), F32)] * 6,
        compiler_params=pltpu.CompilerParams(dimension_semantics=("parallel",)),
    )(q, k, v)


def _dil_bwd_body(q_ref, k_ref, v_ref, do_ref, a_ref, lse_ref, dq_ref, dk_ref, dv_ref, dl_sc, *, nt, unroll):
    band, is_cur = _dil_band()

    def prep(i, carry):
        rs = pl.ds(pl.multiple_of(i * DIL_CHUNK, DIL_CHUNK), DIL_CHUNK)
        dl = jnp.sum(do_ref[rs, :] * a_ref[rs, :].astype(F32), axis=1, keepdims=True)
        dl_sc[rs, :] = jnp.broadcast_to(dl, (DIL_CHUNK, HEAD))
        zero = jnp.zeros((DIL_CHUNK, HEAD), F32)
        dq_ref[rs, :] = zero
        dk_ref[rs, :] = zero
        dv_ref[rs, :] = zero
        return carry

    lax.fori_loop(0, q_ref.shape[0] // DIL_CHUNK, prep, 0)

    for d in DIL:

        def tile(t, carry, d=d):
            rows, prows, has_prev = _dil_rows(t, d)
            q = q_ref[rows, :].astype(MXU_DTYPE)
            kk = jnp.concatenate([k_ref[prows, :], k_ref[rows, :]], axis=0).astype(MXU_DTYPE)
            vv = jnp.concatenate([v_ref[prows, :], v_ref[rows, :]], axis=0).astype(MXU_DTYPE)
            do = do_ref[rows, :].astype(MXU_DTYPE)
            lse = lse_ref[rows, :]
            dl = dl_sc[rows, :]
            ok = band & (is_cur | has_prev)
            s = _dot_nt(q, kk) * DIL_SCALE
            p = jnp.where(ok, jnp.exp(s - jnp.concatenate([lse, lse], axis=1)), 0.0)
            ds = (p * (_dot_nt(do, vv) - jnp.concatenate([dl, dl], axis=1))).astype(MXU_DTYPE)
            dq_ref[rows, :] += _dot_nn(ds, kk) * DIL_SCALE
            dkk = _dot_tn(ds, q) * DIL_SCALE
            dvv = _dot_tn(p.astype(MXU_DTYPE), do)
            dk_ref[rows, :] += dkk[HEAD:, :]
            dv_ref[rows, :] += dvv[HEAD:, :]
            dk_ref[prows, :] += dkk[:HEAD, :]
            dv_ref[prows, :] += dvv[:HEAD, :]
            return carry

        lax.fori_loop(0, nt, tile, 0, unroll=unroll)


def _dil_bwd(q, k, v, dmix, mixed, lse):
    T = q.shape[0]
    spec = pl.BlockSpec((T, HEAD), lambda h: (0, h))
    return _pcall(
        functools.partial(_dil_bwd_body, nt=T // HEAD, unroll=2), name="dil_bwd",
        grid=(NH,), in_specs=[spec] * 6, out_specs=[spec] * 3,
        out_shape=[jax.ShapeDtypeStruct((T, A_W), F32)] * 3,
        scratch_shapes=[pltpu.VMEM((T, HEAD), F32)],
        compiler_params=pltpu.CompilerParams(dimension_semantics=("parallel",)),
    )(q, k, v, dmix, mixed, lse)


MLA_SCALE = (HEAD + ROPE_B) ** -0.5
MLA_T = 512
MLA_HP = 2


def _tri(t):
    row = lax.broadcasted_iota(jnp.int32, (t, t), 0)
    col = lax.broadcasted_iota(jnp.int32, (t, t), 1)
    return col <= row


def _lanes(x, n):
    return jnp.tile(x, (1, n // HEAD))


def _mla_fwd_body(q_ref, kn_ref, kr_ref, v_ref, o_ref, lse_ref, m_sc, l_sc, acc_sc, *, t, hp):
    qi = pl.program_id(1)
    m_sc[...] = jnp.full(m_sc.shape, NEG, F32)
    l_sc[...] = jnp.zeros(l_sc.shape, F32)
    acc_sc[...] = jnp.zeros(acc_sc.shape, F32)

    def step(j, masked):
        ks = pl.ds(pl.multiple_of(j * t, t), t)
        kr = kr_ref[ks, :]
        for hh in range(hp):
            kcat = jnp.concatenate([kn_ref[ks, hh * HEAD:(hh + 1) * HEAD], kr], axis=1)
            s = _dot_nt(q_ref[:, hh * QPAD:(hh + 1) * QPAD], kcat) * MLA_SCALE
            if masked:
                s = jnp.where(_tri(t), s, NEG)
            m_prev = m_sc[hh]
            m_new = jnp.maximum(m_prev, jnp.max(s, axis=1, keepdims=True))
            alpha = jnp.exp(m_prev - m_new)
            p = jnp.exp(s - _lanes(m_new, t))
            l_sc[hh] = alpha * l_sc[hh] + jnp.sum(p, axis=1, keepdims=True)
            acc_sc[hh] = alpha * acc_sc[hh] + _dot_nn(p.astype(MXU_DTYPE), v_ref[ks, hh * HEAD:(hh + 1) * HEAD])
            m_sc[hh] = m_new

    def off_diag(j, carry):
        step(j, False)
        return carry

    lax.fori_loop(0, qi, off_diag, 0)
    step(qi, True)
    for hh in range(hp):
        l = l_sc[hh]
        o_ref[:, hh * HEAD:(hh + 1) * HEAD] = (acc_sc[hh] / l).astype(o_ref.dtype)
        lse_ref[:, hh * HEAD:(hh + 1) * HEAD] = m_sc[hh] + jnp.log(l)


def _mla_fwd(qf, kv, kr):
    T = qf.shape[0]
    t, hp = min(MLA_T, T), MLA_HP
    ng = NH // hp
    return _pcall(
        functools.partial(_mla_fwd_body, t=t, hp=hp), name="mla_fwd",
        grid=(ng, T // t),
        in_specs=[pl.BlockSpec((t, hp * QPAD), lambda g, i: (i, g)),
                  pl.BlockSpec((T, hp * HEAD), lambda g, i: (0, g)),
                  pl.BlockSpec((T, HEAD), lambda g, i: (0, 0)),
                  pl.BlockSpec((T, hp * HEAD), lambda g, i: (0, ng + g))],
        out_specs=[pl.BlockSpec((t, hp * HEAD), lambda g, i: (i, g))] * 2,
        out_shape=[jax.ShapeDtypeStruct((T, A_W), MXU_DTYPE), jax.ShapeDtypeStruct((T, A_W), F32)],
        scratch_shapes=[pltpu.VMEM((hp, t, HEAD), F32)] * 3,
        compiler_params=pltpu.CompilerParams(dimension_semantics=("parallel", "parallel")),
    )(qf, kv, kr, kv)


def _mla_bwd_body(q_ref, kn_ref, kr_ref, v_ref, do_ref, o_ref, lse_ref, dq_ref, dkn_ref, dv_ref, dkr_ref,
                  dl_sc, dk_sc, dv_sc, *, t):
    ki = pl.program_id(1)
    nq = q_ref.shape[0] // t

    @pl.when(ki == 0)
    def _():
        def prep(i, carry):
            rs = pl.ds(pl.multiple_of(i * t, t), t)
            dl = jnp.sum(do_ref[rs, :] * o_ref[rs, :].astype(F32), axis=1, keepdims=True)
            dl_sc[rs, :] = jnp.broadcast_to(dl, (t, HEAD))
            dq_ref[rs, :] = jnp.zeros((t, QPAD), F32)
            return carry
        lax.fori_loop(0, nq, prep, 0)

    kcat = jnp.concatenate([kn_ref[...], kr_ref[...]], axis=1)
    v = v_ref[...]
    dk_sc[...] = jnp.zeros(dk_sc.shape, F32)
    dv_sc[...] = jnp.zeros(dv_sc.shape, F32)

    def step(i, masked):
        qs = pl.ds(pl.multiple_of(i * t, t), t)
        q = q_ref[qs, :]
        do = do_ref[qs, :].astype(MXU_DTYPE)
        p = jnp.exp(_dot_nt(q, kcat) * MLA_SCALE - _lanes(lse_ref[qs, :], t))
        if masked:
            p = jnp.where(_tri(t), p, 0.0)
        ds = (p * (_dot_nt(do, v) - _lanes(dl_sc[qs, :], t))).astype(MXU_DTYPE)
        dv_sc[...] += _dot_tn(p.astype(MXU_DTYPE), do)
        dk_sc[...] += _dot_tn(ds, q)
        dq_ref[qs, :] += _dot_nn(ds, kcat) * MLA_SCALE

    step(ki, True)

    def off_diag(i, carry):
        step(i, False)
        return carry

    lax.fori_loop(ki + 1, nq, off_diag, 0)
    dk = dk_sc[...] * MLA_SCALE
    dkn_ref[...] = dk[:, 0:HEAD].astype(dkn_ref.dtype)
    dkr_ref[...] = dk[:, HEAD:QPAD]
    dv_ref[...] = dv_sc[...].astype(dv_ref.dtype)


def _mla_bwd(qf, kv, kr, dmix, mixed, lse):
    T = qf.shape[0]
    t = min(MLA_T, T)
    head = lambda h, j: (0, h)
    b_half = lambda h, j: (0, NH + h)
    kblk = pl.BlockSpec((t, HEAD), lambda h, j: (j, h))
    return _pcall(
        functools.partial(_mla_bwd_body, t=t), name="mla_bwd",
        grid=(NH, T // t),
        in_specs=[pl.BlockSpec((T, QPAD), head), kblk,
                  pl.BlockSpec((t, HEAD), lambda h, j: (j, 0)),
                  pl.BlockSpec((t, HEAD), lambda h, j: (j, NH + h)),
                  pl.BlockSpec((T, HEAD), b_half), pl.BlockSpec((T, HEAD), b_half),
                  pl.BlockSpec((T, HEAD), head)],
        out_specs=[pl.BlockSpec((T, QPAD), head), kblk, kblk, kblk],
        out_shape=[jax.ShapeDtypeStruct((T, NH * QPAD), F32), jax.ShapeDtypeStruct((T, A_W), MXU_DTYPE),
                   jax.ShapeDtypeStruct((T, A_W), MXU_DTYPE), jax.ShapeDtypeStruct((T, A_W), F32)],
        scratch_shapes=[pltpu.VMEM((T, HEAD), F32), pltpu.VMEM((t, QPAD), F32), pltpu.VMEM((t, HEAD), F32)],
        compiler_params=pltpu.CompilerParams(dimension_semantics=("parallel", "arbitrary")),
    )(qf, kv, kr, kv, dmix, mixed, lse)


def _local_step(x, pos, target, g1, g2, gq, gkv, g3, g4, w_main, w_kr, w_uq_p, w_ukv_p, w_out,
                mlp_weights, mlp_grads_ready):
    T = x.shape[0]
    TR = 256
    mm = functools.partial(_matmul, tm=512, tn=1024, tk=2048, b_outer=True)
    mm_k = functools.partial(_matmul, tm=1024, tn=1024, tk=2048)
    mm_g = functools.partial(_matmul, tm=512, tn=1024, tk=4096, b_outer=True)

    inv_a = ROPE_THETA ** (-jnp.arange(0, ROT_A, 2, dtype=F32) / ROT_A)
    inv_b = ROPE_THETA ** (-jnp.arange(0, ROPE_B, 2, dtype=F32) / ROPE_B)
    inv = jnp.stack([jnp.concatenate([inv_a, inv_a, jnp.zeros((HEAD - ROT_A,), F32)]),
                     jnp.concatenate([inv_b, inv_b, jnp.zeros((HEAD - ROPE_B,), F32)])])
    inv = jnp.concatenate([inv, jnp.zeros((6, HEAD), F32)], axis=0)
    tabs = _rowwise(_rope_tab_body, [pos], [inv], [(HEAD, F32)] * 6, [], tr=512, name="rope_tables")

    (h,) = _rowwise(_rms_fwd_body, [x], [g1], [(D_MODEL, MXU_DTYPE)], [], tr=TR, name="rms_in")
    (proj,) = mm(h, w_main, dims="nn", out_dtypes=[F32], name="proj_main")
    (kr_raw,) = mm(h, w_kr, dims="nn", out_dtypes=[F32], name="proj_kr")
    q, k, v, cqn, ckvn, krope = _rowwise(
        _postproj_body, [proj, kr_raw] + tabs, [gq, gkv],
        [(A_W, F32)] * 3 + [(LORA, MXU_DTYPE)] * 2 + [(HEAD, MXU_DTYPE)], [], tr=TR, name="post_proj")
    a_out, lse_a = _dil_fwd(q, k, v)

    (q_pad,) = mm(cqn, w_uq_p, dims="nn", out_dtypes=[F32], name="q_up")
    (qf,) = _rowwise(_qrope_body, [q_pad] + tabs[3:], [], [(NH * QPAD, MXU_DTYPE)], [], tr=TR, name="q_rope")
    (kv,) = mm(ckvn, w_ukv_p, dims="nn", out_dtypes=[MXU_DTYPE], name="kv_up")
    b_out, lse_b = _mla_fwd(qf, kv, krope)

    mixed = jnp.concatenate([a_out, b_out], axis=1)
    (o,) = mm(mixed, w_out, dims="nn", out_dtypes=[F32], name="out_proj")
    x1, h2 = _rowwise(_mid_body, [x, o], [g2, g3], [(D_MODEL, F32), (D_MODEL, MXU_DTYPE)], [], tr=TR, name="mid_norm")

    w_up, w_down = mlp_weights(h2)

    def up_epi(acc):
        r = jnp.maximum(acc, 0.0)
        return r * r, r
    u, r = mm(h2, w_up, dims="nn", out_dtypes=[MXU_DTYPE, MXU_DTYPE], name="mlp_up", epi=up_epi)
    (dn,) = mm_k(u, w_down, dims="nn", out_dtypes=[F32], name="mlp_down")
    dy, dd, loss8, dg4 = _rowwise(_loss_body, [x1, dn, target], [g4], [(D_MODEL, F32), (D_MODEL, MXU_DTYPE)],
                                  [(8, HEAD), (8, D_MODEL)], tr=TR, name="loss_head")

    def dup_epi(acc, rr):
        return (acc * (2.0 * rr.astype(F32)),)
    (dup,) = mm(dd, w_down, dims="nt", out_dtypes=[MXU_DTYPE], name="d_up", epi=dup_epi, extras=(r,))
    (gw_down,) = mm_g(u, dd, dims="tn", out_dtypes=[WIRE_DTYPE], name="gw_down")
    (dh2,) = mm_k(dup, w_up, dims="nt", out_dtypes=[F32], name="d_h2")
    (gw_up,) = mm_g(h2, dup, dims="tn", out_dtypes=[WIRE_DTYPE], name="gw_up")
    g2 = g2 + mlp_grads_ready(gw_up, gw_down)
    dx1, do, dg3, dg2 = _rowwise(_bmid_body, [dy, dh2, x1, o], [g2, g3], [(D_MODEL, F32), (D_MODEL, MXU_DTYPE)],
                                 [(8, D_MODEL), (8, D_MODEL)], tr=TR, name="bwd_mid")
    (dmix,) = mm(do, w_out, dims="nt", out_dtypes=[F32], name="d_mixed")
    (gw_out,) = mm_g(mixed, do, dims="tn", out_dtypes=[WIRE_DTYPE], name="gw_out")

    dqf, dkn, dvb, dkr = _mla_bwd(qf, kv, krope, dmix, mixed, lse_b)
    (dq_pad,) = _rowwise(_qrope_t_body, [dqf] + tabs[3:], [], [(NH * QPAD, MXU_DTYPE)], [], tr=TR, name="q_rope_t")
    (dcqn,) = mm(dq_pad, w_uq_p, dims="nt", out_dtypes=[F32], name="d_cq")
    (gw_uq_p,) = mm_g(cqn, dq_pad, dims="tn", out_dtypes=[WIRE_DTYPE], name="gw_uq")
    dkv = jnp.concatenate([dkn, dvb], axis=1)
    (dckvn,) = mm(dkv, w_ukv_p, dims="nt", out_dtypes=[F32], name="d_ckv")
    (gw_ukv_p,) = mm_g(ckvn, dkv, dims="tn", out_dtypes=[WIRE_DTYPE], name="gw_ukv")

    dq_a, dk_a, dv_a = _dil_bwd(q, k, v, dmix, mixed, lse_a)
    dproj, dkrp, dgq, dgkv = _rowwise(
        _dproj_body, [dq_a, dk_a, dv_a, dcqn, dckvn, proj, dkr] + tabs, [gq, gkv],
        [(MAIN_COLS, MXU_DTYPE), (HEAD, MXU_DTYPE)], [(8, LORA), (8, LORA)], tr=TR, name="d_proj")
    (dha,) = mm_k(dproj, w_main, dims="nt", out_dtypes=[F32], name="d_h_main")
    (dhb,) = mm(dkrp, w_kr, dims="nt", out_dtypes=[F32], name="d_h_kr")
    (gw_main,) = mm_g(h, dproj, dims="tn", out_dtypes=[WIRE_DTYPE], name="gw_in_main")
    (gw_kr,) = mm_g(h, dkrp, dims="tn", out_dtypes=[WIRE_DTYPE], name="gw_in_kr")
    dx, dg1 = _rowwise(_bin_body, [dx1, dha, dhb, x], [g1], [(D_MODEL, F32)], [(8, D_MODEL)], tr=TR, name="bwd_in")

    small = jnp.concatenate([dg1, dg2, dgq, dgkv, dg3, dg4, loss8], axis=1)
    return dx, (gw_main, gw_kr, gw_uq_p, gw_ukv_p, gw_out), small


def _place():
    x, y, c = lax.axis_index("x"), lax.axis_index("y"), lax.axis_index("c")
    chips = [(1 - x, y), (x, 1 - y), (1 - x, 1 - y)]
    return x, y, c, chips


def _cast_place_body(me_ref, w_ref, o_ref):
    o_ref[...] = w_ref[...].astype(o_ref.dtype)


def _cast_place(me_arr, w, name):
    rows, cols = w.shape
    tr = min(rows, 256)
    grid_spec = pltpu.PrefetchScalarGridSpec(
        num_scalar_prefetch=1, grid=(rows // tr,),
        in_specs=[pl.BlockSpec((tr, cols), lambda i, me: (i, 0))],
        out_specs=pl.BlockSpec((None, tr, cols), lambda i, me: (me[0], i, 0)))
    return _pcall(
        _cast_place_body, name=name, grid_spec=grid_spec,
        out_shape=jax.ShapeDtypeStruct((N_CHIPS, rows, cols), WIRE_DTYPE),
        compiler_params=pltpu.CompilerParams(dimension_semantics=("parallel",)),
    )(me_arr, w)


def _ag_body(*refs, n_w):
    bufs = refs[n_w:2 * n_w]
    send_sems, recv_sems, fsend_sems, frecv_sems = refs[2 * n_w:]
    x, y, c, chips = _place()
    me = 2 * x + y
    sib = (x, y, 1 - c)

    def half_rows(w, which):
        half = bufs[w].shape[1] // 2
        return pl.ds(pl.multiple_of(which * half, 16), half)

    sends, fwds = [], []
    for w in range(n_w):
        mine = bufs[w].at[me, half_rows(w, c)]
        for j, (px, py) in enumerate(chips):
            cp = pltpu.make_async_remote_copy(
                src_ref=mine, dst_ref=mine,
                send_sem=send_sems.at[w * 3 + j], recv_sem=recv_sems.at[w * 3 + j],
                device_id=(px, py, c), device_id_type=MESH)
            cp.start()
            sends.append(cp)
    for w in range(n_w):
        for j, (px, py) in enumerate(chips):
            landed = bufs[w].at[2 * px + py, half_rows(w, c)]
            pltpu.make_async_remote_copy(
                src_ref=landed, dst_ref=landed,
                send_sem=send_sems.at[w * 3 + j], recv_sem=recv_sems.at[w * 3 + j],
                device_id=(px, py, c), device_id_type=MESH).wait_recv()
            fw = pltpu.make_async_remote_copy(
                src_ref=landed, dst_ref=landed,
                send_sem=fsend_sems.at[w * 3 + j], recv_sem=frecv_sems.at[w * 3 + j],
                device_id=sib, device_id_type=MESH)
            fw.start()
            fwds.append(fw)
    for w in range(n_w):
        for j, (px, py) in enumerate(chips):
            passed = bufs[w].at[2 * px + py, half_rows(w, 1 - c)]
            pltpu.make_async_remote_copy(
                src_ref=passed, dst_ref=passed,
                send_sem=fsend_sems.at[w * 3 + j], recv_sem=frecv_sems.at[w * 3 + j],
                device_id=sib, device_id_type=MESH).wait_recv()
    for cp in sends + fwds:
        cp.wait_send()


def _allgather_weights(placed):
    n_w = len(placed)
    return _pcall(
        functools.partial(_ag_body, n_w=n_w), name="weight_allgather",
        in_specs=[ANY] * n_w, out_specs=[ANY] * n_w,
        out_shape=[jax.ShapeDtypeStruct(p.shape, p.dtype) for p in placed],
        input_output_aliases={w: w for w in range(n_w)},
        scratch_shapes=[pltpu.SemaphoreType.DMA((3 * n_w,))] * 4,
    )(*placed)


HBM = pl.BlockSpec(memory_space=pltpu.HBM)
SEM = pl.BlockSpec(memory_space=pltpu.SEMAPHORE)
EFFECT = pltpu.SideEffectType.DATAFLOW_SIDE_EFFECTING


def _in_hbm(a):
    return pltpu.with_memory_space_constraint(a, pltpu.HBM)


def _ag_descs(bufs, send_sems, recv_sems):
    x, y, c, chips = _place()
    me = 2 * x + y
    out = []
    for w, buf in enumerate(bufs):
        half = buf.shape[1] // 2
        rows = pl.ds(pl.multiple_of(c * half, 16), half)
        mine = buf.at[me, rows]
        for j, (px, py) in enumerate(chips):
            landed = buf.at[2 * px + py, rows]
            mk = lambda ref, w=w, j=j, px=px, py=py: pltpu.make_async_remote_copy(
                src_ref=ref, dst_ref=ref, send_sem=send_sems.at[w * 3 + j], recv_sem=recv_sems.at[w * 3 + j],
                device_id=(px, py, c), device_id_type=MESH)
            out.append((mk(mine), mk(landed)))
    return out


def _ag_start_body(*refs, n_w):
    bufs = refs[:n_w]
    send_sems, recv_sems = refs[n_w + 1], refs[n_w + 2]
    token = refs[-1]
    for send, _ in _ag_descs(bufs, send_sems, recv_sems):
        send.start()
    token[...] = jnp.zeros_like(token)


def _ag_start(placed, after):
    n_w = len(placed)
    res = _pcall(
        functools.partial(_ag_start_body, n_w=n_w), name="weight_allgather_start",
        in_specs=[HBM] * n_w + [ANY],
        out_specs=[SEM, SEM] + [HBM] * n_w + [pl.BlockSpec(memory_space=pltpu.VMEM)],
        out_shape=[pltpu.SemaphoreType.DMA((3 * n_w,)), pltpu.SemaphoreType.DMA((3 * n_w,))]
        + [pltpu.HBM(p.shape, p.dtype) for p in placed] + [jax.ShapeDtypeStruct((8, HEAD), F32)],
        input_output_aliases={w: 2 + w for w in range(n_w)},
        compiler_params=pltpu.CompilerParams(has_side_effects=EFFECT),
    )(*[_in_hbm(p) for p in placed], after)
    return res[0], res[1], list(res[2:2 + n_w]), res[-1]


def _ag_wait_body(*refs, n_w):
    bufs = refs[:n_w]
    send_sems, recv_sems = refs[n_w], refs[n_w + 1]
    for send, recv in _ag_descs(bufs, send_sems, recv_sems):
        send.wait_send()
        recv.wait_recv()


def _ag_wait(send_sems, recv_sems, bufs, after):
    n_w = len(bufs)
    return list(_pcall(
        functools.partial(_ag_wait_body, n_w=n_w), name="weight_allgather_wait",
        in_specs=[HBM] * n_w + [SEM, SEM, ANY], out_specs=[HBM] * n_w,
        out_shape=[pltpu.HBM(b.shape, b.dtype) for b in bufs],
        input_output_aliases={w: w for w in range(n_w)},
        compiler_params=pltpu.CompilerParams(has_side_effects=EFFECT),
    )(*bufs, send_sems, recv_sems, after))


def _ag_forward_body(*refs, n_w):
    bufs = refs[n_w:2 * n_w]
    send_sems, recv_sems = refs[2 * n_w:]
    x, y, c, chips = _place()
    fwds = []
    for w, buf in enumerate(bufs):
        half = buf.shape[1] // 2
        for j, (px, py) in enumerate(chips):
            def piece(which, buf=buf, half=half, px=px, py=py):
                return buf.at[2 * px + py, pl.ds(pl.multiple_of(which * half, 16), half)]
            mk = lambda ref, w=w, j=j: pltpu.make_async_remote_copy(
                src_ref=ref, dst_ref=ref, send_sem=send_sems.at[w * 3 + j], recv_sem=recv_sems.at[w * 3 + j],
                device_id=(x, y, 1 - c), device_id_type=MESH)
            fw = mk(piece(c))
            fw.start()
            fwds.append((fw, mk(piece(1 - c))))
    for fw, back in fwds:
        back.wait_recv()
        fw.wait_send()


def _ag_forward(bufs):
    n_w = len(bufs)
    return list(_pcall(
        functools.partial(_ag_forward_body, n_w=n_w), name="weight_allgather_forward",
        in_specs=[ANY] * n_w, out_specs=[ANY] * n_w,
        out_shape=[jax.ShapeDtypeStruct(b.shape, b.dtype) for b in bufs],
        input_output_aliases={w: w for w in range(n_w)},
        scratch_shapes=[pltpu.SemaphoreType.DMA((3 * n_w,))] * 2,
    )(*bufs))


def _sc_descs(ins, outs, send_sems, recv_sems):
    x, y, c, chips = _place()
    me = 2 * x + y
    out = []
    for w in range(len(ins)):
        for j, (px, py) in enumerate(chips):
            out.append(pltpu.make_async_remote_copy(
                src_ref=ins[w].at[2 * px + py], dst_ref=outs[w].at[me],
                send_sem=send_sems.at[w * 3 + j], recv_sem=recv_sems.at[w * 3 + j],
                device_id=(px, py, c), device_id_type=MESH))
    return out


def _scatter_start_body(*refs, n_w):
    ins, lands = refs[:n_w], refs[n_w:2 * n_w]
    send_sems, recv_sems = refs[2 * n_w + 1], refs[2 * n_w + 2]
    token = refs[-1]
    for cp in _sc_descs(ins, lands, send_sems, recv_sems):
        cp.start()
    token[...] = jnp.zeros_like(token)


def _scatter_start(parts, after):
    n_w = len(parts)
    lands = [lax.empty(p.shape, p.dtype) for p in parts]
    res = _pcall(
        functools.partial(_scatter_start_body, n_w=n_w), name="grad_scatter_start",
        in_specs=[HBM] * (2 * n_w) + [ANY],
        out_specs=[SEM, SEM] + [HBM] * (2 * n_w) + [pl.BlockSpec(memory_space=pltpu.VMEM)],
        out_shape=[pltpu.SemaphoreType.DMA((3 * n_w,)), pltpu.SemaphoreType.DMA((3 * n_w,))]
        + [pltpu.HBM(p.shape, p.dtype) for p in parts] * 2 + [jax.ShapeDtypeStruct((8, HEAD), F32)],
        input_output_aliases={i: 2 + i for i in range(2 * n_w)},
        compiler_params=pltpu.CompilerParams(has_side_effects=EFFECT),
    )(*[_in_hbm(p) for p in parts], *[_in_hbm(l) for l in lands], after)
    return res[0], res[1], list(res[2:2 + n_w]), list(res[2 + n_w:2 + 2 * n_w]), res[-1]


def _scatter_wait_body(*refs, n_w):
    ins, lands = refs[:n_w], refs[n_w:2 * n_w]
    send_sems, recv_sems = refs[2 * n_w], refs[2 * n_w + 1]
    for cp in _sc_descs(ins, lands, send_sems, recv_sems):
        cp.wait_send()
        cp.wait_recv()


def _scatter_wait(send_sems, recv_sems, parts, lands, after):
    n_w = len(parts)
    res = _pcall(
        functools.partial(_scatter_wait_body, n_w=n_w), name="grad_scatter_wait",
        in_specs=[HBM] * (2 * n_w) + [SEM, SEM, ANY], out_specs=[HBM] * (2 * n_w),
        out_shape=[pltpu.HBM(p.shape, p.dtype) for p in parts] * 2,
        input_output_aliases={i: i for i in range(2 * n_w)},
        compiler_params=pltpu.CompilerParams(has_side_effects=EFFECT),
    )(*parts, *lands, send_sems, recv_sems, after)
    return list(res[:n_w]), list(res[n_w:])


def _pair_send_body(*refs, n_w):
    ins, outs = refs[:n_w], refs[n_w:2 * n_w]
    send_sems, recv_sems = refs[2 * n_w:]
    x, y, c, _ = _place()
    cps = []
    for w in range(n_w):
        cp = pltpu.make_async_remote_copy(
            src_ref=ins[w].at[:, 1 - c], dst_ref=outs[w],
            send_sem=send_sems.at[w], recv_sem=recv_sems.at[w],
            device_id=(x, y, 1 - c), device_id_type=MESH)
        cp.start()
        cps.append(cp)
    for cp in cps:
        cp.wait()


def _pair_send(grads4, tag):
    n_w = len(grads4)
    return _pcall(
        functools.partial(_pair_send_body, n_w=n_w), name="grad_pair_exchange_" + tag,
        in_specs=[ANY] * n_w, out_specs=[ANY] * n_w,
        out_shape=[jax.ShapeDtypeStruct((g.shape[0],) + g.shape[2:], g.dtype) for g in grads4],
        scratch_shapes=[pltpu.SemaphoreType.DMA((n_w,))] * 2,
    )(*grads4)


def _pair_add_body(c_ref, mine_ref, theirs_ref, o_ref):
    o_ref[...] = (mine_ref[...].astype(F32) + theirs_ref[...].astype(F32)).astype(o_ref.dtype)


def _pair_add(c_arr, g4, recv, name):
    _, _, hr, cols = g4.shape
    tr = min(hr, 256)
    grid_spec = pltpu.PrefetchScalarGridSpec(
        num_scalar_prefetch=1, grid=(N_CHIPS, hr // tr),
        in_specs=[pl.BlockSpec((None, None, tr, cols), lambda s, i, c: (s, c[0], i, 0)),
                  pl.BlockSpec((None, tr, cols), lambda s, i, c: (s, i, 0))],
        out_specs=pl.BlockSpec((None, tr, cols), lambda s, i, c: (s, i, 0)))
    return _pcall(
        _pair_add_body, name=name, grid_spec=grid_spec,
        out_shape=jax.ShapeDtypeStruct(recv.shape, recv.dtype),
        compiler_params=pltpu.CompilerParams(dimension_semantics=("parallel", "parallel")),
    )(c_arr, g4, recv)


def _scatter_body(*refs, n_w):
    ins, outs = refs[:n_w], refs[n_w:2 * n_w]
    send_sems, recv_sems = refs[2 * n_w:]
    x, y, c, chips = _place()
    me = 2 * x + y
    todo = []
    for w in range(n_w):
        for j, (px, py) in enumerate(chips):
            cp = pltpu.make_async_remote_copy(
                src_ref=ins[w].at[2 * px + py], dst_ref=outs[w].at[me],
                send_sem=send_sems.at[w * 3 + j], recv_sem=recv_sems.at[w * 3 + j],
                device_id=(px, py, c), device_id_type=MESH)
            cp.start()
            todo.append(cp)
    for t in todo:
        t.wait()


def _scatter(parts):
    n_w = len(parts)
    return _pcall(
        functools.partial(_scatter_body, n_w=n_w), name="grad_scatter",
        in_specs=[ANY] * n_w, out_specs=[ANY] * n_w,
        out_shape=[jax.ShapeDtypeStruct(p.shape, p.dtype) for p in parts],
        scratch_shapes=[pltpu.SemaphoreType.DMA((3 * n_w,))] * 2,
    )(*parts)


def _sum4_body(me_ref, p_ref, l0, l1, l2, l3, o_ref):
    me = me_ref[0]
    t = [jnp.where(me == j, p_ref[...], l[...]).astype(F32) for j, l in enumerate((l0, l1, l2, l3))]
    o_ref[...] = ((t[0] + t[1]) + t[2]) + t[3]


def _sum4(me_arr, part, landed, name):
    _, hr, cols = part.shape
    tr = min(hr, 256)

    def slot(j):
        return lambda i, me: (jnp.where(me[0] == j, (j + 1) % N_CHIPS, j), i, 0)

    grid_spec = pltpu.PrefetchScalarGridSpec(
        num_scalar_prefetch=1, grid=(hr // tr,),
        in_specs=[pl.BlockSpec((None, tr, cols), lambda i, me: (me[0], i, 0))]
        + [pl.BlockSpec((None, tr, cols), slot(j)) for j in range(N_CHIPS)],
        out_specs=pl.BlockSpec((tr, cols), lambda i, me: (i, 0)))
    return _pcall(
        _sum4_body, name=name, grid_spec=grid_spec,
        out_shape=jax.ShapeDtypeStruct((hr, cols), F32),
        compiler_params=pltpu.CompilerParams(dimension_semantics=("parallel",)),
    )(me_arr, part, landed, landed, landed, landed)


def _pair_swap_body(*refs, n_w):
    ins, outs = refs[:n_w], refs[n_w:2 * n_w]
    send_sems, recv_sems = refs[2 * n_w:]
    x, y, c, _ = _place()
    todo = []
    for w in range(n_w):
        cp = pltpu.make_async_remote_copy(
            src_ref=ins[w], dst_ref=outs[w],
            send_sem=send_sems.at[w], recv_sem=recv_sems.at[w],
            device_id=(x, y, 1 - c), device_id_type=MESH)
        cp.start()
        todo.append(cp)
    for t in todo:
        t.wait()


def _pair_swap(halves):
    n_w = len(halves)
    return _pcall(
        functools.partial(_pair_swap_body, n_w=n_w), name="grad_pair_swap",
        in_specs=[ANY] * n_w, out_specs=[ANY] * n_w,
        out_shape=[jax.ShapeDtypeStruct(h.shape, h.dtype) for h in halves],
        scratch_shapes=[pltpu.SemaphoreType.DMA((n_w,))] * 2,
    )(*halves)


def _small_gather_body(x_ref, out_ref, send_sems, recv_sems, local_sem):
    m_per = x_ref.shape[0]
    x, y, c, chips = _place()
    me, sibling = (x, y, c), (x, y, 1 - c)

    def rows(px, py, pc):
        return out_ref.at[pl.ds((4 * px + 2 * py + pc) * m_per, m_per), :]

    def copy(k, block, to, src=None):
        return pltpu.make_async_remote_copy(
            src_ref=rows(*block) if src is None else src, dst_ref=rows(*block),
            send_sem=send_sems.at[k], recv_sem=recv_sems.at[k], device_id=to, device_id_type=MESH)

    mine = pltpu.make_async_copy(x_ref, rows(*me), local_sem)
    mine.start()
    first = [copy(0, me, sibling, src=x_ref)]
    first += [copy(1 + j, me, (*chip, c), src=x_ref) for j, chip in enumerate(chips)]
    for cp in first:
        cp.start()
    passed = [copy(4 + j, (*chip, c), sibling) for j, chip in enumerate(chips)]
    for j, chip in enumerate(chips):
        copy(1 + j, (*chip, c), me).wait_recv()
        passed[j].start()
    copy(0, sibling, me).wait_recv()
    for j, chip in enumerate(chips):
        copy(4 + j, (*chip, 1 - c), me).wait_recv()
    for cp in first + passed:
        cp.wait_send()
    mine.wait()


def _small_gather(small):
    m_per, n = small.shape
    return _pcall(
        _small_gather_body, name="small_allgather",
        out_shape=jax.ShapeDtypeStruct((N_DEV * m_per, n), small.dtype),
        in_specs=[pl.BlockSpec(memory_space=pltpu.VMEM)],
        out_specs=pl.BlockSpec(memory_space=pltpu.VMEM),
        scratch_shapes=[pltpu.SemaphoreType.DMA((7,)), pltpu.SemaphoreType.DMA((7,)), pltpu.SemaphoreType.DMA],
    )(small)


def _adamw(w, g, m, v):
    m = ADAM_B1 * m + (1.0 - ADAM_B1) * g
    v = ADAM_B2 * v + (1.0 - ADAM_B2) * (g * g)
    m_hat = m / (1.0 - ADAM_B1 ** ADAM_STEP)
    v_hat = v / (1.0 - ADAM_B2 ** ADAM_STEP)
    delta = -ADAM_LR * (m_hat / (jnp.sqrt(v_hat) + ADAM_EPS) + ADAM_WD * w)
    return delta, m, v


def _adamw_body(c_ref, w_ref, own_ref, sib_ref, m_ref, v_ref, g_ref, d_ref, nm_ref, nv_ref, *, nh):
    mine = (pl.program_id(0) // nh) == c_ref[0]
    g = jnp.where(mine, own_ref[...], sib_ref[...])
    g_ref[...] = g
    d, m, v = _adamw(w_ref[...], g, m_ref[...], v_ref[...])
    d_ref[...] = d
    nm_ref[...] = m
    nv_ref[...] = v


def _adamw_call(c_arr, w, own, sib, m, v, name):
    rows, cols = w.shape
    tr = min(rows // 2, 256)
    nh = (rows // 2) // tr
    full = pl.BlockSpec((tr, cols), lambda i, c: (i, 0))
    own_spec = pl.BlockSpec((tr, cols), lambda i, c: (jnp.clip(i - c[0] * nh, 0, nh - 1), 0))
    sib_spec = pl.BlockSpec((tr, cols), lambda i, c: (jnp.clip(i - (1 - c[0]) * nh, 0, nh - 1), 0))
    grid_spec = pltpu.PrefetchScalarGridSpec(
        num_scalar_prefetch=1, grid=(rows // tr,),
        in_specs=[full, own_spec, sib_spec, full, full], out_specs=[full] * 4)
    return _pcall(
        functools.partial(_adamw_body, nh=nh), name=name, grid_spec=grid_spec,
        out_shape=[jax.ShapeDtypeStruct(w.shape, F32)] * 4,
        compiler_params=pltpu.CompilerParams(dimension_semantics=("parallel",)),
    )(c_arr, w, own, sib, m, v)


def _small_update_body(gath_ref, w_ref, m_ref, v_ref, g_ref, d_ref, nm_ref, nv_ref, loss_ref, *, n_gain):
    tot = gath_ref[0:1, :]
    for i in range(1, gath_ref.shape[0]):
        tot = tot + gath_ref[i:i + 1, :]
    g = tot[:, 0:n_gain]
    g_ref[...] = g
    d, m, v = _adamw(w_ref[...], g, m_ref[...], v_ref[...])
    d_ref[...] = d
    nm_ref[...] = m
    nv_ref[...] = v
    loss_ref[...] = (0.5 / D_MODEL) * jnp.sum(tot[:, n_gain:n_gain + HEAD], axis=1, keepdims=True) * jnp.ones((1, HEAD), F32)


def _small_update(gath, w, m, v):
    n_gain = w.shape[1]
    vm = pl.BlockSpec(memory_space=pltpu.VMEM)
    return _pcall(
        functools.partial(_small_update_body, n_gain=n_gain), name="gain_update",
        in_specs=[vm] * 4, out_specs=[vm] * 5,
        out_shape=[jax.ShapeDtypeStruct((1, n_gain), F32)] * 4 + [jax.ShapeDtypeStruct((1, HEAD), F32)],
    )(gath, w, m, v)


def kernel(x, positions, norm_attn_pre, norm_attn_post, w_in, q_latent_norm, kv_latent_norm, w_uq, w_ukv, w_out, norm_mlp_pre, norm_mlp_post, w_up, w_down, loss_target, m_norm_attn_pre, m_norm_attn_post, m_w_in, m_q_latent_norm, m_kv_latent_norm, m_w_uq, m_w_ukv, m_w_out, m_norm_mlp_pre, m_norm_mlp_post, m_w_up, m_w_down, v_norm_attn_pre, v_norm_attn_post, v_w_in, v_q_latent_norm, v_kv_latent_norm, v_w_uq, v_w_ukv, v_w_out, v_norm_mlp_pre, v_norm_mlp_post, v_w_up, v_w_down):
    T = x.shape[1]
    c_arr = lax.axis_index("c").astype(jnp.int32).reshape(1)
    me_arr = (2 * lax.axis_index("x") + lax.axis_index("y")).astype(jnp.int32).reshape(1)
    names = ["w_in", "w_uq", "w_ukv", "w_out", "w_up", "w_down"]

    mats = [w_in[0], w_uq[0], w_ukv[0], w_out[0], w_up[0], w_down[0]]
    placed = [_cast_place(me_arr, w, "cast_" + n) for w, n in zip(mats, names)]
    win_g, wuq_g, wukv_g, wout_g = _allgather_weights(placed[:4])
    mlp_send, mlp_recv, mlp_bufs, started = _ag_start(placed[4:], win_g)

    col_major = lambda g: jnp.transpose(g, (1, 0, 2)).reshape(g.shape[1], N_CHIPS * g.shape[2])
    win_full = col_major(win_g)
    w_main = win_full[:, :MAIN_COLS]
    w_kr = jnp.pad(win_full[:, MAIN_COLS:], ((0, 0), (0, HEAD - ROPE_B)))
    wuq_full = col_major(wuq_g).reshape(LORA, NH, HEAD + ROPE_B)
    w_uq_p = jnp.pad(wuq_full, ((0, 0), (0, 0), (0, QPAD - HEAD - ROPE_B))).reshape(LORA, NH * QPAD)
    w_ukv_p = col_major(wukv_g).reshape(LORA, NH, 2, HEAD).transpose(0, 2, 1, 3).reshape(LORA, 2 * A_W)
    w_out_f = wout_g.reshape(2 * A_W, D_MODEL)
    cast = lambda a: a.astype(MXU_DTYPE)
    to_shards = lambda g: jnp.transpose(g.reshape(g.shape[0], N_CHIPS, g.shape[1] // N_CHIPS), (1, 0, 2))
    halved = lambda g: g.reshape(N_CHIPS, 2, g.shape[1] // 2, g.shape[2])

    def pair_sum(full4, ns):
        from_sib = _pair_send(full4, "_".join(ns))
        return [_pair_add(c_arr, g4, r, "pair_add_" + n) for g4, r, n in zip(full4, from_sib, ns)]

    def mlp_weights(after):
        wup_g, wdown_g = _ag_forward(_ag_wait(mlp_send, mlp_recv, mlp_bufs, after))
        return cast(col_major(wup_g)), cast(wdown_g.reshape(D_FF, D_MODEL))

    mlp_scatter = []

    def mlp_grads_ready(gw_up, gw_down):
        parts = pair_sum([halved(to_shards(gw_up)), halved(gw_down.reshape(N_CHIPS, D_MODEL, D_MODEL))], names[4:])
        mlp_scatter.extend(_scatter_start(parts, started))
        return mlp_scatter[-1][0:1, 0:1]

    dx, gws, small = _local_step(
        x[0], positions[0].astype(F32).reshape(T, 1), loss_target[0],
        norm_attn_pre + started[0:1, 0:1], norm_attn_post, q_latent_norm, kv_latent_norm, norm_mlp_pre, norm_mlp_post,
        cast(w_main), cast(w_kr), cast(w_uq_p), cast(w_ukv_p), cast(w_out_f), mlp_weights, mlp_grads_ready)
    gw_main, gw_kr, gw_uq_p, gw_ukv_p, gw_out = gws

    gw_in = to_shards(jnp.concatenate([gw_main, gw_kr[:, :ROPE_B]], axis=1))
    gw_uq = to_shards(gw_uq_p.reshape(LORA, NH, QPAD)[:, :, :HEAD + ROPE_B].reshape(LORA, NH * (HEAD + ROPE_B)))
    gw_ukv = to_shards(gw_ukv_p.reshape(LORA, 2, NH, HEAD).transpose(0, 2, 1, 3).reshape(LORA, 2 * A_W))
    full4 = [halved(g) for g in (gw_in, gw_uq, gw_ukv, gw_out.reshape(N_CHIPS, LORA, D_MODEL))]

    parts_a = pair_sum(full4, names[:4])
    landed_a = list(_scatter(parts_a))
    s_send, s_recv, parts_b, lands_b, _ = mlp_scatter
    parts_b, landed_b = _scatter_wait(s_send, s_recv, parts_b, lands_b, landed_a[0])
    halves = [_sum4(me_arr, p, l, "chip_sum_" + n)
              for p, l, n in zip(parts_a + parts_b, landed_a + landed_b, names)]
    from_sib2 = _pair_swap(halves)

    ms = [m_w_in[0], m_w_uq[0], m_w_ukv[0], m_w_out[0], m_w_up[0], m_w_down[0]]
    vs = [v_w_in[0], v_w_uq[0], v_w_ukv[0], v_w_out[0], v_w_up[0], v_w_down[0]]
    upd = [_adamw_call(c_arr, w, own, sib, m, v, "adamw_" + n)
           for w, own, sib, m, v, n in zip(mats, halves, from_sib2, ms, vs, names)]
    grads = [u[0] for u in upd]

    gath = _small_gather(small)
    gains = [norm_attn_pre, norm_attn_post, q_latent_norm, kv_latent_norm, norm_mlp_pre, norm_mlp_post]
    gm = [m_norm_attn_pre, m_norm_attn_post, m_q_latent_norm, m_kv_latent_norm, m_norm_mlp_pre, m_norm_mlp_post]
    gv = [v_norm_attn_pre, v_norm_attn_post, v_q_latent_norm, v_kv_latent_norm, v_norm_mlp_pre, v_norm_mlp_post]
    cat = lambda xs: jnp.concatenate(xs, axis=1)
    g_s, d_s, m_s, v_s, loss_v = _small_update(gath, cat(gains), cat(gm), cat(gv))
    widths = [a.shape[1] for a in gains]
    offs = [sum(widths[:i]) for i in range(len(widths))]
    split = lambda a: [a[:, o:o + w] for o, w in zip(offs, widths)]
    g_gain, d_gain, m_gain, v_gain = split(g_s), split(d_s), split(m_s), split(v_s)

    def ordered(gain_list, mat_list):
        gl, ml = gain_list, [a[None] for a in mat_list]
        return [gl[0], gl[1], ml[0], gl[2], gl[3], ml[1], ml[2], ml[3], gl[4], gl[5], ml[4], ml[5]]

    loss = loss_v[0, 0]
    return (loss, dx[None],
            *ordered(g_gain, grads),
            *ordered(d_gain, [u[1] for u in upd]),
            *ordered(m_gain, [u[2] for u in upd]),
            *ordered(v_gain, [u[3] for u in upd]))
```

```python
import functools

import jax
import jax.numpy as jnp
from jax import lax
from jax.experimental import pallas as pl
from jax.experimental.pallas import tpu as pltpu

F32 = jnp.float32
BF16 = jnp.bfloat16
MXU_DTYPE = jnp.bfloat16
WIRE_DTYPE = jnp.bfloat16

D_MODEL = 2048
HEAD = 128
NH = 8
A_W = NH * HEAD
LORA = 512
ROPE_B = 64
QPAD = 256
MAIN_COLS = 3 * A_W + 2 * LORA
IN_COLS = MAIN_COLS + ROPE_B
D_FF = 4 * D_MODEL
DIL = (1, 4, 16)
ROT_A = 32
ROPE_THETA = 500000.0
EPS = 1e-6
NEG = -1e30
N_CHIPS = 4
N_DEV = 8

ADAM_LR = 0.001
ADAM_B1 = 0.9
ADAM_B2 = 0.999
ADAM_EPS = 1e-08
ADAM_WD = 0.01
ADAM_STEP = 10

MESH = pl.DeviceIdType.MESH
ANY = pl.BlockSpec(memory_space=pl.ANY)


def _pcall(body, **kw):
    return pl.pallas_call(body, **kw)


_DIMS = {
    "nn": (((1,), (0,)), ((), ())),
    "nt": (((1,), (1,)), ((), ())),
    "tn": (((0,), (0,)), ((), ())),
}


def _mm_body(*refs, dims, nk, epi, n_extra, n_out):
    a_ref, b_ref = refs[0], refs[1]
    extra = refs[2:2 + n_extra]
    outs = refs[2 + n_extra:2 + n_extra + n_out]
    part = lax.dot_general(a_ref[...], b_ref[...], _DIMS[dims], preferred_element_type=F32)

    def finish(acc):
        res = epi(acc, *[r[...] for r in extra]) if epi is not None else (acc,)
        for o_ref, o in zip(outs, res):
            o_ref[...] = o.astype(o_ref.dtype)

    if nk == 1:
        finish(part)
        return
    acc_ref = refs[-1]
    k = pl.program_id(2)

    @pl.when(k == 0)
    def _():
        acc_ref[...] = part

    @pl.when(k > 0)
    def _():
        acc_ref[...] += part

    @pl.when(k == nk - 1)
    def _():
        finish(acc_ref[...])


def _matmul(a, b, *, dims, out_dtypes, tm, tn, tk, name, epi=None, extras=(), b_outer=False):
    if dims == "nn":
        (M, K), (K2, N) = a.shape, b.shape
    elif dims == "nt":
        (M, K), (N, K2) = a.shape, b.shape
    else:
        (K, M), (K2, N) = a.shape, b.shape
    assert K == K2, (a.shape, b.shape, dims)
    tm, tn, tk = min(tm, M), min(tn, N), min(tk, K)
    assert M % tm == 0 and N % tn == 0 and K % tk == 0, (name, M, N, K, tm, tn, tk)
    nk = K // tk

    def at(f):
        if b_outer:
            return lambda j, i, k: f(i, j, k)
        return f

    a_spec = {"nn": pl.BlockSpec((tm, tk), at(lambda i, j, k: (i, k))),
              "nt": pl.BlockSpec((tm, tk), at(lambda i, j, k: (i, k))),
              "tn": pl.BlockSpec((tk, tm), at(lambda i, j, k: (k, i)))}[dims]
    b_spec = {"nn": pl.BlockSpec((tk, tn), at(lambda i, j, k: (k, j))),
              "nt": pl.BlockSpec((tn, tk), at(lambda i, j, k: (j, k))),
              "tn": pl.BlockSpec((tk, tn), at(lambda i, j, k: (k, j)))}[dims]
    o_spec = pl.BlockSpec((tm, tn), at(lambda i, j, k: (i, j)))
    body = functools.partial(_mm_body, dims=dims, nk=nk, epi=epi,
                             n_extra=len(extras), n_out=len(out_dtypes))
    res = _pcall(
        body, name=name,
        grid=(N // tn, M // tm, nk) if b_outer else (M // tm, N // tn, nk),
        in_specs=[a_spec, b_spec] + [o_spec] * len(extras),
        out_specs=[o_spec] * len(out_dtypes),
        out_shape=[jax.ShapeDtypeStruct((M, N), dt) for dt in out_dtypes],
        scratch_shapes=[pltpu.VMEM((tm, tn), F32)] if nk > 1 else [],
        compiler_params=pltpu.CompilerParams(
            dimension_semantics=("parallel", "parallel", "arbitrary")),
    )(a, b, *extras)
    return list(res)


def _rowwise(body, row_ins, vec_ins, row_outs, acc_outs, *, tr, name):
    T = row_ins[0].shape[0]
    tr = min(tr, T)
    assert T % tr == 0
    in_specs = [pl.BlockSpec((tr, a.shape[1]), lambda i: (i, 0)) for a in row_ins]
    in_specs += [pl.BlockSpec(a.shape, lambda i: (0, 0)) for a in vec_ins]
    out_specs = [pl.BlockSpec((tr, w), lambda i: (i, 0)) for (w, _) in row_outs]
    out_specs += [pl.BlockSpec(s, lambda i: (0, 0)) for s in acc_outs]
    out_shape = [jax.ShapeDtypeStruct((T, w), dt) for (w, dt) in row_outs]
    out_shape += [jax.ShapeDtypeStruct(s, F32) for s in acc_outs]
    sem = "arbitrary" if acc_outs else "parallel"
    return list(_pcall(
        body, name=name, grid=(T // tr,), in_specs=in_specs, out_specs=out_specs,
        out_shape=out_shape,
        compiler_params=pltpu.CompilerParams(dimension_semantics=(sem,)),
    )(*row_ins, *vec_ins))


def _rstd(x):
    return lax.rsqrt(jnp.mean(x * x, axis=-1, keepdims=True) + EPS)


def _rms_bwd(x, rstd, dyg):
    xh = x * rstd
    return rstd * (dyg - xh * jnp.mean(dyg * xh, axis=-1, keepdims=True)), xh


def _fold8(v):
    r, w = v.shape
    return jnp.sum(v.reshape(r // 8, 8, w), axis=0)


def _acc(ref, val):
    first = pl.program_id(0) == 0

    @pl.when(first)
    def _():
        ref[...] = val

    @pl.when(jnp.logical_not(first))
    def _():
        ref[...] += val


def _rope(x, c, sa, sb, half):
    return x * c + pltpu.roll(x, HEAD - half, 1) * sa + pltpu.roll(x, half, 1) * sb


def _rope_t(dy, c, sa, sb, half):
    return dy * c - pltpu.roll(dy, HEAD - half, 1) * sa - pltpu.roll(dy, half, 1) * sb


def _rope_tab_body(pos_ref, inv_ref, ca, saa, sab, cb, sba, sbb):
    pos = pos_ref[...]
    lane = lax.broadcasted_iota(jnp.int32, (pos.shape[0], HEAD), 1)
    ang_a = pos * inv_ref[0:1, :]
    ang_b = pos * inv_ref[1:2, :]
    c, s = jnp.cos(ang_a), jnp.sin(ang_a)
    ha = ROT_A // 2
    ca[...] = jnp.where(lane < ROT_A, c, 1.0)
    saa[...] = jnp.where(lane < ha, -s, 0.0)
    sab[...] = jnp.where((lane >= ha) & (lane < ROT_A), s, 0.0)
    c, s = jnp.cos(ang_b), jnp.sin(ang_b)
    hb = ROPE_B // 2
    cb[...] = jnp.where(lane < ROPE_B, c, 1.0)
    sba[...] = jnp.where(lane < hb, -s, 0.0)
    sbb[...] = jnp.where((lane >= hb) & (lane < ROPE_B), s, 0.0)


def _rms_fwd_body(x_ref, g_ref, h_ref):
    x = x_ref[...]
    h_ref[...] = ((x * _rstd(x)) * g_ref[...]).astype(h_ref.dtype)


def _postproj_body(p_ref, kr_ref, ca, saa, sab, cb, sba, sbb, gq_ref, gkv_ref,
                   q_ref, k_ref, v_ref, cqn_ref, ckvn_ref, krope_ref):
    c, sa, sb = ca[...], saa[...], sab[...]
    for h in range(NH):
        lo = h * HEAD
        q_ref[:, lo:lo + HEAD] = _rope(p_ref[:, lo:lo + HEAD], c, sa, sb, ROT_A // 2).astype(q_ref.dtype)
        k_ref[:, lo:lo + HEAD] = _rope(p_ref[:, A_W + lo:A_W + lo + HEAD], c, sa, sb, ROT_A // 2).astype(k_ref.dtype)
    v_ref[...] = p_ref[:, 2 * A_W:3 * A_W].astype(v_ref.dtype)
    cq = p_ref[:, 3 * A_W:3 * A_W + LORA]
    cqn_ref[...] = ((cq * _rstd(cq)) * gq_ref[...]).astype(cqn_ref.dtype)
    ckv = p_ref[:, 3 * A_W + LORA:MAIN_COLS]
    ckvn_ref[...] = ((ckv * _rstd(ckv)) * gkv_ref[...]).astype(ckvn_ref.dtype)
    krope_ref[...] = _rope(kr_ref[...], cb[...], sba[...], sbb[...], ROPE_B // 2).astype(krope_ref.dtype)


def _qrope_body(qp_ref, cb, sba, sbb, q_ref):
    c, sa, sb = cb[...], sba[...], sbb[...]
    for h in range(NH):
        lo = h * QPAD
        q_ref[:, lo:lo + HEAD] = qp_ref[:, lo:lo + HEAD].astype(q_ref.dtype)
        q_ref[:, lo + HEAD:lo + QPAD] = _rope(qp_ref[:, lo + HEAD:lo + QPAD], c, sa, sb, ROPE_B // 2).astype(q_ref.dtype)


def _qrope_t_body(dq_ref, cb, sba, sbb, o_ref):
    c, sa, sb = cb[...], sba[...], sbb[...]
    for h in range(NH):
        lo = h * QPAD
        o_ref[:, lo:lo + HEAD] = dq_ref[:, lo:lo + HEAD].astype(o_ref.dtype)
        o_ref[:, lo + HEAD:lo + QPAD] = _rope_t(dq_ref[:, lo + HEAD:lo + QPAD], c, sa, sb, ROPE_B // 2).astype(o_ref.dtype)


def _mid_body(x_ref, o_ref, g2_ref, g3_ref, x1_ref, h2_ref):
    o = o_ref[...]
    x1 = x_ref[...] + (o * _rstd(o)) * g2_ref[...]
    x1_ref[...] = x1
    h2_ref[...] = ((x1 * _rstd(x1)) * g3_ref[...]).astype(h2_ref.dtype)


def _loss_body(x1_ref, d_ref, t_ref, g4_ref, dy_ref, dd_ref, loss_ref, dg4_ref):
    d = d_ref[...]
    rstd = _rstd(d)
    y = x1_ref[...] + (d * rstd) * g4_ref[...]
    e = y - t_ref[...]
    dy = e * (1.0 / D_MODEL)
    dy_ref[...] = dy
    dd, dh = _rms_bwd(d, rstd, dy * g4_ref[...])
    dd_ref[...] = dd.astype(dd_ref.dtype)
    _acc(dg4_ref, _fold8(dy * dh))
    e8 = _fold8(e * e)
    l = e8[:, 0:HEAD]
    for j in range(1, D_MODEL // HEAD):
        l = l + e8[:, j * HEAD:(j + 1) * HEAD]
    _acc(loss_ref, l)


def _bmid_body(dy_ref, dh2_ref, x1_ref, o_ref, g2_ref, g3_ref, dx1_ref, do_ref, dg3_ref, dg2_ref):
    x1 = x1_ref[...]
    dh2 = dh2_ref[...]
    dn, x1h = _rms_bwd(x1, _rstd(x1), dh2 * g3_ref[...])
    dx1 = dy_ref[...] + dn
    dx1_ref[...] = dx1
    _acc(dg3_ref, _fold8(dh2 * x1h))
    o = o_ref[...]
    do, oh = _rms_bwd(o, _rstd(o), dx1 * g2_ref[...])
    do_ref[...] = do.astype(do_ref.dtype)
    _acc(dg2_ref, _fold8(dx1 * oh))


def _dproj_body(dq_ref, dk_ref, dv_ref, dcq_ref, dckv_ref, p_ref, dkr_ref,
                ca, saa, sab, cb, sba, sbb, gq_ref, gkv_ref,
                dp_ref, dkrp_ref, dgq_ref, dgkv_ref):
    c, sa, sb = ca[...], saa[...], sab[...]
    for h in range(NH):
        lo = h * HEAD
        dp_ref[:, lo:lo + HEAD] = _rope_t(dq_ref[:, lo:lo + HEAD], c, sa, sb, ROT_A // 2).astype(dp_ref.dtype)
        dp_ref[:, A_W + lo:A_W + lo + HEAD] = _rope_t(dk_ref[:, lo:lo + HEAD], c, sa, sb, ROT_A // 2).astype(dp_ref.dtype)
    dp_ref[:, 2 * A_W:3 * A_W] = dv_ref[...].astype(dp_ref.dtype)
    cq = p_ref[:, 3 * A_W:3 * A_W + LORA]
    dcqn = dcq_ref[...]
    dcq, cqh = _rms_bwd(cq, _rstd(cq), dcqn * gq_ref[...])
    dp_ref[:, 3 * A_W:3 * A_W + LORA] = dcq.astype(dp_ref.dtype)
    _acc(dgq_ref, _fold8(dcqn * cqh))
    ckv = p_ref[:, 3 * A_W + LORA:MAIN_COLS]
    dckvn = dckv_ref[...]
    dckv, ckvh = _rms_bwd(ckv, _rstd(ckv), dckvn * gkv_ref[...])
    dp_ref[:, 3 * A_W + LORA:MAIN_COLS] = dckv.astype(dp_ref.dtype)
    _acc(dgkv_ref, _fold8(dckvn * ckvh))
    dkr = dkr_ref[:, 0:HEAD]
    for h in range(1, NH):
        dkr = dkr + dkr_ref[:, h * HEAD:(h + 1) * HEAD]
    dkrp_ref[...] = _rope_t(dkr, cb[...], sba[...], sbb[...], ROPE_B // 2).astype(dkrp_ref.dtype)


def _bin_body(dx1_ref, dha_ref, dhb_ref, x_ref, g1_ref, dx_ref, dg1_ref):
    x = x_ref[...]
    dh = dha_ref[...] + dhb_ref[...]
    dn, xh = _rms_bwd(x, _rstd(x), dh * g1_ref[...])
    dx_ref[...] = dx1_ref[...] + dn
    _acc(dg1_ref, _fold8(dh * xh))


def _dot_nt(a, b):
    return lax.dot_general(a, b, _DIMS["nt"], preferred_element_type=F32)


def _dot_tn(a, b):
    return lax.dot_general(a, b, _DIMS["tn"], preferred_element_type=F32)


def _dot_nn(a, b):
    return jnp.dot(a, b, preferred_element_type=F32)


DIL_SCALE = HEAD ** -0.5
DIL_CHUNK = 256


def _dil_rows(t, d):
    r = t & (d - 1)
    n = t >> (d.bit_length() - 1)
    start = r + n * (HEAD * d)
    has_prev = n > 0
    pstart = jnp.where(has_prev, start - HEAD * d, start)
    if d == 1:
        return pl.ds(pl.multiple_of(start, HEAD), HEAD), pl.ds(pl.multiple_of(pstart, HEAD), HEAD), has_prev
    return pl.ds(start, HEAD, stride=d), pl.ds(pstart, HEAD, stride=d), has_prev


def _dil_band():
    row = lax.broadcasted_iota(jnp.int32, (HEAD, 2 * HEAD), 0)
    col = lax.broadcasted_iota(jnp.int32, (HEAD, 2 * HEAD), 1)
    return (col >= row) & (col <= row + HEAD), col >= HEAD


def _dil_fwd_body(q_ref, k_ref, v_ref, a_ref, lse_ref, o1, o2, o3, l1, l2, l3, *, nt, unroll):
    band, is_cur = _dil_band()
    for d, o_sc, l_sc in zip(DIL, (o1, o2, o3), (l1, l2, l3)):

        def tile(t, carry, d=d, o_sc=o_sc, l_sc=l_sc):
            rows, prows, has_prev = _dil_rows(t, d)
            q = q_ref[rows, :].astype(MXU_DTYPE)
            kk = jnp.concatenate([k_ref[prows, :], k_ref[rows, :]], axis=0).astype(MXU_DTYPE)
            vv = jnp.concatenate([v_ref[prows, :], v_ref[rows, :]], axis=0).astype(MXU_DTYPE)
            ok = band & (is_cur | has_prev)
            s = jnp.where(ok, _dot_nt(q, kk) * DIL_SCALE, NEG)
            m = jnp.max(s, axis=1, keepdims=True)
            p = jnp.exp(s - m)
            den = jnp.sum(p, axis=1, keepdims=True)
            o_sc[rows, :] = _dot_nn((p / den).astype(MXU_DTYPE), vv)
            l_sc[rows, :] = jnp.broadcast_to(m + jnp.log(den), (HEAD, HEAD))
            return carry

        lax.fori_loop(0, nt, tile, 0, unroll=unroll)

    def merge(i, carry):
        rs = pl.ds(pl.multiple_of(i * DIL_CHUNK, DIL_CHUNK), DIL_CHUNK)
        la, lb, lc = l1[rs, :], l2[rs, :], l3[rs, :]
        m = jnp.maximum(jnp.maximum(la, lb), lc)
        wa, wb, wc = jnp.exp(la - m), jnp.exp(lb - m), jnp.exp(lc - m)
        den = wa + wb + wc
        a = (wa / den) * o1[rs, :] + (wb / den) * o2[rs, :] + (wc / den) * o3[rs, :]
        a_ref[rs, :] = a.astype(a_ref.dtype)
        lse_ref[rs, :] = m + jnp.log(den)
        return carry

    lax.fori_loop(0, q_ref.shape[0] // DIL_CHUNK, merge, 0)


def _dil_fwd(q, k, v):
    T = q.shape[0]
    spec = pl.BlockSpec((T, HEAD), lambda h: (0, h))
    return _pcall(
        functools.partial(_dil_fwd_body, nt=T // HEAD, unroll=4), name="dil_fwd",
        grid=(NH,), in_specs=[spec] * 3, out_specs=[spec] * 2,
        out_shape=[jax.ShapeDtypeStruct((T, A_W), MXU_DTYPE), jax.ShapeDtypeStruct((T, A_W), F32)],
        scratch_shapes=[pltpu.VMEM((T, HEAD), F32)] * 6,
        compiler_params=pltpu.CompilerParams(dimension_semantics=("parallel",)),
    )(q, k, v)


def _dil_bwd_body(q_ref, k_ref, v_ref, do_ref, a_ref, lse_ref, dq_ref, dk_ref, dv_ref, dl_sc, *, nt, unroll):
    band, is_cur = _dil_band()

    def prep(i, carry):
        rs = pl.ds(pl.multiple_of(i * DIL_CHUNK, DIL_CHUNK), DIL_CHUNK)
        dl = jnp.sum(do_ref[rs, :] * a_ref[rs, :].astype(F32), axis=1, keepdims=True)
        dl_sc[rs, :] = jnp.broadcast_to(dl, (DIL_CHUNK, HEAD))
        zero = jnp.zeros((DIL_CHUNK, HEAD), F32)
        dq_ref[rs, :] = zero
        dk_ref[rs, :] = zero
        dv_ref[rs, :] = zero
        return carry

    lax.fori_loop(0, q_ref.shape[0] // DIL_CHUNK, prep, 0)

    for d in DIL:

        def tile(t, carry, d=d):
            rows, prows, has_prev = _dil_rows(t, d)
            q = q_ref[rows, :].astype(MXU_DTYPE)
            kk = jnp.concatenate([k_ref[prows, :], k_ref[rows, :]], axis=0).astype(MXU_DTYPE)
            vv = jnp.concatenate([v_ref[prows, :], v_ref[rows, :]], axis=0).astype(MXU_DTYPE)
            do = do_ref[rows, :].astype(MXU_DTYPE)
            lse = lse_ref[rows, :]
            dl = dl_sc[rows, :]
            ok = band & (is_cur | has_prev)
            s = _dot_nt(q, kk) * DIL_SCALE
            p = jnp.where(ok, jnp.exp(s - jnp.concatenate([lse, lse], axis=1)), 0.0)
            ds = (p * (_dot_nt(do, vv) - jnp.concatenate([dl, dl], axis=1))).astype(MXU_DTYPE)
            dq_ref[rows, :] += _dot_nn(ds, kk) * DIL_SCALE
            dkk = _dot_tn(ds, q) * DIL_SCALE
            dvv = _dot_tn(p.astype(MXU_DTYPE), do)
            dk_ref[rows, :] += dkk[HEAD:, :]
            dv_ref[rows, :] += dvv[HEAD:, :]
            dk_ref[prows, :] += dkk[:HEAD, :]
            dv_ref[prows, :] += dvv[:HEAD, :]
            return carry

        lax.fori_loop(0, nt, tile, 0, unroll=unroll)


def _dil_bwd(q, k, v, dmix, mixed, lse):
    T = q.shape[0]
    spec = pl.BlockSpec((T, HEAD), lambda h: (0, h))
    return _pcall(
        functools.partial(_dil_bwd_body, nt=T // HEAD, unroll=2), name="dil_bwd",
        grid=(NH,), in_specs=[spec] * 6, out_specs=[spec] * 3,
        out_shape=[jax.ShapeDtypeStruct((T, A_W), F32)] * 3,
        scratch_shapes=[pltpu.VMEM((T, HEAD), F32)],
        compiler_params=pltpu.CompilerParams(dimension_semantics=("parallel",)),
    )(q, k, v, dmix, mixed, lse)


MLA_SCALE = (HEAD + ROPE_B) ** -0.5
MLA_T = 512
MLA_HP = 2


def _tri(t):
    row = lax.broadcasted_iota(jnp.int32, (t, t), 0)
    col = lax.broadcasted_iota(jnp.int32, (t, t), 1)
    return col <= row


def _lanes(x, n):
    return jnp.tile(x, (1, n // HEAD))


def _mla_fwd_body(q_ref, kn_ref, kr_ref, v_ref, o_ref, lse_ref, m_sc, l_sc, acc_sc, *, t, hp):
    qi = pl.program_id(1)
    m_sc[...] = jnp.full(m_sc.shape, NEG, F32)
    l_sc[...] = jnp.zeros(l_sc.shape, F32)
    acc_sc[...] = jnp.zeros(acc_sc.shape, F32)

    def step(j, masked):
        ks = pl.ds(pl.multiple_of(j * t, t), t)
        kr = kr_ref[ks, :]
        for hh in range(hp):
            kcat = jnp.concatenate([kn_ref[ks, hh * HEAD:(hh + 1) * HEAD], kr], axis=1)
            s = _dot_nt(q_ref[:, hh * QPAD:(hh + 1) * QPAD], kcat) * MLA_SCALE
            if masked:
                s = jnp.where(_tri(t), s, NEG)
            m_prev = m_sc[hh]
            m_new = jnp.maximum(m_prev, jnp.max(s, axis=1, keepdims=True))
            alpha = jnp.exp(m_prev - m_new)
            p = jnp.exp(s - _lanes(m_new, t))
            l_sc[hh] = alpha * l_sc[hh] + jnp.sum(p, axis=1, keepdims=True)
            acc_sc[hh] = alpha * acc_sc[hh] + _dot_nn(p.astype(MXU_DTYPE), v_ref[ks, hh * HEAD:(hh + 1) * HEAD])
            m_sc[hh] = m_new

    def off_diag(j, carry):
        step(j, False)
        return carry

    lax.fori_loop(0, qi, off_diag, 0)
    step(qi, True)
    for hh in range(hp):
        l = l_sc[hh]
        o_ref[:, hh * HEAD:(hh + 1) * HEAD] = (acc_sc[hh] / l).astype(o_ref.dtype)
        lse_ref[:, hh * HEAD:(hh + 1) * HEAD] = m_sc[hh] + jnp.log(l)


def _mla_fwd(qf, kv, kr):
    T = qf.shape[0]
    t, hp = min(MLA_T, T), MLA_HP
    ng = NH // hp
    return _pcall(
        functools.partial(_mla_fwd_body, t=t, hp=hp), name="mla_fwd",
        grid=(ng, T // t),
        in_specs=[pl.BlockSpec((t, hp * QPAD), lambda g, i: (i, g)),
                  pl.BlockSpec((T, hp * HEAD), lambda g, i: (0, g)),
                  pl.BlockSpec((T, HEAD), lambda g, i: (0, 0)),
                  pl.BlockSpec((T, hp * HEAD), lambda g, i: (0, ng + g))],
        out_specs=[pl.BlockSpec((t, hp * HEAD), lambda g, i: (i, g))] * 2,
        out_shape=[jax.ShapeDtypeStruct((T, A_W), MXU_DTYPE), jax.ShapeDtypeStruct((T, A_W), F32)],
        scratch_shapes=[pltpu.VMEM((hp, t, HEAD), F32)] * 3,
        compiler_params=pltpu.CompilerParams(dimension_semantics=("parallel", "parallel")),
    )(qf, kv, kr, kv)


def _mla_bwd_body(q_ref, kn_ref, kr_ref, v_ref, do_ref, o_ref, lse_ref, dq_ref, dkn_ref, dv_ref, dkr_ref,
                  dl_sc, dk_sc, dv_sc, *, t):
    ki = pl.program_id(1)
    nq = q_ref.shape[0] // t

    @pl.when(ki == 0)
    def _():
        def prep(i, carry):
            rs = pl.ds(pl.multiple_of(i * t, t), t)
            dl = jnp.sum(do_ref[rs, :] * o_ref[rs, :].astype(F32), axis=1, keepdims=True)
            dl_sc[rs, :] = jnp.broadcast_to(dl, (t, HEAD))
            dq_ref[rs, :] = jnp.zeros((t, QPAD), F32)
            return carry
        lax.fori_loop(0, nq, prep, 0)

    kcat = jnp.concatenate([kn_ref[...], kr_ref[...]], axis=1)
    v = v_ref[...]
    dk_sc[...] = jnp.zeros(dk_sc.shape, F32)
    dv_sc[...] = jnp.zeros(dv_sc.shape, F32)

    def step(i, masked):
        qs = pl.ds(pl.multiple_of(i * t, t), t)
        q = q_ref[qs, :]
        do = do_ref[qs, :].astype(MXU_DTYPE)
        p = jnp.exp(_dot_nt(q, kcat) * MLA_SCALE - _lanes(lse_ref[qs, :], t))
        if masked:
            p = jnp.where(_tri(t), p, 0.0)
        ds = (p * (_dot_nt(do, v) - _lanes(dl_sc[qs, :], t))).astype(MXU_DTYPE)
        dv_sc[...] += _dot_tn(p.astype(MXU_DTYPE), do)
        dk_sc[...] += _dot_tn(ds, q)
        dq_ref[qs, :] += _dot_nn(ds, kcat) * MLA_SCALE

    step(ki, True)

    def off_diag(i, carry):
        step(i, False)
        return carry

    lax.fori_loop(ki + 1, nq, off_diag, 0)
    dk = dk_sc[...] * MLA_SCALE
    dkn_ref[...] = dk[:, 0:HEAD].astype(dkn_ref.dtype)
    dkr_ref[...] = dk[:, HEAD:QPAD]
    dv_ref[...] = dv_sc[...].astype(dv_ref.dtype)


def _mla_bwd(qf, kv, kr, dmix, mixed, lse):
    T = qf.shape[0]
    t = min(MLA_T, T)
    head = lambda h, j: (0, h)
    b_half = lambda h, j: (0, NH + h)
    kblk = pl.BlockSpec((t, HEAD), lambda h, j: (j, h))
    return _pcall(
        functools.partial(_mla_bwd_body, t=t), name="mla_bwd",
        grid=(NH, T // t),
        in_specs=[pl.BlockSpec((T, QPAD), head), kblk,
                  pl.BlockSpec((t, HEAD), lambda h, j: (j, 0)),
                  pl.BlockSpec((t, HEAD), lambda h, j: (j, NH + h)),
                  pl.BlockSpec((T, HEAD), b_half), pl.BlockSpec((T, HEAD), b_half),
                  pl.BlockSpec((T, HEAD), head)],
        out_specs=[pl.BlockSpec((T, QPAD), head), kblk, kblk, kblk],
        out_shape=[jax.ShapeDtypeStruct((T, NH * QPAD), F32), jax.ShapeDtypeStruct((T, A_W), MXU_DTYPE),
                   jax.ShapeDtypeStruct((T, A_W), MXU_DTYPE), jax.ShapeDtypeStruct((T, A_W), F32)],
        scratch_shapes=[pltpu.VMEM((T, HEAD), F32), pltpu.VMEM((t, QPAD), F32), pltpu.VMEM((t, HEAD), F32)],
        compiler_params=pltpu.CompilerParams(dimension_semantics=("parallel", "arbitrary")),
    )(qf, kv, kr, kv, dmix, mixed, lse)


def _local_step(x, pos, target, g1, g2, gq, gkv, g3, g4, w_main, w_kr,
                attn_weights, mlp_weights, mlp_grads_ready, attn_grads_ready):
    T = x.shape[0]
    TR = 256
    mm = functools.partial(_matmul, tm=512, tn=1024, tk=2048, b_outer=True)
    mm_k = functools.partial(_matmul, tm=1024, tn=1024, tk=2048)
    mm_g = functools.partial(_matmul, tm=512, tn=1024, tk=4096, b_outer=True)

    inv_a = ROPE_THETA ** (-jnp.arange(0, ROT_A, 2, dtype=F32) / ROT_A)
    inv_b = ROPE_THETA ** (-jnp.arange(0, ROPE_B, 2, dtype=F32) / ROPE_B)
    inv = jnp.stack([jnp.concatenate([inv_a, inv_a, jnp.zeros((HEAD - ROT_A,), F32)]),
                     jnp.concatenate([inv_b, inv_b, jnp.zeros((HEAD - ROPE_B,), F32)])])
    inv = jnp.concatenate([inv, jnp.zeros((6, HEAD), F32)], axis=0)
    tabs = _rowwise(_rope_tab_body, [pos], [inv], [(HEAD, F32)] * 6, [], tr=512, name="rope_tables")

    (h,) = _rowwise(_rms_fwd_body, [x], [g1], [(D_MODEL, MXU_DTYPE)], [], tr=TR, name="rms_in")
    (proj,) = mm(h, w_main, dims="nn", out_dtypes=[F32], name="proj_main")
    (kr_raw,) = mm(h, w_kr, dims="nn", out_dtypes=[F32], name="proj_kr")
    q, k, v, cqn, ckvn, krope = _rowwise(
        _postproj_body, [proj, kr_raw] + tabs, [gq, gkv],
        [(A_W, F32)] * 3 + [(LORA, MXU_DTYPE)] * 2 + [(HEAD, MXU_DTYPE)], [], tr=TR, name="post_proj")
    a_out, lse_a = _dil_fwd(q, k, v)

    w_uq_p, w_ukv_p, w_out = attn_weights(cqn)
    (q_pad,) = mm(cqn, w_uq_p, dims="nn", out_dtypes=[F32], name="q_up")
    (qf,) = _rowwise(_qrope_body, [q_pad] + tabs[3:], [], [(NH * QPAD, MXU_DTYPE)], [], tr=TR, name="q_rope")
    (kv,) = mm(ckvn, w_ukv_p, dims="nn", out_dtypes=[MXU_DTYPE], name="kv_up")
    b_out, lse_b = _mla_fwd(qf, kv, krope)

    mixed = jnp.concatenate([a_out, b_out], axis=1)
    (o,) = mm(mixed, w_out, dims="nn", out_dtypes=[F32], name="out_proj")
    x1, h2 = _rowwise(_mid_body, [x, o], [g2, g3], [(D_MODEL, F32), (D_MODEL, MXU_DTYPE)], [], tr=TR, name="mid_norm")

    w_up, w_down = mlp_weights(h2)

    def up_epi(acc):
        r = jnp.maximum(acc, 0.0)
        return r * r, r
    u, r = mm(h2, w_up, dims="nn", out_dtypes=[MXU_DTYPE, MXU_DTYPE], name="mlp_up", epi=up_epi)
    (dn,) = mm_k(u, w_down, dims="nn", out_dtypes=[F32], name="mlp_down")
    dy, dd, loss8, dg4 = _rowwise(_loss_body, [x1, dn, target], [g4], [(D_MODEL, F32), (D_MODEL, MXU_DTYPE)],
                                  [(8, HEAD), (8, D_MODEL)], tr=TR, name="loss_head")

    def dup_epi(acc, rr):
        return (acc * (2.0 * rr.astype(F32)),)
    (dup,) = mm(dd, w_down, dims="nt", out_dtypes=[MXU_DTYPE], name="d_up", epi=dup_epi, extras=(r,))
    (gw_down,) = mm_g(u, dd, dims="tn", out_dtypes=[WIRE_DTYPE], name="gw_down")
    (dh2,) = mm_k(dup, w_up, dims="nt", out_dtypes=[F32], name="d_h2")
    (gw_up,) = mm_g(h2, dup, dims="tn", out_dtypes=[WIRE_DTYPE], name="gw_up")
    g2 = g2 + mlp_grads_ready(gw_up, gw_down)
    dx1, do, dg3, dg2 = _rowwise(_bmid_body, [dy, dh2, x1, o], [g2, g3], [(D_MODEL, F32), (D_MODEL, MXU_DTYPE)],
                                 [(8, D_MODEL), (8, D_MODEL)], tr=TR, name="bwd_mid")
    (dmix,) = mm(do, w_out, dims="nt", out_dtypes=[F32], name="d_mixed")
    (gw_out,) = mm_g(mixed, do, dims="tn", out_dtypes=[WIRE_DTYPE], name="gw_out")

    dqf, dkn, dvb, dkr = _mla_bwd(qf, kv, krope, dmix, mixed, lse_b)
    (dq_pad,) = _rowwise(_qrope_t_body, [dqf] + tabs[3:], [], [(NH * QPAD, MXU_DTYPE)], [], tr=TR, name="q_rope_t")
    (dcqn,) = mm(dq_pad, w_uq_p, dims="nt", out_dtypes=[F32], name="d_cq")
    (gw_uq_p,) = mm_g(cqn, dq_pad, dims="tn", out_dtypes=[WIRE_DTYPE], name="gw_uq")
    dkv = jnp.concatenate([dkn, dvb], axis=1)
    (dckvn,) = mm(dkv, w_ukv_p, dims="nt", out_dtypes=[F32], name="d_ckv")
    (gw_ukv_p,) = mm_g(ckvn, dkv, dims="tn", out_dtypes=[WIRE_DTYPE], name="gw_ukv")
    gq = gq + attn_grads_ready(gw_out, gw_uq_p, gw_ukv_p)

    dq_a, dk_a, dv_a = _dil_bwd(q, k, v, dmix, mixed, lse_a)
    dproj, dkrp, dgq, dgkv = _rowwise(
        _dproj_body, [dq_a, dk_a, dv_a, dcqn, dckvn, proj, dkr] + tabs, [gq, gkv],
        [(MAIN_COLS, MXU_DTYPE), (HEAD, MXU_DTYPE)], [(8, LORA), (8, LORA)], tr=TR, name="d_proj")
    (dha,) = mm_k(dproj, w_main, dims="nt", out_dtypes=[F32], name="d_h_main")
    (dhb,) = mm(dkrp, w_kr, dims="nt", out_dtypes=[F32], name="d_h_kr")
    (gw_main,) = mm_g(h, dproj, dims="tn", out_dtypes=[WIRE_DTYPE], name="gw_in_main")
    (gw_kr,) = mm_g(h, dkrp, dims="tn", out_dtypes=[WIRE_DTYPE], name="gw_in_kr")
    dx, dg1 = _rowwise(_bin_body, [dx1, dha, dhb, x], [g1], [(D_MODEL, F32)], [(8, D_MODEL)], tr=TR, name="bwd_in")

    small = jnp.concatenate([dg1, dg2, dgq, dgkv, dg3, dg4, loss8], axis=1)
    return dx, (gw_main, gw_kr), small


def _place():
    x, y, c = lax.axis_index("x"), lax.axis_index("y"), lax.axis_index("c")
    chips = [(1 - x, y), (x, 1 - y), (1 - x, 1 - y)]
    return x, y, c, chips


def _cast_place_body(me_ref, w_ref, o_ref):
    o_ref[...] = w_ref[...].astype(o_ref.dtype)


def _cast_place(me_arr, w, name):
    rows, cols = w.shape
    tr = min(rows, 256)
    grid_spec = pltpu.PrefetchScalarGridSpec(
        num_scalar_prefetch=1, grid=(rows // tr,),
        in_specs=[pl.BlockSpec((tr, cols), lambda i, me: (i, 0))],
        out_specs=pl.BlockSpec((None, tr, cols), lambda i, me: (me[0], i, 0)))
    return _pcall(
        _cast_place_body, name=name, grid_spec=grid_spec,
        out_shape=jax.ShapeDtypeStruct((N_CHIPS, rows, cols), WIRE_DTYPE),
        compiler_params=pltpu.CompilerParams(dimension_semantics=("parallel",)),
    )(me_arr, w)


def _ag_body(*refs, n_w):
    bufs = refs[n_w:2 * n_w]
    send_sems, recv_sems, fsend_sems, frecv_sems = refs[2 * n_w:]
    x, y, c, chips = _place()
    me = 2 * x + y
    sib = (x, y, 1 - c)

    def half_rows(w, which):
        half = bufs[w].shape[1] // 2
        return pl.ds(pl.multiple_of(which * half, 16), half)

    sends, fwds = [], []
    for w in range(n_w):
        mine = bufs[w].at[me, half_rows(w, c)]
        for j, (px, py) in enumerate(chips):
            cp = pltpu.make_async_remote_copy(
                src_ref=mine, dst_ref=mine,
                send_sem=send_sems.at[w * 3 + j], recv_sem=recv_sems.at[w * 3 + j],
                device_id=(px, py, c), device_id_type=MESH)
            cp.start()
            sends.append(cp)
    for w in range(n_w):
        for j, (px, py) in enumerate(chips):
            landed = bufs[w].at[2 * px + py, half_rows(w, c)]
            pltpu.make_async_remote_copy(
                src_ref=landed, dst_ref=landed,
                send_sem=send_sems.at[w * 3 + j], recv_sem=recv_sems.at[w * 3 + j],
                device_id=(px, py, c), device_id_type=MESH).wait_recv()
            fw = pltpu.make_async_remote_copy(
                src_ref=landed, dst_ref=landed,
                send_sem=fsend_sems.at[w * 3 + j], recv_sem=frecv_sems.at[w * 3 + j],
                device_id=sib, device_id_type=MESH)
            fw.start()
            fwds.append(fw)
    for w in range(n_w):
        for j, (px, py) in enumerate(chips):
            passed = bufs[w].at[2 * px + py, half_rows(w, 1 - c)]
            pltpu.make_async_remote_copy(
                src_ref=passed, dst_ref=passed,
                send_sem=fsend_sems.at[w * 3 + j], recv_sem=frecv_sems.at[w * 3 + j],
                device_id=sib, device_id_type=MESH).wait_recv()
    for cp in sends + fwds:
        cp.wait_send()


def _allgather_weights(placed):
    n_w = len(placed)
    return _pcall(
        functools.partial(_ag_body, n_w=n_w), name="weight_allgather",
        in_specs=[ANY] * n_w, out_specs=[ANY] * n_w,
        out_shape=[jax.ShapeDtypeStruct(p.shape, p.dtype) for p in placed],
        input_output_aliases={w: w for w in range(n_w)},
        scratch_shapes=[pltpu.SemaphoreType.DMA((3 * n_w,))] * 4,
    )(*placed)


HBM = pl.BlockSpec(memory_space=pltpu.HBM)
SEM = pl.BlockSpec(memory_space=pltpu.SEMAPHORE)
EFFECT = pltpu.SideEffectType.DATAFLOW_SIDE_EFFECTING


def _in_hbm(a):
    return pltpu.with_memory_space_constraint(a, pltpu.HBM)


def _ag_descs(bufs, send_sems, recv_sems):
    x, y, c, chips = _place()
    me = 2 * x + y
    out = []
    for w, buf in enumerate(bufs):
        half = buf.shape[1] // 2
        rows = pl.ds(pl.multiple_of(c * half, 16), half)
        mine = buf.at[me, rows]
        for j, (px, py) in enumerate(chips):
            landed = buf.at[2 * px + py, rows]
            mk = lambda ref, w=w, j=j, px=px, py=py: pltpu.make_async_remote_copy(
                src_ref=ref, dst_ref=ref, send_sem=send_sems.at[w * 3 + j], recv_sem=recv_sems.at[w * 3 + j],
                device_id=(px, py, c), device_id_type=MESH)
            out.append((mk(mine), mk(landed)))
    return out


def _ag_start_body(*refs, n_w):
    bufs = refs[:n_w]
    send_sems, recv_sems = refs[n_w + 1], refs[n_w + 2]
    token = refs[-1]
    for send, _ in _ag_descs(bufs, send_sems, recv_sems):
        send.start()
    token[...] = jnp.zeros_like(token)


def _ag_start(placed, after, tag):
    n_w = len(placed)
    res = _pcall(
        functools.partial(_ag_start_body, n_w=n_w), name="weight_allgather_start_" + tag,
        in_specs=[HBM] * n_w + [ANY],
        out_specs=[SEM, SEM] + [HBM] * n_w + [pl.BlockSpec(memory_space=pltpu.VMEM)],
        out_shape=[pltpu.SemaphoreType.DMA((3 * n_w,)), pltpu.SemaphoreType.DMA((3 * n_w,))]
        + [pltpu.HBM(p.shape, p.dtype) for p in placed] + [jax.ShapeDtypeStruct((8, HEAD), F32)],
        input_output_aliases={w: 2 + w for w in range(n_w)},
        compiler_params=pltpu.CompilerParams(has_side_effects=EFFECT),
    )(*[_in_hbm(p) for p in placed], after)
    return res[0], res[1], list(res[2:2 + n_w]), res[-1]


def _ag_wait_body(*refs, n_w):
    bufs = refs[:n_w]
    send_sems, recv_sems = refs[n_w], refs[n_w + 1]
    for send, recv in _ag_descs(bufs, send_sems, recv_sems):
        send.wait_send()
        recv.wait_recv()


def _ag_wait(send_sems, recv_sems, bufs, after, tag):
    n_w = len(bufs)
    return list(_pcall(
        functools.partial(_ag_wait_body, n_w=n_w), name="weight_allgather_wait_" + tag,
        in_specs=[HBM] * n_w + [SEM, SEM, ANY], out_specs=[HBM] * n_w,
        out_shape=[pltpu.HBM(b.shape, b.dtype) for b in bufs],
        input_output_aliases={w: w for w in range(n_w)},
        compiler_params=pltpu.CompilerParams(has_side_effects=EFFECT),
    )(*bufs, send_sems, recv_sems, after))


def _ag_forward_body(*refs, n_w):
    bufs = refs[n_w:2 * n_w]
    send_sems, recv_sems = refs[2 * n_w:]
    x, y, c, chips = _place()
    fwds = []
    for w, buf in enumerate(bufs):
        half = buf.shape[1] // 2
        for j, (px, py) in enumerate(chips):
            def piece(which, buf=buf, half=half, px=px, py=py):
                return buf.at[2 * px + py, pl.ds(pl.multiple_of(which * half, 16), half)]
            mk = lambda ref, w=w, j=j: pltpu.make_async_remote_copy(
                src_ref=ref, dst_ref=ref, send_sem=send_sems.at[w * 3 + j], recv_sem=recv_sems.at[w * 3 + j],
                device_id=(x, y, 1 - c), device_id_type=MESH)
            fw = mk(piece(c))
            fw.start()
            fwds.append((fw, mk(piece(1 - c))))
    for fw, back in fwds:
        back.wait_recv()
        fw.wait_send()


def _ag_forward(bufs, tag):
    n_w = len(bufs)
    return list(_pcall(
        functools.partial(_ag_forward_body, n_w=n_w), name="weight_allgather_forward_" + tag,
        in_specs=[ANY] * n_w, out_specs=[ANY] * n_w,
        out_shape=[jax.ShapeDtypeStruct(b.shape, b.dtype) for b in bufs],
        input_output_aliases={w: w for w in range(n_w)},
        scratch_shapes=[pltpu.SemaphoreType.DMA((3 * n_w,))] * 2,
    )(*bufs))


def _sc_descs(ins, outs, send_sems, recv_sems):
    x, y, c, chips = _place()
    me = 2 * x + y
    out = []
    for w in range(len(ins)):
        for j, (px, py) in enumerate(chips):
            out.append(pltpu.make_async_remote_copy(
                src_ref=ins[w].at[2 * px + py], dst_ref=outs[w].at[me],
                send_sem=send_sems.at[w * 3 + j], recv_sem=recv_sems.at[w * 3 + j],
                device_id=(px, py, c), device_id_type=MESH))
    return out


def _scatter_start_body(*refs, n_w):
    ins, lands = refs[:n_w], refs[n_w:2 * n_w]
    send_sems, recv_sems = refs[2 * n_w + 1], refs[2 * n_w + 2]
    token = refs[-1]
    for cp in _sc_descs(ins, lands, send_sems, recv_sems):
        cp.start()
    token[...] = jnp.zeros_like(token)


def _scatter_start(parts, after, tag):
    n_w = len(parts)
    lands = [lax.empty(p.shape, p.dtype) for p in parts]
    res = _pcall(
        functools.partial(_scatter_start_body, n_w=n_w), name="grad_scatter_start_" + tag,
        in_specs=[HBM] * (2 * n_w) + [ANY],
        out_specs=[SEM, SEM] + [HBM] * (2 * n_w) + [pl.BlockSpec(memory_space=pltpu.VMEM)],
        out_shape=[pltpu.SemaphoreType.DMA((3 * n_w,)), pltpu.SemaphoreType.DMA((3 * n_w,))]
        + [pltpu.HBM(p.shape, p.dtype) for p in parts] * 2 + [jax.ShapeDtypeStruct((8, HEAD), F32)],
        input_output_aliases={i: 2 + i for i in range(2 * n_w)},
        compiler_params=pltpu.CompilerParams(has_side_effects=EFFECT),
    )(*[_in_hbm(p) for p in parts], *[_in_hbm(l) for l in lands], after)
    return res[0], res[1], list(res[2:2 + n_w]), list(res[2 + n_w:2 + 2 * n_w]), res[-1]


def _scatter_wait_body(*refs, n_w):
    ins, lands = refs[:n_w], refs[n_w:2 * n_w]
    send_sems, recv_sems = refs[2 * n_w], refs[2 * n_w + 1]
    for cp in _sc_descs(ins, lands, send_sems, recv_sems):
        cp.wait_send()
        cp.wait_recv()


def _scatter_wait(send_sems, recv_sems, parts, lands, after, tag):
    n_w = len(parts)
    res = _pcall(
        functools.partial(_scatter_wait_body, n_w=n_w), name="grad_scatter_wait_" + tag,
        in_specs=[HBM] * (2 * n_w) + [SEM, SEM, ANY], out_specs=[HBM] * (2 * n_w),
        out_shape=[pltpu.HBM(p.shape, p.dtype) for p in parts] * 2,
        input_output_aliases={i: i for i in range(2 * n_w)},
        compiler_params=pltpu.CompilerParams(has_side_effects=EFFECT),
    )(*parts, *lands, send_sems, recv_sems, after)
    return list(res[:n_w]), list(res[n_w:])


def _pair_send_body(*refs, n_w):
    ins, outs = refs[:n_w], refs[n_w:2 * n_w]
    send_sems, recv_sems = refs[2 * n_w:]
    x, y, c, _ = _place()
    cps = []
    for w in range(n_w):
        cp = pltpu.make_async_remote_copy(
            src_ref=ins[w].at[:, 1 - c], dst_ref=outs[w],
            send_sem=send_sems.at[w], recv_sem=recv_sems.at[w],
            device_id=(x, y, 1 - c), device_id_type=MESH)
        cp.start()
        cps.append(cp)
    for cp in cps:
        cp.wait()


def _pair_send(grads4, tag):
    n_w = len(grads4)
    return _pcall(
        functools.partial(_pair_send_body, n_w=n_w), name="grad_pair_exchange_" + tag,
        in_specs=[ANY] * n_w, out_specs=[ANY] * n_w,
        out_shape=[jax.ShapeDtypeStruct((g.shape[0],) + g.shape[2:], g.dtype) for g in grads4],
        scratch_shapes=[pltpu.SemaphoreType.DMA((n_w,))] * 2,
    )(*grads4)


def _pair_add_body(c_ref, mine_ref, theirs_ref, o_ref):
    o_ref[...] = (mine_ref[...].astype(F32) + theirs_ref[...].astype(F32)).astype(o_ref.dtype)


def _pair_add(c_arr, g4, recv, name):
    _, _, hr, cols = g4.shape
    tr = min(hr, 256)
    grid_spec = pltpu.PrefetchScalarGridSpec(
        num_scalar_prefetch=1, grid=(N_CHIPS, hr // tr),
        in_specs=[pl.BlockSpec((None, None, tr, cols), lambda s, i, c: (s, c[0], i, 0)),
                  pl.BlockSpec((None, tr, cols), lambda s, i, c: (s, i, 0))],
        out_specs=pl.BlockSpec((None, tr, cols), lambda s, i, c: (s, i, 0)))
    return _pcall(
        _pair_add_body, name=name, grid_spec=grid_spec,
        out_shape=jax.ShapeDtypeStruct(recv.shape, recv.dtype),
        compiler_params=pltpu.CompilerParams(dimension_semantics=("parallel", "parallel")),
    )(c_arr, g4, recv)


def _scatter_body(*refs, n_w):
    ins, outs = refs[:n_w], refs[n_w:2 * n_w]
    send_sems, recv_sems = refs[2 * n_w:]
    x, y, c, chips = _place()
    me = 2 * x + y
    todo = []
    for w in range(n_w):
        for j, (px, py) in enumerate(chips):
            cp = pltpu.make_async_remote_copy(
                src_ref=ins[w].at[2 * px + py], dst_ref=outs[w].at[me],
                send_sem=send_sems.at[w * 3 + j], recv_sem=recv_sems.at[w * 3 + j],
                device_id=(px, py, c), device_id_type=MESH)
            cp.start()
            todo.append(cp)
    for t in todo:
        t.wait()


def _scatter(parts):
    n_w = len(parts)
    return _pcall(
        functools.partial(_scatter_body, n_w=n_w), name="grad_scatter",
        in_specs=[ANY] * n_w, out_specs=[ANY] * n_w,
        out_shape=[jax.ShapeDtypeStruct(p.shape, p.dtype) for p in parts],
        scratch_shapes=[pltpu.SemaphoreType.DMA((3 * n_w,))] * 2,
    )(*parts)


def _sum4_body(me_ref, p_ref, l0, l1, l2, l3, o_ref):
    me = me_ref[0]
    t = [jnp.where(me == j, p_ref[...], l[...]).astype(F32) for j, l in enumerate((l0, l1, l2, l3))]
    o_ref[...] = ((t[0] + t[1]) + t[2]) + t[3]


def _sum4(me_arr, part, landed, name):
    _, hr, cols = part.shape
    tr = min(hr, 256)

    def slot(j):
        return lambda i, me: (jnp.where(me[0] == j, (j + 1) % N_CHIPS, j), i, 0)

    grid_spec = pltpu.PrefetchScalarGridSpec(
        num_scalar_prefetch=1, grid=(hr // tr,),
        in_specs=[pl.BlockSpec((None, tr, cols), lambda i, me: (me[0], i, 0))]
        + [pl.BlockSpec((None, tr, cols), slot(j)) for j in range(N_CHIPS)],
        out_specs=pl.BlockSpec((tr, cols), lambda i, me: (i, 0)))
    return _pcall(
        _sum4_body, name=name, grid_spec=grid_spec,
        out_shape=jax.ShapeDtypeStruct((hr, cols), F32),
        compiler_params=pltpu.CompilerParams(dimension_semantics=("parallel",)),
    )(me_arr, part, landed, landed, landed, landed)


def _pair_swap_body(*refs, n_w):
    ins, outs = refs[:n_w], refs[n_w:2 * n_w]
    send_sems, recv_sems = refs[2 * n_w:]
    x, y, c, _ = _place()
    todo = []
    for w in range(n_w):
        cp = pltpu.make_async_remote_copy(
            src_ref=ins[w], dst_ref=outs[w],
            send_sem=send_sems.at[w], recv_sem=recv_sems.at[w],
            device_id=(x, y, 1 - c), device_id_type=MESH)
        cp.start()
        todo.append(cp)
    for t in todo:
        t.wait()


def _pair_swap(halves):
    n_w = len(halves)
    return _pcall(
        functools.partial(_pair_swap_body, n_w=n_w), name="grad_pair_swap",
        in_specs=[ANY] * n_w, out_specs=[ANY] * n_w,
        out_shape=[jax.ShapeDtypeStruct(h.shape, h.dtype) for h in halves],
        scratch_shapes=[pltpu.SemaphoreType.DMA((n_w,))] * 2,
    )(*halves)


def _small_gather_body(x_ref, out_ref, send_sems, recv_sems, local_sem):
    m_per = x_ref.shape[0]
    x, y, c, chips = _place()
    me, sibling = (x, y, c), (x, y, 1 - c)

    def rows(px, py, pc):
        return out_ref.at[pl.ds((4 * px + 2 * py + pc) * m_per, m_per), :]

    def copy(k, block, to, src=None):
        return pltpu.make_async_remote_copy(
            src_ref=rows(*block) if src is None else src, dst_ref=rows(*block),
            send_sem=send_sems.at[k], recv_sem=recv_sems.at[k], device_id=to, device_id_type=MESH)

    mine = pltpu.make_async_copy(x_ref, rows(*me), local_sem)
    mine.start()
    first = [copy(0, me, sibling, src=x_ref)]
    first += [copy(1 + j, me, (*chip, c), src=x_ref) for j, chip in enumerate(chips)]
    for cp in first:
        cp.start()
    passed = [copy(4 + j, (*chip, c), sibling) for j, chip in enumerate(chips)]
    for j, chip in enumerate(chips):
        copy(1 + j, (*chip, c), me).wait_recv()
        passed[j].start()
    copy(0, sibling, me).wait_recv()
    for j, chip in enumerate(chips):
        copy(4 + j, (*chip, 1 - c), me).wait_recv()
    for cp in first + passed:
        cp.wait_send()
    mine.wait()


def _small_gather(small):
    m_per, n = small.shape
    return _pcall(
        _small_gather_body, name="small_allgather",
        out_shape=jax.ShapeDtypeStruct((N_DEV * m_per, n), small.dtype),
        in_specs=[pl.BlockSpec(memory_space=pltpu.VMEM)],
        out_specs=pl.BlockSpec(memory_space=pltpu.VMEM),
        scratch_shapes=[pltpu.SemaphoreType.DMA((7,)), pltpu.SemaphoreType.DMA((7,)), pltpu.SemaphoreType.DMA],
    )(small)


def _adamw(w, g, m, v):
    m = ADAM_B1 * m + (1.0 - ADAM_B1) * g
    v = ADAM_B2 * v + (1.0 - ADAM_B2) * (g * g)
    m_hat = m / (1.0 - ADAM_B1 ** ADAM_STEP)
    v_hat = v / (1.0 - ADAM_B2 ** ADAM_STEP)
    delta = -ADAM_LR * (m_hat / (jnp.sqrt(v_hat) + ADAM_EPS) + ADAM_WD * w)
    return delta, m, v


def _adamw_body(c_ref, w_ref, own_ref, sib_ref, m_ref, v_ref, g_ref, d_ref, nm_ref, nv_ref, *, nh):
    mine = (pl.program_id(0) // nh) == c_ref[0]
    g = jnp.where(mine, own_ref[...], sib_ref[...])
    g_ref[...] = g
    d, m, v = _adamw(w_ref[...], g, m_ref[...], v_ref[...])
    d_ref[...] = d
    nm_ref[...] = m
    nv_ref[...] = v


def _adamw_call(c_arr, w, own, sib, m, v, name):
    rows, cols = w.shape
    tr = min(rows // 2, 256)
    nh = (rows // 2) // tr
    full = pl.BlockSpec((tr, cols), lambda i, c: (i, 0))
    own_spec = pl.BlockSpec((tr, cols), lambda i, c: (jnp.clip(i - c[0] * nh, 0, nh - 1), 0))
    sib_spec = pl.BlockSpec((tr, cols), lambda i, c: (jnp.clip(i - (1 - c[0]) * nh, 0, nh - 1), 0))
    grid_spec = pltpu.PrefetchScalarGridSpec(
        num_scalar_prefetch=1, grid=(rows // tr,),
        in_specs=[full, own_spec, sib_spec, full, full], out_specs=[full] * 4)
    return _pcall(
        functools.partial(_adamw_body, nh=nh), name=name, grid_spec=grid_spec,
        out_shape=[jax.ShapeDtypeStruct(w.shape, F32)] * 4,
        compiler_params=pltpu.CompilerParams(dimension_semantics=("parallel",)),
    )(c_arr, w, own, sib, m, v)


def _small_update_body(gath_ref, w_ref, m_ref, v_ref, g_ref, d_ref, nm_ref, nv_ref, loss_ref, *, n_gain):
    tot = gath_ref[0:1, :]
    for i in range(1, gath_ref.shape[0]):
        tot = tot + gath_ref[i:i + 1, :]
    g = tot[:, 0:n_gain]
    g_ref[...] = g
    d, m, v = _adamw(w_ref[...], g, m_ref[...], v_ref[...])
    d_ref[...] = d
    nm_ref[...] = m
    nv_ref[...] = v
    loss_ref[...] = (0.5 / D_MODEL) * jnp.sum(tot[:, n_gain:n_gain + HEAD], axis=1, keepdims=True) * jnp.ones((1, HEAD), F32)


def _small_update(gath, w, m, v):
    n_gain = w.shape[1]
    vm = pl.BlockSpec(memory_space=pltpu.VMEM)
    return _pcall(
        functools.partial(_small_update_body, n_gain=n_gain), name="gain_update",
        in_specs=[vm] * 4, out_specs=[vm] * 5,
        out_shape=[jax.ShapeDtypeStruct((1, n_gain), F32)] * 4 + [jax.ShapeDtypeStruct((1, HEAD), F32)],
    )(gath, w, m, v)


def kernel(x, positions, norm_attn_pre, norm_attn_post, w_in, q_latent_norm, kv_latent_norm, w_uq, w_ukv, w_out, norm_mlp_pre, norm_mlp_post, w_up, w_down, loss_target, m_norm_attn_pre, m_norm_attn_post, m_w_in, m_q_latent_norm, m_kv_latent_norm, m_w_uq, m_w_ukv, m_w_out, m_norm_mlp_pre, m_norm_mlp_post, m_w_up, m_w_down, v_norm_attn_pre, v_norm_attn_post, v_w_in, v_q_latent_norm, v_kv_latent_norm, v_w_uq, v_w_ukv, v_w_out, v_norm_mlp_pre, v_norm_mlp_post, v_w_up, v_w_down):
    T = x.shape[1]
    c_arr = lax.axis_index("c").astype(jnp.int32).reshape(1)
    me_arr = (2 * lax.axis_index("x") + lax.axis_index("y")).astype(jnp.int32).reshape(1)
    names = ["w_in", "w_uq", "w_ukv", "w_out", "w_up", "w_down"]

    mats = [w_in[0], w_uq[0], w_ukv[0], w_out[0], w_up[0], w_down[0]]
    placed = [_cast_place(me_arr, w, "cast_" + n) for w, n in zip(mats, names)]
    (win_g,) = _allgather_weights(placed[:1])
    att_send, att_recv, att_bufs, att_started = _ag_start(placed[1:4], win_g, "attn")
    mlp_send, mlp_recv, mlp_bufs, started = _ag_start(placed[4:], att_started, "mlp")

    col_major = lambda g: jnp.transpose(g, (1, 0, 2)).reshape(g.shape[1], N_CHIPS * g.shape[2])
    win_full = col_major(win_g)
    w_main = win_full[:, :MAIN_COLS]
    w_kr = jnp.pad(win_full[:, MAIN_COLS:], ((0, 0), (0, HEAD - ROPE_B)))
    cast = lambda a: a.astype(MXU_DTYPE)
    to_shards = lambda g: jnp.transpose(g.reshape(g.shape[0], N_CHIPS, g.shape[1] // N_CHIPS), (1, 0, 2))
    halved = lambda g: g.reshape(N_CHIPS, 2, g.shape[1] // 2, g.shape[2])

    def pair_sum(full4, ns):
        from_sib = _pair_send(full4, "_".join(ns))
        return [_pair_add(c_arr, g4, r, "pair_add_" + n) for g4, r, n in zip(full4, from_sib, ns)]

    def attn_weights(after):
        wuq_g, wukv_g, wout_g = _ag_forward(_ag_wait(att_send, att_recv, att_bufs, after, "attn"), "attn")
        wuq_full = col_major(wuq_g).reshape(LORA, NH, HEAD + ROPE_B)
        w_uq_p = jnp.pad(wuq_full, ((0, 0), (0, 0), (0, QPAD - HEAD - ROPE_B))).reshape(LORA, NH * QPAD)
        w_ukv_p = col_major(wukv_g).reshape(LORA, NH, 2, HEAD).transpose(0, 2, 1, 3).reshape(LORA, 2 * A_W)
        return cast(w_uq_p), cast(w_ukv_p), cast(wout_g.reshape(2 * A_W, D_MODEL))

    def mlp_weights(after):
        wup_g, wdown_g = _ag_forward(_ag_wait(mlp_send, mlp_recv, mlp_bufs, after, "mlp"), "mlp")
        return cast(col_major(wup_g)), cast(wdown_g.reshape(D_FF, D_MODEL))

    in_flight = {}

    def mlp_grads_ready(gw_up, gw_down):
        parts = pair_sum([halved(to_shards(gw_up)), halved(gw_down.reshape(N_CHIPS, D_MODEL, D_MODEL))], names[4:])
        in_flight["mlp"] = _scatter_start(parts, started, "mlp")
        return in_flight["mlp"][-1][0:1, 0:1]

    def attn_grads_ready(gw_out, gw_uq_p, gw_ukv_p):
        gw_uq = to_shards(gw_uq_p.reshape(LORA, NH, QPAD)[:, :, :HEAD + ROPE_B].reshape(LORA, NH * (HEAD + ROPE_B)))
        gw_ukv = to_shards(gw_ukv_p.reshape(LORA, 2, NH, HEAD).transpose(0, 2, 1, 3).reshape(LORA, 2 * A_W))
        parts = pair_sum([halved(g) for g in (gw_uq, gw_ukv, gw_out.reshape(N_CHIPS, LORA, D_MODEL))], names[1:4])
        in_flight["attn"] = _scatter_start(parts, in_flight["mlp"][-1], "attn")
        return in_flight["attn"][-1][0:1, 0:1]

    dx, (gw_main, gw_kr), small = _local_step(
        x[0], positions[0].astype(F32).reshape(T, 1), loss_target[0],
        norm_attn_pre + started[0:1, 0:1], norm_attn_post, q_latent_norm, kv_latent_norm, norm_mlp_pre, norm_mlp_post,
        cast(w_main), cast(w_kr), attn_weights, mlp_weights, mlp_grads_ready, attn_grads_ready)

    gw_in = to_shards(jnp.concatenate([gw_main, gw_kr[:, :ROPE_B]], axis=1))
    parts_in = pair_sum([halved(gw_in)], names[:1])
    landed_in = list(_scatter(parts_in))
    a_send, a_recv, parts_att, lands_att, _ = in_flight["attn"]
    parts_att, landed_att = _scatter_wait(a_send, a_recv, parts_att, lands_att, landed_in[0], "attn")
    m_send, m_recv, parts_mlp, lands_mlp, _ = in_flight["mlp"]
    parts_mlp, landed_mlp = _scatter_wait(m_send, m_recv, parts_mlp, lands_mlp, landed_att[0], "mlp")
    halves = [_sum4(me_arr, p, l, "chip_sum_" + n)
              for p, l, n in zip(parts_in + parts_att + parts_mlp, landed_in + landed_att + landed_mlp, names)]
    from_sib2 = _pair_swap(halves)

    ms = [m_w_in[0], m_w_uq[0], m_w_ukv[0], m_w_out[0], m_w_up[0], m_w_down[0]]
    vs = [v_w_in[0], v_w_uq[0], v_w_ukv[0], v_w_out[0], v_w_up[0], v_w_down[0]]
    upd = [_adamw_call(c_arr, w, own, sib, m, v, "adamw_" + n)
           for w, own, sib, m, v, n in zip(mats, halves, from_sib2, ms, vs, names)]
    grads = [u[0] for u in upd]

    gath = _small_gather(small)
    gains = [norm_attn_pre, norm_attn_post, q_latent_norm, kv_latent_norm, norm_mlp_pre, norm_mlp_post]
    gm = [m_norm_attn_pre, m_norm_attn_post, m_q_latent_norm, m_kv_latent_norm, m_norm_mlp_pre, m_norm_mlp_post]
    gv = [v_norm_attn_pre, v_norm_attn_post, v_q_latent_norm, v_kv_latent_norm, v_norm_mlp_pre, v_norm_mlp_post]
    cat = lambda xs: jnp.concatenate(xs, axis=1)
    g_s, d_s, m_s, v_s, loss_v = _small_update(gath, cat(gains), cat(gm), cat(gv))
    widths = [a.shape[1] for a in gains]
    offs = [sum(widths[:i]) for i in range(len(widths))]
    split = lambda a: [a[:, o:o + w] for o, w in zip(offs, widths)]
    g_gain, d_gain, m_gain, v_gain = split(g_s), split(d_s), split(m_s), split(v_s)

    def ordered(gain_list, mat_list):
        gl, ml = gain_list, [a[None] for a in mat_list]
        return [gl[0], gl[1], ml[0], gl[2], gl[3], ml[1], ml[2], ml[3], gl[4], gl[5], ml[4], ml[5]]

    loss = loss_v[0, 0]
    return (loss, dx[None],
            *ordered(g_gain, grads),
            *ordered(d_gain, [u[1] for u in upd]),
            *ordered(m_gain, [u[2] for u in upd]),
            *ordered(v_gain, [u[3] for u in upd]))
```

```python
import functools

import jax
import jax.numpy as jnp
from jax import lax
from jax.experimental import pallas as pl
from jax.experimental.pallas import tpu as pltpu

F32 = jnp.float32
BF16 = jnp.bfloat16
MXU_DTYPE = jnp.bfloat16
WIRE_DTYPE = jnp.bfloat16

D_MODEL = 2048
HEAD = 128
NH = 8
A_W = NH * HEAD
LORA = 512
ROPE_B = 64
QPAD = 256
MAIN_COLS = 3 * A_W + 2 * LORA
IN_COLS = MAIN_COLS + ROPE_B
D_FF = 4 * D_MODEL
DIL = (1, 4, 16)
ROT_A = 32
ROPE_THETA = 500000.0
EPS = 1e-6
NEG = -1e30
N_CHIPS = 4
N_DEV = 8

ADAM_LR = 0.001
ADAM_B1 = 0.9
ADAM_B2 = 0.999
ADAM_EPS = 1e-08
ADAM_WD = 0.01
ADAM_STEP = 10

MESH = pl.DeviceIdType.MESH
ANY = pl.BlockSpec(memory_space=pl.ANY)


def _pcall(body, **kw):
    return pl.pallas_call(body, **kw)


_DIMS = {
    "nn": (((1,), (0,)), ((), ())),
    "nt": (((1,), (1,)), ((), ())),
    "tn": (((0,), (0,)), ((), ())),
}


def _mm_body(*refs, dims, nk, epi, n_extra, n_out):
    a_ref, b_ref = refs[0], refs[1]
    extra = refs[2:2 + n_extra]
    outs = refs[2 + n_extra:2 + n_extra + n_out]
    part = lax.dot_general(a_ref[...], b_ref[...], _DIMS[dims], preferred_element_type=F32)

    def finish(acc):
        res = epi(acc, *[r[...] for r in extra]) if epi is not None else (acc,)
        for o_ref, o in zip(outs, res):
            o_ref[...] = o.astype(o_ref.dtype)

    if nk == 1:
        finish(part)
        return
    acc_ref = refs[-1]
    k = pl.program_id(2)

    @pl.when(k == 0)
    def _():
        acc_ref[...] = part

    @pl.when(k > 0)
    def _():
        acc_ref[...] += part

    @pl.when(k == nk - 1)
    def _():
        finish(acc_ref[...])


def _matmul(a, b, *, dims, out_dtypes, tm, tn, tk, name, epi=None, extras=(), b_outer=False):
    if dims == "nn":
        (M, K), (K2, N) = a.shape, b.shape
    elif dims == "nt":
        (M, K), (N, K2) = a.shape, b.shape
    else:
        (K, M), (K2, N) = a.shape, b.shape
    assert K == K2, (a.shape, b.shape, dims)
    tm, tn, tk = min(tm, M), min(tn, N), min(tk, K)
    assert M % tm == 0 and N % tn == 0 and K % tk == 0, (name, M, N, K, tm, tn, tk)
    nk = K // tk

    def at(f):
        if b_outer:
            return lambda j, i, k: f(i, j, k)
        return f

    a_spec = {"nn": pl.BlockSpec((tm, tk), at(lambda i, j, k: (i, k))),
              "nt": pl.BlockSpec((tm, tk), at(lambda i, j, k: (i, k))),
              "tn": pl.BlockSpec((tk, tm), at(lambda i, j, k: (k, i)))}[dims]
    b_spec = {"nn": pl.BlockSpec((tk, tn), at(lambda i, j, k: (k, j))),
              "nt": pl.BlockSpec((tn, tk), at(lambda i, j, k: (j, k))),
              "tn": pl.BlockSpec((tk, tn), at(lambda i, j, k: (k, j)))}[dims]
    o_spec = pl.BlockSpec((tm, tn), at(lambda i, j, k: (i, j)))
    body = functools.partial(_mm_body, dims=dims, nk=nk, epi=epi,
                             n_extra=len(extras), n_out=len(out_dtypes))
    res = _pcall(
        body, name=name,
        grid=(N // tn, M // tm, nk) if b_outer else (M // tm, N // tn, nk),
        in_specs=[a_spec, b_spec] + [o_spec] * len(extras),
        out_specs=[o_spec] * len(out_dtypes),
        out_shape=[jax.ShapeDtypeStruct((M, N), dt) for dt in out_dtypes],
        scratch_shapes=[pltpu.VMEM((tm, tn), F32)] if nk > 1 else [],
        compiler_params=pltpu.CompilerParams(
            dimension_semantics=("parallel", "parallel", "arbitrary")),
    )(a, b, *extras)
    return list(res)


def _rowwise(body, row_ins, vec_ins, row_outs, acc_outs, *, tr, name):
    T = row_ins[0].shape[0]
    tr = min(tr, T)
    assert T % tr == 0
    in_specs = [pl.BlockSpec((tr, a.shape[1]), lambda i: (i, 0)) for a in row_ins]
    in_specs += [pl.BlockSpec(a.shape, lambda i: (0, 0)) for a in vec_ins]
    out_specs = [pl.BlockSpec((tr, w), lambda i: (i, 0)) for (w, _) in row_outs]
    out_specs += [pl.BlockSpec(s, lambda i: (0, 0)) for s in acc_outs]
    out_shape = [jax.ShapeDtypeStruct((T, w), dt) for (w, dt) in row_outs]
    out_shape += [jax.ShapeDtypeStruct(s, F32) for s in acc_outs]
    sem = "arbitrary" if acc_outs else "parallel"
    return list(_pcall(
        body, name=name, grid=(T // tr,), in_specs=in_specs, out_specs=out_specs,
        out_shape=out_shape,
        compiler_params=pltpu.CompilerParams(dimension_semantics=(sem,)),
    )(*row_ins, *vec_ins))


def _rstd(x):
    return lax.rsqrt(jnp.mean(x * x, axis=-1, keepdims=True) + EPS)


def _rms_bwd(x, rstd, dyg):
    xh = x * rstd
    return rstd * (dyg - xh * jnp.mean(dyg * xh, axis=-1, keepdims=True)), xh


def _fold8(v):
    r, w = v.shape
    return jnp.sum(v.reshape(r // 8, 8, w), axis=0)


def _acc(ref, val):
    first = pl.program_id(0) == 0

    @pl.when(first)
    def _():
        ref[...] = val

    @pl.when(jnp.logical_not(first))
    def _():
        ref[...] += val


def _rope(x, c, sa, sb, half):
    return x * c + pltpu.roll(x, HEAD - half, 1) * sa + pltpu.roll(x, half, 1) * sb


def _rope_t(dy, c, sa, sb, half):
    return dy * c - pltpu.roll(dy, HEAD - half, 1) * sa - pltpu.roll(dy, half, 1) * sb


def _rope_tab_body(pos_ref, inv_ref, ca, saa, sab, cb, sba, sbb):
    pos = pos_ref[...]
    lane = lax.broadcasted_iota(jnp.int32, (pos.shape[0], HEAD), 1)
    ang_a = pos * inv_ref[0:1, :]
    ang_b = pos * inv_ref[1:2, :]
    c, s = jnp.cos(ang_a), jnp.sin(ang_a)
    ha = ROT_A // 2
    ca[...] = jnp.where(lane < ROT_A, c, 1.0)
    saa[...] = jnp.where(lane < ha, -s, 0.0)
    sab[...] = jnp.where((lane >= ha) & (lane < ROT_A), s, 0.0)
    c, s = jnp.cos(ang_b), jnp.sin(ang_b)
    hb = ROPE_B // 2
    cb[...] = jnp.where(lane < ROPE_B, c, 1.0)
    sba[...] = jnp.where(lane < hb, -s, 0.0)
    sbb[...] = jnp.where((lane >= hb) & (lane < ROPE_B), s, 0.0)


def _rms_fwd_body(x_ref, g_ref, h_ref):
    x = x_ref[...]
    h_ref[...] = ((x * _rstd(x)) * g_ref[...]).astype(h_ref.dtype)


def _postproj_body(p_ref, kr_ref, ca, saa, sab, cb, sba, sbb, gq_ref, gkv_ref,
                   q_ref, k_ref, v_ref, cqn_ref, ckvn_ref, krope_ref):
    c, sa, sb = ca[...], saa[...], sab[...]
    for h in range(NH):
        lo = h * HEAD
        q_ref[:, lo:lo + HEAD] = _rope(p_ref[:, lo:lo + HEAD], c, sa, sb, ROT_A // 2).astype(q_ref.dtype)
        k_ref[:, lo:lo + HEAD] = _rope(p_ref[:, A_W + lo:A_W + lo + HEAD], c, sa, sb, ROT_A // 2).astype(k_ref.dtype)
    v_ref[...] = p_ref[:, 2 * A_W:3 * A_W].astype(v_ref.dtype)
    cq = p_ref[:, 3 * A_W:3 * A_W + LORA]
    cqn_ref[...] = ((cq * _rstd(cq)) * gq_ref[...]).astype(cqn_ref.dtype)
    ckv = p_ref[:, 3 * A_W + LORA:MAIN_COLS]
    ckvn_ref[...] = ((ckv * _rstd(ckv)) * gkv_ref[...]).astype(ckvn_ref.dtype)
    krope_ref[...] = _rope(kr_ref[...], cb[...], sba[...], sbb[...], ROPE_B // 2).astype(krope_ref.dtype)


def _qrope_body(qp_ref, cb, sba, sbb, q_ref):
    c, sa, sb = cb[...], sba[...], sbb[...]
    for h in range(NH):
        lo = h * QPAD
        q_ref[:, lo:lo + HEAD] = qp_ref[:, lo:lo + HEAD].astype(q_ref.dtype)
        q_ref[:, lo + HEAD:lo + QPAD] = _rope(qp_ref[:, lo + HEAD:lo + QPAD], c, sa, sb, ROPE_B // 2).astype(q_ref.dtype)


def _qrope_t_body(dq_ref, cb, sba, sbb, o_ref):
    c, sa, sb = cb[...], sba[...], sbb[...]
    for h in range(NH):
        lo = h * QPAD
        o_ref[:, lo:lo + HEAD] = dq_ref[:, lo:lo + HEAD].astype(o_ref.dtype)
        o_ref[:, lo + HEAD:lo + QPAD] = _rope_t(dq_ref[:, lo + HEAD:lo + QPAD], c, sa, sb, ROPE_B // 2).astype(o_ref.dtype)


def _mid_body(x_ref, o_ref, g2_ref, g3_ref, x1_ref, h2_ref):
    o = o_ref[...]
    x1 = x_ref[...] + (o * _rstd(o)) * g2_ref[...]
    x1_ref[...] = x1
    h2_ref[...] = ((x1 * _rstd(x1)) * g3_ref[...]).astype(h2_ref.dtype)


def _loss_body(x1_ref, d_ref, t_ref, g4_ref, dy_ref, dd_ref, loss_ref, dg4_ref):
    d = d_ref[...]
    rstd = _rstd(d)
    y = x1_ref[...] + (d * rstd) * g4_ref[...]
    e = y - t_ref[...]
    dy = e * (1.0 / D_MODEL)
    dy_ref[...] = dy
    dd, dh = _rms_bwd(d, rstd, dy * g4_ref[...])
    dd_ref[...] = dd.astype(dd_ref.dtype)
    _acc(dg4_ref, _fold8(dy * dh))
    e8 = _fold8(e * e)
    l = e8[:, 0:HEAD]
    for j in range(1, D_MODEL // HEAD):
        l = l + e8[:, j * HEAD:(j + 1) * HEAD]
    _acc(loss_ref, l)


def _bmid_body(dy_ref, dh2_ref, x1_ref, o_ref, g2_ref, g3_ref, dx1_ref, do_ref, dg3_ref, dg2_ref):
    x1 = x1_ref[...]
    dh2 = dh2_ref[...]
    dn, x1h = _rms_bwd(x1, _rstd(x1), dh2 * g3_ref[...])
    dx1 = dy_ref[...] + dn
    dx1_ref[...] = dx1
    _acc(dg3_ref, _fold8(dh2 * x1h))
    o = o_ref[...]
    do, oh = _rms_bwd(o, _rstd(o), dx1 * g2_ref[...])
    do_ref[...] = do.astype(do_ref.dtype)
    _acc(dg2_ref, _fold8(dx1 * oh))


def _dproj_body(dq_ref, dk_ref, dv_ref, dcq_ref, dckv_ref, p_ref, dkr_ref,
                ca, saa, sab, cb, sba, sbb, gq_ref, gkv_ref,
                dp_ref, dkrp_ref, dgq_ref, dgkv_ref):
    c, sa, sb = ca[...], saa[...], sab[...]
    for h in range(NH):
        lo = h * HEAD
        dp_ref[:, lo:lo + HEAD] = _rope_t(dq_ref[:, lo:lo + HEAD], c, sa, sb, ROT_A // 2).astype(dp_ref.dtype)
        dp_ref[:, A_W + lo:A_W + lo + HEAD] = _rope_t(dk_ref[:, lo:lo + HEAD], c, sa, sb, ROT_A // 2).astype(dp_ref.dtype)
    dp_ref[:, 2 * A_W:3 * A_W] = dv_ref[...].astype(dp_ref.dtype)
    cq = p_ref[:, 3 * A_W:3 * A_W + LORA]
    dcqn = dcq_ref[...]
    dcq, cqh = _rms_bwd(cq, _rstd(cq), dcqn * gq_ref[...])
    dp_ref[:, 3 * A_W:3 * A_W + LORA] = dcq.astype(dp_ref.dtype)
    _acc(dgq_ref, _fold8(dcqn * cqh))
    ckv = p_ref[:, 3 * A_W + LORA:MAIN_COLS]
    dckvn = dckv_ref[...]
    dckv, ckvh = _rms_bwd(ckv, _rstd(ckv), dckvn * gkv_ref[...])
    dp_ref[:, 3 * A_W + LORA:MAIN_COLS] = dckv.astype(dp_ref.dtype)
    _acc(dgkv_ref, _fold8(dckvn * ckvh))
    dkr = dkr_ref[:, 0:HEAD]
    for h in range(1, NH):
        dkr = dkr + dkr_ref[:, h * HEAD:(h + 1) * HEAD]
    dkrp_ref[...] = _rope_t(dkr, cb[...], sba[...], sbb[...], ROPE_B // 2).astype(dkrp_ref.dtype)


def _bin_body(dx1_ref, dha_ref, dhb_ref, x_ref, g1_ref, dx_ref, dg1_ref):
    x = x_ref[...]
    dh = dha_ref[...] + dhb_ref[...]
    dn, xh = _rms_bwd(x, _rstd(x), dh * g1_ref[...])
    dx_ref[...] = dx1_ref[...] + dn
    _acc(dg1_ref, _fold8(dh * xh))


def _dot_nt(a, b):
    return lax.dot_general(a, b, _DIMS["nt"], preferred_element_type=F32)


def _dot_tn(a, b):
    return lax.dot_general(a, b, _DIMS["tn"], preferred_element_type=F32)


def _dot_nn(a, b):
    return jnp.dot(a, b, preferred_element_type=F32)


DIL_SCALE = HEAD ** -0.5
DIL_CHUNK = 256


def _dil_rows(t, d):
    r = t & (d - 1)
    n = t >> (d.bit_length() - 1)
    start = r + n * (HEAD * d)
    has_prev = n > 0
    pstart = jnp.where(has_prev, start - HEAD * d, start)
    if d == 1:
        return pl.ds(pl.multiple_of(start, HEAD), HEAD), pl.ds(pl.multiple_of(pstart, HEAD), HEAD), has_prev
    return pl.ds(start, HEAD, stride=d), pl.ds(pstart, HEAD, stride=d), has_prev


def _dil_band():
    row = lax.broadcasted_iota(jnp.int32, (HEAD, 2 * HEAD), 0)
    col = lax.broadcasted_iota(jnp.int32, (HEAD, 2 * HEAD), 1)
    return (col >= row) & (col <= row + HEAD), col >= HEAD


def _dil_fwd_body(q_ref, k_ref, v_ref, a_ref, lse_ref, o1, o2, o3, l1, l2, l3, *, nt, unroll):
    band, is_cur = _dil_band()
    for d, o_sc, l_sc in zip(DIL, (o1, o2, o3), (l1, l2, l3)):

        def tile(t, carry, d=d, o_sc=o_sc, l_sc=l_sc):
            rows, prows, has_prev = _dil_rows(t, d)
            q = q_ref[rows, :].astype(MXU_DTYPE)
            kk = jnp.concatenate([k_ref[prows, :], k_ref[rows, :]], axis=0).astype(MXU_DTYPE)
            vv = jnp.concatenate([v_ref[prows, :], v_ref[rows, :]], axis=0).astype(MXU_DTYPE)
            ok = band & (is_cur | has_prev)
            s = jnp.where(ok, _dot_nt(q, kk) * DIL_SCALE, NEG)
            m = jnp.max(s, axis=1, keepdims=True)
            p = jnp.exp(s - m)
            den = jnp.sum(p, axis=1, keepdims=True)
            o_sc[rows, :] = _dot_nn((p / den).astype(MXU_DTYPE), vv)
            l_sc[rows, :] = jnp.broadcast_to(m + jnp.log(den), (HEAD, HEAD))
            return carry

        lax.fori_loop(0, nt, tile, 0, unroll=unroll)

    def merge(i, carry):
        rs = pl.ds(pl.multiple_of(i * DIL_CHUNK, DIL_CHUNK), DIL_CHUNK)
        la, lb, lc = l1[rs, :], l2[rs, :], l3[rs, :]
        m = jnp.maximum(jnp.maximum(la, lb), lc)
        wa, wb, wc = jnp.exp(la - m), jnp.exp(lb - m), jnp.exp(lc - m)
        den = wa + wb + wc
        a = (wa / den) * o1[rs, :] + (wb / den) * o2[rs, :] + (wc / den) * o3[rs, :]
        a_ref[rs, :] = a.astype(a_ref.dtype)
        lse_ref[rs, :] = m + jnp.log(den)
        return carry

    lax.fori_loop(0, q_ref.shape[0] // DIL_CHUNK, merge, 0)


def _dil_fwd(q, k, v):
    T = q.shape[0]
    spec = pl.BlockSpec((T, HEAD), lambda h: (0, h))
    return _pcall(
        functools.partial(_dil_fwd_body, nt=T // HEAD, unroll=4), name="dil_fwd",
        grid=(NH,), in_specs=[spec] * 3, out_specs=[spec] * 2,
        out_shape=[jax.ShapeDtypeStruct((T, A_W), MXU_DTYPE), jax.ShapeDtypeStruct((T, A_W), F32)],
        scratch_shapes=[pltpu.VMEM((T, HEAD), F32)] * 6,
        compiler_params=pltpu.CompilerParams(dimension_semantics=("parallel",)),
    )(q, k, v)


def _dil_bwd_body(q_ref, k_ref, v_ref, do_ref, a_ref, lse_ref, dq_ref, dk_ref, dv_ref, dl_sc, *, nt, unroll):
    band, is_cur = _dil_band()

    def prep(i, carry):
        rs = pl.ds(pl.multiple_of(i * DIL_CHUNK, DIL_CHUNK), DIL_CHUNK)
        dl = jnp.sum(do_ref[rs, :] * a_ref[rs, :].astype(F32), axis=1, keepdims=True)
        dl_sc[rs, :] = jnp.broadcast_to(dl, (DIL_CHUNK, HEAD))
        zero = jnp.zeros((DIL_CHUNK, HEAD), F32)
        dq_ref[rs, :] = zero
        dk_ref[rs, :] = zero
        dv_ref[rs, :] = zero
        return carry

    lax.fori_loop(0, q_ref.shape[0] // DIL_CHUNK, prep, 0)

    for d in DIL:

        def tile(t, carry, d=d):
            rows, prows, has_prev = _dil_rows(t, d)
            q = q_ref[rows, :].astype(MXU_DTYPE)
            kk = jnp.concatenate([k_ref[prows, :], k_ref[rows, :]], axis=0).astype(MXU_DTYPE)
            vv = jnp.concatenate([v_ref[prows, :], v_ref[rows, :]], axis=0).astype(MXU_DTYPE)
            do = do_ref[rows, :].astype(MXU_DTYPE)
            lse = lse_ref[rows, :]
            dl = dl_sc[rows, :]
            ok = band & (is_cur | has_prev)
            s = _dot_nt(q, kk) * DIL_SCALE
            p = jnp.where(ok, jnp.exp(s - jnp.concatenate([lse, lse], axis=1)), 0.0)
            ds = (p * (_dot_nt(do, vv) - jnp.concatenate([dl, dl], axis=1))).astype(MXU_DTYPE)
            dq_ref[rows, :] += _dot_nn(ds, kk) * DIL_SCALE
            dkk = _dot_tn(ds, q) * DIL_SCALE
            dvv = _dot_tn(p.astype(MXU_DTYPE), do)
            dk_ref[rows, :] += dkk[HEAD:, :]
            dv_ref[rows, :] += dvv[HEAD:, :]
            dk_ref[prows, :] += dkk[:HEAD, :]
            dv_ref[prows, :] += dvv[:HEAD, :]
            return carry

        lax.fori_loop(0, nt, tile, 0, unroll=unroll)


def _dil_bwd(q, k, v, dmix, mixed, lse):
    T = q.shape[0]
    spec = pl.BlockSpec((T, HEAD), lambda h: (0, h))
    return _pcall(
        functools.partial(_dil_bwd_body, nt=T // HEAD, unroll=2), name="dil_bwd",
        grid=(NH,), in_specs=[spec] * 6, out_specs=[spec] * 3,
        out_shape=[jax.ShapeDtypeStruct((T, A_W), F32)] * 3,
        scratch_shapes=[pltpu.VMEM((T, HEAD), F32)],
        compiler_params=pltpu.CompilerParams(dimension_semantics=("parallel",)),
    )(q, k, v, dmix, mixed, lse)


MLA_SCALE = (HEAD + ROPE_B) ** -0.5
MLA_T = 512
MLA_HP = 2


def _tri(t):
    row = lax.broadcasted_iota(jnp.int32, (t, t), 0)
    col = lax.broadcasted_iota(jnp.int32, (t, t), 1)
    return col <= row


def _lanes(x, n):
    return jnp.tile(x, (1, n // HEAD))


def _mla_fwd_body(q_ref, kn_ref, kr_ref, v_ref, o_ref, lse_ref, m_sc, l_sc, acc_sc, *, t, hp):
    qi = pl.program_id(1)
    m_sc[...] = jnp.full(m_sc.shape, NEG, F32)
    l_sc[...] = jnp.zeros(l_sc.shape, F32)
    acc_sc[...] = jnp.zeros(acc_sc.shape, F32)

    def step(j, masked):
        ks = pl.ds(pl.multiple_of(j * t, t), t)
        kr = kr_ref[ks, :]
        for hh in range(hp):
            kcat = jnp.concatenate([kn_ref[ks, hh * HEAD:(hh + 1) * HEAD], kr], axis=1)
            s = _dot_nt(q_ref[:, hh * QPAD:(hh + 1) * QPAD], kcat) * MLA_SCALE
            if masked:
                s = jnp.where(_tri(t), s, NEG)
            m_prev = m_sc[hh]
            m_new = jnp.maximum(m_prev, jnp.max(s, axis=1, keepdims=True))
            alpha = jnp.exp(m_prev - m_new)
            p = jnp.exp(s - _lanes(m_new, t))
            l_sc[hh] = alpha * l_sc[hh] + jnp.sum(p, axis=1, keepdims=True)
            acc_sc[hh] = alpha * acc_sc[hh] + _dot_nn(p.astype(MXU_DTYPE), v_ref[ks, hh * HEAD:(hh + 1) * HEAD])
            m_sc[hh] = m_new

    def off_diag(j, carry):
        step(j, False)
        return carry

    lax.fori_loop(0, qi, off_diag, 0)
    step(qi, True)
    for hh in range(hp):
        l = l_sc[hh]
        o_ref[:, hh * HEAD:(hh + 1) * HEAD] = (acc_sc[hh] / l).astype(o_ref.dtype)
        lse_ref[:, hh * HEAD:(hh + 1) * HEAD] = m_sc[hh] + jnp.log(l)


def _mla_fwd(qf, kv, kr):
    T = qf.shape[0]
    t, hp = min(MLA_T, T), MLA_HP
    ng = NH // hp
    return _pcall(
        functools.partial(_mla_fwd_body, t=t, hp=hp), name="mla_fwd",
        grid=(ng, T // t),
        in_specs=[pl.BlockSpec((t, hp * QPAD), lambda g, i: (i, g)),
                  pl.BlockSpec((T, hp * HEAD), lambda g, i: (0, g)),
                  pl.BlockSpec((T, HEAD), lambda g, i: (0, 0)),
                  pl.BlockSpec((T, hp * HEAD), lambda g, i: (0, ng + g))],
        out_specs=[pl.BlockSpec((t, hp * HEAD), lambda g, i: (i, g))] * 2,
        out_shape=[jax.ShapeDtypeStruct((T, A_W), MXU_DTYPE), jax.ShapeDtypeStruct((T, A_W), F32)],
        scratch_shapes=[pltpu.VMEM((hp, t, HEAD), F32)] * 3,
        compiler_params=pltpu.CompilerParams(dimension_semantics=("parallel", "parallel")),
    )(qf, kv, kr, kv)


def _mla_bwd_body(q_ref, kn_ref, kr_ref, v_ref, do_ref, o_ref, lse_ref, dq_ref, dkn_ref, dv_ref, dkr_ref,
                  dl_sc, dk_sc, dv_sc, *, t):
    ki = pl.program_id(1)
    nq = q_ref.shape[0] // t

    @pl.when(ki == 0)
    def _():
        def prep(i, carry):
            rs = pl.ds(pl.multiple_of(i * t, t), t)
            dl = jnp.sum(do_ref[rs, :] * o_ref[rs, :].astype(F32), axis=1, keepdims=True)
            dl_sc[rs, :] = jnp.broadcast_to(dl, (t, HEAD))
            dq_ref[rs, :] = jnp.zeros((t, QPAD), F32)
            return carry
        lax.fori_loop(0, nq, prep, 0)

    kcat = jnp.concatenate([kn_ref[...], kr_ref[...]], axis=1)
    v = v_ref[...]
    dk_sc[...] = jnp.zeros(dk_sc.shape, F32)
    dv_sc[...] = jnp.zeros(dv_sc.shape, F32)

    def step(i, masked):
        qs = pl.ds(pl.multiple_of(i * t, t), t)
        q = q_ref[qs, :]
        do = do_ref[qs, :].astype(MXU_DTYPE)
        p = jnp.exp(_dot_nt(q, kcat) * MLA_SCALE - _lanes(lse_ref[qs, :], t))
        if masked:
            p = jnp.where(_tri(t), p, 0.0)
        ds = (p * (_dot_nt(do, v) - _lanes(dl_sc[qs, :], t))).astype(MXU_DTYPE)
        dv_sc[...] += _dot_tn(p.astype(MXU_DTYPE), do)
        dk_sc[...] += _dot_tn(ds, q)
        dq_ref[qs, :] += _dot_nn(ds, kcat) * MLA_SCALE

    step(ki, True)

    def off_diag(i, carry):
        step(i, False)
        return carry

    lax.fori_loop(ki + 1, nq, off_diag, 0)
    dk = dk_sc[...] * MLA_SCALE
    dkn_ref[...] = dk[:, 0:HEAD].astype(dkn_ref.dtype)
    dkr_ref[...] = dk[:, HEAD:QPAD]
    dv_ref[...] = dv_sc[...].astype(dv_ref.dtype)


def _mla_bwd(qf, kv, kr, dmix, mixed, lse):
    T = qf.shape[0]
    t = min(MLA_T, T)
    head = lambda h, j: (0, h)
    b_half = lambda h, j: (0, NH + h)
    kblk = pl.BlockSpec((t, HEAD), lambda h, j: (j, h))
    return _pcall(
        functools.partial(_mla_bwd_body, t=t), name="mla_bwd",
        grid=(NH, T // t),
        in_specs=[pl.BlockSpec((T, QPAD), head), kblk,
                  pl.BlockSpec((t, HEAD), lambda h, j: (j, 0)),
                  pl.BlockSpec((t, HEAD), lambda h, j: (j, NH + h)),
                  pl.BlockSpec((T, HEAD), b_half), pl.BlockSpec((T, HEAD), b_half),
                  pl.BlockSpec((T, HEAD), head)],
        out_specs=[pl.BlockSpec((T, QPAD), head), kblk, kblk, kblk],
        out_shape=[jax.ShapeDtypeStruct((T, NH * QPAD), F32), jax.ShapeDtypeStruct((T, A_W), MXU_DTYPE),
                   jax.ShapeDtypeStruct((T, A_W), MXU_DTYPE), jax.ShapeDtypeStruct((T, A_W), F32)],
        scratch_shapes=[pltpu.VMEM((T, HEAD), F32), pltpu.VMEM((t, QPAD), F32), pltpu.VMEM((t, HEAD), F32)],
        compiler_params=pltpu.CompilerParams(dimension_semantics=("parallel", "arbitrary")),
    )(qf, kv, kr, kv, dmix, mixed, lse)


def _local_step(x, pos, target, g1, g2, gq, gkv, g3, g4,
                in_weights, attn_weights, mlp_prefetch, mlp_weights, mlp_grads_ready, attn_grads_ready):
    T = x.shape[0]
    TR = 256
    mm = functools.partial(_matmul, tm=512, tn=1024, tk=2048, b_outer=True)
    mm_k = functools.partial(_matmul, tm=1024, tn=1024, tk=2048)
    mm_g = functools.partial(_matmul, tm=512, tn=1024, tk=4096, b_outer=True)

    inv_a = ROPE_THETA ** (-jnp.arange(0, ROT_A, 2, dtype=F32) / ROT_A)
    inv_b = ROPE_THETA ** (-jnp.arange(0, ROPE_B, 2, dtype=F32) / ROPE_B)
    inv = jnp.stack([jnp.concatenate([inv_a, inv_a, jnp.zeros((HEAD - ROT_A,), F32)]),
                     jnp.concatenate([inv_b, inv_b, jnp.zeros((HEAD - ROPE_B,), F32)])])
    inv = jnp.concatenate([inv, jnp.zeros((6, HEAD), F32)], axis=0)
    tabs = _rowwise(_rope_tab_body, [pos], [inv], [(HEAD, F32)] * 6, [], tr=512, name="rope_tables")

    (h,) = _rowwise(_rms_fwd_body, [x], [g1], [(D_MODEL, MXU_DTYPE)], [], tr=TR, name="rms_in")
    w_main, w_kr = in_weights(h)
    (proj,) = mm(h, w_main, dims="nn", out_dtypes=[F32], name="proj_main")
    (kr_raw,) = mm(h, w_kr, dims="nn", out_dtypes=[F32], name="proj_kr")
    q, k, v, cqn, ckvn, krope = _rowwise(
        _postproj_body, [proj, kr_raw] + tabs, [gq, gkv],
        [(A_W, F32)] * 3 + [(LORA, MXU_DTYPE)] * 2 + [(HEAD, MXU_DTYPE)], [], tr=TR, name="post_proj")
    a_out, lse_a = _dil_fwd(q, k, v)

    w_uq_p, w_ukv_p, w_out = attn_weights(cqn)
    (q_pad,) = mm(cqn, w_uq_p, dims="nn", out_dtypes=[F32], name="q_up")
    (qf,) = _rowwise(_qrope_body, [q_pad] + tabs[3:], [], [(NH * QPAD, MXU_DTYPE)], [], tr=TR, name="q_rope")
    (kv,) = mm(ckvn, w_ukv_p, dims="nn", out_dtypes=[MXU_DTYPE], name="kv_up")
    b_out, lse_b = _mla_fwd(qf, kv, krope)
    mlp_prefetch(b_out)

    mixed = jnp.concatenate([a_out, b_out], axis=1)
    (o,) = mm(mixed, w_out, dims="nn", out_dtypes=[F32], name="out_proj")
    x1, h2 = _rowwise(_mid_body, [x, o], [g2, g3], [(D_MODEL, F32), (D_MODEL, MXU_DTYPE)], [], tr=TR, name="mid_norm")

    w_up, w_down = mlp_weights(h2)

    def up_epi(acc):
        r = jnp.maximum(acc, 0.0)
        return r * r, r
    u, r = mm(h2, w_up, dims="nn", out_dtypes=[MXU_DTYPE, MXU_DTYPE], name="mlp_up", epi=up_epi)
    (dn,) = mm_k(u, w_down, dims="nn", out_dtypes=[F32], name="mlp_down")
    dy, dd, loss8, dg4 = _rowwise(_loss_body, [x1, dn, target], [g4], [(D_MODEL, F32), (D_MODEL, MXU_DTYPE)],
                                  [(8, HEAD), (8, D_MODEL)], tr=TR, name="loss_head")

    def dup_epi(acc, rr):
        return (acc * (2.0 * rr.astype(F32)),)
    (dup,) = mm(dd, w_down, dims="nt", out_dtypes=[MXU_DTYPE], name="d_up", epi=dup_epi, extras=(r,))
    (gw_down,) = mm_g(u, dd, dims="tn", out_dtypes=[WIRE_DTYPE], name="gw_down")
    (dh2,) = mm_k(dup, w_up, dims="nt", out_dtypes=[F32], name="d_h2")
    (gw_up,) = mm_g(h2, dup, dims="tn", out_dtypes=[WIRE_DTYPE], name="gw_up")
    g2 = g2 + mlp_grads_ready(gw_up, gw_down)
    dx1, do, dg3, dg2 = _rowwise(_bmid_body, [dy, dh2, x1, o], [g2, g3], [(D_MODEL, F32), (D_MODEL, MXU_DTYPE)],
                                 [(8, D_MODEL), (8, D_MODEL)], tr=TR, name="bwd_mid")
    (dmix,) = mm(do, w_out, dims="nt", out_dtypes=[F32], name="d_mixed")
    (gw_out,) = mm_g(mixed, do, dims="tn", out_dtypes=[WIRE_DTYPE], name="gw_out")

    dqf, dkn, dvb, dkr = _mla_bwd(qf, kv, krope, dmix, mixed, lse_b)
    (dq_pad,) = _rowwise(_qrope_t_body, [dqf] + tabs[3:], [], [(NH * QPAD, MXU_DTYPE)], [], tr=TR, name="q_rope_t")
    (dcqn,) = mm(dq_pad, w_uq_p, dims="nt", out_dtypes=[F32], name="d_cq")
    (gw_uq_p,) = mm_g(cqn, dq_pad, dims="tn", out_dtypes=[WIRE_DTYPE], name="gw_uq")
    dkv = jnp.concatenate([dkn, dvb], axis=1)
    (dckvn,) = mm(dkv, w_ukv_p, dims="nt", out_dtypes=[F32], name="d_ckv")
    (gw_ukv_p,) = mm_g(ckvn, dkv, dims="tn", out_dtypes=[WIRE_DTYPE], name="gw_ukv")
    gq = gq + attn_grads_ready(gw_out, gw_uq_p, gw_ukv_p)

    dq_a, dk_a, dv_a = _dil_bwd(q, k, v, dmix, mixed, lse_a)
    dproj, dkrp, dgq, dgkv = _rowwise(
        _dproj_body, [dq_a, dk_a, dv_a, dcqn, dckvn, proj, dkr] + tabs, [gq, gkv],
        [(MAIN_COLS, MXU_DTYPE), (HEAD, MXU_DTYPE)], [(8, LORA), (8, LORA)], tr=TR, name="d_proj")
    (dha,) = mm_k(dproj, w_main, dims="nt", out_dtypes=[F32], name="d_h_main")
    (dhb,) = mm(dkrp, w_kr, dims="nt", out_dtypes=[F32], name="d_h_kr")
    (gw_main,) = mm_g(h, dproj, dims="tn", out_dtypes=[WIRE_DTYPE], name="gw_in_main")
    (gw_kr,) = mm_g(h, dkrp, dims="tn", out_dtypes=[WIRE_DTYPE], name="gw_in_kr")
    dx, dg1 = _rowwise(_bin_body, [dx1, dha, dhb, x], [g1], [(D_MODEL, F32)], [(8, D_MODEL)], tr=TR, name="bwd_in")

    small = jnp.concatenate([dg1, dg2, dgq, dgkv, dg3, dg4, loss8], axis=1)
    return dx, (gw_main, gw_kr), small


def _place():
    x, y, c = lax.axis_index("x"), lax.axis_index("y"), lax.axis_index("c")
    chips = [(1 - x, y), (x, 1 - y), (1 - x, 1 - y)]
    return x, y, c, chips


def _cast_place_body(me_ref, w_ref, o_ref):
    o_ref[...] = w_ref[...].astype(o_ref.dtype)


def _cast_place(me_arr, w, name):
    rows, cols = w.shape
    tr = min(rows, 256)
    grid_spec = pltpu.PrefetchScalarGridSpec(
        num_scalar_prefetch=1, grid=(rows // tr,),
        in_specs=[pl.BlockSpec((tr, cols), lambda i, me: (i, 0))],
        out_specs=pl.BlockSpec((None, tr, cols), lambda i, me: (me[0], i, 0)))
    return _pcall(
        _cast_place_body, name=name, grid_spec=grid_spec,
        out_shape=jax.ShapeDtypeStruct((N_CHIPS, rows, cols), WIRE_DTYPE),
        compiler_params=pltpu.CompilerParams(dimension_semantics=("parallel",)),
    )(me_arr, w)


HBM = pl.BlockSpec(memory_space=pltpu.HBM)
SEM = pl.BlockSpec(memory_space=pltpu.SEMAPHORE)
EFFECT = pltpu.SideEffectType.DATAFLOW_SIDE_EFFECTING


def _in_hbm(a):
    return pltpu.with_memory_space_constraint(a, pltpu.HBM)


def _ag_descs(bufs, send_sems, recv_sems):
    x, y, c, chips = _place()
    me = 2 * x + y
    out = []
    for w, buf in enumerate(bufs):
        half = buf.shape[1] // 2
        rows = pl.ds(pl.multiple_of(c * half, 16), half)
        mine = buf.at[me, rows]
        for j, (px, py) in enumerate(chips):
            landed = buf.at[2 * px + py, rows]
            mk = lambda ref, w=w, j=j, px=px, py=py: pltpu.make_async_remote_copy(
                src_ref=ref, dst_ref=ref, send_sem=send_sems.at[w * 3 + j], recv_sem=recv_sems.at[w * 3 + j],
                device_id=(px, py, c), device_id_type=MESH)
            out.append((mk(mine), mk(landed)))
    return out


def _ag_start_body(*refs, n_w):
    bufs = refs[:n_w]
    send_sems, recv_sems = refs[-n_w - 3], refs[-n_w - 2]
    token = refs[-1]
    for send, _ in _ag_descs(bufs, send_sems, recv_sems):
        send.start()
    token[...] = jnp.zeros_like(token)


def _ag_start(placed, after, tag):
    n_w = len(placed)
    after = [] if after is None else [after]
    res = _pcall(
        functools.partial(_ag_start_body, n_w=n_w), name="weight_allgather_start_" + tag,
        in_specs=[HBM] * n_w + [ANY] * len(after),
        out_specs=[SEM, SEM] + [HBM] * n_w + [pl.BlockSpec(memory_space=pltpu.VMEM)],
        out_shape=[pltpu.SemaphoreType.DMA((3 * n_w,)), pltpu.SemaphoreType.DMA((3 * n_w,))]
        + [pltpu.HBM(p.shape, p.dtype) for p in placed] + [jax.ShapeDtypeStruct((8, HEAD), F32)],
        input_output_aliases={w: 2 + w for w in range(n_w)},
        compiler_params=pltpu.CompilerParams(has_side_effects=EFFECT),
    )(*[_in_hbm(p) for p in placed], *after)
    return res[0], res[1], list(res[2:2 + n_w]), res[-1]


def _ag_wait_body(*refs, n_w):
    bufs = refs[:n_w]
    send_sems, recv_sems = refs[n_w], refs[n_w + 1]
    for send, recv in _ag_descs(bufs, send_sems, recv_sems):
        send.wait_send()
        recv.wait_recv()


def _ag_wait(send_sems, recv_sems, bufs, after, tag):
    n_w = len(bufs)
    return list(_pcall(
        functools.partial(_ag_wait_body, n_w=n_w), name="weight_allgather_wait_" + tag,
        in_specs=[HBM] * n_w + [SEM, SEM, ANY], out_specs=[HBM] * n_w,
        out_shape=[pltpu.HBM(b.shape, b.dtype) for b in bufs],
        input_output_aliases={w: w for w in range(n_w)},
        compiler_params=pltpu.CompilerParams(has_side_effects=EFFECT),
    )(*bufs, send_sems, recv_sems, after))


def _fw_descs(bufs, send_sems, recv_sems):
    x, y, c, chips = _place()
    out = []
    for w, buf in enumerate(bufs):
        half = buf.shape[1] // 2
        for j, (px, py) in enumerate(chips):
            def mk(which, w=w, j=j, buf=buf, half=half, px=px, py=py):
                ref = buf.at[2 * px + py, pl.ds(pl.multiple_of(which * half, 16), half)]
                return pltpu.make_async_remote_copy(
                    src_ref=ref, dst_ref=ref, send_sem=send_sems.at[w * 3 + j], recv_sem=recv_sems.at[w * 3 + j],
                    device_id=(x, y, 1 - c), device_id_type=MESH)
            out.append((mk(c), mk(1 - c)))
    return out


def _fw_start_body(*refs, n_w):
    bufs = refs[:n_w]
    send_sems, recv_sems = refs[n_w], refs[n_w + 1]
    token = refs[-1]
    for send, _ in _fw_descs(bufs, send_sems, recv_sems):
        send.start()
    token[...] = jnp.zeros_like(token)


def _fw_start(bufs, tag):
    n_w = len(bufs)
    res = _pcall(
        functools.partial(_fw_start_body, n_w=n_w), name="weight_allgather_forward_start_" + tag,
        in_specs=[HBM] * n_w,
        out_specs=[SEM, SEM] + [HBM] * n_w + [pl.BlockSpec(memory_space=pltpu.VMEM)],
        out_shape=[pltpu.SemaphoreType.DMA((3 * n_w,)), pltpu.SemaphoreType.DMA((3 * n_w,))]
        + [pltpu.HBM(b.shape, b.dtype) for b in bufs] + [jax.ShapeDtypeStruct((8, HEAD), F32)],
        input_output_aliases={w: 2 + w for w in range(n_w)},
        compiler_params=pltpu.CompilerParams(has_side_effects=EFFECT),
    )(*bufs)
    return res[0], res[1], list(res[2:2 + n_w]), res[-1]


def _fw_wait_body(*refs, n_w):
    bufs = refs[:n_w]
    send_sems, recv_sems = refs[n_w], refs[n_w + 1]
    for send, back in _fw_descs(bufs, send_sems, recv_sems):
        send.wait_send()
        back.wait_recv()


def _fw_wait(send_sems, recv_sems, bufs, after, tag):
    n_w = len(bufs)
    return list(_pcall(
        functools.partial(_fw_wait_body, n_w=n_w), name="weight_allgather_forward_wait_" + tag,
        in_specs=[HBM] * n_w + [SEM, SEM, ANY], out_specs=[HBM] * n_w,
        out_shape=[pltpu.HBM(b.shape, b.dtype) for b in bufs],
        input_output_aliases={w: w for w in range(n_w)},
        compiler_params=pltpu.CompilerParams(has_side_effects=EFFECT),
    )(*bufs, send_sems, recv_sems, after))


def _ag_forward_body(*refs, n_w):
    bufs = refs[n_w:2 * n_w]
    send_sems, recv_sems = refs[2 * n_w:]
    x, y, c, chips = _place()
    fwds = []
    for w, buf in enumerate(bufs):
        half = buf.shape[1] // 2
        for j, (px, py) in enumerate(chips):
            def piece(which, buf=buf, half=half, px=px, py=py):
                return buf.at[2 * px + py, pl.ds(pl.multiple_of(which * half, 16), half)]
            mk = lambda ref, w=w, j=j: pltpu.make_async_remote_copy(
                src_ref=ref, dst_ref=ref, send_sem=send_sems.at[w * 3 + j], recv_sem=recv_sems.at[w * 3 + j],
                device_id=(x, y, 1 - c), device_id_type=MESH)
            fw = mk(piece(c))
            fw.start()
            fwds.append((fw, mk(piece(1 - c))))
    for fw, back in fwds:
        back.wait_recv()
        fw.wait_send()


def _ag_forward(bufs, tag):
    n_w = len(bufs)
    return list(_pcall(
        functools.partial(_ag_forward_body, n_w=n_w), name="weight_allgather_forward_" + tag,
        in_specs=[ANY] * n_w, out_specs=[ANY] * n_w,
        out_shape=[jax.ShapeDtypeStruct(b.shape, b.dtype) for b in bufs],
        input_output_aliases={w: w for w in range(n_w)},
        scratch_shapes=[pltpu.SemaphoreType.DMA((3 * n_w,))] * 2,
    )(*bufs))


def _sc_descs(ins, outs, send_sems, recv_sems):
    x, y, c, chips = _place()
    me = 2 * x + y
    out = []
    for w in range(len(ins)):
        for j, (px, py) in enumerate(chips):
            out.append(pltpu.make_async_remote_copy(
                src_ref=ins[w].at[2 * px + py], dst_ref=outs[w].at[me],
                send_sem=send_sems.at[w * 3 + j], recv_sem=recv_sems.at[w * 3 + j],
                device_id=(px, py, c), device_id_type=MESH))
    return out


def _scatter_start_body(*refs, n_w):
    ins, lands = refs[:n_w], refs[n_w:2 * n_w]
    send_sems, recv_sems = refs[-2 * n_w - 3], refs[-2 * n_w - 2]
    token = refs[-1]
    for cp in _sc_descs(ins, lands, send_sems, recv_sems):
        cp.start()
    token[...] = jnp.zeros_like(token)


def _scatter_start(parts, after, tag):
    n_w = len(parts)
    lands = [lax.empty(p.shape, p.dtype) for p in parts]
    after = [] if after is None else [after]
    res = _pcall(
        functools.partial(_scatter_start_body, n_w=n_w), name="grad_scatter_start_" + tag,
        in_specs=[HBM] * (2 * n_w) + [ANY] * len(after),
        out_specs=[SEM, SEM] + [HBM] * (2 * n_w) + [pl.BlockSpec(memory_space=pltpu.VMEM)],
        out_shape=[pltpu.SemaphoreType.DMA((3 * n_w,)), pltpu.SemaphoreType.DMA((3 * n_w,))]
        + [pltpu.HBM(p.shape, p.dtype) for p in parts] * 2 + [jax.ShapeDtypeStruct((8, HEAD), F32)],
        input_output_aliases={i: 2 + i for i in range(2 * n_w)},
        compiler_params=pltpu.CompilerParams(has_side_effects=EFFECT),
    )(*[_in_hbm(p) for p in parts], *[_in_hbm(l) for l in lands], *after)
    return res[0], res[1], list(res[2:2 + n_w]), list(res[2 + n_w:2 + 2 * n_w]), res[-1]


def _scatter_wait_body(*refs, n_w):
    ins, lands = refs[:n_w], refs[n_w:2 * n_w]
    send_sems, recv_sems = refs[2 * n_w], refs[2 * n_w + 1]
    for cp in _sc_descs(ins, lands, send_sems, recv_sems):
        cp.wait_send()
        cp.wait_recv()


def _scatter_wait(send_sems, recv_sems, parts, lands, after, tag):
    n_w = len(parts)
    res = _pcall(
        functools.partial(_scatter_wait_body, n_w=n_w), name="grad_scatter_wait_" + tag,
        in_specs=[HBM] * (2 * n_w) + [SEM, SEM, ANY], out_specs=[HBM] * (2 * n_w),
        out_shape=[pltpu.HBM(p.shape, p.dtype) for p in parts] * 2,
        input_output_aliases={i: i for i in range(2 * n_w)},
        compiler_params=pltpu.CompilerParams(has_side_effects=EFFECT),
    )(*parts, *lands, send_sems, recv_sems, after)
    return list(res[:n_w]), list(res[n_w:])


def _pair_send_body(*refs, n_w):
    ins, outs = refs[:n_w], refs[n_w:2 * n_w]
    send_sems, recv_sems = refs[2 * n_w:]
    x, y, c, _ = _place()
    cps = []
    for w in range(n_w):
        cp = pltpu.make_async_remote_copy(
            src_ref=ins[w].at[:, 1 - c], dst_ref=outs[w],
            send_sem=send_sems.at[w], recv_sem=recv_sems.at[w],
            device_id=(x, y, 1 - c), device_id_type=MESH)
        cp.start()
        cps.append(cp)
    for cp in cps:
        cp.wait()


def _pair_send(grads4, tag):
    n_w = len(grads4)
    return _pcall(
        functools.partial(_pair_send_body, n_w=n_w), name="grad_pair_exchange_" + tag,
        in_specs=[ANY] * n_w, out_specs=[ANY] * n_w,
        out_shape=[jax.ShapeDtypeStruct((g.shape[0],) + g.shape[2:], g.dtype) for g in grads4],
        scratch_shapes=[pltpu.SemaphoreType.DMA((n_w,))] * 2,
    )(*grads4)


def _pair_add_body(c_ref, mine_ref, theirs_ref, o_ref):
    o_ref[...] = (mine_ref[...].astype(F32) + theirs_ref[...].astype(F32)).astype(o_ref.dtype)


def _pair_add(c_arr, g4, recv, name):
    _, _, hr, cols = g4.shape
    tr = min(hr, 256)
    grid_spec = pltpu.PrefetchScalarGridSpec(
        num_scalar_prefetch=1, grid=(N_CHIPS, hr // tr),
        in_specs=[pl.BlockSpec((None, None, tr, cols), lambda s, i, c: (s, c[0], i, 0)),
                  pl.BlockSpec((None, tr, cols), lambda s, i, c: (s, i, 0))],
        out_specs=pl.BlockSpec((None, tr, cols), lambda s, i, c: (s, i, 0)))
    return _pcall(
        _pair_add_body, name=name, grid_spec=grid_spec,
        out_shape=jax.ShapeDtypeStruct(recv.shape, recv.dtype),
        compiler_params=pltpu.CompilerParams(dimension_semantics=("parallel", "parallel")),
    )(c_arr, g4, recv)


def _sum4_body(me_ref, p_ref, l0, l1, l2, l3, o_ref):
    me = me_ref[0]
    t = [jnp.where(me == j, p_ref[...], l[...]).astype(F32) for j, l in enumerate((l0, l1, l2, l3))]
    o_ref[...] = ((t[0] + t[1]) + t[2]) + t[3]


def _sum4(me_arr, part, landed, name):
    _, hr, cols = part.shape
    tr = min(hr, 256)

    def slot(j):
        return lambda i, me: (jnp.where(me[0] == j, (j + 1) % N_CHIPS, j), i, 0)

    grid_spec = pltpu.PrefetchScalarGridSpec(
        num_scalar_prefetch=1, grid=(hr // tr,),
        in_specs=[pl.BlockSpec((None, tr, cols), lambda i, me: (me[0], i, 0))]
        + [pl.BlockSpec((None, tr, cols), slot(j)) for j in range(N_CHIPS)],
        out_specs=pl.BlockSpec((tr, cols), lambda i, me: (i, 0)))
    return _pcall(
        _sum4_body, name=name, grid_spec=grid_spec,
        out_shape=jax.ShapeDtypeStruct((hr, cols), F32),
        compiler_params=pltpu.CompilerParams(dimension_semantics=("parallel",)),
    )(me_arr, part, landed, landed, landed, landed)


def _pair_swap_body(*refs, n_w):
    ins, outs = refs[:n_w], refs[n_w:2 * n_w]
    send_sems, recv_sems = refs[2 * n_w:]
    x, y, c, _ = _place()
    todo = []
    for w in range(n_w):
        cp = pltpu.make_async_remote_copy(
            src_ref=ins[w], dst_ref=outs[w],
            send_sem=send_sems.at[w], recv_sem=recv_sems.at[w],
            device_id=(x, y, 1 - c), device_id_type=MESH)
        cp.start()
        todo.append(cp)
    for t in todo:
        t.wait()


def _pair_swap(halves, tag):
    n_w = len(halves)
    return _pcall(
        functools.partial(_pair_swap_body, n_w=n_w), name="grad_pair_swap_" + tag,
        in_specs=[ANY] * n_w, out_specs=[ANY] * n_w,
        out_shape=[jax.ShapeDtypeStruct(h.shape, h.dtype) for h in halves],
        scratch_shapes=[pltpu.SemaphoreType.DMA((n_w,))] * 2,
    )(*halves)


def _small_gather_body(x_ref, out_ref, send_sems, recv_sems, local_sem):
    m_per = x_ref.shape[0]
    x, y, c, chips = _place()
    me, sibling = (x, y, c), (x, y, 1 - c)

    def rows(px, py, pc):
        return out_ref.at[pl.ds((4 * px + 2 * py + pc) * m_per, m_per), :]

    def copy(k, block, to, src=None):
        return pltpu.make_async_remote_copy(
            src_ref=rows(*block) if src is None else src, dst_ref=rows(*block),
            send_sem=send_sems.at[k], recv_sem=recv_sems.at[k], device_id=to, device_id_type=MESH)

    mine = pltpu.make_async_copy(x_ref, rows(*me), local_sem)
    mine.start()
    first = [copy(0, me, sibling, src=x_ref)]
    first += [copy(1 + j, me, (*chip, c), src=x_ref) for j, chip in enumerate(chips)]
    for cp in first:
        cp.start()
    passed = [copy(4 + j, (*chip, c), sibling) for j, chip in enumerate(chips)]
    for j, chip in enumerate(chips):
        copy(1 + j, (*chip, c), me).wait_recv()
        passed[j].start()
    copy(0, sibling, me).wait_recv()
    for j, chip in enumerate(chips):
        copy(4 + j, (*chip, 1 - c), me).wait_recv()
    for cp in first + passed:
        cp.wait_send()
    mine.wait()


def _small_gather(small):
    m_per, n = small.shape
    return _pcall(
        _small_gather_body, name="small_allgather",
        out_shape=jax.ShapeDtypeStruct((N_DEV * m_per, n), small.dtype),
        in_specs=[pl.BlockSpec(memory_space=pltpu.VMEM)],
        out_specs=pl.BlockSpec(memory_space=pltpu.VMEM),
        scratch_shapes=[pltpu.SemaphoreType.DMA((7,)), pltpu.SemaphoreType.DMA((7,)), pltpu.SemaphoreType.DMA],
    )(small)


def _adamw(w, g, m, v):
    m = ADAM_B1 * m + (1.0 - ADAM_B1) * g
    v = ADAM_B2 * v + (1.0 - ADAM_B2) * (g * g)
    m_hat = m / (1.0 - ADAM_B1 ** ADAM_STEP)
    v_hat = v / (1.0 - ADAM_B2 ** ADAM_STEP)
    delta = -ADAM_LR * (m_hat / (jnp.sqrt(v_hat) + ADAM_EPS) + ADAM_WD * w)
    return delta, m, v


def _adamw_body(c_ref, w_ref, own_ref, sib_ref, m_ref, v_ref, g_ref, d_ref, nm_ref, nv_ref, *, nh):
    mine = (pl.program_id(0) // nh) == c_ref[0]
    g = jnp.where(mine, own_ref[...], sib_ref[...])
    g_ref[...] = g
    d, m, v = _adamw(w_ref[...], g, m_ref[...], v_ref[...])
    d_ref[...] = d
    nm_ref[...] = m
    nv_ref[...] = v


def _adamw_call(c_arr, w, own, sib, m, v, name):
    rows, cols = w.shape
    tr = min(rows // 2, 256)
    nh = (rows // 2) // tr
    full = pl.BlockSpec((tr, cols), lambda i, c: (i, 0))
    own_spec = pl.BlockSpec((tr, cols), lambda i, c: (jnp.clip(i - c[0] * nh, 0, nh - 1), 0))
    sib_spec = pl.BlockSpec((tr, cols), lambda i, c: (jnp.clip(i - (1 - c[0]) * nh, 0, nh - 1), 0))
    grid_spec = pltpu.PrefetchScalarGridSpec(
        num_scalar_prefetch=1, grid=(rows // tr,),
        in_specs=[full, own_spec, sib_spec, full, full], out_specs=[full] * 4)
    return _pcall(
        functools.partial(_adamw_body, nh=nh), name=name, grid_spec=grid_spec,
        out_shape=[jax.ShapeDtypeStruct(w.shape, F32)] * 4,
        compiler_params=pltpu.CompilerParams(dimension_semantics=("parallel",)),
    )(c_arr, w, own, sib, m, v)


def _small_update_body(gath_ref, w_ref, m_ref, v_ref, g_ref, d_ref, nm_ref, nv_ref, loss_ref, *, n_gain):
    tot = gath_ref[0:1, :]
    for i in range(1, gath_ref.shape[0]):
        tot = tot + gath_ref[i:i + 1, :]
    g = tot[:, 0:n_gain]
    g_ref[...] = g
    d, m, v = _adamw(w_ref[...], g, m_ref[...], v_ref[...])
    d_ref[...] = d
    nm_ref[...] = m
    nv_ref[...] = v
    loss_ref[...] = (0.5 / D_MODEL) * jnp.sum(tot[:, n_gain:n_gain + HEAD], axis=1, keepdims=True) * jnp.ones((1, HEAD), F32)


def _small_update(gath, w, m, v):
    n_gain = w.shape[1]
    vm = pl.BlockSpec(memory_space=pltpu.VMEM)
    return _pcall(
        functools.partial(_small_update_body, n_gain=n_gain), name="gain_update",
        in_specs=[vm] * 4, out_specs=[vm] * 5,
        out_shape=[jax.ShapeDtypeStruct((1, n_gain), F32)] * 4 + [jax.ShapeDtypeStruct((1, HEAD), F32)],
    )(gath, w, m, v)


def kernel(x, positions, norm_attn_pre, norm_attn_post, w_in, q_latent_norm, kv_latent_norm, w_uq, w_ukv, w_out, norm_mlp_pre, norm_mlp_post, w_up, w_down, loss_target, m_norm_attn_pre, m_norm_attn_post, m_w_in, m_q_latent_norm, m_kv_latent_norm, m_w_uq, m_w_ukv, m_w_out, m_norm_mlp_pre, m_norm_mlp_post, m_w_up, m_w_down, v_norm_attn_pre, v_norm_attn_post, v_w_in, v_q_latent_norm, v_kv_latent_norm, v_w_uq, v_w_ukv, v_w_out, v_norm_mlp_pre, v_norm_mlp_post, v_w_up, v_w_down):
    T = x.shape[1]
    c_arr = lax.axis_index("c").astype(jnp.int32).reshape(1)
    me_arr = (2 * lax.axis_index("x") + lax.axis_index("y")).astype(jnp.int32).reshape(1)
    names = ["w_in", "w_uq", "w_ukv", "w_out", "w_up", "w_down"]

    mats = [w_in[0], w_uq[0], w_ukv[0], w_out[0], w_up[0], w_down[0]]
    in_send, in_recv, in_bufs, in_started = _ag_start([_cast_place(me_arr, mats[0], "cast_w_in")], None, "in")
    placed = [_cast_place(me_arr, w, "cast_" + n) for w, n in zip(mats[1:], names[1:])]
    att_send, att_recv, att_bufs, att_started = _ag_start(placed[:3], in_started, "attn")
    mlp_send, mlp_recv, mlp_bufs, started = _ag_start(placed[3:], att_started, "mlp")

    col_major = lambda g: jnp.transpose(g, (1, 0, 2)).reshape(g.shape[1], N_CHIPS * g.shape[2])
    cast = lambda a: a.astype(MXU_DTYPE)
    to_shards = lambda g: jnp.transpose(g.reshape(g.shape[0], N_CHIPS, g.shape[1] // N_CHIPS), (1, 0, 2))
    halved = lambda g: g.reshape(N_CHIPS, 2, g.shape[1] // 2, g.shape[2])

    def pair_sum(full4, ns):
        from_sib = _pair_send(full4, "_".join(ns))
        return [_pair_add(c_arr, g4, r, "pair_add_" + n) for g4, r, n in zip(full4, from_sib, ns)]

    def in_weights(after):
        (win_g,) = _ag_forward(_ag_wait(in_send, in_recv, in_bufs, after, "in"), "in")
        win_full = col_major(win_g)
        w_kr = jnp.pad(win_full[:, MAIN_COLS:], ((0, 0), (0, HEAD - ROPE_B)))
        return cast(win_full[:, :MAIN_COLS]), cast(w_kr)

    def attn_weights(after):
        wuq_g, wukv_g, wout_g = _ag_forward(_ag_wait(att_send, att_recv, att_bufs, after, "attn"), "attn")
        wuq_full = col_major(wuq_g).reshape(LORA, NH, HEAD + ROPE_B)
        w_uq_p = jnp.pad(wuq_full, ((0, 0), (0, 0), (0, QPAD - HEAD - ROPE_B))).reshape(LORA, NH * QPAD)
        w_ukv_p = col_major(wukv_g).reshape(LORA, NH, 2, HEAD).transpose(0, 2, 1, 3).reshape(LORA, 2 * A_W)
        return cast(w_uq_p), cast(w_ukv_p), cast(wout_g.reshape(2 * A_W, D_MODEL))

    in_flight = {}

    def mlp_prefetch(after):
        in_flight["fw"] = _fw_start(_ag_wait(mlp_send, mlp_recv, mlp_bufs, after, "mlp"), "mlp")

    def mlp_weights(after):
        f_send, f_recv, bufs, _ = in_flight["fw"]
        wup_g, wdown_g = _fw_wait(f_send, f_recv, bufs, after, "mlp")
        return cast(col_major(wup_g)), cast(wdown_g.reshape(D_FF, D_MODEL))

    def mlp_grads_ready(gw_up, gw_down):
        parts = pair_sum([halved(to_shards(gw_up)), halved(gw_down.reshape(N_CHIPS, D_MODEL, D_MODEL))], names[4:])
        in_flight["mlp"] = _scatter_start(parts, started, "mlp")
        return in_flight["mlp"][-1][0:1, 0:1]

    def attn_grads_ready(gw_out, gw_uq_p, gw_ukv_p):
        gw_uq = to_shards(gw_uq_p.reshape(LORA, NH, QPAD)[:, :, :HEAD + ROPE_B].reshape(LORA, NH * (HEAD + ROPE_B)))
        gw_ukv = to_shards(gw_ukv_p.reshape(LORA, 2, NH, HEAD).transpose(0, 2, 1, 3).reshape(LORA, 2 * A_W))
        parts = pair_sum([halved(g) for g in (gw_uq, gw_ukv, gw_out.reshape(N_CHIPS, LORA, D_MODEL))], names[1:4])
        in_flight["attn"] = _scatter_start(parts, in_flight["mlp"][-1], "attn")
        return in_flight["attn"][-1][0:1, 0:1]

    dx, (gw_main, gw_kr), small = _local_step(
        x[0], positions[0].astype(F32).reshape(T, 1), loss_target[0],
        norm_attn_pre + started[0:1, 0:1], norm_attn_post, q_latent_norm, kv_latent_norm, norm_mlp_pre, norm_mlp_post,
        in_weights, attn_weights, mlp_prefetch, mlp_weights, mlp_grads_ready, attn_grads_ready)

    ms = [m_w_in[0], m_w_uq[0], m_w_ukv[0], m_w_out[0], m_w_up[0], m_w_down[0]]
    vs = [v_w_in[0], v_w_uq[0], v_w_ukv[0], v_w_out[0], v_w_up[0], v_w_down[0]]

    def finish(parts, landed, lo, hi, tag):
        halves = [_sum4(me_arr, p, l, "chip_sum_" + n) for p, l, n in zip(parts, landed, names[lo:hi])]
        from_sib2 = _pair_swap(halves, tag)
        return [_adamw_call(c_arr, w, own, sib, m, v, "adamw_" + n)
                for w, own, sib, m, v, n in zip(mats[lo:hi], halves, from_sib2, ms[lo:hi], vs[lo:hi], names[lo:hi])]

    gw_in = to_shards(jnp.concatenate([gw_main, gw_kr[:, :ROPE_B]], axis=1))
    i_send, i_recv, parts_in, lands_in, in_going = _scatter_start(pair_sum([halved(gw_in)], names[:1]), None, "in")
    a_send, a_recv, parts_att, lands_att, _ = in_flight["attn"]
    parts_att, landed_att = _scatter_wait(a_send, a_recv, parts_att, lands_att, in_going, "attn")
    m_send, m_recv, parts_mlp, lands_mlp, _ = in_flight["mlp"]
    parts_mlp, landed_mlp = _scatter_wait(m_send, m_recv, parts_mlp, lands_mlp, landed_att[0], "mlp")
    upd_rest = finish(parts_att + parts_mlp, landed_att + landed_mlp, 1, 6, "rest")
    parts_in, landed_in = _scatter_wait(i_send, i_recv, parts_in, lands_in, upd_rest[-1][0], "in")
    upd = finish(parts_in, landed_in, 0, 1, "in") + upd_rest
    grads = [u[0] for u in upd]

    gath = _small_gather(small)
    gains = [norm_attn_pre, norm_attn_post, q_latent_norm, kv_latent_norm, norm_mlp_pre, norm_mlp_post]
    gm = [m_norm_attn_pre, m_norm_attn_post, m_q_latent_norm, m_kv_latent_norm, m_norm_mlp_pre, m_norm_mlp_post]
    gv = [v_norm_attn_pre, v_norm_attn_post, v_q_latent_norm, v_kv_latent_norm, v_norm_mlp_pre, v_norm_mlp_post]
    cat = lambda xs: jnp.concatenate(xs, axis=1)
    g_s, d_s, m_s, v_s, loss_v = _small_update(gath, cat(gains), cat(gm), cat(gv))
    widths = [a.shape[1] for a in gains]
    offs = [sum(widths[:i]) for i in range(len(widths))]
    split = lambda a: [a[:, o:o + w] for o, w in zip(offs, widths)]
    g_gain, d_gain, m_gain, v_gain = split(g_s), split(d_s), split(m_s), split(v_s)

    def ordered(gain_list, mat_list):
        gl, ml = gain_list, [a[None] for a in mat_list]
        return [gl[0], gl[1], ml[0], gl[2], gl[3], ml[1], ml[2], ml[3], gl[4], gl[5], ml[4], ml[5]]

    loss = loss_v[0, 0]
    return (loss, dx[None],
            *ordered(g_gain, grads),
            *ordered(d_gain, [u[1] for u in upd]),
            *ordered(m_gain, [u[2] for u in upd]),
            *ordered(v_gain, [u[3] for u in upd]))
```

```python
import functools

import jax
import jax.numpy as jnp
from jax import lax
from jax.experimental import pallas as pl
from jax.experimental.pallas import tpu as pltpu

F32 = jnp.float32
BF16 = jnp.bfloat16
MXU_DTYPE = jnp.bfloat16
WIRE_DTYPE = jnp.bfloat16

D_MODEL = 2048
HEAD = 128
NH = 8
A_W = NH * HEAD
LORA = 512
ROPE_B = 64
QPAD = 256
MAIN_COLS = 3 * A_W + 2 * LORA
IN_COLS = MAIN_COLS + ROPE_B
PROJ_COLS = MAIN_COLS + HEAD
PROJ_TILE = PROJ_COLS // 3
D_FF = 4 * D_MODEL
DIL = (1, 4, 16)
ROT_A = 32
ROPE_THETA = 500000.0
EPS = 1e-6
NEG = -1e30
N_CHIPS = 4
N_DEV = 8

ADAM_LR = 0.001
ADAM_B1 = 0.9
ADAM_B2 = 0.999
ADAM_EPS = 1e-08
ADAM_WD = 0.01
ADAM_STEP = 10

MESH = pl.DeviceIdType.MESH
ANY = pl.BlockSpec(memory_space=pl.ANY)


def _pcall(body, **kw):
    return pl.pallas_call(body, **kw)


_DIMS = {
    "nn": (((1,), (0,)), ((), ())),
    "nt": (((1,), (1,)), ((), ())),
    "tn": (((0,), (0,)), ((), ())),
}


def _mm_body(*refs, dims, nk, epi, n_extra, n_out):
    a_ref, b_ref = refs[0], refs[1]
    extra = refs[2:2 + n_extra]
    outs = refs[2 + n_extra:2 + n_extra + n_out]
    part = lax.dot_general(a_ref[...], b_ref[...], _DIMS[dims], preferred_element_type=F32)

    def finish(acc):
        res = epi(acc, *[r[...] for r in extra]) if epi is not None else (acc,)
        for o_ref, o in zip(outs, res):
            o_ref[...] = o.astype(o_ref.dtype)

    if nk == 1:
        finish(part)
        return
    acc_ref = refs[-1]
    k = pl.program_id(2)

    @pl.when(k == 0)
    def _():
        acc_ref[...] = part

    @pl.when(k > 0)
    def _():
        acc_ref[...] += part

    @pl.when(k == nk - 1)
    def _():
        finish(acc_ref[...])


def _matmul(a, b, *, dims, out_dtypes, tm, tn, tk, name, epi=None, extras=(), row_extras=(), b_outer=False):
    if dims == "nn":
        (M, K), (K2, N) = a.shape, b.shape
    elif dims == "nt":
        (M, K), (N, K2) = a.shape, b.shape
    else:
        (K, M), (K2, N) = a.shape, b.shape
    assert K == K2, (a.shape, b.shape, dims)
    tm, tn, tk = min(tm, M), min(tn, N), min(tk, K)
    assert M % tm == 0 and N % tn == 0 and K % tk == 0, (name, M, N, K, tm, tn, tk)
    nk = K // tk

    def at(f):
        if b_outer:
            return lambda j, i, k: f(i, j, k)
        return f

    a_spec = {"nn": pl.BlockSpec((tm, tk), at(lambda i, j, k: (i, k))),
              "nt": pl.BlockSpec((tm, tk), at(lambda i, j, k: (i, k))),
              "tn": pl.BlockSpec((tk, tm), at(lambda i, j, k: (k, i)))}[dims]
    b_spec = {"nn": pl.BlockSpec((tk, tn), at(lambda i, j, k: (k, j))),
              "nt": pl.BlockSpec((tn, tk), at(lambda i, j, k: (j, k))),
              "tn": pl.BlockSpec((tk, tn), at(lambda i, j, k: (k, j)))}[dims]
    o_spec = pl.BlockSpec((tm, tn), at(lambda i, j, k: (i, j)))
    r_specs = [pl.BlockSpec((tm, r.shape[1]), at(lambda i, j, k: (i, 0))) for r in row_extras]
    body = functools.partial(_mm_body, dims=dims, nk=nk, epi=epi,
                             n_extra=len(extras) + len(row_extras), n_out=len(out_dtypes))
    res = _pcall(
        body, name=name,
        grid=(N // tn, M // tm, nk) if b_outer else (M // tm, N // tn, nk),
        in_specs=[a_spec, b_spec] + [o_spec] * len(extras) + r_specs,
        out_specs=[o_spec] * len(out_dtypes),
        out_shape=[jax.ShapeDtypeStruct((M, N), dt) for dt in out_dtypes],
        scratch_shapes=[pltpu.VMEM((tm, tn), F32)] if nk > 1 else [],
        compiler_params=pltpu.CompilerParams(
            dimension_semantics=("parallel", "parallel", "arbitrary")),
    )(a, b, *extras, *row_extras)
    return list(res)


def _rowwise(body, row_ins, vec_ins, row_outs, acc_outs, *, tr, name):
    T = row_ins[0].shape[0]
    tr = min(tr, T)
    assert T % tr == 0
    in_specs = [pl.BlockSpec((tr, a.shape[1]), lambda i: (i, 0)) for a in row_ins]
    in_specs += [pl.BlockSpec(a.shape, lambda i: (0, 0)) for a in vec_ins]
    out_specs = [pl.BlockSpec((tr, w), lambda i: (i, 0)) for (w, _) in row_outs]
    out_specs += [pl.BlockSpec(s, lambda i: (0, 0)) for s in acc_outs]
    out_shape = [jax.ShapeDtypeStruct((T, w), dt) for (w, dt) in row_outs]
    out_shape += [jax.ShapeDtypeStruct(s, F32) for s in acc_outs]
    sem = "arbitrary" if acc_outs else "parallel"
    return list(_pcall(
        body, name=name, grid=(T // tr,), in_specs=in_specs, out_specs=out_specs,
        out_shape=out_shape,
        compiler_params=pltpu.CompilerParams(dimension_semantics=(sem,)),
    )(*row_ins, *vec_ins))


def _rstd(x):
    return lax.rsqrt(jnp.mean(x * x, axis=-1, keepdims=True) + EPS)


def _rms_bwd(x, rstd, dyg):
    xh = x * rstd
    return rstd * (dyg - xh * jnp.mean(dyg * xh, axis=-1, keepdims=True)), xh


def _fold8(v):
    r, w = v.shape
    return jnp.sum(v.reshape(r // 8, 8, w), axis=0)


def _acc(ref, val):
    first = pl.program_id(0) == 0

    @pl.when(first)
    def _():
        ref[...] = val

    @pl.when(jnp.logical_not(first))
    def _():
        ref[...] += val


def _rope(x, c, sa, sb, half):
    return x * c + pltpu.roll(x, HEAD - half, 1) * sa + pltpu.roll(x, half, 1) * sb


def _rope_t(dy, c, sa, sb, half):
    return dy * c - pltpu.roll(dy, HEAD - half, 1) * sa - pltpu.roll(dy, half, 1) * sb


def _rope_tab_body(pos_ref, inv_ref, ca, saa, sab, cb, sba, sbb):
    pos = pos_ref[...]
    lane = lax.broadcasted_iota(jnp.int32, (pos.shape[0], HEAD), 1)
    ang_a = pos * inv_ref[0:1, :]
    ang_b = pos * inv_ref[1:2, :]
    c, s = jnp.cos(ang_a), jnp.sin(ang_a)
    ha = ROT_A // 2
    ca[...] = jnp.where(lane < ROT_A, c, 1.0)
    saa[...] = jnp.where(lane < ha, -s, 0.0)
    sab[...] = jnp.where((lane >= ha) & (lane < ROT_A), s, 0.0)
    c, s = jnp.cos(ang_b), jnp.sin(ang_b)
    hb = ROPE_B // 2
    cb[...] = jnp.where(lane < ROPE_B, c, 1.0)
    sba[...] = jnp.where(lane < hb, -s, 0.0)
    sbb[...] = jnp.where((lane >= hb) & (lane < ROPE_B), s, 0.0)


def _rms_fwd_body(x_ref, g_ref, h_ref):
    x = x_ref[...]
    h_ref[...] = ((x * _rstd(x)) * g_ref[...]).astype(h_ref.dtype)


def _postproj_body(p_ref, ca, saa, sab, cb, sba, sbb, gq_ref, gkv_ref,
                   q_ref, k_ref, v_ref, cqn_ref, ckvn_ref, krope_ref):
    c, sa, sb = ca[...], saa[...], sab[...]
    for h in range(NH):
        lo = h * HEAD
        q_ref[:, lo:lo + HEAD] = _rope(p_ref[:, lo:lo + HEAD], c, sa, sb, ROT_A // 2).astype(q_ref.dtype)
        k_ref[:, lo:lo + HEAD] = _rope(p_ref[:, A_W + lo:A_W + lo + HEAD], c, sa, sb, ROT_A // 2).astype(k_ref.dtype)
    v_ref[...] = p_ref[:, 2 * A_W:3 * A_W].astype(v_ref.dtype)
    cq = p_ref[:, 3 * A_W:3 * A_W + LORA]
    cqn_ref[...] = ((cq * _rstd(cq)) * gq_ref[...]).astype(cqn_ref.dtype)
    ckv = p_ref[:, 3 * A_W + LORA:MAIN_COLS]
    ckvn_ref[...] = ((ckv * _rstd(ckv)) * gkv_ref[...]).astype(ckvn_ref.dtype)
    krope_ref[...] = _rope(p_ref[:, MAIN_COLS:PROJ_COLS], cb[...], sba[...], sbb[...], ROPE_B // 2).astype(krope_ref.dtype)


def _mid_body(x_ref, o_ref, g2_ref, g3_ref, x1_ref, h2_ref):
    o = o_ref[...]
    x1 = x_ref[...] + (o * _rstd(o)) * g2_ref[...]
    x1_ref[...] = x1
    h2_ref[...] = ((x1 * _rstd(x1)) * g3_ref[...]).astype(h2_ref.dtype)


def _loss_body(x1_ref, d_ref, t_ref, g4_ref, dy_ref, dd_ref, loss_ref, dg4_ref):
    d = d_ref[...]
    rstd = _rstd(d)
    y = x1_ref[...] + (d * rstd) * g4_ref[...]
    e = y - t_ref[...]
    dy = e * (1.0 / D_MODEL)
    dy_ref[...] = dy
    dd, dh = _rms_bwd(d, rstd, dy * g4_ref[...])
    dd_ref[...] = dd.astype(dd_ref.dtype)
    _acc(dg4_ref, _fold8(dy * dh))
    e8 = _fold8(e * e)
    l = e8[:, 0:HEAD]
    for j in range(1, D_MODEL // HEAD):
        l = l + e8[:, j * HEAD:(j + 1) * HEAD]
    _acc(loss_ref, l)


def _bmid_body(dy_ref, dh2_ref, x1_ref, o_ref, g2_ref, g3_ref, dx1_ref, do_ref, dg3_ref, dg2_ref):
    x1 = x1_ref[...]
    dh2 = dh2_ref[...]
    dn, x1h = _rms_bwd(x1, _rstd(x1), dh2 * g3_ref[...])
    dx1 = dy_ref[...] + dn
    dx1_ref[...] = dx1
    _acc(dg3_ref, _fold8(dh2 * x1h))
    o = o_ref[...]
    do, oh = _rms_bwd(o, _rstd(o), dx1 * g2_ref[...])
    do_ref[...] = do.astype(do_ref.dtype)
    _acc(dg2_ref, _fold8(dx1 * oh))


def _dproj_body(dq_ref, dk_ref, dv_ref, dcq_ref, dckv_ref, p_ref, dkr_ref,
                ca, saa, sab, cb, sba, sbb, gq_ref, gkv_ref,
                dp_ref, dgq_ref, dgkv_ref):
    c, sa, sb = ca[...], saa[...], sab[...]
    for h in range(NH):
        lo = h * HEAD
        dp_ref[:, lo:lo + HEAD] = _rope_t(dq_ref[:, lo:lo + HEAD], c, sa, sb, ROT_A // 2).astype(dp_ref.dtype)
        dp_ref[:, A_W + lo:A_W + lo + HEAD] = _rope_t(dk_ref[:, lo:lo + HEAD], c, sa, sb, ROT_A // 2).astype(dp_ref.dtype)
    dp_ref[:, 2 * A_W:3 * A_W] = dv_ref[...].astype(dp_ref.dtype)
    cq = p_ref[:, 3 * A_W:3 * A_W + LORA]
    dcqn = dcq_ref[...]
    dcq, cqh = _rms_bwd(cq, _rstd(cq), dcqn * gq_ref[...])
    dp_ref[:, 3 * A_W:3 * A_W + LORA] = dcq.astype(dp_ref.dtype)
    _acc(dgq_ref, _fold8(dcqn * cqh))
    ckv = p_ref[:, 3 * A_W + LORA:MAIN_COLS]
    dckvn = dckv_ref[...]
    dckv, ckvh = _rms_bwd(ckv, _rstd(ckv), dckvn * gkv_ref[...])
    dp_ref[:, 3 * A_W + LORA:MAIN_COLS] = dckv.astype(dp_ref.dtype)
    _acc(dgkv_ref, _fold8(dckvn * ckvh))
    dkr = dkr_ref[:, 0:HEAD]
    for h in range(1, NH):
        dkr = dkr + dkr_ref[:, h * HEAD:(h + 1) * HEAD]
    dp_ref[:, MAIN_COLS:PROJ_COLS] = _rope_t(dkr, cb[...], sba[...], sbb[...], ROPE_B // 2).astype(dp_ref.dtype)


def _bin_body(dx1_ref, dh_ref, x_ref, g1_ref, dx_ref, dg1_ref):
    x = x_ref[...]
    dh = dh_ref[...]
    dn, xh = _rms_bwd(x, _rstd(x), dh * g1_ref[...])
    dx_ref[...] = dx1_ref[...] + dn
    _acc(dg1_ref, _fold8(dh * xh))


def _dot_nt(a, b):
    return lax.dot_general(a, b, _DIMS["nt"], preferred_element_type=F32)


def _dot_tn(a, b):
    return lax.dot_general(a, b, _DIMS["tn"], preferred_element_type=F32)


def _dot_nn(a, b):
    return jnp.dot(a, b, preferred_element_type=F32)


DIL_SCALE = HEAD ** -0.5
DIL_CHUNK = 256


def _dil_rows(t, d):
    r = t & (d - 1)
    n = t >> (d.bit_length() - 1)
    start = r + n * (HEAD * d)
    has_prev = n > 0
    pstart = jnp.where(has_prev, start - HEAD * d, start)
    if d == 1:
        return pl.ds(pl.multiple_of(start, HEAD), HEAD), pl.ds(pl.multiple_of(pstart, HEAD), HEAD), has_prev
    return pl.ds(start, HEAD, stride=d), pl.ds(pstart, HEAD, stride=d), has_prev


def _dil_band():
    row = lax.broadcasted_iota(jnp.int32, (HEAD, 2 * HEAD), 0)
    col = lax.broadcasted_iota(jnp.int32, (HEAD, 2 * HEAD), 1)
    return (col >= row) & (col <= row + HEAD), col >= HEAD


def _dil_fwd_body(q_ref, k_ref, v_ref, a_ref, lse_ref, o1, o2, o3, l1, l2, l3, *, nt, unroll):
    band, is_cur = _dil_band()
    for d, o_sc, l_sc in zip(DIL, (o1, o2, o3), (l1, l2, l3)):

        def tile(t, carry, d=d, o_sc=o_sc, l_sc=l_sc):
            rows, prows, has_prev = _dil_rows(t, d)
            q = q_ref[rows, :].astype(MXU_DTYPE)
            kk = jnp.concatenate([k_ref[prows, :], k_ref[rows, :]], axis=0).astype(MXU_DTYPE)
            vv = jnp.concatenate([v_ref[prows, :], v_ref[rows, :]], axis=0).astype(MXU_DTYPE)
            ok = band & (is_cur | has_prev)
            s = jnp.where(ok, _dot_nt(q, kk) * DIL_SCALE, NEG)
            m = jnp.max(s, axis=1, keepdims=True)
            p = jnp.exp(s - m)
            den = jnp.sum(p, axis=1, keepdims=True)
            o_sc[rows, :] = _dot_nn((p / den).astype(MXU_DTYPE), vv)
            l_sc[rows, :] = jnp.broadcast_to(m + jnp.log(den), (HEAD, HEAD))
            return carry

        lax.fori_loop(0, nt, tile, 0, unroll=unroll)

    def merge(i, carry):
        rs = pl.ds(pl.multiple_of(i * DIL_CHUNK, DIL_CHUNK), DIL_CHUNK)
        la, lb, lc = l1[rs, :], l2[rs, :], l3[rs, :]
        m = jnp.maximum(jnp.maximum(la, lb), lc)
        wa, wb, wc = jnp.exp(la - m), jnp.exp(lb - m), jnp.exp(lc - m)
        den = wa + wb + wc
        a = (wa / den) * o1[rs, :] + (wb / den) * o2[rs, :] + (wc / den) * o3[rs, :]
        a_ref[rs, :] = a.astype(a_ref.dtype)
        lse_ref[rs, :] = m + jnp.log(den)
        return carry

    lax.fori_loop(0, q_ref.shape[0] // DIL_CHUNK, merge, 0)


def _dil_fwd(q, k, v):
    T = q.shape[0]
    spec = pl.BlockSpec((T, HEAD), lambda h: (0, h))
    return _pcall(
        functools.partial(_dil_fwd_body, nt=T // HEAD, unroll=4), name="dil_fwd",
        grid=(NH,), in_specs=[spec] * 3, out_specs=[spec] * 2,
        out_shape=[jax.ShapeDtypeStruct((T, 2 * A_W), MXU_DTYPE), jax.ShapeDtypeStruct((T, A_W), F32)],
        scratch_shapes=[pltpu.VMEM((T, HEAD), F32)] * 6,
        compiler_params=pltpu.CompilerParams(dimension_semantics=("parallel",)),
    )(q, k, v)


def _dil_bwd_body(q_ref, k_ref, v_ref, do_ref, a_ref, lse_ref, dq_ref, dk_ref, dv_ref, dl_sc, *, nt, unroll):
    band, is_cur = _dil_band()

    def prep(i, carry):
        rs = pl.ds(pl.multiple_of(i * DIL_CHUNK, DIL_CHUNK), DIL_CHUNK)
        dl = jnp.sum(do_ref[rs, :] * a_ref[rs, :].astype(F32), axis=1, keepdims=True)
        dl_sc[rs, :] = jnp.broadcast_to(dl, (DIL_CHUNK, HEAD))
        zero = jnp.zeros((DIL_CHUNK, HEAD), F32)
        dq_ref[rs, :] = zero
        dk_ref[rs, :] = zero
        dv_ref[rs, :] = zero
        return carry

    lax.fori_loop(0, q_ref.shape[0] // DIL_CHUNK, prep, 0)

    for d in DIL:

        def tile(t, carry, d=d):
            rows, prows, has_prev = _dil_rows(t, d)
            q = q_ref[rows, :].astype(MXU_DTYPE)
            kk = jnp.concatenate([k_ref[prows, :], k_ref[rows, :]], axis=0).astype(MXU_DTYPE)
            vv = jnp.concatenate([v_ref[prows, :], v_ref[rows, :]], axis=0).astype(MXU_DTYPE)
            do = do_ref[rows, :].astype(MXU_DTYPE)
            lse = lse_ref[rows, :]
            dl = dl_sc[rows, :]
            ok = band & (is_cur | has_prev)
            s = _dot_nt(q, kk) * DIL_SCALE
            p = jnp.where(ok, jnp.exp(s - jnp.concatenate([lse, lse], axis=1)), 0.0)
            ds = (p * (_dot_nt(do, vv) - jnp.concatenate([dl, dl], axis=1))).astype(MXU_DTYPE)
            dq_ref[rows, :] += _dot_nn(ds, kk) * DIL_SCALE
            dkk = _dot_tn(ds, q) * DIL_SCALE
            dvv = _dot_tn(p.astype(MXU_DTYPE), do)
            dk_ref[rows, :] += dkk[HEAD:, :]
            dv_ref[rows, :] += dvv[HEAD:, :]
            dk_ref[prows, :] += dkk[:HEAD, :]
            dv_ref[prows, :] += dvv[:HEAD, :]
            return carry

        lax.fori_loop(0, nt, tile, 0, unroll=unroll)


def _dil_bwd(q, k, v, dmix, mixed, lse):
    T = q.shape[0]
    spec = pl.BlockSpec((T, HEAD), lambda h: (0, h))
    return _pcall(
        functools.partial(_dil_bwd_body, nt=T // HEAD, unroll=2), name="dil_bwd",
        grid=(NH,), in_specs=[spec] * 6, out_specs=[spec] * 3,
        out_shape=[jax.ShapeDtypeStruct((T, A_W), F32)] * 3,
        scratch_shapes=[pltpu.VMEM((T, HEAD), F32)],
        compiler_params=pltpu.CompilerParams(dimension_semantics=("parallel",)),
    )(q, k, v, dmix, mixed, lse)


MLA_SCALE = (HEAD + ROPE_B) ** -0.5
MLA_T = 512
MLA_HP = 2


def _tri(t):
    row = lax.broadcasted_iota(jnp.int32, (t, t), 0)
    col = lax.broadcasted_iota(jnp.int32, (t, t), 1)
    return col <= row


def _lanes(x, n):
    return jnp.tile(x, (1, n // HEAD))


def _mla_fwd_body(q_ref, kn_ref, kr_ref, v_ref, mixed_ref, o_ref, lse_ref, m_sc, l_sc, acc_sc, *, t, hp):
    del mixed_ref
    qi = pl.program_id(1)
    m_sc[...] = jnp.full(m_sc.shape, NEG, F32)
    l_sc[...] = jnp.zeros(l_sc.shape, F32)
    acc_sc[...] = jnp.zeros(acc_sc.shape, F32)

    def step(j, masked):
        ks = pl.ds(pl.multiple_of(j * t, t), t)
        kr = kr_ref[ks, :]
        for hh in range(hp):
            kcat = jnp.concatenate([kn_ref[ks, hh * HEAD:(hh + 1) * HEAD], kr], axis=1)
            s = _dot_nt(q_ref[:, hh * QPAD:(hh + 1) * QPAD], kcat) * MLA_SCALE
            if masked:
                s = jnp.where(_tri(t), s, NEG)
            m_prev = m_sc[hh]
            m_new = jnp.maximum(m_prev, jnp.max(s, axis=1, keepdims=True))
            alpha = jnp.exp(m_prev - m_new)
            p = jnp.exp(s - _lanes(m_new, t))
            l_sc[hh] = alpha * l_sc[hh] + jnp.sum(p, axis=1, keepdims=True)
            acc_sc[hh] = alpha * acc_sc[hh] + _dot_nn(p.astype(MXU_DTYPE), v_ref[ks, hh * HEAD:(hh + 1) * HEAD])
            m_sc[hh] = m_new

    def off_diag(j, carry):
        step(j, False)
        return carry

    lax.fori_loop(0, qi, off_diag, 0)
    step(qi, True)
    for hh in range(hp):
        l = l_sc[hh]
        o_ref[:, hh * HEAD:(hh + 1) * HEAD] = (acc_sc[hh] / l).astype(o_ref.dtype)
        lse_ref[:, hh * HEAD:(hh + 1) * HEAD] = m_sc[hh] + jnp.log(l)


def _mla_fwd(qf, kv, kr, mixed):
    T = qf.shape[0]
    t, hp = min(MLA_T, T), MLA_HP
    ng = NH // hp
    return _pcall(
        functools.partial(_mla_fwd_body, t=t, hp=hp), name="mla_fwd",
        grid=(ng, T // t),
        in_specs=[pl.BlockSpec((t, hp * QPAD), lambda g, i: (i, g)),
                  pl.BlockSpec((T, hp * HEAD), lambda g, i: (0, g)),
                  pl.BlockSpec((T, HEAD), lambda g, i: (0, 0)),
                  pl.BlockSpec((T, hp * HEAD), lambda g, i: (0, ng + g)), ANY],
        out_specs=[pl.BlockSpec((t, hp * HEAD), lambda g, i: (i, ng + g)),
                   pl.BlockSpec((t, hp * HEAD), lambda g, i: (i, g))],
        out_shape=[jax.ShapeDtypeStruct(mixed.shape, mixed.dtype), jax.ShapeDtypeStruct((T, A_W), F32)],
        input_output_aliases={4: 0},
        scratch_shapes=[pltpu.VMEM((hp, t, HEAD), F32)] * 3,
        compiler_params=pltpu.CompilerParams(dimension_semantics=("parallel", "parallel")),
    )(qf, kv, kr, kv, mixed)


def _mla_bwd_body(q_ref, kn_ref, kr_ref, v_ref, do_ref, o_ref, lse_ref, cb, sba, sbb,
                  dq_ref, dkn_ref, dv_ref, dkr_ref, dq_sc, dl_sc, dk_sc, dv_sc, *, t):
    ki = pl.program_id(1)
    nq = q_ref.shape[0] // t

    @pl.when(ki == 0)
    def _():
        def prep(i, carry):
            rs = pl.ds(pl.multiple_of(i * t, t), t)
            dl = jnp.sum(do_ref[rs, :] * o_ref[rs, :].astype(F32), axis=1, keepdims=True)
            dl_sc[rs, :] = jnp.broadcast_to(dl, (t, HEAD))
            dq_sc[rs, :] = jnp.zeros((t, QPAD), F32)
            return carry
        lax.fori_loop(0, nq, prep, 0)

    kcat = jnp.concatenate([kn_ref[...], kr_ref[...]], axis=1)
    v = v_ref[...]
    dk_sc[...] = jnp.zeros(dk_sc.shape, F32)
    dv_sc[...] = jnp.zeros(dv_sc.shape, F32)

    def step(i, masked):
        qs = pl.ds(pl.multiple_of(i * t, t), t)
        q = q_ref[qs, :]
        do = do_ref[qs, :].astype(MXU_DTYPE)
        p = jnp.exp(_dot_nt(q, kcat) * MLA_SCALE - _lanes(lse_ref[qs, :], t))
        if masked:
            p = jnp.where(_tri(t), p, 0.0)
        ds = (p * (_dot_nt(do, v) - _lanes(dl_sc[qs, :], t))).astype(MXU_DTYPE)
        dv_sc[...] += _dot_tn(p.astype(MXU_DTYPE), do)
        dk_sc[...] += _dot_tn(ds, q)
        dq_sc[qs, :] += _dot_nn(ds, kcat) * MLA_SCALE

    step(ki, True)

    def off_diag(i, carry):
        step(i, False)
        return carry

    lax.fori_loop(ki + 1, nq, off_diag, 0)
    dk = dk_sc[...] * MLA_SCALE
    dkn_ref[...] = dk[:, 0:HEAD].astype(dkn_ref.dtype)
    dkr_ref[...] = dk[:, HEAD:QPAD]
    dv_ref[...] = dv_sc[...].astype(dv_ref.dtype)

    @pl.when(ki == nq - 1)
    def _():
        def emit(i, carry):
            rs = pl.ds(pl.multiple_of(i * t, t), t)
            dq_ref[rs, 0:HEAD] = dq_sc[rs, 0:HEAD].astype(dq_ref.dtype)
            dq_ref[rs, HEAD:QPAD] = _rope_t(dq_sc[rs, HEAD:QPAD], cb[rs, :], sba[rs, :], sbb[rs, :],
                                            ROPE_B // 2).astype(dq_ref.dtype)
            return carry
        lax.fori_loop(0, nq, emit, 0)


def _mla_bwd(qf, kv, kr, dmix, mixed, lse, tabs_b):
    T = qf.shape[0]
    t = min(MLA_T, T)
    head = lambda h, j: (0, h)
    b_half = lambda h, j: (0, NH + h)
    kblk = pl.BlockSpec((t, HEAD), lambda h, j: (j, h))
    return _pcall(
        functools.partial(_mla_bwd_body, t=t), name="mla_bwd",
        grid=(NH, T // t),
        in_specs=[pl.BlockSpec((T, QPAD), head), kblk,
                  pl.BlockSpec((t, HEAD), lambda h, j: (j, 0)),
                  pl.BlockSpec((t, HEAD), lambda h, j: (j, NH + h)),
                  pl.BlockSpec((T, HEAD), b_half), pl.BlockSpec((T, HEAD), b_half),
                  pl.BlockSpec((T, HEAD), head)] + [pl.BlockSpec((T, HEAD), lambda h, j: (0, 0))] * 3,
        out_specs=[pl.BlockSpec((T, QPAD), head), kblk, kblk, kblk],
        out_shape=[jax.ShapeDtypeStruct((T, NH * QPAD), MXU_DTYPE), jax.ShapeDtypeStruct((T, A_W), MXU_DTYPE),
                   jax.ShapeDtypeStruct((T, A_W), MXU_DTYPE), jax.ShapeDtypeStruct((T, A_W), F32)],
        scratch_shapes=[pltpu.VMEM((T, QPAD), F32), pltpu.VMEM((T, HEAD), F32), pltpu.VMEM((t, QPAD), F32),
                        pltpu.VMEM((t, HEAD), F32)],
        compiler_params=pltpu.CompilerParams(dimension_semantics=("parallel", "arbitrary")),
    )(qf, kv, kr, kv, dmix, mixed, lse, *tabs_b)


def _local_step(x, pos, target, g1, g2, gq, gkv, g3, g4,
                in_weights, attn_weights, mlp_prefetch, mlp_weights, mlp_grads_ready, attn_grads_ready):
    T = x.shape[0]
    TR = 256
    mm = functools.partial(_matmul, tm=1024, tn=1024, tk=2048, b_outer=True)
    mm_k = functools.partial(_matmul, tm=1024, tn=1024, tk=2048)
    mm_g = functools.partial(_matmul, tm=512, tn=1024, tk=4096, b_outer=True)

    inv_a = ROPE_THETA ** (-jnp.arange(0, ROT_A, 2, dtype=F32) / ROT_A)
    inv_b = ROPE_THETA ** (-jnp.arange(0, ROPE_B, 2, dtype=F32) / ROPE_B)
    inv = jnp.stack([jnp.concatenate([inv_a, inv_a, jnp.zeros((HEAD - ROT_A,), F32)]),
                     jnp.concatenate([inv_b, inv_b, jnp.zeros((HEAD - ROPE_B,), F32)])])
    inv = jnp.concatenate([inv, jnp.zeros((6, HEAD), F32)], axis=0)
    tabs = _rowwise(_rope_tab_body, [pos], [inv], [(HEAD, F32)] * 6, [], tr=512, name="rope_tables")

    (h,) = _rowwise(_rms_fwd_body, [x], [g1], [(D_MODEL, MXU_DTYPE)], [], tr=TR, name="rms_in")
    w_proj = in_weights(h)
    (proj,) = mm(h, w_proj, dims="nn", out_dtypes=[F32], tn=PROJ_TILE, name="proj_in")
    q, k, v, cqn, ckvn, krope = _rowwise(
        _postproj_body, [proj] + tabs, [gq, gkv],
        [(A_W, F32)] * 3 + [(LORA, MXU_DTYPE)] * 2 + [(HEAD, MXU_DTYPE)], [], tr=TR, name="post_proj")
    mixed, lse_a = _dil_fwd(q, k, v)

    w_uq_p, w_ukv_p, w_out = attn_weights(cqn)

    def q_epi(acc, cb, sba, sbb):
        cols = []
        for hh in range(acc.shape[1] // QPAD):
            lo = hh * QPAD
            cols += [acc[:, lo:lo + HEAD], _rope(acc[:, lo + HEAD:lo + QPAD], cb, sba, sbb, ROPE_B // 2)]
        return (jnp.concatenate(cols, axis=1),)
    (qf,) = mm(cqn, w_uq_p, dims="nn", out_dtypes=[MXU_DTYPE], name="q_up", epi=q_epi, row_extras=tuple(tabs[3:]))
    (kv,) = mm(ckvn, w_ukv_p, dims="nn", out_dtypes=[MXU_DTYPE], name="kv_up")
    mixed, lse_b = _mla_fwd(qf, kv, krope, mixed)
    mlp_prefetch(mixed)

    (o,) = mm(mixed, w_out, dims="nn", out_dtypes=[F32], name="out_proj")
    x1, h2 = _rowwise(_mid_body, [x, o], [g2, g3], [(D_MODEL, F32), (D_MODEL, MXU_DTYPE)], [], tr=TR, name="mid_norm")

    w_up, w_down = mlp_weights(h2)

    def up_epi(acc):
        r = jnp.maximum(acc, 0.0)
        return r * r, r
    u, r = mm(h2, w_up, dims="nn", out_dtypes=[MXU_DTYPE, MXU_DTYPE], name="mlp_up", epi=up_epi)
    (dn,) = mm_k(u, w_down, dims="nn", out_dtypes=[F32], name="mlp_down")
    dy, dd, loss8, dg4 = _rowwise(_loss_body, [x1, dn, target], [g4], [(D_MODEL, F32), (D_MODEL, MXU_DTYPE)],
                                  [(8, HEAD), (8, D_MODEL)], tr=TR, name="loss_head")

    def dup_epi(acc, rr):
        return (acc * (2.0 * rr.astype(F32)),)
    (dup,) = mm(dd, w_down, dims="nt", out_dtypes=[MXU_DTYPE], name="d_up", epi=dup_epi, extras=(r,))
    (gw_down,) = mm_g(u, dd, dims="tn", out_dtypes=[WIRE_DTYPE], name="gw_down")
    (dh2,) = mm_k(dup, w_up, dims="nt", out_dtypes=[F32], name="d_h2")
    (gw_up,) = mm_g(h2, dup, dims="tn", out_dtypes=[WIRE_DTYPE], name="gw_up")
    g2 = g2 + mlp_grads_ready(gw_up, gw_down)
    dx1, do, dg3, dg2 = _rowwise(_bmid_body, [dy, dh2, x1, o], [g2, g3], [(D_MODEL, F32), (D_MODEL, MXU_DTYPE)],
                                 [(8, D_MODEL), (8, D_MODEL)], tr=TR, name="bwd_mid")
    (dmix,) = mm(do, w_out, dims="nt", out_dtypes=[F32], name="d_mixed")
    (gw_out,) = mm_g(mixed, do, dims="tn", out_dtypes=[WIRE_DTYPE], name="gw_out")

    dq_pad, dkn, dvb, dkr = _mla_bwd(qf, kv, krope, dmix, mixed, lse_b, tabs[3:])
    (dcqn,) = mm(dq_pad, w_uq_p, dims="nt", out_dtypes=[F32], name="d_cq")
    (gw_uq_p,) = mm_g(cqn, dq_pad, dims="tn", out_dtypes=[WIRE_DTYPE], name="gw_uq")
    dkv = jnp.concatenate([dkn, dvb], axis=1)
    (dckvn,) = mm(dkv, w_ukv_p, dims="nt", out_dtypes=[F32], name="d_ckv")
    (gw_ukv_p,) = mm_g(ckvn, dkv, dims="tn", out_dtypes=[WIRE_DTYPE], name="gw_ukv")
    gq = gq + attn_grads_ready(gw_out, gw_uq_p, gw_ukv_p)

    dq_a, dk_a, dv_a = _dil_bwd(q, k, v, dmix, mixed, lse_a)
    dproj, dgq, dgkv = _rowwise(
        _dproj_body, [dq_a, dk_a, dv_a, dcqn, dckvn, proj, dkr] + tabs, [gq, gkv],
        [(PROJ_COLS, MXU_DTYPE)], [(8, LORA), (8, LORA)], tr=TR, name="d_proj")
    (dh,) = mm_k(dproj, w_proj, dims="nt", out_dtypes=[F32], tk=PROJ_TILE, name="d_h")
    (gw_proj,) = mm_g(h, dproj, dims="tn", out_dtypes=[WIRE_DTYPE], tn=PROJ_TILE, name="gw_in")
    dx, dg1 = _rowwise(_bin_body, [dx1, dh, x], [g1], [(D_MODEL, F32)], [(8, D_MODEL)], tr=TR, name="bwd_in")

    small = jnp.concatenate([dg1, dg2, dgq, dgkv, dg3, dg4, loss8], axis=1)
    return dx, gw_proj, small


def _place():
    x, y, c = lax.axis_index("x"), lax.axis_index("y"), lax.axis_index("c")
    chips = [(1 - x, y), (x, 1 - y), (1 - x, 1 - y)]
    return x, y, c, chips


def _cast_place_body(me_ref, w_ref, o_ref):
    o_ref[...] = w_ref[...].astype(o_ref.dtype)


def _cast_place(me_arr, w, name):
    rows, cols = w.shape
    tr = min(rows, 256)
    grid_spec = pltpu.PrefetchScalarGridSpec(
        num_scalar_prefetch=1, grid=(rows // tr,),
        in_specs=[pl.BlockSpec((tr, cols), lambda i, me: (i, 0))],
        out_specs=pl.BlockSpec((None, tr, cols), lambda i, me: (me[0], i, 0)))
    return _pcall(
        _cast_place_body, name=name, grid_spec=grid_spec,
        out_shape=jax.ShapeDtypeStruct((N_CHIPS, rows, cols), WIRE_DTYPE),
        compiler_params=pltpu.CompilerParams(dimension_semantics=("parallel",)),
    )(me_arr, w)


HBM = pl.BlockSpec(memory_space=pltpu.HBM)
SEM = pl.BlockSpec(memory_space=pltpu.SEMAPHORE)
EFFECT = pltpu.SideEffectType.DATAFLOW_SIDE_EFFECTING


def _in_hbm(a):
    return pltpu.with_memory_space_constraint(a, pltpu.HBM)


def _ag_descs(bufs, send_sems, recv_sems):
    x, y, c, chips = _place()
    me = 2 * x + y
    out = []
    for w, buf in enumerate(bufs):
        half = buf.shape[1] // 2
        rows = pl.ds(pl.multiple_of(c * half, 16), half)
        mine = buf.at[me, rows]
        for j, (px, py) in enumerate(chips):
            landed = buf.at[2 * px + py, rows]
            mk = lambda ref, w=w, j=j, px=px, py=py: pltpu.make_async_remote_copy(
                src_ref=ref, dst_ref=ref, send_sem=send_sems.at[w * 3 + j], recv_sem=recv_sems.at[w * 3 + j],
                device_id=(px, py, c), device_id_type=MESH)
            out.append((mk(mine), mk(landed)))
    return out


def _ag_start_body(*refs, n_w):
    bufs = refs[:n_w]
    send_sems, recv_sems = refs[-n_w - 3], refs[-n_w - 2]
    token = refs[-1]
    for send, _ in _ag_descs(bufs, send_sems, recv_sems):
        send.start()
    token[...] = jnp.zeros_like(token)


def _ag_start(placed, after, tag):
    n_w = len(placed)
    after = [] if after is None else [after]
    res = _pcall(
        functools.partial(_ag_start_body, n_w=n_w), name="weight_allgather_start_" + tag,
        in_specs=[HBM] * n_w + [ANY] * len(after),
        out_specs=[SEM, SEM] + [HBM] * n_w + [pl.BlockSpec(memory_space=pltpu.VMEM)],
        out_shape=[pltpu.SemaphoreType.DMA((3 * n_w,)), pltpu.SemaphoreType.DMA((3 * n_w,))]
        + [pltpu.HBM(p.shape, p.dtype) for p in placed] + [jax.ShapeDtypeStruct((8, HEAD), F32)],
        input_output_aliases={w: 2 + w for w in range(n_w)},
        compiler_params=pltpu.CompilerParams(has_side_effects=EFFECT),
    )(*[_in_hbm(p) for p in placed], *after)
    return res[0], res[1], list(res[2:2 + n_w]), res[-1]


def _ag_wait_body(*refs, n_w):
    bufs = refs[:n_w]
    send_sems, recv_sems = refs[n_w], refs[n_w + 1]
    for send, recv in _ag_descs(bufs, send_sems, recv_sems):
        send.wait_send()
        recv.wait_recv()


def _ag_wait(send_sems, recv_sems, bufs, after, tag):
    n_w = len(bufs)
    return list(_pcall(
        functools.partial(_ag_wait_body, n_w=n_w), name="weight_allgather_wait_" + tag,
        in_specs=[HBM] * n_w + [SEM, SEM, ANY], out_specs=[HBM] * n_w,
        out_shape=[pltpu.HBM(b.shape, b.dtype) for b in bufs],
        input_output_aliases={w: w for w in range(n_w)},
        compiler_params=pltpu.CompilerParams(has_side_effects=EFFECT),
    )(*bufs, send_sems, recv_sems, after))


def _fw_descs(bufs, send_sems, recv_sems):
    x, y, c, chips = _place()
    out = []
    for w, buf in enumerate(bufs):
        half = buf.shape[1] // 2
        for j, (px, py) in enumerate(chips):
            def mk(which, w=w, j=j, buf=buf, half=half, px=px, py=py):
                ref = buf.at[2 * px + py, pl.ds(pl.multiple_of(which * half, 16), half)]
                return pltpu.make_async_remote_copy(
                    src_ref=ref, dst_ref=ref, send_sem=send_sems.at[w * 3 + j], recv_sem=recv_sems.at[w * 3 + j],
                    device_id=(x, y, 1 - c), device_id_type=MESH)
            out.append((mk(c), mk(1 - c)))
    return out


def _fw_start_body(*refs, n_w):
    bufs = refs[:n_w]
    send_sems, recv_sems = refs[n_w], refs[n_w + 1]
    token = refs[-1]
    for send, _ in _fw_descs(bufs, send_sems, recv_sems):
        send.start()
    token[...] = jnp.zeros_like(token)


def _fw_start(bufs, tag):
    n_w = len(bufs)
    res = _pcall(
        functools.partial(_fw_start_body, n_w=n_w), name="weight_allgather_forward_start_" + tag,
        in_specs=[HBM] * n_w,
        out_specs=[SEM, SEM] + [HBM] * n_w + [pl.BlockSpec(memory_space=pltpu.VMEM)],
        out_shape=[pltpu.SemaphoreType.DMA((3 * n_w,)), pltpu.SemaphoreType.DMA((3 * n_w,))]
        + [pltpu.HBM(b.shape, b.dtype) for b in bufs] + [jax.ShapeDtypeStruct((8, HEAD), F32)],
        input_output_aliases={w: 2 + w for w in range(n_w)},
        compiler_params=pltpu.CompilerParams(has_side_effects=EFFECT),
    )(*bufs)
    return res[0], res[1], list(res[2:2 + n_w]), res[-1]


def _fw_wait_body(*refs, n_w):
    bufs = refs[:n_w]
    send_sems, recv_sems = refs[n_w], refs[n_w + 1]
    for send, back in _fw_descs(bufs, send_sems, recv_sems):
        send.wait_send()
        back.wait_recv()


def _fw_wait(send_sems, recv_sems, bufs, after, tag):
    n_w = len(bufs)
    return list(_pcall(
        functools.partial(_fw_wait_body, n_w=n_w), name="weight_allgather_forward_wait_" + tag,
        in_specs=[HBM] * n_w + [SEM, SEM, ANY], out_specs=[HBM] * n_w,
        out_shape=[pltpu.HBM(b.shape, b.dtype) for b in bufs],
        input_output_aliases={w: w for w in range(n_w)},
        compiler_params=pltpu.CompilerParams(has_side_effects=EFFECT),
    )(*bufs, send_sems, recv_sems, after))


def _ag_forward_body(*refs, n_w):
    bufs = refs[n_w:2 * n_w]
    send_sems, recv_sems = refs[2 * n_w:]
    x, y, c, chips = _place()
    fwds = []
    for w, buf in enumerate(bufs):
        half = buf.shape[1] // 2
        for j, (px, py) in enumerate(chips):
            def piece(which, buf=buf, half=half, px=px, py=py):
                return buf.at[2 * px + py, pl.ds(pl.multiple_of(which * half, 16), half)]
            mk = lambda ref, w=w, j=j: pltpu.make_async_remote_copy(
                src_ref=ref, dst_ref=ref, send_sem=send_sems.at[w * 3 + j], recv_sem=recv_sems.at[w * 3 + j],
                device_id=(x, y, 1 - c), device_id_type=MESH)
            fw = mk(piece(c))
            fw.start()
            fwds.append((fw, mk(piece(1 - c))))
    for fw, back in fwds:
        back.wait_recv()
        fw.wait_send()


def _ag_forward(bufs, tag):
    n_w = len(bufs)
    return list(_pcall(
        functools.partial(_ag_forward_body, n_w=n_w), name="weight_allgather_forward_" + tag,
        in_specs=[ANY] * n_w, out_specs=[ANY] * n_w,
        out_shape=[jax.ShapeDtypeStruct(b.shape, b.dtype) for b in bufs],
        input_output_aliases={w: w for w in range(n_w)},
        scratch_shapes=[pltpu.SemaphoreType.DMA((3 * n_w,))] * 2,
    )(*bufs))


def _sc_descs(ins, outs, send_sems, recv_sems):
    x, y, c, chips = _place()
    me = 2 * x + y
    out = []
    for w in range(len(ins)):
        for j, (px, py) in enumerate(chips):
            out.append(pltpu.make_async_remote_copy(
                src_ref=ins[w].at[2 * px + py], dst_ref=outs[w].at[me],
                send_sem=send_sems.at[w * 3 + j], recv_sem=recv_sems.at[w * 3 + j],
                device_id=(px, py, c), device_id_type=MESH))
    return out


def _scatter_start_body(*refs, n_w):
    ins, lands = refs[:n_w], refs[n_w:2 * n_w]
    send_sems, recv_sems = refs[-2 * n_w - 3], refs[-2 * n_w - 2]
    token = refs[-1]
    for cp in _sc_descs(ins, lands, send_sems, recv_sems):
        cp.start()
    token[...] = jnp.zeros_like(token)


def _scatter_start(parts, after, tag):
    n_w = len(parts)
    lands = [lax.empty(p.shape, p.dtype) for p in parts]
    after = [] if after is None else [after]
    res = _pcall(
        functools.partial(_scatter_start_body, n_w=n_w), name="grad_scatter_start_" + tag,
        in_specs=[HBM] * (2 * n_w) + [ANY] * len(after),
        out_specs=[SEM, SEM] + [HBM] * (2 * n_w) + [pl.BlockSpec(memory_space=pltpu.VMEM)],
        out_shape=[pltpu.SemaphoreType.DMA((3 * n_w,)), pltpu.SemaphoreType.DMA((3 * n_w,))]
        + [pltpu.HBM(p.shape, p.dtype) for p in parts] * 2 + [jax.ShapeDtypeStruct((8, HEAD), F32)],
        input_output_aliases={i: 2 + i for i in range(2 * n_w)},
        compiler_params=pltpu.CompilerParams(has_side_effects=EFFECT),
    )(*[_in_hbm(p) for p in parts], *[_in_hbm(l) for l in lands], *after)
    return res[0], res[1], list(res[2:2 + n_w]), list(res[2 + n_w:2 + 2 * n_w]), res[-1]


def _scatter_wait_body(*refs, n_w):
    ins, lands = refs[:n_w], refs[n_w:2 * n_w]
    send_sems, recv_sems = refs[2 * n_w], refs[2 * n_w + 1]
    for cp in _sc_descs(ins, lands, send_sems, recv_sems):
        cp.wait_send()
        cp.wait_recv()


def _scatter_wait(send_sems, recv_sems, parts, lands, after, tag):
    n_w = len(parts)
    res = _pcall(
        functools.partial(_scatter_wait_body, n_w=n_w), name="grad_scatter_wait_" + tag,
        in_specs=[HBM] * (2 * n_w) + [SEM, SEM, ANY], out_specs=[HBM] * (2 * n_w),
        out_shape=[pltpu.HBM(p.shape, p.dtype) for p in parts] * 2,
        input_output_aliases={i: i for i in range(2 * n_w)},
        compiler_params=pltpu.CompilerParams(has_side_effects=EFFECT),
    )(*parts, *lands, send_sems, recv_sems, after)
    return list(res[:n_w]), list(res[n_w:])


def _pair_send_body(*refs, n_w):
    ins, outs = refs[:n_w], refs[n_w:2 * n_w]
    send_sems, recv_sems = refs[2 * n_w:]
    x, y, c, _ = _place()
    cps = []
    for w in range(n_w):
        cp = pltpu.make_async_remote_copy(
            src_ref=ins[w].at[:, 1 - c], dst_ref=outs[w],
            send_sem=send_sems.at[w], recv_sem=recv_sems.at[w],
            device_id=(x, y, 1 - c), device_id_type=MESH)
        cp.start()
        cps.append(cp)
    for cp in cps:
        cp.wait()


def _pair_send(grads4, tag):
    n_w = len(grads4)
    return _pcall(
        functools.partial(_pair_send_body, n_w=n_w), name="grad_pair_exchange_" + tag,
        in_specs=[ANY] * n_w, out_specs=[ANY] * n_w,
        out_shape=[jax.ShapeDtypeStruct((g.shape[0],) + g.shape[2:], g.dtype) for g in grads4],
        scratch_shapes=[pltpu.SemaphoreType.DMA((n_w,))] * 2,
    )(*grads4)


def _pair_add_body(c_ref, mine_ref, theirs_ref, o_ref):
    o_ref[...] = (mine_ref[...].astype(F32) + theirs_ref[...].astype(F32)).astype(o_ref.dtype)


def _pair_add(c_arr, g4, recv, name):
    _, _, hr, cols = g4.shape
    tr = min(hr, 256)
    grid_spec = pltpu.PrefetchScalarGridSpec(
        num_scalar_prefetch=1, grid=(N_CHIPS, hr // tr),
        in_specs=[pl.BlockSpec((None, None, tr, cols), lambda s, i, c: (s, c[0], i, 0)),
                  pl.BlockSpec((None, tr, cols), lambda s, i, c: (s, i, 0))],
        out_specs=pl.BlockSpec((None, tr, cols), lambda s, i, c: (s, i, 0)))
    return _pcall(
        _pair_add_body, name=name, grid_spec=grid_spec,
        out_shape=jax.ShapeDtypeStruct(recv.shape, recv.dtype),
        compiler_params=pltpu.CompilerParams(dimension_semantics=("parallel", "parallel")),
    )(c_arr, g4, recv)


def _sum4_body(me_ref, p_ref, l0, l1, l2, l3, o_ref):
    me = me_ref[0]
    t = [jnp.where(me == j, p_ref[...], l[...]).astype(F32) for j, l in enumerate((l0, l1, l2, l3))]
    o_ref[...] = ((t[0] + t[1]) + t[2]) + t[3]


def _sum4(me_arr, part, landed, name):
    _, hr, cols = part.shape
    tr = min(hr, 256)

    def slot(j):
        return lambda i, me: (jnp.where(me[0] == j, (j + 1) % N_CHIPS, j), i, 0)

    grid_spec = pltpu.PrefetchScalarGridSpec(
        num_scalar_prefetch=1, grid=(hr // tr,),
        in_specs=[pl.BlockSpec((None, tr, cols), lambda i, me: (me[0], i, 0))]
        + [pl.BlockSpec((None, tr, cols), slot(j)) for j in range(N_CHIPS)],
        out_specs=pl.BlockSpec((tr, cols), lambda i, me: (i, 0)))
    return _pcall(
        _sum4_body, name=name, grid_spec=grid_spec,
        out_shape=jax.ShapeDtypeStruct((hr, cols), F32),
        compiler_params=pltpu.CompilerParams(dimension_semantics=("parallel",)),
    )(me_arr, part, landed, landed, landed, landed)


def _pair_swap_body(*refs, n_w):
    ins, outs = refs[:n_w], refs[n_w:2 * n_w]
    send_sems, recv_sems = refs[2 * n_w:]
    x, y, c, _ = _place()
    todo = []
    for w in range(n_w):
        cp = pltpu.make_async_remote_copy(
            src_ref=ins[w], dst_ref=outs[w],
            send_sem=send_sems.at[w], recv_sem=recv_sems.at[w],
            device_id=(x, y, 1 - c), device_id_type=MESH)
        cp.start()
        todo.append(cp)
    for t in todo:
        t.wait()


def _pair_swap(halves, tag):
    n_w = len(halves)
    return _pcall(
        functools.partial(_pair_swap_body, n_w=n_w), name="grad_pair_swap_" + tag,
        in_specs=[ANY] * n_w, out_specs=[ANY] * n_w,
        out_shape=[jax.ShapeDtypeStruct(h.shape, h.dtype) for h in halves],
        scratch_shapes=[pltpu.SemaphoreType.DMA((n_w,))] * 2,
    )(*halves)


def _small_gather_body(x_ref, out_ref, send_sems, recv_sems, local_sem):
    m_per = x_ref.shape[0]
    x, y, c, chips = _place()
    me, sibling = (x, y, c), (x, y, 1 - c)

    def rows(px, py, pc):
        return out_ref.at[pl.ds((4 * px + 2 * py + pc) * m_per, m_per), :]

    def copy(k, block, to, src=None):
        return pltpu.make_async_remote_copy(
            src_ref=rows(*block) if src is None else src, dst_ref=rows(*block),
            send_sem=send_sems.at[k], recv_sem=recv_sems.at[k], device_id=to, device_id_type=MESH)

    mine = pltpu.make_async_copy(x_ref, rows(*me), local_sem)
    mine.start()
    first = [copy(0, me, sibling, src=x_ref)]
    first += [copy(1 + j, me, (*chip, c), src=x_ref) for j, chip in enumerate(chips)]
    for cp in first:
        cp.start()
    passed = [copy(4 + j, (*chip, c), sibling) for j, chip in enumerate(chips)]
    for j, chip in enumerate(chips):
        copy(1 + j, (*chip, c), me).wait_recv()
        passed[j].start()
    copy(0, sibling, me).wait_recv()
    for j, chip in enumerate(chips):
        copy(4 + j, (*chip, 1 - c), me).wait_recv()
    for cp in first + passed:
        cp.wait_send()
    mine.wait()


def _small_gather(small):
    m_per, n = small.shape
    return _pcall(
        _small_gather_body, name="small_allgather",
        out_shape=jax.ShapeDtypeStruct((N_DEV * m_per, n), small.dtype),
        in_specs=[pl.BlockSpec(memory_space=pltpu.VMEM)],
        out_specs=pl.BlockSpec(memory_space=pltpu.VMEM),
        scratch_shapes=[pltpu.SemaphoreType.DMA((7,)), pltpu.SemaphoreType.DMA((7,)), pltpu.SemaphoreType.DMA],
    )(small)


def _adamw(w, g, m, v):
    m = ADAM_B1 * m + (1.0 - ADAM_B1) * g
    v = ADAM_B2 * v + (1.0 - ADAM_B2) * (g * g)
    m_hat = m / (1.0 - ADAM_B1 ** ADAM_STEP)
    v_hat = v / (1.0 - ADAM_B2 ** ADAM_STEP)
    delta = -ADAM_LR * (m_hat / (jnp.sqrt(v_hat) + ADAM_EPS) + ADAM_WD * w)
    return delta, m, v


def _adamw_body(c_ref, w_ref, own_ref, sib_ref, m_ref, v_ref, g_ref, d_ref, nm_ref, nv_ref, *, nh):
    mine = (pl.program_id(0) // nh) == c_ref[0]
    g = jnp.where(mine, own_ref[...], sib_ref[...])
    g_ref[...] = g
    d, m, v = _adamw(w_ref[...], g, m_ref[...], v_ref[...])
    d_ref[...] = d
    nm_ref[...] = m
    nv_ref[...] = v


def _adamw_call(c_arr, w, own, sib, m, v, name):
    rows, cols = w.shape
    tr = min(rows // 2, 256)
    nh = (rows // 2) // tr
    full = pl.BlockSpec((tr, cols), lambda i, c: (i, 0))
    own_spec = pl.BlockSpec((tr, cols), lambda i, c: (jnp.clip(i - c[0] * nh, 0, nh - 1), 0))
    sib_spec = pl.BlockSpec((tr, cols), lambda i, c: (jnp.clip(i - (1 - c[0]) * nh, 0, nh - 1), 0))
    grid_spec = pltpu.PrefetchScalarGridSpec(
        num_scalar_prefetch=1, grid=(rows // tr,),
        in_specs=[full, own_spec, sib_spec, full, full], out_specs=[full] * 4)
    return _pcall(
        functools.partial(_adamw_body, nh=nh), name=name, grid_spec=grid_spec,
        out_shape=[jax.ShapeDtypeStruct(w.shape, F32)] * 4,
        compiler_params=pltpu.CompilerParams(dimension_semantics=("parallel",)),
    )(c_arr, w, own, sib, m, v)


def _small_update_body(gath_ref, w_ref, m_ref, v_ref, g_ref, d_ref, nm_ref, nv_ref, loss_ref, *, n_gain):
    tot = gath_ref[0:1, :]
    for i in range(1, gath_ref.shape[0]):
        tot = tot + gath_ref[i:i + 1, :]
    g = tot[:, 0:n_gain]
    g_ref[...] = g
    d, m, v = _adamw(w_ref[...], g, m_ref[...], v_ref[...])
    d_ref[...] = d
    nm_ref[...] = m
    nv_ref[...] = v
    loss_ref[...] = (0.5 / D_MODEL) * jnp.sum(tot[:, n_gain:n_gain + HEAD], axis=1, keepdims=True) * jnp.ones((1, HEAD), F32)


def _small_update(gath, w, m, v):
    n_gain = w.shape[1]
    vm = pl.BlockSpec(memory_space=pltpu.VMEM)
    return _pcall(
        functools.partial(_small_update_body, n_gain=n_gain), name="gain_update",
        in_specs=[vm] * 4, out_specs=[vm] * 5,
        out_shape=[jax.ShapeDtypeStruct((1, n_gain), F32)] * 4 + [jax.ShapeDtypeStruct((1, HEAD), F32)],
    )(gath, w, m, v)


def kernel(x, positions, norm_attn_pre, norm_attn_post, w_in, q_latent_norm, kv_latent_norm, w_uq, w_ukv, w_out, norm_mlp_pre, norm_mlp_post, w_up, w_down, loss_target, m_norm_attn_pre, m_norm_attn_post, m_w_in, m_q_latent_norm, m_kv_latent_norm, m_w_uq, m_w_ukv, m_w_out, m_norm_mlp_pre, m_norm_mlp_post, m_w_up, m_w_down, v_norm_attn_pre, v_norm_attn_post, v_w_in, v_q_latent_norm, v_kv_latent_norm, v_w_uq, v_w_ukv, v_w_out, v_norm_mlp_pre, v_norm_mlp_post, v_w_up, v_w_down):
    T = x.shape[1]
    c_arr = lax.axis_index("c").astype(jnp.int32).reshape(1)
    me_arr = (2 * lax.axis_index("x") + lax.axis_index("y")).astype(jnp.int32).reshape(1)
    names = ["w_in", "w_uq", "w_ukv", "w_out", "w_up", "w_down"]

    mats = [w_in[0], w_uq[0], w_ukv[0], w_out[0], w_up[0], w_down[0]]
    in_send, in_recv, in_bufs, in_started = _ag_start([_cast_place(me_arr, mats[0], "cast_w_in")], None, "in")
    placed = [_cast_place(me_arr, w, "cast_" + n) for w, n in zip(mats[1:], names[1:])]
    att_send, att_recv, att_bufs, att_started = _ag_start(placed[:3], in_started, "attn")
    mlp_send, mlp_recv, mlp_bufs, started = _ag_start(placed[3:], att_started, "mlp")

    col_major = lambda g: jnp.transpose(g, (1, 0, 2)).reshape(g.shape[1], N_CHIPS * g.shape[2])
    cast = lambda a: a.astype(MXU_DTYPE)
    to_shards = lambda g: jnp.transpose(g.reshape(g.shape[0], N_CHIPS, g.shape[1] // N_CHIPS), (1, 0, 2))
    halved = lambda g: g.reshape(N_CHIPS, 2, g.shape[1] // 2, g.shape[2])

    def pair_sum(full4, ns):
        from_sib = _pair_send(full4, "_".join(ns))
        return [_pair_add(c_arr, g4, r, "pair_add_" + n) for g4, r, n in zip(full4, from_sib, ns)]

    def in_weights(after):
        (win_g,) = _ag_forward(_ag_wait(in_send, in_recv, in_bufs, after, "in"), "in")
        return cast(jnp.pad(col_major(win_g), ((0, 0), (0, PROJ_COLS - IN_COLS))))

    def attn_weights(after):
        wuq_g, wukv_g, wout_g = _ag_forward(_ag_wait(att_send, att_recv, att_bufs, after, "attn"), "attn")
        wuq_full = col_major(wuq_g).reshape(LORA, NH, HEAD + ROPE_B)
        w_uq_p = jnp.pad(wuq_full, ((0, 0), (0, 0), (0, QPAD - HEAD - ROPE_B))).reshape(LORA, NH * QPAD)
        w_ukv_p = col_major(wukv_g).reshape(LORA, NH, 2, HEAD).transpose(0, 2, 1, 3).reshape(LORA, 2 * A_W)
        return cast(w_uq_p), cast(w_ukv_p), cast(wout_g.reshape(2 * A_W, D_MODEL))

    in_flight = {}

    def mlp_prefetch(after):
        in_flight["fw"] = _fw_start(_ag_wait(mlp_send, mlp_recv, mlp_bufs, after, "mlp"), "mlp")

    def mlp_weights(after):
        f_send, f_recv, bufs, _ = in_flight["fw"]
        wup_g, wdown_g = _fw_wait(f_send, f_recv, bufs, after, "mlp")
        return cast(col_major(wup_g)), cast(wdown_g.reshape(D_FF, D_MODEL))

    def mlp_grads_ready(gw_up, gw_down):
        parts = pair_sum([halved(to_shards(gw_up)), halved(gw_down.reshape(N_CHIPS, D_MODEL, D_MODEL))], names[4:])
        in_flight["mlp"] = _scatter_start(parts, started, "mlp")
        return in_flight["mlp"][-1][0:1, 0:1]

    def attn_grads_ready(gw_out, gw_uq_p, gw_ukv_p):
        gw_uq = to_shards(gw_uq_p.reshape(LORA, NH, QPAD)[:, :, :HEAD + ROPE_B].reshape(LORA, NH * (HEAD + ROPE_B)))
        gw_ukv = to_shards(gw_ukv_p.reshape(LORA, 2, NH, HEAD).transpose(0, 2, 1, 3).reshape(LORA, 2 * A_W))
        parts = pair_sum([halved(g) for g in (gw_uq, gw_ukv, gw_out.reshape(N_CHIPS, LORA, D_MODEL))], names[1:4])
        in_flight["attn"] = _scatter_start(parts, in_flight["mlp"][-1], "attn")
        return in_flight["attn"][-1][0:1, 0:1]

    dx, gw_proj, small = _local_step(
        x[0], positions[0].astype(F32).reshape(T, 1), loss_target[0],
        norm_attn_pre + started[0:1, 0:1], norm_attn_post, q_latent_norm, kv_latent_norm, norm_mlp_pre, norm_mlp_post,
        in_weights, attn_weights, mlp_prefetch, mlp_weights, mlp_grads_ready, attn_grads_ready)

    ms = [m_w_in[0], m_w_uq[0], m_w_ukv[0], m_w_out[0], m_w_up[0], m_w_down[0]]
    vs = [v_w_in[0], v_w_uq[0], v_w_ukv[0], v_w_out[0], v_w_up[0], v_w_down[0]]

    def finish(parts, landed, lo, hi, tag):
        halves = [_sum4(me_arr, p, l, "chip_sum_" + n) for p, l, n in zip(parts, landed, names[lo:hi])]
        from_sib2 = _pair_swap(halves, tag)
        return [_adamw_call(c_arr, w, own, sib, m, v, "adamw_" + n)
                for w, own, sib, m, v, n in zip(mats[lo:hi], halves, from_sib2, ms[lo:hi], vs[lo:hi], names[lo:hi])]

    gw_in = to_shards(gw_proj[:, :IN_COLS])
    i_send, i_recv, parts_in, lands_in, in_going = _scatter_start(pair_sum([halved(gw_in)], names[:1]), None, "in")
    a_send, a_recv, parts_att, lands_att, _ = in_flight["attn"]
    parts_att, landed_att = _scatter_wait(a_send, a_recv, parts_att, lands_att, in_going, "attn")
    m_send, m_recv, parts_mlp, lands_mlp, _ = in_flight["mlp"]
    parts_mlp, landed_mlp = _scatter_wait(m_send, m_recv, parts_mlp, lands_mlp, landed_att[0], "mlp")
    upd_rest = finish(parts_att + parts_mlp, landed_att + landed_mlp, 1, 6, "rest")
    parts_in, landed_in = _scatter_wait(i_send, i_recv, parts_in, lands_in, upd_rest[-1][0], "in")
    upd = finish(parts_in, landed_in, 0, 1, "in") + upd_rest
    grads = [u[0] for u in upd]

    gath = _small_gather(small)
    gains = [norm_attn_pre, norm_attn_post, q_latent_norm, kv_latent_norm, norm_mlp_pre, norm_mlp_post]
    gm = [m_norm_attn_pre, m_norm_attn_post, m_q_latent_norm, m_kv_latent_norm, m_norm_mlp_pre, m_norm_mlp_post]
    gv = [v_norm_attn_pre, v_norm_attn_post, v_q_latent_norm, v_kv_latent_norm, v_norm_mlp_pre, v_norm_mlp_post]
    cat = lambda xs: jnp.concatenate(xs, axis=1)
    g_s, d_s, m_s, v_s, loss_v = _small_update(gath, cat(gains), cat(gm), cat(gv))
    widths = [a.shape[1] for a in gains]
    offs = [sum(widths[:i]) for i in range(len(widths))]
    split = lambda a: [a[:, o:o + w] for o, w in zip(offs, widths)]
    g_gain, d_gain, m_gain, v_gain = split(g_s), split(d_s), split(m_s), split(v_s)

    def ordered(gain_list, mat_list):
        gl, ml = gain_list, [a[None] for a in mat_list]
        return [gl[0], gl[1], ml[0], gl[2], gl[3], ml[1], ml[2], ml[3], gl[4], gl[5], ml[4], ml[5]]

    loss = loss_v[0, 0]
    return (loss, dx[None],
            *ordered(g_gain, grads),
            *ordered(d_gain, [u[1] for u in upd]),
            *ordered(m_gain, [u[2] for u in upd]),
            *ordered(v_gain, [u[3] for u in upd]))
```

```python
import functools

import jax
import jax.numpy as jnp
from jax import lax
from jax.experimental import pallas as pl
from jax.experimental.pallas import tpu as pltpu

F32 = jnp.float32
BF16 = jnp.bfloat16
MXU_DTYPE = jnp.bfloat16
WIRE_DTYPE = jnp.bfloat16

D_MODEL = 2048
HEAD = 128
NH = 8
A_W = NH * HEAD
LORA = 512
ROPE_B = 64
QPAD = 256
MAIN_COLS = 3 * A_W + 2 * LORA
IN_COLS = MAIN_COLS + ROPE_B
PROJ_COLS = MAIN_COLS + HEAD
PROJ_TILE = PROJ_COLS // 3
D_FF = 4 * D_MODEL
DIL = (1, 4, 16)
ROT_A = 32
ROPE_THETA = 500000.0
EPS = 1e-6
NEG = -1e30
N_CHIPS = 4
N_DEV = 8

ADAM_LR = 0.001
ADAM_B1 = 0.9
ADAM_B2 = 0.999
ADAM_EPS = 1e-08
ADAM_WD = 0.01
ADAM_STEP = 10

MESH = pl.DeviceIdType.MESH
ANY = pl.BlockSpec(memory_space=pl.ANY)


def _pcall(body, **kw):
    return pl.pallas_call(body, **kw)


_DIMS = {
    "nn": (((1,), (0,)), ((), ())),
    "nt": (((1,), (1,)), ((), ())),
    "tn": (((0,), (0,)), ((), ())),
}


def _mm_body(*refs, dims, nk, epi, n_extra, n_out):
    a_ref, b_ref = refs[0], refs[1]
    extra = refs[2:2 + n_extra]
    outs = refs[2 + n_extra:2 + n_extra + n_out]
    part = lax.dot_general(a_ref[...], b_ref[...], _DIMS[dims], preferred_element_type=F32)

    def finish(acc):
        res = epi(acc, *[r[...] for r in extra]) if epi is not None else (acc,)
        for o_ref, o in zip(outs, res):
            o_ref[...] = o.astype(o_ref.dtype)

    if nk == 1:
        finish(part)
        return
    acc_ref = refs[-1]
    k = pl.program_id(2)

    @pl.when(k == 0)
    def _():
        acc_ref[...] = part

    @pl.when(k > 0)
    def _():
        acc_ref[...] += part

    @pl.when(k == nk - 1)
    def _():
        finish(acc_ref[...])


def _matmul(a, b, *, dims, out_dtypes, tm, tn, tk, name, epi=None, extras=(), row_extras=(), b_outer=False,
            b_shards=0, out_shards=0):
    if b_shards:
        assert dims in ("nn", "nt") and b.shape[0] == b_shards
        b2 = (b.shape[1], b_shards * b.shape[2])
    else:
        b2 = b.shape
    if dims == "nn":
        (M, K), (K2, N) = a.shape, b2
    elif dims == "nt":
        (M, K), (N, K2) = a.shape, b2
    else:
        (K, M), (K2, N) = a.shape, b2
    assert K == K2, (a.shape, b.shape, dims)
    tm, tn, tk = min(tm, M), min(tn, N), min(tk, K)
    assert M % tm == 0 and N % tn == 0 and K % tk == 0, (name, M, N, K, tm, tn, tk)
    nk = K // tk

    def at(f):
        if b_outer:
            return lambda j, i, k: f(i, j, k)
        return f

    a_spec = {"nn": pl.BlockSpec((tm, tk), at(lambda i, j, k: (i, k))),
              "nt": pl.BlockSpec((tm, tk), at(lambda i, j, k: (i, k))),
              "tn": pl.BlockSpec((tk, tm), at(lambda i, j, k: (k, i)))}[dims]
    b_spec = {"nn": pl.BlockSpec((tk, tn), at(lambda i, j, k: (k, j))),
              "nt": pl.BlockSpec((tn, tk), at(lambda i, j, k: (j, k))),
              "tn": pl.BlockSpec((tk, tn), at(lambda i, j, k: (k, j)))}[dims]
    if b_shards:
        per = b.shape[2] // (tn if dims == "nn" else tk)
        assert per >= 1 and b.shape[2] % (tn if dims == "nn" else tk) == 0
        b_spec = {"nn": pl.BlockSpec((None, tk, tn), at(lambda i, j, k: (j // per, k, j % per))),
                  "nt": pl.BlockSpec((None, tn, tk), at(lambda i, j, k: (k // per, j, k % per)))}[dims]
    o_spec = pl.BlockSpec((tm, tn), at(lambda i, j, k: (i, j)))
    o_shape = (M, N)
    if out_shards:
        assert not extras and N % out_shards == 0 and (N // out_shards) % tn == 0
        o_per = (N // out_shards) // tn
        o_spec = pl.BlockSpec((None, tm, tn), at(lambda i, j, k: (j // o_per, i, j % o_per)))
        o_shape = (out_shards, M, N // out_shards)
    r_specs = [pl.BlockSpec((tm, r.shape[1]), at(lambda i, j, k: (i, 0))) for r in row_extras]
    body = functools.partial(_mm_body, dims=dims, nk=nk, epi=epi,
                             n_extra=len(extras) + len(row_extras), n_out=len(out_dtypes))
    res = _pcall(
        body, name=name,
        grid=(N // tn, M // tm, nk) if b_outer else (M // tm, N // tn, nk),
        in_specs=[a_spec, b_spec] + [o_spec] * len(extras) + r_specs,
        out_specs=[o_spec] * len(out_dtypes),
        out_shape=[jax.ShapeDtypeStruct(o_shape, dt) for dt in out_dtypes],
        scratch_shapes=[pltpu.VMEM((tm, tn), F32)] if nk > 1 else [],
        compiler_params=pltpu.CompilerParams(
            dimension_semantics=("parallel", "parallel", "arbitrary")),
    )(a, b, *extras, *row_extras)
    return list(res)


def _rowwise(body, row_ins, vec_ins, row_outs, acc_outs, *, tr, name):
    T = row_ins[0].shape[0]
    tr = min(tr, T)
    assert T % tr == 0
    in_specs = [pl.BlockSpec((tr, a.shape[1]), lambda i: (i, 0)) for a in row_ins]
    in_specs += [pl.BlockSpec(a.shape, lambda i: (0, 0)) for a in vec_ins]
    out_specs = [pl.BlockSpec((tr, w), lambda i: (i, 0)) for (w, _) in row_outs]
    out_specs += [pl.BlockSpec(s, lambda i: (0, 0)) for s in acc_outs]
    out_shape = [jax.ShapeDtypeStruct((T, w), dt) for (w, dt) in row_outs]
    out_shape += [jax.ShapeDtypeStruct(s, F32) for s in acc_outs]
    sem = "arbitrary" if acc_outs else "parallel"
    return list(_pcall(
        body, name=name, grid=(T // tr,), in_specs=in_specs, out_specs=out_specs,
        out_shape=out_shape,
        compiler_params=pltpu.CompilerParams(dimension_semantics=(sem,)),
    )(*row_ins, *vec_ins))


def _rstd(x):
    return lax.rsqrt(jnp.mean(x * x, axis=-1, keepdims=True) + EPS)


def _rms_bwd(x, rstd, dyg):
    xh = x * rstd
    return rstd * (dyg - xh * jnp.mean(dyg * xh, axis=-1, keepdims=True)), xh


def _fold8(v):
    r, w = v.shape
    return jnp.sum(v.reshape(r // 8, 8, w), axis=0)


def _acc(ref, val):
    first = pl.program_id(0) == 0

    @pl.when(first)
    def _():
        ref[...] = val

    @pl.when(jnp.logical_not(first))
    def _():
        ref[...] += val


def _rope(x, c, sa, sb, half):
    return x * c + pltpu.roll(x, HEAD - half, 1) * sa + pltpu.roll(x, half, 1) * sb


def _rope_t(dy, c, sa, sb, half):
    return dy * c - pltpu.roll(dy, HEAD - half, 1) * sa - pltpu.roll(dy, half, 1) * sb


def _rope_tab_body(pos_ref, inv_ref, ca, saa, sab, cb, sba, sbb):
    pos = pos_ref[...]
    lane = lax.broadcasted_iota(jnp.int32, (pos.shape[0], HEAD), 1)
    ang_a = pos * inv_ref[0:1, :]
    ang_b = pos * inv_ref[1:2, :]
    c, s = jnp.cos(ang_a), jnp.sin(ang_a)
    ha = ROT_A // 2
    ca[...] = jnp.where(lane < ROT_A, c, 1.0)
    saa[...] = jnp.where(lane < ha, -s, 0.0)
    sab[...] = jnp.where((lane >= ha) & (lane < ROT_A), s, 0.0)
    c, s = jnp.cos(ang_b), jnp.sin(ang_b)
    hb = ROPE_B // 2
    cb[...] = jnp.where(lane < ROPE_B, c, 1.0)
    sba[...] = jnp.where(lane < hb, -s, 0.0)
    sbb[...] = jnp.where((lane >= hb) & (lane < ROPE_B), s, 0.0)


def _rms_fwd_body(x_ref, g_ref, h_ref):
    x = x_ref[...]
    h_ref[...] = ((x * _rstd(x)) * g_ref[...]).astype(h_ref.dtype)


def _postproj_body(p_ref, ca, saa, sab, cb, sba, sbb, gq_ref, gkv_ref,
                   q_ref, k_ref, v_ref, cqn_ref, ckvn_ref, krope_ref):
    c, sa, sb = ca[...], saa[...], sab[...]
    for h in range(NH):
        lo = h * HEAD
        q_ref[:, lo:lo + HEAD] = _rope(p_ref[:, lo:lo + HEAD], c, sa, sb, ROT_A // 2).astype(q_ref.dtype)
        k_ref[:, lo:lo + HEAD] = _rope(p_ref[:, A_W + lo:A_W + lo + HEAD], c, sa, sb, ROT_A // 2).astype(k_ref.dtype)
    v_ref[...] = p_ref[:, 2 * A_W:3 * A_W].astype(v_ref.dtype)
    cq = p_ref[:, 3 * A_W:3 * A_W + LORA]
    cqn_ref[...] = ((cq * _rstd(cq)) * gq_ref[...]).astype(cqn_ref.dtype)
    ckv = p_ref[:, 3 * A_W + LORA:MAIN_COLS]
    ckvn_ref[...] = ((ckv * _rstd(ckv)) * gkv_ref[...]).astype(ckvn_ref.dtype)
    krope_ref[...] = _rope(p_ref[:, MAIN_COLS:PROJ_COLS], cb[...], sba[...], sbb[...], ROPE_B // 2).astype(krope_ref.dtype)


def _mid_body(x_ref, o_ref, g2_ref, g3_ref, x1_ref, h2_ref):
    o = o_ref[...]
    x1 = x_ref[...] + (o * _rstd(o)) * g2_ref[...]
    x1_ref[...] = x1
    h2_ref[...] = ((x1 * _rstd(x1)) * g3_ref[...]).astype(h2_ref.dtype)


def _loss_body(x1_ref, d_ref, t_ref, g4_ref, dy_ref, dd_ref, loss_ref, dg4_ref):
    d = d_ref[...]
    rstd = _rstd(d)
    y = x1_ref[...] + (d * rstd) * g4_ref[...]
    e = y - t_ref[...]
    dy = e * (1.0 / D_MODEL)
    dy_ref[...] = dy
    dd, dh = _rms_bwd(d, rstd, dy * g4_ref[...])
    dd_ref[...] = dd.astype(dd_ref.dtype)
    _acc(dg4_ref, _fold8(dy * dh))
    e8 = _fold8(e * e)
    l = e8[:, 0:HEAD]
    for j in range(1, D_MODEL // HEAD):
        l = l + e8[:, j * HEAD:(j + 1) * HEAD]
    _acc(loss_ref, l)


def _bmid_body(dy_ref, dh2_ref, x1_ref, o_ref, g2_ref, g3_ref, dx1_ref, do_ref, dg3_ref, dg2_ref):
    x1 = x1_ref[...]
    dh2 = dh2_ref[...]
    dn, x1h = _rms_bwd(x1, _rstd(x1), dh2 * g3_ref[...])
    dx1 = dy_ref[...] + dn
    dx1_ref[...] = dx1
    _acc(dg3_ref, _fold8(dh2 * x1h))
    o = o_ref[...]
    do, oh = _rms_bwd(o, _rstd(o), dx1 * g2_ref[...])
    do_ref[...] = do.astype(do_ref.dtype)
    _acc(dg2_ref, _fold8(dx1 * oh))


def _dproj_body(dq_ref, dk_ref, dv_ref, dcq_ref, dckv_ref, p_ref, dkr_ref,
                ca, saa, sab, cb, sba, sbb, gq_ref, gkv_ref,
                dp_ref, dgq_ref, dgkv_ref):
    c, sa, sb = ca[...], saa[...], sab[...]
    for h in range(NH):
        lo = h * HEAD
        dp_ref[:, lo:lo + HEAD] = _rope_t(dq_ref[:, lo:lo + HEAD], c, sa, sb, ROT_A // 2).astype(dp_ref.dtype)
        dp_ref[:, A_W + lo:A_W + lo + HEAD] = _rope_t(dk_ref[:, lo:lo + HEAD], c, sa, sb, ROT_A // 2).astype(dp_ref.dtype)
    dp_ref[:, 2 * A_W:3 * A_W] = dv_ref[...].astype(dp_ref.dtype)
    cq = p_ref[:, 3 * A_W:3 * A_W + LORA]
    dcqn = dcq_ref[...]
    dcq, cqh = _rms_bwd(cq, _rstd(cq), dcqn * gq_ref[...])
    dp_ref[:, 3 * A_W:3 * A_W + LORA] = dcq.astype(dp_ref.dtype)
    _acc(dgq_ref, _fold8(dcqn * cqh))
    ckv = p_ref[:, 3 * A_W + LORA:MAIN_COLS]
    dckvn = dckv_ref[...]
    dckv, ckvh = _rms_bwd(ckv, _rstd(ckv), dckvn * gkv_ref[...])
    dp_ref[:, 3 * A_W + LORA:MAIN_COLS] = dckv.astype(dp_ref.dtype)
    _acc(dgkv_ref, _fold8(dckvn * ckvh))
    dkr = dkr_ref[:, 0:HEAD]
    for h in range(1, NH):
        dkr = dkr + dkr_ref[:, h * HEAD:(h + 1) * HEAD]
    dp_ref[:, MAIN_COLS:PROJ_COLS] = _rope_t(dkr, cb[...], sba[...], sbb[...], ROPE_B // 2).astype(dp_ref.dtype)


def _bin_body(dx1_ref, dh_ref, x_ref, g1_ref, dx_ref, dg1_ref):
    x = x_ref[...]
    dh = dh_ref[...]
    dn, xh = _rms_bwd(x, _rstd(x), dh * g1_ref[...])
    dx_ref[...] = dx1_ref[...] + dn
    _acc(dg1_ref, _fold8(dh * xh))


def _dot_nt(a, b):
    return lax.dot_general(a, b, _DIMS["nt"], preferred_element_type=F32)


def _dot_tn(a, b):
    return lax.dot_general(a, b, _DIMS["tn"], preferred_element_type=F32)


def _dot_nn(a, b):
    return jnp.dot(a, b, preferred_element_type=F32)


DIL_SCALE = HEAD ** -0.5
DIL_CHUNK = 256


def _dil_rows(t, d):
    r = t & (d - 1)
    n = t >> (d.bit_length() - 1)
    start = r + n * (HEAD * d)
    has_prev = n > 0
    pstart = jnp.where(has_prev, start - HEAD * d, start)
    if d == 1:
        return pl.ds(pl.multiple_of(start, HEAD), HEAD), pl.ds(pl.multiple_of(pstart, HEAD), HEAD), has_prev
    return pl.ds(start, HEAD, stride=d), pl.ds(pstart, HEAD, stride=d), has_prev


def _dil_band():
    row = lax.broadcasted_iota(jnp.int32, (HEAD, 2 * HEAD), 0)
    col = lax.broadcasted_iota(jnp.int32, (HEAD, 2 * HEAD), 1)
    return (col >= row) & (col <= row + HEAD), col >= HEAD


def _dil_fwd_body(q_ref, k_ref, v_ref, a_ref, lse_ref, o1, o2, o3, l1, l2, l3, *, nt, unroll):
    band, is_cur = _dil_band()
    for d, o_sc, l_sc in zip(DIL, (o1, o2, o3), (l1, l2, l3)):

        def tile(t, carry, d=d, o_sc=o_sc, l_sc=l_sc):
            rows, prows, has_prev = _dil_rows(t, d)
            q = q_ref[rows, :].astype(MXU_DTYPE)
            kk = jnp.concatenate([k_ref[prows, :], k_ref[rows, :]], axis=0).astype(MXU_DTYPE)
            vv = jnp.concatenate([v_ref[prows, :], v_ref[rows, :]], axis=0).astype(MXU_DTYPE)
            ok = band & (is_cur | has_prev)
            s = jnp.where(ok, _dot_nt(q, kk) * DIL_SCALE, NEG)
            m = jnp.max(s, axis=1, keepdims=True)
            p = jnp.exp(s - m)
            den = jnp.sum(p, axis=1, keepdims=True)
            o_sc[rows, :] = _dot_nn((p / den).astype(MXU_DTYPE), vv)
            l_sc[rows, :] = jnp.broadcast_to(m + jnp.log(den), (HEAD, HEAD))
            return carry

        lax.fori_loop(0, nt, tile, 0, unroll=unroll)

    def merge(i, carry):
        rs = pl.ds(pl.multiple_of(i * DIL_CHUNK, DIL_CHUNK), DIL_CHUNK)
        la, lb, lc = l1[rs, :], l2[rs, :], l3[rs, :]
        m = jnp.maximum(jnp.maximum(la, lb), lc)
        wa, wb, wc = jnp.exp(la - m), jnp.exp(lb - m), jnp.exp(lc - m)
        den = wa + wb + wc
        a = (wa / den) * o1[rs, :] + (wb / den) * o2[rs, :] + (wc / den) * o3[rs, :]
        a_ref[rs, :] = a.astype(a_ref.dtype)
        lse_ref[rs, :] = m + jnp.log(den)
        return carry

    lax.fori_loop(0, q_ref.shape[0] // DIL_CHUNK, merge, 0)


def _dil_fwd(q, k, v):
    T = q.shape[0]
    spec = pl.BlockSpec((T, HEAD), lambda h: (0, h))
    return _pcall(
        functools.partial(_dil_fwd_body, nt=T // HEAD, unroll=16), name="dil_fwd",
        grid=(NH,), in_specs=[spec] * 3, out_specs=[spec] * 2,
        out_shape=[jax.ShapeDtypeStruct((T, 2 * A_W), MXU_DTYPE), jax.ShapeDtypeStruct((T, A_W), F32)],
        scratch_shapes=[pltpu.VMEM((T, HEAD), F32)] * 6,
        compiler_params=pltpu.CompilerParams(dimension_semantics=("parallel",)),
    )(q, k, v)


def _dil_bwd_body(q_ref, k_ref, v_ref, do_ref, a_ref, lse_ref, dq_ref, dk_ref, dv_ref, dl_sc, *, nt, unroll):
    band, is_cur = _dil_band()

    def prep(i, carry):
        rs = pl.ds(pl.multiple_of(i * DIL_CHUNK, DIL_CHUNK), DIL_CHUNK)
        dl = jnp.sum(do_ref[rs, :] * a_ref[rs, :].astype(F32), axis=1, keepdims=True)
        dl_sc[rs, :] = jnp.broadcast_to(dl, (DIL_CHUNK, HEAD))
        zero = jnp.zeros((DIL_CHUNK, HEAD), F32)
        dq_ref[rs, :] = zero
        dk_ref[rs, :] = zero
        dv_ref[rs, :] = zero
        return carry

    lax.fori_loop(0, q_ref.shape[0] // DIL_CHUNK, prep, 0)

    for d in DIL:

        def tile(t, carry, d=d):
            rows, prows, has_prev = _dil_rows(t, d)
            q = q_ref[rows, :].astype(MXU_DTYPE)
            kk = jnp.concatenate([k_ref[prows, :], k_ref[rows, :]], axis=0).astype(MXU_DTYPE)
            vv = jnp.concatenate([v_ref[prows, :], v_ref[rows, :]], axis=0).astype(MXU_DTYPE)
            do = do_ref[rows, :].astype(MXU_DTYPE)
            lse = lse_ref[rows, :]
            dl = dl_sc[rows, :]
            ok = band & (is_cur | has_prev)
            s = _dot_nt(q, kk) * DIL_SCALE
            p = jnp.where(ok, jnp.exp(s - jnp.concatenate([lse, lse], axis=1)), 0.0)
            ds = (p * (_dot_nt(do, vv) - jnp.concatenate([dl, dl], axis=1))).astype(MXU_DTYPE)
            dq_ref[rows, :] += _dot_nn(ds, kk) * DIL_SCALE
            dkk = _dot_tn(ds, q) * DIL_SCALE
            dvv = _dot_tn(p.astype(MXU_DTYPE), do)
            dk_ref[rows, :] += dkk[HEAD:, :]
            dv_ref[rows, :] += dvv[HEAD:, :]
            dk_ref[prows, :] += dkk[:HEAD, :]
            dv_ref[prows, :] += dvv[:HEAD, :]
            return carry

        lax.fori_loop(0, nt, tile, 0, unroll=unroll)


def _dil_bwd(q, k, v, dmix, mixed, lse):
    T = q.shape[0]
    spec = pl.BlockSpec((T, HEAD), lambda h: (0, h))
    return _pcall(
        functools.partial(_dil_bwd_body, nt=T // HEAD, unroll=8), name="dil_bwd",
        grid=(NH,), in_specs=[spec] * 6, out_specs=[spec] * 3,
        out_shape=[jax.ShapeDtypeStruct((T, A_W), F32)] * 3,
        scratch_shapes=[pltpu.VMEM((T, HEAD), F32)],
        compiler_params=pltpu.CompilerParams(dimension_semantics=("parallel",)),
    )(q, k, v, dmix, mixed, lse)


MLA_SCALE = (HEAD + ROPE_B) ** -0.5
MLA_T = 512
MLA_HP = 2


def _tri(t):
    row = lax.broadcasted_iota(jnp.int32, (t, t), 0)
    col = lax.broadcasted_iota(jnp.int32, (t, t), 1)
    return col <= row


def _lanes(x, n):
    return jnp.tile(x, (1, n // HEAD))


def _mla_fwd_body(q_ref, kn_ref, kr_ref, v_ref, mixed_ref, o_ref, lse_ref, m_sc, l_sc, acc_sc, *, t, hp):
    del mixed_ref
    qi = pl.program_id(1)
    m_sc[...] = jnp.full(m_sc.shape, NEG, F32)
    l_sc[...] = jnp.zeros(l_sc.shape, F32)
    acc_sc[...] = jnp.zeros(acc_sc.shape, F32)

    def step(j, masked):
        ks = pl.ds(pl.multiple_of(j * t, t), t)
        kr = kr_ref[ks, :]
        for hh in range(hp):
            kcat = jnp.concatenate([kn_ref[ks, hh * HEAD:(hh + 1) * HEAD], kr], axis=1)
            s = _dot_nt(q_ref[:, hh * QPAD:(hh + 1) * QPAD], kcat) * MLA_SCALE
            if masked:
                s = jnp.where(_tri(t), s, NEG)
            m_prev = m_sc[hh]
            m_new = jnp.maximum(m_prev, jnp.max(s, axis=1, keepdims=True))
            alpha = jnp.exp(m_prev - m_new)
            p = jnp.exp(s - _lanes(m_new, t))
            l_sc[hh] = alpha * l_sc[hh] + jnp.sum(p, axis=1, keepdims=True)
            acc_sc[hh] = alpha * acc_sc[hh] + _dot_nn(p.astype(MXU_DTYPE), v_ref[ks, hh * HEAD:(hh + 1) * HEAD])
            m_sc[hh] = m_new

    def off_diag(j, carry):
        step(j, False)
        return carry

    lax.fori_loop(0, qi, off_diag, 0)
    step(qi, True)
    for hh in range(hp):
        l = l_sc[hh]
        o_ref[:, hh * HEAD:(hh + 1) * HEAD] = (acc_sc[hh] / l).astype(o_ref.dtype)
        lse_ref[:, hh * HEAD:(hh + 1) * HEAD] = m_sc[hh] + jnp.log(l)


def _mla_fwd(qf, kv, kr, mixed):
    T = qf.shape[0]
    t, hp = min(MLA_T, T), MLA_HP
    ng = NH // hp
    return _pcall(
        functools.partial(_mla_fwd_body, t=t, hp=hp), name="mla_fwd",
        grid=(ng, T // t),
        in_specs=[pl.BlockSpec((t, hp * QPAD), lambda g, i: (i, g)),
                  pl.BlockSpec((T, hp * HEAD), lambda g, i: (0, g)),
                  pl.BlockSpec((T, HEAD), lambda g, i: (0, 0)),
                  pl.BlockSpec((T, hp * HEAD), lambda g, i: (0, ng + g)), ANY],
        out_specs=[pl.BlockSpec((t, hp * HEAD), lambda g, i: (i, ng + g)),
                   pl.BlockSpec((t, hp * HEAD), lambda g, i: (i, g))],
        out_shape=[jax.ShapeDtypeStruct(mixed.shape, mixed.dtype), jax.ShapeDtypeStruct((T, A_W), F32)],
        input_output_aliases={4: 0},
        scratch_shapes=[pltpu.VMEM((hp, t, HEAD), F32)] * 3,
        compiler_params=pltpu.CompilerParams(dimension_semantics=("parallel", "parallel")),
    )(qf, kv, kr, kv, mixed)


def _mla_bwd_body(q_ref, kn_ref, kr_ref, v_ref, do_ref, o_ref, lse_ref, cb, sba, sbb,
                  dq_ref, dkn_ref, dv_ref, dkr_ref, dq_sc, dl_sc, dk_sc, dv_sc, *, t):
    ki = pl.program_id(1)
    nq = q_ref.shape[0] // t

    @pl.when(ki == 0)
    def _():
        def prep(i, carry):
            rs = pl.ds(pl.multiple_of(i * t, t), t)
            dl = jnp.sum(do_ref[rs, :] * o_ref[rs, :].astype(F32), axis=1, keepdims=True)
            dl_sc[rs, :] = jnp.broadcast_to(dl, (t, HEAD))
            dq_sc[rs, :] = jnp.zeros((t, QPAD), F32)
            return carry
        lax.fori_loop(0, nq, prep, 0)

    kcat = jnp.concatenate([kn_ref[...], kr_ref[...]], axis=1)
    v = v_ref[...]
    dk_sc[...] = jnp.zeros(dk_sc.shape, F32)
    dv_sc[...] = jnp.zeros(dv_sc.shape, F32)

    def step(i, masked):
        qs = pl.ds(pl.multiple_of(i * t, t), t)
        q = q_ref[qs, :]
        do = do_ref[qs, :].astype(MXU_DTYPE)
        p = jnp.exp(_dot_nt(q, kcat) * MLA_SCALE - _lanes(lse_ref[qs, :], t))
        if masked:
            p = jnp.where(_tri(t), p, 0.0)
        ds = (p * (_dot_nt(do, v) - _lanes(dl_sc[qs, :], t))).astype(MXU_DTYPE)
        dv_sc[...] += _dot_tn(p.astype(MXU_DTYPE), do)
        dk_sc[...] += _dot_tn(ds, q)
        dq_sc[qs, :] += _dot_nn(ds, kcat) * MLA_SCALE

    step(ki, True)

    def off_diag(i, carry):
        step(i, False)
        return carry

    lax.fori_loop(ki + 1, nq, off_diag, 0)
    dk = dk_sc[...] * MLA_SCALE
    dkn_ref[...] = dk[:, 0:HEAD].astype(dkn_ref.dtype)
    dkr_ref[...] = dk[:, HEAD:QPAD]
    dv_ref[...] = dv_sc[...].astype(dv_ref.dtype)

    @pl.when(ki == nq - 1)
    def _():
        def emit(i, carry):
            rs = pl.ds(pl.multiple_of(i * t, t), t)
            dq_ref[rs, 0:HEAD] = dq_sc[rs, 0:HEAD].astype(dq_ref.dtype)
            dq_ref[rs, HEAD:QPAD] = _rope_t(dq_sc[rs, HEAD:QPAD], cb[rs, :], sba[rs, :], sbb[rs, :],
                                            ROPE_B // 2).astype(dq_ref.dtype)
            return carry
        lax.fori_loop(0, nq, emit, 0)


def _mla_bwd(qf, kv, kr, dmix, mixed, lse, tabs_b):
    T = qf.shape[0]
    t = min(MLA_T, T)
    head = lambda h, j: (0, h)
    b_half = lambda h, j: (0, NH + h)
    kblk = pl.BlockSpec((t, HEAD), lambda h, j: (j, h))
    return _pcall(
        functools.partial(_mla_bwd_body, t=t), name="mla_bwd",
        grid=(NH, T // t),
        in_specs=[pl.BlockSpec((T, QPAD), head), kblk,
                  pl.BlockSpec((t, HEAD), lambda h, j: (j, 0)),
                  pl.BlockSpec((t, HEAD), lambda h, j: (j, NH + h)),
                  pl.BlockSpec((T, HEAD), b_half), pl.BlockSpec((T, HEAD), b_half),
                  pl.BlockSpec((T, HEAD), head)] + [pl.BlockSpec((T, HEAD), lambda h, j: (0, 0))] * 3,
        out_specs=[pl.BlockSpec((T, QPAD), head), kblk, kblk, kblk],
        out_shape=[jax.ShapeDtypeStruct((T, NH * QPAD), MXU_DTYPE), jax.ShapeDtypeStruct((T, A_W), MXU_DTYPE),
                   jax.ShapeDtypeStruct((T, A_W), MXU_DTYPE), jax.ShapeDtypeStruct((T, A_W), F32)],
        scratch_shapes=[pltpu.VMEM((T, QPAD), F32), pltpu.VMEM((T, HEAD), F32), pltpu.VMEM((t, QPAD), F32),
                        pltpu.VMEM((t, HEAD), F32)],
        compiler_params=pltpu.CompilerParams(dimension_semantics=("parallel", "arbitrary")),
    )(qf, kv, kr, kv, dmix, mixed, lse, *tabs_b)


def _local_step(x, pos, target, g1, g2, gq, gkv, g3, g4,
                in_weights, attn_weights, mlp_prefetch, mlp_weights, mlp_grads_ready, attn_grads_ready):
    T = x.shape[0]
    TR = 256
    mm = functools.partial(_matmul, tm=1024, tn=1024, tk=2048, b_outer=True)
    mm_k = functools.partial(_matmul, tm=1024, tn=1024, tk=2048)
    mm_g = functools.partial(_matmul, tm=512, tn=1024, tk=4096, b_outer=True)
    mm_f = functools.partial(_matmul, tm=512, tn=512, tk=D_FF)

    inv_a = ROPE_THETA ** (-jnp.arange(0, ROT_A, 2, dtype=F32) / ROT_A)
    inv_b = ROPE_THETA ** (-jnp.arange(0, ROPE_B, 2, dtype=F32) / ROPE_B)
    inv = jnp.stack([jnp.concatenate([inv_a, inv_a, jnp.zeros((HEAD - ROT_A,), F32)]),
                     jnp.concatenate([inv_b, inv_b, jnp.zeros((HEAD - ROPE_B,), F32)])])
    inv = jnp.concatenate([inv, jnp.zeros((6, HEAD), F32)], axis=0)
    tabs = _rowwise(_rope_tab_body, [pos], [inv], [(HEAD, F32)] * 6, [], tr=512, name="rope_tables")

    (h,) = _rowwise(_rms_fwd_body, [x], [g1], [(D_MODEL, MXU_DTYPE)], [], tr=TR, name="rms_in")
    w_proj = in_weights(h)
    (proj,) = mm(h, w_proj, dims="nn", out_dtypes=[F32], tn=PROJ_TILE, name="proj_in")
    q, k, v, cqn, ckvn, krope = _rowwise(
        _postproj_body, [proj] + tabs, [gq, gkv],
        [(A_W, F32)] * 3 + [(LORA, MXU_DTYPE)] * 2 + [(HEAD, MXU_DTYPE)], [], tr=TR, name="post_proj")
    mixed, lse_a = _dil_fwd(q, k, v)

    w_uq_p, w_ukv_p, w_out = attn_weights(cqn)

    def q_epi(acc, cb, sba, sbb):
        cols = []
        for hh in range(acc.shape[1] // QPAD):
            lo = hh * QPAD
            cols += [acc[:, lo:lo + HEAD], _rope(acc[:, lo + HEAD:lo + QPAD], cb, sba, sbb, ROPE_B // 2)]
        return (jnp.concatenate(cols, axis=1),)
    (qf,) = mm(cqn, w_uq_p, dims="nn", out_dtypes=[MXU_DTYPE], name="q_up", epi=q_epi, row_extras=tuple(tabs[3:]))
    (kv,) = mm(ckvn, w_ukv_p, dims="nn", out_dtypes=[MXU_DTYPE], name="kv_up")
    mixed, lse_b = _mla_fwd(qf, kv, krope, mixed)
    mlp_prefetch(mixed)

    (o,) = mm(mixed, w_out, dims="nn", out_dtypes=[F32], name="out_proj")
    x1, h2 = _rowwise(_mid_body, [x, o], [g2, g3], [(D_MODEL, F32), (D_MODEL, MXU_DTYPE)], [], tr=TR, name="mid_norm")

    w_up, w_down = mlp_weights(h2)

    def up_epi(acc):
        r = jnp.maximum(acc, 0.0)
        return r * r, r
    u, r = mm(h2, w_up, dims="nn", out_dtypes=[MXU_DTYPE, MXU_DTYPE], name="mlp_up", epi=up_epi, b_shards=N_CHIPS)
    (dn,) = mm_f(u, w_down, dims="nn", out_dtypes=[F32], name="mlp_down")
    dy, dd, loss8, dg4 = _rowwise(_loss_body, [x1, dn, target], [g4], [(D_MODEL, F32), (D_MODEL, MXU_DTYPE)],
                                  [(8, HEAD), (8, D_MODEL)], tr=TR, name="loss_head")

    def dup_epi(acc, rr):
        return (acc * (2.0 * rr.astype(F32)),)
    (dup,) = mm(dd, w_down, dims="nt", out_dtypes=[MXU_DTYPE], name="d_up", epi=dup_epi, extras=(r,))
    (gw_down,) = mm_g(u, dd, dims="tn", out_dtypes=[WIRE_DTYPE], name="gw_down")
    (dh2,) = mm_k(dup, w_up, dims="nt", out_dtypes=[F32], name="d_h2", b_shards=N_CHIPS)
    (gw_up,) = mm_g(h2, dup, dims="tn", out_dtypes=[WIRE_DTYPE], name="gw_up", out_shards=N_CHIPS)
    g2 = g2 + mlp_grads_ready(gw_up, gw_down)
    dx1, do, dg3, dg2 = _rowwise(_bmid_body, [dy, dh2, x1, o], [g2, g3], [(D_MODEL, F32), (D_MODEL, MXU_DTYPE)],
                                 [(8, D_MODEL), (8, D_MODEL)], tr=TR, name="bwd_mid")
    (dmix,) = mm(do, w_out, dims="nt", out_dtypes=[F32], name="d_mixed")
    (gw_out,) = mm_g(mixed, do, dims="tn", out_dtypes=[WIRE_DTYPE], name="gw_out")

    dq_pad, dkn, dvb, dkr = _mla_bwd(qf, kv, krope, dmix, mixed, lse_b, tabs[3:])
    (dcqn,) = mm(dq_pad, w_uq_p, dims="nt", out_dtypes=[F32], name="d_cq")
    (gw_uq_p,) = mm_g(cqn, dq_pad, dims="tn", out_dtypes=[WIRE_DTYPE], name="gw_uq")
    dkv = jnp.concatenate([dkn, dvb], axis=1)
    (dckvn,) = mm(dkv, w_ukv_p, dims="nt", out_dtypes=[F32], name="d_ckv")
    (gw_ukv_p,) = mm_g(ckvn, dkv, dims="tn", out_dtypes=[WIRE_DTYPE], name="gw_ukv")
    gq = gq + attn_grads_ready(gw_out, gw_uq_p, gw_ukv_p)

    dq_a, dk_a, dv_a = _dil_bwd(q, k, v, dmix, mixed, lse_a)
    dproj, dgq, dgkv = _rowwise(
        _dproj_body, [dq_a, dk_a, dv_a, dcqn, dckvn, proj, dkr] + tabs, [gq, gkv],
        [(PROJ_COLS, MXU_DTYPE)], [(8, LORA), (8, LORA)], tr=TR, name="d_proj")
    (dh,) = mm_k(dproj, w_proj, dims="nt", out_dtypes=[F32], tk=PROJ_TILE, name="d_h")
    (gw_proj,) = mm_g(h, dproj, dims="tn", out_dtypes=[WIRE_DTYPE], tn=PROJ_TILE, name="gw_in")
    dx, dg1 = _rowwise(_bin_body, [dx1, dh, x], [g1], [(D_MODEL, F32)], [(8, D_MODEL)], tr=TR, name="bwd_in")

    small = jnp.concatenate([dg1, dg2, dgq, dgkv, dg3, dg4, loss8], axis=1)
    return dx, gw_proj, small


def _place():
    x, y, c = lax.axis_index("x"), lax.axis_index("y"), lax.axis_index("c")
    chips = [(1 - x, y), (x, 1 - y), (1 - x, 1 - y)]
    return x, y, c, chips


def _cast_place_body(me_ref, w_ref, o_ref):
    o_ref[...] = w_ref[...].astype(o_ref.dtype)


def _cast_place(me_arr, w, name):
    rows, cols = w.shape
    tr = min(rows, 256)
    grid_spec = pltpu.PrefetchScalarGridSpec(
        num_scalar_prefetch=1, grid=(rows // tr,),
        in_specs=[pl.BlockSpec((tr, cols), lambda i, me: (i, 0))],
        out_specs=pl.BlockSpec((None, tr, cols), lambda i, me: (me[0], i, 0)))
    return _pcall(
        _cast_place_body, name=name, grid_spec=grid_spec,
        out_shape=jax.ShapeDtypeStruct((N_CHIPS, rows, cols), WIRE_DTYPE),
        compiler_params=pltpu.CompilerParams(dimension_semantics=("parallel",)),
    )(me_arr, w)


HBM = pl.BlockSpec(memory_space=pltpu.HBM)
SEM = pl.BlockSpec(memory_space=pltpu.SEMAPHORE)
EFFECT = pltpu.SideEffectType.DATAFLOW_SIDE_EFFECTING


def _in_hbm(a):
    return pltpu.with_memory_space_constraint(a, pltpu.HBM)


def _ag_descs(bufs, send_sems, recv_sems):
    x, y, c, chips = _place()
    me = 2 * x + y
    out = []
    for w, buf in enumerate(bufs):
        half = buf.shape[1] // 2
        rows = pl.ds(pl.multiple_of(c * half, 16), half)
        mine = buf.at[me, rows]
        for j, (px, py) in enumerate(chips):
            landed = buf.at[2 * px + py, rows]
            mk = lambda ref, w=w, j=j, px=px, py=py: pltpu.make_async_remote_copy(
                src_ref=ref, dst_ref=ref, send_sem=send_sems.at[w * 3 + j], recv_sem=recv_sems.at[w * 3 + j],
                device_id=(px, py, c), device_id_type=MESH)
            out.append((mk(mine), mk(landed)))
    return out


def _ag_start_body(*refs, n_w):
    bufs = refs[:n_w]
    send_sems, recv_sems = refs[-n_w - 3], refs[-n_w - 2]
    token = refs[-1]
    for send, _ in _ag_descs(bufs, send_sems, recv_sems):
        send.start()
    token[...] = jnp.zeros_like(token)


def _ag_start(placed, after, tag):
    n_w = len(placed)
    after = [] if after is None else [after]
    res = _pcall(
        functools.partial(_ag_start_body, n_w=n_w), name="weight_allgather_start_" + tag,
        in_specs=[HBM] * n_w + [ANY] * len(after),
        out_specs=[SEM, SEM] + [HBM] * n_w + [pl.BlockSpec(memory_space=pltpu.VMEM)],
        out_shape=[pltpu.SemaphoreType.DMA((3 * n_w,)), pltpu.SemaphoreType.DMA((3 * n_w,))]
        + [pltpu.HBM(p.shape, p.dtype) for p in placed] + [jax.ShapeDtypeStruct((8, HEAD), F32)],
        input_output_aliases={w: 2 + w for w in range(n_w)},
        compiler_params=pltpu.CompilerParams(has_side_effects=EFFECT),
    )(*[_in_hbm(p) for p in placed], *after)
    return res[0], res[1], list(res[2:2 + n_w]), res[-1]


def _ag_wait_body(*refs, n_w):
    bufs = refs[:n_w]
    send_sems, recv_sems = refs[n_w], refs[n_w + 1]
    for send, recv in _ag_descs(bufs, send_sems, recv_sems):
        send.wait_send()
        recv.wait_recv()


def _ag_wait(send_sems, recv_sems, bufs, after, tag):
    n_w = len(bufs)
    return list(_pcall(
        functools.partial(_ag_wait_body, n_w=n_w), name="weight_allgather_wait_" + tag,
        in_specs=[HBM] * n_w + [SEM, SEM, ANY], out_specs=[HBM] * n_w,
        out_shape=[pltpu.HBM(b.shape, b.dtype) for b in bufs],
        input_output_aliases={w: w for w in range(n_w)},
        compiler_params=pltpu.CompilerParams(has_side_effects=EFFECT),
    )(*bufs, send_sems, recv_sems, after))


def _fw_descs(bufs, send_sems, recv_sems):
    x, y, c, chips = _place()
    out = []
    for w, buf in enumerate(bufs):
        half = buf.shape[1] // 2
        for j, (px, py) in enumerate(chips):
            def mk(which, w=w, j=j, buf=buf, half=half, px=px, py=py):
                ref = buf.at[2 * px + py, pl.ds(pl.multiple_of(which * half, 16), half)]
                return pltpu.make_async_remote_copy(
                    src_ref=ref, dst_ref=ref, send_sem=send_sems.at[w * 3 + j], recv_sem=recv_sems.at[w * 3 + j],
                    device_id=(x, y, 1 - c), device_id_type=MESH)
            out.append((mk(c), mk(1 - c)))
    return out


def _fw_start_body(*refs, n_w):
    bufs = refs[:n_w]
    send_sems, recv_sems = refs[n_w], refs[n_w + 1]
    token = refs[-1]
    for send, _ in _fw_descs(bufs, send_sems, recv_sems):
        send.start()
    token[...] = jnp.zeros_like(token)


def _fw_start(bufs, tag):
    n_w = len(bufs)
    res = _pcall(
        functools.partial(_fw_start_body, n_w=n_w), name="weight_allgather_forward_start_" + tag,
        in_specs=[HBM] * n_w,
        out_specs=[SEM, SEM] + [HBM] * n_w + [pl.BlockSpec(memory_space=pltpu.VMEM)],
        out_shape=[pltpu.SemaphoreType.DMA((3 * n_w,)), pltpu.SemaphoreType.DMA((3 * n_w,))]
        + [pltpu.HBM(b.shape, b.dtype) for b in bufs] + [jax.ShapeDtypeStruct((8, HEAD), F32)],
        input_output_aliases={w: 2 + w for w in range(n_w)},
        compiler_params=pltpu.CompilerParams(has_side_effects=EFFECT),
    )(*bufs)
    return res[0], res[1], list(res[2:2 + n_w]), res[-1]


def _fw_wait_body(*refs, n_w):
    bufs = refs[:n_w]
    send_sems, recv_sems = refs[n_w], refs[n_w + 1]
    for send, back in _fw_descs(bufs, send_sems, recv_sems):
        send.wait_send()
        back.wait_recv()


def _fw_wait(send_sems, recv_sems, bufs, after, tag):
    n_w = len(bufs)
    return list(_pcall(
        functools.partial(_fw_wait_body, n_w=n_w), name="weight_allgather_forward_wait_" + tag,
        in_specs=[HBM] * n_w + [SEM, SEM, ANY], out_specs=[HBM] * n_w,
        out_shape=[pltpu.HBM(b.shape, b.dtype) for b in bufs],
        input_output_aliases={w: w for w in range(n_w)},
        compiler_params=pltpu.CompilerParams(has_side_effects=EFFECT),
    )(*bufs, send_sems, recv_sems, after))


def _ag_forward_body(*refs, n_w):
    bufs = refs[n_w:2 * n_w]
    send_sems, recv_sems = refs[2 * n_w:]
    x, y, c, chips = _place()
    fwds = []
    for w, buf in enumerate(bufs):
        half = buf.shape[1] // 2
        for j, (px, py) in enumerate(chips):
            def piece(which, buf=buf, half=half, px=px, py=py):
                return buf.at[2 * px + py, pl.ds(pl.multiple_of(which * half, 16), half)]
            mk = lambda ref, w=w, j=j: pltpu.make_async_remote_copy(
                src_ref=ref, dst_ref=ref, send_sem=send_sems.at[w * 3 + j], recv_sem=recv_sems.at[w * 3 + j],
                device_id=(x, y, 1 - c), device_id_type=MESH)
            fw = mk(piece(c))
            fw.start()
            fwds.append((fw, mk(piece(1 - c))))
    for fw, back in fwds:
        back.wait_recv()
        fw.wait_send()


def _ag_forward(bufs, tag):
    n_w = len(bufs)
    return list(_pcall(
        functools.partial(_ag_forward_body, n_w=n_w), name="weight_allgather_forward_" + tag,
        in_specs=[ANY] * n_w, out_specs=[ANY] * n_w,
        out_shape=[jax.ShapeDtypeStruct(b.shape, b.dtype) for b in bufs],
        input_output_aliases={w: w for w in range(n_w)},
        scratch_shapes=[pltpu.SemaphoreType.DMA((3 * n_w,))] * 2,
    )(*bufs))


def _sc_descs(ins, outs, send_sems, recv_sems):
    x, y, c, chips = _place()
    me = 2 * x + y
    out = []
    for w in range(len(ins)):
        for j, (px, py) in enumerate(chips):
            out.append(pltpu.make_async_remote_copy(
                src_ref=ins[w].at[2 * px + py], dst_ref=outs[w].at[me],
                send_sem=send_sems.at[w * 3 + j], recv_sem=recv_sems.at[w * 3 + j],
                device_id=(px, py, c), device_id_type=MESH))
    return out


def _scatter_start_body(*refs, n_w):
    ins, lands = refs[:n_w], refs[n_w:2 * n_w]
    send_sems, recv_sems = refs[-2 * n_w - 3], refs[-2 * n_w - 2]
    token = refs[-1]
    for cp in _sc_descs(ins, lands, send_sems, recv_sems):
        cp.start()
    token[...] = jnp.zeros_like(token)


def _scatter_start(parts, after, tag):
    n_w = len(parts)
    lands = [lax.empty(p.shape, p.dtype) for p in parts]
    after = [] if after is None else [after]
    res = _pcall(
        functools.partial(_scatter_start_body, n_w=n_w), name="grad_scatter_start_" + tag,
        in_specs=[HBM] * (2 * n_w) + [ANY] * len(after),
        out_specs=[SEM, SEM] + [HBM] * (2 * n_w) + [pl.BlockSpec(memory_space=pltpu.VMEM)],
        out_shape=[pltpu.SemaphoreType.DMA((3 * n_w,)), pltpu.SemaphoreType.DMA((3 * n_w,))]
        + [pltpu.HBM(p.shape, p.dtype) for p in parts] * 2 + [jax.ShapeDtypeStruct((8, HEAD), F32)],
        input_output_aliases={i: 2 + i for i in range(2 * n_w)},
        compiler_params=pltpu.CompilerParams(has_side_effects=EFFECT),
    )(*[_in_hbm(p) for p in parts], *[_in_hbm(l) for l in lands], *after)
    return res[0], res[1], list(res[2:2 + n_w]), list(res[2 + n_w:2 + 2 * n_w]), res[-1]


def _scatter_wait_body(*refs, n_w):
    ins, lands = refs[:n_w], refs[n_w:2 * n_w]
    send_sems, recv_sems = refs[2 * n_w], refs[2 * n_w + 1]
    for cp in _sc_descs(ins, lands, send_sems, recv_sems):
        cp.wait_send()
        cp.wait_recv()


def _scatter_wait(send_sems, recv_sems, parts, lands, after, tag):
    n_w = len(parts)
    res = _pcall(
        functools.partial(_scatter_wait_body, n_w=n_w), name="grad_scatter_wait_" + tag,
        in_specs=[HBM] * (2 * n_w) + [SEM, SEM, ANY], out_specs=[HBM] * (2 * n_w),
        out_shape=[pltpu.HBM(p.shape, p.dtype) for p in parts] * 2,
        input_output_aliases={i: i for i in range(2 * n_w)},
        compiler_params=pltpu.CompilerParams(has_side_effects=EFFECT),
    )(*parts, *lands, send_sems, recv_sems, after)
    return list(res[:n_w]), list(res[n_w:])


def _pair_send_body(*refs, n_w):
    ins, outs = refs[:n_w], refs[n_w:2 * n_w]
    send_sems, recv_sems = refs[2 * n_w:]
    x, y, c, _ = _place()
    cps = []
    for w in range(n_w):
        cp = pltpu.make_async_remote_copy(
            src_ref=ins[w].at[:, 1 - c], dst_ref=outs[w],
            send_sem=send_sems.at[w], recv_sem=recv_sems.at[w],
            device_id=(x, y, 1 - c), device_id_type=MESH)
        cp.start()
        cps.append(cp)
    for cp in cps:
        cp.wait()


def _pair_send(grads4, tag):
    n_w = len(grads4)
    return _pcall(
        functools.partial(_pair_send_body, n_w=n_w), name="grad_pair_exchange_" + tag,
        in_specs=[ANY] * n_w, out_specs=[ANY] * n_w,
        out_shape=[jax.ShapeDtypeStruct((g.shape[0],) + g.shape[2:], g.dtype) for g in grads4],
        scratch_shapes=[pltpu.SemaphoreType.DMA((n_w,))] * 2,
    )(*grads4)


def _pair_add_body(c_ref, mine_ref, theirs_ref, o_ref):
    o_ref[...] = (mine_ref[...].astype(F32) + theirs_ref[...].astype(F32)).astype(o_ref.dtype)


def _pair_add(c_arr, g4, recv, name):
    _, _, hr, cols = g4.shape
    tr = min(hr, 256)
    grid_spec = pltpu.PrefetchScalarGridSpec(
        num_scalar_prefetch=1, grid=(N_CHIPS, hr // tr),
        in_specs=[pl.BlockSpec((None, None, tr, cols), lambda s, i, c: (s, c[0], i, 0)),
                  pl.BlockSpec((None, tr, cols), lambda s, i, c: (s, i, 0))],
        out_specs=pl.BlockSpec((None, tr, cols), lambda s, i, c: (s, i, 0)))
    return _pcall(
        _pair_add_body, name=name, grid_spec=grid_spec,
        out_shape=jax.ShapeDtypeStruct(recv.shape, recv.dtype),
        compiler_params=pltpu.CompilerParams(dimension_semantics=("parallel", "parallel")),
    )(c_arr, g4, recv)


def _sum4_body(me_ref, p_ref, l0, l1, l2, l3, o_ref):
    me = me_ref[0]
    t = [jnp.where(me == j, p_ref[...], l[...]).astype(F32) for j, l in enumerate((l0, l1, l2, l3))]
    o_ref[...] = ((t[0] + t[1]) + t[2]) + t[3]


def _sum4(me_arr, part, landed, name):
    _, hr, cols = part.shape
    tr = min(hr, 256)

    def slot(j):
        return lambda i, me: (jnp.where(me[0] == j, (j + 1) % N_CHIPS, j), i, 0)

    grid_spec = pltpu.PrefetchScalarGridSpec(
        num_scalar_prefetch=1, grid=(hr // tr,),
        in_specs=[pl.BlockSpec((None, tr, cols), lambda i, me: (me[0], i, 0))]
        + [pl.BlockSpec((None, tr, cols), slot(j)) for j in range(N_CHIPS)],
        out_specs=pl.BlockSpec((tr, cols), lambda i, me: (i, 0)))
    return _pcall(
        _sum4_body, name=name, grid_spec=grid_spec,
        out_shape=jax.ShapeDtypeStruct((hr, cols), F32),
        compiler_params=pltpu.CompilerParams(dimension_semantics=("parallel",)),
    )(me_arr, part, landed, landed, landed, landed)


def _pair_swap_body(*refs, n_w):
    ins, outs = refs[:n_w], refs[n_w:2 * n_w]
    send_sems, recv_sems = refs[2 * n_w:]
    x, y, c, _ = _place()
    todo = []
    for w in range(n_w):
        cp = pltpu.make_async_remote_copy(
            src_ref=ins[w], dst_ref=outs[w],
            send_sem=send_sems.at[w], recv_sem=recv_sems.at[w],
            device_id=(x, y, 1 - c), device_id_type=MESH)
        cp.start()
        todo.append(cp)
    for t in todo:
        t.wait()


def _pair_swap(halves, tag):
    n_w = len(halves)
    return _pcall(
        functools.partial(_pair_swap_body, n_w=n_w), name="grad_pair_swap_" + tag,
        in_specs=[ANY] * n_w, out_specs=[ANY] * n_w,
        out_shape=[jax.ShapeDtypeStruct(h.shape, h.dtype) for h in halves],
        scratch_shapes=[pltpu.SemaphoreType.DMA((n_w,))] * 2,
    )(*halves)


def _small_gather_body(x_ref, out_ref, send_sems, recv_sems, local_sem):
    m_per = x_ref.shape[0]
    x, y, c, chips = _place()
    me, sibling = (x, y, c), (x, y, 1 - c)

    def rows(px, py, pc):
        return out_ref.at[pl.ds((4 * px + 2 * py + pc) * m_per, m_per), :]

    def copy(k, block, to, src=None):
        return pltpu.make_async_remote_copy(
            src_ref=rows(*block) if src is None else src, dst_ref=rows(*block),
            send_sem=send_sems.at[k], recv_sem=recv_sems.at[k], device_id=to, device_id_type=MESH)

    mine = pltpu.make_async_copy(x_ref, rows(*me), local_sem)
    mine.start()
    first = [copy(0, me, sibling, src=x_ref)]
    first += [copy(1 + j, me, (*chip, c), src=x_ref) for j, chip in enumerate(chips)]
    for cp in first:
        cp.start()
    passed = [copy(4 + j, (*chip, c), sibling) for j, chip in enumerate(chips)]
    for j, chip in enumerate(chips):
        copy(1 + j, (*chip, c), me).wait_recv()
        passed[j].start()
    copy(0, sibling, me).wait_recv()
    for j, chip in enumerate(chips):
        copy(4 + j, (*chip, 1 - c), me).wait_recv()
    for cp in first + passed:
        cp.wait_send()
    mine.wait()


def _small_gather(small):
    m_per, n = small.shape
    return _pcall(
        _small_gather_body, name="small_allgather",
        out_shape=jax.ShapeDtypeStruct((N_DEV * m_per, n), small.dtype),
        in_specs=[pl.BlockSpec(memory_space=pltpu.VMEM)],
        out_specs=pl.BlockSpec(memory_space=pltpu.VMEM),
        scratch_shapes=[pltpu.SemaphoreType.DMA((7,)), pltpu.SemaphoreType.DMA((7,)), pltpu.SemaphoreType.DMA],
    )(small)


def _adamw(w, g, m, v):
    m = ADAM_B1 * m + (1.0 - ADAM_B1) * g
    v = ADAM_B2 * v + (1.0 - ADAM_B2) * (g * g)
    m_hat = m / (1.0 - ADAM_B1 ** ADAM_STEP)
    v_hat = v / (1.0 - ADAM_B2 ** ADAM_STEP)
    delta = -ADAM_LR * (m_hat / (jnp.sqrt(v_hat) + ADAM_EPS) + ADAM_WD * w)
    return delta, m, v


def _adamw_body(c_ref, w_ref, own_ref, sib_ref, m_ref, v_ref, g_ref, d_ref, nm_ref, nv_ref, *, nh):
    mine = (pl.program_id(0) // nh) == c_ref[0]
    g = jnp.where(mine, own_ref[...], sib_ref[...])
    g_ref[...] = g
    d, m, v = _adamw(w_ref[...], g, m_ref[...], v_ref[...])
    d_ref[...] = d
    nm_ref[...] = m
    nv_ref[...] = v


def _adamw_call(c_arr, w, own, sib, m, v, name):
    rows, cols = w.shape
    tr = min(rows // 2, 256)
    nh = (rows // 2) // tr
    full = pl.BlockSpec((tr, cols), lambda i, c: (i, 0))
    own_spec = pl.BlockSpec((tr, cols), lambda i, c: (jnp.clip(i - c[0] * nh, 0, nh - 1), 0))
    sib_spec = pl.BlockSpec((tr, cols), lambda i, c: (jnp.clip(i - (1 - c[0]) * nh, 0, nh - 1), 0))
    grid_spec = pltpu.PrefetchScalarGridSpec(
        num_scalar_prefetch=1, grid=(rows // tr,),
        in_specs=[full, own_spec, sib_spec, full, full], out_specs=[full] * 4)
    return _pcall(
        functools.partial(_adamw_body, nh=nh), name=name, grid_spec=grid_spec,
        out_shape=[jax.ShapeDtypeStruct(w.shape, F32)] * 4,
        compiler_params=pltpu.CompilerParams(dimension_semantics=("parallel",)),
    )(c_arr, w, own, sib, m, v)


def _small_update_body(gath_ref, w_ref, m_ref, v_ref, g_ref, d_ref, nm_ref, nv_ref, loss_ref, *, n_gain):
    tot = gath_ref[0:1, :]
    for i in range(1, gath_ref.shape[0]):
        tot = tot + gath_ref[i:i + 1, :]
    g = tot[:, 0:n_gain]
    g_ref[...] = g
    d, m, v = _adamw(w_ref[...], g, m_ref[...], v_ref[...])
    d_ref[...] = d
    nm_ref[...] = m
    nv_ref[...] = v
    loss_ref[...] = (0.5 / D_MODEL) * jnp.sum(tot[:, n_gain:n_gain + HEAD], axis=1, keepdims=True) * jnp.ones((1, HEAD), F32)


def _small_update(gath, w, m, v):
    n_gain = w.shape[1]
    vm = pl.BlockSpec(memory_space=pltpu.VMEM)
    return _pcall(
        functools.partial(_small_update_body, n_gain=n_gain), name="gain_update",
        in_specs=[vm] * 4, out_specs=[vm] * 5,
        out_shape=[jax.ShapeDtypeStruct((1, n_gain), F32)] * 4 + [jax.ShapeDtypeStruct((1, HEAD), F32)],
    )(gath, w, m, v)


def kernel(x, positions, norm_attn_pre, norm_attn_post, w_in, q_latent_norm, kv_latent_norm, w_uq, w_ukv, w_out, norm_mlp_pre, norm_mlp_post, w_up, w_down, loss_target, m_norm_attn_pre, m_norm_attn_post, m_w_in, m_q_latent_norm, m_kv_latent_norm, m_w_uq, m_w_ukv, m_w_out, m_norm_mlp_pre, m_norm_mlp_post, m_w_up, m_w_down, v_norm_attn_pre, v_norm_attn_post, v_w_in, v_q_latent_norm, v_kv_latent_norm, v_w_uq, v_w_ukv, v_w_out, v_norm_mlp_pre, v_norm_mlp_post, v_w_up, v_w_down):
    T = x.shape[1]
    c_arr = lax.axis_index("c").astype(jnp.int32).reshape(1)
    me_arr = (2 * lax.axis_index("x") + lax.axis_index("y")).astype(jnp.int32).reshape(1)
    names = ["w_in", "w_uq", "w_ukv", "w_out", "w_up", "w_down"]

    mats = [w_in[0], w_uq[0], w_ukv[0], w_out[0], w_up[0], w_down[0]]
    in_send, in_recv, in_bufs, in_started = _ag_start([_cast_place(me_arr, mats[0], "cast_w_in")], None, "in")
    placed = [_cast_place(me_arr, w, "cast_" + n) for w, n in zip(mats[1:], names[1:])]
    att_send, att_recv, att_bufs, att_started = _ag_start(placed[:3], in_started, "attn")
    mlp_send, mlp_recv, mlp_bufs, started = _ag_start(placed[3:], att_started, "mlp")

    col_major = lambda g: jnp.transpose(g, (1, 0, 2)).reshape(g.shape[1], N_CHIPS * g.shape[2])
    cast = lambda a: a.astype(MXU_DTYPE)
    to_shards = lambda g: jnp.transpose(g.reshape(g.shape[0], N_CHIPS, g.shape[1] // N_CHIPS), (1, 0, 2))
    halved = lambda g: g.reshape(N_CHIPS, 2, g.shape[1] // 2, g.shape[2])

    def pair_sum(full4, ns):
        from_sib = _pair_send(full4, "_".join(ns))
        return [_pair_add(c_arr, g4, r, "pair_add_" + n) for g4, r, n in zip(full4, from_sib, ns)]

    def in_weights(after):
        (win_g,) = _ag_forward(_ag_wait(in_send, in_recv, in_bufs, after, "in"), "in")
        return cast(jnp.pad(col_major(win_g), ((0, 0), (0, PROJ_COLS - IN_COLS))))

    def attn_weights(after):
        wuq_g, wukv_g, wout_g = _ag_forward(_ag_wait(att_send, att_recv, att_bufs, after, "attn"), "attn")
        wuq_full = col_major(wuq_g).reshape(LORA, NH, HEAD + ROPE_B)
        w_uq_p = jnp.pad(wuq_full, ((0, 0), (0, 0), (0, QPAD - HEAD - ROPE_B))).reshape(LORA, NH * QPAD)
        w_ukv_p = col_major(wukv_g).reshape(LORA, NH, 2, HEAD).transpose(0, 2, 1, 3).reshape(LORA, 2 * A_W)
        return cast(w_uq_p), cast(w_ukv_p), cast(wout_g.reshape(2 * A_W, D_MODEL))

    in_flight = {}

    def mlp_prefetch(after):
        in_flight["fw"] = _fw_start(_ag_wait(mlp_send, mlp_recv, mlp_bufs, after, "mlp"), "mlp")

    def mlp_weights(after):
        f_send, f_recv, bufs, _ = in_flight["fw"]
        wup_g, wdown_g = _fw_wait(f_send, f_recv, bufs, after, "mlp")
        return cast(wup_g), cast(wdown_g.reshape(D_FF, D_MODEL))

    def mlp_grads_ready(gw_up, gw_down):
        parts = pair_sum([halved(gw_up), halved(gw_down.reshape(N_CHIPS, D_MODEL, D_MODEL))], names[4:])
        in_flight["mlp"] = _scatter_start(parts, started, "mlp")
        return in_flight["mlp"][-1][0:1, 0:1]

    def attn_grads_ready(gw_out, gw_uq_p, gw_ukv_p):
        gw_uq = to_shards(gw_uq_p.reshape(LORA, NH, QPAD)[:, :, :HEAD + ROPE_B].reshape(LORA, NH * (HEAD + ROPE_B)))
        gw_ukv = to_shards(gw_ukv_p.reshape(LORA, 2, NH, HEAD).transpose(0, 2, 1, 3).reshape(LORA, 2 * A_W))
        parts = pair_sum([halved(g) for g in (gw_uq, gw_ukv, gw_out.reshape(N_CHIPS, LORA, D_MODEL))], names[1:4])
        in_flight["attn"] = _scatter_start(parts, in_flight["mlp"][-1], "attn")
        return in_flight["attn"][-1][0:1, 0:1]

    dx, gw_proj, small = _local_step(
        x[0], positions[0].astype(F32).reshape(T, 1), loss_target[0],
        norm_attn_pre + started[0:1, 0:1], norm_attn_post, q_latent_norm, kv_latent_norm, norm_mlp_pre, norm_mlp_post,
        in_weights, attn_weights, mlp_prefetch, mlp_weights, mlp_grads_ready, attn_grads_ready)

    ms = [m_w_in[0], m_w_uq[0], m_w_ukv[0], m_w_out[0], m_w_up[0], m_w_down[0]]
    vs = [v_w_in[0], v_w_uq[0], v_w_ukv[0], v_w_out[0], v_w_up[0], v_w_down[0]]

    def finish(parts, landed, lo, hi, tag):
        halves = [_sum4(me_arr, p, l, "chip_sum_" + n) for p, l, n in zip(parts, landed, names[lo:hi])]
        from_sib2 = _pair_swap(halves, tag)
        return [_adamw_call(c_arr, w, own, sib, m, v, "adamw_" + n)
                for w, own, sib, m, v, n in zip(mats[lo:hi], halves, from_sib2, ms[lo:hi], vs[lo:hi], names[lo:hi])]

    gw_in = to_shards(gw_proj[:, :IN_COLS])
    i_send, i_recv, parts_in, lands_in, in_going = _scatter_start(pair_sum([halved(gw_in)], names[:1]), None, "in")
    a_send, a_recv, parts_att, lands_att, _ = in_flight["attn"]
    parts_att, landed_att = _scatter_wait(a_send, a_recv, parts_att, lands_att, in_going, "attn")
    m_send, m_recv, parts_mlp, lands_mlp, _ = in_flight["mlp"]
    parts_mlp, landed_mlp = _scatter_wait(m_send, m_recv, parts_mlp, lands_mlp, landed_att[0], "mlp")
    upd_rest = finish(parts_att + parts_mlp, landed_att + landed_mlp, 1, 6, "rest")
    parts_in, landed_in = _scatter_wait(i_send, i_recv, parts_in, lands_in, upd_rest[-1][0], "in")
    upd = finish(parts_in, landed_in, 0, 1, "in") + upd_rest
    grads = [u[0] for u in upd]

    gath = _small_gather(small)
    gains = [norm_attn_pre, norm_attn_post, q_latent_norm, kv_latent_norm, norm_mlp_pre, norm_mlp_post]
    gm = [m_norm_attn_pre, m_norm_attn_post, m_q_latent_norm, m_kv_latent_norm, m_norm_mlp_pre, m_norm_mlp_post]
    gv = [v_norm_attn_pre, v_norm_attn_post, v_q_latent_norm, v_kv_latent_norm, v_norm_mlp_pre, v_norm_mlp_post]
    cat = lambda xs: jnp.concatenate(xs, axis=1)
    g_s, d_s, m_s, v_s, loss_v = _small_update(gath, cat(gains), cat(gm), cat(gv))
    widths = [a.shape[1] for a in gains]
    offs = [sum(widths[:i]) for i in range(len(widths))]
    split = lambda a: [a[:, o:o + w] for o, w in zip(offs, widths)]
    g_gain, d_gain, m_gain, v_gain = split(g_s), split(d_s), split(m_s), split(v_s)

    def ordered(gain_list, mat_list):
        gl, ml = gain_list, [a[None] for a in mat_list]
        return [gl[0], gl[1], ml[0], gl[2], gl[3], ml[1], ml[2], ml[3], gl[4], gl[5], ml[4], ml[5]]

    loss = loss_v[0, 0]
    return (loss, dx[None],
            *ordered(g_gain, grads),
            *ordered(d_gain, [u[1] for u in upd]),
            *ordered(m_gain, [u[2] for u in upd]),
            *ordered(v_gain, [u[3] for u in upd]))
```

```python
import functools

import jax
import jax.numpy as jnp
from jax import lax
from jax.experimental import pallas as pl
from jax.experimental.pallas import tpu as pltpu

F32 = jnp.float32
BF16 = jnp.bfloat16
MXU_DTYPE = jnp.bfloat16
WIRE_DTYPE = jnp.bfloat16

D_MODEL = 2048
HEAD = 128
NH = 8
A_W = NH * HEAD
LORA = 512
ROPE_B = 64
QPAD = 256
MAIN_COLS = 3 * A_W + 2 * LORA
IN_COLS = MAIN_COLS + ROPE_B
PROJ_COLS = MAIN_COLS + HEAD
PROJ_TILE = PROJ_COLS // 3
D_FF = 4 * D_MODEL
DIL = (1, 4, 16)
ROT_A = 32
ROPE_THETA = 500000.0
EPS = 1e-6
NEG = -1e30
N_CHIPS = 4
N_DEV = 8

ADAM_LR = 0.001
ADAM_B1 = 0.9
ADAM_B2 = 0.999
ADAM_EPS = 1e-08
ADAM_WD = 0.01
ADAM_STEP = 10

MESH = pl.DeviceIdType.MESH
ANY = pl.BlockSpec(memory_space=pl.ANY)


def _pcall(body, **kw):
    return pl.pallas_call(body, **kw)


_DIMS = {
    "nn": (((1,), (0,)), ((), ())),
    "nt": (((1,), (1,)), ((), ())),
    "tn": (((0,), (0,)), ((), ())),
}


def _mm_body(*refs, dims, nk, epi, n_extra, n_out):
    a_ref, b_ref = refs[0], refs[1]
    extra = refs[2:2 + n_extra]
    outs = refs[2 + n_extra:2 + n_extra + n_out]
    part = lax.dot_general(a_ref[...], b_ref[...], _DIMS[dims], preferred_element_type=F32)

    def finish(acc):
        res = epi(acc, *[r[...] for r in extra]) if epi is not None else (acc,)
        for o_ref, o in zip(outs, res):
            o_ref[...] = o.astype(o_ref.dtype)

    if nk == 1:
        finish(part)
        return
    acc_ref = refs[-1]
    k = pl.program_id(2)

    @pl.when(k == 0)
    def _():
        acc_ref[...] = part

    @pl.when(k > 0)
    def _():
        acc_ref[...] += part

    @pl.when(k == nk - 1)
    def _():
        finish(acc_ref[...])


def _matmul(a, b, *, dims, out_dtypes, tm, tn, tk, name, epi=None, extras=(), row_extras=(), b_outer=False,
            b_shards=0, out_shards=0):
    if b_shards:
        assert dims in ("nn", "nt") and b.shape[0] == b_shards
        b2 = (b.shape[1], b_shards * b.shape[2])
    else:
        b2 = b.shape
    if dims == "nn":
        (M, K), (K2, N) = a.shape, b2
    elif dims == "nt":
        (M, K), (N, K2) = a.shape, b2
    else:
        (K, M), (K2, N) = a.shape, b2
    assert K == K2, (a.shape, b.shape, dims)
    tm, tn, tk = min(tm, M), min(tn, N), min(tk, K)
    assert M % tm == 0 and N % tn == 0 and K % tk == 0, (name, M, N, K, tm, tn, tk)
    nk = K // tk

    def at(f):
        if b_outer:
            return lambda j, i, k: f(i, j, k)
        return f

    a_spec = {"nn": pl.BlockSpec((tm, tk), at(lambda i, j, k: (i, k))),
              "nt": pl.BlockSpec((tm, tk), at(lambda i, j, k: (i, k))),
              "tn": pl.BlockSpec((tk, tm), at(lambda i, j, k: (k, i)))}[dims]
    b_spec = {"nn": pl.BlockSpec((tk, tn), at(lambda i, j, k: (k, j))),
              "nt": pl.BlockSpec((tn, tk), at(lambda i, j, k: (j, k))),
              "tn": pl.BlockSpec((tk, tn), at(lambda i, j, k: (k, j)))}[dims]
    if b_shards:
        per = b.shape[2] // (tn if dims == "nn" else tk)
        assert per >= 1 and b.shape[2] % (tn if dims == "nn" else tk) == 0
        b_spec = {"nn": pl.BlockSpec((None, tk, tn), at(lambda i, j, k: (j // per, k, j % per))),
                  "nt": pl.BlockSpec((None, tn, tk), at(lambda i, j, k: (k // per, j, k % per)))}[dims]
    o_spec = pl.BlockSpec((tm, tn), at(lambda i, j, k: (i, j)))
    o_shape = (M, N)
    if out_shards:
        assert not extras and N % out_shards == 0 and (N // out_shards) % tn == 0
        o_per = (N // out_shards) // tn
        o_spec = pl.BlockSpec((None, tm, tn), at(lambda i, j, k: (j // o_per, i, j % o_per)))
        o_shape = (out_shards, M, N // out_shards)
    r_specs = [pl.BlockSpec((tm, r.shape[1]), at(lambda i, j, k: (i, 0))) for r in row_extras]
    body = functools.partial(_mm_body, dims=dims, nk=nk, epi=epi,
                             n_extra=len(extras) + len(row_extras), n_out=len(out_dtypes))
    res = _pcall(
        body, name=name,
        grid=(N // tn, M // tm, nk) if b_outer else (M // tm, N // tn, nk),
        in_specs=[a_spec, b_spec] + [o_spec] * len(extras) + r_specs,
        out_specs=[o_spec] * len(out_dtypes),
        out_shape=[jax.ShapeDtypeStruct(o_shape, dt) for dt in out_dtypes],
        scratch_shapes=[pltpu.VMEM((tm, tn), F32)] if nk > 1 else [],
        compiler_params=pltpu.CompilerParams(
            dimension_semantics=("parallel", "parallel", "arbitrary")),
    )(a, b, *extras, *row_extras)
    return list(res)


def _rowwise(body, row_ins, vec_ins, row_outs, acc_outs, *, tr, name):
    T = row_ins[0].shape[0]
    tr = min(tr, T)
    assert T % tr == 0
    in_specs = [pl.BlockSpec((tr, a.shape[1]), lambda i: (i, 0)) for a in row_ins]
    in_specs += [pl.BlockSpec(a.shape, lambda i: (0, 0)) for a in vec_ins]
    out_specs = [pl.BlockSpec((tr, w), lambda i: (i, 0)) for (w, _) in row_outs]
    out_specs += [pl.BlockSpec(s, lambda i: (0, 0)) for s in acc_outs]
    out_shape = [jax.ShapeDtypeStruct((T, w), dt) for (w, dt) in row_outs]
    out_shape += [jax.ShapeDtypeStruct(s, F32) for s in acc_outs]
    sem = "arbitrary" if acc_outs else "parallel"
    return list(_pcall(
        body, name=name, grid=(T // tr,), in_specs=in_specs, out_specs=out_specs,
        out_shape=out_shape,
        compiler_params=pltpu.CompilerParams(dimension_semantics=(sem,)),
    )(*row_ins, *vec_ins))


def _rstd(x):
    return lax.rsqrt(jnp.mean(x * x, axis=-1, keepdims=True) + EPS)


def _rms_bwd(x, rstd, dyg):
    xh = x * rstd
    return rstd * (dyg - xh * jnp.mean(dyg * xh, axis=-1, keepdims=True)), xh


def _fold8(v):
    r, w = v.shape
    return jnp.sum(v.reshape(r // 8, 8, w), axis=0)


def _acc(ref, val):
    first = pl.program_id(0) == 0

    @pl.when(first)
    def _():
        ref[...] = val

    @pl.when(jnp.logical_not(first))
    def _():
        ref[...] += val


def _rope(x, c, sa, sb, half):
    return x * c + pltpu.roll(x, HEAD - half, 1) * sa + pltpu.roll(x, half, 1) * sb


def _rope_t(dy, c, sa, sb, half):
    return dy * c - pltpu.roll(dy, HEAD - half, 1) * sa - pltpu.roll(dy, half, 1) * sb


def _rope_tab_body(pos_ref, inv_ref, ca, saa, sab, cb, sba, sbb):
    pos = pos_ref[...]
    lane = lax.broadcasted_iota(jnp.int32, (pos.shape[0], HEAD), 1)
    ang_a = pos * inv_ref[0:1, :]
    ang_b = pos * inv_ref[1:2, :]
    c, s = jnp.cos(ang_a), jnp.sin(ang_a)
    ha = ROT_A // 2
    ca[...] = jnp.where(lane < ROT_A, c, 1.0)
    saa[...] = jnp.where(lane < ha, -s, 0.0)
    sab[...] = jnp.where((lane >= ha) & (lane < ROT_A), s, 0.0)
    c, s = jnp.cos(ang_b), jnp.sin(ang_b)
    hb = ROPE_B // 2
    cb[...] = jnp.where(lane < ROPE_B, c, 1.0)
    sba[...] = jnp.where(lane < hb, -s, 0.0)
    sbb[...] = jnp.where((lane >= hb) & (lane < ROPE_B), s, 0.0)


def _rms_fwd_body(x_ref, g_ref, h_ref):
    x = x_ref[...]
    h_ref[...] = ((x * _rstd(x)) * g_ref[...]).astype(h_ref.dtype)


def _postproj_body(p_ref, ca, saa, sab, cb, sba, sbb, gq_ref, gkv_ref,
                   q_ref, k_ref, v_ref, cqn_ref, ckvn_ref, krope_ref):
    c, sa, sb = ca[...], saa[...], sab[...]
    for h in range(NH):
        lo = h * HEAD
        q_ref[:, lo:lo + HEAD] = _rope(p_ref[:, lo:lo + HEAD], c, sa, sb, ROT_A // 2).astype(q_ref.dtype)
        k_ref[:, lo:lo + HEAD] = _rope(p_ref[:, A_W + lo:A_W + lo + HEAD], c, sa, sb, ROT_A // 2).astype(k_ref.dtype)
    v_ref[...] = p_ref[:, 2 * A_W:3 * A_W].astype(v_ref.dtype)
    cq = p_ref[:, 3 * A_W:3 * A_W + LORA]
    cqn_ref[...] = ((cq * _rstd(cq)) * gq_ref[...]).astype(cqn_ref.dtype)
    ckv = p_ref[:, 3 * A_W + LORA:MAIN_COLS]
    ckvn_ref[...] = ((ckv * _rstd(ckv)) * gkv_ref[...]).astype(ckvn_ref.dtype)
    krope_ref[...] = _rope(p_ref[:, MAIN_COLS:PROJ_COLS], cb[...], sba[...], sbb[...], ROPE_B // 2).astype(krope_ref.dtype)


def _mid_body(x_ref, o_ref, g2_ref, g3_ref, x1_ref, h2_ref):
    o = o_ref[...]
    x1 = x_ref[...] + (o * _rstd(o)) * g2_ref[...]
    x1_ref[...] = x1
    h2_ref[...] = ((x1 * _rstd(x1)) * g3_ref[...]).astype(h2_ref.dtype)


def _loss_body(x1_ref, d_ref, t_ref, g4_ref, dy_ref, dd_ref, loss_ref, dg4_ref):
    d = d_ref[...]
    rstd = _rstd(d)
    y = x1_ref[...] + (d * rstd) * g4_ref[...]
    e = y - t_ref[...]
    dy = e * (1.0 / D_MODEL)
    dy_ref[...] = dy
    dd, dh = _rms_bwd(d, rstd, dy * g4_ref[...])
    dd_ref[...] = dd.astype(dd_ref.dtype)
    _acc(dg4_ref, _fold8(dy * dh))
    e8 = _fold8(e * e)
    l = e8[:, 0:HEAD]
    for j in range(1, D_MODEL // HEAD):
        l = l + e8[:, j * HEAD:(j + 1) * HEAD]
    _acc(loss_ref, l)


def _bmid_body(dy_ref, dh2_ref, x1_ref, o_ref, g2_ref, g3_ref, dx1_ref, do_ref, dg3_ref, dg2_ref):
    x1 = x1_ref[...]
    dh2 = dh2_ref[...]
    dn, x1h = _rms_bwd(x1, _rstd(x1), dh2 * g3_ref[...])
    dx1 = dy_ref[...] + dn
    dx1_ref[...] = dx1
    _acc(dg3_ref, _fold8(dh2 * x1h))
    o = o_ref[...]
    do, oh = _rms_bwd(o, _rstd(o), dx1 * g2_ref[...])
    do_ref[...] = do.astype(do_ref.dtype)
    _acc(dg2_ref, _fold8(dx1 * oh))


def _dproj_body(dq_ref, dk_ref, dv_ref, dcq_ref, dckv_ref, p_ref, dkr_ref,
                ca, saa, sab, cb, sba, sbb, gq_ref, gkv_ref,
                dp_ref, dgq_ref, dgkv_ref):
    c, sa, sb = ca[...], saa[...], sab[...]
    for h in range(NH):
        lo = h * HEAD
        dp_ref[:, lo:lo + HEAD] = _rope_t(dq_ref[:, lo:lo + HEAD], c, sa, sb, ROT_A // 2).astype(dp_ref.dtype)
        dp_ref[:, A_W + lo:A_W + lo + HEAD] = _rope_t(dk_ref[:, lo:lo + HEAD], c, sa, sb, ROT_A // 2).astype(dp_ref.dtype)
    dp_ref[:, 2 * A_W:3 * A_W] = dv_ref[...].astype(dp_ref.dtype)
    cq = p_ref[:, 3 * A_W:3 * A_W + LORA]
    dcqn = dcq_ref[...]
    dcq, cqh = _rms_bwd(cq, _rstd(cq), dcqn * gq_ref[...])
    dp_ref[:, 3 * A_W:3 * A_W + LORA] = dcq.astype(dp_ref.dtype)
    _acc(dgq_ref, _fold8(dcqn * cqh))
    ckv = p_ref[:, 3 * A_W + LORA:MAIN_COLS]
    dckvn = dckv_ref[...]
    dckv, ckvh = _rms_bwd(ckv, _rstd(ckv), dckvn * gkv_ref[...])
    dp_ref[:, 3 * A_W + LORA:MAIN_COLS] = dckv.astype(dp_ref.dtype)
    _acc(dgkv_ref, _fold8(dckvn * ckvh))
    dkr = dkr_ref[:, 0:HEAD]
    for h in range(1, NH):
        dkr = dkr + dkr_ref[:, h * HEAD:(h + 1) * HEAD]
    dp_ref[:, MAIN_COLS:PROJ_COLS] = _rope_t(dkr, cb[...], sba[...], sbb[...], ROPE_B // 2).astype(dp_ref.dtype)


def _bin_body(dx1_ref, dh_ref, x_ref, g1_ref, dx_ref, dg1_ref):
    x = x_ref[...]
    dh = dh_ref[...]
    dn, xh = _rms_bwd(x, _rstd(x), dh * g1_ref[...])
    dx_ref[...] = dx1_ref[...] + dn
    _acc(dg1_ref, _fold8(dh * xh))


def _dot_nt(a, b):
    return lax.dot_general(a, b, _DIMS["nt"], preferred_element_type=F32)


def _dot_tn(a, b):
    return lax.dot_general(a, b, _DIMS["tn"], preferred_element_type=F32)


def _dot_nn(a, b):
    return jnp.dot(a, b, preferred_element_type=F32)


DIL_SCALE = HEAD ** -0.5
DIL_CHUNK = 256


def _dil_rows(t, d):
    r = t & (d - 1)
    n = t >> (d.bit_length() - 1)
    start = r + n * (HEAD * d)
    has_prev = n > 0
    pstart = jnp.where(has_prev, start - HEAD * d, start)
    if d == 1:
        return pl.ds(pl.multiple_of(start, HEAD), HEAD), pl.ds(pl.multiple_of(pstart, HEAD), HEAD), has_prev
    return pl.ds(start, HEAD, stride=d), pl.ds(pstart, HEAD, stride=d), has_prev


def _dil_band():
    row = lax.broadcasted_iota(jnp.int32, (HEAD, 2 * HEAD), 0)
    col = lax.broadcasted_iota(jnp.int32, (HEAD, 2 * HEAD), 1)
    return (col >= row) & (col <= row + HEAD), col >= HEAD


def _dil_fwd_body(q_ref, k_ref, v_ref, a_ref, lse_ref, o1, o2, o3, l1, l2, l3, *, nt, unroll):
    band, is_cur = _dil_band()
    for d, o_sc, l_sc in zip(DIL, (o1, o2, o3), (l1, l2, l3)):

        def tile(t, carry, d=d, o_sc=o_sc, l_sc=l_sc):
            rows, prows, has_prev = _dil_rows(t, d)
            q = q_ref[rows, :].astype(MXU_DTYPE)
            kk = jnp.concatenate([k_ref[prows, :], k_ref[rows, :]], axis=0).astype(MXU_DTYPE)
            vv = jnp.concatenate([v_ref[prows, :], v_ref[rows, :]], axis=0).astype(MXU_DTYPE)
            ok = band & (is_cur | has_prev)
            s = jnp.where(ok, _dot_nt(q, kk) * DIL_SCALE, NEG)
            m = jnp.max(s, axis=1, keepdims=True)
            p = jnp.exp(s - m)
            den = jnp.sum(p, axis=1, keepdims=True)
            o_sc[rows, :] = _dot_nn((p / den).astype(MXU_DTYPE), vv)
            l_sc[rows, :] = jnp.broadcast_to(m + jnp.log(den), (HEAD, HEAD))
            return carry

        lax.fori_loop(0, nt, tile, 0, unroll=unroll)

    def merge(i, carry):
        rs = pl.ds(pl.multiple_of(i * DIL_CHUNK, DIL_CHUNK), DIL_CHUNK)
        la, lb, lc = l1[rs, :], l2[rs, :], l3[rs, :]
        m = jnp.maximum(jnp.maximum(la, lb), lc)
        wa, wb, wc = jnp.exp(la - m), jnp.exp(lb - m), jnp.exp(lc - m)
        den = wa + wb + wc
        a = (wa / den) * o1[rs, :] + (wb / den) * o2[rs, :] + (wc / den) * o3[rs, :]
        a_ref[rs, :] = a.astype(a_ref.dtype)
        lse_ref[rs, :] = m + jnp.log(den)
        return carry

    lax.fori_loop(0, q_ref.shape[0] // DIL_CHUNK, merge, 0)


def _dil_fwd(q, k, v):
    T = q.shape[0]
    spec = pl.BlockSpec((T, HEAD), lambda h: (0, h))
    return _pcall(
        functools.partial(_dil_fwd_body, nt=T // HEAD, unroll=16), name="dil_fwd",
        grid=(NH,), in_specs=[spec] * 3, out_specs=[spec] * 2,
        out_shape=[jax.ShapeDtypeStruct((T, 2 * A_W), MXU_DTYPE), jax.ShapeDtypeStruct((T, A_W), F32)],
        scratch_shapes=[pltpu.VMEM((T, HEAD), F32)] * 6,
        compiler_params=pltpu.CompilerParams(dimension_semantics=("parallel",)),
    )(q, k, v)


def _dil_bwd_body(q_ref, k_ref, v_ref, do_ref, a_ref, lse_ref, dq_ref, dk_ref, dv_ref, dl_sc, *, nt, unroll):
    band, is_cur = _dil_band()

    def prep(i, carry):
        rs = pl.ds(pl.multiple_of(i * DIL_CHUNK, DIL_CHUNK), DIL_CHUNK)
        dl = jnp.sum(do_ref[rs, :] * a_ref[rs, :].astype(F32), axis=1, keepdims=True)
        dl_sc[rs, :] = jnp.broadcast_to(dl, (DIL_CHUNK, HEAD))
        zero = jnp.zeros((DIL_CHUNK, HEAD), F32)
        dq_ref[rs, :] = zero
        dk_ref[rs, :] = zero
        dv_ref[rs, :] = zero
        return carry

    lax.fori_loop(0, q_ref.shape[0] // DIL_CHUNK, prep, 0)

    for d in DIL:

        def tile(t, carry, d=d):
            rows, prows, has_prev = _dil_rows(t, d)
            q = q_ref[rows, :].astype(MXU_DTYPE)
            kk = jnp.concatenate([k_ref[prows, :], k_ref[rows, :]], axis=0).astype(MXU_DTYPE)
            vv = jnp.concatenate([v_ref[prows, :], v_ref[rows, :]], axis=0).astype(MXU_DTYPE)
            do = do_ref[rows, :].astype(MXU_DTYPE)
            lse = lse_ref[rows, :]
            dl = dl_sc[rows, :]
            ok = band & (is_cur | has_prev)
            s = _dot_nt(q, kk) * DIL_SCALE
            p = jnp.where(ok, jnp.exp(s - jnp.concatenate([lse, lse], axis=1)), 0.0)
            ds = (p * (_dot_nt(do, vv) - jnp.concatenate([dl, dl], axis=1))).astype(MXU_DTYPE)
            dq_ref[rows, :] += _dot_nn(ds, kk) * DIL_SCALE
            dkk = _dot_tn(ds, q) * DIL_SCALE
            dvv = _dot_tn(p.astype(MXU_DTYPE), do)
            dk_ref[rows, :] += dkk[HEAD:, :]
            dv_ref[rows, :] += dvv[HEAD:, :]
            dk_ref[prows, :] += dkk[:HEAD, :]
            dv_ref[prows, :] += dvv[:HEAD, :]
            return carry

        lax.fori_loop(0, nt, tile, 0, unroll=unroll)


def _dil_bwd(q, k, v, dmix, mixed, lse):
    T = q.shape[0]
    spec = pl.BlockSpec((T, HEAD), lambda h: (0, h))
    return _pcall(
        functools.partial(_dil_bwd_body, nt=T // HEAD, unroll=8), name="dil_bwd",
        grid=(NH,), in_specs=[spec] * 6, out_specs=[spec] * 3,
        out_shape=[jax.ShapeDtypeStruct((T, A_W), F32)] * 3,
        scratch_shapes=[pltpu.VMEM((T, HEAD), F32)],
        compiler_params=pltpu.CompilerParams(dimension_semantics=("parallel",)),
    )(q, k, v, dmix, mixed, lse)


MLA_SCALE = (HEAD + ROPE_B) ** -0.5
LOG2E = 1.4426950408889634
MLA_QSCALE = MLA_SCALE * LOG2E
MLA_T = 512
MLA_HP = 2


def _tri(t):
    row = lax.broadcasted_iota(jnp.int32, (t, t), 0)
    col = lax.broadcasted_iota(jnp.int32, (t, t), 1)
    return col <= row


def _lanes(x, n):
    return jnp.tile(x, (1, n // HEAD))


def _mla_fwd_body(q_ref, kn_ref, kr_ref, v_ref, mixed_ref, o_ref, lse_ref, m_sc, l_sc, acc_sc, *, t, hp):
    del mixed_ref
    qi = pl.program_id(1)
    m_sc[...] = jnp.full(m_sc.shape, NEG, F32)
    l_sc[...] = jnp.zeros(l_sc.shape, F32)
    acc_sc[...] = jnp.zeros(acc_sc.shape, F32)

    def step(j, masked):
        ks = pl.ds(pl.multiple_of(j * t, t), t)
        kr = kr_ref[ks, :]
        for hh in range(hp):
            kcat = jnp.concatenate([kn_ref[ks, hh * HEAD:(hh + 1) * HEAD], kr], axis=1)
            s = _dot_nt(q_ref[:, hh * QPAD:(hh + 1) * QPAD], kcat)
            if masked:
                s = jnp.where(_tri(t), s, NEG)
            m_prev = m_sc[hh]
            m_new = jnp.maximum(m_prev, jnp.max(s, axis=1, keepdims=True))
            alpha = jnp.exp2(m_prev - m_new)
            p = jnp.exp2(s - _lanes(m_new, t))
            l_sc[hh] = alpha * l_sc[hh] + jnp.sum(p, axis=1, keepdims=True)
            acc_sc[hh] = alpha * acc_sc[hh] + _dot_nn(p.astype(MXU_DTYPE), v_ref[ks, hh * HEAD:(hh + 1) * HEAD])
            m_sc[hh] = m_new

    def off_diag(j, carry):
        step(j, False)
        return carry

    lax.fori_loop(0, qi, off_diag, 0)
    step(qi, True)
    for hh in range(hp):
        l = l_sc[hh]
        o_ref[:, hh * HEAD:(hh + 1) * HEAD] = (acc_sc[hh] / l).astype(o_ref.dtype)
        lse_ref[:, hh * HEAD:(hh + 1) * HEAD] = m_sc[hh] + jnp.log2(l)


def _mla_fwd(qf, kv, kr, mixed):
    T = qf.shape[0]
    t, hp = min(MLA_T, T), MLA_HP
    ng = NH // hp
    return _pcall(
        functools.partial(_mla_fwd_body, t=t, hp=hp), name="mla_fwd",
        grid=(ng, T // t),
        in_specs=[pl.BlockSpec((t, hp * QPAD), lambda g, i: (i, g)),
                  pl.BlockSpec((T, hp * HEAD), lambda g, i: (0, g)),
                  pl.BlockSpec((T, HEAD), lambda g, i: (0, 0)),
                  pl.BlockSpec((T, hp * HEAD), lambda g, i: (0, ng + g)), ANY],
        out_specs=[pl.BlockSpec((t, hp * HEAD), lambda g, i: (i, ng + g)),
                   pl.BlockSpec((t, hp * HEAD), lambda g, i: (i, g))],
        out_shape=[jax.ShapeDtypeStruct(mixed.shape, mixed.dtype), jax.ShapeDtypeStruct((T, A_W), F32)],
        input_output_aliases={4: 0},
        scratch_shapes=[pltpu.VMEM((hp, t, HEAD), F32)] * 3,
        compiler_params=pltpu.CompilerParams(dimension_semantics=("parallel", "parallel")),
    )(qf, kv, kr, kv, mixed)


def _mla_bwd_body(q_ref, kn_ref, kr_ref, v_ref, do_ref, o_ref, lse_ref, cb, sba, sbb,
                  dq_ref, dkn_ref, dv_ref, dkr_ref, dq_sc, dl_sc, dk_sc, dv_sc, *, t):
    ki = pl.program_id(1)
    nq = q_ref.shape[0] // t

    @pl.when(ki == 0)
    def _():
        def prep(i, carry):
            rs = pl.ds(pl.multiple_of(i * t, t), t)
            dl = jnp.sum(do_ref[rs, :] * o_ref[rs, :].astype(F32), axis=1, keepdims=True)
            dl_sc[rs, :] = jnp.broadcast_to(dl, (t, HEAD))
            dq_sc[rs, :] = jnp.zeros((t, QPAD), F32)
            return carry
        lax.fori_loop(0, nq, prep, 0)

    kcat = jnp.concatenate([kn_ref[...], kr_ref[...]], axis=1)
    v = v_ref[...]
    dk_sc[...] = jnp.zeros(dk_sc.shape, F32)
    dv_sc[...] = jnp.zeros(dv_sc.shape, F32)

    def step(i, masked):
        qs = pl.ds(pl.multiple_of(i * t, t), t)
        q = q_ref[qs, :]
        do = do_ref[qs, :].astype(MXU_DTYPE)
        p = jnp.exp2(_dot_nt(q, kcat) - _lanes(lse_ref[qs, :], t))
        if masked:
            p = jnp.where(_tri(t), p, 0.0)
        ds = (p * (_dot_nt(do, v) - _lanes(dl_sc[qs, :], t))).astype(MXU_DTYPE)
        dv_sc[...] += _dot_tn(p.astype(MXU_DTYPE), do)
        dk_sc[...] += _dot_tn(ds, q)
        dq_sc[qs, :] += _dot_nn(ds, kcat) * MLA_SCALE

    step(ki, True)

    def off_diag(i, carry):
        step(i, False)
        return carry

    lax.fori_loop(ki + 1, nq, off_diag, 0)
    dk = dk_sc[...] * (1.0 / LOG2E)
    dkn_ref[...] = dk[:, 0:HEAD].astype(dkn_ref.dtype)
    dkr_ref[...] = dk[:, HEAD:QPAD]
    dv_ref[...] = dv_sc[...].astype(dv_ref.dtype)

    @pl.when(ki == nq - 1)
    def _():
        def emit(i, carry):
            rs = pl.ds(pl.multiple_of(i * t, t), t)
            dq_ref[rs, 0:HEAD] = dq_sc[rs, 0:HEAD].astype(dq_ref.dtype)
            dq_ref[rs, HEAD:QPAD] = _rope_t(dq_sc[rs, HEAD:QPAD], cb[rs, :], sba[rs, :], sbb[rs, :],
                                            ROPE_B // 2).astype(dq_ref.dtype)
            return carry
        lax.fori_loop(0, nq, emit, 0)


def _mla_bwd(qf, kv, kr, dmix, mixed, lse, tabs_b):
    T = qf.shape[0]
    t = min(MLA_T, T)
    head = lambda h, j: (0, h)
    b_half = lambda h, j: (0, NH + h)
    kblk = pl.BlockSpec((t, HEAD), lambda h, j: (j, h))
    return _pcall(
        functools.partial(_mla_bwd_body, t=t), name="mla_bwd",
        grid=(NH, T // t),
        in_specs=[pl.BlockSpec((T, QPAD), head), kblk,
                  pl.BlockSpec((t, HEAD), lambda h, j: (j, 0)),
                  pl.BlockSpec((t, HEAD), lambda h, j: (j, NH + h)),
                  pl.BlockSpec((T, HEAD), b_half), pl.BlockSpec((T, HEAD), b_half),
                  pl.BlockSpec((T, HEAD), head)] + [pl.BlockSpec((T, HEAD), lambda h, j: (0, 0))] * 3,
        out_specs=[pl.BlockSpec((T, QPAD), head), kblk, kblk, kblk],
        out_shape=[jax.ShapeDtypeStruct((T, NH * QPAD), MXU_DTYPE), jax.ShapeDtypeStruct((T, A_W), MXU_DTYPE),
                   jax.ShapeDtypeStruct((T, A_W), MXU_DTYPE), jax.ShapeDtypeStruct((T, A_W), F32)],
        scratch_shapes=[pltpu.VMEM((T, QPAD), F32), pltpu.VMEM((T, HEAD), F32), pltpu.VMEM((t, QPAD), F32),
                        pltpu.VMEM((t, HEAD), F32)],
        compiler_params=pltpu.CompilerParams(dimension_semantics=("parallel", "arbitrary")),
    )(qf, kv, kr, kv, dmix, mixed, lse, *tabs_b)


def _local_step(x, pos, target, g1, g2, gq, gkv, g3, g4,
                in_weights, attn_weights, mlp_prefetch, mlp_weights, mlp_grads_ready, attn_grads_ready):
    T = x.shape[0]
    TR = 256
    mm = functools.partial(_matmul, tm=2048, tn=1024, tk=2048, b_outer=True)
    mm_k = functools.partial(_matmul, tm=2048, tn=512, tk=2048)
    mm_g = functools.partial(_matmul, tm=1024, tn=1024, tk=4096, b_outer=True)

    inv_a = ROPE_THETA ** (-jnp.arange(0, ROT_A, 2, dtype=F32) / ROT_A)
    inv_b = ROPE_THETA ** (-jnp.arange(0, ROPE_B, 2, dtype=F32) / ROPE_B)
    inv = jnp.stack([jnp.concatenate([inv_a, inv_a, jnp.zeros((HEAD - ROT_A,), F32)]),
                     jnp.concatenate([inv_b, inv_b, jnp.zeros((HEAD - ROPE_B,), F32)])])
    inv = jnp.concatenate([inv, jnp.zeros((6, HEAD), F32)], axis=0)
    tabs = _rowwise(_rope_tab_body, [pos], [inv], [(HEAD, F32)] * 6, [], tr=512, name="rope_tables")

    (h,) = _rowwise(_rms_fwd_body, [x], [g1], [(D_MODEL, MXU_DTYPE)], [], tr=TR, name="rms_in")
    w_proj = in_weights(h)
    (proj,) = mm(h, w_proj, dims="nn", out_dtypes=[F32], tm=1024, tn=PROJ_TILE, name="proj_in")
    q, k, v, cqn, ckvn, krope = _rowwise(
        _postproj_body, [proj] + tabs, [gq, gkv],
        [(A_W, F32)] * 3 + [(LORA, MXU_DTYPE)] * 2 + [(HEAD, MXU_DTYPE)], [], tr=TR, name="post_proj")
    mixed, lse_a = _dil_fwd(q, k, v)

    w_uq_p, w_ukv_p, w_out = attn_weights(cqn)

    def q_epi(acc, cb, sba, sbb):
        cols = []
        for hh in range(acc.shape[1] // QPAD):
            lo = hh * QPAD
            cols += [acc[:, lo:lo + HEAD], _rope(acc[:, lo + HEAD:lo + QPAD], cb, sba, sbb, ROPE_B // 2)]
        return (jnp.concatenate(cols, axis=1) * MLA_QSCALE,)
    (qf,) = mm(cqn, w_uq_p, dims="nn", out_dtypes=[MXU_DTYPE], name="q_up", epi=q_epi, row_extras=tuple(tabs[3:]))
    (kv,) = mm(ckvn, w_ukv_p, dims="nn", out_dtypes=[MXU_DTYPE], name="kv_up")
    mixed, lse_b = _mla_fwd(qf, kv, krope, mixed)
    mlp_prefetch(mixed)

    (o,) = mm(mixed, w_out, dims="nn", out_dtypes=[F32], name="out_proj")
    x1, h2 = _rowwise(_mid_body, [x, o], [g2, g3], [(D_MODEL, F32), (D_MODEL, MXU_DTYPE)], [], tr=TR, name="mid_norm")

    w_up, w_down = mlp_weights(h2)

    def up_epi(acc):
        r = jnp.maximum(acc, 0.0)
        return r * r, r
    u, r = mm(h2, w_up, dims="nn", out_dtypes=[MXU_DTYPE, MXU_DTYPE], name="mlp_up", epi=up_epi, b_shards=N_CHIPS)
    (dn,) = mm_k(u, w_down, dims="nn", out_dtypes=[F32], name="mlp_down")
    dy, dd, loss8, dg4 = _rowwise(_loss_body, [x1, dn, target], [g4], [(D_MODEL, F32), (D_MODEL, MXU_DTYPE)],
                                  [(8, HEAD), (8, D_MODEL)], tr=TR, name="loss_head")

    def dup_epi(acc, rr):
        return (acc * (2.0 * rr.astype(F32)),)
    (dup,) = mm(dd, w_down, dims="nt", out_dtypes=[MXU_DTYPE], name="d_up", epi=dup_epi, extras=(r,))
    (gw_down,) = mm_g(u, dd, dims="tn", out_dtypes=[WIRE_DTYPE], name="gw_down")
    (dh2,) = mm_k(dup, w_up, dims="nt", out_dtypes=[F32], name="d_h2", b_shards=N_CHIPS)
    (gw_up,) = mm_g(h2, dup, dims="tn", out_dtypes=[WIRE_DTYPE], name="gw_up", out_shards=N_CHIPS)
    g2 = g2 + mlp_grads_ready(gw_up, gw_down)
    dx1, do, dg3, dg2 = _rowwise(_bmid_body, [dy, dh2, x1, o], [g2, g3], [(D_MODEL, F32), (D_MODEL, MXU_DTYPE)],
                                 [(8, D_MODEL), (8, D_MODEL)], tr=TR, name="bwd_mid")
    (dmix,) = mm(do, w_out, dims="nt", out_dtypes=[F32], name="d_mixed")
    (gw_out,) = mm_g(mixed, do, dims="tn", out_dtypes=[WIRE_DTYPE], name="gw_out")

    dq_pad, dkn, dvb, dkr = _mla_bwd(qf, kv, krope, dmix, mixed, lse_b, tabs[3:])
    (dcqn,) = mm(dq_pad, w_uq_p, dims="nt", out_dtypes=[F32], name="d_cq")
    (gw_uq_p,) = mm_g(cqn, dq_pad, dims="tn", out_dtypes=[WIRE_DTYPE], name="gw_uq")
    dkv = jnp.concatenate([dkn, dvb], axis=1)
    (dckvn,) = mm(dkv, w_ukv_p, dims="nt", out_dtypes=[F32], name="d_ckv")
    (gw_ukv_p,) = mm_g(ckvn, dkv, dims="tn", out_dtypes=[WIRE_DTYPE], name="gw_ukv")
    gq = gq + attn_grads_ready(gw_out, gw_uq_p, gw_ukv_p)

    dq_a, dk_a, dv_a = _dil_bwd(q, k, v, dmix, mixed, lse_a)
    dproj, dgq, dgkv = _rowwise(
        _dproj_body, [dq_a, dk_a, dv_a, dcqn, dckvn, proj, dkr] + tabs, [gq, gkv],
        [(PROJ_COLS, MXU_DTYPE)], [(8, LORA), (8, LORA)], tr=TR, name="d_proj")
    (dh,) = mm_k(dproj, w_proj, dims="nt", out_dtypes=[F32], tk=PROJ_TILE, name="d_h")
    (gw_proj,) = mm_g(h, dproj, dims="tn", out_dtypes=[WIRE_DTYPE], tn=PROJ_TILE, name="gw_in")
    dx, dg1 = _rowwise(_bin_body, [dx1, dh, x], [g1], [(D_MODEL, F32)], [(8, D_MODEL)], tr=TR, name="bwd_in")

    small = jnp.concatenate([dg1, dg2, dgq, dgkv, dg3, dg4, loss8], axis=1)
    return dx, gw_proj, small


def _place():
    x, y, c = lax.axis_index("x"), lax.axis_index("y"), lax.axis_index("c")
    chips = [(1 - x, y), (x, 1 - y), (1 - x, 1 - y)]
    return x, y, c, chips


def _cast_place_body(me_ref, w_ref, o_ref):
    o_ref[...] = w_ref[...].astype(o_ref.dtype)


def _cast_place(me_arr, w, name):
    rows, cols = w.shape
    tr = min(rows, 256)
    grid_spec = pltpu.PrefetchScalarGridSpec(
        num_scalar_prefetch=1, grid=(rows // tr,),
        in_specs=[pl.BlockSpec((tr, cols), lambda i, me: (i, 0))],
        out_specs=pl.BlockSpec((None, tr, cols), lambda i, me: (me[0], i, 0)))
    return _pcall(
        _cast_place_body, name=name, grid_spec=grid_spec,
        out_shape=jax.ShapeDtypeStruct((N_CHIPS, rows, cols), WIRE_DTYPE),
        compiler_params=pltpu.CompilerParams(dimension_semantics=("parallel",)),
    )(me_arr, w)


HBM = pl.BlockSpec(memory_space=pltpu.HBM)
SEM = pl.BlockSpec(memory_space=pltpu.SEMAPHORE)
EFFECT = pltpu.SideEffectType.DATAFLOW_SIDE_EFFECTING


def _in_hbm(a):
    return pltpu.with_memory_space_constraint(a, pltpu.HBM)


def _ag_descs(bufs, send_sems, recv_sems):
    x, y, c, chips = _place()
    me = 2 * x + y
    out = []
    for w, buf in enumerate(bufs):
        half = buf.shape[1] // 2
        rows = pl.ds(pl.multiple_of(c * half, 16), half)
        mine = buf.at[me, rows]
        for j, (px, py) in enumerate(chips):
            landed = buf.at[2 * px + py, rows]
            mk = lambda ref, w=w, j=j, px=px, py=py: pltpu.make_async_remote_copy(
                src_ref=ref, dst_ref=ref, send_sem=send_sems.at[w * 3 + j], recv_sem=recv_sems.at[w * 3 + j],
                device_id=(px, py, c), device_id_type=MESH)
            out.append((mk(mine), mk(landed)))
    return out


def _ag_start_body(*refs, n_w):
    bufs = refs[:n_w]
    send_sems, recv_sems = refs[-n_w - 3], refs[-n_w - 2]
    token = refs[-1]
    for send, _ in _ag_descs(bufs, send_sems, recv_sems):
        send.start()
    token[...] = jnp.zeros_like(token)


def _ag_start(placed, after, tag):
    n_w = len(placed)
    after = [] if after is None else [after]
    res = _pcall(
        functools.partial(_ag_start_body, n_w=n_w), name="weight_allgather_start_" + tag,
        in_specs=[HBM] * n_w + [ANY] * len(after),
        out_specs=[SEM, SEM] + [HBM] * n_w + [pl.BlockSpec(memory_space=pltpu.VMEM)],
        out_shape=[pltpu.SemaphoreType.DMA((3 * n_w,)), pltpu.SemaphoreType.DMA((3 * n_w,))]
        + [pltpu.HBM(p.shape, p.dtype) for p in placed] + [jax.ShapeDtypeStruct((8, HEAD), F32)],
        input_output_aliases={w: 2 + w for w in range(n_w)},
        compiler_params=pltpu.CompilerParams(has_side_effects=EFFECT),
    )(*[_in_hbm(p) for p in placed], *after)
    return res[0], res[1], list(res[2:2 + n_w]), res[-1]


def _ag_wait_body(*refs, n_w):
    bufs = refs[:n_w]
    send_sems, recv_sems = refs[n_w], refs[n_w + 1]
    for send, recv in _ag_descs(bufs, send_sems, recv_sems):
        send.wait_send()
        recv.wait_recv()


def _ag_wait(send_sems, recv_sems, bufs, after, tag):
    n_w = len(bufs)
    return list(_pcall(
        functools.partial(_ag_wait_body, n_w=n_w), name="weight_allgather_wait_" + tag,
        in_specs=[HBM] * n_w + [SEM, SEM, ANY], out_specs=[HBM] * n_w,
        out_shape=[pltpu.HBM(b.shape, b.dtype) for b in bufs],
        input_output_aliases={w: w for w in range(n_w)},
        compiler_params=pltpu.CompilerParams(has_side_effects=EFFECT),
    )(*bufs, send_sems, recv_sems, after))


def _fw_descs(bufs, send_sems, recv_sems):
    x, y, c, chips = _place()
    out = []
    for w, buf in enumerate(bufs):
        half = buf.shape[1] // 2
        for j, (px, py) in enumerate(chips):
            def mk(which, w=w, j=j, buf=buf, half=half, px=px, py=py):
                ref = buf.at[2 * px + py, pl.ds(pl.multiple_of(which * half, 16), half)]
                return pltpu.make_async_remote_copy(
                    src_ref=ref, dst_ref=ref, send_sem=send_sems.at[w * 3 + j], recv_sem=recv_sems.at[w * 3 + j],
                    device_id=(x, y, 1 - c), device_id_type=MESH)
            out.append((mk(c), mk(1 - c)))
    return out


def _fw_start_body(*refs, n_w):
    bufs = refs[:n_w]
    send_sems, recv_sems = refs[n_w], refs[n_w + 1]
    token = refs[-1]
    for send, _ in _fw_descs(bufs, send_sems, recv_sems):
        send.start()
    token[...] = jnp.zeros_like(token)


def _fw_start(bufs, tag):
    n_w = len(bufs)
    res = _pcall(
        functools.partial(_fw_start_body, n_w=n_w), name="weight_allgather_forward_start_" + tag,
        in_specs=[HBM] * n_w,
        out_specs=[SEM, SEM] + [HBM] * n_w + [pl.BlockSpec(memory_space=pltpu.VMEM)],
        out_shape=[pltpu.SemaphoreType.DMA((3 * n_w,)), pltpu.SemaphoreType.DMA((3 * n_w,))]
        + [pltpu.HBM(b.shape, b.dtype) for b in bufs] + [jax.ShapeDtypeStruct((8, HEAD), F32)],
        input_output_aliases={w: 2 + w for w in range(n_w)},
        compiler_params=pltpu.CompilerParams(has_side_effects=EFFECT),
    )(*bufs)
    return res[0], res[1], list(res[2:2 + n_w]), res[-1]


def _fw_wait_body(*refs, n_w):
    bufs = refs[:n_w]
    send_sems, recv_sems = refs[n_w], refs[n_w + 1]
    for send, back in _fw_descs(bufs, send_sems, recv_sems):
        send.wait_send()
        back.wait_recv()


def _fw_wait(send_sems, recv_sems, bufs, after, tag):
    n_w = len(bufs)
    return list(_pcall(
        functools.partial(_fw_wait_body, n_w=n_w), name="weight_allgather_forward_wait_" + tag,
        in_specs=[HBM] * n_w + [SEM, SEM, ANY], out_specs=[HBM] * n_w,
        out_shape=[pltpu.HBM(b.shape, b.dtype) for b in bufs],
        input_output_aliases={w: w for w in range(n_w)},
        compiler_params=pltpu.CompilerParams(has_side_effects=EFFECT),
    )(*bufs, send_sems, recv_sems, after))


def _ag_forward_body(*refs, n_w):
    bufs = refs[n_w:2 * n_w]
    send_sems, recv_sems = refs[2 * n_w:]
    x, y, c, chips = _place()
    fwds = []
    for w, buf in enumerate(bufs):
        half = buf.shape[1] // 2
        for j, (px, py) in enumerate(chips):
            def piece(which, buf=buf, half=half, px=px, py=py):
                return buf.at[2 * px + py, pl.ds(pl.multiple_of(which * half, 16), half)]
            mk = lambda ref, w=w, j=j: pltpu.make_async_remote_copy(
                src_ref=ref, dst_ref=ref, send_sem=send_sems.at[w * 3 + j], recv_sem=recv_sems.at[w * 3 + j],
                device_id=(x, y, 1 - c), device_id_type=MESH)
            fw = mk(piece(c))
            fw.start()
            fwds.append((fw, mk(piece(1 - c))))
    for fw, back in fwds:
        back.wait_recv()
        fw.wait_send()


def _ag_forward(bufs, tag):
    n_w = len(bufs)
    return list(_pcall(
        functools.partial(_ag_forward_body, n_w=n_w), name="weight_allgather_forward_" + tag,
        in_specs=[ANY] * n_w, out_specs=[ANY] * n_w,
        out_shape=[jax.ShapeDtypeStruct(b.shape, b.dtype) for b in bufs],
        input_output_aliases={w: w for w in range(n_w)},
        scratch_shapes=[pltpu.SemaphoreType.DMA((3 * n_w,))] * 2,
    )(*bufs))


def _sc_descs(ins, outs, send_sems, recv_sems):
    x, y, c, chips = _place()
    me = 2 * x + y
    out = []
    for w in range(len(ins)):
        for j, (px, py) in enumerate(chips):
            out.append(pltpu.make_async_remote_copy(
                src_ref=ins[w].at[2 * px + py], dst_ref=outs[w].at[me],
                send_sem=send_sems.at[w * 3 + j], recv_sem=recv_sems.at[w * 3 + j],
                device_id=(px, py, c), device_id_type=MESH))
    return out


def _scatter_start_body(*refs, n_w):
    ins, lands = refs[:n_w], refs[n_w:2 * n_w]
    send_sems, recv_sems = refs[-2 * n_w - 3], refs[-2 * n_w - 2]
    token = refs[-1]
    for cp in _sc_descs(ins, lands, send_sems, recv_sems):
        cp.start()
    token[...] = jnp.zeros_like(token)


def _scatter_start(parts, after, tag):
    n_w = len(parts)
    lands = [lax.empty(p.shape, p.dtype) for p in parts]
    after = [] if after is None else [after]
    res = _pcall(
        functools.partial(_scatter_start_body, n_w=n_w), name="grad_scatter_start_" + tag,
        in_specs=[HBM] * (2 * n_w) + [ANY] * len(after),
        out_specs=[SEM, SEM] + [HBM] * (2 * n_w) + [pl.BlockSpec(memory_space=pltpu.VMEM)],
        out_shape=[pltpu.SemaphoreType.DMA((3 * n_w,)), pltpu.SemaphoreType.DMA((3 * n_w,))]
        + [pltpu.HBM(p.shape, p.dtype) for p in parts] * 2 + [jax.ShapeDtypeStruct((8, HEAD), F32)],
        input_output_aliases={i: 2 + i for i in range(2 * n_w)},
        compiler_params=pltpu.CompilerParams(has_side_effects=EFFECT),
    )(*[_in_hbm(p) for p in parts], *[_in_hbm(l) for l in lands], *after)
    return res[0], res[1], list(res[2:2 + n_w]), list(res[2 + n_w:2 + 2 * n_w]), res[-1]


def _scatter_wait_body(*refs, n_w):
    ins, lands = refs[:n_w], refs[n_w:2 * n_w]
    send_sems, recv_sems = refs[2 * n_w], refs[2 * n_w + 1]
    for cp in _sc_descs(ins, lands, send_sems, recv_sems):
        cp.wait_send()
        cp.wait_recv()


def _scatter_wait(send_sems, recv_sems, parts, lands, after, tag):
    n_w = len(parts)
    res = _pcall(
        functools.partial(_scatter_wait_body, n_w=n_w), name="grad_scatter_wait_" + tag,
        in_specs=[HBM] * (2 * n_w) + [SEM, SEM, ANY], out_specs=[HBM] * (2 * n_w),
        out_shape=[pltpu.HBM(p.shape, p.dtype) for p in parts] * 2,
        input_output_aliases={i: i for i in range(2 * n_w)},
        compiler_params=pltpu.CompilerParams(has_side_effects=EFFECT),
    )(*parts, *lands, send_sems, recv_sems, after)
    return list(res[:n_w]), list(res[n_w:])


def _pair_send_body(*refs, n_w):
    ins, outs = refs[:n_w], refs[n_w:2 * n_w]
    send_sems, recv_sems = refs[2 * n_w:]
    x, y, c, _ = _place()
    cps = []
    for w in range(n_w):
        cp = pltpu.make_async_remote_copy(
            src_ref=ins[w].at[:, 1 - c], dst_ref=outs[w],
            send_sem=send_sems.at[w], recv_sem=recv_sems.at[w],
            device_id=(x, y, 1 - c), device_id_type=MESH)
        cp.start()
        cps.append(cp)
    for cp in cps:
        cp.wait()


def _pair_send(grads4, tag):
    n_w = len(grads4)
    return _pcall(
        functools.partial(_pair_send_body, n_w=n_w), name="grad_pair_exchange_" + tag,
        in_specs=[ANY] * n_w, out_specs=[ANY] * n_w,
        out_shape=[jax.ShapeDtypeStruct((g.shape[0],) + g.shape[2:], g.dtype) for g in grads4],
        scratch_shapes=[pltpu.SemaphoreType.DMA((n_w,))] * 2,
    )(*grads4)


def _pair_add_body(c_ref, mine_ref, theirs_ref, o_ref):
    o_ref[...] = (mine_ref[...].astype(F32) + theirs_ref[...].astype(F32)).astype(o_ref.dtype)


def _pair_add(c_arr, g4, recv, name):
    _, _, hr, cols = g4.shape
    tr = min(hr, 256)
    grid_spec = pltpu.PrefetchScalarGridSpec(
        num_scalar_prefetch=1, grid=(N_CHIPS, hr // tr),
        in_specs=[pl.BlockSpec((None, None, tr, cols), lambda s, i, c: (s, c[0], i, 0)),
                  pl.BlockSpec((None, tr, cols), lambda s, i, c: (s, i, 0))],
        out_specs=pl.BlockSpec((None, tr, cols), lambda s, i, c: (s, i, 0)))
    return _pcall(
        _pair_add_body, name=name, grid_spec=grid_spec,
        out_shape=jax.ShapeDtypeStruct(recv.shape, recv.dtype),
        compiler_params=pltpu.CompilerParams(dimension_semantics=("parallel", "parallel")),
    )(c_arr, g4, recv)


def _sum4_body(me_ref, p_ref, l0, l1, l2, l3, o_ref):
    me = me_ref[0]
    t = [jnp.where(me == j, p_ref[...], l[...]).astype(F32) for j, l in enumerate((l0, l1, l2, l3))]
    o_ref[...] = ((t[0] + t[1]) + t[2]) + t[3]


def _sum4(me_arr, part, landed, name):
    _, hr, cols = part.shape
    tr = min(hr, 256)

    def slot(j):
        return lambda i, me: (jnp.where(me[0] == j, (j + 1) % N_CHIPS, j), i, 0)

    grid_spec = pltpu.PrefetchScalarGridSpec(
        num_scalar_prefetch=1, grid=(hr // tr,),
        in_specs=[pl.BlockSpec((None, tr, cols), lambda i, me: (me[0], i, 0))]
        + [pl.BlockSpec((None, tr, cols), slot(j)) for j in range(N_CHIPS)],
        out_specs=pl.BlockSpec((tr, cols), lambda i, me: (i, 0)))
    return _pcall(
        _sum4_body, name=name, grid_spec=grid_spec,
        out_shape=jax.ShapeDtypeStruct((hr, cols), F32),
        compiler_params=pltpu.CompilerParams(dimension_semantics=("parallel",)),
    )(me_arr, part, landed, landed, landed, landed)


def _pair_swap_body(*refs, n_w):
    ins, outs = refs[:n_w], refs[n_w:2 * n_w]
    send_sems, recv_sems = refs[2 * n_w:]
    x, y, c, _ = _place()
    todo = []
    for w in range(n_w):
        cp = pltpu.make_async_remote_copy(
            src_ref=ins[w], dst_ref=outs[w],
            send_sem=send_sems.at[w], recv_sem=recv_sems.at[w],
            device_id=(x, y, 1 - c), device_id_type=MESH)
        cp.start()
        todo.append(cp)
    for t in todo:
        t.wait()


def _pair_swap(halves, tag):
    n_w = len(halves)
    return _pcall(
        functools.partial(_pair_swap_body, n_w=n_w), name="grad_pair_swap_" + tag,
        in_specs=[ANY] * n_w, out_specs=[ANY] * n_w,
        out_shape=[jax.ShapeDtypeStruct(h.shape, h.dtype) for h in halves],
        scratch_shapes=[pltpu.SemaphoreType.DMA((n_w,))] * 2,
    )(*halves)


def _small_gather_body(x_ref, out_ref, send_sems, recv_sems, local_sem):
    m_per = x_ref.shape[0]
    x, y, c, chips = _place()
    me, sibling = (x, y, c), (x, y, 1 - c)

    def rows(px, py, pc):
        return out_ref.at[pl.ds((4 * px + 2 * py + pc) * m_per, m_per), :]

    def copy(k, block, to, src=None):
        return pltpu.make_async_remote_copy(
            src_ref=rows(*block) if src is None else src, dst_ref=rows(*block),
            send_sem=send_sems.at[k], recv_sem=recv_sems.at[k], device_id=to, device_id_type=MESH)

    mine = pltpu.make_async_copy(x_ref, rows(*me), local_sem)
    mine.start()
    first = [copy(0, me, sibling, src=x_ref)]
    first += [copy(1 + j, me, (*chip, c), src=x_ref) for j, chip in enumerate(chips)]
    for cp in first:
        cp.start()
    passed = [copy(4 + j, (*chip, c), sibling) for j, chip in enumerate(chips)]
    for j, chip in enumerate(chips):
        copy(1 + j, (*chip, c), me).wait_recv()
        passed[j].start()
    copy(0, sibling, me).wait_recv()
    for j, chip in enumerate(chips):
        copy(4 + j, (*chip, 1 - c), me).wait_recv()
    for cp in first + passed:
        cp.wait_send()
    mine.wait()


def _small_gather(small):
    m_per, n = small.shape
    return _pcall(
        _small_gather_body, name="small_allgather",
        out_shape=jax.ShapeDtypeStruct((N_DEV * m_per, n), small.dtype),
        in_specs=[pl.BlockSpec(memory_space=pltpu.VMEM)],
        out_specs=pl.BlockSpec(memory_space=pltpu.VMEM),
        scratch_shapes=[pltpu.SemaphoreType.DMA((7,)), pltpu.SemaphoreType.DMA((7,)), pltpu.SemaphoreType.DMA],
    )(small)


def _adamw(w, g, m, v):
    m = ADAM_B1 * m + (1.0 - ADAM_B1) * g
    v = ADAM_B2 * v + (1.0 - ADAM_B2) * (g * g)
    m_hat = m / (1.0 - ADAM_B1 ** ADAM_STEP)
    v_hat = v / (1.0 - ADAM_B2 ** ADAM_STEP)
    delta = -ADAM_LR * (m_hat / (jnp.sqrt(v_hat) + ADAM_EPS) + ADAM_WD * w)
    return delta, m, v


def _adamw_body(c_ref, w_ref, own_ref, sib_ref, m_ref, v_ref, g_ref, d_ref, nm_ref, nv_ref, *, nh):
    mine = (pl.program_id(0) // nh) == c_ref[0]
    g = jnp.where(mine, own_ref[...], sib_ref[...])
    g_ref[...] = g
    d, m, v = _adamw(w_ref[...], g, m_ref[...], v_ref[...])
    d_ref[...] = d
    nm_ref[...] = m
    nv_ref[...] = v


def _adamw_call(c_arr, w, own, sib, m, v, name):
    rows, cols = w.shape
    tr = min(rows // 2, 256)
    nh = (rows // 2) // tr
    full = pl.BlockSpec((tr, cols), lambda i, c: (i, 0))
    own_spec = pl.BlockSpec((tr, cols), lambda i, c: (jnp.clip(i - c[0] * nh, 0, nh - 1), 0))
    sib_spec = pl.BlockSpec((tr, cols), lambda i, c: (jnp.clip(i - (1 - c[0]) * nh, 0, nh - 1), 0))
    grid_spec = pltpu.PrefetchScalarGridSpec(
        num_scalar_prefetch=1, grid=(rows // tr,),
        in_specs=[full, own_spec, sib_spec, full, full], out_specs=[full] * 4)
    return _pcall(
        functools.partial(_adamw_body, nh=nh), name=name, grid_spec=grid_spec,
        out_shape=[jax.ShapeDtypeStruct(w.shape, F32)] * 4,
        compiler_params=pltpu.CompilerParams(dimension_semantics=("parallel",)),
    )(c_arr, w, own, sib, m, v)


def _small_update_body(gath_ref, w_ref, m_ref, v_ref, g_ref, d_ref, nm_ref, nv_ref, loss_ref, *, n_gain):
    tot = gath_ref[0:1, :]
    for i in range(1, gath_ref.shape[0]):
        tot = tot + gath_ref[i:i + 1, :]
    g = tot[:, 0:n_gain]
    g_ref[...] = g
    d, m, v = _adamw(w_ref[...], g, m_ref[...], v_ref[...])
    d_ref[...] = d
    nm_ref[...] = m
    nv_ref[...] = v
    loss_ref[...] = (0.5 / D_MODEL) * jnp.sum(tot[:, n_gain:n_gain + HEAD], axis=1, keepdims=True) * jnp.ones((1, HEAD), F32)


def _small_update(gath, w, m, v):
    n_gain = w.shape[1]
    vm = pl.BlockSpec(memory_space=pltpu.VMEM)
    return _pcall(
        functools.partial(_small_update_body, n_gain=n_gain), name="gain_update",
        in_specs=[vm] * 4, out_specs=[vm] * 5,
        out_shape=[jax.ShapeDtypeStruct((1, n_gain), F32)] * 4 + [jax.ShapeDtypeStruct((1, HEAD), F32)],
    )(gath, w, m, v)


def kernel(x, positions, norm_attn_pre, norm_attn_post, w_in, q_latent_norm, kv_latent_norm, w_uq, w_ukv, w_out, norm_mlp_pre, norm_mlp_post, w_up, w_down, loss_target, m_norm_attn_pre, m_norm_attn_post, m_w_in, m_q_latent_norm, m_kv_latent_norm, m_w_uq, m_w_ukv, m_w_out, m_norm_mlp_pre, m_norm_mlp_post, m_w_up, m_w_down, v_norm_attn_pre, v_norm_attn_post, v_w_in, v_q_latent_norm, v_kv_latent_norm, v_w_uq, v_w_ukv, v_w_out, v_norm_mlp_pre, v_norm_mlp_post, v_w_up, v_w_down):
    T = x.shape[1]
    c_arr = lax.axis_index("c").astype(jnp.int32).reshape(1)
    me_arr = (2 * lax.axis_index("x") + lax.axis_index("y")).astype(jnp.int32).reshape(1)
    names = ["w_in", "w_uq", "w_ukv", "w_out", "w_up", "w_down"]

    mats = [w_in[0], w_uq[0], w_ukv[0], w_out[0], w_up[0], w_down[0]]
    in_send, in_recv, in_bufs, in_started = _ag_start([_cast_place(me_arr, mats[0], "cast_w_in")], None, "in")
    placed = [_cast_place(me_arr, w, "cast_" + n) for w, n in zip(mats[1:], names[1:])]
    att_send, att_recv, att_bufs, att_started = _ag_start(placed[:3], in_started, "attn")
    mlp_send, mlp_recv, mlp_bufs, started = _ag_start(placed[3:], att_started, "mlp")

    col_major = lambda g: jnp.transpose(g, (1, 0, 2)).reshape(g.shape[1], N_CHIPS * g.shape[2])
    cast = lambda a: a.astype(MXU_DTYPE)
    to_shards = lambda g: jnp.transpose(g.reshape(g.shape[0], N_CHIPS, g.shape[1] // N_CHIPS), (1, 0, 2))
    halved = lambda g: g.reshape(N_CHIPS, 2, g.shape[1] // 2, g.shape[2])

    def pair_sum(full4, ns):
        from_sib = _pair_send(full4, "_".join(ns))
        return [_pair_add(c_arr, g4, r, "pair_add_" + n) for g4, r, n in zip(full4, from_sib, ns)]

    def in_weights(after):
        (win_g,) = _ag_forward(_ag_wait(in_send, in_recv, in_bufs, after, "in"), "in")
        return cast(jnp.pad(col_major(win_g), ((0, 0), (0, PROJ_COLS - IN_COLS))))

    def attn_weights(after):
        wuq_g, wukv_g, wout_g = _ag_forward(_ag_wait(att_send, att_recv, att_bufs, after, "attn"), "attn")
        wuq_full = col_major(wuq_g).reshape(LORA, NH, HEAD + ROPE_B)
        w_uq_p = jnp.pad(wuq_full, ((0, 0), (0, 0), (0, QPAD - HEAD - ROPE_B))).reshape(LORA, NH * QPAD)
        w_ukv_p = col_major(wukv_g).reshape(LORA, NH, 2, HEAD).transpose(0, 2, 1, 3).reshape(LORA, 2 * A_W)
        return cast(w_uq_p), cast(w_ukv_p), cast(wout_g.reshape(2 * A_W, D_MODEL))

    in_flight = {}

    def mlp_prefetch(after):
        in_flight["fw"] = _fw_start(_ag_wait(mlp_send, mlp_recv, mlp_bufs, after, "mlp"), "mlp")

    def mlp_weights(after):
        f_send, f_recv, bufs, _ = in_flight["fw"]
        wup_g, wdown_g = _fw_wait(f_send, f_recv, bufs, after, "mlp")
        return cast(wup_g), cast(wdown_g.reshape(D_FF, D_MODEL))

    def mlp_grads_ready(gw_up, gw_down):
        parts = pair_sum([halved(gw_up), halved(gw_down.reshape(N_CHIPS, D_MODEL, D_MODEL))], names[4:])
        in_flight["mlp"] = _scatter_start(parts, started, "mlp")
        return in_flight["mlp"][-1][0:1, 0:1]

    def attn_grads_ready(gw_out, gw_uq_p, gw_ukv_p):
        gw_uq = to_shards(gw_uq_p.reshape(LORA, NH, QPAD)[:, :, :HEAD + ROPE_B].reshape(LORA, NH * (HEAD + ROPE_B)))
        gw_ukv = to_shards(gw_ukv_p.reshape(LORA, 2, NH, HEAD).transpose(0, 2, 1, 3).reshape(LORA, 2 * A_W))
        parts = pair_sum([halved(g) for g in (gw_uq, gw_ukv, gw_out.reshape(N_CHIPS, LORA, D_MODEL))], names[1:4])
        in_flight["attn"] = _scatter_start(parts, in_flight["mlp"][-1], "attn")
        return in_flight["attn"][-1][0:1, 0:1]

    dx, gw_proj, small = _local_step(
        x[0], positions[0].astype(F32).reshape(T, 1), loss_target[0],
        norm_attn_pre + started[0:1, 0:1], norm_attn_post, q_latent_norm, kv_latent_norm, norm_mlp_pre, norm_mlp_post,
        in_weights, attn_weights, mlp_prefetch, mlp_weights, mlp_grads_ready, attn_grads_ready)

    ms = [m_w_in[0], m_w_uq[0], m_w_ukv[0], m_w_out[0], m_w_up[0], m_w_down[0]]
    vs = [v_w_in[0], v_w_uq[0], v_w_ukv[0], v_w_out[0], v_w_up[0], v_w_down[0]]

    def finish(parts, landed, lo, hi, tag):
        halves = [_sum4(me_arr, p, l, "chip_sum_" + n) for p, l, n in zip(parts, landed, names[lo:hi])]
        from_sib2 = _pair_swap(halves, tag)
        return [_adamw_call(c_arr, w, own, sib, m, v, "adamw_" + n)
                for w, own, sib, m, v, n in zip(mats[lo:hi], halves, from_sib2, ms[lo:hi], vs[lo:hi], names[lo:hi])]

    gw_in = to_shards(gw_proj[:, :IN_COLS])
    i_send, i_recv, parts_in, lands_in, in_going = _scatter_start(pair_sum([halved(gw_in)], names[:1]), None, "in")
    a_send, a_recv, parts_att, lands_att, _ = in_flight["attn"]
    parts_att, landed_att = _scatter_wait(a_send, a_recv, parts_att, lands_att, in_going, "attn")
    m_send, m_recv, parts_mlp, lands_mlp, _ = in_flight["mlp"]
    parts_mlp, landed_mlp = _scatter_wait(m_send, m_recv, parts_mlp, lands_mlp, landed_att[0], "mlp")
    upd_rest = finish(parts_att + parts_mlp, landed_att + landed_mlp, 1, 6, "rest")
    parts_in, landed_in = _scatter_wait(i_send, i_recv, parts_in, lands_in, upd_rest[-1][0], "in")
    upd = finish(parts_in, landed_in, 0, 1, "in") + upd_rest
    grads = [u[0] for u in upd]

    gath = _small_gather(small)
    gains = [norm_attn_pre, norm_attn_post, q_latent_norm, kv_latent_norm, norm_mlp_pre, norm_mlp_post]
    gm = [m_norm_attn_pre, m_norm_attn_post, m_q_latent_norm, m_kv_latent_norm, m_norm_mlp_pre, m_norm_mlp_post]
    gv = [v_norm_attn_pre, v_norm_attn_post, v_q_latent_norm, v_kv_latent_norm, v_norm_mlp_pre, v_norm_mlp_post]
    cat = lambda xs: jnp.concatenate(xs, axis=1)
    g_s, d_s, m_s, v_s, loss_v = _small_update(gath, cat(gains), cat(gm), cat(gv))
    widths = [a.shape[1] for a in gains]
    offs = [sum(widths[:i]) for i in range(len(widths))]
    split = lambda a: [a[:, o:o + w] for o, w in zip(offs, widths)]
    g_gain, d_gain, m_gain, v_gain = split(g_s), split(d_s), split(m_s), split(v_s)

    def ordered(gain_list, mat_list):
        gl, ml = gain_list, [a[None] for a in mat_list]
        return [gl[0], gl[1], ml[0], gl[2], gl[3], ml[1], ml[2], ml[3], gl[4], gl[5], ml[4], ml[5]]

    loss = loss_v[0, 0]
    return (loss, dx[None],
            *ordered(g_gain, grads),
            *ordered(d_gain, [u[1] for u in upd]),
            *ordered(m_gain, [u[2] for u in upd]),
            *ordered(v_gain, [u[3] for u in upd]))
```

```python
import functools

import jax
import jax.numpy as jnp
from jax import lax
from jax.experimental import pallas as pl
from jax.experimental.pallas import tpu as pltpu

F32 = jnp.float32
BF16 = jnp.bfloat16
MXU_DTYPE = jnp.bfloat16
WIRE_DTYPE = jnp.bfloat16

D_MODEL = 2048
HEAD = 128
NH = 8
A_W = NH * HEAD
LORA = 512
ROPE_B = 64
QPAD = 256
MAIN_COLS = 3 * A_W + 2 * LORA
IN_COLS = MAIN_COLS + ROPE_B
PROJ_COLS = MAIN_COLS + HEAD
PROJ_TILE = PROJ_COLS // 3
D_FF = 4 * D_MODEL
DIL = (1, 4, 16)
ROT_A = 32
ROPE_THETA = 500000.0
EPS = 1e-6
NEG = -1e30
N_CHIPS = 4
N_DEV = 8

ADAM_LR = 0.001
ADAM_B1 = 0.9
ADAM_B2 = 0.999
ADAM_EPS = 1e-08
ADAM_WD = 0.01
ADAM_STEP = 10

MESH = pl.DeviceIdType.MESH
ANY = pl.BlockSpec(memory_space=pl.ANY)


def _pcall(body, **kw):
    return pl.pallas_call(body, **kw)


_DIMS = {
    "nn": (((1,), (0,)), ((), ())),
    "nt": (((1,), (1,)), ((), ())),
    "tn": (((0,), (0,)), ((), ())),
}


def _mm_body(*refs, dims, nk, epi, n_extra, n_out):
    a_ref, b_ref = refs[0], refs[1]
    extra = refs[2:2 + n_extra]
    outs = refs[2 + n_extra:2 + n_extra + n_out]
    part = lax.dot_general(a_ref[...], b_ref[...], _DIMS[dims], preferred_element_type=F32)

    def finish(acc):
        res = epi(acc, *[r[...] for r in extra]) if epi is not None else (acc,)
        for o_ref, o in zip(outs, res):
            o_ref[...] = o.astype(o_ref.dtype)

    if nk == 1:
        finish(part)
        return
    acc_ref = refs[-1]
    k = pl.program_id(2)

    @pl.when(k == 0)
    def _():
        acc_ref[...] = part

    @pl.when(k > 0)
    def _():
        acc_ref[...] += part

    @pl.when(k == nk - 1)
    def _():
        finish(acc_ref[...])


def _matmul(a, b, *, dims, out_dtypes, tm, tn, tk, name, epi=None, extras=(), row_extras=(), b_outer=False,
            b_shards=0, out_shards=0):
    if b_shards:
        assert dims in ("nn", "nt") and b.shape[0] == b_shards
        b2 = (b.shape[1], b_shards * b.shape[2])
    else:
        b2 = b.shape
    if dims == "nn":
        (M, K), (K2, N) = a.shape, b2
    elif dims == "nt":
        (M, K), (N, K2) = a.shape, b2
    else:
        (K, M), (K2, N) = a.shape, b2
    assert K == K2, (a.shape, b.shape, dims)
    tm, tn, tk = min(tm, M), min(tn, N), min(tk, K)
    assert M % tm == 0 and N % tn == 0 and K % tk == 0, (name, M, N, K, tm, tn, tk)
    nk = K // tk

    def at(f):
        if b_outer:
            return lambda j, i, k: f(i, j, k)
        return f

    a_spec = {"nn": pl.BlockSpec((tm, tk), at(lambda i, j, k: (i, k))),
              "nt": pl.BlockSpec((tm, tk), at(lambda i, j, k: (i, k))),
              "tn": pl.BlockSpec((tk, tm), at(lambda i, j, k: (k, i)))}[dims]
    b_spec = {"nn": pl.BlockSpec((tk, tn), at(lambda i, j, k: (k, j))),
              "nt": pl.BlockSpec((tn, tk), at(lambda i, j, k: (j, k))),
              "tn": pl.BlockSpec((tk, tn), at(lambda i, j, k: (k, j)))}[dims]
    if b_shards:
        per = b.shape[2] // (tn if dims == "nn" else tk)
        assert per >= 1 and b.shape[2] % (tn if dims == "nn" else tk) == 0
        b_spec = {"nn": pl.BlockSpec((None, tk, tn), at(lambda i, j, k: (j // per, k, j % per))),
                  "nt": pl.BlockSpec((None, tn, tk), at(lambda i, j, k: (k // per, j, k % per)))}[dims]
    o_spec = pl.BlockSpec((tm, tn), at(lambda i, j, k: (i, j)))
    o_shape = (M, N)
    if out_shards:
        assert not extras and N % out_shards == 0 and (N // out_shards) % tn == 0
        o_per = (N // out_shards) // tn
        o_spec = pl.BlockSpec((None, tm, tn), at(lambda i, j, k: (j // o_per, i, j % o_per)))
        o_shape = (out_shards, M, N // out_shards)
    r_specs = [pl.BlockSpec((tm, r.shape[1]), at(lambda i, j, k: (i, 0))) for r in row_extras]
    body = functools.partial(_mm_body, dims=dims, nk=nk, epi=epi,
                             n_extra=len(extras) + len(row_extras), n_out=len(out_dtypes))
    res = _pcall(
        body, name=name,
        grid=(N // tn, M // tm, nk) if b_outer else (M // tm, N // tn, nk),
        in_specs=[a_spec, b_spec] + [o_spec] * len(extras) + r_specs,
        out_specs=[o_spec] * len(out_dtypes),
        out_shape=[jax.ShapeDtypeStruct(o_shape, dt) for dt in out_dtypes],
        scratch_shapes=[pltpu.VMEM((tm, tn), F32)] if nk > 1 else [],
        compiler_params=pltpu.CompilerParams(
            dimension_semantics=("parallel", "parallel", "arbitrary")),
    )(a, b, *extras, *row_extras)
    return list(res)


def _rowwise(body, row_ins, vec_ins, row_outs, acc_outs, *, tr, name):
    T = row_ins[0].shape[0]
    tr = min(tr, T)
    assert T % tr == 0
    in_specs = [pl.BlockSpec((tr, a.shape[1]), lambda i: (i, 0)) for a in row_ins]
    in_specs += [pl.BlockSpec(a.shape, lambda i: (0, 0)) for a in vec_ins]
    out_specs = [pl.BlockSpec((tr, w), lambda i: (i, 0)) for (w, _) in row_outs]
    out_specs += [pl.BlockSpec(s, lambda i: (0, 0)) for s in acc_outs]
    out_shape = [jax.ShapeDtypeStruct((T, w), dt) for (w, dt) in row_outs]
    out_shape += [jax.ShapeDtypeStruct(s, F32) for s in acc_outs]
    sem = "arbitrary" if acc_outs else "parallel"
    return list(_pcall(
        body, name=name, grid=(T // tr,), in_specs=in_specs, out_specs=out_specs,
        out_shape=out_shape,
        compiler_params=pltpu.CompilerParams(dimension_semantics=(sem,)),
    )(*row_ins, *vec_ins))


def _rstd(x):
    return lax.rsqrt(jnp.mean(x * x, axis=-1, keepdims=True) + EPS)


def _rms_bwd(x, rstd, dyg):
    xh = x * rstd
    return rstd * (dyg - xh * jnp.mean(dyg * xh, axis=-1, keepdims=True)), xh


def _fold8(v):
    r, w = v.shape
    return jnp.sum(v.reshape(r // 8, 8, w), axis=0)


def _acc(ref, val):
    first = pl.program_id(0) == 0

    @pl.when(first)
    def _():
        ref[...] = val

    @pl.when(jnp.logical_not(first))
    def _():
        ref[...] += val


def _rope(x, c, sa, sb, half):
    return x * c + pltpu.roll(x, HEAD - half, 1) * sa + pltpu.roll(x, half, 1) * sb


def _rope_t(dy, c, sa, sb, half):
    return dy * c - pltpu.roll(dy, HEAD - half, 1) * sa - pltpu.roll(dy, half, 1) * sb


def _rope_tab_body(pos_ref, inv_ref, ca, saa, sab, cb, sba, sbb):
    pos = pos_ref[...]
    lane = lax.broadcasted_iota(jnp.int32, (pos.shape[0], HEAD), 1)
    ang_a = pos * inv_ref[0:1, :]
    ang_b = pos * inv_ref[1:2, :]
    c, s = jnp.cos(ang_a), jnp.sin(ang_a)
    ha = ROT_A // 2
    ca[...] = jnp.where(lane < ROT_A, c, 1.0)
    saa[...] = jnp.where(lane < ha, -s, 0.0)
    sab[...] = jnp.where((lane >= ha) & (lane < ROT_A), s, 0.0)
    c, s = jnp.cos(ang_b), jnp.sin(ang_b)
    hb = ROPE_B // 2
    cb[...] = jnp.where(lane < ROPE_B, c, 1.0)
    sba[...] = jnp.where(lane < hb, -s, 0.0)
    sbb[...] = jnp.where((lane >= hb) & (lane < ROPE_B), s, 0.0)


def _rms_fwd_body(x_ref, g_ref, h_ref):
    x = x_ref[...]
    h_ref[...] = ((x * _rstd(x)) * g_ref[...]).astype(h_ref.dtype)


def _postproj_body(p_ref, ca, saa, sab, cb, sba, sbb, gq_ref, gkv_ref,
                   q_ref, k_ref, v_ref, cqn_ref, ckvn_ref, krope_ref):
    c, sa, sb = ca[...], saa[...], sab[...]
    for h in range(NH):
        lo = h * HEAD
        q_ref[:, lo:lo + HEAD] = _rope(p_ref[:, lo:lo + HEAD], c, sa, sb, ROT_A // 2).astype(q_ref.dtype)
        k_ref[:, lo:lo + HEAD] = _rope(p_ref[:, A_W + lo:A_W + lo + HEAD], c, sa, sb, ROT_A // 2).astype(k_ref.dtype)
    v_ref[...] = p_ref[:, 2 * A_W:3 * A_W].astype(v_ref.dtype)
    cq = p_ref[:, 3 * A_W:3 * A_W + LORA]
    cqn_ref[...] = ((cq * _rstd(cq)) * gq_ref[...]).astype(cqn_ref.dtype)
    ckv = p_ref[:, 3 * A_W + LORA:MAIN_COLS]
    ckvn_ref[...] = ((ckv * _rstd(ckv)) * gkv_ref[...]).astype(ckvn_ref.dtype)
    krope_ref[...] = _rope(p_ref[:, MAIN_COLS:PROJ_COLS], cb[...], sba[...], sbb[...], ROPE_B // 2).astype(krope_ref.dtype)


def _mid_body(x_ref, o_ref, g2_ref, g3_ref, x1_ref, h2_ref):
    o = o_ref[...]
    x1 = x_ref[...] + (o * _rstd(o)) * g2_ref[...]
    x1_ref[...] = x1
    h2_ref[...] = ((x1 * _rstd(x1)) * g3_ref[...]).astype(h2_ref.dtype)


def _loss_body(x1_ref, d_ref, t_ref, g4_ref, dy_ref, dd_ref, loss_ref, dg4_ref):
    d = d_ref[...]
    rstd = _rstd(d)
    y = x1_ref[...] + (d * rstd) * g4_ref[...]
    e = y - t_ref[...]
    dy = e * (1.0 / D_MODEL)
    dy_ref[...] = dy
    dd, dh = _rms_bwd(d, rstd, dy * g4_ref[...])
    dd_ref[...] = dd.astype(dd_ref.dtype)
    _acc(dg4_ref, _fold8(dy * dh))
    e8 = _fold8(e * e)
    l = e8[:, 0:HEAD]
    for j in range(1, D_MODEL // HEAD):
        l = l + e8[:, j * HEAD:(j + 1) * HEAD]
    _acc(loss_ref, l)


def _bmid_body(dy_ref, dh2_ref, x1_ref, o_ref, g2_ref, g3_ref, dx1_ref, do_ref, dg3_ref, dg2_ref):
    x1 = x1_ref[...]
    dh2 = dh2_ref[...]
    dn, x1h = _rms_bwd(x1, _rstd(x1), dh2 * g3_ref[...])
    dx1 = dy_ref[...] + dn
    dx1_ref[...] = dx1
    _acc(dg3_ref, _fold8(dh2 * x1h))
    o = o_ref[...]
    do, oh = _rms_bwd(o, _rstd(o), dx1 * g2_ref[...])
    do_ref[...] = do.astype(do_ref.dtype)
    _acc(dg2_ref, _fold8(dx1 * oh))


def _dproj_body(dq_ref, dk_ref, dv_ref, dcq_ref, dckv_ref, p_ref, dkr_ref,
                ca, saa, sab, cb, sba, sbb, gq_ref, gkv_ref,
                dp_ref, dgq_ref, dgkv_ref):
    c, sa, sb = ca[...], saa[...], sab[...]
    for h in range(NH):
        lo = h * HEAD
        dp_ref[:, lo:lo + HEAD] = _rope_t(dq_ref[:, lo:lo + HEAD], c, sa, sb, ROT_A // 2).astype(dp_ref.dtype)
        dp_ref[:, A_W + lo:A_W + lo + HEAD] = _rope_t(dk_ref[:, lo:lo + HEAD], c, sa, sb, ROT_A // 2).astype(dp_ref.dtype)
    dp_ref[:, 2 * A_W:3 * A_W] = dv_ref[...].astype(dp_ref.dtype)
    cq = p_ref[:, 3 * A_W:3 * A_W + LORA]
    dcqn = dcq_ref[...]
    dcq, cqh = _rms_bwd(cq, _rstd(cq), dcqn * gq_ref[...])
    dp_ref[:, 3 * A_W:3 * A_W + LORA] = dcq.astype(dp_ref.dtype)
    _acc(dgq_ref, _fold8(dcqn * cqh))
    ckv = p_ref[:, 3 * A_W + LORA:MAIN_COLS]
    dckvn = dckv_ref[...]
    dckv, ckvh = _rms_bwd(ckv, _rstd(ckv), dckvn * gkv_ref[...])
    dp_ref[:, 3 * A_W + LORA:MAIN_COLS] = dckv.astype(dp_ref.dtype)
    _acc(dgkv_ref, _fold8(dckvn * ckvh))
    dkr = dkr_ref[:, 0:HEAD]
    for h in range(1, NH):
        dkr = dkr + dkr_ref[:, h * HEAD:(h + 1) * HEAD]
    dp_ref[:, MAIN_COLS:PROJ_COLS] = _rope_t(dkr, cb[...], sba[...], sbb[...], ROPE_B // 2).astype(dp_ref.dtype)


def _bin_body(dx1_ref, dh_ref, x_ref, g1_ref, dx_ref, dg1_ref):
    x = x_ref[...]
    dh = dh_ref[...]
    dn, xh = _rms_bwd(x, _rstd(x), dh * g1_ref[...])
    dx_ref[...] = dx1_ref[...] + dn
    _acc(dg1_ref, _fold8(dh * xh))


def _dot_nt(a, b):
    return lax.dot_general(a, b, _DIMS["nt"], preferred_element_type=F32)


def _dot_tn(a, b):
    return lax.dot_general(a, b, _DIMS["tn"], preferred_element_type=F32)


def _dot_nn(a, b):
    return jnp.dot(a, b, preferred_element_type=F32)


DIL_SCALE = HEAD ** -0.5
DIL_CHUNK = 256


def _dil_rows(t, d):
    r = t & (d - 1)
    n = t >> (d.bit_length() - 1)
    start = r + n * (HEAD * d)
    has_prev = n > 0
    pstart = jnp.where(has_prev, start - HEAD * d, start)
    if d == 1:
        return pl.ds(pl.multiple_of(start, HEAD), HEAD), pl.ds(pl.multiple_of(pstart, HEAD), HEAD), has_prev
    return pl.ds(start, HEAD, stride=d), pl.ds(pstart, HEAD, stride=d), has_prev


def _dil_band():
    row = lax.broadcasted_iota(jnp.int32, (HEAD, 2 * HEAD), 0)
    col = lax.broadcasted_iota(jnp.int32, (HEAD, 2 * HEAD), 1)
    return (col >= row) & (col <= row + HEAD), col >= HEAD


def _dil_fwd_body(q_ref, k_ref, v_ref, a_ref, lse_ref, o1, o2, o3, l1, l2, l3, *, nt, unroll):
    band, is_cur = _dil_band()
    for d, o_sc, l_sc in zip(DIL, (o1, o2, o3), (l1, l2, l3)):

        def tile(t, carry, d=d, o_sc=o_sc, l_sc=l_sc):
            rows, prows, has_prev = _dil_rows(t, d)
            q = q_ref[rows, :].astype(MXU_DTYPE)
            kk = jnp.concatenate([k_ref[prows, :], k_ref[rows, :]], axis=0).astype(MXU_DTYPE)
            vv = jnp.concatenate([v_ref[prows, :], v_ref[rows, :]], axis=0).astype(MXU_DTYPE)
            ok = band & (is_cur | has_prev)
            s = jnp.where(ok, _dot_nt(q, kk) * DIL_SCALE, NEG)
            m = jnp.max(s, axis=1, keepdims=True)
            p = jnp.exp(s - m)
            den = jnp.sum(p, axis=1, keepdims=True)
            o_sc[rows, :] = _dot_nn((p / den).astype(MXU_DTYPE), vv)
            l_sc[rows, :] = jnp.broadcast_to(m + jnp.log(den), (HEAD, HEAD))
            return carry

        lax.fori_loop(0, nt, tile, 0, unroll=unroll)

    def merge(i, carry):
        rs = pl.ds(pl.multiple_of(i * DIL_CHUNK, DIL_CHUNK), DIL_CHUNK)
        la, lb, lc = l1[rs, :], l2[rs, :], l3[rs, :]
        m = jnp.maximum(jnp.maximum(la, lb), lc)
        wa, wb, wc = jnp.exp(la - m), jnp.exp(lb - m), jnp.exp(lc - m)
        den = wa + wb + wc
        a = (wa / den) * o1[rs, :] + (wb / den) * o2[rs, :] + (wc / den) * o3[rs, :]
        a_ref[rs, :] = a.astype(a_ref.dtype)
        lse_ref[rs, :] = m + jnp.log(den)
        return carry

    lax.fori_loop(0, q_ref.shape[0] // DIL_CHUNK, merge, 0)


def _dil_fwd(q, k, v):
    T = q.shape[0]
    spec = pl.BlockSpec((T, HEAD), lambda h: (0, h))
    return _pcall(
        functools.partial(_dil_fwd_body, nt=T // HEAD, unroll=16), name="dil_fwd",
        grid=(NH,), in_specs=[spec] * 3, out_specs=[spec] * 2,
        out_shape=[jax.ShapeDtypeStruct((T, 2 * A_W), MXU_DTYPE), jax.ShapeDtypeStruct((T, A_W), F32)],
        scratch_shapes=[pltpu.VMEM((T, HEAD), F32)] * 6,
        compiler_params=pltpu.CompilerParams(dimension_semantics=("parallel",)),
    )(q, k, v)


def _dil_bwd_body(q_ref, k_ref, v_ref, do_ref, a_ref, lse_ref, dq_ref, dk_ref, dv_ref, dl_sc, *, nt, unroll):
    band, is_cur = _dil_band()

    def prep(i, carry):
        rs = pl.ds(pl.multiple_of(i * DIL_CHUNK, DIL_CHUNK), DIL_CHUNK)
        dl = jnp.sum(do_ref[rs, :] * a_ref[rs, :].astype(F32), axis=1, keepdims=True)
        dl_sc[rs, :] = jnp.broadcast_to(dl, (DIL_CHUNK, HEAD))
        zero = jnp.zeros((DIL_CHUNK, HEAD), F32)
        dq_ref[rs, :] = zero
        dk_ref[rs, :] = zero
        dv_ref[rs, :] = zero
        return carry

    lax.fori_loop(0, q_ref.shape[0] // DIL_CHUNK, prep, 0)

    for d in DIL:

        def tile(t, carry, d=d):
            rows, prows, has_prev = _dil_rows(t, d)
            q = q_ref[rows, :].astype(MXU_DTYPE)
            kk = jnp.concatenate([k_ref[prows, :], k_ref[rows, :]], axis=0).astype(MXU_DTYPE)
            vv = jnp.concatenate([v_ref[prows, :], v_ref[rows, :]], axis=0).astype(MXU_DTYPE)
            do = do_ref[rows, :].astype(MXU_DTYPE)
            lse = lse_ref[rows, :]
            dl = dl_sc[rows, :]
            ok = band & (is_cur | has_prev)
            s = _dot_nt(q, kk) * DIL_SCALE
            p = jnp.where(ok, jnp.exp(s - jnp.concatenate([lse, lse], axis=1)), 0.0)
            ds = (p * (_dot_nt(do, vv) - jnp.concatenate([dl, dl], axis=1))).astype(MXU_DTYPE)
            dq_ref[rows, :] += _dot_nn(ds, kk) * DIL_SCALE
            dkk = _dot_tn(ds, q) * DIL_SCALE
            dvv = _dot_tn(p.astype(MXU_DTYPE), do)
            dk_ref[rows, :] += dkk[HEAD:, :]
            dv_ref[rows, :] += dvv[HEAD:, :]
            dk_ref[prows, :] += dkk[:HEAD, :]
            dv_ref[prows, :] += dvv[:HEAD, :]
            return carry

        lax.fori_loop(0, nt, tile, 0, unroll=unroll)


def _dil_bwd(q, k, v, dmix, mixed, lse):
    T = q.shape[0]
    spec = pl.BlockSpec((T, HEAD), lambda h: (0, h))
    return _pcall(
        functools.partial(_dil_bwd_body, nt=T // HEAD, unroll=8), name="dil_bwd",
        grid=(NH,), in_specs=[spec] * 6, out_specs=[spec] * 3,
        out_shape=[jax.ShapeDtypeStruct((T, A_W), F32)] * 3,
        scratch_shapes=[pltpu.VMEM((T, HEAD), F32)],
        compiler_params=pltpu.CompilerParams(dimension_semantics=("parallel",)),
    )(q, k, v, dmix, mixed, lse)


MLA_SCALE = (HEAD + ROPE_B) ** -0.5
LOG2E = 1.4426950408889634
MLA_QSCALE = MLA_SCALE * LOG2E
MLA_T = 512
MLA_HP = 2


def _tri(t):
    row = lax.broadcasted_iota(jnp.int32, (t, t), 0)
    col = lax.broadcasted_iota(jnp.int32, (t, t), 1)
    return col <= row


def _lanes(x, n):
    return jnp.tile(x, (1, n // HEAD))


def _mla_fwd_body(q_ref, kn_ref, kr_ref, v_ref, mixed_ref, o_ref, lse_ref, m_sc, l_sc, acc_sc, *, t, hp):
    del mixed_ref
    qi = pl.program_id(1)
    m_sc[...] = jnp.full(m_sc.shape, NEG, F32)
    l_sc[...] = jnp.zeros(l_sc.shape, F32)
    acc_sc[...] = jnp.zeros(acc_sc.shape, F32)

    def step(j, masked):
        ks = pl.ds(pl.multiple_of(j * t, t), t)
        kr = kr_ref[ks, :]
        for hh in range(hp):
            kcat = jnp.concatenate([kn_ref[ks, hh * HEAD:(hh + 1) * HEAD], kr], axis=1)
            s = _dot_nt(q_ref[:, hh * QPAD:(hh + 1) * QPAD], kcat)
            if masked:
                s = jnp.where(_tri(t), s, NEG)
            m_prev = m_sc[hh]
            m_new = jnp.maximum(m_prev, jnp.max(s, axis=1, keepdims=True))
            alpha = jnp.exp2(m_prev - m_new)
            p = jnp.exp2(s - _lanes(m_new, t))
            l_sc[hh] = alpha * l_sc[hh] + jnp.sum(p, axis=1, keepdims=True)
            acc_sc[hh] = alpha * acc_sc[hh] + _dot_nn(p.astype(MXU_DTYPE), v_ref[ks, hh * HEAD:(hh + 1) * HEAD])
            m_sc[hh] = m_new

    def off_diag(j, carry):
        step(j, False)
        return carry

    lax.fori_loop(0, qi, off_diag, 0)
    step(qi, True)
    for hh in range(hp):
        l = l_sc[hh]
        o_ref[:, hh * HEAD:(hh + 1) * HEAD] = (acc_sc[hh] / l).astype(o_ref.dtype)
        lse_ref[:, hh * HEAD:(hh + 1) * HEAD] = m_sc[hh] + jnp.log2(l)


def _mla_fwd(qf, kv, kr, mixed):
    T = qf.shape[0]
    t, hp = min(MLA_T, T), MLA_HP
    ng = NH // hp
    return _pcall(
        functools.partial(_mla_fwd_body, t=t, hp=hp), name="mla_fwd",
        grid=(ng, T // t),
        in_specs=[pl.BlockSpec((t, hp * QPAD), lambda g, i: (i, g)),
                  pl.BlockSpec((T, hp * HEAD), lambda g, i: (0, g)),
                  pl.BlockSpec((T, HEAD), lambda g, i: (0, 0)),
                  pl.BlockSpec((T, hp * HEAD), lambda g, i: (0, ng + g)), ANY],
        out_specs=[pl.BlockSpec((t, hp * HEAD), lambda g, i: (i, ng + g)),
                   pl.BlockSpec((t, hp * HEAD), lambda g, i: (i, g))],
        out_shape=[jax.ShapeDtypeStruct(mixed.shape, mixed.dtype), jax.ShapeDtypeStruct((T, A_W), F32)],
        input_output_aliases={4: 0},
        scratch_shapes=[pltpu.VMEM((hp, t, HEAD), F32)] * 3,
        compiler_params=pltpu.CompilerParams(dimension_semantics=("parallel", "parallel")),
    )(qf, kv, kr, kv, mixed)


def _mla_bwd_body(q_ref, kn_ref, kr_ref, v_ref, do_ref, o_ref, lse_ref, cb, sba, sbb,
                  dq_ref, dkn_ref, dv_ref, dkr_ref, dq_sc, dl_sc, dk_sc, dv_sc, *, t):
    ki = pl.program_id(1)
    nq = q_ref.shape[0] // t

    @pl.when(ki == 0)
    def _():
        def prep(i, carry):
            rs = pl.ds(pl.multiple_of(i * t, t), t)
            dl = jnp.sum(do_ref[rs, :] * o_ref[rs, :].astype(F32), axis=1, keepdims=True)
            dl_sc[rs, :] = jnp.broadcast_to(dl, (t, HEAD))
            dq_sc[rs, :] = jnp.zeros((t, QPAD), F32)
            return carry
        lax.fori_loop(0, nq, prep, 0)

    kcat = jnp.concatenate([kn_ref[...], kr_ref[...]], axis=1)
    v = v_ref[...]
    dk_sc[...] = jnp.zeros(dk_sc.shape, F32)
    dv_sc[...] = jnp.zeros(dv_sc.shape, F32)

    def step(i, masked):
        qs = pl.ds(pl.multiple_of(i * t, t), t)
        q = q_ref[qs, :]
        do = do_ref[qs, :].astype(MXU_DTYPE)
        p = jnp.exp2(_dot_nt(q, kcat) - _lanes(lse_ref[qs, :], t))
        if masked:
            p = jnp.where(_tri(t), p, 0.0)
        ds = (p * (_dot_nt(do, v) - _lanes(dl_sc[qs, :], t))).astype(MXU_DTYPE)
        dv_sc[...] += _dot_tn(p.astype(MXU_DTYPE), do)
        dk_sc[...] += _dot_tn(ds, q)
        dq_sc[qs, :] += _dot_nn(ds, kcat) * MLA_SCALE

    step(ki, True)

    def off_diag(i, carry):
        step(i, False)
        return carry

    lax.fori_loop(ki + 1, nq, off_diag, 0)
    dk = dk_sc[...] * (1.0 / LOG2E)
    dkn_ref[...] = dk[:, 0:HEAD].astype(dkn_ref.dtype)
    dkr_ref[...] = dk[:, HEAD:QPAD]
    dv_ref[...] = dv_sc[...].astype(dv_ref.dtype)

    @pl.when(ki == nq - 1)
    def _():
        def emit(i, carry):
            rs = pl.ds(pl.multiple_of(i * t, t), t)
            dq_ref[rs, 0:HEAD] = dq_sc[rs, 0:HEAD].astype(dq_ref.dtype)
            dq_ref[rs, HEAD:QPAD] = _rope_t(dq_sc[rs, HEAD:QPAD], cb[rs, :], sba[rs, :], sbb[rs, :],
                                            ROPE_B // 2).astype(dq_ref.dtype)
            return carry
        lax.fori_loop(0, nq, emit, 0)


def _mla_bwd(qf, kv, kr, dmix, mixed, lse, tabs_b):
    T = qf.shape[0]
    t = min(MLA_T, T)
    head = lambda h, j: (0, h)
    b_half = lambda h, j: (0, NH + h)
    kblk = pl.BlockSpec((t, HEAD), lambda h, j: (j, h))
    return _pcall(
        functools.partial(_mla_bwd_body, t=t), name="mla_bwd",
        grid=(NH, T // t),
        in_specs=[pl.BlockSpec((T, QPAD), head), kblk,
                  pl.BlockSpec((t, HEAD), lambda h, j: (j, 0)),
                  pl.BlockSpec((t, HEAD), lambda h, j: (j, NH + h)),
                  pl.BlockSpec((T, HEAD), b_half), pl.BlockSpec((T, HEAD), b_half),
                  pl.BlockSpec((T, HEAD), head)] + [pl.BlockSpec((T, HEAD), lambda h, j: (0, 0))] * 3,
        out_specs=[pl.BlockSpec((T, QPAD), head), kblk, kblk, kblk],
        out_shape=[jax.ShapeDtypeStruct((T, NH * QPAD), MXU_DTYPE), jax.ShapeDtypeStruct((T, A_W), MXU_DTYPE),
                   jax.ShapeDtypeStruct((T, A_W), MXU_DTYPE), jax.ShapeDtypeStruct((T, A_W), F32)],
        scratch_shapes=[pltpu.VMEM((T, QPAD), F32), pltpu.VMEM((T, HEAD), F32), pltpu.VMEM((t, QPAD), F32),
                        pltpu.VMEM((t, HEAD), F32)],
        compiler_params=pltpu.CompilerParams(dimension_semantics=("parallel", "arbitrary")),
    )(qf, kv, kr, kv, dmix, mixed, lse, *tabs_b)


def _local_step(x, pos, target, g1, g2, gq, gkv, g3, g4,
                in_weights, attn_weights, mlp_prefetch, mlp_weights, down_grad_ready, up_grad_ready, attn_grads_ready):
    T = x.shape[0]
    TR = 256
    mm = functools.partial(_matmul, tm=2048, tn=1024, tk=2048, b_outer=True)
    mm_k = functools.partial(_matmul, tm=1024, tn=1024, tk=2048)
    mm_g = functools.partial(_matmul, tm=1024, tn=1024, tk=4096, b_outer=True)

    inv_a = ROPE_THETA ** (-jnp.arange(0, ROT_A, 2, dtype=F32) / ROT_A)
    inv_b = ROPE_THETA ** (-jnp.arange(0, ROPE_B, 2, dtype=F32) / ROPE_B)
    inv = jnp.stack([jnp.concatenate([inv_a, inv_a, jnp.zeros((HEAD - ROT_A,), F32)]),
                     jnp.concatenate([inv_b, inv_b, jnp.zeros((HEAD - ROPE_B,), F32)])])
    inv = jnp.concatenate([inv, jnp.zeros((6, HEAD), F32)], axis=0)
    tabs = _rowwise(_rope_tab_body, [pos], [inv], [(HEAD, F32)] * 6, [], tr=512, name="rope_tables")

    (h,) = _rowwise(_rms_fwd_body, [x], [g1], [(D_MODEL, MXU_DTYPE)], [], tr=TR, name="rms_in")
    w_proj = in_weights(h)
    (proj,) = mm(h, w_proj, dims="nn", out_dtypes=[F32], tm=1024, tn=PROJ_TILE, name="proj_in")
    q, k, v, cqn, ckvn, krope = _rowwise(
        _postproj_body, [proj] + tabs, [gq, gkv],
        [(A_W, F32)] * 3 + [(LORA, MXU_DTYPE)] * 2 + [(HEAD, MXU_DTYPE)], [], tr=TR, name="post_proj")
    mixed, lse_a = _dil_fwd(q, k, v)

    w_uq_p, w_ukv_p, w_out = attn_weights(cqn)

    def q_epi(acc, cb, sba, sbb):
        cols = []
        for hh in range(acc.shape[1] // QPAD):
            lo = hh * QPAD
            cols += [acc[:, lo:lo + HEAD], _rope(acc[:, lo + HEAD:lo + QPAD], cb, sba, sbb, ROPE_B // 2)]
        return (jnp.concatenate(cols, axis=1) * MLA_QSCALE,)
    (qf,) = mm(cqn, w_uq_p, dims="nn", out_dtypes=[MXU_DTYPE], name="q_up", epi=q_epi, row_extras=tuple(tabs[3:]))
    (kv,) = mm(ckvn, w_ukv_p, dims="nn", out_dtypes=[MXU_DTYPE], name="kv_up")
    mixed, lse_b = _mla_fwd(qf, kv, krope, mixed)
    mlp_prefetch(mixed)

    (o,) = mm(mixed, w_out, dims="nn", out_dtypes=[F32], name="out_proj")
    x1, h2 = _rowwise(_mid_body, [x, o], [g2, g3], [(D_MODEL, F32), (D_MODEL, MXU_DTYPE)], [], tr=TR, name="mid_norm")

    w_up, w_down = mlp_weights(h2)

    def up_epi(acc):
        r = jnp.maximum(acc, 0.0)
        return r * r, r
    u, r = mm(h2, w_up, dims="nn", out_dtypes=[MXU_DTYPE, MXU_DTYPE], name="mlp_up", epi=up_epi, b_shards=N_CHIPS)
    (dn,) = mm_k(u, w_down, dims="nn", out_dtypes=[F32], name="mlp_down")
    dy, dd, loss8, dg4 = _rowwise(_loss_body, [x1, dn, target], [g4], [(D_MODEL, F32), (D_MODEL, MXU_DTYPE)],
                                  [(8, HEAD), (8, D_MODEL)], tr=TR, name="loss_head")

    def dup_epi(acc, rr):
        return (acc * (2.0 * rr.astype(F32)),)
    (dup,) = mm(dd, w_down, dims="nt", out_dtypes=[MXU_DTYPE], name="d_up", epi=dup_epi, extras=(r,))
    (gw_down,) = mm_g(u, dd, dims="tn", out_dtypes=[WIRE_DTYPE], name="gw_down")
    down_grad_ready(gw_down)
    (dh2,) = mm_k(dup, w_up, dims="nt", out_dtypes=[F32], name="d_h2", b_shards=N_CHIPS)
    (gw_up,) = mm_g(h2, dup, dims="tn", out_dtypes=[WIRE_DTYPE], name="gw_up", out_shards=N_CHIPS)
    g2 = g2 + up_grad_ready(gw_up)
    dx1, do, dg3, dg2 = _rowwise(_bmid_body, [dy, dh2, x1, o], [g2, g3], [(D_MODEL, F32), (D_MODEL, MXU_DTYPE)],
                                 [(8, D_MODEL), (8, D_MODEL)], tr=TR, name="bwd_mid")
    (dmix,) = mm(do, w_out, dims="nt", out_dtypes=[F32], name="d_mixed")
    (gw_out,) = mm_g(mixed, do, dims="tn", out_dtypes=[WIRE_DTYPE], name="gw_out")

    dq_pad, dkn, dvb, dkr = _mla_bwd(qf, kv, krope, dmix, mixed, lse_b, tabs[3:])
    (dcqn,) = mm(dq_pad, w_uq_p, dims="nt", out_dtypes=[F32], name="d_cq")
    (gw_uq_p,) = mm_g(cqn, dq_pad, dims="tn", out_dtypes=[WIRE_DTYPE], name="gw_uq")
    dkv = jnp.concatenate([dkn, dvb], axis=1)
    (dckvn,) = mm(dkv, w_ukv_p, dims="nt", out_dtypes=[F32], name="d_ckv")
    (gw_ukv_p,) = mm_g(ckvn, dkv, dims="tn", out_dtypes=[WIRE_DTYPE], name="gw_ukv")
    gq = gq + attn_grads_ready(gw_out, gw_uq_p, gw_ukv_p)

    dq_a, dk_a, dv_a = _dil_bwd(q, k, v, dmix, mixed, lse_a)
    dproj, dgq, dgkv = _rowwise(
        _dproj_body, [dq_a, dk_a, dv_a, dcqn, dckvn, proj, dkr] + tabs, [gq, gkv],
        [(PROJ_COLS, MXU_DTYPE)], [(8, LORA), (8, LORA)], tr=TR, name="d_proj")
    (dh,) = mm_k(dproj, w_proj, dims="nt", out_dtypes=[F32], tk=PROJ_TILE, name="d_h")
    (gw_proj,) = mm_g(h, dproj, dims="tn", out_dtypes=[WIRE_DTYPE], tn=PROJ_TILE, name="gw_in")
    dx, dg1 = _rowwise(_bin_body, [dx1, dh, x], [g1], [(D_MODEL, F32)], [(8, D_MODEL)], tr=TR, name="bwd_in")

    small = jnp.concatenate([dg1, dg2, dgq, dgkv, dg3, dg4, loss8], axis=1)
    return dx, gw_proj, small


def _place():
    x, y, c = lax.axis_index("x"), lax.axis_index("y"), lax.axis_index("c")
    chips = [(1 - x, y), (x, 1 - y), (1 - x, 1 - y)]
    return x, y, c, chips


def _cast_place_body(me_ref, w_ref, o_ref):
    o_ref[...] = w_ref[...].astype(o_ref.dtype)


def _cast_place(me_arr, w, name):
    rows, cols = w.shape
    tr = min(rows, 256)
    grid_spec = pltpu.PrefetchScalarGridSpec(
        num_scalar_prefetch=1, grid=(rows // tr,),
        in_specs=[pl.BlockSpec((tr, cols), lambda i, me: (i, 0))],
        out_specs=pl.BlockSpec((None, tr, cols), lambda i, me: (me[0], i, 0)))
    return _pcall(
        _cast_place_body, name=name, grid_spec=grid_spec,
        out_shape=jax.ShapeDtypeStruct((N_CHIPS, rows, cols), WIRE_DTYPE),
        compiler_params=pltpu.CompilerParams(dimension_semantics=("parallel",)),
    )(me_arr, w)


HBM = pl.BlockSpec(memory_space=pltpu.HBM)
SEM = pl.BlockSpec(memory_space=pltpu.SEMAPHORE)
EFFECT = pltpu.SideEffectType.DATAFLOW_SIDE_EFFECTING


def _copy_start(make, arrays, after, name, n_sems):
    n_a = len(arrays)
    after = [] if after is None else [after]

    def body(*refs):
        for send, _ in make(refs[:n_a], refs[-n_a - 3], refs[-n_a - 2]):
            send.start()
        refs[-1][...] = jnp.zeros_like(refs[-1])

    res = _pcall(
        body, name=name,
        in_specs=[HBM] * n_a + [ANY] * len(after),
        out_specs=[SEM, SEM] + [HBM] * n_a + [pl.BlockSpec(memory_space=pltpu.VMEM)],
        out_shape=[pltpu.SemaphoreType.DMA((n_sems,)), pltpu.SemaphoreType.DMA((n_sems,))]
        + [pltpu.HBM(a.shape, a.dtype) for a in arrays] + [jax.ShapeDtypeStruct((8, HEAD), F32)],
        input_output_aliases={i: 2 + i for i in range(n_a)},
        compiler_params=pltpu.CompilerParams(has_side_effects=EFFECT),
    )(*[pltpu.with_memory_space_constraint(a, pltpu.HBM) for a in arrays], *after)
    return (res[0], res[1]), list(res[2:2 + n_a]), res[-1]


def _copy_wait(make, sems, arrays, after, name):
    n_a = len(arrays)

    def body(*refs):
        for send, recv in make(refs[:n_a], refs[n_a], refs[n_a + 1]):
            send.wait_send()
            recv.wait_recv()

    return list(_pcall(
        body, name=name,
        in_specs=[HBM] * n_a + [SEM, SEM, ANY], out_specs=[HBM] * n_a,
        out_shape=[pltpu.HBM(a.shape, a.dtype) for a in arrays],
        input_output_aliases={i: i for i in range(n_a)},
        compiler_params=pltpu.CompilerParams(has_side_effects=EFFECT),
    )(*arrays, sems[0], sems[1], after))


def _ag_descs(bufs, send_sems, recv_sems):
    x, y, c, chips = _place()
    me = 2 * x + y
    out = []
    for w, buf in enumerate(bufs):
        half = buf.shape[1] // 2
        rows = pl.ds(pl.multiple_of(c * half, 16), half)
        for j, (px, py) in enumerate(chips):
            mk = lambda ref, w=w, j=j, px=px, py=py: pltpu.make_async_remote_copy(
                src_ref=ref, dst_ref=ref, send_sem=send_sems.at[w * 3 + j], recv_sem=recv_sems.at[w * 3 + j],
                device_id=(px, py, c), device_id_type=MESH)
            out.append((mk(buf.at[me, rows]), mk(buf.at[2 * px + py, rows])))
    return out


def _fw_descs(bufs, send_sems, recv_sems):
    x, y, c, chips = _place()
    out = []
    for w, buf in enumerate(bufs):
        half = buf.shape[1] // 2
        for j, (px, py) in enumerate(chips):
            def mk(which, w=w, j=j, buf=buf, half=half, px=px, py=py):
                ref = buf.at[2 * px + py, pl.ds(pl.multiple_of(which * half, 16), half)]
                return pltpu.make_async_remote_copy(
                    src_ref=ref, dst_ref=ref, send_sem=send_sems.at[w * 3 + j], recv_sem=recv_sems.at[w * 3 + j],
                    device_id=(x, y, 1 - c), device_id_type=MESH)
            out.append((mk(c), mk(1 - c)))
    return out


def _sc_descs(refs, send_sems, recv_sems):
    n_w = len(refs) // 2
    x, y, c, chips = _place()
    me = 2 * x + y
    out = []
    for w in range(n_w):
        for j, (px, py) in enumerate(chips):
            d = pltpu.make_async_remote_copy(
                src_ref=refs[w].at[2 * px + py], dst_ref=refs[n_w + w].at[me],
                send_sem=send_sems.at[w * 3 + j], recv_sem=recv_sems.at[w * 3 + j],
                device_id=(px, py, c), device_id_type=MESH)
            out.append((d, d))
    return out


def _pair_descs(src_of):
    def make(refs, send_sems, recv_sems):
        n_w = len(refs) // 2
        x, y, c, _ = _place()
        out = []
        for w in range(n_w):
            d = pltpu.make_async_remote_copy(
                src_ref=src_of(refs[w], c), dst_ref=refs[n_w + w],
                send_sem=send_sems.at[w], recv_sem=recv_sems.at[w],
                device_id=(x, y, 1 - c), device_id_type=MESH)
            out.append((d, d))
        return out
    return make


_EX_DESCS = _pair_descs(lambda g4, c: g4.at[:, 1 - c])
_SW_DESCS = _pair_descs(lambda half, c: half)


def _sm_descs(refs, send_sems, recv_sems):
    buf = refs[0]
    rows8 = buf.shape[0] // N_DEV
    x, y, c, _ = _place()
    flip = lambda v, d: 1 - v if d else v
    blk = lambda px, py, pc: buf.at[pl.ds(pl.multiple_of((4 * px + 2 * py + pc) * rows8, 8), rows8)]
    out = []
    for k in range(1, N_DEV):
        px, py, pc = flip(x, k & 4), flip(y, k & 2), flip(c, k & 1)
        mk = lambda ref, k=k, px=px, py=py, pc=pc: pltpu.make_async_remote_copy(
            src_ref=ref, dst_ref=ref, send_sem=send_sems.at[k - 1], recv_sem=recv_sems.at[k - 1],
            device_id=(px, py, pc), device_id_type=MESH)
        out.append((mk(blk(x, y, c)), mk(blk(px, py, pc))))
    return out


def _place_rows_body(i_ref, x_ref, o_ref):
    o_ref[...] = x_ref[...]


def _place_rows(i_arr, x, n_blocks, name):
    r, n = x.shape
    grid_spec = pltpu.PrefetchScalarGridSpec(
        num_scalar_prefetch=1, grid=(1,),
        in_specs=[pl.BlockSpec((r, n), lambda g, i: (0, 0))],
        out_specs=pl.BlockSpec((r, n), lambda g, i: (i[0], 0)))
    return _pcall(_place_rows_body, name=name, grid_spec=grid_spec,
                  out_shape=jax.ShapeDtypeStruct((n_blocks * r, n), x.dtype))(i_arr, x)


def _ag_forward_body(*refs, n_w):
    bufs = refs[n_w:2 * n_w]
    send_sems, recv_sems = refs[2 * n_w:]
    pairs = _fw_descs(bufs, send_sems, recv_sems)
    for fw, _ in pairs:
        fw.start()
    for fw, back in pairs:
        back.wait_recv()
        fw.wait_send()


def _ag_forward(bufs, tag):
    n_w = len(bufs)
    return list(_pcall(
        functools.partial(_ag_forward_body, n_w=n_w), name="weight_allgather_forward_" + tag,
        in_specs=[ANY] * n_w, out_specs=[ANY] * n_w,
        out_shape=[jax.ShapeDtypeStruct(b.shape, b.dtype) for b in bufs],
        input_output_aliases={w: w for w in range(n_w)},
        scratch_shapes=[pltpu.SemaphoreType.DMA((3 * n_w,))] * 2,
    )(*bufs))


def _pair_send_body(*refs, n_w):
    pairs = _EX_DESCS(refs[:2 * n_w], refs[2 * n_w], refs[2 * n_w + 1])
    for cp, _ in pairs:
        cp.start()
    for cp, _ in pairs:
        cp.wait()


def _pair_send(grads4, tag):
    n_w = len(grads4)
    return _pcall(
        functools.partial(_pair_send_body, n_w=n_w), name="grad_pair_exchange_" + tag,
        in_specs=[ANY] * n_w, out_specs=[ANY] * n_w,
        out_shape=[jax.ShapeDtypeStruct((g.shape[0],) + g.shape[2:], g.dtype) for g in grads4],
        scratch_shapes=[pltpu.SemaphoreType.DMA((n_w,))] * 2,
    )(*grads4)


def _pair_add_body(c_ref, mine_ref, theirs_ref, o_ref):
    o_ref[...] = (mine_ref[...].astype(F32) + theirs_ref[...].astype(F32)).astype(o_ref.dtype)


def _pair_add(c_arr, g4, recv, name):
    _, _, hr, cols = g4.shape
    tr = min(hr, 256)
    grid_spec = pltpu.PrefetchScalarGridSpec(
        num_scalar_prefetch=1, grid=(N_CHIPS, hr // tr),
        in_specs=[pl.BlockSpec((None, None, tr, cols), lambda s, i, c: (s, c[0], i, 0)),
                  pl.BlockSpec((None, tr, cols), lambda s, i, c: (s, i, 0))],
        out_specs=pl.BlockSpec((None, tr, cols), lambda s, i, c: (s, i, 0)))
    return _pcall(
        _pair_add_body, name=name, grid_spec=grid_spec,
        out_shape=jax.ShapeDtypeStruct(recv.shape, recv.dtype),
        compiler_params=pltpu.CompilerParams(dimension_semantics=("parallel", "parallel")),
    )(c_arr, g4, recv)


def _sum4_body(me_ref, p_ref, l0, l1, l2, l3, o_ref):
    me = me_ref[0]
    t = [jnp.where(me == j, p_ref[...], l[...]).astype(F32) for j, l in enumerate((l0, l1, l2, l3))]
    o_ref[...] = ((t[0] + t[1]) + t[2]) + t[3]


def _sum4(me_arr, part, landed, name):
    _, hr, cols = part.shape
    tr = min(hr, 256)

    def slot(j):
        return lambda i, me: (jnp.where(me[0] == j, (j + 1) % N_CHIPS, j), i, 0)

    grid_spec = pltpu.PrefetchScalarGridSpec(
        num_scalar_prefetch=1, grid=(hr // tr,),
        in_specs=[pl.BlockSpec((None, tr, cols), lambda i, me: (me[0], i, 0))]
        + [pl.BlockSpec((None, tr, cols), slot(j)) for j in range(N_CHIPS)],
        out_specs=pl.BlockSpec((tr, cols), lambda i, me: (i, 0)))
    return _pcall(
        _sum4_body, name=name, grid_spec=grid_spec,
        out_shape=jax.ShapeDtypeStruct((hr, cols), F32),
        compiler_params=pltpu.CompilerParams(dimension_semantics=("parallel",)),
    )(me_arr, part, landed, landed, landed, landed)


def _adamw(w, g, m, v):
    m = ADAM_B1 * m + (1.0 - ADAM_B1) * g
    v = ADAM_B2 * v + (1.0 - ADAM_B2) * (g * g)
    m_hat = m / (1.0 - ADAM_B1 ** ADAM_STEP)
    v_hat = v / (1.0 - ADAM_B2 ** ADAM_STEP)
    delta = -ADAM_LR * (m_hat / (jnp.sqrt(v_hat) + ADAM_EPS) + ADAM_WD * w)
    return delta, m, v


def _adamw_half_body(h_ref, w_ref, g_in_ref, m_ref, v_ref, *rest):
    g_ref, d_ref, nm_ref, nv_ref, done_ref = rest[-5:]
    done_ref[...] = jnp.zeros_like(done_ref)
    g = g_in_ref[...]
    g_ref[...] = g
    d, m, v = _adamw(w_ref[...], g, m_ref[...], v_ref[...])
    d_ref[...] = d
    nm_ref[...] = m
    nv_ref[...] = v


def _adamw_half(h_arr, w, g_half, m, v, prev, name):
    rows, cols = w.shape
    tr = min(rows // 2, 256)
    nh = (rows // 2) // tr
    at_half = pl.BlockSpec((tr, cols), lambda i, h: (h[0] * nh + i, 0))
    grid_spec = pltpu.PrefetchScalarGridSpec(
        num_scalar_prefetch=1, grid=(nh,),
        in_specs=[at_half, pl.BlockSpec((tr, cols), lambda i, h: (i, 0)), at_half, at_half] + [ANY] * len(prev),
        out_specs=[at_half] * 4 + [pl.BlockSpec((8, HEAD), lambda i, h: (0, 0))])
    return list(_pcall(
        _adamw_half_body, name=name, grid_spec=grid_spec,
        out_shape=[jax.ShapeDtypeStruct(w.shape, F32)] * 4 + [jax.ShapeDtypeStruct((8, HEAD), F32)],
        input_output_aliases={5 + k: k for k in range(len(prev))},
        compiler_params=pltpu.CompilerParams(dimension_semantics=("arbitrary",)),
    )(h_arr, w, g_half, m, v, *prev))


def _small_update_body(gath_ref, w_ref, m_ref, v_ref, g_ref, d_ref, nm_ref, nv_ref, loss_ref, *, n_gain):
    tot = gath_ref[0:1, :]
    for i in range(1, gath_ref.shape[0]):
        tot = tot + gath_ref[i:i + 1, :]
    g = tot[:, 0:n_gain]
    g_ref[...] = g
    d, m, v = _adamw(w_ref[...], g, m_ref[...], v_ref[...])
    d_ref[...] = d
    nm_ref[...] = m
    nv_ref[...] = v
    loss_ref[...] = (0.5 / D_MODEL) * jnp.sum(tot[:, n_gain:n_gain + HEAD], axis=1, keepdims=True) * jnp.ones((1, HEAD), F32)


def _small_update(gath, w, m, v):
    n_gain = w.shape[1]
    vm = pl.BlockSpec(memory_space=pltpu.VMEM)
    return _pcall(
        functools.partial(_small_update_body, n_gain=n_gain), name="gain_update",
        in_specs=[vm] * 4, out_specs=[vm] * 5,
        out_shape=[jax.ShapeDtypeStruct((1, n_gain), F32)] * 4 + [jax.ShapeDtypeStruct((1, HEAD), F32)],
    )(gath, w, m, v)


def kernel(x, positions, norm_attn_pre, norm_attn_post, w_in, q_latent_norm, kv_latent_norm, w_uq, w_ukv, w_out, norm_mlp_pre, norm_mlp_post, w_up, w_down, loss_target, m_norm_attn_pre, m_norm_attn_post, m_w_in, m_q_latent_norm, m_kv_latent_norm, m_w_uq, m_w_ukv, m_w_out, m_norm_mlp_pre, m_norm_mlp_post, m_w_up, m_w_down, v_norm_attn_pre, v_norm_attn_post, v_w_in, v_q_latent_norm, v_kv_latent_norm, v_w_uq, v_w_ukv, v_w_out, v_norm_mlp_pre, v_norm_mlp_post, v_w_up, v_w_down):
    T = x.shape[1]
    c_arr = lax.axis_index("c").astype(jnp.int32).reshape(1)
    me_arr = (2 * lax.axis_index("x") + lax.axis_index("y")).astype(jnp.int32).reshape(1)
    names = ["w_in", "w_uq", "w_ukv", "w_out", "w_up", "w_down"]

    mats = [w_in[0], w_uq[0], w_ukv[0], w_out[0], w_up[0], w_down[0]]
    me8_arr = (4 * lax.axis_index("x") + 2 * lax.axis_index("y") + lax.axis_index("c")).astype(jnp.int32).reshape(1)
    col_major = lambda g: jnp.transpose(g, (1, 0, 2)).reshape(g.shape[1], N_CHIPS * g.shape[2])
    cast = lambda a: a.astype(MXU_DTYPE)
    to_shards = lambda g: jnp.transpose(g.reshape(g.shape[0], N_CHIPS, g.shape[1] // N_CHIPS), (1, 0, 2))
    halved = lambda g: g.reshape(N_CHIPS, 2, g.shape[1] // 2, g.shape[2])
    empty = lambda a, shape=None: lax.empty(a.shape if shape is None else shape, a.dtype)

    sem_in, buf_in, going = _copy_start(_ag_descs, [_cast_place(me_arr, mats[0], "cast_w_in")], None,
                                        "weight_allgather_start_in", 3)
    placed = [_cast_place(me_arr, w, "cast_" + n) for w, n in zip(mats[1:], names[1:])]
    sem_att, buf_att, going = _copy_start(_ag_descs, placed[:3], going, "weight_allgather_start_attn", 9)
    sem_mlp, buf_mlp, started = _copy_start(_ag_descs, placed[3:], going, "weight_allgather_start_mlp", 6)

    def in_weights(after):
        (win_g,) = _ag_forward(_copy_wait(_ag_descs, sem_in, buf_in, after, "weight_allgather_wait_in"), "in")
        return cast(jnp.pad(col_major(win_g), ((0, 0), (0, PROJ_COLS - IN_COLS))))

    def attn_weights(after):
        wuq_g, wukv_g, wout_g = _ag_forward(
            _copy_wait(_ag_descs, sem_att, buf_att, after, "weight_allgather_wait_attn"), "attn")
        wuq_full = col_major(wuq_g).reshape(LORA, NH, HEAD + ROPE_B)
        w_uq_p = jnp.pad(wuq_full, ((0, 0), (0, 0), (0, QPAD - HEAD - ROPE_B))).reshape(LORA, NH * QPAD)
        w_ukv_p = col_major(wukv_g).reshape(LORA, NH, 2, HEAD).transpose(0, 2, 1, 3).reshape(LORA, 2 * A_W)
        return cast(w_uq_p), cast(w_ukv_p), cast(wout_g.reshape(2 * A_W, D_MODEL))

    going_on = {}

    def mlp_prefetch(after):
        landed = _copy_wait(_ag_descs, sem_mlp, buf_mlp, after, "weight_allgather_wait_mlp")
        going_on["fw"] = _copy_start(_fw_descs, landed, None, "weight_allgather_forward_start_mlp", 6)

    def mlp_weights(after):
        sems, bufs, _ = going_on["fw"]
        wup_g, wdown_g = _copy_wait(_fw_descs, sems, bufs, after, "weight_allgather_forward_wait_mlp")
        return cast(wup_g), cast(wdown_g.reshape(D_FF, D_MODEL))

    def exchange_start(g4s, tag):
        lands = [empty(g, (g.shape[0],) + g.shape[2:]) for g in g4s]
        return _copy_start(_EX_DESCS, g4s + lands, None, "grad_pair_exchange_start_" + tag, len(g4s))

    def exchange_finish(started_ex, after, ns, tag):
        sems, arrs, _ = started_ex
        arrs = _copy_wait(_EX_DESCS, sems, arrs, after, "grad_pair_exchange_wait_" + tag)
        n = len(ns)
        return [_pair_add(c_arr, g4, r, "pair_add_" + nm) for g4, r, nm in zip(arrs[:n], arrs[n:], ns)]

    def scatter_start(parts, after, tag):
        return _copy_start(_sc_descs, parts + [empty(p) for p in parts], after, "grad_scatter_start_" + tag,
                           3 * len(parts))

    def scatter_finish(started_sc, after, tag):
        sems, arrs, _ = started_sc
        arrs = _copy_wait(_sc_descs, sems, arrs, after, "grad_scatter_wait_" + tag)
        return arrs[:len(arrs) // 2], arrs[len(arrs) // 2:]

    def down_grad_ready(gw_down):
        going_on["x_down"] = exchange_start([halved(gw_down.reshape(N_CHIPS, D_MODEL, D_MODEL))], "down")

    def up_grad_ready(gw_up):
        going_on["x_up"] = exchange_start([halved(gw_up)], "up")
        parts = exchange_finish(going_on["x_down"], going_on["x_up"][-1], names[5:], "down")
        going_on["s_down"] = scatter_start(parts, started, "down")
        return going_on["s_down"][-1][0:1, 0:1]

    def attn_grads_ready(gw_out, gw_uq_p, gw_ukv_p):
        gw_uq = to_shards(gw_uq_p.reshape(LORA, NH, QPAD)[:, :, :HEAD + ROPE_B].reshape(LORA, NH * (HEAD + ROPE_B)))
        gw_ukv = to_shards(gw_ukv_p.reshape(LORA, 2, NH, HEAD).transpose(0, 2, 1, 3).reshape(LORA, 2 * A_W))
        full4 = [halved(g) for g in (gw_uq, gw_ukv, gw_out.reshape(N_CHIPS, LORA, D_MODEL))]
        from_sib = _pair_send(full4, "attn")
        parts = [_pair_add(c_arr, g4, r, "pair_add_" + n) for g4, r, n in zip(full4, from_sib, names[1:4])]
        parts += exchange_finish(going_on["x_up"], from_sib[-1], names[4:5], "up")
        going_on["s_rest"] = scatter_start(parts, going_on["s_down"][-1], "attn_up")
        return going_on["s_rest"][-1][0:1, 0:1]

    dx, gw_proj, small = _local_step(
        x[0], positions[0].astype(F32).reshape(T, 1), loss_target[0],
        norm_attn_pre + started[0:1, 0:1], norm_attn_post, q_latent_norm, kv_latent_norm, norm_mlp_pre, norm_mlp_post,
        in_weights, attn_weights, mlp_prefetch, mlp_weights, down_grad_ready, up_grad_ready, attn_grads_ready)

    ms = [m_w_in[0], m_w_uq[0], m_w_ukv[0], m_w_out[0], m_w_up[0], m_w_down[0]]
    vs = [v_w_in[0], v_w_uq[0], v_w_ukv[0], v_w_out[0], v_w_up[0], v_w_down[0]]
    sib_arr = 1 - c_arr

    def finish(parts, landed, lo, hi, tag):
        sl = slice(lo, hi)
        halves = [_sum4(me_arr, p, l, "chip_sum_" + n) for p, l, n in zip(parts, landed, names[sl])]
        n = len(halves)
        sems, arrs, _ = _copy_start(_SW_DESCS, halves + [empty(h) for h in halves], None,
                                    "grad_pair_swap_start_" + tag, n)
        own = [_adamw_half(c_arr, w, g, m, v, [], "adamw_own_" + nm)
               for w, g, m, v, nm in zip(mats[sl], arrs[:n], ms[sl], vs[sl], names[sl])]
        arrs = _copy_wait(_SW_DESCS, sems, arrs, own[-1][4], "grad_pair_swap_wait_" + tag)
        return [_adamw_half(sib_arr, w, g, m, v, prev[:4], "adamw_sib_" + nm)
                for w, g, m, v, prev, nm in zip(mats[sl], arrs[n:], ms[sl], vs[sl], own, names[sl])]

    sem_small, (gath,), small_going = _copy_start(
        _sm_descs, [_place_rows(me8_arr, small, N_DEV, "place_small")], None, "small_allgather_start", N_DEV - 1)
    x_in = exchange_start([halved(to_shards(gw_proj[:, :IN_COLS]))], "in")
    s_in = scatter_start(exchange_finish(x_in, small_going, names[:1], "in"), None, "in")
    parts_rest, landed_rest = scatter_finish(going_on["s_rest"], s_in[-1], "attn_up")
    parts_down, landed_down = scatter_finish(going_on["s_down"], landed_rest[0], "down")
    upd_rest = finish(parts_rest + parts_down, landed_rest + landed_down, 1, 6, "rest")
    parts_in, landed_in = scatter_finish(s_in, upd_rest[-1][0], "in")
    upd = finish(parts_in, landed_in, 0, 1, "in") + upd_rest
    grads = [u[0] for u in upd]

    (gath,) = _copy_wait(_sm_descs, sem_small, [gath], grads[0], "small_allgather_wait")
    gains = [norm_attn_pre, norm_attn_post, q_latent_norm, kv_latent_norm, norm_mlp_pre, norm_mlp_post]
    gm = [m_norm_attn_pre, m_norm_attn_post, m_q_latent_norm, m_kv_latent_norm, m_norm_mlp_pre, m_norm_mlp_post]
    gv = [v_norm_attn_pre, v_norm_attn_post, v_q_latent_norm, v_kv_latent_norm, v_norm_mlp_pre, v_norm_mlp_post]
    cat = lambda xs: jnp.concatenate(xs, axis=1)
    g_s, d_s, m_s, v_s, loss_v = _small_update(gath, cat(gains), cat(gm), cat(gv))
    widths = [a.shape[1] for a in gains]
    offs = [sum(widths[:i]) for i in range(len(widths))]
    split = lambda a: [a[:, o:o + w] for o, w in zip(offs, widths)]
    g_gain, d_gain, m_gain, v_gain = split(g_s), split(d_s), split(m_s), split(v_s)

    def ordered(gain_list, mat_list):
        gl, ml = gain_list, [a[None] for a in mat_list]
        return [gl[0], gl[1], ml[0], gl[2], gl[3], ml[1], ml[2], ml[3], gl[4], gl[5], ml[4], ml[5]]

    loss = loss_v[0, 0]
    return (loss, dx[None],
            *ordered(g_gain, grads),
            *ordered(d_gain, [u[1] for u in upd]),
            *ordered(m_gain, [u[2] for u in upd]),
            *ordered(v_gain, [u[3] for u in upd]))
```

```python
import functools

import jax
import jax.numpy as jnp
from jax import lax
from jax.experimental import pallas as pl
from jax.experimental.pallas import tpu as pltpu

F32 = jnp.float32
BF16 = jnp.bfloat16
MXU_DTYPE = jnp.bfloat16
WIRE_DTYPE = jnp.bfloat16

D_MODEL = 2048
HEAD = 128
NH = 8
A_W = NH * HEAD
LORA = 512
ROPE_B = 64
QPAD = 256
MAIN_COLS = 3 * A_W + 2 * LORA
IN_COLS = MAIN_COLS + ROPE_B
PROJ_COLS = MAIN_COLS + HEAD
PROJ_TILE = PROJ_COLS // 3
D_FF = 4 * D_MODEL
DIL = (1, 4, 16)
ROT_A = 32
ROPE_THETA = 500000.0
EPS = 1e-6
NEG = -1e30
N_CHIPS = 4
N_DEV = 8

ADAM_LR = 0.001
ADAM_B1 = 0.9
ADAM_B2 = 0.999
ADAM_EPS = 1e-08
ADAM_WD = 0.01
ADAM_STEP = 10

MESH = pl.DeviceIdType.MESH
ANY = pl.BlockSpec(memory_space=pl.ANY)


def _pcall(body, **kw):
    return pl.pallas_call(body, **kw)


_DIMS = {
    "nn": (((1,), (0,)), ((), ())),
    "nt": (((1,), (1,)), ((), ())),
    "tn": (((0,), (0,)), ((), ())),
}


def _mm_body(*refs, dims, nk, epi, n_extra, n_after, n_out):
    a_ref, b_ref = refs[0], refs[1]
    extra = refs[2:2 + n_extra]
    outs = refs[2 + n_extra + n_after:2 + n_extra + n_after + n_out]
    part = lax.dot_general(a_ref[...], b_ref[...], _DIMS[dims], preferred_element_type=F32)

    def finish(acc):
        res = epi(acc, *[r[...] for r in extra]) if epi is not None else (acc,)
        for o_ref, o in zip(outs, res):
            o_ref[...] = o.astype(o_ref.dtype)

    if nk == 1:
        finish(part)
        return
    acc_ref = refs[-1]
    k = pl.program_id(2)

    @pl.when(k == 0)
    def _():
        acc_ref[...] = part

    @pl.when(k > 0)
    def _():
        acc_ref[...] += part

    @pl.when(k == nk - 1)
    def _():
        finish(acc_ref[...])


def _matmul(a, b, *, dims, out_dtypes, tm, tn, tk, name, epi=None, extras=(), row_extras=(), b_outer=False,
            b_shards=0, out_shards=0, after=None):
    if b_shards:
        assert dims in ("nn", "nt") and b.shape[0] == b_shards
        b2 = (b.shape[1], b_shards * b.shape[2])
    else:
        b2 = b.shape
    if dims == "nn":
        (M, K), (K2, N) = a.shape, b2
    elif dims == "nt":
        (M, K), (N, K2) = a.shape, b2
    else:
        (K, M), (K2, N) = a.shape, b2
    assert K == K2, (a.shape, b.shape, dims)
    tm, tn, tk = min(tm, M), min(tn, N), min(tk, K)
    assert M % tm == 0 and N % tn == 0 and K % tk == 0, (name, M, N, K, tm, tn, tk)
    nk = K // tk

    def at(f):
        if b_outer:
            return lambda j, i, k: f(i, j, k)
        return f

    a_spec = {"nn": pl.BlockSpec((tm, tk), at(lambda i, j, k: (i, k))),
              "nt": pl.BlockSpec((tm, tk), at(lambda i, j, k: (i, k))),
              "tn": pl.BlockSpec((tk, tm), at(lambda i, j, k: (k, i)))}[dims]
    b_spec = {"nn": pl.BlockSpec((tk, tn), at(lambda i, j, k: (k, j))),
              "nt": pl.BlockSpec((tn, tk), at(lambda i, j, k: (j, k))),
              "tn": pl.BlockSpec((tk, tn), at(lambda i, j, k: (k, j)))}[dims]
    if b_shards:
        per = b.shape[2] // (tn if dims == "nn" else tk)
        assert per >= 1 and b.shape[2] % (tn if dims == "nn" else tk) == 0
        b_spec = {"nn": pl.BlockSpec((None, tk, tn), at(lambda i, j, k: (j // per, k, j % per))),
                  "nt": pl.BlockSpec((None, tn, tk), at(lambda i, j, k: (k // per, j, k % per)))}[dims]
    o_spec = pl.BlockSpec((tm, tn), at(lambda i, j, k: (i, j)))
    o_shape = (M, N)
    if out_shards:
        assert not extras and N % out_shards == 0 and (N // out_shards) % tn == 0
        o_per = (N // out_shards) // tn
        o_spec = pl.BlockSpec((None, tm, tn), at(lambda i, j, k: (j // o_per, i, j % o_per)))
        o_shape = (out_shards, M, N // out_shards)
    r_specs = [pl.BlockSpec((tm, r.shape[1]), at(lambda i, j, k: (i, 0))) for r in row_extras]
    after = [] if after is None else [after]
    body = functools.partial(_mm_body, dims=dims, nk=nk, epi=epi, n_extra=len(extras) + len(row_extras),
                             n_after=len(after), n_out=len(out_dtypes))
    res = _pcall(
        body, name=name,
        grid=(N // tn, M // tm, nk) if b_outer else (M // tm, N // tn, nk),
        in_specs=[a_spec, b_spec] + [o_spec] * len(extras) + r_specs + [ANY] * len(after),
        out_specs=[o_spec] * len(out_dtypes),
        out_shape=[jax.ShapeDtypeStruct(o_shape, dt) for dt in out_dtypes],
        scratch_shapes=[pltpu.VMEM((tm, tn), F32)] if nk > 1 else [],
        compiler_params=pltpu.CompilerParams(
            dimension_semantics=("parallel", "parallel", "arbitrary")),
    )(a, b, *extras, *row_extras, *after)
    return list(res)


def _rowwise(body, row_ins, vec_ins, row_outs, acc_outs, *, tr, name):
    T = row_ins[0].shape[0]
    tr = min(tr, T)
    assert T % tr == 0
    in_specs = [pl.BlockSpec((tr, a.shape[1]), lambda i: (i, 0)) for a in row_ins]
    in_specs += [pl.BlockSpec(a.shape, lambda i: (0, 0)) for a in vec_ins]
    out_specs = [pl.BlockSpec((tr, w), lambda i: (i, 0)) for (w, _) in row_outs]
    out_specs += [pl.BlockSpec(s, lambda i: (0, 0)) for s in acc_outs]
    out_shape = [jax.ShapeDtypeStruct((T, w), dt) for (w, dt) in row_outs]
    out_shape += [jax.ShapeDtypeStruct(s, F32) for s in acc_outs]
    sem = "arbitrary" if acc_outs else "parallel"
    return list(_pcall(
        body, name=name, grid=(T // tr,), in_specs=in_specs, out_specs=out_specs,
        out_shape=out_shape,
        compiler_params=pltpu.CompilerParams(dimension_semantics=(sem,)),
    )(*row_ins, *vec_ins))


def _rstd(x):
    return lax.rsqrt(jnp.mean(x * x, axis=-1, keepdims=True) + EPS)


def _rms_bwd(x, rstd, dyg):
    xh = x * rstd
    return rstd * (dyg - xh * jnp.mean(dyg * xh, axis=-1, keepdims=True)), xh


def _fold8(v):
    r, w = v.shape
    return jnp.sum(v.reshape(r // 8, 8, w), axis=0)


def _acc(ref, val):
    first = pl.program_id(0) == 0

    @pl.when(first)
    def _():
        ref[...] = val

    @pl.when(jnp.logical_not(first))
    def _():
        ref[...] += val


def _rope(x, c, sa, sb, half):
    return x * c + pltpu.roll(x, HEAD - half, 1) * sa + pltpu.roll(x, half, 1) * sb


def _rope_t(dy, c, sa, sb, half):
    return dy * c - pltpu.roll(dy, HEAD - half, 1) * sa - pltpu.roll(dy, half, 1) * sb


def _rope_tab_body(pos_ref, inv_ref, ca, saa, sab, cb, sba, sbb):
    pos = pos_ref[...]
    lane = lax.broadcasted_iota(jnp.int32, (pos.shape[0], HEAD), 1)
    ang_a = pos * inv_ref[0:1, :]
    ang_b = pos * inv_ref[1:2, :]
    c, s = jnp.cos(ang_a), jnp.sin(ang_a)
    ha = ROT_A // 2
    ca[...] = jnp.where(lane < ROT_A, c, 1.0)
    saa[...] = jnp.where(lane < ha, -s, 0.0)
    sab[...] = jnp.where((lane >= ha) & (lane < ROT_A), s, 0.0)
    c, s = jnp.cos(ang_b), jnp.sin(ang_b)
    hb = ROPE_B // 2
    cb[...] = jnp.where(lane < ROPE_B, c, 1.0)
    sba[...] = jnp.where(lane < hb, -s, 0.0)
    sbb[...] = jnp.where((lane >= hb) & (lane < ROPE_B), s, 0.0)


def _rms_fwd_body(x_ref, g_ref, h_ref):
    x = x_ref[...]
    h_ref[...] = ((x * _rstd(x)) * g_ref[...]).astype(h_ref.dtype)


def _postproj_body(p_ref, ca, saa, sab, cb, sba, sbb, gq_ref, gkv_ref,
                   q_ref, k_ref, v_ref, cqn_ref, ckvn_ref, krope_ref):
    c, sa, sb = ca[...], saa[...], sab[...]
    for h in range(NH):
        lo = h * HEAD
        q_ref[:, lo:lo + HEAD] = _rope(p_ref[:, lo:lo + HEAD], c, sa, sb, ROT_A // 2).astype(q_ref.dtype)
        k_ref[:, lo:lo + HEAD] = _rope(p_ref[:, A_W + lo:A_W + lo + HEAD], c, sa, sb, ROT_A // 2).astype(k_ref.dtype)
    v_ref[...] = p_ref[:, 2 * A_W:3 * A_W].astype(v_ref.dtype)
    cq = p_ref[:, 3 * A_W:3 * A_W + LORA]
    cqn_ref[...] = ((cq * _rstd(cq)) * gq_ref[...]).astype(cqn_ref.dtype)
    ckv = p_ref[:, 3 * A_W + LORA:MAIN_COLS]
    ckvn_ref[...] = ((ckv * _rstd(ckv)) * gkv_ref[...]).astype(ckvn_ref.dtype)
    krope_ref[...] = _rope(p_ref[:, MAIN_COLS:PROJ_COLS], cb[...], sba[...], sbb[...], ROPE_B // 2).astype(krope_ref.dtype)


def _mid_body(x_ref, o_ref, g2_ref, g3_ref, x1_ref, h2_ref):
    o = o_ref[...]
    x1 = x_ref[...] + (o * _rstd(o)) * g2_ref[...]
    x1_ref[...] = x1
    h2_ref[...] = ((x1 * _rstd(x1)) * g3_ref[...]).astype(h2_ref.dtype)


def _loss_body(x1_ref, d_ref, t_ref, g4_ref, dy_ref, dd_ref, loss_ref, dg4_ref):
    d = d_ref[...]
    rstd = _rstd(d)
    y = x1_ref[...] + (d * rstd) * g4_ref[...]
    e = y - t_ref[...]
    dy = e * (1.0 / D_MODEL)
    dy_ref[...] = dy
    dd, dh = _rms_bwd(d, rstd, dy * g4_ref[...])
    dd_ref[...] = dd.astype(dd_ref.dtype)
    _acc(dg4_ref, _fold8(dy * dh))
    e8 = _fold8(e * e)
    l = e8[:, 0:HEAD]
    for j in range(1, D_MODEL // HEAD):
        l = l + e8[:, j * HEAD:(j + 1) * HEAD]
    _acc(loss_ref, l)


def _bmid_body(dy_ref, dh2_ref, x1_ref, o_ref, g2_ref, g3_ref, dx1_ref, do_ref, dg3_ref, dg2_ref):
    x1 = x1_ref[...]
    dh2 = dh2_ref[...]
    dn, x1h = _rms_bwd(x1, _rstd(x1), dh2 * g3_ref[...])
    dx1 = dy_ref[...] + dn
    dx1_ref[...] = dx1
    _acc(dg3_ref, _fold8(dh2 * x1h))
    o = o_ref[...]
    do, oh = _rms_bwd(o, _rstd(o), dx1 * g2_ref[...])
    do_ref[...] = do.astype(do_ref.dtype)
    _acc(dg2_ref, _fold8(dx1 * oh))


def _dproj_body(dq_ref, dk_ref, dv_ref, dcq_ref, dckv_ref, p_ref, dkr_ref,
                ca, saa, sab, cb, sba, sbb, gq_ref, gkv_ref,
                dp_ref, dgq_ref, dgkv_ref):
    c, sa, sb = ca[...], saa[...], sab[...]
    for h in range(NH):
        lo = h * HEAD
        dp_ref[:, lo:lo + HEAD] = _rope_t(dq_ref[:, lo:lo + HEAD], c, sa, sb, ROT_A // 2).astype(dp_ref.dtype)
        dp_ref[:, A_W + lo:A_W + lo + HEAD] = _rope_t(dk_ref[:, lo:lo + HEAD], c, sa, sb, ROT_A // 2).astype(dp_ref.dtype)
    dp_ref[:, 2 * A_W:3 * A_W] = dv_ref[...].astype(dp_ref.dtype)
    cq = p_ref[:, 3 * A_W:3 * A_W + LORA]
    dcqn = dcq_ref[...]
    dcq, cqh = _rms_bwd(cq, _rstd(cq), dcqn * gq_ref[...])
    dp_ref[:, 3 * A_W:3 * A_W + LORA] = dcq.astype(dp_ref.dtype)
    _acc(dgq_ref, _fold8(dcqn * cqh))
    ckv = p_ref[:, 3 * A_W + LORA:MAIN_COLS]
    dckvn = dckv_ref[...]
    dckv, ckvh = _rms_bwd(ckv, _rstd(ckv), dckvn * gkv_ref[...])
    dp_ref[:, 3 * A_W + LORA:MAIN_COLS] = dckv.astype(dp_ref.dtype)
    _acc(dgkv_ref, _fold8(dckvn * ckvh))
    dkr = dkr_ref[:, 0:HEAD]
    for h in range(1, NH):
        dkr = dkr + dkr_ref[:, h * HEAD:(h + 1) * HEAD]
    dp_ref[:, MAIN_COLS:PROJ_COLS] = _rope_t(dkr, cb[...], sba[...], sbb[...], ROPE_B // 2).astype(dp_ref.dtype)


def _bin_body(dx1_ref, dh_ref, x_ref, g1_ref, dx_ref, dg1_ref):
    x = x_ref[...]
    dh = dh_ref[...]
    dn, xh = _rms_bwd(x, _rstd(x), dh * g1_ref[...])
    dx_ref[...] = dx1_ref[...] + dn
    _acc(dg1_ref, _fold8(dh * xh))


def _dot_nt(a, b):
    return lax.dot_general(a, b, _DIMS["nt"], preferred_element_type=F32)


def _dot_tn(a, b):
    return lax.dot_general(a, b, _DIMS["tn"], preferred_element_type=F32)


def _dot_nn(a, b):
    return jnp.dot(a, b, preferred_element_type=F32)


DIL_SCALE = HEAD ** -0.5
DIL_CHUNK = 256


def _dil_rows(t, d):
    r = t & (d - 1)
    n = t >> (d.bit_length() - 1)
    start = r + n * (HEAD * d)
    has_prev = n > 0
    pstart = jnp.where(has_prev, start - HEAD * d, start)
    if d == 1:
        return pl.ds(pl.multiple_of(start, HEAD), HEAD), pl.ds(pl.multiple_of(pstart, HEAD), HEAD), has_prev
    return pl.ds(start, HEAD, stride=d), pl.ds(pstart, HEAD, stride=d), has_prev


def _dil_band():
    row = lax.broadcasted_iota(jnp.int32, (HEAD, 2 * HEAD), 0)
    col = lax.broadcasted_iota(jnp.int32, (HEAD, 2 * HEAD), 1)
    return (col >= row) & (col <= row + HEAD), col >= HEAD


def _dil_fwd_body(q_ref, k_ref, v_ref, a_ref, lse_ref, o1, o2, o3, l1, l2, l3, *, nt, unroll):
    band, is_cur = _dil_band()
    for d, o_sc, l_sc in zip(DIL, (o1, o2, o3), (l1, l2, l3)):

        def tile(t, carry, d=d, o_sc=o_sc, l_sc=l_sc):
            rows, prows, has_prev = _dil_rows(t, d)
            q = q_ref[rows, :].astype(MXU_DTYPE)
            kk = jnp.concatenate([k_ref[prows, :], k_ref[rows, :]], axis=0).astype(MXU_DTYPE)
            vv = jnp.concatenate([v_ref[prows, :], v_ref[rows, :]], axis=0).astype(MXU_DTYPE)
            ok = band & (is_cur | has_prev)
            s = jnp.where(ok, _dot_nt(q, kk) * DIL_SCALE, NEG)
            m = jnp.max(s, axis=1, keepdims=True)
            p = jnp.exp(s - m)
            den = jnp.sum(p, axis=1, keepdims=True)
            o_sc[rows, :] = _dot_nn((p / den).astype(MXU_DTYPE), vv)
            l_sc[rows, :] = jnp.broadcast_to(m + jnp.log(den), (HEAD, HEAD))
            return carry

        lax.fori_loop(0, nt, tile, 0, unroll=unroll)

    def merge(i, carry):
        rs = pl.ds(pl.multiple_of(i * DIL_CHUNK, DIL_CHUNK), DIL_CHUNK)
        la, lb, lc = l1[rs, :], l2[rs, :], l3[rs, :]
        m = jnp.maximum(jnp.maximum(la, lb), lc)
        wa, wb, wc = jnp.exp(la - m), jnp.exp(lb - m), jnp.exp(lc - m)
        den = wa + wb + wc
        a = (wa / den) * o1[rs, :] + (wb / den) * o2[rs, :] + (wc / den) * o3[rs, :]
        a_ref[rs, :] = a.astype(a_ref.dtype)
        lse_ref[rs, :] = m + jnp.log(den)
        return carry

    lax.fori_loop(0, q_ref.shape[0] // DIL_CHUNK, merge, 0)


def _dil_fwd(q, k, v):
    T = q.shape[0]
    spec = pl.BlockSpec((T, HEAD), lambda h: (0, h))
    return _pcall(
        functools.partial(_dil_fwd_body, nt=T // HEAD, unroll=16), name="dil_fwd",
        grid=(NH,), in_specs=[spec] * 3, out_specs=[spec] * 2,
        out_shape=[jax.ShapeDtypeStruct((T, 2 * A_W), MXU_DTYPE), jax.ShapeDtypeStruct((T, A_W), F32)],
        scratch_shapes=[pltpu.VMEM((T, HEAD), F32)] * 6,
        compiler_params=pltpu.CompilerParams(dimension_semantics=("parallel",)),
    )(q, k, v)


def _dil_bwd_body(q_ref, k_ref, v_ref, do_ref, a_ref, lse_ref, dq_ref, dk_ref, dv_ref, dl_sc, *, nt, unroll):
    band, is_cur = _dil_band()

    def prep(i, carry):
        rs = pl.ds(pl.multiple_of(i * DIL_CHUNK, DIL_CHUNK), DIL_CHUNK)
        dl = jnp.sum(do_ref[rs, :] * a_ref[rs, :].astype(F32), axis=1, keepdims=True)
        dl_sc[rs, :] = jnp.broadcast_to(dl, (DIL_CHUNK, HEAD))
        zero = jnp.zeros((DIL_CHUNK, HEAD), F32)
        dq_ref[rs, :] = zero
        dk_ref[rs, :] = zero
        dv_ref[rs, :] = zero
        return carry

    lax.fori_loop(0, q_ref.shape[0] // DIL_CHUNK, prep, 0)

    for d in DIL:

        def tile(t, carry, d=d):
            rows, prows, has_prev = _dil_rows(t, d)
            q = q_ref[rows, :].astype(MXU_DTYPE)
            kk = jnp.concatenate([k_ref[prows, :], k_ref[rows, :]], axis=0).astype(MXU_DTYPE)
            vv = jnp.concatenate([v_ref[prows, :], v_ref[rows, :]], axis=0).astype(MXU_DTYPE)
            do = do_ref[rows, :].astype(MXU_DTYPE)
            lse = lse_ref[rows, :]
            dl = dl_sc[rows, :]
            ok = band & (is_cur | has_prev)
            s = _dot_nt(q, kk) * DIL_SCALE
            p = jnp.where(ok, jnp.exp(s - jnp.concatenate([lse, lse], axis=1)), 0.0)
            ds = (p * (_dot_nt(do, vv) - jnp.concatenate([dl, dl], axis=1))).astype(MXU_DTYPE)
            dq_ref[rows, :] += _dot_nn(ds, kk) * DIL_SCALE
            dkk = _dot_tn(ds, q) * DIL_SCALE
            dvv = _dot_tn(p.astype(MXU_DTYPE), do)
            dk_ref[rows, :] += dkk[HEAD:, :]
            dv_ref[rows, :] += dvv[HEAD:, :]
            dk_ref[prows, :] += dkk[:HEAD, :]
            dv_ref[prows, :] += dvv[:HEAD, :]
            return carry

        lax.fori_loop(0, nt, tile, 0, unroll=unroll)


def _dil_bwd(q, k, v, dmix, mixed, lse):
    T = q.shape[0]
    spec = pl.BlockSpec((T, HEAD), lambda h: (0, h))
    return _pcall(
        functools.partial(_dil_bwd_body, nt=T // HEAD, unroll=8), name="dil_bwd",
        grid=(NH,), in_specs=[spec] * 6, out_specs=[spec] * 3,
        out_shape=[jax.ShapeDtypeStruct((T, A_W), F32)] * 3,
        scratch_shapes=[pltpu.VMEM((T, HEAD), F32)],
        compiler_params=pltpu.CompilerParams(dimension_semantics=("parallel",)),
    )(q, k, v, dmix, mixed, lse)


MLA_SCALE = (HEAD + ROPE_B) ** -0.5
LOG2E = 1.4426950408889634
MLA_QSCALE = MLA_SCALE * LOG2E
MLA_T = 512
MLA_HP = 2


def _tri(t):
    row = lax.broadcasted_iota(jnp.int32, (t, t), 0)
    col = lax.broadcasted_iota(jnp.int32, (t, t), 1)
    return col <= row


def _lanes(x, n):
    return jnp.tile(x, (1, n // HEAD))


def _mla_fwd_body(q_ref, kn_ref, kr_ref, v_ref, mixed_ref, o_ref, lse_ref, m_sc, l_sc, acc_sc, *, t, hp):
    del mixed_ref
    qi = pl.program_id(1)
    m_sc[...] = jnp.full(m_sc.shape, NEG, F32)
    l_sc[...] = jnp.zeros(l_sc.shape, F32)
    acc_sc[...] = jnp.zeros(acc_sc.shape, F32)

    def step(j, masked):
        ks = pl.ds(pl.multiple_of(j * t, t), t)
        kr = kr_ref[ks, :]
        for hh in range(hp):
            kcat = jnp.concatenate([kn_ref[ks, hh * HEAD:(hh + 1) * HEAD], kr], axis=1)
            s = _dot_nt(q_ref[:, hh * QPAD:(hh + 1) * QPAD], kcat)
            if masked:
                s = jnp.where(_tri(t), s, NEG)
            m_prev = m_sc[hh]
            m_new = jnp.maximum(m_prev, jnp.max(s, axis=1, keepdims=True))
            alpha = jnp.exp2(m_prev - m_new)
            p = jnp.exp2(s - _lanes(m_new, t))
            l_sc[hh] = alpha * l_sc[hh] + jnp.sum(p, axis=1, keepdims=True)
            acc_sc[hh] = alpha * acc_sc[hh] + _dot_nn(p.astype(MXU_DTYPE), v_ref[ks, hh * HEAD:(hh + 1) * HEAD])
            m_sc[hh] = m_new

    def off_diag(j, carry):
        step(j, False)
        return carry

    lax.fori_loop(0, qi, off_diag, 0)
    step(qi, True)
    for hh in range(hp):
        l = l_sc[hh]
        o_ref[:, hh * HEAD:(hh + 1) * HEAD] = (acc_sc[hh] / l).astype(o_ref.dtype)
        lse_ref[:, hh * HEAD:(hh + 1) * HEAD] = m_sc[hh] + jnp.log2(l)


def _mla_fwd(qf, kv, kr, mixed):
    T = qf.shape[0]
    t, hp = min(MLA_T, T), MLA_HP
    ng = NH // hp
    return _pcall(
        functools.partial(_mla_fwd_body, t=t, hp=hp), name="mla_fwd",
        grid=(ng, T // t),
        in_specs=[pl.BlockSpec((t, hp * QPAD), lambda g, i: (i, g)),
                  pl.BlockSpec((T, hp * HEAD), lambda g, i: (0, g)),
                  pl.BlockSpec((T, HEAD), lambda g, i: (0, 0)),
                  pl.BlockSpec((T, hp * HEAD), lambda g, i: (0, ng + g)), ANY],
        out_specs=[pl.BlockSpec((t, hp * HEAD), lambda g, i: (i, ng + g)),
                   pl.BlockSpec((t, hp * HEAD), lambda g, i: (i, g))],
        out_shape=[jax.ShapeDtypeStruct(mixed.shape, mixed.dtype), jax.ShapeDtypeStruct((T, A_W), F32)],
        input_output_aliases={4: 0},
        scratch_shapes=[pltpu.VMEM((hp, t, HEAD), F32)] * 3,
        compiler_params=pltpu.CompilerParams(dimension_semantics=("parallel", "parallel")),
    )(qf, kv, kr, kv, mixed)


def _mla_bwd_body(q_ref, kn_ref, kr_ref, v_ref, do_ref, o_ref, lse_ref, cb, sba, sbb,
                  dq_ref, dkn_ref, dv_ref, dkr_ref, dq_sc, dl_sc, dk_sc, dv_sc, *, t):
    ki = pl.program_id(1)
    nq = q_ref.shape[0] // t

    @pl.when(ki == 0)
    def _():
        def prep(i, carry):
            rs = pl.ds(pl.multiple_of(i * t, t), t)
            dl = jnp.sum(do_ref[rs, :] * o_ref[rs, :].astype(F32), axis=1, keepdims=True)
            dl_sc[rs, :] = jnp.broadcast_to(dl, (t, HEAD))
            dq_sc[rs, :] = jnp.zeros((t, QPAD), F32)
            return carry
        lax.fori_loop(0, nq, prep, 0)

    kcat = jnp.concatenate([kn_ref[...], kr_ref[...]], axis=1)
    v = v_ref[...]
    dk_sc[...] = jnp.zeros(dk_sc.shape, F32)
    dv_sc[...] = jnp.zeros(dv_sc.shape, F32)

    def step(i, masked):
        qs = pl.ds(pl.multiple_of(i * t, t), t)
        q = q_ref[qs, :]
        do = do_ref[qs, :].astype(MXU_DTYPE)
        p = jnp.exp2(_dot_nt(q, kcat) - _lanes(lse_ref[qs, :], t))
        if masked:
            p = jnp.where(_tri(t), p, 0.0)
        ds = (p * (_dot_nt(do, v) - _lanes(dl_sc[qs, :], t))).astype(MXU_DTYPE)
        dv_sc[...] += _dot_tn(p.astype(MXU_DTYPE), do)
        dk_sc[...] += _dot_tn(ds, q)
        dq_sc[qs, :] += _dot_nn(ds, kcat) * MLA_SCALE

    step(ki, True)

    def off_diag(i, carry):
        step(i, False)
        return carry

    lax.fori_loop(ki + 1, nq, off_diag, 0)
    dk = dk_sc[...] * (1.0 / LOG2E)
    dkn_ref[...] = dk[:, 0:HEAD].astype(dkn_ref.dtype)
    dkr_ref[...] = dk[:, HEAD:QPAD]
    dv_ref[...] = dv_sc[...].astype(dv_ref.dtype)

    @pl.when(ki == nq - 1)
    def _():
        def emit(i, carry):
            rs = pl.ds(pl.multiple_of(i * t, t), t)
            dq_ref[rs, 0:HEAD] = dq_sc[rs, 0:HEAD].astype(dq_ref.dtype)
            dq_ref[rs, HEAD:QPAD] = _rope_t(dq_sc[rs, HEAD:QPAD], cb[rs, :], sba[rs, :], sbb[rs, :],
                                            ROPE_B // 2).astype(dq_ref.dtype)
            return carry
        lax.fori_loop(0, nq, emit, 0)


def _mla_bwd(qf, kv, kr, dmix, mixed, lse, tabs_b):
    T = qf.shape[0]
    t = min(MLA_T, T)
    head = lambda h, j: (0, h)
    b_half = lambda h, j: (0, NH + h)
    kblk = pl.BlockSpec((t, HEAD), lambda h, j: (j, h))
    return _pcall(
        functools.partial(_mla_bwd_body, t=t), name="mla_bwd",
        grid=(NH, T // t),
        in_specs=[pl.BlockSpec((T, QPAD), head), kblk,
                  pl.BlockSpec((t, HEAD), lambda h, j: (j, 0)),
                  pl.BlockSpec((t, HEAD), lambda h, j: (j, NH + h)),
                  pl.BlockSpec((T, HEAD), b_half), pl.BlockSpec((T, HEAD), b_half),
                  pl.BlockSpec((T, HEAD), head)] + [pl.BlockSpec((T, HEAD), lambda h, j: (0, 0))] * 3,
        out_specs=[pl.BlockSpec((T, QPAD), head), kblk, kblk, kblk],
        out_shape=[jax.ShapeDtypeStruct((T, NH * QPAD), MXU_DTYPE), jax.ShapeDtypeStruct((T, A_W), MXU_DTYPE),
                   jax.ShapeDtypeStruct((T, A_W), MXU_DTYPE), jax.ShapeDtypeStruct((T, A_W), F32)],
        scratch_shapes=[pltpu.VMEM((T, QPAD), F32), pltpu.VMEM((T, HEAD), F32), pltpu.VMEM((t, QPAD), F32),
                        pltpu.VMEM((t, HEAD), F32)],
        compiler_params=pltpu.CompilerParams(dimension_semantics=("parallel", "arbitrary")),
    )(qf, kv, kr, kv, dmix, mixed, lse, *tabs_b)


def _local_step(x, pos, target, g1, g2, gq, gkv, g3, g4,
                in_weights, attn_weights, mlp_prefetch, mlp_weights, down_grad_ready, up_grad_ready, attn_grads_ready):
    T = x.shape[0]
    TR = 256
    mm = functools.partial(_matmul, tm=2048, tn=1024, tk=2048, b_outer=True)
    mm_k = functools.partial(_matmul, tm=1024, tn=1024, tk=2048)
    mm_g = functools.partial(_matmul, tm=1024, tn=1024, tk=4096, b_outer=True)

    inv_a = ROPE_THETA ** (-jnp.arange(0, ROT_A, 2, dtype=F32) / ROT_A)
    inv_b = ROPE_THETA ** (-jnp.arange(0, ROPE_B, 2, dtype=F32) / ROPE_B)
    inv = jnp.stack([jnp.concatenate([inv_a, inv_a, jnp.zeros((HEAD - ROT_A,), F32)]),
                     jnp.concatenate([inv_b, inv_b, jnp.zeros((HEAD - ROPE_B,), F32)])])
    inv = jnp.concatenate([inv, jnp.zeros((6, HEAD), F32)], axis=0)
    tabs = _rowwise(_rope_tab_body, [pos], [inv], [(HEAD, F32)] * 6, [], tr=512, name="rope_tables")

    (h,) = _rowwise(_rms_fwd_body, [x], [g1], [(D_MODEL, MXU_DTYPE)], [], tr=TR, name="rms_in")
    w_proj = in_weights([h, tabs[0]])
    (proj,) = mm(h, w_proj, dims="nn", out_dtypes=[F32], tm=1024, tn=PROJ_TILE, name="proj_in")
    q, k, v, cqn, ckvn, krope = _rowwise(
        _postproj_body, [proj] + tabs, [gq, gkv],
        [(A_W, F32)] * 3 + [(LORA, MXU_DTYPE)] * 2 + [(HEAD, MXU_DTYPE)], [], tr=TR, name="post_proj")
    mixed, lse_a = _dil_fwd(q, k, v)

    w_uq_p, w_ukv_p, w_out = attn_weights(cqn)

    def q_epi(acc, cb, sba, sbb):
        cols = []
        for hh in range(acc.shape[1] // QPAD):
            lo = hh * QPAD
            cols += [acc[:, lo:lo + HEAD], _rope(acc[:, lo + HEAD:lo + QPAD], cb, sba, sbb, ROPE_B // 2)]
        return (jnp.concatenate(cols, axis=1) * MLA_QSCALE,)
    (qf,) = mm(cqn, w_uq_p, dims="nn", out_dtypes=[MXU_DTYPE], name="q_up", epi=q_epi, row_extras=tuple(tabs[3:]))
    (kv,) = mm(ckvn, w_ukv_p, dims="nn", out_dtypes=[MXU_DTYPE], name="kv_up")
    mixed, lse_b = _mla_fwd(qf, kv, krope, mixed)
    mlp_prefetch(mixed)

    (o,) = mm(mixed, w_out, dims="nn", out_dtypes=[F32], name="out_proj")
    x1, h2 = _rowwise(_mid_body, [x, o], [g2, g3], [(D_MODEL, F32), (D_MODEL, MXU_DTYPE)], [], tr=TR, name="mid_norm")

    w_up, w_down = mlp_weights(h2)

    def up_epi(acc):
        r = jnp.maximum(acc, 0.0)
        return r * r, r
    u, r = mm(h2, w_up, dims="nn", out_dtypes=[MXU_DTYPE, MXU_DTYPE], name="mlp_up", epi=up_epi, b_shards=N_CHIPS)
    (dn,) = mm_k(u, w_down, dims="nn", out_dtypes=[F32], name="mlp_down")
    dy, dd, loss8, dg4 = _rowwise(_loss_body, [x1, dn, target], [g4], [(D_MODEL, F32), (D_MODEL, MXU_DTYPE)],
                                  [(8, HEAD), (8, D_MODEL)], tr=TR, name="loss_head")

    def dup_epi(acc, rr):
        return (acc * (2.0 * rr.astype(F32)),)
    (dup,) = mm(dd, w_down, dims="nt", out_dtypes=[MXU_DTYPE], name="d_up", epi=dup_epi, extras=(r,))
    (gw_down,) = mm_g(u, dd, dims="tn", out_dtypes=[WIRE_DTYPE], name="gw_down")
    (dh2,) = mm_k(dup, w_up, dims="nt", out_dtypes=[F32], name="d_h2", b_shards=N_CHIPS,
                  after=down_grad_ready(gw_down))
    (gw_up,) = mm_g(h2, dup, dims="tn", out_dtypes=[WIRE_DTYPE], name="gw_up", out_shards=N_CHIPS)
    g2 = g2 + up_grad_ready(gw_up)
    dx1, do, dg3, dg2 = _rowwise(_bmid_body, [dy, dh2, x1, o], [g2, g3], [(D_MODEL, F32), (D_MODEL, MXU_DTYPE)],
                                 [(8, D_MODEL), (8, D_MODEL)], tr=TR, name="bwd_mid")
    (dmix,) = mm(do, w_out, dims="nt", out_dtypes=[F32], name="d_mixed")
    (gw_out,) = mm_g(mixed, do, dims="tn", out_dtypes=[WIRE_DTYPE], name="gw_out")

    dq_pad, dkn, dvb, dkr = _mla_bwd(qf, kv, krope, dmix, mixed, lse_b, tabs[3:])
    (dcqn,) = mm(dq_pad, w_uq_p, dims="nt", out_dtypes=[F32], name="d_cq")
    (gw_uq_p,) = mm_g(cqn, dq_pad, dims="tn", out_dtypes=[WIRE_DTYPE], name="gw_uq")
    dkv = jnp.concatenate([dkn, dvb], axis=1)
    (dckvn,) = mm(dkv, w_ukv_p, dims="nt", out_dtypes=[F32], name="d_ckv")
    (gw_ukv_p,) = mm_g(ckvn, dkv, dims="tn", out_dtypes=[WIRE_DTYPE], name="gw_ukv")
    gq = gq + attn_grads_ready(gw_out, gw_uq_p, gw_ukv_p)

    dq_a, dk_a, dv_a = _dil_bwd(q, k, v, dmix, mixed, lse_a)
    dproj, dgq, dgkv = _rowwise(
        _dproj_body, [dq_a, dk_a, dv_a, dcqn, dckvn, proj, dkr] + tabs, [gq, gkv],
        [(PROJ_COLS, MXU_DTYPE)], [(8, LORA), (8, LORA)], tr=TR, name="d_proj")
    (dh,) = mm_k(dproj, w_proj, dims="nt", out_dtypes=[F32], tk=PROJ_TILE, name="d_h")
    (gw_proj,) = mm_g(h, dproj, dims="tn", out_dtypes=[WIRE_DTYPE], tn=PROJ_TILE, name="gw_in")
    dx, dg1 = _rowwise(_bin_body, [dx1, dh, x], [g1], [(D_MODEL, F32)], [(8, D_MODEL)], tr=TR, name="bwd_in")

    small = jnp.concatenate([dg1, dg2, dgq, dgkv, dg3, dg4, loss8], axis=1)
    return dx, gw_proj, small


def _place():
    x, y, c = lax.axis_index("x"), lax.axis_index("y"), lax.axis_index("c")
    chips = [(1 - x, y), (x, 1 - y), (1 - x, 1 - y)]
    return x, y, c, chips


def _cast_place_body(me_ref, w_ref, *rest):
    o_ref = rest[-1]
    o_ref[...] = w_ref[...].astype(o_ref.dtype)


def _cast_place(me_arr, w, name, after=None):
    rows, cols = w.shape
    tr = min(rows, 256)
    after = [] if after is None else [after]
    grid_spec = pltpu.PrefetchScalarGridSpec(
        num_scalar_prefetch=1, grid=(rows // tr,),
        in_specs=[pl.BlockSpec((tr, cols), lambda i, me: (i, 0))] + [ANY] * len(after),
        out_specs=pl.BlockSpec((None, tr, cols), lambda i, me: (me[0], i, 0)))
    return _pcall(
        _cast_place_body, name=name, grid_spec=grid_spec,
        out_shape=jax.ShapeDtypeStruct((N_CHIPS, rows, cols), WIRE_DTYPE),
        compiler_params=pltpu.CompilerParams(dimension_semantics=("parallel",)),
    )(me_arr, w, *after)


def _w_in_natural_body(w_ref, o_ref):
    c = w_ref.shape[2]
    for k in range(N_CHIPS):
        o_ref[:, k * c:(k + 1) * c] = w_ref[k]
    o_ref[:, N_CHIPS * c:] = jnp.zeros((o_ref.shape[0], o_ref.shape[1] - N_CHIPS * c), o_ref.dtype)


def _w_in_natural(win_g):
    _, rows, c = win_g.shape
    tr = 256
    return _pcall(
        _w_in_natural_body, name="w_in_natural", grid=(rows // tr,),
        in_specs=[pl.BlockSpec((N_CHIPS, tr, c), lambda i: (0, i, 0))],
        out_specs=pl.BlockSpec((tr, PROJ_COLS), lambda i: (i, 0)),
        out_shape=jax.ShapeDtypeStruct((rows, PROJ_COLS), win_g.dtype),
        compiler_params=pltpu.CompilerParams(dimension_semantics=("parallel",)),
    )(win_g)


def _gw_in_shards_body(g_ref, o_ref):
    c = o_ref.shape[2]
    for k in range(N_CHIPS):
        o_ref[k] = g_ref[:, k * c:(k + 1) * c]


def _gw_in_shards(gw_proj):
    rows = gw_proj.shape[0]
    c = IN_COLS // N_CHIPS
    tr = 256
    return _pcall(
        _gw_in_shards_body, name="gw_in_shards", grid=(rows // tr,),
        in_specs=[pl.BlockSpec((tr, PROJ_COLS), lambda i: (i, 0))],
        out_specs=pl.BlockSpec((N_CHIPS, tr, c), lambda i: (0, i, 0)),
        out_shape=jax.ShapeDtypeStruct((N_CHIPS, rows, c), gw_proj.dtype),
        compiler_params=pltpu.CompilerParams(dimension_semantics=("parallel",)),
    )(gw_proj)


HBM = pl.BlockSpec(memory_space=pltpu.HBM)
SEM = pl.BlockSpec(memory_space=pltpu.SEMAPHORE)
EFFECT = pltpu.SideEffectType.DATAFLOW_SIDE_EFFECTING


def _copy_start(make, arrays, after, name, n_sems):
    n_a = len(arrays)
    after = [] if after is None else [after]

    def body(*refs):
        for send, _ in make(refs[:n_a], refs[-n_a - 3], refs[-n_a - 2]):
            send.start()
        refs[-1][...] = jnp.zeros_like(refs[-1])

    res = _pcall(
        body, name=name,
        in_specs=[HBM] * n_a + [ANY] * len(after),
        out_specs=[SEM, SEM] + [HBM] * n_a + [pl.BlockSpec(memory_space=pltpu.VMEM)],
        out_shape=[pltpu.SemaphoreType.DMA((n_sems,)), pltpu.SemaphoreType.DMA((n_sems,))]
        + [pltpu.HBM(a.shape, a.dtype) for a in arrays] + [jax.ShapeDtypeStruct((8, HEAD), F32)],
        input_output_aliases={i: 2 + i for i in range(n_a)},
        compiler_params=pltpu.CompilerParams(has_side_effects=EFFECT),
    )(*[pltpu.with_memory_space_constraint(a, pltpu.HBM) for a in arrays], *after)
    return (res[0], res[1]), list(res[2:2 + n_a]), res[-1]


def _copy_wait(make, sems, arrays, after, name):
    n_a = len(arrays)
    after = list(after) if isinstance(after, (list, tuple)) else [after]

    def body(*refs):
        for send, recv in make(refs[:n_a], refs[n_a], refs[n_a + 1]):
            send.wait_send()
            recv.wait_recv()

    return list(_pcall(
        body, name=name,
        in_specs=[HBM] * n_a + [SEM, SEM] + [ANY] * len(after), out_specs=[HBM] * n_a,
        out_shape=[pltpu.HBM(a.shape, a.dtype) for a in arrays],
        input_output_aliases={i: i for i in range(n_a)},
        compiler_params=pltpu.CompilerParams(has_side_effects=EFFECT),
    )(*arrays, sems[0], sems[1], *after))


def _ag_descs(bufs, send_sems, recv_sems):
    x, y, c, chips = _place()
    me = 2 * x + y
    out = []
    for w, buf in enumerate(bufs):
        half = buf.shape[1] // 2
        rows = pl.ds(pl.multiple_of(c * half, 16), half)
        for j, (px, py) in enumerate(chips):
            mk = lambda ref, w=w, j=j, px=px, py=py: pltpu.make_async_remote_copy(
                src_ref=ref, dst_ref=ref, send_sem=send_sems.at[w * 3 + j], recv_sem=recv_sems.at[w * 3 + j],
                device_id=(px, py, c), device_id_type=MESH)
            out.append((mk(buf.at[me, rows]), mk(buf.at[2 * px + py, rows])))
    return out


def _fw_descs(bufs, send_sems, recv_sems):
    x, y, c, chips = _place()
    out = []
    for w, buf in enumerate(bufs):
        half = buf.shape[1] // 2
        for j, (px, py) in enumerate(chips):
            def mk(which, w=w, j=j, buf=buf, half=half, px=px, py=py):
                ref = buf.at[2 * px + py, pl.ds(pl.multiple_of(which * half, 16), half)]
                return pltpu.make_async_remote_copy(
                    src_ref=ref, dst_ref=ref, send_sem=send_sems.at[w * 3 + j], recv_sem=recv_sems.at[w * 3 + j],
                    device_id=(x, y, 1 - c), device_id_type=MESH)
            out.append((mk(c), mk(1 - c)))
    return out


def _sc_descs(refs, send_sems, recv_sems):
    n_w = len(refs) // 2
    x, y, c, chips = _place()
    me = 2 * x + y
    out = []
    for w in range(n_w):
        for j, (px, py) in enumerate(chips):
            d = pltpu.make_async_remote_copy(
                src_ref=refs[w].at[2 * px + py], dst_ref=refs[n_w + w].at[me],
                send_sem=send_sems.at[w * 3 + j], recv_sem=recv_sems.at[w * 3 + j],
                device_id=(px, py, c), device_id_type=MESH)
            out.append((d, d))
    return out


def _pair_descs(src_of):
    def make(refs, send_sems, recv_sems):
        n_w = len(refs) // 2
        x, y, c, _ = _place()
        out = []
        for w in range(n_w):
            d = pltpu.make_async_remote_copy(
                src_ref=src_of(refs[w], c), dst_ref=refs[n_w + w],
                send_sem=send_sems.at[w], recv_sem=recv_sems.at[w],
                device_id=(x, y, 1 - c), device_id_type=MESH)
            out.append((d, d))
        return out
    return make


_EX_DESCS = _pair_descs(lambda g4, c: g4.at[:, 1 - c])
_SW_DESCS = _pair_descs(lambda half, c: half)


def _sm_descs(refs, send_sems, recv_sems):
    buf = refs[0]
    rows8 = buf.shape[0] // N_DEV
    x, y, c, _ = _place()
    flip = lambda v, d: 1 - v if d else v
    blk = lambda px, py, pc: buf.at[pl.ds(pl.multiple_of((4 * px + 2 * py + pc) * rows8, 8), rows8)]
    out = []
    for k in range(1, N_DEV):
        px, py, pc = flip(x, k & 4), flip(y, k & 2), flip(c, k & 1)
        mk = lambda ref, k=k, px=px, py=py, pc=pc: pltpu.make_async_remote_copy(
            src_ref=ref, dst_ref=ref, send_sem=send_sems.at[k - 1], recv_sem=recv_sems.at[k - 1],
            device_id=(px, py, pc), device_id_type=MESH)
        out.append((mk(blk(x, y, c)), mk(blk(px, py, pc))))
    return out


def _place_rows_body(i_ref, x_ref, o_ref):
    o_ref[...] = x_ref[...]


def _place_rows(i_arr, x, n_blocks, name):
    r, n = x.shape
    grid_spec = pltpu.PrefetchScalarGridSpec(
        num_scalar_prefetch=1, grid=(1,),
        in_specs=[pl.BlockSpec((r, n), lambda g, i: (0, 0))],
        out_specs=pl.BlockSpec((r, n), lambda g, i: (i[0], 0)))
    return _pcall(_place_rows_body, name=name, grid_spec=grid_spec,
                  out_shape=jax.ShapeDtypeStruct((n_blocks * r, n), x.dtype))(i_arr, x)


def _ag_forward_body(*refs, n_w):
    bufs = refs[n_w:2 * n_w]
    send_sems, recv_sems = refs[2 * n_w:]
    pairs = _fw_descs(bufs, send_sems, recv_sems)
    for fw, _ in pairs:
        fw.start()
    for fw, back in pairs:
        back.wait_recv()
        fw.wait_send()


def _ag_forward(bufs, tag):
    n_w = len(bufs)
    return list(_pcall(
        functools.partial(_ag_forward_body, n_w=n_w), name="weight_allgather_forward_" + tag,
        in_specs=[ANY] * n_w, out_specs=[ANY] * n_w,
        out_shape=[jax.ShapeDtypeStruct(b.shape, b.dtype) for b in bufs],
        input_output_aliases={w: w for w in range(n_w)},
        scratch_shapes=[pltpu.SemaphoreType.DMA((3 * n_w,))] * 2,
    )(*bufs))


def _pair_send_body(*refs, n_w):
    pairs = _EX_DESCS(refs[:2 * n_w], refs[2 * n_w], refs[2 * n_w + 1])
    for cp, _ in pairs:
        cp.start()
    for cp, _ in pairs:
        cp.wait()


def _pair_send(grads4, tag):
    n_w = len(grads4)
    return _pcall(
        functools.partial(_pair_send_body, n_w=n_w), name="grad_pair_exchange_" + tag,
        in_specs=[ANY] * n_w, out_specs=[ANY] * n_w,
        out_shape=[jax.ShapeDtypeStruct((g.shape[0],) + g.shape[2:], g.dtype) for g in grads4],
        scratch_shapes=[pltpu.SemaphoreType.DMA((n_w,))] * 2,
    )(*grads4)


def _pair_add_body(c_ref, mine_ref, theirs_ref, o_ref):
    o_ref[...] = (mine_ref[...].astype(F32) + theirs_ref[...].astype(F32)).astype(o_ref.dtype)


def _pair_add(c_arr, g4, recv, name):
    _, _, hr, cols = g4.shape
    tr = min(hr, 256)
    grid_spec = pltpu.PrefetchScalarGridSpec(
        num_scalar_prefetch=1, grid=(N_CHIPS, hr // tr),
        in_specs=[pl.BlockSpec((None, None, tr, cols), lambda s, i, c: (s, c[0], i, 0)),
                  pl.BlockSpec((None, tr, cols), lambda s, i, c: (s, i, 0))],
        out_specs=pl.BlockSpec((None, tr, cols), lambda s, i, c: (s, i, 0)))
    return _pcall(
        _pair_add_body, name=name, grid_spec=grid_spec,
        out_shape=jax.ShapeDtypeStruct(recv.shape, recv.dtype),
        compiler_params=pltpu.CompilerParams(dimension_semantics=("parallel", "parallel")),
    )(c_arr, g4, recv)


def _sum4_body(me_ref, p_ref, l0, l1, l2, l3, o_ref):
    me = me_ref[0]
    t = [jnp.where(me == j, p_ref[...], l[...]).astype(F32) for j, l in enumerate((l0, l1, l2, l3))]
    o_ref[...] = ((t[0] + t[1]) + t[2]) + t[3]


def _sum4(me_arr, part, landed, name):
    _, hr, cols = part.shape
    tr = min(hr, 256)

    def slot(j):
        return lambda i, me: (jnp.where(me[0] == j, (j + 1) % N_CHIPS, j), i, 0)

    grid_spec = pltpu.PrefetchScalarGridSpec(
        num_scalar_prefetch=1, grid=(hr // tr,),
        in_specs=[pl.BlockSpec((None, tr, cols), lambda i, me: (me[0], i, 0))]
        + [pl.BlockSpec((None, tr, cols), slot(j)) for j in range(N_CHIPS)],
        out_specs=pl.BlockSpec((tr, cols), lambda i, me: (i, 0)))
    return _pcall(
        _sum4_body, name=name, grid_spec=grid_spec,
        out_shape=jax.ShapeDtypeStruct((hr, cols), F32),
        compiler_params=pltpu.CompilerParams(dimension_semantics=("parallel",)),
    )(me_arr, part, landed, landed, landed, landed)


def _adamw(w, g, m, v):
    m = ADAM_B1 * m + (1.0 - ADAM_B1) * g
    v = ADAM_B2 * v + (1.0 - ADAM_B2) * (g * g)
    m_hat = m / (1.0 - ADAM_B1 ** ADAM_STEP)
    v_hat = v / (1.0 - ADAM_B2 ** ADAM_STEP)
    delta = -ADAM_LR * (m_hat / (jnp.sqrt(v_hat) + ADAM_EPS) + ADAM_WD * w)
    return delta, m, v


def _adamw_half_body(h_ref, w_ref, g_in_ref, m_ref, v_ref, *rest):
    g_ref, d_ref, nm_ref, nv_ref, done_ref = rest[-5:]
    done_ref[...] = jnp.zeros_like(done_ref)
    g = g_in_ref[...]
    g_ref[...] = g
    d, m, v = _adamw(w_ref[...], g, m_ref[...], v_ref[...])
    d_ref[...] = d
    nm_ref[...] = m
    nv_ref[...] = v


def _adamw_half(h_arr, w, g_half, m, v, prev, name):
    rows, cols = w.shape
    tr = min(rows // 2, 256)
    nh = (rows // 2) // tr
    at_half = pl.BlockSpec((tr, cols), lambda i, h: (h[0] * nh + i, 0))
    grid_spec = pltpu.PrefetchScalarGridSpec(
        num_scalar_prefetch=1, grid=(nh,),
        in_specs=[at_half, pl.BlockSpec((tr, cols), lambda i, h: (i, 0)), at_half, at_half] + [ANY] * len(prev),
        out_specs=[at_half] * 4 + [pl.BlockSpec((8, HEAD), lambda i, h: (0, 0))])
    return list(_pcall(
        _adamw_half_body, name=name, grid_spec=grid_spec,
        out_shape=[jax.ShapeDtypeStruct(w.shape, F32)] * 4 + [jax.ShapeDtypeStruct((8, HEAD), F32)],
        input_output_aliases={5 + k: k for k in range(len(prev))},
        compiler_params=pltpu.CompilerParams(dimension_semantics=("arbitrary",)),
    )(h_arr, w, g_half, m, v, *prev))


def _small_update_body(gath_ref, w_ref, m_ref, v_ref, g_ref, d_ref, nm_ref, nv_ref, loss_ref, *, n_gain):
    tot = gath_ref[0:1, :]
    for i in range(1, gath_ref.shape[0]):
        tot = tot + gath_ref[i:i + 1, :]
    g = tot[:, 0:n_gain]
    g_ref[...] = g
    d, m, v = _adamw(w_ref[...], g, m_ref[...], v_ref[...])
    d_ref[...] = d
    nm_ref[...] = m
    nv_ref[...] = v
    loss_ref[...] = (0.5 / D_MODEL) * jnp.sum(tot[:, n_gain:n_gain + HEAD], axis=1, keepdims=True) * jnp.ones((1, HEAD), F32)


def _small_update(gath, w, m, v):
    n_gain = w.shape[1]
    vm = pl.BlockSpec(memory_space=pltpu.VMEM)
    return _pcall(
        functools.partial(_small_update_body, n_gain=n_gain), name="gain_update",
        in_specs=[vm] * 4, out_specs=[vm] * 5,
        out_shape=[jax.ShapeDtypeStruct((1, n_gain), F32)] * 4 + [jax.ShapeDtypeStruct((1, HEAD), F32)],
    )(gath, w, m, v)


def kernel(x, positions, norm_attn_pre, norm_attn_post, w_in, q_latent_norm, kv_latent_norm, w_uq, w_ukv, w_out, norm_mlp_pre, norm_mlp_post, w_up, w_down, loss_target, m_norm_attn_pre, m_norm_attn_post, m_w_in, m_q_latent_norm, m_kv_latent_norm, m_w_uq, m_w_ukv, m_w_out, m_norm_mlp_pre, m_norm_mlp_post, m_w_up, m_w_down, v_norm_attn_pre, v_norm_attn_post, v_w_in, v_q_latent_norm, v_kv_latent_norm, v_w_uq, v_w_ukv, v_w_out, v_norm_mlp_pre, v_norm_mlp_post, v_w_up, v_w_down):
    T = x.shape[1]
    c_arr = lax.axis_index("c").astype(jnp.int32).reshape(1)
    me_arr = (2 * lax.axis_index("x") + lax.axis_index("y")).astype(jnp.int32).reshape(1)
    names = ["w_in", "w_uq", "w_ukv", "w_out", "w_up", "w_down"]

    mats = [w_in[0], w_uq[0], w_ukv[0], w_out[0], w_up[0], w_down[0]]
    me8_arr = (4 * lax.axis_index("x") + 2 * lax.axis_index("y") + lax.axis_index("c")).astype(jnp.int32).reshape(1)
    col_major = lambda g: jnp.transpose(g, (1, 0, 2)).reshape(g.shape[1], N_CHIPS * g.shape[2])
    cast = lambda a: a.astype(MXU_DTYPE)
    to_shards = lambda g: jnp.transpose(g.reshape(g.shape[0], N_CHIPS, g.shape[1] // N_CHIPS), (1, 0, 2))
    halved = lambda g: g.reshape(N_CHIPS, 2, g.shape[1] // 2, g.shape[2])
    empty = lambda a, shape=None: lax.empty(a.shape if shape is None else shape, a.dtype)

    sem_in, buf_in, going = _copy_start(_ag_descs, [_cast_place(me_arr, mats[0], "cast_w_in")], None,
                                        "weight_allgather_start_in", 3)
    placed = [_cast_place(me_arr, w, "cast_" + n, going) for w, n in zip(mats[1:], names[1:])]
    sem_att, buf_att, going = _copy_start(_ag_descs, placed[:3], going, "weight_allgather_start_attn", 9)
    sem_mlp, buf_mlp, started = _copy_start(_ag_descs, placed[3:], going, "weight_allgather_start_mlp", 6)

    def in_weights(after):
        (win_g,) = _ag_forward(_copy_wait(_ag_descs, sem_in, buf_in, after, "weight_allgather_wait_in"), "in")
        return cast(_w_in_natural(win_g))

    def attn_weights(after):
        wuq_g, wukv_g, wout_g = _ag_forward(
            _copy_wait(_ag_descs, sem_att, buf_att, after, "weight_allgather_wait_attn"), "attn")
        wuq_full = col_major(wuq_g).reshape(LORA, NH, HEAD + ROPE_B)
        w_uq_p = jnp.pad(wuq_full, ((0, 0), (0, 0), (0, QPAD - HEAD - ROPE_B))).reshape(LORA, NH * QPAD)
        w_ukv_p = col_major(wukv_g).reshape(LORA, NH, 2, HEAD).transpose(0, 2, 1, 3).reshape(LORA, 2 * A_W)
        return cast(w_uq_p), cast(w_ukv_p), cast(wout_g.reshape(2 * A_W, D_MODEL))

    going_on = {}

    def mlp_prefetch(after):
        landed = _copy_wait(_ag_descs, sem_mlp, buf_mlp, after, "weight_allgather_wait_mlp")
        going_on["fw"] = _copy_start(_fw_descs, landed, None, "weight_allgather_forward_start_mlp", 6)

    def mlp_weights(after):
        sems, bufs, _ = going_on["fw"]
        wup_g, wdown_g = _copy_wait(_fw_descs, sems, bufs, after, "weight_allgather_forward_wait_mlp")
        return cast(wup_g), cast(wdown_g.reshape(D_FF, D_MODEL))

    def exchange_start(g4s, tag):
        lands = [empty(g, (g.shape[0],) + g.shape[2:]) for g in g4s]
        return _copy_start(_EX_DESCS, g4s + lands, None, "grad_pair_exchange_start_" + tag, len(g4s))

    def exchange_finish(started_ex, after, ns, tag):
        sems, arrs, _ = started_ex
        arrs = _copy_wait(_EX_DESCS, sems, arrs, after, "grad_pair_exchange_wait_" + tag)
        n = len(ns)
        return [_pair_add(c_arr, g4, r, "pair_add_" + nm) for g4, r, nm in zip(arrs[:n], arrs[n:], ns)]

    def scatter_start(parts, after, tag):
        return _copy_start(_sc_descs, parts + [empty(p) for p in parts], after, "grad_scatter_start_" + tag,
                           3 * len(parts))

    def scatter_finish(started_sc, after, tag):
        sems, arrs, _ = started_sc
        arrs = _copy_wait(_sc_descs, sems, arrs, after, "grad_scatter_wait_" + tag)
        return arrs[:len(arrs) // 2], arrs[len(arrs) // 2:]

    def down_grad_ready(gw_down):
        going_on["x_down"] = exchange_start([halved(gw_down.reshape(N_CHIPS, D_MODEL, D_MODEL))], "down")
        return going_on["x_down"][-1]

    def up_grad_ready(gw_up):
        going_on["x_up"] = exchange_start([halved(gw_up)], "up")
        parts = exchange_finish(going_on["x_down"], going_on["x_up"][-1], names[5:], "down")
        going_on["s_down"] = scatter_start(parts, started, "down")
        return going_on["s_down"][-1][0:1, 0:1]

    def attn_grads_ready(gw_out, gw_uq_p, gw_ukv_p):
        gw_uq = to_shards(gw_uq_p.reshape(LORA, NH, QPAD)[:, :, :HEAD + ROPE_B].reshape(LORA, NH * (HEAD + ROPE_B)))
        gw_ukv = to_shards(gw_ukv_p.reshape(LORA, 2, NH, HEAD).transpose(0, 2, 1, 3).reshape(LORA, 2 * A_W))
        full4 = [halved(g) for g in (gw_uq, gw_ukv, gw_out.reshape(N_CHIPS, LORA, D_MODEL))]
        from_sib = _pair_send(full4, "attn")
        parts = [_pair_add(c_arr, g4, r, "pair_add_" + n) for g4, r, n in zip(full4, from_sib, names[1:4])]
        parts += exchange_finish(going_on["x_up"], from_sib[-1], names[4:5], "up")
        going_on["s_rest"] = scatter_start(parts, going_on["s_down"][-1], "attn_up")
        return going_on["s_rest"][-1][0:1, 0:1]

    dx, gw_proj, small = _local_step(
        x[0], positions[0].astype(F32).reshape(T, 1), loss_target[0],
        norm_attn_pre + started[0:1, 0:1], norm_attn_post, q_latent_norm, kv_latent_norm, norm_mlp_pre, norm_mlp_post,
        in_weights, attn_weights, mlp_prefetch, mlp_weights, down_grad_ready, up_grad_ready, attn_grads_ready)

    ms = [m_w_in[0], m_w_uq[0], m_w_ukv[0], m_w_out[0], m_w_up[0], m_w_down[0]]
    vs = [v_w_in[0], v_w_uq[0], v_w_ukv[0], v_w_out[0], v_w_up[0], v_w_down[0]]
    sib_arr = 1 - c_arr

    def finish(parts, landed, lo, hi, tag):
        sl = slice(lo, hi)
        halves = [_sum4(me_arr, p, l, "chip_sum_" + n) for p, l, n in zip(parts, landed, names[sl])]
        n = len(halves)
        sems, arrs, _ = _copy_start(_SW_DESCS, halves + [empty(h) for h in halves], None,
                                    "grad_pair_swap_start_" + tag, n)
        own = [_adamw_half(c_arr, w, g, m, v, [], "adamw_own_" + nm)
               for w, g, m, v, nm in zip(mats[sl], arrs[:n], ms[sl], vs[sl], names[sl])]
        arrs = _copy_wait(_SW_DESCS, sems, arrs, own[-1][4], "grad_pair_swap_wait_" + tag)
        return [_adamw_half(sib_arr, w, g, m, v, prev[:4], "adamw_sib_" + nm)
                for w, g, m, v, prev, nm in zip(mats[sl], arrs[n:], ms[sl], vs[sl], own, names[sl])]

    sem_small, (gath,), small_going = _copy_start(
        _sm_descs, [_place_rows(me8_arr, small, N_DEV, "place_small")], None, "small_allgather_start", N_DEV - 1)
    x_in = exchange_start([halved(_gw_in_shards(gw_proj))], "in")
    s_in = scatter_start(exchange_finish(x_in, small_going, names[:1], "in"), None, "in")
    parts_rest, landed_rest = scatter_finish(going_on["s_rest"], s_in[-1], "attn_up")
    parts_down, landed_down = scatter_finish(going_on["s_down"], landed_rest[0], "down")
    upd_rest = finish(parts_rest + parts_down, landed_rest + landed_down, 1, 6, "rest")
    parts_in, landed_in = scatter_finish(s_in, upd_rest[-1][0], "in")
    upd = finish(parts_in, landed_in, 0, 1, "in") + upd_rest
    grads = [u[0] for u in upd]

    (gath,) = _copy_wait(_sm_descs, sem_small, [gath], grads[0], "small_allgather_wait")
    gains = [norm_attn_pre, norm_attn_post, q_latent_norm, kv_latent_norm, norm_mlp_pre, norm_mlp_post]
    gm = [m_norm_attn_pre, m_norm_attn_post, m_q_latent_norm, m_kv_latent_norm, m_norm_mlp_pre, m_norm_mlp_post]
    gv = [v_norm_attn_pre, v_norm_attn_post, v_q_latent_norm, v_kv_latent_norm, v_norm_mlp_pre, v_norm_mlp_post]
    cat = lambda xs: jnp.concatenate(xs, axis=1)
    g_s, d_s, m_s, v_s, loss_v = _small_update(gath, cat(gains), cat(gm), cat(gv))
    widths = [a.shape[1] for a in gains]
    offs = [sum(widths[:i]) for i in range(len(widths))]
    split = lambda a: [a[:, o:o + w] for o, w in zip(offs, widths)]
    g_gain, d_gain, m_gain, v_gain = split(g_s), split(d_s), split(m_s), split(v_s)

    def ordered(gain_list, mat_list):
        gl, ml = gain_list, [a[None] for a in mat_list]
        return [gl[0], gl[1], ml[0], gl[2], gl[3], ml[1], ml[2], ml[3], gl[4], gl[5], ml[4], ml[5]]

    loss = loss_v[0, 0]
    return (loss, dx[None],
            *ordered(g_gain, grads),
            *ordered(d_gain, [u[1] for u in upd]),
            *ordered(m_gain, [u[2] for u in upd]),
            *ordered(v_gain, [u[3] for u in upd]))
```

```python
import functools

import jax
import jax.numpy as jnp
from jax import lax
from jax.experimental import pallas as pl
from jax.experimental.pallas import tpu as pltpu

F32 = jnp.float32
BF16 = jnp.bfloat16
MXU_DTYPE = jnp.bfloat16
WIRE_DTYPE = jnp.bfloat16

D_MODEL = 2048
HEAD = 128
NH = 8
A_W = NH * HEAD
LORA = 512
ROPE_B = 64
QPAD = 256
MAIN_COLS = 3 * A_W + 2 * LORA
IN_COLS = MAIN_COLS + ROPE_B
PROJ_COLS = MAIN_COLS + HEAD
PROJ_TILE = PROJ_COLS // 3
D_FF = 4 * D_MODEL
DIL = (1, 4, 16)
ROT_A = 32
ROPE_THETA = 500000.0
EPS = 1e-6
NEG = -1e30
N_CHIPS = 4
N_DEV = 8

ADAM_LR = 0.001
ADAM_B1 = 0.9
ADAM_B2 = 0.999
ADAM_EPS = 1e-08
ADAM_WD = 0.01
ADAM_STEP = 10

MESH = pl.DeviceIdType.MESH
ANY = pl.BlockSpec(memory_space=pl.ANY)


def _pcall(body, **kw):
    return pl.pallas_call(body, **kw)


_DIMS = {
    "nn": (((1,), (0,)), ((), ())),
    "nt": (((1,), (1,)), ((), ())),
    "tn": (((0,), (0,)), ((), ())),
}


def _mm_body(*refs, dims, nk, epi, n_extra, n_after, n_out):
    a_ref, b_ref = refs[0], refs[1]
    extra = refs[2:2 + n_extra]
    outs = refs[2 + n_extra + n_after:2 + n_extra + n_after + n_out]
    part = lax.dot_general(a_ref[...], b_ref[...], _DIMS[dims], preferred_element_type=F32)

    def finish(acc):
        res = epi(acc, *[r[...] for r in extra]) if epi is not None else (acc,)
        for o_ref, o in zip(outs, res):
            o_ref[...] = o.astype(o_ref.dtype)

    if nk == 1:
        finish(part)
        return
    acc_ref = refs[-1]
    k = pl.program_id(2)

    @pl.when(k == 0)
    def _():
        acc_ref[...] = part

    @pl.when(k > 0)
    def _():
        acc_ref[...] += part

    @pl.when(k == nk - 1)
    def _():
        finish(acc_ref[...])


def _matmul(a, b, *, dims, out_dtypes, tm, tn, tk, name, epi=None, extras=(), row_extras=(), b_outer=False,
            b_shards=0, out_shards=0, after=None):
    if b_shards:
        assert dims in ("nn", "nt") and b.shape[0] == b_shards
        b2 = (b.shape[1], b_shards * b.shape[2])
    else:
        b2 = b.shape
    if dims == "nn":
        (M, K), (K2, N) = a.shape, b2
    elif dims == "nt":
        (M, K), (N, K2) = a.shape, b2
    else:
        (K, M), (K2, N) = a.shape, b2
    assert K == K2, (a.shape, b.shape, dims)
    tm, tn, tk = min(tm, M), min(tn, N), min(tk, K)
    assert M % tm == 0 and N % tn == 0 and K % tk == 0, (name, M, N, K, tm, tn, tk)
    nk = K // tk

    def at(f):
        if b_outer:
            return lambda j, i, k: f(i, j, k)
        return f

    a_spec = {"nn": pl.BlockSpec((tm, tk), at(lambda i, j, k: (i, k))),
              "nt": pl.BlockSpec((tm, tk), at(lambda i, j, k: (i, k))),
              "tn": pl.BlockSpec((tk, tm), at(lambda i, j, k: (k, i)))}[dims]
    b_spec = {"nn": pl.BlockSpec((tk, tn), at(lambda i, j, k: (k, j))),
              "nt": pl.BlockSpec((tn, tk), at(lambda i, j, k: (j, k))),
              "tn": pl.BlockSpec((tk, tn), at(lambda i, j, k: (k, j)))}[dims]
    if b_shards:
        per = b.shape[2] // (tn if dims == "nn" else tk)
        assert per >= 1 and b.shape[2] % (tn if dims == "nn" else tk) == 0
        b_spec = {"nn": pl.BlockSpec((None, tk, tn), at(lambda i, j, k: (j // per, k, j % per))),
                  "nt": pl.BlockSpec((None, tn, tk), at(lambda i, j, k: (k // per, j, k % per)))}[dims]
    o_spec = pl.BlockSpec((tm, tn), at(lambda i, j, k: (i, j)))
    o_shape = (M, N)
    if out_shards:
        assert not extras and N % out_shards == 0 and (N // out_shards) % tn == 0
        o_per = (N // out_shards) // tn
        o_spec = pl.BlockSpec((None, tm, tn), at(lambda i, j, k: (j // o_per, i, j % o_per)))
        o_shape = (out_shards, M, N // out_shards)
    r_specs = [pl.BlockSpec((tm, r.shape[1]), at(lambda i, j, k: (i, 0))) for r in row_extras]
    after = [] if after is None else [after]
    body = functools.partial(_mm_body, dims=dims, nk=nk, epi=epi, n_extra=len(extras) + len(row_extras),
                             n_after=len(after), n_out=len(out_dtypes))
    res = _pcall(
        body, name=name,
        grid=(N // tn, M // tm, nk) if b_outer else (M // tm, N // tn, nk),
        in_specs=[a_spec, b_spec] + [o_spec] * len(extras) + r_specs + [ANY] * len(after),
        out_specs=[o_spec] * len(out_dtypes),
        out_shape=[jax.ShapeDtypeStruct(o_shape, dt) for dt in out_dtypes],
        scratch_shapes=[pltpu.VMEM((tm, tn), F32)] if nk > 1 else [],
        compiler_params=pltpu.CompilerParams(
            dimension_semantics=("parallel", "parallel", "arbitrary")),
    )(a, b, *extras, *row_extras, *after)
    return list(res)


def _rowwise(body, row_ins, vec_ins, row_outs, acc_outs, *, tr, name):
    T = row_ins[0].shape[0]
    tr = min(tr, T)
    assert T % tr == 0
    in_specs = [pl.BlockSpec((tr, a.shape[1]), lambda i: (i, 0)) for a in row_ins]
    in_specs += [pl.BlockSpec(a.shape, lambda i: (0, 0)) for a in vec_ins]
    out_specs = [pl.BlockSpec((tr, w), lambda i: (i, 0)) for (w, _) in row_outs]
    out_specs += [pl.BlockSpec(s, lambda i: (0, 0)) for s in acc_outs]
    out_shape = [jax.ShapeDtypeStruct((T, w), dt) for (w, dt) in row_outs]
    out_shape += [jax.ShapeDtypeStruct(s, F32) for s in acc_outs]
    sem = "arbitrary" if acc_outs else "parallel"
    return list(_pcall(
        body, name=name, grid=(T // tr,), in_specs=in_specs, out_specs=out_specs,
        out_shape=out_shape,
        compiler_params=pltpu.CompilerParams(dimension_semantics=(sem,)),
    )(*row_ins, *vec_ins))


def _rstd(x):
    return lax.rsqrt(jnp.mean(x * x, axis=-1, keepdims=True) + EPS)


def _rms_bwd(x, rstd, dyg):
    xh = x * rstd
    return rstd * (dyg - xh * jnp.mean(dyg * xh, axis=-1, keepdims=True)), xh


def _fold8(v):
    r, w = v.shape
    return jnp.sum(v.reshape(r // 8, 8, w), axis=0)


def _acc(ref, val):
    first = pl.program_id(0) == 0

    @pl.when(first)
    def _():
        ref[...] = val

    @pl.when(jnp.logical_not(first))
    def _():
        ref[...] += val


def _rope(x, c, sa, sb, half):
    return x * c + pltpu.roll(x, HEAD - half, 1) * sa + pltpu.roll(x, half, 1) * sb


def _rope_t(dy, c, sa, sb, half):
    return dy * c - pltpu.roll(dy, HEAD - half, 1) * sa - pltpu.roll(dy, half, 1) * sb


def _rope_tab_body(pos_ref, inv_ref, ca, saa, sab, cb, sba, sbb):
    pos = pos_ref[...]
    lane = lax.broadcasted_iota(jnp.int32, (pos.shape[0], HEAD), 1)
    ang_a = pos * inv_ref[0:1, :]
    ang_b = pos * inv_ref[1:2, :]
    c, s = jnp.cos(ang_a), jnp.sin(ang_a)
    ha = ROT_A // 2
    ca[...] = jnp.where(lane < ROT_A, c, 1.0)
    saa[...] = jnp.where(lane < ha, -s, 0.0)
    sab[...] = jnp.where((lane >= ha) & (lane < ROT_A), s, 0.0)
    c, s = jnp.cos(ang_b), jnp.sin(ang_b)
    hb = ROPE_B // 2
    cb[...] = jnp.where(lane < ROPE_B, c, 1.0)
    sba[...] = jnp.where(lane < hb, -s, 0.0)
    sbb[...] = jnp.where((lane >= hb) & (lane < ROPE_B), s, 0.0)


def _rms_fwd_body(x_ref, g_ref, h_ref):
    x = x_ref[...]
    h_ref[...] = ((x * _rstd(x)) * g_ref[...]).astype(h_ref.dtype)


def _postproj_body(p_ref, ca, saa, sab, cb, sba, sbb, gq_ref, gkv_ref,
                   q_ref, k_ref, v_ref, cqn_ref, ckvn_ref, krope_ref):
    c, sa, sb = ca[...], saa[...], sab[...]
    for h in range(NH):
        lo = h * HEAD
        q_ref[:, lo:lo + HEAD] = _rope(p_ref[:, lo:lo + HEAD], c, sa, sb, ROT_A // 2).astype(q_ref.dtype)
        k_ref[:, lo:lo + HEAD] = _rope(p_ref[:, A_W + lo:A_W + lo + HEAD], c, sa, sb, ROT_A // 2).astype(k_ref.dtype)
    v_ref[...] = p_ref[:, 2 * A_W:3 * A_W].astype(v_ref.dtype)
    cq = p_ref[:, 3 * A_W:3 * A_W + LORA]
    cqn_ref[...] = ((cq * _rstd(cq)) * gq_ref[...]).astype(cqn_ref.dtype)
    ckv = p_ref[:, 3 * A_W + LORA:MAIN_COLS]
    ckvn_ref[...] = ((ckv * _rstd(ckv)) * gkv_ref[...]).astype(ckvn_ref.dtype)
    krope_ref[...] = _rope(p_ref[:, MAIN_COLS:PROJ_COLS], cb[...], sba[...], sbb[...], ROPE_B // 2).astype(krope_ref.dtype)


def _mid_body(x_ref, o_ref, g2_ref, g3_ref, x1_ref, h2_ref):
    o = o_ref[...]
    x1 = x_ref[...] + (o * _rstd(o)) * g2_ref[...]
    x1_ref[...] = x1
    h2_ref[...] = ((x1 * _rstd(x1)) * g3_ref[...]).astype(h2_ref.dtype)


def _loss_body(x1_ref, d_ref, t_ref, g4_ref, dy_ref, dd_ref, loss_ref, dg4_ref):
    d = d_ref[...]
    rstd = _rstd(d)
    y = x1_ref[...] + (d * rstd) * g4_ref[...]
    e = y - t_ref[...]
    dy = e * (1.0 / D_MODEL)
    dy_ref[...] = dy
    dd, dh = _rms_bwd(d, rstd, dy * g4_ref[...])
    dd_ref[...] = dd.astype(dd_ref.dtype)
    _acc(dg4_ref, _fold8(dy * dh))
    e8 = _fold8(e * e)
    l = e8[:, 0:HEAD]
    for j in range(1, D_MODEL // HEAD):
        l = l + e8[:, j * HEAD:(j + 1) * HEAD]
    _acc(loss_ref, l)


def _bmid_body(dy_ref, dh2_ref, x1_ref, o_ref, g2_ref, g3_ref, dx1_ref, do_ref, dg3_ref, dg2_ref):
    x1 = x1_ref[...]
    dh2 = dh2_ref[...]
    dn, x1h = _rms_bwd(x1, _rstd(x1), dh2 * g3_ref[...])
    dx1 = dy_ref[...] + dn
    dx1_ref[...] = dx1
    _acc(dg3_ref, _fold8(dh2 * x1h))
    o = o_ref[...]
    do, oh = _rms_bwd(o, _rstd(o), dx1 * g2_ref[...])
    do_ref[...] = do.astype(do_ref.dtype)
    _acc(dg2_ref, _fold8(dx1 * oh))


def _dproj_body(dq_ref, dk_ref, dv_ref, dcq_ref, dckv_ref, p_ref, dkr_ref,
                ca, saa, sab, cb, sba, sbb, gq_ref, gkv_ref,
                dp_ref, dgq_ref, dgkv_ref):
    c, sa, sb = ca[...], saa[...], sab[...]
    for h in range(NH):
        lo = h * HEAD
        dp_ref[:, lo:lo + HEAD] = _rope_t(dq_ref[:, lo:lo + HEAD], c, sa, sb, ROT_A // 2).astype(dp_ref.dtype)
        dp_ref[:, A_W + lo:A_W + lo + HEAD] = _rope_t(dk_ref[:, lo:lo + HEAD], c, sa, sb, ROT_A // 2).astype(dp_ref.dtype)
    dp_ref[:, 2 * A_W:3 * A_W] = dv_ref[...].astype(dp_ref.dtype)
    cq = p_ref[:, 3 * A_W:3 * A_W + LORA]
    dcqn = dcq_ref[...]
    dcq, cqh = _rms_bwd(cq, _rstd(cq), dcqn * gq_ref[...])
    dp_ref[:, 3 * A_W:3 * A_W + LORA] = dcq.astype(dp_ref.dtype)
    _acc(dgq_ref, _fold8(dcqn * cqh))
    ckv = p_ref[:, 3 * A_W + LORA:MAIN_COLS]
    dckvn = dckv_ref[...]
    dckv, ckvh = _rms_bwd(ckv, _rstd(ckv), dckvn * gkv_ref[...])
    dp_ref[:, 3 * A_W + LORA:MAIN_COLS] = dckv.astype(dp_ref.dtype)
    _acc(dgkv_ref, _fold8(dckvn * ckvh))
    dkr = dkr_ref[:, 0:HEAD]
    for h in range(1, NH):
        dkr = dkr + dkr_ref[:, h * HEAD:(h + 1) * HEAD]
    dp_ref[:, MAIN_COLS:PROJ_COLS] = _rope_t(dkr, cb[...], sba[...], sbb[...], ROPE_B // 2).astype(dp_ref.dtype)


def _bin_body(dx1_ref, dh_ref, x_ref, g1_ref, dx_ref, dg1_ref):
    x = x_ref[...]
    dh = dh_ref[...]
    dn, xh = _rms_bwd(x, _rstd(x), dh * g1_ref[...])
    dx_ref[...] = dx1_ref[...] + dn
    _acc(dg1_ref, _fold8(dh * xh))


def _dot_nt(a, b):
    return lax.dot_general(a, b, _DIMS["nt"], preferred_element_type=F32)


def _dot_tn(a, b):
    return lax.dot_general(a, b, _DIMS["tn"], preferred_element_type=F32)


def _dot_nn(a, b):
    return jnp.dot(a, b, preferred_element_type=F32)


DIL_SCALE = HEAD ** -0.5
DIL_CHUNK = 256


def _dil_rows(t, d):
    r = t & (d - 1)
    n = t >> (d.bit_length() - 1)
    start = r + n * (HEAD * d)
    has_prev = n > 0
    pstart = jnp.where(has_prev, start - HEAD * d, start)
    if d == 1:
        return pl.ds(pl.multiple_of(start, HEAD), HEAD), pl.ds(pl.multiple_of(pstart, HEAD), HEAD), has_prev
    return pl.ds(start, HEAD, stride=d), pl.ds(pstart, HEAD, stride=d), has_prev


def _dil_band():
    row = lax.broadcasted_iota(jnp.int32, (HEAD, 2 * HEAD), 0)
    col = lax.broadcasted_iota(jnp.int32, (HEAD, 2 * HEAD), 1)
    return (col >= row) & (col <= row + HEAD), col >= HEAD


def _dil_fwd_body(q_ref, k_ref, v_ref, a_ref, lse_ref, o1, o2, o3, l1, l2, l3, *, nt, unroll):
    band, is_cur = _dil_band()
    for d, o_sc, l_sc in zip(DIL, (o1, o2, o3), (l1, l2, l3)):

        def tile(t, carry, d=d, o_sc=o_sc, l_sc=l_sc):
            rows, prows, has_prev = _dil_rows(t, d)
            q = q_ref[rows, :].astype(MXU_DTYPE)
            kk = jnp.concatenate([k_ref[prows, :], k_ref[rows, :]], axis=0).astype(MXU_DTYPE)
            vv = jnp.concatenate([v_ref[prows, :], v_ref[rows, :]], axis=0).astype(MXU_DTYPE)
            ok = band & (is_cur | has_prev)
            s = jnp.where(ok, _dot_nt(q, kk) * DIL_SCALE, NEG)
            m = jnp.max(s, axis=1, keepdims=True)
            p = jnp.exp(s - m)
            den = jnp.sum(p, axis=1, keepdims=True)
            o_sc[rows, :] = _dot_nn((p / den).astype(MXU_DTYPE), vv)
            l_sc[rows, :] = jnp.broadcast_to(m + jnp.log(den), (HEAD, HEAD))
            return carry

        lax.fori_loop(0, nt, tile, 0, unroll=unroll)

    def merge(i, carry):
        rs = pl.ds(pl.multiple_of(i * DIL_CHUNK, DIL_CHUNK), DIL_CHUNK)
        la, lb, lc = l1[rs, :], l2[rs, :], l3[rs, :]
        m = jnp.maximum(jnp.maximum(la, lb), lc)
        wa, wb, wc = jnp.exp(la - m), jnp.exp(lb - m), jnp.exp(lc - m)
        den = wa + wb + wc
        a = (wa / den) * o1[rs, :] + (wb / den) * o2[rs, :] + (wc / den) * o3[rs, :]
        a_ref[rs, :] = a.astype(a_ref.dtype)
        lse_ref[rs, :] = m + jnp.log(den)
        return carry

    lax.fori_loop(0, q_ref.shape[0] // DIL_CHUNK, merge, 0)


def _dil_fwd(q, k, v):
    T = q.shape[0]
    spec = pl.BlockSpec((T, HEAD), lambda h: (0, h))
    return _pcall(
        functools.partial(_dil_fwd_body, nt=T // HEAD, unroll=16), name="dil_fwd",
        grid=(NH,), in_specs=[spec] * 3, out_specs=[spec] * 2,
        out_shape=[jax.ShapeDtypeStruct((T, 2 * A_W), MXU_DTYPE), jax.ShapeDtypeStruct((T, A_W), F32)],
        scratch_shapes=[pltpu.VMEM((T, HEAD), F32)] * 6,
        compiler_params=pltpu.CompilerParams(dimension_semantics=("parallel",)),
    )(q, k, v)


def _dil_bwd_body(q_ref, k_ref, v_ref, do_ref, a_ref, lse_ref, dq_ref, dk_ref, dv_ref, dl_sc, *, nt, unroll):
    band, is_cur = _dil_band()

    def prep(i, carry):
        rs = pl.ds(pl.multiple_of(i * DIL_CHUNK, DIL_CHUNK), DIL_CHUNK)
        dl = jnp.sum(do_ref[rs, :] * a_ref[rs, :].astype(F32), axis=1, keepdims=True)
        dl_sc[rs, :] = jnp.broadcast_to(dl, (DIL_CHUNK, HEAD))
        zero = jnp.zeros((DIL_CHUNK, HEAD), F32)
        dq_ref[rs, :] = zero
        dk_ref[rs, :] = zero
        dv_ref[rs, :] = zero
        return carry

    lax.fori_loop(0, q_ref.shape[0] // DIL_CHUNK, prep, 0)

    for d in DIL:

        def tile(t, carry, d=d):
            rows, prows, has_prev = _dil_rows(t, d)
            q = q_ref[rows, :].astype(MXU_DTYPE)
            kk = jnp.concatenate([k_ref[prows, :], k_ref[rows, :]], axis=0).astype(MXU_DTYPE)
            vv = jnp.concatenate([v_ref[prows, :], v_ref[rows, :]], axis=0).astype(MXU_DTYPE)
            do = do_ref[rows, :].astype(MXU_DTYPE)
            lse = lse_ref[rows, :]
            dl = dl_sc[rows, :]
            ok = band & (is_cur | has_prev)
            s = _dot_nt(q, kk) * DIL_SCALE
            p = jnp.where(ok, jnp.exp(s - jnp.concatenate([lse, lse], axis=1)), 0.0)
            ds = (p * (_dot_nt(do, vv) - jnp.concatenate([dl, dl], axis=1))).astype(MXU_DTYPE)
            dq_ref[rows, :] += _dot_nn(ds, kk) * DIL_SCALE
            dkk = _dot_tn(ds, q) * DIL_SCALE
            dvv = _dot_tn(p.astype(MXU_DTYPE), do)
            dk_ref[rows, :] += dkk[HEAD:, :]
            dv_ref[rows, :] += dvv[HEAD:, :]
            dk_ref[prows, :] += dkk[:HEAD, :]
            dv_ref[prows, :] += dvv[:HEAD, :]
            return carry

        lax.fori_loop(0, nt, tile, 0, unroll=unroll)


def _dil_bwd(q, k, v, dmix, mixed, lse):
    T = q.shape[0]
    spec = pl.BlockSpec((T, HEAD), lambda h: (0, h))
    return _pcall(
        functools.partial(_dil_bwd_body, nt=T // HEAD, unroll=8), name="dil_bwd",
        grid=(NH,), in_specs=[spec] * 6, out_specs=[spec] * 3,
        out_shape=[jax.ShapeDtypeStruct((T, A_W), F32)] * 3,
        scratch_shapes=[pltpu.VMEM((T, HEAD), F32)],
        compiler_params=pltpu.CompilerParams(dimension_semantics=("parallel",)),
    )(q, k, v, dmix, mixed, lse)


MLA_SCALE = (HEAD + ROPE_B) ** -0.5
LOG2E = 1.4426950408889634
MLA_QSCALE = MLA_SCALE * LOG2E
MLA_T = 512
MLA_HP = 4


def _tri(t):
    row = lax.broadcasted_iota(jnp.int32, (t, t), 0)
    col = lax.broadcasted_iota(jnp.int32, (t, t), 1)
    return col <= row


def _lanes(x, n):
    return jnp.tile(x, (1, n // HEAD))


def _mla_fwd_body(q_ref, kn_ref, kr_ref, v_ref, mixed_ref, o_ref, lse_ref, m_sc, l_sc, acc_sc, *, t, hp):
    del mixed_ref
    qi = pl.program_id(1)
    m_sc[...] = jnp.full(m_sc.shape, NEG, F32)
    l_sc[...] = jnp.zeros(l_sc.shape, F32)
    acc_sc[...] = jnp.zeros(acc_sc.shape, F32)

    def step(j, masked):
        ks = pl.ds(pl.multiple_of(j * t, t), t)
        kr = kr_ref[ks, :]
        logits = []
        for hh in range(hp):
            kcat = jnp.concatenate([kn_ref[ks, hh * HEAD:(hh + 1) * HEAD], kr], axis=1)
            logits.append(_dot_nt(q_ref[:, hh * QPAD:(hh + 1) * QPAD], kcat))
        for hh in range(hp):
            s = logits[hh]
            if masked:
                s = jnp.where(_tri(t), s, NEG)
            m_prev = m_sc[hh]
            m_new = jnp.maximum(m_prev, jnp.max(s, axis=1, keepdims=True))
            alpha = jnp.exp2(m_prev - m_new)
            p = jnp.exp2(s - _lanes(m_new, t))
            l_sc[hh] = alpha * l_sc[hh] + jnp.sum(p, axis=1, keepdims=True)
            acc_sc[hh] = alpha * acc_sc[hh] + _dot_nn(p.astype(MXU_DTYPE), v_ref[ks, hh * HEAD:(hh + 1) * HEAD])
            m_sc[hh] = m_new

    def off_diag(j, carry):
        step(j, False)
        return carry

    lax.fori_loop(0, qi, off_diag, 0)
    step(qi, True)
    for hh in range(hp):
        l = l_sc[hh]
        o_ref[:, hh * HEAD:(hh + 1) * HEAD] = (acc_sc[hh] / l).astype(o_ref.dtype)
        lse_ref[:, hh * HEAD:(hh + 1) * HEAD] = m_sc[hh] + jnp.log2(l)


def _mla_fwd(qf, kv, kr, mixed):
    T = qf.shape[0]
    t, hp = min(MLA_T, T), MLA_HP
    ng = NH // hp
    return _pcall(
        functools.partial(_mla_fwd_body, t=t, hp=hp), name="mla_fwd",
        grid=(ng, T // t),
        in_specs=[pl.BlockSpec((t, hp * QPAD), lambda g, i: (i, g)),
                  pl.BlockSpec((T, hp * HEAD), lambda g, i: (0, g)),
                  pl.BlockSpec((T, HEAD), lambda g, i: (0, 0)),
                  pl.BlockSpec((T, hp * HEAD), lambda g, i: (0, ng + g)), ANY],
        out_specs=[pl.BlockSpec((t, hp * HEAD), lambda g, i: (i, ng + g)),
                   pl.BlockSpec((t, hp * HEAD), lambda g, i: (i, g))],
        out_shape=[jax.ShapeDtypeStruct(mixed.shape, mixed.dtype), jax.ShapeDtypeStruct((T, A_W), F32)],
        input_output_aliases={4: 0},
        scratch_shapes=[pltpu.VMEM((hp, t, HEAD), F32)] * 3,
        compiler_params=pltpu.CompilerParams(dimension_semantics=("parallel", "parallel")),
    )(qf, kv, kr, kv, mixed)


def _mla_bwd_body(q_ref, kn_ref, kr_ref, v_ref, do_ref, o_ref, lse_ref, cb, sba, sbb,
                  dq_ref, dkn_ref, dv_ref, dkr_ref, dq_sc, dl_sc, dk_sc, dv_sc, *, t):
    ki = pl.program_id(1)
    nq = q_ref.shape[0] // t

    @pl.when(ki == 0)
    def _():
        def prep(i, carry):
            rs = pl.ds(pl.multiple_of(i * t, t), t)
            dl = jnp.sum(do_ref[rs, :] * o_ref[rs, :].astype(F32), axis=1, keepdims=True)
            dl_sc[rs, :] = jnp.broadcast_to(dl, (t, HEAD))
            dq_sc[rs, :] = jnp.zeros((t, QPAD), F32)
            return carry
        lax.fori_loop(0, nq, prep, 0)

    kcat = jnp.concatenate([kn_ref[...], kr_ref[...]], axis=1)
    v = v_ref[...]
    dk_sc[...] = jnp.zeros(dk_sc.shape, F32)
    dv_sc[...] = jnp.zeros(dv_sc.shape, F32)

    def step(i, masked):
        qs = pl.ds(pl.multiple_of(i * t, t), t)
        q = q_ref[qs, :]
        do = do_ref[qs, :].astype(MXU_DTYPE)
        p = jnp.exp2(_dot_nt(q, kcat) - _lanes(lse_ref[qs, :], t))
        if masked:
            p = jnp.where(_tri(t), p, 0.0)
        ds = (p * (_dot_nt(do, v) - _lanes(dl_sc[qs, :], t))).astype(MXU_DTYPE)
        dv_sc[...] += _dot_tn(p.astype(MXU_DTYPE), do)
        dk_sc[...] += _dot_tn(ds, q)
        dq_sc[qs, :] += _dot_nn(ds, kcat) * MLA_SCALE

    step(ki, True)

    def off_diag(i, carry):
        step(i, False)
        return carry

    lax.fori_loop(ki + 1, nq, off_diag, 0)
    dk = dk_sc[...] * (1.0 / LOG2E)
    dkn_ref[...] = dk[:, 0:HEAD].astype(dkn_ref.dtype)
    dkr_ref[...] = dk[:, HEAD:QPAD]
    dv_ref[...] = dv_sc[...].astype(dv_ref.dtype)

    @pl.when(ki == nq - 1)
    def _():
        def emit(i, carry):
            rs = pl.ds(pl.multiple_of(i * t, t), t)
            dq_ref[rs, 0:HEAD] = dq_sc[rs, 0:HEAD].astype(dq_ref.dtype)
            dq_ref[rs, HEAD:QPAD] = _rope_t(dq_sc[rs, HEAD:QPAD], cb[rs, :], sba[rs, :], sbb[rs, :],
                                            ROPE_B // 2).astype(dq_ref.dtype)
            return carry
        lax.fori_loop(0, nq, emit, 0)


def _mla_bwd(qf, kv, kr, dmix, mixed, lse, tabs_b):
    T = qf.shape[0]
    t = min(MLA_T, T)
    head = lambda h, j: (0, h)
    b_half = lambda h, j: (0, NH + h)
    kblk = pl.BlockSpec((t, HEAD), lambda h, j: (j, h))
    return _pcall(
        functools.partial(_mla_bwd_body, t=t), name="mla_bwd",
        grid=(NH, T // t),
        in_specs=[pl.BlockSpec((T, QPAD), head), kblk,
                  pl.BlockSpec((t, HEAD), lambda h, j: (j, 0)),
                  pl.BlockSpec((t, HEAD), lambda h, j: (j, NH + h)),
                  pl.BlockSpec((T, HEAD), b_half), pl.BlockSpec((T, HEAD), b_half),
                  pl.BlockSpec((T, HEAD), head)] + [pl.BlockSpec((T, HEAD), lambda h, j: (0, 0))] * 3,
        out_specs=[pl.BlockSpec((T, QPAD), head), kblk, kblk, kblk],
        out_shape=[jax.ShapeDtypeStruct((T, NH * QPAD), MXU_DTYPE), jax.ShapeDtypeStruct((T, A_W), MXU_DTYPE),
                   jax.ShapeDtypeStruct((T, A_W), MXU_DTYPE), jax.ShapeDtypeStruct((T, A_W), F32)],
        scratch_shapes=[pltpu.VMEM((T, QPAD), F32), pltpu.VMEM((T, HEAD), F32), pltpu.VMEM((t, QPAD), F32),
                        pltpu.VMEM((t, HEAD), F32)],
        compiler_params=pltpu.CompilerParams(dimension_semantics=("parallel", "arbitrary")),
    )(qf, kv, kr, kv, dmix, mixed, lse, *tabs_b)


def _local_step(x, pos, target, g1, g2, gq, gkv, g3, g4,
                in_weights, attn_weights, mlp_prefetch, mlp_weights, down_grad_ready, up_grad_ready, attn_grads_ready):
    T = x.shape[0]
    TR = 256
    mm = functools.partial(_matmul, tm=2048, tn=1024, tk=2048, b_outer=True)
    mm_k = functools.partial(_matmul, tm=1024, tn=1024, tk=2048)
    mm_g = functools.partial(_matmul, tm=1024, tn=1024, tk=4096, b_outer=True)

    inv_a = ROPE_THETA ** (-jnp.arange(0, ROT_A, 2, dtype=F32) / ROT_A)
    inv_b = ROPE_THETA ** (-jnp.arange(0, ROPE_B, 2, dtype=F32) / ROPE_B)
    inv = jnp.stack([jnp.concatenate([inv_a, inv_a, jnp.zeros((HEAD - ROT_A,), F32)]),
                     jnp.concatenate([inv_b, inv_b, jnp.zeros((HEAD - ROPE_B,), F32)])])
    inv = jnp.concatenate([inv, jnp.zeros((6, HEAD), F32)], axis=0)
    tabs = _rowwise(_rope_tab_body, [pos], [inv], [(HEAD, F32)] * 6, [], tr=512, name="rope_tables")

    (h,) = _rowwise(_rms_fwd_body, [x], [g1], [(D_MODEL, MXU_DTYPE)], [], tr=TR, name="rms_in")
    w_proj = in_weights([h, tabs[0]])
    (proj,) = mm(h, w_proj, dims="nn", out_dtypes=[F32], tm=1024, tn=PROJ_TILE, name="proj_in")
    q, k, v, cqn, ckvn, krope = _rowwise(
        _postproj_body, [proj] + tabs, [gq, gkv],
        [(A_W, F32)] * 3 + [(LORA, MXU_DTYPE)] * 2 + [(HEAD, MXU_DTYPE)], [], tr=TR, name="post_proj")
    mixed, lse_a = _dil_fwd(q, k, v)

    w_uq_p, w_ukv_p, w_out = attn_weights(cqn)

    def q_epi(acc, cb, sba, sbb):
        cols = []
        for hh in range(acc.shape[1] // QPAD):
            lo = hh * QPAD
            cols += [acc[:, lo:lo + HEAD], _rope(acc[:, lo + HEAD:lo + QPAD], cb, sba, sbb, ROPE_B // 2)]
        return (jnp.concatenate(cols, axis=1) * MLA_QSCALE,)
    (qf,) = mm(cqn, w_uq_p, dims="nn", out_dtypes=[MXU_DTYPE], name="q_up", epi=q_epi, row_extras=tuple(tabs[3:]))
    (kv,) = mm(ckvn, w_ukv_p, dims="nn", out_dtypes=[MXU_DTYPE], name="kv_up")
    mixed, lse_b = _mla_fwd(qf, kv, krope, mixed)
    mlp_prefetch(mixed)

    (o,) = mm(mixed, w_out, dims="nn", out_dtypes=[F32], name="out_proj")
    x1, h2 = _rowwise(_mid_body, [x, o], [g2, g3], [(D_MODEL, F32), (D_MODEL, MXU_DTYPE)], [], tr=TR, name="mid_norm")

    w_up, w_down = mlp_weights(h2)

    def up_epi(acc):
        r = jnp.maximum(acc, 0.0)
        return r * r, r
    u, r = mm(h2, w_up, dims="nn", out_dtypes=[MXU_DTYPE, MXU_DTYPE], name="mlp_up", epi=up_epi, b_shards=N_CHIPS)
    (dn,) = mm_k(u, w_down, dims="nn", out_dtypes=[F32], name="mlp_down")
    dy, dd, loss8, dg4 = _rowwise(_loss_body, [x1, dn, target], [g4], [(D_MODEL, F32), (D_MODEL, MXU_DTYPE)],
                                  [(8, HEAD), (8, D_MODEL)], tr=TR, name="loss_head")

    def dup_epi(acc, rr):
        return (acc * (2.0 * rr.astype(F32)),)
    (dup,) = mm(dd, w_down, dims="nt", out_dtypes=[MXU_DTYPE], name="d_up", epi=dup_epi, extras=(r,))
    (gw_down,) = mm_g(u, dd, dims="tn", out_dtypes=[WIRE_DTYPE], name="gw_down")
    (dh2,) = mm_k(dup, w_up, dims="nt", out_dtypes=[F32], name="d_h2", b_shards=N_CHIPS,
                  after=down_grad_ready(gw_down))
    (gw_up,) = mm_g(h2, dup, dims="tn", out_dtypes=[WIRE_DTYPE], name="gw_up", out_shards=N_CHIPS)
    g2 = g2 + up_grad_ready(gw_up)
    dx1, do, dg3, dg2 = _rowwise(_bmid_body, [dy, dh2, x1, o], [g2, g3], [(D_MODEL, F32), (D_MODEL, MXU_DTYPE)],
                                 [(8, D_MODEL), (8, D_MODEL)], tr=TR, name="bwd_mid")
    (dmix,) = mm(do, w_out, dims="nt", out_dtypes=[F32], name="d_mixed")
    (gw_out,) = mm_g(mixed, do, dims="tn", out_dtypes=[WIRE_DTYPE], name="gw_out")

    dq_pad, dkn, dvb, dkr = _mla_bwd(qf, kv, krope, dmix, mixed, lse_b, tabs[3:])
    (dcqn,) = mm(dq_pad, w_uq_p, dims="nt", out_dtypes=[F32], name="d_cq")
    (gw_uq_p,) = mm_g(cqn, dq_pad, dims="tn", out_dtypes=[WIRE_DTYPE], name="gw_uq")
    dkv = jnp.concatenate([dkn, dvb], axis=1)
    (dckvn,) = mm(dkv, w_ukv_p, dims="nt", out_dtypes=[F32], name="d_ckv")
    (gw_ukv_p,) = mm_g(ckvn, dkv, dims="tn", out_dtypes=[WIRE_DTYPE], name="gw_ukv")
    gq = gq + attn_grads_ready(gw_out, gw_uq_p, gw_ukv_p)

    dq_a, dk_a, dv_a = _dil_bwd(q, k, v, dmix, mixed, lse_a)
    dproj, dgq, dgkv = _rowwise(
        _dproj_body, [dq_a, dk_a, dv_a, dcqn, dckvn, proj, dkr] + tabs, [gq, gkv],
        [(PROJ_COLS, MXU_DTYPE)], [(8, LORA), (8, LORA)], tr=TR, name="d_proj")
    (dh,) = mm_k(dproj, w_proj, dims="nt", out_dtypes=[F32], tk=PROJ_TILE, name="d_h")
    (gw_proj,) = mm_g(h, dproj, dims="tn", out_dtypes=[WIRE_DTYPE], tn=PROJ_TILE, name="gw_in")
    dx, dg1 = _rowwise(_bin_body, [dx1, dh, x], [g1], [(D_MODEL, F32)], [(8, D_MODEL)], tr=TR, name="bwd_in")

    small = jnp.concatenate([dg1, dg2, dgq, dgkv, dg3, dg4, loss8], axis=1)
    return dx, gw_proj, small


def _place():
    x, y, c = lax.axis_index("x"), lax.axis_index("y"), lax.axis_index("c")
    chips = [(1 - x, y), (x, 1 - y), (1 - x, 1 - y)]
    return x, y, c, chips


def _cast_place_body(me_ref, w_ref, *rest):
    o_ref = rest[-1]
    o_ref[...] = w_ref[...].astype(o_ref.dtype)


def _cast_place(me_arr, w, name, after=None):
    rows, cols = w.shape
    tr = min(rows, 256)
    after = [] if after is None else [after]
    grid_spec = pltpu.PrefetchScalarGridSpec(
        num_scalar_prefetch=1, grid=(rows // tr,),
        in_specs=[pl.BlockSpec((tr, cols), lambda i, me: (i, 0))] + [ANY] * len(after),
        out_specs=pl.BlockSpec((None, tr, cols), lambda i, me: (me[0], i, 0)))
    return _pcall(
        _cast_place_body, name=name, grid_spec=grid_spec,
        out_shape=jax.ShapeDtypeStruct((N_CHIPS, rows, cols), WIRE_DTYPE),
        compiler_params=pltpu.CompilerParams(dimension_semantics=("parallel",)),
    )(me_arr, w, *after)


def _w_in_natural_body(w_ref, o_ref):
    c = w_ref.shape[2]
    for k in range(N_CHIPS):
        o_ref[:, k * c:(k + 1) * c] = w_ref[k]
    o_ref[:, N_CHIPS * c:] = jnp.zeros((o_ref.shape[0], o_ref.shape[1] - N_CHIPS * c), o_ref.dtype)


def _w_in_natural(win_g):
    _, rows, c = win_g.shape
    tr = 256
    return _pcall(
        _w_in_natural_body, name="w_in_natural", grid=(rows // tr,),
        in_specs=[pl.BlockSpec((N_CHIPS, tr, c), lambda i: (0, i, 0))],
        out_specs=pl.BlockSpec((tr, PROJ_COLS), lambda i: (i, 0)),
        out_shape=jax.ShapeDtypeStruct((rows, PROJ_COLS), win_g.dtype),
        compiler_params=pltpu.CompilerParams(dimension_semantics=("parallel",)),
    )(win_g)


def _gw_in_shards_body(g_ref, o_ref):
    c = o_ref.shape[2]
    for k in range(N_CHIPS):
        o_ref[k] = g_ref[:, k * c:(k + 1) * c]


def _gw_in_shards(gw_proj):
    rows = gw_proj.shape[0]
    c = IN_COLS // N_CHIPS
    tr = 256
    return _pcall(
        _gw_in_shards_body, name="gw_in_shards", grid=(rows // tr,),
        in_specs=[pl.BlockSpec((tr, PROJ_COLS), lambda i: (i, 0))],
        out_specs=pl.BlockSpec((N_CHIPS, tr, c), lambda i: (0, i, 0)),
        out_shape=jax.ShapeDtypeStruct((N_CHIPS, rows, c), gw_proj.dtype),
        compiler_params=pltpu.CompilerParams(dimension_semantics=("parallel",)),
    )(gw_proj)


HBM = pl.BlockSpec(memory_space=pltpu.HBM)
SEM = pl.BlockSpec(memory_space=pltpu.SEMAPHORE)
EFFECT = pltpu.SideEffectType.DATAFLOW_SIDE_EFFECTING


def _copy_start(make, arrays, after, name, n_sems):
    n_a = len(arrays)
    after = [] if after is None else [after]

    def body(*refs):
        for send, _ in make(refs[:n_a], refs[-n_a - 3], refs[-n_a - 2]):
            send.start()
        refs[-1][...] = jnp.zeros_like(refs[-1])

    res = _pcall(
        body, name=name,
        in_specs=[HBM] * n_a + [ANY] * len(after),
        out_specs=[SEM, SEM] + [HBM] * n_a + [pl.BlockSpec(memory_space=pltpu.VMEM)],
        out_shape=[pltpu.SemaphoreType.DMA((n_sems,)), pltpu.SemaphoreType.DMA((n_sems,))]
        + [pltpu.HBM(a.shape, a.dtype) for a in arrays] + [jax.ShapeDtypeStruct((8, HEAD), F32)],
        input_output_aliases={i: 2 + i for i in range(n_a)},
        compiler_params=pltpu.CompilerParams(has_side_effects=EFFECT),
    )(*[pltpu.with_memory_space_constraint(a, pltpu.HBM) for a in arrays], *after)
    return (res[0], res[1]), list(res[2:2 + n_a]), res[-1]


def _copy_wait(make, sems, arrays, after, name):
    n_a = len(arrays)
    after = list(after) if isinstance(after, (list, tuple)) else [after]

    def body(*refs):
        for send, recv in make(refs[:n_a], refs[n_a], refs[n_a + 1]):
            send.wait_send()
            recv.wait_recv()

    return list(_pcall(
        body, name=name,
        in_specs=[HBM] * n_a + [SEM, SEM] + [ANY] * len(after), out_specs=[HBM] * n_a,
        out_shape=[pltpu.HBM(a.shape, a.dtype) for a in arrays],
        input_output_aliases={i: i for i in range(n_a)},
        compiler_params=pltpu.CompilerParams(has_side_effects=EFFECT),
    )(*arrays, sems[0], sems[1], *after))


def _ag_descs(bufs, send_sems, recv_sems):
    x, y, c, chips = _place()
    me = 2 * x + y
    out = []
    for w, buf in enumerate(bufs):
        half = buf.shape[1] // 2
        rows = pl.ds(pl.multiple_of(c * half, 16), half)
        for j, (px, py) in enumerate(chips):
            mk = lambda ref, w=w, j=j, px=px, py=py: pltpu.make_async_remote_copy(
                src_ref=ref, dst_ref=ref, send_sem=send_sems.at[w * 3 + j], recv_sem=recv_sems.at[w * 3 + j],
                device_id=(px, py, c), device_id_type=MESH)
            out.append((mk(buf.at[me, rows]), mk(buf.at[2 * px + py, rows])))
    return out


def _fw_descs(bufs, send_sems, recv_sems):
    x, y, c, chips = _place()
    out = []
    for w, buf in enumerate(bufs):
        half = buf.shape[1] // 2
        for j, (px, py) in enumerate(chips):
            def mk(which, w=w, j=j, buf=buf, half=half, px=px, py=py):
                ref = buf.at[2 * px + py, pl.ds(pl.multiple_of(which * half, 16), half)]
                return pltpu.make_async_remote_copy(
                    src_ref=ref, dst_ref=ref, send_sem=send_sems.at[w * 3 + j], recv_sem=recv_sems.at[w * 3 + j],
                    device_id=(x, y, 1 - c), device_id_type=MESH)
            out.append((mk(c), mk(1 - c)))
    return out


def _sc_descs(refs, send_sems, recv_sems):
    n_w = len(refs) // 2
    x, y, c, chips = _place()
    me = 2 * x + y
    out = []
    for w in range(n_w):
        for j, (px, py) in enumerate(chips):
            d = pltpu.make_async_remote_copy(
                src_ref=refs[w].at[2 * px + py], dst_ref=refs[n_w + w].at[me],
                send_sem=send_sems.at[w * 3 + j], recv_sem=recv_sems.at[w * 3 + j],
                device_id=(px, py, c), device_id_type=MESH)
            out.append((d, d))
    return out


def _pair_descs(src_of):
    def make(refs, send_sems, recv_sems):
        n_w = len(refs) // 2
        x, y, c, _ = _place()
        out = []
        for w in range(n_w):
            d = pltpu.make_async_remote_copy(
                src_ref=src_of(refs[w], c), dst_ref=refs[n_w + w],
                send_sem=send_sems.at[w], recv_sem=recv_sems.at[w],
                device_id=(x, y, 1 - c), device_id_type=MESH)
            out.append((d, d))
        return out
    return make


_EX_DESCS = _pair_descs(lambda g4, c: g4.at[:, 1 - c])
_SW_DESCS = _pair_descs(lambda half, c: half)


def _sm_descs(refs, send_sems, recv_sems):
    buf = refs[0]
    rows8 = buf.shape[0] // N_DEV
    x, y, c, _ = _place()
    flip = lambda v, d: 1 - v if d else v
    blk = lambda px, py, pc: buf.at[pl.ds(pl.multiple_of((4 * px + 2 * py + pc) * rows8, 8), rows8)]
    out = []
    for k in range(1, N_DEV):
        px, py, pc = flip(x, k & 4), flip(y, k & 2), flip(c, k & 1)
        mk = lambda ref, k=k, px=px, py=py, pc=pc: pltpu.make_async_remote_copy(
            src_ref=ref, dst_ref=ref, send_sem=send_sems.at[k - 1], recv_sem=recv_sems.at[k - 1],
            device_id=(px, py, pc), device_id_type=MESH)
        out.append((mk(blk(x, y, c)), mk(blk(px, py, pc))))
    return out


def _place_rows_body(i_ref, x_ref, o_ref):
    o_ref[...] = x_ref[...]


def _place_rows(i_arr, x, n_blocks, name):
    r, n = x.shape
    grid_spec = pltpu.PrefetchScalarGridSpec(
        num_scalar_prefetch=1, grid=(1,),
        in_specs=[pl.BlockSpec((r, n), lambda g, i: (0, 0))],
        out_specs=pl.BlockSpec((r, n), lambda g, i: (i[0], 0)))
    return _pcall(_place_rows_body, name=name, grid_spec=grid_spec,
                  out_shape=jax.ShapeDtypeStruct((n_blocks * r, n), x.dtype))(i_arr, x)


def _ag_forward_body(*refs, n_w):
    bufs = refs[n_w:2 * n_w]
    send_sems, recv_sems = refs[2 * n_w:]
    pairs = _fw_descs(bufs, send_sems, recv_sems)
    for fw, _ in pairs:
        fw.start()
    for fw, back in pairs:
        back.wait_recv()
        fw.wait_send()


def _ag_forward(bufs, tag):
    n_w = len(bufs)
    return list(_pcall(
        functools.partial(_ag_forward_body, n_w=n_w), name="weight_allgather_forward_" + tag,
        in_specs=[ANY] * n_w, out_specs=[ANY] * n_w,
        out_shape=[jax.ShapeDtypeStruct(b.shape, b.dtype) for b in bufs],
        input_output_aliases={w: w for w in range(n_w)},
        scratch_shapes=[pltpu.SemaphoreType.DMA((3 * n_w,))] * 2,
    )(*bufs))


def _pair_send_body(*refs, n_w):
    pairs = _EX_DESCS(refs[:2 * n_w], refs[2 * n_w], refs[2 * n_w + 1])
    for cp, _ in pairs:
        cp.start()
    for cp, _ in pairs:
        cp.wait()


def _pair_send(grads4, tag):
    n_w = len(grads4)
    return _pcall(
        functools.partial(_pair_send_body, n_w=n_w), name="grad_pair_exchange_" + tag,
        in_specs=[ANY] * n_w, out_specs=[ANY] * n_w,
        out_shape=[jax.ShapeDtypeStruct((g.shape[0],) + g.shape[2:], g.dtype) for g in grads4],
        scratch_shapes=[pltpu.SemaphoreType.DMA((n_w,))] * 2,
    )(*grads4)


def _pair_add_body(c_ref, mine_ref, theirs_ref, o_ref):
    o_ref[...] = (mine_ref[...].astype(F32) + theirs_ref[...].astype(F32)).astype(o_ref.dtype)


def _pair_add(c_arr, g4, recv, name):
    _, _, hr, cols = g4.shape
    tr = min(hr, 256)
    grid_spec = pltpu.PrefetchScalarGridSpec(
        num_scalar_prefetch=1, grid=(N_CHIPS, hr // tr),
        in_specs=[pl.BlockSpec((None, None, tr, cols), lambda s, i, c: (s, c[0], i, 0)),
                  pl.BlockSpec((None, tr, cols), lambda s, i, c: (s, i, 0))],
        out_specs=pl.BlockSpec((None, tr, cols), lambda s, i, c: (s, i, 0)))
    return _pcall(
        _pair_add_body, name=name, grid_spec=grid_spec,
        out_shape=jax.ShapeDtypeStruct(recv.shape, recv.dtype),
        compiler_params=pltpu.CompilerParams(dimension_semantics=("parallel", "parallel")),
    )(c_arr, g4, recv)


def _sum4_body(me_ref, p_ref, l0, l1, l2, l3, o_ref):
    me = me_ref[0]
    t = [jnp.where(me == j, p_ref[...], l[...]).astype(F32) for j, l in enumerate((l0, l1, l2, l3))]
    o_ref[...] = ((t[0] + t[1]) + t[2]) + t[3]


def _sum4(me_arr, part, landed, name):
    _, hr, cols = part.shape
    tr = min(hr, 256)

    def slot(j):
        return lambda i, me: (jnp.where(me[0] == j, (j + 1) % N_CHIPS, j), i, 0)

    grid_spec = pltpu.PrefetchScalarGridSpec(
        num_scalar_prefetch=1, grid=(hr // tr,),
        in_specs=[pl.BlockSpec((None, tr, cols), lambda i, me: (me[0], i, 0))]
        + [pl.BlockSpec((None, tr, cols), slot(j)) for j in range(N_CHIPS)],
        out_specs=pl.BlockSpec((tr, cols), lambda i, me: (i, 0)))
    return _pcall(
        _sum4_body, name=name, grid_spec=grid_spec,
        out_shape=jax.ShapeDtypeStruct((hr, cols), F32),
        compiler_params=pltpu.CompilerParams(dimension_semantics=("parallel",)),
    )(me_arr, part, landed, landed, landed, landed)


def _adamw(w, g, m, v):
    m = ADAM_B1 * m + (1.0 - ADAM_B1) * g
    v = ADAM_B2 * v + (1.0 - ADAM_B2) * (g * g)
    m_hat = m / (1.0 - ADAM_B1 ** ADAM_STEP)
    v_hat = v / (1.0 - ADAM_B2 ** ADAM_STEP)
    delta = -ADAM_LR * (m_hat / (jnp.sqrt(v_hat) + ADAM_EPS) + ADAM_WD * w)
    return delta, m, v


def _adamw_half_body(h_ref, w_ref, g_in_ref, m_ref, v_ref, *rest):
    g_ref, d_ref, nm_ref, nv_ref, done_ref = rest[-5:]
    done_ref[...] = jnp.zeros_like(done_ref)
    g = g_in_ref[...]
    g_ref[...] = g
    d, m, v = _adamw(w_ref[...], g, m_ref[...], v_ref[...])
    d_ref[...] = d
    nm_ref[...] = m
    nv_ref[...] = v


def _adamw_half(h_arr, w, g_half, m, v, prev, name):
    rows, cols = w.shape
    tr = min(rows // 2, 128)
    nh = (rows // 2) // tr
    at_half = pl.BlockSpec((tr, cols), lambda i, h: (h[0] * nh + i, 0))
    grid_spec = pltpu.PrefetchScalarGridSpec(
        num_scalar_prefetch=1, grid=(nh,),
        in_specs=[at_half, pl.BlockSpec((tr, cols), lambda i, h: (i, 0)), at_half, at_half] + [ANY] * len(prev),
        out_specs=[at_half] * 4 + [pl.BlockSpec((8, HEAD), lambda i, h: (0, 0))])
    return list(_pcall(
        _adamw_half_body, name=name, grid_spec=grid_spec,
        out_shape=[jax.ShapeDtypeStruct(w.shape, F32)] * 4 + [jax.ShapeDtypeStruct((8, HEAD), F32)],
        input_output_aliases={5 + k: k for k in range(len(prev))},
        compiler_params=pltpu.CompilerParams(dimension_semantics=("arbitrary",)),
    )(h_arr, w, g_half, m, v, *prev))


def _small_update_body(gath_ref, w_ref, m_ref, v_ref, g_ref, d_ref, nm_ref, nv_ref, loss_ref, *, n_gain):
    tot = gath_ref[0:1, :]
    for i in range(1, gath_ref.shape[0]):
        tot = tot + gath_ref[i:i + 1, :]
    g = tot[:, 0:n_gain]
    g_ref[...] = g
    d, m, v = _adamw(w_ref[...], g, m_ref[...], v_ref[...])
    d_ref[...] = d
    nm_ref[...] = m
    nv_ref[...] = v
    loss_ref[...] = (0.5 / D_MODEL) * jnp.sum(tot[:, n_gain:n_gain + HEAD], axis=1, keepdims=True) * jnp.ones((1, HEAD), F32)


def _small_update(gath, w, m, v):
    n_gain = w.shape[1]
    vm = pl.BlockSpec(memory_space=pltpu.VMEM)
    return _pcall(
        functools.partial(_small_update_body, n_gain=n_gain), name="gain_update",
        in_specs=[vm] * 4, out_specs=[vm] * 5,
        out_shape=[jax.ShapeDtypeStruct((1, n_gain), F32)] * 4 + [jax.ShapeDtypeStruct((1, HEAD), F32)],
    )(gath, w, m, v)


def kernel(x, positions, norm_attn_pre, norm_attn_post, w_in, q_latent_norm, kv_latent_norm, w_uq, w_ukv, w_out, norm_mlp_pre, norm_mlp_post, w_up, w_down, loss_target, m_norm_attn_pre, m_norm_attn_post, m_w_in, m_q_latent_norm, m_kv_latent_norm, m_w_uq, m_w_ukv, m_w_out, m_norm_mlp_pre, m_norm_mlp_post, m_w_up, m_w_down, v_norm_attn_pre, v_norm_attn_post, v_w_in, v_q_latent_norm, v_kv_latent_norm, v_w_uq, v_w_ukv, v_w_out, v_norm_mlp_pre, v_norm_mlp_post, v_w_up, v_w_down):
    T = x.shape[1]
    c_arr = lax.axis_index("c").astype(jnp.int32).reshape(1)
    me_arr = (2 * lax.axis_index("x") + lax.axis_index("y")).astype(jnp.int32).reshape(1)
    names = ["w_in", "w_uq", "w_ukv", "w_out", "w_up", "w_down"]

    mats = [w_in[0], w_uq[0], w_ukv[0], w_out[0], w_up[0], w_down[0]]
    me8_arr = (4 * lax.axis_index("x") + 2 * lax.axis_index("y") + lax.axis_index("c")).astype(jnp.int32).reshape(1)
    col_major = lambda g: jnp.transpose(g, (1, 0, 2)).reshape(g.shape[1], N_CHIPS * g.shape[2])
    cast = lambda a: a.astype(MXU_DTYPE)
    to_shards = lambda g: jnp.transpose(g.reshape(g.shape[0], N_CHIPS, g.shape[1] // N_CHIPS), (1, 0, 2))
    halved = lambda g: g.reshape(N_CHIPS, 2, g.shape[1] // 2, g.shape[2])
    empty = lambda a, shape=None: lax.empty(a.shape if shape is None else shape, a.dtype)

    sem_in, buf_in, going = _copy_start(_ag_descs, [_cast_place(me_arr, mats[0], "cast_w_in")], None,
                                        "weight_allgather_start_in", 3)
    placed = [_cast_place(me_arr, w, "cast_" + n, going) for w, n in zip(mats[1:], names[1:])]
    sem_att, buf_att, going = _copy_start(_ag_descs, placed[:3], going, "weight_allgather_start_attn", 9)
    sem_mlp, buf_mlp, started = _copy_start(_ag_descs, placed[3:], going, "weight_allgather_start_mlp", 6)

    def in_weights(after):
        (win_g,) = _ag_forward(_copy_wait(_ag_descs, sem_in, buf_in, after, "weight_allgather_wait_in"), "in")
        return cast(_w_in_natural(win_g))

    def attn_weights(after):
        wuq_g, wukv_g, wout_g = _ag_forward(
            _copy_wait(_ag_descs, sem_att, buf_att, after, "weight_allgather_wait_attn"), "attn")
        wuq_full = col_major(wuq_g).reshape(LORA, NH, HEAD + ROPE_B)
        w_uq_p = jnp.pad(wuq_full, ((0, 0), (0, 0), (0, QPAD - HEAD - ROPE_B))).reshape(LORA, NH * QPAD)
        w_ukv_p = col_major(wukv_g).reshape(LORA, NH, 2, HEAD).transpose(0, 2, 1, 3).reshape(LORA, 2 * A_W)
        return cast(w_uq_p), cast(w_ukv_p), cast(wout_g.reshape(2 * A_W, D_MODEL))

    going_on = {}

    def mlp_prefetch(after):
        landed = _copy_wait(_ag_descs, sem_mlp, buf_mlp, after, "weight_allgather_wait_mlp")
        going_on["fw"] = _copy_start(_fw_descs, landed, None, "weight_allgather_forward_start_mlp", 6)

    def mlp_weights(after):
        sems, bufs, _ = going_on["fw"]
        wup_g, wdown_g = _copy_wait(_fw_descs, sems, bufs, after, "weight_allgather_forward_wait_mlp")
        return cast(wup_g), cast(wdown_g.reshape(D_FF, D_MODEL))

    def exchange_start(g4s, tag):
        lands = [empty(g, (g.shape[0],) + g.shape[2:]) for g in g4s]
        return _copy_start(_EX_DESCS, g4s + lands, None, "grad_pair_exchange_start_" + tag, len(g4s))

    def exchange_finish(started_ex, after, ns, tag):
        sems, arrs, _ = started_ex
        arrs = _copy_wait(_EX_DESCS, sems, arrs, after, "grad_pair_exchange_wait_" + tag)
        n = len(ns)
        return [_pair_add(c_arr, g4, r, "pair_add_" + nm) for g4, r, nm in zip(arrs[:n], arrs[n:], ns)]

    def scatter_start(parts, after, tag):
        return _copy_start(_sc_descs, parts + [empty(p) for p in parts], after, "grad_scatter_start_" + tag,
                           3 * len(parts))

    def scatter_finish(started_sc, after, tag):
        sems, arrs, _ = started_sc
        arrs = _copy_wait(_sc_descs, sems, arrs, after, "grad_scatter_wait_" + tag)
        return arrs[:len(arrs) // 2], arrs[len(arrs) // 2:]

    def down_grad_ready(gw_down):
        going_on["x_down"] = exchange_start([halved(gw_down.reshape(N_CHIPS, D_MODEL, D_MODEL))], "down")
        return going_on["x_down"][-1]

    def up_grad_ready(gw_up):
        going_on["x_up"] = exchange_start([halved(gw_up)], "up")
        parts = exchange_finish(going_on["x_down"], going_on["x_up"][-1], names[5:], "down")
        going_on["s_down"] = scatter_start(parts, started, "down")
        return going_on["s_down"][-1][0:1, 0:1]

    def attn_grads_ready(gw_out, gw_uq_p, gw_ukv_p):
        gw_uq = to_shards(gw_uq_p.reshape(LORA, NH, QPAD)[:, :, :HEAD + ROPE_B].reshape(LORA, NH * (HEAD + ROPE_B)))
        gw_ukv = to_shards(gw_ukv_p.reshape(LORA, 2, NH, HEAD).transpose(0, 2, 1, 3).reshape(LORA, 2 * A_W))
        full4 = [halved(g) for g in (gw_uq, gw_ukv, gw_out.reshape(N_CHIPS, LORA, D_MODEL))]
        from_sib = _pair_send(full4, "attn")
        parts = [_pair_add(c_arr, g4, r, "pair_add_" + n) for g4, r, n in zip(full4, from_sib, names[1:4])]
        parts += exchange_finish(going_on["x_up"], from_sib[-1], names[4:5], "up")
        going_on["s_rest"] = scatter_start(parts, going_on["s_down"][-1], "attn_up")
        return going_on["s_rest"][-1][0:1, 0:1]

    dx, gw_proj, small = _local_step(
        x[0], positions[0].astype(F32).reshape(T, 1), loss_target[0],
        norm_attn_pre + started[0:1, 0:1], norm_attn_post, q_latent_norm, kv_latent_norm, norm_mlp_pre, norm_mlp_post,
        in_weights, attn_weights, mlp_prefetch, mlp_weights, down_grad_ready, up_grad_ready, attn_grads_ready)

    ms = [m_w_in[0], m_w_uq[0], m_w_ukv[0], m_w_out[0], m_w_up[0], m_w_down[0]]
    vs = [v_w_in[0], v_w_uq[0], v_w_ukv[0], v_w_out[0], v_w_up[0], v_w_down[0]]
    sib_arr = 1 - c_arr

    def finish(parts, landed, lo, hi, tag):
        sl = slice(lo, hi)
        halves = [_sum4(me_arr, p, l, "chip_sum_" + n) for p, l, n in zip(parts, landed, names[sl])]
        n = len(halves)
        sems, arrs, _ = _copy_start(_SW_DESCS, halves + [empty(h) for h in halves], None,
                                    "grad_pair_swap_start_" + tag, n)
        own = [_adamw_half(c_arr, w, g, m, v, [], "adamw_own_" + nm)
               for w, g, m, v, nm in zip(mats[sl], arrs[:n], ms[sl], vs[sl], names[sl])]
        arrs = _copy_wait(_SW_DESCS, sems, arrs, own[-1][4], "grad_pair_swap_wait_" + tag)
        return [_adamw_half(sib_arr, w, g, m, v, prev[:4], "adamw_sib_" + nm)
                for w, g, m, v, prev, nm in zip(mats[sl], arrs[n:], ms[sl], vs[sl], own, names[sl])]

    sem_small, (gath,), small_going = _copy_start(
        _sm_descs, [_place_rows(me8_arr, small, N_DEV, "place_small")], None, "small_allgather_start", N_DEV - 1)
    x_in = exchange_start([halved(_gw_in_shards(gw_proj))], "in")
    s_in = scatter_start(exchange_finish(x_in, small_going, names[:1], "in"), None, "in")
    parts_rest, landed_rest = scatter_finish(going_on["s_rest"], s_in[-1], "attn_up")
    parts_down, landed_down = scatter_finish(going_on["s_down"], landed_rest[0], "down")
    upd_rest = finish(parts_rest + parts_down, landed_rest + landed_down, 1, 6, "rest")
    parts_in, landed_in = scatter_finish(s_in, upd_rest[-1][0], "in")
    upd = finish(parts_in, landed_in, 0, 1, "in") + upd_rest
    grads = [u[0] for u in upd]

    (gath,) = _copy_wait(_sm_descs, sem_small, [gath], grads[0], "small_allgather_wait")
    gains = [norm_attn_pre, norm_attn_post, q_latent_norm, kv_latent_norm, norm_mlp_pre, norm_mlp_post]
    gm = [m_norm_attn_pre, m_norm_attn_post, m_q_latent_norm, m_kv_latent_norm, m_norm_mlp_pre, m_norm_mlp_post]
    gv = [v_norm_attn_pre, v_norm_attn_post, v_q_latent_norm, v_kv_latent_norm, v_norm_mlp_pre, v_norm_mlp_post]
    cat = lambda xs: jnp.concatenate(xs, axis=1)
    g_s, d_s, m_s, v_s, loss_v = _small_update(gath, cat(gains), cat(gm), cat(gv))
    widths = [a.shape[1] for a in gains]
    offs = [sum(widths[:i]) for i in range(len(widths))]
    split = lambda a: [a[:, o:o + w] for o, w in zip(offs, widths)]
    g_gain, d_gain, m_gain, v_gain = split(g_s), split(d_s), split(m_s), split(v_s)

    def ordered(gain_list, mat_list):
        gl, ml = gain_list, [a[None] for a in mat_list]
        return [gl[0], gl[1], ml[0], gl[2], gl[3], ml[1], ml[2], ml[3], gl[4], gl[5], ml[4], ml[5]]

    loss = loss_v[0, 0]
    return (loss, dx[None],
            *ordered(g_gain, grads),
            *ordered(d_gain, [u[1] for u in upd]),
            *ordered(m_gain, [u[2] for u in upd]),
            *ordered(v_gain, [u[3] for u in upd]))
```

```python
import functools

import jax
import jax.numpy as jnp
from jax import lax
from jax.experimental import pallas as pl
from jax.experimental.pallas import tpu as pltpu

F32 = jnp.float32
BF16 = jnp.bfloat16
MXU_DTYPE = jnp.bfloat16
WIRE_DTYPE = jnp.bfloat16

D_MODEL = 2048
HEAD = 128
NH = 8
A_W = NH * HEAD
LORA = 512
ROPE_B = 64
QPAD = 256
MAIN_COLS = 3 * A_W + 2 * LORA
IN_COLS = MAIN_COLS + ROPE_B
PROJ_COLS = MAIN_COLS + HEAD
PROJ_TILE = PROJ_COLS // 3
IN_SHARD = 1040
IN_TR = 208
D_FF = 4 * D_MODEL
DIL = (1, 4, 16)
ROT_A = 32
ROPE_THETA = 500000.0
EPS = 1e-6
NEG = -1e30
N_CHIPS = 4
N_DEV = 8

ADAM_LR = 0.001
ADAM_B1 = 0.9
ADAM_B2 = 0.999
ADAM_EPS = 1e-08
ADAM_WD = 0.01
ADAM_STEP = 10

MESH = pl.DeviceIdType.MESH
ANY = pl.BlockSpec(memory_space=pl.ANY)


def _pcall(body, **kw):
    return pl.pallas_call(body, **kw)


_DIMS = {
    "nn": (((1,), (0,)), ((), ())),
    "nt": (((1,), (1,)), ((), ())),
    "tn": (((0,), (0,)), ((), ())),
}


def _mm_body(*refs, dims, nk, epi, n_extra, n_after, n_out):
    a_ref, b_ref = refs[0], refs[1]
    extra = refs[2:2 + n_extra]
    outs = refs[2 + n_extra + n_after:2 + n_extra + n_after + n_out]
    part = lax.dot_general(a_ref[...], b_ref[...], _DIMS[dims], preferred_element_type=F32)

    def finish(acc):
        res = epi(acc, *[r[...] for r in extra]) if epi is not None else (acc,)
        for o_ref, o in zip(outs, res):
            o_ref[...] = o.astype(o_ref.dtype)

    if nk == 1:
        finish(part)
        return
    acc_ref = refs[-1]
    k = pl.program_id(2)

    @pl.when(k == 0)
    def _():
        acc_ref[...] = part

    @pl.when(k > 0)
    def _():
        acc_ref[...] += part

    @pl.when(k == nk - 1)
    def _():
        finish(acc_ref[...])


def _matmul(a, b, *, dims, out_dtypes, tm, tn, tk, name, epi=None, extras=(), row_extras=(), b_outer=False,
            b_shards=0, out_shards=0, after=None):
    if b_shards:
        assert dims in ("nn", "nt") and b.shape[0] == b_shards
        b2 = (b.shape[1], b_shards * b.shape[2])
    else:
        b2 = b.shape
    if dims == "nn":
        (M, K), (K2, N) = a.shape, b2
    elif dims == "nt":
        (M, K), (N, K2) = a.shape, b2
    else:
        (K, M), (K2, N) = a.shape, b2
    assert K == K2, (a.shape, b.shape, dims)
    tm, tn, tk = min(tm, M), min(tn, N), min(tk, K)
    assert M % tm == 0 and N % tn == 0 and K % tk == 0, (name, M, N, K, tm, tn, tk)
    nk = K // tk

    def at(f):
        if b_outer:
            return lambda j, i, k: f(i, j, k)
        return f

    a_spec = {"nn": pl.BlockSpec((tm, tk), at(lambda i, j, k: (i, k))),
              "nt": pl.BlockSpec((tm, tk), at(lambda i, j, k: (i, k))),
              "tn": pl.BlockSpec((tk, tm), at(lambda i, j, k: (k, i)))}[dims]
    b_spec = {"nn": pl.BlockSpec((tk, tn), at(lambda i, j, k: (k, j))),
              "nt": pl.BlockSpec((tn, tk), at(lambda i, j, k: (j, k))),
              "tn": pl.BlockSpec((tk, tn), at(lambda i, j, k: (k, j)))}[dims]
    if b_shards:
        per = b.shape[2] // (tn if dims == "nn" else tk)
        assert per >= 1 and b.shape[2] % (tn if dims == "nn" else tk) == 0
        b_spec = {"nn": pl.BlockSpec((None, tk, tn), at(lambda i, j, k: (j // per, k, j % per))),
                  "nt": pl.BlockSpec((None, tn, tk), at(lambda i, j, k: (k // per, j, k % per)))}[dims]
    o_spec = pl.BlockSpec((tm, tn), at(lambda i, j, k: (i, j)))
    o_shape = (M, N)
    if out_shards:
        assert not extras and N % out_shards == 0 and (N // out_shards) % tn == 0
        o_per = (N // out_shards) // tn
        o_spec = pl.BlockSpec((None, tm, tn), at(lambda i, j, k: (j // o_per, i, j % o_per)))
        o_shape = (out_shards, M, N // out_shards)
    r_specs = [pl.BlockSpec((tm, r.shape[1]), at(lambda i, j, k: (i, 0))) for r in row_extras]
    after = [] if after is None else [after]
    body = functools.partial(_mm_body, dims=dims, nk=nk, epi=epi, n_extra=len(extras) + len(row_extras),
                             n_after=len(after), n_out=len(out_dtypes))
    res = _pcall(
        body, name=name,
        grid=(N // tn, M // tm, nk) if b_outer else (M // tm, N // tn, nk),
        in_specs=[a_spec, b_spec] + [o_spec] * len(extras) + r_specs + [ANY] * len(after),
        out_specs=[o_spec] * len(out_dtypes),
        out_shape=[jax.ShapeDtypeStruct(o_shape, dt) for dt in out_dtypes],
        scratch_shapes=[pltpu.VMEM((tm, tn), F32)] if nk > 1 else [],
        compiler_params=pltpu.CompilerParams(
            dimension_semantics=("parallel", "parallel", "arbitrary")),
    )(a, b, *extras, *row_extras, *after)
    return list(res)


def _rowwise(body, row_ins, vec_ins, row_outs, acc_outs, *, tr, name):
    T = row_ins[0].shape[0]
    tr = min(tr, T)
    assert T % tr == 0
    in_specs = [pl.BlockSpec((tr, a.shape[1]), lambda i: (i, 0)) for a in row_ins]
    in_specs += [pl.BlockSpec(a.shape, lambda i: (0, 0)) for a in vec_ins]
    out_specs = [pl.BlockSpec((tr, w), lambda i: (i, 0)) for (w, _) in row_outs]
    out_specs += [pl.BlockSpec(s, lambda i: (0, 0)) for s in acc_outs]
    out_shape = [jax.ShapeDtypeStruct((T, w), dt) for (w, dt) in row_outs]
    out_shape += [jax.ShapeDtypeStruct(s, F32) for s in acc_outs]
    sem = "arbitrary" if acc_outs else "parallel"
    return list(_pcall(
        body, name=name, grid=(T // tr,), in_specs=in_specs, out_specs=out_specs,
        out_shape=out_shape,
        compiler_params=pltpu.CompilerParams(dimension_semantics=(sem,)),
    )(*row_ins, *vec_ins))


def _rstd(x):
    return lax.rsqrt(jnp.mean(x * x, axis=-1, keepdims=True) + EPS)


def _rms_bwd(x, rstd, dyg):
    xh = x * rstd
    return rstd * (dyg - xh * jnp.mean(dyg * xh, axis=-1, keepdims=True)), xh


def _fold8(v):
    r, w = v.shape
    return jnp.sum(v.reshape(r // 8, 8, w), axis=0)


def _acc(ref, val):
    first = pl.program_id(0) == 0

    @pl.when(first)
    def _():
        ref[...] = val

    @pl.when(jnp.logical_not(first))
    def _():
        ref[...] += val


def _rope(x, c, sa, sb, half):
    return x * c + pltpu.roll(x, HEAD - half, 1) * sa + pltpu.roll(x, half, 1) * sb


def _rope_t(dy, c, sa, sb, half):
    return dy * c - pltpu.roll(dy, HEAD - half, 1) * sa - pltpu.roll(dy, half, 1) * sb


def _rope_tab_body(pos_ref, inv_ref, ca, saa, sab, cb, sba, sbb):
    pos = pos_ref[...]
    lane = lax.broadcasted_iota(jnp.int32, (pos.shape[0], HEAD), 1)
    ang_a = pos * inv_ref[0:1, :]
    ang_b = pos * inv_ref[1:2, :]
    c, s = jnp.cos(ang_a), jnp.sin(ang_a)
    ha = ROT_A // 2
    ca[...] = jnp.where(lane < ROT_A, c, 1.0)
    saa[...] = jnp.where(lane < ha, -s, 0.0)
    sab[...] = jnp.where((lane >= ha) & (lane < ROT_A), s, 0.0)
    c, s = jnp.cos(ang_b), jnp.sin(ang_b)
    hb = ROPE_B // 2
    cb[...] = jnp.where(lane < ROPE_B, c, 1.0)
    sba[...] = jnp.where(lane < hb, -s, 0.0)
    sbb[...] = jnp.where((lane >= hb) & (lane < ROPE_B), s, 0.0)


def _rms_fwd_body(x_ref, g_ref, h_ref):
    x = x_ref[...]
    h_ref[...] = ((x * _rstd(x)) * g_ref[...]).astype(h_ref.dtype)


def _postproj_body(p_ref, ca, saa, sab, cb, sba, sbb, gq_ref, gkv_ref,
                   q_ref, k_ref, v_ref, cqn_ref, ckvn_ref, krope_ref):
    c, sa, sb = ca[...], saa[...], sab[...]
    for h in range(NH):
        lo = h * HEAD
        q_ref[:, lo:lo + HEAD] = _rope(p_ref[:, lo:lo + HEAD], c, sa, sb, ROT_A // 2).astype(q_ref.dtype)
        k_ref[:, lo:lo + HEAD] = _rope(p_ref[:, A_W + lo:A_W + lo + HEAD], c, sa, sb, ROT_A // 2).astype(k_ref.dtype)
    v_ref[...] = p_ref[:, 2 * A_W:3 * A_W].astype(v_ref.dtype)
    cq = p_ref[:, 3 * A_W:3 * A_W + LORA]
    cqn_ref[...] = ((cq * _rstd(cq)) * gq_ref[...]).astype(cqn_ref.dtype)
    ckv = p_ref[:, 3 * A_W + LORA:MAIN_COLS]
    ckvn_ref[...] = ((ckv * _rstd(ckv)) * gkv_ref[...]).astype(ckvn_ref.dtype)
    krope_ref[...] = _rope(p_ref[:, MAIN_COLS:PROJ_COLS], cb[...], sba[...], sbb[...], ROPE_B // 2).astype(krope_ref.dtype)


def _mid_body(x_ref, o_ref, g2_ref, g3_ref, x1_ref, h2_ref):
    o = o_ref[...]
    x1 = x_ref[...] + (o * _rstd(o)) * g2_ref[...]
    x1_ref[...] = x1
    h2_ref[...] = ((x1 * _rstd(x1)) * g3_ref[...]).astype(h2_ref.dtype)


def _loss_body(x1_ref, d_ref, t_ref, g4_ref, dy_ref, dd_ref, loss_ref, dg4_ref):
    d = d_ref[...]
    rstd = _rstd(d)
    y = x1_ref[...] + (d * rstd) * g4_ref[...]
    e = y - t_ref[...]
    dy = e * (1.0 / D_MODEL)
    dy_ref[...] = dy
    dd, dh = _rms_bwd(d, rstd, dy * g4_ref[...])
    dd_ref[...] = dd.astype(dd_ref.dtype)
    _acc(dg4_ref, _fold8(dy * dh))
    e8 = _fold8(e * e)
    l = e8[:, 0:HEAD]
    for j in range(1, D_MODEL // HEAD):
        l = l + e8[:, j * HEAD:(j + 1) * HEAD]
    _acc(loss_ref, l)


def _bmid_body(dy_ref, dh2_ref, x1_ref, o_ref, g2_ref, g3_ref, dx1_ref, do_ref, dg3_ref, dg2_ref):
    x1 = x1_ref[...]
    dh2 = dh2_ref[...]
    dn, x1h = _rms_bwd(x1, _rstd(x1), dh2 * g3_ref[...])
    dx1 = dy_ref[...] + dn
    dx1_ref[...] = dx1
    _acc(dg3_ref, _fold8(dh2 * x1h))
    o = o_ref[...]
    do, oh = _rms_bwd(o, _rstd(o), dx1 * g2_ref[...])
    do_ref[...] = do.astype(do_ref.dtype)
    _acc(dg2_ref, _fold8(dx1 * oh))


def _dproj_body(dq_ref, dk_ref, dv_ref, dcq_ref, dckv_ref, p_ref, dkr_ref,
                ca, saa, sab, cb, sba, sbb, gq_ref, gkv_ref,
                dp_ref, dgq_ref, dgkv_ref):
    c, sa, sb = ca[...], saa[...], sab[...]
    for h in range(NH):
        lo = h * HEAD
        dp_ref[:, lo:lo + HEAD] = _rope_t(dq_ref[:, lo:lo + HEAD], c, sa, sb, ROT_A // 2).astype(dp_ref.dtype)
        dp_ref[:, A_W + lo:A_W + lo + HEAD] = _rope_t(dk_ref[:, lo:lo + HEAD], c, sa, sb, ROT_A // 2).astype(dp_ref.dtype)
    dp_ref[:, 2 * A_W:3 * A_W] = dv_ref[...].astype(dp_ref.dtype)
    cq = p_ref[:, 3 * A_W:3 * A_W + LORA]
    dcqn = dcq_ref[...]
    dcq, cqh = _rms_bwd(cq, _rstd(cq), dcqn * gq_ref[...])
    dp_ref[:, 3 * A_W:3 * A_W + LORA] = dcq.astype(dp_ref.dtype)
    _acc(dgq_ref, _fold8(dcqn * cqh))
    ckv = p_ref[:, 3 * A_W + LORA:MAIN_COLS]
    dckvn = dckv_ref[...]
    dckv, ckvh = _rms_bwd(ckv, _rstd(ckv), dckvn * gkv_ref[...])
    dp_ref[:, 3 * A_W + LORA:MAIN_COLS] = dckv.astype(dp_ref.dtype)
    _acc(dgkv_ref, _fold8(dckvn * ckvh))
    dkr = dkr_ref[:, 0:HEAD]
    for h in range(1, NH):
        dkr = dkr + dkr_ref[:, h * HEAD:(h + 1) * HEAD]
    dp_ref[:, MAIN_COLS:PROJ_COLS] = _rope_t(dkr, cb[...], sba[...], sbb[...], ROPE_B // 2).astype(dp_ref.dtype)


def _bin_body(dx1_ref, dh_ref, x_ref, g1_ref, dx_ref, dg1_ref):
    x = x_ref[...]
    dh = dh_ref[...]
    dn, xh = _rms_bwd(x, _rstd(x), dh * g1_ref[...])
    dx_ref[...] = dx1_ref[...] + dn
    _acc(dg1_ref, _fold8(dh * xh))


def _dot_nt(a, b):
    return lax.dot_general(a, b, _DIMS["nt"], preferred_element_type=F32)


def _dot_tn(a, b):
    return lax.dot_general(a, b, _DIMS["tn"], preferred_element_type=F32)


def _dot_nn(a, b):
    return jnp.dot(a, b, preferred_element_type=F32)


DIL_SCALE = HEAD ** -0.5
DIL_CHUNK = 256


def _dil_rows(t, d):
    r = t & (d - 1)
    n = t >> (d.bit_length() - 1)
    start = r + n * (HEAD * d)
    has_prev = n > 0
    pstart = jnp.where(has_prev, start - HEAD * d, start)
    if d == 1:
        return pl.ds(pl.multiple_of(start, HEAD), HEAD), pl.ds(pl.multiple_of(pstart, HEAD), HEAD), has_prev
    return pl.ds(start, HEAD, stride=d), pl.ds(pstart, HEAD, stride=d), has_prev


def _dil_band():
    row = lax.broadcasted_iota(jnp.int32, (HEAD, 2 * HEAD), 0)
    col = lax.broadcasted_iota(jnp.int32, (HEAD, 2 * HEAD), 1)
    return (col >= row) & (col <= row + HEAD), col >= HEAD


def _dil_fwd_body(q_ref, k_ref, v_ref, a_ref, lse_ref, o1, o2, o3, l1, l2, l3, *, nt, unroll):
    band, is_cur = _dil_band()
    for d, o_sc, l_sc in zip(DIL, (o1, o2, o3), (l1, l2, l3)):

        def tile(t, carry, d=d, o_sc=o_sc, l_sc=l_sc):
            rows, prows, has_prev = _dil_rows(t, d)
            q = q_ref[rows, :].astype(MXU_DTYPE)
            kk = jnp.concatenate([k_ref[prows, :], k_ref[rows, :]], axis=0).astype(MXU_DTYPE)
            vv = jnp.concatenate([v_ref[prows, :], v_ref[rows, :]], axis=0).astype(MXU_DTYPE)
            ok = band & (is_cur | has_prev)
            s = jnp.where(ok, _dot_nt(q, kk) * DIL_SCALE, NEG)
            m = jnp.max(s, axis=1, keepdims=True)
            p = jnp.exp(s - m)
            den = jnp.sum(p, axis=1, keepdims=True)
            o_sc[rows, :] = _dot_nn((p / den).astype(MXU_DTYPE), vv)
            l_sc[rows, :] = jnp.broadcast_to(m + jnp.log(den), (HEAD, HEAD))
            return carry

        lax.fori_loop(0, nt, tile, 0, unroll=unroll)

    def merge(i, carry):
        rs = pl.ds(pl.multiple_of(i * DIL_CHUNK, DIL_CHUNK), DIL_CHUNK)
        la, lb, lc = l1[rs, :], l2[rs, :], l3[rs, :]
        m = jnp.maximum(jnp.maximum(la, lb), lc)
        wa, wb, wc = jnp.exp(la - m), jnp.exp(lb - m), jnp.exp(lc - m)
        den = wa + wb + wc
        a = (wa / den) * o1[rs, :] + (wb / den) * o2[rs, :] + (wc / den) * o3[rs, :]
        a_ref[rs, :] = a.astype(a_ref.dtype)
        lse_ref[rs, :] = m + jnp.log(den)
        return carry

    lax.fori_loop(0, q_ref.shape[0] // DIL_CHUNK, merge, 0)


def _dil_fwd(q, k, v):
    T = q.shape[0]
    spec = pl.BlockSpec((T, HEAD), lambda h: (0, h))
    return _pcall(
        functools.partial(_dil_fwd_body, nt=T // HEAD, unroll=16), name="dil_fwd",
        grid=(NH,), in_specs=[spec] * 3, out_specs=[spec] * 2,
        out_shape=[jax.ShapeDtypeStruct((T, 2 * A_W), MXU_DTYPE), jax.ShapeDtypeStruct((T, A_W), F32)],
        scratch_shapes=[pltpu.VMEM((T, HEAD), F32)] * 6,
        compiler_params=pltpu.CompilerParams(dimension_semantics=("parallel",)),
    )(q, k, v)


def _dil_bwd_body(q_ref, k_ref, v_ref, do_ref, a_ref, lse_ref, dq_ref, dk_ref, dv_ref, dl_sc, *, nt, unroll):
    band, is_cur = _dil_band()

    def prep(i, carry):
        rs = pl.ds(pl.multiple_of(i * DIL_CHUNK, DIL_CHUNK), DIL_CHUNK)
        dl = jnp.sum(do_ref[rs, :] * a_ref[rs, :].astype(F32), axis=1, keepdims=True)
        dl_sc[rs, :] = jnp.broadcast_to(dl, (DIL_CHUNK, HEAD))
        zero = jnp.zeros((DIL_CHUNK, HEAD), F32)
        dq_ref[rs, :] = zero
        dk_ref[rs, :] = zero
        dv_ref[rs, :] = zero
        return carry

    lax.fori_loop(0, q_ref.shape[0] // DIL_CHUNK, prep, 0)

    for d in DIL:

        def tile(t, carry, d=d):
            rows, prows, has_prev = _dil_rows(t, d)
            q = q_ref[rows, :].astype(MXU_DTYPE)
            kk = jnp.concatenate([k_ref[prows, :], k_ref[rows, :]], axis=0).astype(MXU_DTYPE)
            vv = jnp.concatenate([v_ref[prows, :], v_ref[rows, :]], axis=0).astype(MXU_DTYPE)
            do = do_ref[rows, :].astype(MXU_DTYPE)
            lse = lse_ref[rows, :]
            dl = dl_sc[rows, :]
            ok = band & (is_cur | has_prev)
            s = _dot_nt(q, kk) * DIL_SCALE
            p = jnp.where(ok, jnp.exp(s - jnp.concatenate([lse, lse], axis=1)), 0.0)
            ds = (p * (_dot_nt(do, vv) - jnp.concatenate([dl, dl], axis=1))).astype(MXU_DTYPE)
            dq_ref[rows, :] += _dot_nn(ds, kk) * DIL_SCALE
            dkk = _dot_tn(ds, q) * DIL_SCALE
            dvv = _dot_tn(p.astype(MXU_DTYPE), do)
            dk_ref[rows, :] += dkk[HEAD:, :]
            dv_ref[rows, :] += dvv[HEAD:, :]
            dk_ref[prows, :] += dkk[:HEAD, :]
            dv_ref[prows, :] += dvv[:HEAD, :]
            return carry

        lax.fori_loop(0, nt, tile, 0, unroll=unroll)


def _dil_bwd(q, k, v, dmix, mixed, lse):
    T = q.shape[0]
    spec = pl.BlockSpec((T, HEAD), lambda h: (0, h))
    return _pcall(
        functools.partial(_dil_bwd_body, nt=T // HEAD, unroll=8), name="dil_bwd",
        grid=(NH,), in_specs=[spec] * 6, out_specs=[spec] * 3,
        out_shape=[jax.ShapeDtypeStruct((T, A_W), F32)] * 3,
        scratch_shapes=[pltpu.VMEM((T, HEAD), F32)],
        compiler_params=pltpu.CompilerParams(dimension_semantics=("parallel",)),
    )(q, k, v, dmix, mixed, lse)


MLA_SCALE = (HEAD + ROPE_B) ** -0.5
LOG2E = 1.4426950408889634
MLA_QSCALE = MLA_SCALE * LOG2E
MLA_T = 512
MLA_HP = 4


def _tri(t):
    row = lax.broadcasted_iota(jnp.int32, (t, t), 0)
    col = lax.broadcasted_iota(jnp.int32, (t, t), 1)
    return col <= row


def _lanes(x, n):
    return jnp.tile(x, (1, n // HEAD))


def _mla_fwd_body(q_ref, kn_ref, kr_ref, v_ref, mixed_ref, o_ref, lse_ref, m_sc, l_sc, acc_sc, *, t, hp):
    del mixed_ref
    qi = pl.program_id(1)
    m_sc[...] = jnp.full(m_sc.shape, NEG, F32)
    l_sc[...] = jnp.zeros(l_sc.shape, F32)
    acc_sc[...] = jnp.zeros(acc_sc.shape, F32)

    def step(j, masked):
        ks = pl.ds(pl.multiple_of(j * t, t), t)
        kr = kr_ref[ks, :]
        logits = []
        for hh in range(hp):
            kcat = jnp.concatenate([kn_ref[ks, hh * HEAD:(hh + 1) * HEAD], kr], axis=1)
            logits.append(_dot_nt(q_ref[:, hh * QPAD:(hh + 1) * QPAD], kcat))
        for hh in range(hp):
            s = logits[hh]
            if masked:
                s = jnp.where(_tri(t), s, NEG)
            m_prev = m_sc[hh]
            m_new = jnp.maximum(m_prev, jnp.max(s, axis=1, keepdims=True))
            alpha = jnp.exp2(m_prev - m_new)
            p = jnp.exp2(s - _lanes(m_new, t))
            l_sc[hh] = alpha * l_sc[hh] + jnp.sum(p, axis=1, keepdims=True)
            acc_sc[hh] = alpha * acc_sc[hh] + _dot_nn(p.astype(MXU_DTYPE), v_ref[ks, hh * HEAD:(hh + 1) * HEAD])
            m_sc[hh] = m_new

    def off_diag(j, carry):
        step(j, False)
        return carry

    lax.fori_loop(0, qi, off_diag, 0)
    step(qi, True)
    for hh in range(hp):
        l = l_sc[hh]
        o_ref[:, hh * HEAD:(hh + 1) * HEAD] = (acc_sc[hh] / l).astype(o_ref.dtype)
        lse_ref[:, hh * HEAD:(hh + 1) * HEAD] = m_sc[hh] + jnp.log2(l)


def _mla_fwd(qf, kv, kr, mixed):
    T = qf.shape[0]
    t, hp = min(MLA_T, T), MLA_HP
    ng = NH // hp
    return _pcall(
        functools.partial(_mla_fwd_body, t=t, hp=hp), name="mla_fwd",
        grid=(ng, T // t),
        in_specs=[pl.BlockSpec((t, hp * QPAD), lambda g, i: (i, g)),
                  pl.BlockSpec((T, hp * HEAD), lambda g, i: (0, g)),
                  pl.BlockSpec((T, HEAD), lambda g, i: (0, 0)),
                  pl.BlockSpec((T, hp * HEAD), lambda g, i: (0, ng + g)), ANY],
        out_specs=[pl.BlockSpec((t, hp * HEAD), lambda g, i: (i, ng + g)),
                   pl.BlockSpec((t, hp * HEAD), lambda g, i: (i, g))],
        out_shape=[jax.ShapeDtypeStruct(mixed.shape, mixed.dtype), jax.ShapeDtypeStruct((T, A_W), F32)],
        input_output_aliases={4: 0},
        scratch_shapes=[pltpu.VMEM((hp, t, HEAD), F32)] * 3,
        compiler_params=pltpu.CompilerParams(dimension_semantics=("parallel", "parallel")),
    )(qf, kv, kr, kv, mixed)


def _mla_bwd_body(q_ref, kn_ref, kr_ref, v_ref, do_ref, o_ref, lse_ref, cb, sba, sbb,
                  dq_ref, dkn_ref, dv_ref, dkr_ref, dq_sc, dl_sc, dk_sc, dv_sc, *, t):
    ki = pl.program_id(1)
    nq = q_ref.shape[0] // t

    @pl.when(ki == 0)
    def _():
        def prep(i, carry):
            rs = pl.ds(pl.multiple_of(i * t, t), t)
            dl = jnp.sum(do_ref[rs, :] * o_ref[rs, :].astype(F32), axis=1, keepdims=True)
            dl_sc[rs, :] = jnp.broadcast_to(dl, (t, HEAD))
            dq_sc[rs, :] = jnp.zeros((t, QPAD), F32)
            return carry
        lax.fori_loop(0, nq, prep, 0)

    kcat = jnp.concatenate([kn_ref[...], kr_ref[...]], axis=1)
    v = v_ref[...]
    dk_sc[...] = jnp.zeros(dk_sc.shape, F32)
    dv_sc[...] = jnp.zeros(dv_sc.shape, F32)

    def step(i, masked):
        qs = pl.ds(pl.multiple_of(i * t, t), t)
        q = q_ref[qs, :]
        do = do_ref[qs, :].astype(MXU_DTYPE)
        s = _dot_nt(q, kcat)
        dp = _dot_nt(do, v)
        p = jnp.exp2(s - _lanes(lse_ref[qs, :], t))
        if masked:
            p = jnp.where(_tri(t), p, 0.0)
        ds = (p * (dp - _lanes(dl_sc[qs, :], t))).astype(MXU_DTYPE)
        dv_sc[...] += _dot_tn(p.astype(MXU_DTYPE), do)
        dk_sc[...] += _dot_tn(ds, q)
        dq_sc[qs, :] += _dot_nn(ds, kcat) * MLA_SCALE

    step(ki, True)

    def off_diag(i, carry):
        step(i, False)
        return carry

    lax.fori_loop(ki + 1, nq, off_diag, 0)
    dk = dk_sc[...] * (1.0 / LOG2E)
    dkn_ref[...] = dk[:, 0:HEAD].astype(dkn_ref.dtype)
    dkr_ref[...] = dk[:, HEAD:QPAD]
    dv_ref[...] = dv_sc[...].astype(dv_ref.dtype)

    @pl.when(ki == nq - 1)
    def _():
        def emit(i, carry):
            rs = pl.ds(pl.multiple_of(i * t, t), t)
            dq_ref[rs, 0:HEAD] = dq_sc[rs, 0:HEAD].astype(dq_ref.dtype)
            dq_ref[rs, HEAD:QPAD] = _rope_t(dq_sc[rs, HEAD:QPAD], cb[rs, :], sba[rs, :], sbb[rs, :],
                                            ROPE_B // 2).astype(dq_ref.dtype)
            return carry
        lax.fori_loop(0, nq, emit, 0)


def _mla_bwd(qf, kv, kr, dmix, mixed, lse, tabs_b):
    T = qf.shape[0]
    t = min(MLA_T, T)
    head = lambda h, j: (0, h)
    b_half = lambda h, j: (0, NH + h)
    kblk = pl.BlockSpec((t, HEAD), lambda h, j: (j, h))
    return _pcall(
        functools.partial(_mla_bwd_body, t=t), name="mla_bwd",
        grid=(NH, T // t),
        in_specs=[pl.BlockSpec((T, QPAD), head), kblk,
                  pl.BlockSpec((t, HEAD), lambda h, j: (j, 0)),
                  pl.BlockSpec((t, HEAD), lambda h, j: (j, NH + h)),
                  pl.BlockSpec((T, HEAD), b_half), pl.BlockSpec((T, HEAD), b_half),
                  pl.BlockSpec((T, HEAD), head)] + [pl.BlockSpec((T, HEAD), lambda h, j: (0, 0))] * 3,
        out_specs=[pl.BlockSpec((T, QPAD), head), kblk, kblk, kblk],
        out_shape=[jax.ShapeDtypeStruct((T, NH * QPAD), MXU_DTYPE), jax.ShapeDtypeStruct((T, A_W), MXU_DTYPE),
                   jax.ShapeDtypeStruct((T, A_W), MXU_DTYPE), jax.ShapeDtypeStruct((T, A_W), F32)],
        scratch_shapes=[pltpu.VMEM((T, QPAD), F32), pltpu.VMEM((T, HEAD), F32), pltpu.VMEM((t, QPAD), F32),
                        pltpu.VMEM((t, HEAD), F32)],
        compiler_params=pltpu.CompilerParams(dimension_semantics=("parallel", "arbitrary")),
    )(qf, kv, kr, kv, dmix, mixed, lse, *tabs_b)


def _local_step(x, pos, target, g1, g2, gq, gkv, g3, g4,
                in_weights, attn_weights, mlp_prefetch, mlp_weights, down_grad_ready, up_grad_ready, attn_grads_ready):
    T = x.shape[0]
    TR = 256
    mm = functools.partial(_matmul, tm=2048, tn=1024, tk=2048, b_outer=True)
    mm_k = functools.partial(_matmul, tm=1024, tn=1024, tk=2048)
    mm_g = functools.partial(_matmul, tm=1024, tn=1024, tk=4096, b_outer=True)

    inv_a = ROPE_THETA ** (-jnp.arange(0, ROT_A, 2, dtype=F32) / ROT_A)
    inv_b = ROPE_THETA ** (-jnp.arange(0, ROPE_B, 2, dtype=F32) / ROPE_B)
    inv = jnp.stack([jnp.concatenate([inv_a, inv_a, jnp.zeros((HEAD - ROT_A,), F32)]),
                     jnp.concatenate([inv_b, inv_b, jnp.zeros((HEAD - ROPE_B,), F32)])])
    inv = jnp.concatenate([inv, jnp.zeros((6, HEAD), F32)], axis=0)
    tabs = _rowwise(_rope_tab_body, [pos], [inv], [(HEAD, F32)] * 6, [], tr=512, name="rope_tables")

    (h,) = _rowwise(_rms_fwd_body, [x], [g1], [(D_MODEL, MXU_DTYPE)], [], tr=TR, name="rms_in")
    w_proj = in_weights([h, tabs[0]])
    (proj,) = mm(h, w_proj, dims="nt", out_dtypes=[F32], tm=1024, tn=PROJ_TILE, name="proj_in")
    q, k, v, cqn, ckvn, krope = _rowwise(
        _postproj_body, [proj] + tabs, [gq, gkv],
        [(A_W, F32)] * 3 + [(LORA, MXU_DTYPE)] * 2 + [(HEAD, MXU_DTYPE)], [], tr=TR, name="post_proj")
    mixed, lse_a = _dil_fwd(q, k, v)

    w_uq_p, w_ukv_p, w_out = attn_weights(cqn)

    def q_epi(acc, cb, sba, sbb):
        cols = []
        for hh in range(acc.shape[1] // QPAD):
            lo = hh * QPAD
            cols += [acc[:, lo:lo + HEAD], _rope(acc[:, lo + HEAD:lo + QPAD], cb, sba, sbb, ROPE_B // 2)]
        return (jnp.concatenate(cols, axis=1) * MLA_QSCALE,)
    (qf,) = mm(cqn, w_uq_p, dims="nn", out_dtypes=[MXU_DTYPE], name="q_up", epi=q_epi, row_extras=tuple(tabs[3:]))
    (kv,) = mm(ckvn, w_ukv_p, dims="nn", out_dtypes=[MXU_DTYPE], name="kv_up")
    mixed, lse_b = _mla_fwd(qf, kv, krope, mixed)
    mlp_prefetch(mixed)

    (o,) = mm(mixed, w_out, dims="nn", out_dtypes=[F32], name="out_proj")
    x1, h2 = _rowwise(_mid_body, [x, o], [g2, g3], [(D_MODEL, F32), (D_MODEL, MXU_DTYPE)], [], tr=TR, name="mid_norm")

    w_up, w_down = mlp_weights(h2)

    def up_epi(acc):
        r = jnp.maximum(acc, 0.0)
        return r * r, r
    u, r = mm(h2, w_up, dims="nn", out_dtypes=[MXU_DTYPE, MXU_DTYPE], name="mlp_up", epi=up_epi, b_shards=N_CHIPS)
    (dn,) = mm_k(u, w_down, dims="nn", out_dtypes=[F32], name="mlp_down")
    dy, dd, loss8, dg4 = _rowwise(_loss_body, [x1, dn, target], [g4], [(D_MODEL, F32), (D_MODEL, MXU_DTYPE)],
                                  [(8, HEAD), (8, D_MODEL)], tr=TR, name="loss_head")

    def dup_epi(acc, rr):
        return (acc * (2.0 * rr.astype(F32)),)
    (dup,) = mm(dd, w_down, dims="nt", out_dtypes=[MXU_DTYPE], name="d_up", epi=dup_epi, extras=(r,))
    (gw_down,) = mm_g(u, dd, dims="tn", out_dtypes=[WIRE_DTYPE], name="gw_down")
    (dh2,) = mm_k(dup, w_up, dims="nt", out_dtypes=[F32], name="d_h2", b_shards=N_CHIPS,
                  after=down_grad_ready(gw_down))
    (gw_up,) = mm_g(h2, dup, dims="tn", out_dtypes=[WIRE_DTYPE], name="gw_up", out_shards=N_CHIPS)
    g2 = g2 + up_grad_ready(gw_up)
    dx1, do, dg3, dg2 = _rowwise(_bmid_body, [dy, dh2, x1, o], [g2, g3], [(D_MODEL, F32), (D_MODEL, MXU_DTYPE)],
                                 [(8, D_MODEL), (8, D_MODEL)], tr=TR, name="bwd_mid")
    (dmix,) = mm(do, w_out, dims="nt", out_dtypes=[F32], name="d_mixed")
    (gw_out,) = mm_g(mixed, do, dims="tn", out_dtypes=[WIRE_DTYPE], name="gw_out")

    dq_pad, dkn, dvb, dkr = _mla_bwd(qf, kv, krope, dmix, mixed, lse_b, tabs[3:])
    (dcqn,) = mm(dq_pad, w_uq_p, dims="nt", out_dtypes=[F32], name="d_cq")
    (gw_uq_p,) = mm_g(cqn, dq_pad, dims="tn", out_dtypes=[WIRE_DTYPE], name="gw_uq")
    dkv = jnp.concatenate([dkn, dvb], axis=1)
    (dckvn,) = mm(dkv, w_ukv_p, dims="nt", out_dtypes=[F32], name="d_ckv")
    (gw_ukv_p,) = mm_g(ckvn, dkv, dims="tn", out_dtypes=[WIRE_DTYPE], name="gw_ukv")
    gq = gq + attn_grads_ready(gw_out, gw_uq_p, gw_ukv_p)

    dq_a, dk_a, dv_a = _dil_bwd(q, k, v, dmix, mixed, lse_a)
    dproj, dgq, dgkv = _rowwise(
        _dproj_body, [dq_a, dk_a, dv_a, dcqn, dckvn, proj, dkr] + tabs, [gq, gkv],
        [(PROJ_COLS, MXU_DTYPE)], [(8, LORA), (8, LORA)], tr=TR, name="d_proj")
    (dh,) = mm_k(dproj, w_proj, dims="nn", out_dtypes=[F32], tk=PROJ_TILE, name="d_h")
    (gw_proj,) = mm_g(dproj, h, dims="tn", out_dtypes=[WIRE_DTYPE], tm=PROJ_TILE, name="gw_in")
    dx, dg1 = _rowwise(_bin_body, [dx1, dh, x], [g1], [(D_MODEL, F32)], [(8, D_MODEL)], tr=TR, name="bwd_in")

    small = jnp.concatenate([dg1, dg2, dgq, dgkv, dg3, dg4, loss8], axis=1)
    return dx, gw_proj, small


def _place():
    x, y, c = lax.axis_index("x"), lax.axis_index("y"), lax.axis_index("c")
    chips = [(1 - x, y), (x, 1 - y), (1 - x, 1 - y)]
    return x, y, c, chips


def _cast_place_body(me_ref, w_ref, *rest):
    o_ref = rest[-1]
    o_ref[...] = w_ref[...].astype(o_ref.dtype)


def _cast_place(me_arr, w, name, after=None):
    rows, cols = w.shape
    tr = min(rows, 256)
    after = [] if after is None else [after]
    grid_spec = pltpu.PrefetchScalarGridSpec(
        num_scalar_prefetch=1, grid=(rows // tr,),
        in_specs=[pl.BlockSpec((tr, cols), lambda i, me: (i, 0))] + [ANY] * len(after),
        out_specs=pl.BlockSpec((None, tr, cols), lambda i, me: (me[0], i, 0)))
    return _pcall(
        _cast_place_body, name=name, grid_spec=grid_spec,
        out_shape=jax.ShapeDtypeStruct((N_CHIPS, rows, cols), WIRE_DTYPE),
        compiler_params=pltpu.CompilerParams(dimension_semantics=("parallel",)),
    )(me_arr, w, *after)


def _cast_place_t_body(me_ref, w_ref, o_ref, *, n):
    i = pl.program_id(0)

    @pl.when(i < n)
    def _():
        o_ref[...] = w_ref[...].astype(o_ref.dtype)

    @pl.when(i == n)
    def _():
        o_ref[...] = jnp.zeros_like(o_ref)


def _cast_place_t(me_arr, w_t, name):
    rows, cols = w_t.shape
    n = rows // IN_TR
    grid_spec = pltpu.PrefetchScalarGridSpec(
        num_scalar_prefetch=1, grid=(n + 1,),
        in_specs=[pl.BlockSpec((IN_TR, cols), lambda i, me: (jnp.minimum(i, n - 1), 0))],
        out_specs=pl.BlockSpec((IN_TR, cols), lambda i, me: (jnp.where(i < n, me[0] * n + i, N_CHIPS * n), 0)))
    return _pcall(
        functools.partial(_cast_place_t_body, n=n), name=name, grid_spec=grid_spec,
        out_shape=jax.ShapeDtypeStruct((PROJ_COLS, cols), WIRE_DTYPE),
        compiler_params=pltpu.CompilerParams(dimension_semantics=("arbitrary",)),
    )(me_arr, w_t)


HBM = pl.BlockSpec(memory_space=pltpu.HBM)
SEM = pl.BlockSpec(memory_space=pltpu.SEMAPHORE)
EFFECT = pltpu.SideEffectType.DATAFLOW_SIDE_EFFECTING


def _copy_start(make, arrays, after, name, n_sems):
    n_a = len(arrays)
    after = [] if after is None else [after]

    def body(*refs):
        for send, _ in make(refs[:n_a], refs[-n_a - 3], refs[-n_a - 2]):
            send.start()
        refs[-1][...] = jnp.zeros_like(refs[-1])

    res = _pcall(
        body, name=name,
        in_specs=[HBM] * n_a + [ANY] * len(after),
        out_specs=[SEM, SEM] + [HBM] * n_a + [pl.BlockSpec(memory_space=pltpu.VMEM)],
        out_shape=[pltpu.SemaphoreType.DMA((n_sems,)), pltpu.SemaphoreType.DMA((n_sems,))]
        + [pltpu.HBM(a.shape, a.dtype) for a in arrays] + [jax.ShapeDtypeStruct((8, HEAD), F32)],
        input_output_aliases={i: 2 + i for i in range(n_a)},
        compiler_params=pltpu.CompilerParams(has_side_effects=EFFECT),
    )(*[pltpu.with_memory_space_constraint(a, pltpu.HBM) for a in arrays], *after)
    return (res[0], res[1]), list(res[2:2 + n_a]), res[-1]


def _copy_wait(make, sems, arrays, after, name):
    n_a = len(arrays)
    after = list(after) if isinstance(after, (list, tuple)) else [after]

    def body(*refs):
        for send, recv in make(refs[:n_a], refs[n_a], refs[n_a + 1]):
            send.wait_send()
            recv.wait_recv()

    return list(_pcall(
        body, name=name,
        in_specs=[HBM] * n_a + [SEM, SEM] + [ANY] * len(after), out_specs=[HBM] * n_a,
        out_shape=[pltpu.HBM(a.shape, a.dtype) for a in arrays],
        input_output_aliases={i: i for i in range(n_a)},
        compiler_params=pltpu.CompilerParams(has_side_effects=EFFECT),
    )(*arrays, sems[0], sems[1], *after))


def _slot(buf, chip, half):
    if buf.ndim == 2:
        hc = buf.shape[1] // 2
        return buf.at[pl.ds(pl.multiple_of(chip * IN_SHARD, 16), IN_SHARD), pl.ds(pl.multiple_of(half * hc, HEAD), hc)]
    hr = buf.shape[1] // 2
    return buf.at[chip, pl.ds(pl.multiple_of(half * hr, 16), hr)]


def _ag_descs(bufs, send_sems, recv_sems):
    x, y, c, chips = _place()
    me = 2 * x + y
    out = []
    for w, buf in enumerate(bufs):
        for j, (px, py) in enumerate(chips):
            mk = lambda ref, w=w, j=j, px=px, py=py: pltpu.make_async_remote_copy(
                src_ref=ref, dst_ref=ref, send_sem=send_sems.at[w * 3 + j], recv_sem=recv_sems.at[w * 3 + j],
                device_id=(px, py, c), device_id_type=MESH)
            out.append((mk(_slot(buf, me, c)), mk(_slot(buf, 2 * px + py, c))))
    return out


def _fw_descs(bufs, send_sems, recv_sems):
    x, y, c, chips = _place()
    out = []
    for w, buf in enumerate(bufs):
        for j, (px, py) in enumerate(chips):
            def mk(which, w=w, j=j, buf=buf, px=px, py=py):
                ref = _slot(buf, 2 * px + py, which)
                return pltpu.make_async_remote_copy(
                    src_ref=ref, dst_ref=ref, send_sem=send_sems.at[w * 3 + j], recv_sem=recv_sems.at[w * 3 + j],
                    device_id=(x, y, 1 - c), device_id_type=MESH)
            out.append((mk(c), mk(1 - c)))
    return out


def _sc_descs(refs, send_sems, recv_sems):
    n_w = len(refs) // 2
    x, y, c, chips = _place()
    me = 2 * x + y
    out = []
    for w in range(n_w):
        for j, (px, py) in enumerate(chips):
            d = pltpu.make_async_remote_copy(
                src_ref=refs[w].at[2 * px + py], dst_ref=refs[n_w + w].at[me],
                send_sem=send_sems.at[w * 3 + j], recv_sem=recv_sems.at[w * 3 + j],
                device_id=(px, py, c), device_id_type=MESH)
            out.append((d, d))
    return out


def _pair_descs(src_of):
    def make(refs, send_sems, recv_sems):
        n_w = len(refs) // 2
        x, y, c, _ = _place()
        out = []
        for w in range(n_w):
            d = pltpu.make_async_remote_copy(
                src_ref=src_of(refs[w], c), dst_ref=refs[n_w + w],
                send_sem=send_sems.at[w], recv_sem=recv_sems.at[w],
                device_id=(x, y, 1 - c), device_id_type=MESH)
            out.append((d, d))
        return out
    return make


_EX_DESCS = _pair_descs(lambda g4, c: g4.at[:, 1 - c])
_SW_DESCS = _pair_descs(lambda half, c: half)
_EXT_DESCS = _pair_descs(lambda g, c: g.at[pl.ds(0, IN_COLS),
                                           pl.ds(pl.multiple_of((1 - c) * (D_MODEL // 2), HEAD), D_MODEL // 2)])


def _sm_descs(refs, send_sems, recv_sems):
    buf = refs[0]
    rows8 = buf.shape[0] // N_DEV
    x, y, c, _ = _place()
    flip = lambda v, d: 1 - v if d else v
    blk = lambda px, py, pc: buf.at[pl.ds(pl.multiple_of((4 * px + 2 * py + pc) * rows8, 8), rows8)]
    out = []
    for k in range(1, N_DEV):
        px, py, pc = flip(x, k & 4), flip(y, k & 2), flip(c, k & 1)
        mk = lambda ref, k=k, px=px, py=py, pc=pc: pltpu.make_async_remote_copy(
            src_ref=ref, dst_ref=ref, send_sem=send_sems.at[k - 1], recv_sem=recv_sems.at[k - 1],
            device_id=(px, py, pc), device_id_type=MESH)
        out.append((mk(blk(x, y, c)), mk(blk(px, py, pc))))
    return out


def _place_rows_body(i_ref, x_ref, o_ref):
    o_ref[...] = x_ref[...]


def _place_rows(i_arr, x, n_blocks, name):
    r, n = x.shape
    grid_spec = pltpu.PrefetchScalarGridSpec(
        num_scalar_prefetch=1, grid=(1,),
        in_specs=[pl.BlockSpec((r, n), lambda g, i: (0, 0))],
        out_specs=pl.BlockSpec((r, n), lambda g, i: (i[0], 0)))
    return _pcall(_place_rows_body, name=name, grid_spec=grid_spec,
                  out_shape=jax.ShapeDtypeStruct((n_blocks * r, n), x.dtype))(i_arr, x)


def _ag_forward_body(*refs, n_w):
    bufs = refs[n_w:2 * n_w]
    send_sems, recv_sems = refs[2 * n_w:]
    pairs = _fw_descs(bufs, send_sems, recv_sems)
    for fw, _ in pairs:
        fw.start()
    for fw, back in pairs:
        back.wait_recv()
        fw.wait_send()


def _ag_forward(bufs, tag):
    n_w = len(bufs)
    return list(_pcall(
        functools.partial(_ag_forward_body, n_w=n_w), name="weight_allgather_forward_" + tag,
        in_specs=[ANY] * n_w, out_specs=[ANY] * n_w,
        out_shape=[jax.ShapeDtypeStruct(b.shape, b.dtype) for b in bufs],
        input_output_aliases={w: w for w in range(n_w)},
        scratch_shapes=[pltpu.SemaphoreType.DMA((3 * n_w,))] * 2,
    )(*bufs))


def _pair_send_body(*refs, n_w):
    pairs = _EX_DESCS(refs[:2 * n_w], refs[2 * n_w], refs[2 * n_w + 1])
    for cp, _ in pairs:
        cp.start()
    for cp, _ in pairs:
        cp.wait()


def _pair_send(grads4, tag):
    n_w = len(grads4)
    return _pcall(
        functools.partial(_pair_send_body, n_w=n_w), name="grad_pair_exchange_" + tag,
        in_specs=[ANY] * n_w, out_specs=[ANY] * n_w,
        out_shape=[jax.ShapeDtypeStruct((g.shape[0],) + g.shape[2:], g.dtype) for g in grads4],
        scratch_shapes=[pltpu.SemaphoreType.DMA((n_w,))] * 2,
    )(*grads4)


def _pair_add_body(c_ref, mine_ref, theirs_ref, o_ref):
    o_ref[...] = (mine_ref[...].astype(F32) + theirs_ref[...].astype(F32)).astype(o_ref.dtype)


def _pair_add(c_arr, g4, recv, name):
    _, _, hr, cols = g4.shape
    tr = min(hr, 256)
    grid_spec = pltpu.PrefetchScalarGridSpec(
        num_scalar_prefetch=1, grid=(N_CHIPS, hr // tr),
        in_specs=[pl.BlockSpec((None, None, tr, cols), lambda s, i, c: (s, c[0], i, 0)),
                  pl.BlockSpec((None, tr, cols), lambda s, i, c: (s, i, 0))],
        out_specs=pl.BlockSpec((None, tr, cols), lambda s, i, c: (s, i, 0)))
    return _pcall(
        _pair_add_body, name=name, grid_spec=grid_spec,
        out_shape=jax.ShapeDtypeStruct(recv.shape, recv.dtype),
        compiler_params=pltpu.CompilerParams(dimension_semantics=("parallel", "parallel")),
    )(c_arr, g4, recv)


def _pair_add_t(c_arr, g, recv, name):
    rows, hc = recv.shape
    grid_spec = pltpu.PrefetchScalarGridSpec(
        num_scalar_prefetch=1, grid=(rows // IN_TR,),
        in_specs=[pl.BlockSpec((IN_TR, hc), lambda i, c: (i, c[0])), pl.BlockSpec((IN_TR, hc), lambda i, c: (i, 0))],
        out_specs=pl.BlockSpec((IN_TR, hc), lambda i, c: (i, 0)))
    return _pcall(
        _pair_add_body, name=name, grid_spec=grid_spec,
        out_shape=jax.ShapeDtypeStruct(recv.shape, recv.dtype),
        compiler_params=pltpu.CompilerParams(dimension_semantics=("parallel",)),
    )(c_arr, g, recv)


def _sum4_body(me_ref, p_ref, l0, l1, l2, l3, o_ref):
    me = me_ref[0]
    t = [jnp.where(me == j, p_ref[...], l[...]).astype(F32) for j, l in enumerate((l0, l1, l2, l3))]
    o_ref[...] = ((t[0] + t[1]) + t[2]) + t[3]


def _sum4(me_arr, part, landed, name):
    _, hr, cols = part.shape
    tr = IN_TR if hr == IN_SHARD else min(hr, 256)

    def slot(j):
        return lambda i, me: (jnp.where(me[0] == j, (j + 1) % N_CHIPS, j), i, 0)

    grid_spec = pltpu.PrefetchScalarGridSpec(
        num_scalar_prefetch=1, grid=(hr // tr,),
        in_specs=[pl.BlockSpec((None, tr, cols), lambda i, me: (me[0], i, 0))]
        + [pl.BlockSpec((None, tr, cols), slot(j)) for j in range(N_CHIPS)],
        out_specs=pl.BlockSpec((tr, cols), lambda i, me: (i, 0)))
    return _pcall(
        _sum4_body, name=name, grid_spec=grid_spec,
        out_shape=jax.ShapeDtypeStruct((hr, cols), F32),
        compiler_params=pltpu.CompilerParams(dimension_semantics=("parallel",)),
    )(me_arr, part, landed, landed, landed, landed)


def _adamw(w, g, m, v):
    m = ADAM_B1 * m + (1.0 - ADAM_B1) * g
    v = ADAM_B2 * v + (1.0 - ADAM_B2) * (g * g)
    m_hat = m / (1.0 - ADAM_B1 ** ADAM_STEP)
    v_hat = v / (1.0 - ADAM_B2 ** ADAM_STEP)
    delta = -ADAM_LR * (m_hat / (jnp.sqrt(v_hat) + ADAM_EPS) + ADAM_WD * w)
    return delta, m, v


def _adamw_half_body(h_ref, w_ref, g_in_ref, m_ref, v_ref, *rest):
    g_ref, d_ref, nm_ref, nv_ref, done_ref = rest[-5:]
    done_ref[...] = jnp.zeros_like(done_ref)
    g = g_in_ref[...]
    g_ref[...] = g
    d, m, v = _adamw(w_ref[...], g, m_ref[...], v_ref[...])
    d_ref[...] = d
    nm_ref[...] = m
    nv_ref[...] = v


def _adamw_half(h_arr, w, g_half, m, v, prev, name):
    rows, cols = w.shape
    if g_half.shape[0] == rows:
        tr, nh = IN_TR, rows // IN_TR
        at_half = pl.BlockSpec((tr, cols // 2), lambda i, h: (i, h[0]))
        g_spec = pl.BlockSpec((tr, cols // 2), lambda i, h: (i, 0))
    else:
        tr = min(rows // 2, 128)
        nh = (rows // 2) // tr
        at_half = pl.BlockSpec((tr, cols), lambda i, h: (h[0] * nh + i, 0))
        g_spec = pl.BlockSpec((tr, cols), lambda i, h: (i, 0))
    grid_spec = pltpu.PrefetchScalarGridSpec(
        num_scalar_prefetch=1, grid=(nh,),
        in_specs=[at_half, g_spec, at_half, at_half] + [ANY] * len(prev),
        out_specs=[at_half] * 4 + [pl.BlockSpec((8, HEAD), lambda i, h: (0, 0))])
    return list(_pcall(
        _adamw_half_body, name=name, grid_spec=grid_spec,
        out_shape=[jax.ShapeDtypeStruct(w.shape, F32)] * 4 + [jax.ShapeDtypeStruct((8, HEAD), F32)],
        input_output_aliases={5 + k: k for k in range(len(prev))},
        compiler_params=pltpu.CompilerParams(dimension_semantics=("arbitrary",)),
    )(h_arr, w, g_half, m, v, *prev))


def _small_update_body(gath_ref, w_ref, m_ref, v_ref, g_ref, d_ref, nm_ref, nv_ref, loss_ref, *, n_gain):
    tot = gath_ref[0:1, :]
    for i in range(1, gath_ref.shape[0]):
        tot = tot + gath_ref[i:i + 1, :]
    g = tot[:, 0:n_gain]
    g_ref[...] = g
    d, m, v = _adamw(w_ref[...], g, m_ref[...], v_ref[...])
    d_ref[...] = d
    nm_ref[...] = m
    nv_ref[...] = v
    loss_ref[...] = (0.5 / D_MODEL) * jnp.sum(tot[:, n_gain:n_gain + HEAD], axis=1, keepdims=True) * jnp.ones((1, HEAD), F32)


def _small_update(gath, w, m, v):
    n_gain = w.shape[1]
    vm = pl.BlockSpec(memory_space=pltpu.VMEM)
    return _pcall(
        functools.partial(_small_update_body, n_gain=n_gain), name="gain_update",
        in_specs=[vm] * 4, out_specs=[vm] * 5,
        out_shape=[jax.ShapeDtypeStruct((1, n_gain), F32)] * 4 + [jax.ShapeDtypeStruct((1, HEAD), F32)],
    )(gath, w, m, v)


def kernel(x, positions, norm_attn_pre, norm_attn_post, w_in, q_latent_norm, kv_latent_norm, w_uq, w_ukv, w_out, norm_mlp_pre, norm_mlp_post, w_up, w_down, loss_target, m_norm_attn_pre, m_norm_attn_post, m_w_in, m_q_latent_norm, m_kv_latent_norm, m_w_uq, m_w_ukv, m_w_out, m_norm_mlp_pre, m_norm_mlp_post, m_w_up, m_w_down, v_norm_attn_pre, v_norm_attn_post, v_w_in, v_q_latent_norm, v_kv_latent_norm, v_w_uq, v_w_ukv, v_w_out, v_norm_mlp_pre, v_norm_mlp_post, v_w_up, v_w_down):
    T = x.shape[1]
    c_arr = lax.axis_index("c").astype(jnp.int32).reshape(1)
    me_arr = (2 * lax.axis_index("x") + lax.axis_index("y")).astype(jnp.int32).reshape(1)
    names = ["w_in", "w_uq", "w_ukv", "w_out", "w_up", "w_down"]

    transposed = lambda a: jnp.swapaxes(a, 1, 2)
    mats = [transposed(w_in)[0], w_uq[0], w_ukv[0], w_out[0], w_up[0], w_down[0]]
    me8_arr = (4 * lax.axis_index("x") + 2 * lax.axis_index("y") + lax.axis_index("c")).astype(jnp.int32).reshape(1)
    col_major = lambda g: jnp.transpose(g, (1, 0, 2)).reshape(g.shape[1], N_CHIPS * g.shape[2])
    cast = lambda a: a.astype(MXU_DTYPE)
    to_shards = lambda g: jnp.transpose(g.reshape(g.shape[0], N_CHIPS, g.shape[1] // N_CHIPS), (1, 0, 2))
    halved = lambda g: g.reshape(N_CHIPS, 2, g.shape[1] // 2, g.shape[2])
    empty = lambda a, shape=None: lax.empty(a.shape if shape is None else shape, a.dtype)

    sem_in, buf_in, going = _copy_start(_ag_descs, [_cast_place_t(me_arr, mats[0], "cast_w_in")], None,
                                        "weight_allgather_start_in", 3)
    placed = [_cast_place(me_arr, w, "cast_" + n, going) for w, n in zip(mats[1:], names[1:])]
    sem_att, buf_att, going = _copy_start(_ag_descs, placed[:3], going, "weight_allgather_start_attn", 9)
    sem_mlp, buf_mlp, started = _copy_start(_ag_descs, placed[3:], going, "weight_allgather_start_mlp", 6)

    def in_weights(after):
        (win_g,) = _ag_forward(_copy_wait(_ag_descs, sem_in, buf_in, after, "weight_allgather_wait_in"), "in")
        return cast(win_g)

    def attn_weights(after):
        wuq_g, wukv_g, wout_g = _ag_forward(
            _copy_wait(_ag_descs, sem_att, buf_att, after, "weight_allgather_wait_attn"), "attn")
        wuq_full = col_major(wuq_g).reshape(LORA, NH, HEAD + ROPE_B)
        w_uq_p = jnp.pad(wuq_full, ((0, 0), (0, 0), (0, QPAD - HEAD - ROPE_B))).reshape(LORA, NH * QPAD)
        w_ukv_p = col_major(wukv_g).reshape(LORA, NH, 2, HEAD).transpose(0, 2, 1, 3).reshape(LORA, 2 * A_W)
        return cast(w_uq_p), cast(w_ukv_p), cast(wout_g.reshape(2 * A_W, D_MODEL))

    going_on = {}

    def mlp_prefetch(after):
        landed = _copy_wait(_ag_descs, sem_mlp, buf_mlp, after, "weight_allgather_wait_mlp")
        going_on["fw"] = _copy_start(_fw_descs, landed, None, "weight_allgather_forward_start_mlp", 6)

    def mlp_weights(after):
        sems, bufs, _ = going_on["fw"]
        wup_g, wdown_g = _copy_wait(_fw_descs, sems, bufs, after, "weight_allgather_forward_wait_mlp")
        return cast(wup_g), cast(wdown_g.reshape(D_FF, D_MODEL))

    def exchange_start(g4s, tag):
        lands = [empty(g, (g.shape[0],) + g.shape[2:]) for g in g4s]
        return _copy_start(_EX_DESCS, g4s + lands, None, "grad_pair_exchange_start_" + tag, len(g4s))

    def exchange_finish(started_ex, after, ns, tag):
        sems, arrs, _ = started_ex
        arrs = _copy_wait(_EX_DESCS, sems, arrs, after, "grad_pair_exchange_wait_" + tag)
        n = len(ns)
        return [_pair_add(c_arr, g4, r, "pair_add_" + nm) for g4, r, nm in zip(arrs[:n], arrs[n:], ns)]

    def scatter_start(parts, after, tag):
        return _copy_start(_sc_descs, parts + [empty(p) for p in parts], after, "grad_scatter_start_" + tag,
                           3 * len(parts))

    def scatter_finish(started_sc, after, tag):
        sems, arrs, _ = started_sc
        arrs = _copy_wait(_sc_descs, sems, arrs, after, "grad_scatter_wait_" + tag)
        return arrs[:len(arrs) // 2], arrs[len(arrs) // 2:]

    def down_grad_ready(gw_down):
        going_on["x_down"] = exchange_start([halved(gw_down.reshape(N_CHIPS, D_MODEL, D_MODEL))], "down")
        return going_on["x_down"][-1]

    def up_grad_ready(gw_up):
        going_on["x_up"] = exchange_start([halved(gw_up)], "up")
        parts = exchange_finish(going_on["x_down"], going_on["x_up"][-1], names[5:], "down")
        going_on["s_down"] = scatter_start(parts, started, "down")
        return going_on["s_down"][-1][0:1, 0:1]

    def attn_grads_ready(gw_out, gw_uq_p, gw_ukv_p):
        gw_uq = to_shards(gw_uq_p.reshape(LORA, NH, QPAD)[:, :, :HEAD + ROPE_B].reshape(LORA, NH * (HEAD + ROPE_B)))
        gw_ukv = to_shards(gw_ukv_p.reshape(LORA, 2, NH, HEAD).transpose(0, 2, 1, 3).reshape(LORA, 2 * A_W))
        full4 = [halved(g) for g in (gw_uq, gw_ukv, gw_out.reshape(N_CHIPS, LORA, D_MODEL))]
        from_sib = _pair_send(full4, "attn")
        parts = [_pair_add(c_arr, g4, r, "pair_add_" + n) for g4, r, n in zip(full4, from_sib, names[1:4])]
        parts += exchange_finish(going_on["x_up"], from_sib[-1], names[4:5], "up")
        going_on["s_rest"] = scatter_start(parts, going_on["s_down"][-1], "attn_up")
        return going_on["s_rest"][-1][0:1, 0:1]

    dx, gw_proj, small = _local_step(
        x[0], positions[0].astype(F32).reshape(T, 1), loss_target[0],
        norm_attn_pre + started[0:1, 0:1], norm_attn_post, q_latent_norm, kv_latent_norm, norm_mlp_pre, norm_mlp_post,
        in_weights, attn_weights, mlp_prefetch, mlp_weights, down_grad_ready, up_grad_ready, attn_grads_ready)

    ms = [transposed(m_w_in)[0], m_w_uq[0], m_w_ukv[0], m_w_out[0], m_w_up[0], m_w_down[0]]
    vs = [transposed(v_w_in)[0], v_w_uq[0], v_w_ukv[0], v_w_out[0], v_w_up[0], v_w_down[0]]
    sib_arr = 1 - c_arr

    def finish(parts, landed, lo, hi, tag):
        sl = slice(lo, hi)
        halves = [_sum4(me_arr, p, l, "chip_sum_" + n) for p, l, n in zip(parts, landed, names[sl])]
        n = len(halves)
        sems, arrs, _ = _copy_start(_SW_DESCS, halves + [empty(h) for h in halves], None,
                                    "grad_pair_swap_start_" + tag, n)
        own = [_adamw_half(c_arr, w, g, m, v, [], "adamw_own_" + nm)
               for w, g, m, v, nm in zip(mats[sl], arrs[:n], ms[sl], vs[sl], names[sl])]
        arrs = _copy_wait(_SW_DESCS, sems, arrs, own[-1][4], "grad_pair_swap_wait_" + tag)
        return [_adamw_half(sib_arr, w, g, m, v, prev[:4], "adamw_sib_" + nm)
                for w, g, m, v, prev, nm in zip(mats[sl], arrs[n:], ms[sl], vs[sl], own, names[sl])]

    sem_small, (gath,), small_going = _copy_start(
        _sm_descs, [_place_rows(me8_arr, small, N_DEV, "place_small")], None, "small_allgather_start", N_DEV - 1)
    sems, arrs, _ = _copy_start(_EXT_DESCS, [gw_proj, lax.empty((IN_COLS, D_MODEL // 2), WIRE_DTYPE)], None,
                                "grad_pair_exchange_start_in", 1)
    gw_proj, from_sib = _copy_wait(_EXT_DESCS, sems, arrs, small_going, "grad_pair_exchange_wait_in")
    part_in = _pair_add_t(c_arr, gw_proj, from_sib, "pair_add_w_in").reshape(N_CHIPS, IN_SHARD, D_MODEL // 2)
    s_in = scatter_start([part_in], None, "in")
    parts_rest, landed_rest = scatter_finish(going_on["s_rest"], s_in[-1], "attn_up")
    parts_down, landed_down = scatter_finish(going_on["s_down"], landed_rest[0], "down")
    upd_rest = finish(parts_rest + parts_down, landed_rest + landed_down, 1, 6, "rest")
    parts_in, landed_in = scatter_finish(s_in, upd_rest[-1][0], "in")
    upd = finish(parts_in, landed_in, 0, 1, "in") + upd_rest
    grads = [u[0] for u in upd]

    (gath,) = _copy_wait(_sm_descs, sem_small, [gath], grads[0], "small_allgather_wait")
    gains = [norm_attn_pre, norm_attn_post, q_latent_norm, kv_latent_norm, norm_mlp_pre, norm_mlp_post]
    gm = [m_norm_attn_pre, m_norm_attn_post, m_q_latent_norm, m_kv_latent_norm, m_norm_mlp_pre, m_norm_mlp_post]
    gv = [v_norm_attn_pre, v_norm_attn_post, v_q_latent_norm, v_kv_latent_norm, v_norm_mlp_pre, v_norm_mlp_post]
    cat = lambda xs: jnp.concatenate(xs, axis=1)
    g_s, d_s, m_s, v_s, loss_v = _small_update(gath, cat(gains), cat(gm), cat(gv))
    widths = [a.shape[1] for a in gains]
    offs = [sum(widths[:i]) for i in range(len(widths))]
    split = lambda a: [a[:, o:o + w] for o, w in zip(offs, widths)]
    g_gain, d_gain, m_gain, v_gain = split(g_s), split(d_s), split(m_s), split(v_s)

    def ordered(gain_list, mat_list):
        gl, ml = gain_list, [transposed(mat_list[0][None])] + [a[None] for a in mat_list[1:]]
        return [gl[0], gl[1], ml[0], gl[2], gl[3], ml[1], ml[2], ml[3], gl[4], gl[5], ml[4], ml[5]]

    loss = loss_v[0, 0]
    return (loss, dx[None],
            *ordered(g_gain, grads),
            *ordered(d_gain, [u[1] for u in upd]),
            *ordered(m_gain, [u[2] for u in upd]),
            *ordered(v_gain, [u[3] for u in upd]))
```

```python
import functools

import jax
import jax.numpy as jnp
from jax import lax
from jax.experimental import pallas as pl
from jax.experimental.pallas import tpu as pltpu

F32 = jnp.float32
BF16 = jnp.bfloat16
MXU_DTYPE = jnp.bfloat16
WIRE_DTYPE = jnp.bfloat16

D_MODEL = 2048
HEAD = 128
NH = 8
A_W = NH * HEAD
LORA = 512
ROPE_B = 64
QPAD = 256
MAIN_COLS = 3 * A_W + 2 * LORA
IN_COLS = MAIN_COLS + ROPE_B
PROJ_COLS = MAIN_COLS + HEAD
PROJ_TILE = PROJ_COLS // 3
IN_SHARD = 1040
IN_TR = 208
D_FF = 4 * D_MODEL
DIL = (1, 4, 16)
ROT_A = 32
ROPE_THETA = 500000.0
EPS = 1e-6
NEG = -1e30
N_CHIPS = 4
N_DEV = 8

ADAM_LR = 0.001
ADAM_B1 = 0.9
ADAM_B2 = 0.999
ADAM_EPS = 1e-08
ADAM_WD = 0.01
ADAM_STEP = 10

MESH = pl.DeviceIdType.MESH
ANY = pl.BlockSpec(memory_space=pl.ANY)


def _pcall(body, **kw):
    return pl.pallas_call(body, **kw)


_DIMS = {
    "nn": (((1,), (0,)), ((), ())),
    "nt": (((1,), (1,)), ((), ())),
    "tn": (((0,), (0,)), ((), ())),
}


def _mm_body(*refs, dims, nk, epi, n_extra, n_after, n_out):
    a_ref, b_ref = refs[0], refs[1]
    extra = refs[2:2 + n_extra]
    outs = refs[2 + n_extra + n_after:2 + n_extra + n_after + n_out]
    part = lax.dot_general(a_ref[...], b_ref[...], _DIMS[dims], preferred_element_type=F32)

    def finish(acc):
        res = epi(acc, *[r[...] for r in extra]) if epi is not None else (acc,)
        for o_ref, o in zip(outs, res):
            o_ref[...] = o.astype(o_ref.dtype)

    if nk == 1:
        finish(part)
        return
    acc_ref = refs[-1]
    k = pl.program_id(2)

    @pl.when(k == 0)
    def _():
        acc_ref[...] = part

    @pl.when(k > 0)
    def _():
        acc_ref[...] += part

    @pl.when(k == nk - 1)
    def _():
        finish(acc_ref[...])


def _matmul(a, b, *, dims, out_dtypes, tm, tn, tk, name, epi=None, extras=(), row_extras=(), b_outer=False,
            b_shards=0, out_shards=0, after=None):
    if b_shards:
        assert dims in ("nn", "nt") and b.shape[0] == b_shards
        b2 = (b.shape[1], b_shards * b.shape[2])
    else:
        b2 = b.shape
    if dims == "nn":
        (M, K), (K2, N) = a.shape, b2
    elif dims == "nt":
        (M, K), (N, K2) = a.shape, b2
    else:
        (K, M), (K2, N) = a.shape, b2
    assert K == K2, (a.shape, b.shape, dims)
    tm, tn, tk = min(tm, M), min(tn, N), min(tk, K)
    assert M % tm == 0 and N % tn == 0 and K % tk == 0, (name, M, N, K, tm, tn, tk)
    nk = K // tk

    def at(f):
        if b_outer:
            return lambda j, i, k: f(i, j, k)
        return f

    a_spec = {"nn": pl.BlockSpec((tm, tk), at(lambda i, j, k: (i, k))),
              "nt": pl.BlockSpec((tm, tk), at(lambda i, j, k: (i, k))),
              "tn": pl.BlockSpec((tk, tm), at(lambda i, j, k: (k, i)))}[dims]
    b_spec = {"nn": pl.BlockSpec((tk, tn), at(lambda i, j, k: (k, j))),
              "nt": pl.BlockSpec((tn, tk), at(lambda i, j, k: (j, k))),
              "tn": pl.BlockSpec((tk, tn), at(lambda i, j, k: (k, j)))}[dims]
    if b_shards:
        per = b.shape[2] // (tn if dims == "nn" else tk)
        assert per >= 1 and b.shape[2] % (tn if dims == "nn" else tk) == 0
        b_spec = {"nn": pl.BlockSpec((None, tk, tn), at(lambda i, j, k: (j // per, k, j % per))),
                  "nt": pl.BlockSpec((None, tn, tk), at(lambda i, j, k: (k // per, j, k % per)))}[dims]
    o_spec = pl.BlockSpec((tm, tn), at(lambda i, j, k: (i, j)))
    o_shape = (M, N)
    if out_shards:
        assert not extras and N % out_shards == 0 and (N // out_shards) % tn == 0
        o_per = (N // out_shards) // tn
        o_spec = pl.BlockSpec((None, tm, tn), at(lambda i, j, k: (j // o_per, i, j % o_per)))
        o_shape = (out_shards, M, N // out_shards)
    r_specs = [pl.BlockSpec((tm, r.shape[1]), at(lambda i, j, k: (i, 0))) for r in row_extras]
    after = [] if after is None else [after]
    body = functools.partial(_mm_body, dims=dims, nk=nk, epi=epi, n_extra=len(extras) + len(row_extras),
                             n_after=len(after), n_out=len(out_dtypes))
    res = _pcall(
        body, name=name,
        grid=(N // tn, M // tm, nk) if b_outer else (M // tm, N // tn, nk),
        in_specs=[a_spec, b_spec] + [o_spec] * len(extras) + r_specs + [ANY] * len(after),
        out_specs=[o_spec] * len(out_dtypes),
        out_shape=[jax.ShapeDtypeStruct(o_shape, dt) for dt in out_dtypes],
        scratch_shapes=[pltpu.VMEM((tm, tn), F32)] if nk > 1 else [],
        compiler_params=pltpu.CompilerParams(
            dimension_semantics=("parallel", "parallel", "arbitrary")),
    )(a, b, *extras, *row_extras, *after)
    return list(res)


def _rowwise(body, row_ins, vec_ins, row_outs, acc_outs, *, tr, name):
    T = row_ins[0].shape[0]
    tr = min(tr, T)
    assert T % tr == 0
    in_specs = [pl.BlockSpec((tr, a.shape[1]), lambda i: (i, 0)) for a in row_ins]
    in_specs += [pl.BlockSpec(a.shape, lambda i: (0, 0)) for a in vec_ins]
    out_specs = [pl.BlockSpec((tr, w), lambda i: (i, 0)) for (w, _) in row_outs]
    out_specs += [pl.BlockSpec(s, lambda i: (0, 0)) for s in acc_outs]
    out_shape = [jax.ShapeDtypeStruct((T, w), dt) for (w, dt) in row_outs]
    out_shape += [jax.ShapeDtypeStruct(s, F32) for s in acc_outs]
    sem = "arbitrary" if acc_outs else "parallel"
    return list(_pcall(
        body, name=name, grid=(T // tr,), in_specs=in_specs, out_specs=out_specs,
        out_shape=out_shape,
        compiler_params=pltpu.CompilerParams(dimension_semantics=(sem,)),
    )(*row_ins, *vec_ins))


def _rstd(x):
    return lax.rsqrt(jnp.mean(x * x, axis=-1, keepdims=True) + EPS)


def _rms_bwd(x, rstd, dyg):
    xh = x * rstd
    return rstd * (dyg - xh * jnp.mean(dyg * xh, axis=-1, keepdims=True)), xh


def _fold8(v):
    r, w = v.shape
    return jnp.sum(v.reshape(r // 8, 8, w), axis=0)


def _acc(ref, val):
    first = pl.program_id(0) == 0

    @pl.when(first)
    def _():
        ref[...] = val

    @pl.when(jnp.logical_not(first))
    def _():
        ref[...] += val


def _rope(x, c, sa, sb, half):
    return x * c + pltpu.roll(x, HEAD - half, 1) * sa + pltpu.roll(x, half, 1) * sb


def _rope_t(dy, c, sa, sb, half):
    return dy * c - pltpu.roll(dy, HEAD - half, 1) * sa - pltpu.roll(dy, half, 1) * sb


def _rope_tab_body(pos_ref, inv_ref, ca, saa, sab, cb, sba, sbb):
    pos = pos_ref[...]
    lane = lax.broadcasted_iota(jnp.int32, (pos.shape[0], HEAD), 1)
    ang_a = pos * inv_ref[0:1, :]
    ang_b = pos * inv_ref[1:2, :]
    c, s = jnp.cos(ang_a), jnp.sin(ang_a)
    ha = ROT_A // 2
    ca[...] = jnp.where(lane < ROT_A, c, 1.0)
    saa[...] = jnp.where(lane < ha, -s, 0.0)
    sab[...] = jnp.where((lane >= ha) & (lane < ROT_A), s, 0.0)
    c, s = jnp.cos(ang_b), jnp.sin(ang_b)
    hb = ROPE_B // 2
    cb[...] = jnp.where(lane < ROPE_B, c, 1.0)
    sba[...] = jnp.where(lane < hb, -s, 0.0)
    sbb[...] = jnp.where((lane >= hb) & (lane < ROPE_B), s, 0.0)


def _rms_fwd_body(x_ref, g_ref, h_ref):
    x = x_ref[...]
    h_ref[...] = ((x * _rstd(x)) * g_ref[...]).astype(h_ref.dtype)


def _postproj_body(p_ref, ca, saa, sab, cb, sba, sbb, gq_ref, gkv_ref,
                   q_ref, k_ref, v_ref, cqn_ref, ckvn_ref, krope_ref):
    c, sa, sb = ca[...], saa[...], sab[...]
    for h in range(NH):
        lo = h * HEAD
        q_ref[:, lo:lo + HEAD] = _rope(p_ref[:, lo:lo + HEAD], c, sa, sb, ROT_A // 2).astype(q_ref.dtype)
        k_ref[:, lo:lo + HEAD] = _rope(p_ref[:, A_W + lo:A_W + lo + HEAD], c, sa, sb, ROT_A // 2).astype(k_ref.dtype)
    v_ref[...] = p_ref[:, 2 * A_W:3 * A_W].astype(v_ref.dtype)
    cq = p_ref[:, 3 * A_W:3 * A_W + LORA]
    cqn_ref[...] = ((cq * _rstd(cq)) * gq_ref[...]).astype(cqn_ref.dtype)
    ckv = p_ref[:, 3 * A_W + LORA:MAIN_COLS]
    ckvn_ref[...] = ((ckv * _rstd(ckv)) * gkv_ref[...]).astype(ckvn_ref.dtype)
    krope_ref[...] = _rope(p_ref[:, MAIN_COLS:PROJ_COLS], cb[...], sba[...], sbb[...], ROPE_B // 2).astype(krope_ref.dtype)


def _mid_body(x_ref, o_ref, g2_ref, g3_ref, x1_ref, h2_ref):
    o = o_ref[...]
    x1 = x_ref[...] + (o * _rstd(o)) * g2_ref[...]
    x1_ref[...] = x1
    h2_ref[...] = ((x1 * _rstd(x1)) * g3_ref[...]).astype(h2_ref.dtype)


def _loss_body(x1_ref, d_ref, t_ref, g4_ref, dy_ref, dd_ref, loss_ref, dg4_ref):
    d = d_ref[...]
    rstd = _rstd(d)
    y = x1_ref[...] + (d * rstd) * g4_ref[...]
    e = y - t_ref[...]
    dy = e * (1.0 / D_MODEL)
    dy_ref[...] = dy
    dd, dh = _rms_bwd(d, rstd, dy * g4_ref[...])
    dd_ref[...] = dd.astype(dd_ref.dtype)
    _acc(dg4_ref, _fold8(dy * dh))
    e8 = _fold8(e * e)
    l = e8[:, 0:HEAD]
    for j in range(1, D_MODEL // HEAD):
        l = l + e8[:, j * HEAD:(j + 1) * HEAD]
    _acc(loss_ref, l)


def _bmid_body(dy_ref, dh2_ref, x1_ref, o_ref, g2_ref, g3_ref, dx1_ref, do_ref, dg3_ref, dg2_ref):
    x1 = x1_ref[...]
    dh2 = dh2_ref[...]
    dn, x1h = _rms_bwd(x1, _rstd(x1), dh2 * g3_ref[...])
    dx1 = dy_ref[...] + dn
    dx1_ref[...] = dx1
    _acc(dg3_ref, _fold8(dh2 * x1h))
    o = o_ref[...]
    do, oh = _rms_bwd(o, _rstd(o), dx1 * g2_ref[...])
    do_ref[...] = do.astype(do_ref.dtype)
    _acc(dg2_ref, _fold8(dx1 * oh))


def _dproj_body(dq_ref, dk_ref, dv_ref, dcq_ref, dckv_ref, p_ref, dkr_ref,
                ca, saa, sab, cb, sba, sbb, gq_ref, gkv_ref,
                dp_ref, dgq_ref, dgkv_ref):
    c, sa, sb = ca[...], saa[...], sab[...]
    for h in range(NH):
        lo = h * HEAD
        dp_ref[:, lo:lo + HEAD] = _rope_t(dq_ref[:, lo:lo + HEAD], c, sa, sb, ROT_A // 2).astype(dp_ref.dtype)
        dp_ref[:, A_W + lo:A_W + lo + HEAD] = _rope_t(dk_ref[:, lo:lo + HEAD], c, sa, sb, ROT_A // 2).astype(dp_ref.dtype)
    dp_ref[:, 2 * A_W:3 * A_W] = dv_ref[...].astype(dp_ref.dtype)
    cq = p_ref[:, 3 * A_W:3 * A_W + LORA]
    dcqn = dcq_ref[...]
    dcq, cqh = _rms_bwd(cq, _rstd(cq), dcqn * gq_ref[...])
    dp_ref[:, 3 * A_W:3 * A_W + LORA] = dcq.astype(dp_ref.dtype)
    _acc(dgq_ref, _fold8(dcqn * cqh))
    ckv = p_ref[:, 3 * A_W + LORA:MAIN_COLS]
    dckvn = dckv_ref[...]
    dckv, ckvh = _rms_bwd(ckv, _rstd(ckv), dckvn * gkv_ref[...])
    dp_ref[:, 3 * A_W + LORA:MAIN_COLS] = dckv.astype(dp_ref.dtype)
    _acc(dgkv_ref, _fold8(dckvn * ckvh))
    dkr = dkr_ref[:, 0:HEAD]
    for h in range(1, NH):
        dkr = dkr + dkr_ref[:, h * HEAD:(h + 1) * HEAD]
    dp_ref[:, MAIN_COLS:PROJ_COLS] = _rope_t(dkr, cb[...], sba[...], sbb[...], ROPE_B // 2).astype(dp_ref.dtype)


def _bin_body(dx1_ref, dh_ref, x_ref, g1_ref, dx_ref, dg1_ref):
    x = x_ref[...]
    dh = dh_ref[...]
    dn, xh = _rms_bwd(x, _rstd(x), dh * g1_ref[...])
    dx_ref[...] = dx1_ref[...] + dn
    _acc(dg1_ref, _fold8(dh * xh))


def _dot_nt(a, b):
    return lax.dot_general(a, b, _DIMS["nt"], preferred_element_type=F32)


def _dot_tn(a, b):
    return lax.dot_general(a, b, _DIMS["tn"], preferred_element_type=F32)


def _dot_nn(a, b):
    return jnp.dot(a, b, preferred_element_type=F32)


DIL_SCALE = HEAD ** -0.5
DIL_CHUNK = 256


def _dil_rows(t, d):
    r = t & (d - 1)
    n = t >> (d.bit_length() - 1)
    start = r + n * (HEAD * d)
    has_prev = n > 0
    pstart = jnp.where(has_prev, start - HEAD * d, start)
    if d == 1:
        return pl.ds(pl.multiple_of(start, HEAD), HEAD), pl.ds(pl.multiple_of(pstart, HEAD), HEAD), has_prev
    return pl.ds(start, HEAD, stride=d), pl.ds(pstart, HEAD, stride=d), has_prev


def _dil_band():
    row = lax.broadcasted_iota(jnp.int32, (HEAD, 2 * HEAD), 0)
    col = lax.broadcasted_iota(jnp.int32, (HEAD, 2 * HEAD), 1)
    return (col >= row) & (col <= row + HEAD), col >= HEAD


def _dil_fwd_body(q_ref, k_ref, v_ref, a_ref, lse_ref, o1, o2, o3, l1, l2, l3, *, nt, unroll):
    band, is_cur = _dil_band()
    for d, o_sc, l_sc in zip(DIL, (o1, o2, o3), (l1, l2, l3)):

        def tile(t, carry, d=d, o_sc=o_sc, l_sc=l_sc):
            rows, prows, has_prev = _dil_rows(t, d)
            q = q_ref[rows, :].astype(MXU_DTYPE)
            kk = jnp.concatenate([k_ref[prows, :], k_ref[rows, :]], axis=0).astype(MXU_DTYPE)
            vv = jnp.concatenate([v_ref[prows, :], v_ref[rows, :]], axis=0).astype(MXU_DTYPE)
            ok = band & (is_cur | has_prev)
            s = jnp.where(ok, _dot_nt(q, kk) * DIL_SCALE, NEG)
            m = jnp.max(s, axis=1, keepdims=True)
            p = jnp.exp(s - m)
            den = jnp.sum(p, axis=1, keepdims=True)
            o_sc[rows, :] = _dot_nn((p / den).astype(MXU_DTYPE), vv)
            l_sc[rows, :] = jnp.broadcast_to(m + jnp.log(den), (HEAD, HEAD))
            return carry

        lax.fori_loop(0, nt, tile, 0, unroll=unroll)

    def merge(i, carry):
        rs = pl.ds(pl.multiple_of(i * DIL_CHUNK, DIL_CHUNK), DIL_CHUNK)
        la, lb, lc = l1[rs, :], l2[rs, :], l3[rs, :]
        m = jnp.maximum(jnp.maximum(la, lb), lc)
        wa, wb, wc = jnp.exp(la - m), jnp.exp(lb - m), jnp.exp(lc - m)
        den = wa + wb + wc
        a = (wa / den) * o1[rs, :] + (wb / den) * o2[rs, :] + (wc / den) * o3[rs, :]
        a_ref[rs, :] = a.astype(a_ref.dtype)
        lse_ref[rs, :] = m + jnp.log(den)
        return carry

    lax.fori_loop(0, q_ref.shape[0] // DIL_CHUNK, merge, 0)


def _dil_fwd(q, k, v):
    T = q.shape[0]
    spec = pl.BlockSpec((T, HEAD), lambda h: (0, h))
    return _pcall(
        functools.partial(_dil_fwd_body, nt=T // HEAD, unroll=16), name="dil_fwd",
        grid=(NH,), in_specs=[spec] * 3, out_specs=[spec] * 2,
        out_shape=[jax.ShapeDtypeStruct((T, 2 * A_W), MXU_DTYPE), jax.ShapeDtypeStruct((T, A_W), F32)],
        scratch_shapes=[pltpu.VMEM((T, HEAD), F32)] * 6,
        compiler_params=pltpu.CompilerParams(dimension_semantics=("parallel",)),
    )(q, k, v)


def _dil_bwd_body(q_ref, k_ref, v_ref, do_ref, a_ref, lse_ref, dq_ref, dk_ref, dv_ref, dl_sc, *, nt, unroll):
    band, is_cur = _dil_band()

    def prep(i, carry):
        rs = pl.ds(pl.multiple_of(i * DIL_CHUNK, DIL_CHUNK), DIL_CHUNK)
        dl = jnp.sum(do_ref[rs, :] * a_ref[rs, :].astype(F32), axis=1, keepdims=True)
        dl_sc[rs, :] = jnp.broadcast_to(dl, (DIL_CHUNK, HEAD))
        zero = jnp.zeros((DIL_CHUNK, HEAD), F32)
        dq_ref[rs, :] = zero
        dk_ref[rs, :] = zero
        dv_ref[rs, :] = zero
        return carry

    lax.fori_loop(0, q_ref.shape[0] // DIL_CHUNK, prep, 0)

    for d in DIL:

        def tile(t, carry, d=d):
            rows, prows, has_prev = _dil_rows(t, d)
            q = q_ref[rows, :].astype(MXU_DTYPE)
            kk = jnp.concatenate([k_ref[prows, :], k_ref[rows, :]], axis=0).astype(MXU_DTYPE)
            vv = jnp.concatenate([v_ref[prows, :], v_ref[rows, :]], axis=0).astype(MXU_DTYPE)
            do = do_ref[rows, :].astype(MXU_DTYPE)
            lse = lse_ref[rows, :]
            dl = dl_sc[rows, :]
            ok = band & (is_cur | has_prev)
            s = _dot_nt(q, kk) * DIL_SCALE
            p = jnp.where(ok, jnp.exp(s - jnp.concatenate([lse, lse], axis=1)), 0.0)
            ds = (p * (_dot_nt(do, vv) - jnp.concatenate([dl, dl], axis=1))).astype(MXU_DTYPE)
            dq_ref[rows, :] += _dot_nn(ds, kk) * DIL_SCALE
            dkk = _dot_tn(ds, q) * DIL_SCALE
            dvv = _dot_tn(p.astype(MXU_DTYPE), do)
            dk_ref[rows, :] += dkk[HEAD:, :]
            dv_ref[rows, :] += dvv[HEAD:, :]
            dk_ref[prows, :] += dkk[:HEAD, :]
            dv_ref[prows, :] += dvv[:HEAD, :]
            return carry

        lax.fori_loop(0, nt, tile, 0, unroll=unroll)


def _dil_bwd(q, k, v, dmix, mixed, lse):
    T = q.shape[0]
    spec = pl.BlockSpec((T, HEAD), lambda h: (0, h))
    return _pcall(
        functools.partial(_dil_bwd_body, nt=T // HEAD, unroll=8), name="dil_bwd",
        grid=(NH,), in_specs=[spec] * 6, out_specs=[spec] * 3,
        out_shape=[jax.ShapeDtypeStruct((T, A_W), F32)] * 3,
        scratch_shapes=[pltpu.VMEM((T, HEAD), F32)],
        compiler_params=pltpu.CompilerParams(dimension_semantics=("parallel",)),
    )(q, k, v, dmix, mixed, lse)


MLA_SCALE = (HEAD + ROPE_B) ** -0.5
LOG2E = 1.4426950408889634
MLA_QSCALE = MLA_SCALE * LOG2E
MLA_T = 512
MLA_HP = 4


def _tri(t):
    row = lax.broadcasted_iota(jnp.int32, (t, t), 0)
    col = lax.broadcasted_iota(jnp.int32, (t, t), 1)
    return col <= row


def _lanes(x, n):
    return jnp.tile(x, (1, n // HEAD))


def _mla_fwd_body(q_ref, kn_ref, kr_ref, v_ref, mixed_ref, o_ref, lse_ref, m_sc, l_sc, acc_sc, *, t, hp):
    del mixed_ref
    qi = pl.program_id(1)
    m_sc[...] = jnp.full(m_sc.shape, NEG, F32)
    l_sc[...] = jnp.zeros(l_sc.shape, F32)
    acc_sc[...] = jnp.zeros(acc_sc.shape, F32)

    def step(j, masked):
        ks = pl.ds(pl.multiple_of(j * t, t), t)
        kr = kr_ref[ks, :]
        logits = []
        for hh in range(hp):
            kcat = jnp.concatenate([kn_ref[ks, hh * HEAD:(hh + 1) * HEAD], kr], axis=1)
            logits.append(_dot_nt(q_ref[:, hh * QPAD:(hh + 1) * QPAD], kcat))
        for hh in range(hp):
            s = logits[hh]
            if masked:
                s = jnp.where(_tri(t), s, NEG)
            m_prev = m_sc[hh]
            m_new = jnp.maximum(m_prev, jnp.max(s, axis=1, keepdims=True))
            alpha = jnp.exp2(m_prev - m_new)
            p = jnp.exp2(s - _lanes(m_new, t))
            l_sc[hh] = alpha * l_sc[hh] + jnp.sum(p, axis=1, keepdims=True)
            acc_sc[hh] = alpha * acc_sc[hh] + _dot_nn(p.astype(MXU_DTYPE), v_ref[ks, hh * HEAD:(hh + 1) * HEAD])
            m_sc[hh] = m_new

    def off_diag(j, carry):
        step(j, False)
        return carry

    lax.fori_loop(0, qi, off_diag, 0)
    step(qi, True)
    for hh in range(hp):
        l = l_sc[hh]
        o_ref[:, hh * HEAD:(hh + 1) * HEAD] = (acc_sc[hh] / l).astype(o_ref.dtype)
        lse_ref[:, hh * HEAD:(hh + 1) * HEAD] = m_sc[hh] + jnp.log2(l)


def _mla_fwd(qf, kv, kr, mixed):
    T = qf.shape[0]
    t, hp = min(MLA_T, T), MLA_HP
    ng = NH // hp
    return _pcall(
        functools.partial(_mla_fwd_body, t=t, hp=hp), name="mla_fwd",
        grid=(ng, T // t),
        in_specs=[pl.BlockSpec((t, hp * QPAD), lambda g, i: (i, g)),
                  pl.BlockSpec((T, hp * HEAD), lambda g, i: (0, g)),
                  pl.BlockSpec((T, HEAD), lambda g, i: (0, 0)),
                  pl.BlockSpec((T, hp * HEAD), lambda g, i: (0, ng + g)), ANY],
        out_specs=[pl.BlockSpec((t, hp * HEAD), lambda g, i: (i, ng + g)),
                   pl.BlockSpec((t, hp * HEAD), lambda g, i: (i, g))],
        out_shape=[jax.ShapeDtypeStruct(mixed.shape, mixed.dtype), jax.ShapeDtypeStruct((T, A_W), F32)],
        input_output_aliases={4: 0},
        scratch_shapes=[pltpu.VMEM((hp, t, HEAD), F32)] * 3,
        compiler_params=pltpu.CompilerParams(dimension_semantics=("parallel", "parallel")),
    )(qf, kv, kr, kv, mixed)


def _mla_bwd_body(q_ref, kn_ref, kr_ref, v_ref, do_ref, o_ref, lse_ref, cb, sba, sbb,
                  dq_ref, dkn_ref, dv_ref, dkr_ref, dq_sc, dl_sc, dk_sc, dv_sc, *, t):
    ki = pl.program_id(1)
    nq = q_ref.shape[0] // t

    @pl.when(ki == 0)
    def _():
        def prep(i, carry):
            rs = pl.ds(pl.multiple_of(i * t, t), t)
            dl = jnp.sum(do_ref[rs, :] * o_ref[rs, :].astype(F32), axis=1, keepdims=True)
            dl_sc[rs, :] = jnp.broadcast_to(dl, (t, HEAD))
            dq_sc[rs, :] = jnp.zeros((t, QPAD), F32)
            return carry
        lax.fori_loop(0, nq, prep, 0)

    kcat = jnp.concatenate([kn_ref[...], kr_ref[...]], axis=1)
    v = v_ref[...]
    dk_sc[...] = jnp.zeros(dk_sc.shape, F32)
    dv_sc[...] = jnp.zeros(dv_sc.shape, F32)

    def step(i, masked):
        qs = pl.ds(pl.multiple_of(i * t, t), t)
        q = q_ref[qs, :]
        do = do_ref[qs, :].astype(MXU_DTYPE)
        s = _dot_nt(q, kcat)
        dp = _dot_nt(do, v)
        p = jnp.exp2(s - _lanes(lse_ref[qs, :], t))
        if masked:
            p = jnp.where(_tri(t), p, 0.0)
        ds = (p * (dp - _lanes(dl_sc[qs, :], t))).astype(MXU_DTYPE)
        dv_sc[...] += _dot_tn(p.astype(MXU_DTYPE), do)
        dk_sc[...] += _dot_tn(ds, q)
        dq_sc[qs, :] += _dot_nn(ds, kcat) * MLA_SCALE

    step(ki, True)

    def off_diag(i, carry):
        step(i, False)
        return carry

    lax.fori_loop(ki + 1, nq, off_diag, 0)
    dk = dk_sc[...] * (1.0 / LOG2E)
    dkn_ref[...] = dk[:, 0:HEAD].astype(dkn_ref.dtype)
    dkr_ref[...] = dk[:, HEAD:QPAD]
    dv_ref[...] = dv_sc[...].astype(dv_ref.dtype)

    @pl.when(ki == nq - 1)
    def _():
        def emit(i, carry):
            rs = pl.ds(pl.multiple_of(i * t, t), t)
            dq_ref[rs, 0:HEAD] = dq_sc[rs, 0:HEAD].astype(dq_ref.dtype)
            dq_ref[rs, HEAD:QPAD] = _rope_t(dq_sc[rs, HEAD:QPAD], cb[rs, :], sba[rs, :], sbb[rs, :],
                                            ROPE_B // 2).astype(dq_ref.dtype)
            return carry
        lax.fori_loop(0, nq, emit, 0)


def _mla_bwd(qf, kv, kr, dmix, mixed, lse, tabs_b):
    T = qf.shape[0]
    t = min(MLA_T, T)
    head = lambda h, j: (0, h)
    b_half = lambda h, j: (0, NH + h)
    kblk = pl.BlockSpec((t, HEAD), lambda h, j: (j, h))
    return _pcall(
        functools.partial(_mla_bwd_body, t=t), name="mla_bwd",
        grid=(NH, T // t),
        in_specs=[pl.BlockSpec((T, QPAD), head), kblk,
                  pl.BlockSpec((t, HEAD), lambda h, j: (j, 0)),
                  pl.BlockSpec((t, HEAD), lambda h, j: (j, NH + h)),
                  pl.BlockSpec((T, HEAD), b_half), pl.BlockSpec((T, HEAD), b_half),
                  pl.BlockSpec((T, HEAD), head)] + [pl.BlockSpec((T, HEAD), lambda h, j: (0, 0))] * 3,
        out_specs=[pl.BlockSpec((T, QPAD), head), kblk, kblk, kblk],
        out_shape=[jax.ShapeDtypeStruct((T, NH * QPAD), MXU_DTYPE), jax.ShapeDtypeStruct((T, A_W), MXU_DTYPE),
                   jax.ShapeDtypeStruct((T, A_W), MXU_DTYPE), jax.ShapeDtypeStruct((T, A_W), F32)],
        scratch_shapes=[pltpu.VMEM((T, QPAD), F32), pltpu.VMEM((T, HEAD), F32), pltpu.VMEM((t, QPAD), F32),
                        pltpu.VMEM((t, HEAD), F32)],
        compiler_params=pltpu.CompilerParams(dimension_semantics=("parallel", "arbitrary")),
    )(qf, kv, kr, kv, dmix, mixed, lse, *tabs_b)


def _local_step(x, pos, target, g1, g2, gq, gkv, g3, g4,
                in_weights, attn_weights, mlp_prefetch, mlp_weights, down_grad_ready, up_grad_ready, attn_grads_ready):
    T = x.shape[0]
    TR = 256
    mm = functools.partial(_matmul, tm=2048, tn=1024, tk=2048, b_outer=True)
    mm_k = functools.partial(_matmul, tm=1024, tn=1024, tk=2048)
    mm_g = functools.partial(_matmul, tm=1024, tn=1024, tk=4096, b_outer=True)

    inv_a = ROPE_THETA ** (-jnp.arange(0, ROT_A, 2, dtype=F32) / ROT_A)
    inv_b = ROPE_THETA ** (-jnp.arange(0, ROPE_B, 2, dtype=F32) / ROPE_B)
    inv = jnp.stack([jnp.concatenate([inv_a, inv_a, jnp.zeros((HEAD - ROT_A,), F32)]),
                     jnp.concatenate([inv_b, inv_b, jnp.zeros((HEAD - ROPE_B,), F32)])])
    inv = jnp.concatenate([inv, jnp.zeros((6, HEAD), F32)], axis=0)
    tabs = _rowwise(_rope_tab_body, [pos], [inv], [(HEAD, F32)] * 6, [], tr=512, name="rope_tables")

    (h,) = _rowwise(_rms_fwd_body, [x], [g1], [(D_MODEL, MXU_DTYPE)], [], tr=TR, name="rms_in")
    w_proj, attn_prefetched = in_weights([h, tabs[0]])
    (proj,) = mm(h, w_proj, dims="nt", out_dtypes=[F32], tm=1024, tn=PROJ_TILE, name="proj_in", after=attn_prefetched)
    q, k, v, cqn, ckvn, krope = _rowwise(
        _postproj_body, [proj] + tabs, [gq, gkv],
        [(A_W, F32)] * 3 + [(LORA, MXU_DTYPE)] * 2 + [(HEAD, MXU_DTYPE)], [], tr=TR, name="post_proj")
    mixed, lse_a = _dil_fwd(q, k, v)

    w_uq_p, w_ukv_p, w_out = attn_weights(cqn)

    def q_epi(acc, cb, sba, sbb):
        cols = []
        for hh in range(acc.shape[1] // QPAD):
            lo = hh * QPAD
            cols += [acc[:, lo:lo + HEAD], _rope(acc[:, lo + HEAD:lo + QPAD], cb, sba, sbb, ROPE_B // 2)]
        return (jnp.concatenate(cols, axis=1) * MLA_QSCALE,)
    (qf,) = mm(cqn, w_uq_p, dims="nn", out_dtypes=[MXU_DTYPE], name="q_up", epi=q_epi, row_extras=tuple(tabs[3:]))
    (kv,) = mm(ckvn, w_ukv_p, dims="nn", out_dtypes=[MXU_DTYPE], name="kv_up")
    mixed, lse_b = _mla_fwd(qf, kv, krope, mixed)
    (o,) = mm(mixed, w_out, dims="nn", out_dtypes=[F32], name="out_proj", after=mlp_prefetch(mixed))
    x1, h2 = _rowwise(_mid_body, [x, o], [g2, g3], [(D_MODEL, F32), (D_MODEL, MXU_DTYPE)], [], tr=TR, name="mid_norm")

    w_up, w_down = mlp_weights(h2)

    def up_epi(acc):
        r = jnp.maximum(acc, 0.0)
        return r * r, r
    u, r = mm(h2, w_up, dims="nn", out_dtypes=[MXU_DTYPE, MXU_DTYPE], name="mlp_up", epi=up_epi, b_shards=N_CHIPS)
    (dn,) = mm_k(u, w_down, dims="nn", out_dtypes=[F32], name="mlp_down")
    dy, dd, loss8, dg4 = _rowwise(_loss_body, [x1, dn, target], [g4], [(D_MODEL, F32), (D_MODEL, MXU_DTYPE)],
                                  [(8, HEAD), (8, D_MODEL)], tr=TR, name="loss_head")

    def dup_epi(acc, rr):
        return (acc * (2.0 * rr.astype(F32)),)
    (dup,) = mm(dd, w_down, dims="nt", out_dtypes=[MXU_DTYPE], name="d_up", epi=dup_epi, extras=(r,))
    (gw_down,) = mm_g(u, dd, dims="tn", out_dtypes=[WIRE_DTYPE], name="gw_down")
    (dh2,) = mm_k(dup, w_up, dims="nt", out_dtypes=[F32], name="d_h2", b_shards=N_CHIPS,
                  after=down_grad_ready(gw_down))
    (gw_up,) = mm_g(h2, dup, dims="tn", out_dtypes=[WIRE_DTYPE], name="gw_up", out_shards=N_CHIPS)
    g2 = g2 + up_grad_ready(gw_up)
    dx1, do, dg3, dg2 = _rowwise(_bmid_body, [dy, dh2, x1, o], [g2, g3], [(D_MODEL, F32), (D_MODEL, MXU_DTYPE)],
                                 [(8, D_MODEL), (8, D_MODEL)], tr=TR, name="bwd_mid")
    (dmix,) = mm(do, w_out, dims="nt", out_dtypes=[F32], name="d_mixed")
    (gw_out,) = mm_g(mixed, do, dims="tn", out_dtypes=[WIRE_DTYPE], name="gw_out")

    dq_pad, dkn, dvb, dkr = _mla_bwd(qf, kv, krope, dmix, mixed, lse_b, tabs[3:])
    (dcqn,) = mm(dq_pad, w_uq_p, dims="nt", out_dtypes=[F32], name="d_cq")
    (gw_uq_p,) = mm_g(cqn, dq_pad, dims="tn", out_dtypes=[WIRE_DTYPE], name="gw_uq")
    dkv = jnp.concatenate([dkn, dvb], axis=1)
    (dckvn,) = mm(dkv, w_ukv_p, dims="nt", out_dtypes=[F32], name="d_ckv")
    (gw_ukv_p,) = mm_g(ckvn, dkv, dims="tn", out_dtypes=[WIRE_DTYPE], name="gw_ukv")
    gq = gq + attn_grads_ready(gw_out, gw_uq_p, gw_ukv_p)

    dq_a, dk_a, dv_a = _dil_bwd(q, k, v, dmix, mixed, lse_a)
    dproj, dgq, dgkv = _rowwise(
        _dproj_body, [dq_a, dk_a, dv_a, dcqn, dckvn, proj, dkr] + tabs, [gq, gkv],
        [(PROJ_COLS, MXU_DTYPE)], [(8, LORA), (8, LORA)], tr=TR, name="d_proj")
    (dh,) = mm_k(dproj, w_proj, dims="nn", out_dtypes=[F32], tk=PROJ_TILE, name="d_h")
    (gw_proj,) = mm_g(dproj, h, dims="tn", out_dtypes=[WIRE_DTYPE], tm=PROJ_TILE, name="gw_in")
    dx, dg1 = _rowwise(_bin_body, [dx1, dh, x], [g1], [(D_MODEL, F32)], [(8, D_MODEL)], tr=TR, name="bwd_in")

    small = jnp.concatenate([dg1, dg2, dgq, dgkv, dg3, dg4, loss8], axis=1)
    return dx, gw_proj, small


def _place():
    x, y, c = lax.axis_index("x"), lax.axis_index("y"), lax.axis_index("c")
    chips = [(1 - x, y), (x, 1 - y), (1 - x, 1 - y)]
    return x, y, c, chips


def _cast_place_body(me_ref, w_ref, *rest):
    o_ref = rest[-1]
    o_ref[...] = w_ref[...].astype(o_ref.dtype)


def _cast_place(me_arr, w, name, after=None):
    rows, cols = w.shape
    tr = min(rows, 256)
    after = [] if after is None else [after]
    grid_spec = pltpu.PrefetchScalarGridSpec(
        num_scalar_prefetch=1, grid=(rows // tr,),
        in_specs=[pl.BlockSpec((tr, cols), lambda i, me: (i, 0))] + [ANY] * len(after),
        out_specs=pl.BlockSpec((None, tr, cols), lambda i, me: (me[0], i, 0)))
    return _pcall(
        _cast_place_body, name=name, grid_spec=grid_spec,
        out_shape=jax.ShapeDtypeStruct((N_CHIPS, rows, cols), WIRE_DTYPE),
        compiler_params=pltpu.CompilerParams(dimension_semantics=("parallel",)),
    )(me_arr, w, *after)


def _cast_place_t_body(me_ref, w_ref, o_ref, *, n):
    i = pl.program_id(0)

    @pl.when(i < n)
    def _():
        o_ref[...] = w_ref[...].astype(o_ref.dtype)

    @pl.when(i == n)
    def _():
        o_ref[...] = jnp.zeros_like(o_ref)


def _cast_place_t(me_arr, w_t, name):
    rows, cols = w_t.shape
    n = rows // IN_TR
    grid_spec = pltpu.PrefetchScalarGridSpec(
        num_scalar_prefetch=1, grid=(n + 1,),
        in_specs=[pl.BlockSpec((IN_TR, cols), lambda i, me: (jnp.minimum(i, n - 1), 0))],
        out_specs=pl.BlockSpec((IN_TR, cols), lambda i, me: (jnp.where(i < n, me[0] * n + i, N_CHIPS * n), 0)))
    return _pcall(
        functools.partial(_cast_place_t_body, n=n), name=name, grid_spec=grid_spec,
        out_shape=jax.ShapeDtypeStruct((PROJ_COLS, cols), WIRE_DTYPE),
        compiler_params=pltpu.CompilerParams(dimension_semantics=("arbitrary",)),
    )(me_arr, w_t)


HBM = pl.BlockSpec(memory_space=pltpu.HBM)
SEM = pl.BlockSpec(memory_space=pltpu.SEMAPHORE)
EFFECT = pltpu.SideEffectType.DATAFLOW_SIDE_EFFECTING


def _copy_start(make, arrays, after, name, n_sems):
    n_a = len(arrays)
    after = [] if after is None else [after]

    def body(*refs):
        for send, _ in make(refs[:n_a], refs[-n_a - 3], refs[-n_a - 2]):
            send.start()
        refs[-1][...] = jnp.zeros_like(refs[-1])

    res = _pcall(
        body, name=name,
        in_specs=[HBM] * n_a + [ANY] * len(after),
        out_specs=[SEM, SEM] + [HBM] * n_a + [pl.BlockSpec(memory_space=pltpu.VMEM)],
        out_shape=[pltpu.SemaphoreType.DMA((n_sems,)), pltpu.SemaphoreType.DMA((n_sems,))]
        + [pltpu.HBM(a.shape, a.dtype) for a in arrays] + [jax.ShapeDtypeStruct((8, HEAD), F32)],
        input_output_aliases={i: 2 + i for i in range(n_a)},
        compiler_params=pltpu.CompilerParams(has_side_effects=EFFECT),
    )(*[pltpu.with_memory_space_constraint(a, pltpu.HBM) for a in arrays], *after)
    return (res[0], res[1]), list(res[2:2 + n_a]), res[-1]


def _copy_wait(make, sems, arrays, after, name):
    n_a = len(arrays)
    after = list(after) if isinstance(after, (list, tuple)) else [after]

    def body(*refs):
        for send, recv in make(refs[:n_a], refs[n_a], refs[n_a + 1]):
            send.wait_send()
            recv.wait_recv()

    return list(_pcall(
        body, name=name,
        in_specs=[HBM] * n_a + [SEM, SEM] + [ANY] * len(after), out_specs=[HBM] * n_a,
        out_shape=[pltpu.HBM(a.shape, a.dtype) for a in arrays],
        input_output_aliases={i: i for i in range(n_a)},
        compiler_params=pltpu.CompilerParams(has_side_effects=EFFECT),
    )(*arrays, sems[0], sems[1], *after))


def _slot(buf, chip, half):
    if buf.ndim == 2:
        hc = buf.shape[1] // 2
        return buf.at[pl.ds(pl.multiple_of(chip * IN_SHARD, 16), IN_SHARD), pl.ds(pl.multiple_of(half * hc, HEAD), hc)]
    hr = buf.shape[1] // 2
    return buf.at[chip, pl.ds(pl.multiple_of(half * hr, 16), hr)]


def _ag_descs(bufs, send_sems, recv_sems):
    x, y, c, chips = _place()
    me = 2 * x + y
    out = []
    for w, buf in enumerate(bufs):
        for j, (px, py) in enumerate(chips):
            mk = lambda ref, w=w, j=j, px=px, py=py: pltpu.make_async_remote_copy(
                src_ref=ref, dst_ref=ref, send_sem=send_sems.at[w * 3 + j], recv_sem=recv_sems.at[w * 3 + j],
                device_id=(px, py, c), device_id_type=MESH)
            out.append((mk(_slot(buf, me, c)), mk(_slot(buf, 2 * px + py, c))))
    return out


def _fw_descs(bufs, send_sems, recv_sems):
    x, y, c, chips = _place()
    out = []
    for w, buf in enumerate(bufs):
        for j, (px, py) in enumerate(chips):
            def mk(which, w=w, j=j, buf=buf, px=px, py=py):
                ref = _slot(buf, 2 * px + py, which)
                return pltpu.make_async_remote_copy(
                    src_ref=ref, dst_ref=ref, send_sem=send_sems.at[w * 3 + j], recv_sem=recv_sems.at[w * 3 + j],
                    device_id=(x, y, 1 - c), device_id_type=MESH)
            out.append((mk(c), mk(1 - c)))
    return out


def _sc_descs(refs, send_sems, recv_sems):
    n_w = len(refs) // 2
    x, y, c, chips = _place()
    me = 2 * x + y
    out = []
    for w in range(n_w):
        for j, (px, py) in enumerate(chips):
            d = pltpu.make_async_remote_copy(
                src_ref=refs[w].at[2 * px + py], dst_ref=refs[n_w + w].at[me],
                send_sem=send_sems.at[w * 3 + j], recv_sem=recv_sems.at[w * 3 + j],
                device_id=(px, py, c), device_id_type=MESH)
            out.append((d, d))
    return out


def _pair_descs(src_of):
    def make(refs, send_sems, recv_sems):
        n_w = len(refs) // 2
        x, y, c, _ = _place()
        out = []
        for w in range(n_w):
            d = pltpu.make_async_remote_copy(
                src_ref=src_of(refs[w], c), dst_ref=refs[n_w + w],
                send_sem=send_sems.at[w], recv_sem=recv_sems.at[w],
                device_id=(x, y, 1 - c), device_id_type=MESH)
            out.append((d, d))
        return out
    return make


_EX_DESCS = _pair_descs(lambda g4, c: g4.at[:, 1 - c])
_SW_DESCS = _pair_descs(lambda half, c: half)
_EXT_DESCS = _pair_descs(lambda g, c: g.at[pl.ds(0, IN_COLS),
                                           pl.ds(pl.multiple_of((1 - c) * (D_MODEL // 2), HEAD), D_MODEL // 2)])


def _sm_descs(refs, send_sems, recv_sems):
    buf = refs[0]
    rows8 = buf.shape[0] // N_DEV
    x, y, c, _ = _place()
    flip = lambda v, d: 1 - v if d else v
    blk = lambda px, py, pc: buf.at[pl.ds(pl.multiple_of((4 * px + 2 * py + pc) * rows8, 8), rows8)]
    out = []
    for k in range(1, N_DEV):
        px, py, pc = flip(x, k & 4), flip(y, k & 2), flip(c, k & 1)
        mk = lambda ref, k=k, px=px, py=py, pc=pc: pltpu.make_async_remote_copy(
            src_ref=ref, dst_ref=ref, send_sem=send_sems.at[k - 1], recv_sem=recv_sems.at[k - 1],
            device_id=(px, py, pc), device_id_type=MESH)
        out.append((mk(blk(x, y, c)), mk(blk(px, py, pc))))
    return out


def _place_rows_body(i_ref, x_ref, o_ref):
    o_ref[...] = x_ref[...]


def _place_rows(i_arr, x, n_blocks, name):
    r, n = x.shape
    grid_spec = pltpu.PrefetchScalarGridSpec(
        num_scalar_prefetch=1, grid=(1,),
        in_specs=[pl.BlockSpec((r, n), lambda g, i: (0, 0))],
        out_specs=pl.BlockSpec((r, n), lambda g, i: (i[0], 0)))
    return _pcall(_place_rows_body, name=name, grid_spec=grid_spec,
                  out_shape=jax.ShapeDtypeStruct((n_blocks * r, n), x.dtype))(i_arr, x)


def _ag_forward_body(*refs, n_w):
    bufs = refs[n_w:2 * n_w]
    send_sems, recv_sems = refs[2 * n_w:]
    pairs = _fw_descs(bufs, send_sems, recv_sems)
    for fw, _ in pairs:
        fw.start()
    for fw, back in pairs:
        back.wait_recv()
        fw.wait_send()


def _ag_forward(bufs, tag):
    n_w = len(bufs)
    return list(_pcall(
        functools.partial(_ag_forward_body, n_w=n_w), name="weight_allgather_forward_" + tag,
        in_specs=[ANY] * n_w, out_specs=[ANY] * n_w,
        out_shape=[jax.ShapeDtypeStruct(b.shape, b.dtype) for b in bufs],
        input_output_aliases={w: w for w in range(n_w)},
        scratch_shapes=[pltpu.SemaphoreType.DMA((3 * n_w,))] * 2,
    )(*bufs))


def _pair_add_body(c_ref, mine_ref, theirs_ref, o_ref):
    o_ref[...] = (mine_ref[...].astype(F32) + theirs_ref[...].astype(F32)).astype(o_ref.dtype)


def _pair_add(c_arr, g4, recv, name):
    _, _, hr, cols = g4.shape
    tr = min(hr, 256)
    grid_spec = pltpu.PrefetchScalarGridSpec(
        num_scalar_prefetch=1, grid=(N_CHIPS, hr // tr),
        in_specs=[pl.BlockSpec((None, None, tr, cols), lambda s, i, c: (s, c[0], i, 0)),
                  pl.BlockSpec((None, tr, cols), lambda s, i, c: (s, i, 0))],
        out_specs=pl.BlockSpec((None, tr, cols), lambda s, i, c: (s, i, 0)))
    return _pcall(
        _pair_add_body, name=name, grid_spec=grid_spec,
        out_shape=jax.ShapeDtypeStruct(recv.shape, recv.dtype),
        compiler_params=pltpu.CompilerParams(dimension_semantics=("parallel", "parallel")),
    )(c_arr, g4, recv)


def _pair_add_t(c_arr, g, recv, name):
    rows, hc = recv.shape
    grid_spec = pltpu.PrefetchScalarGridSpec(
        num_scalar_prefetch=1, grid=(rows // IN_TR,),
        in_specs=[pl.BlockSpec((IN_TR, hc), lambda i, c: (i, c[0])), pl.BlockSpec((IN_TR, hc), lambda i, c: (i, 0))],
        out_specs=pl.BlockSpec((IN_TR, hc), lambda i, c: (i, 0)))
    return _pcall(
        _pair_add_body, name=name, grid_spec=grid_spec,
        out_shape=jax.ShapeDtypeStruct(recv.shape, recv.dtype),
        compiler_params=pltpu.CompilerParams(dimension_semantics=("parallel",)),
    )(c_arr, g, recv)


def _sum4_body(me_ref, p_ref, l0, l1, l2, l3, o_ref):
    me = me_ref[0]
    t = [jnp.where(me == j, p_ref[...], l[...]).astype(F32) for j, l in enumerate((l0, l1, l2, l3))]
    o_ref[...] = ((t[0] + t[1]) + t[2]) + t[3]


def _sum4(me_arr, part, landed, name):
    _, hr, cols = part.shape
    tr = IN_TR if hr == IN_SHARD else min(hr, 256)

    def slot(j):
        return lambda i, me: (jnp.where(me[0] == j, (j + 1) % N_CHIPS, j), i, 0)

    grid_spec = pltpu.PrefetchScalarGridSpec(
        num_scalar_prefetch=1, grid=(hr // tr,),
        in_specs=[pl.BlockSpec((None, tr, cols), lambda i, me: (me[0], i, 0))]
        + [pl.BlockSpec((None, tr, cols), slot(j)) for j in range(N_CHIPS)],
        out_specs=pl.BlockSpec((tr, cols), lambda i, me: (i, 0)))
    return _pcall(
        _sum4_body, name=name, grid_spec=grid_spec,
        out_shape=jax.ShapeDtypeStruct((hr, cols), F32),
        compiler_params=pltpu.CompilerParams(dimension_semantics=("parallel",)),
    )(me_arr, part, landed, landed, landed, landed)


def _adamw(w, g, m, v):
    m = ADAM_B1 * m + (1.0 - ADAM_B1) * g
    v = ADAM_B2 * v + (1.0 - ADAM_B2) * (g * g)
    m_hat = m / (1.0 - ADAM_B1 ** ADAM_STEP)
    v_hat = v / (1.0 - ADAM_B2 ** ADAM_STEP)
    delta = -ADAM_LR * (m_hat / (jnp.sqrt(v_hat) + ADAM_EPS) + ADAM_WD * w)
    return delta, m, v


def _adamw_half_body(h_ref, w_ref, g_in_ref, m_ref, v_ref, *rest):
    g_ref, d_ref, nm_ref, nv_ref, done_ref = rest[-5:]
    done_ref[...] = jnp.zeros_like(done_ref)
    g = g_in_ref[...]
    g_ref[...] = g
    d, m, v = _adamw(w_ref[...], g, m_ref[...], v_ref[...])
    d_ref[...] = d
    nm_ref[...] = m
    nv_ref[...] = v


def _adamw_half(h_arr, w, g_half, m, v, prev, name):
    rows, cols = w.shape
    if g_half.shape[0] == rows:
        tr, nh = IN_TR, rows // IN_TR
        at_half = pl.BlockSpec((tr, cols // 2), lambda i, h: (i, h[0]))
        g_spec = pl.BlockSpec((tr, cols // 2), lambda i, h: (i, 0))
    else:
        tr = min(rows // 2, 128)
        nh = (rows // 2) // tr
        at_half = pl.BlockSpec((tr, cols), lambda i, h: (h[0] * nh + i, 0))
        g_spec = pl.BlockSpec((tr, cols), lambda i, h: (i, 0))
    grid_spec = pltpu.PrefetchScalarGridSpec(
        num_scalar_prefetch=1, grid=(nh,),
        in_specs=[at_half, g_spec, at_half, at_half] + [ANY] * len(prev),
        out_specs=[at_half] * 4 + [pl.BlockSpec((8, HEAD), lambda i, h: (0, 0))])
    return list(_pcall(
        _adamw_half_body, name=name, grid_spec=grid_spec,
        out_shape=[jax.ShapeDtypeStruct(w.shape, F32)] * 4 + [jax.ShapeDtypeStruct((8, HEAD), F32)],
        input_output_aliases={5 + k: k for k in range(len(prev))},
        compiler_params=pltpu.CompilerParams(dimension_semantics=("arbitrary",)),
    )(h_arr, w, g_half, m, v, *prev))


def _small_update_body(gath_ref, w_ref, m_ref, v_ref, g_ref, d_ref, nm_ref, nv_ref, loss_ref, *, n_gain):
    tot = gath_ref[0:1, :]
    for i in range(1, gath_ref.shape[0]):
        tot = tot + gath_ref[i:i + 1, :]
    g = tot[:, 0:n_gain]
    g_ref[...] = g
    d, m, v = _adamw(w_ref[...], g, m_ref[...], v_ref[...])
    d_ref[...] = d
    nm_ref[...] = m
    nv_ref[...] = v
    loss_ref[...] = (0.5 / D_MODEL) * jnp.sum(tot[:, n_gain:n_gain + HEAD], axis=1, keepdims=True) * jnp.ones((1, HEAD), F32)


def _small_update(gath, w, m, v):
    n_gain = w.shape[1]
    vm = pl.BlockSpec(memory_space=pltpu.VMEM)
    return _pcall(
        functools.partial(_small_update_body, n_gain=n_gain), name="gain_update",
        in_specs=[vm] * 4, out_specs=[vm] * 5,
        out_shape=[jax.ShapeDtypeStruct((1, n_gain), F32)] * 4 + [jax.ShapeDtypeStruct((1, HEAD), F32)],
    )(gath, w, m, v)


def kernel(x, positions, norm_attn_pre, norm_attn_post, w_in, q_latent_norm, kv_latent_norm, w_uq, w_ukv, w_out, norm_mlp_pre, norm_mlp_post, w_up, w_down, loss_target, m_norm_attn_pre, m_norm_attn_post, m_w_in, m_q_latent_norm, m_kv_latent_norm, m_w_uq, m_w_ukv, m_w_out, m_norm_mlp_pre, m_norm_mlp_post, m_w_up, m_w_down, v_norm_attn_pre, v_norm_attn_post, v_w_in, v_q_latent_norm, v_kv_latent_norm, v_w_uq, v_w_ukv, v_w_out, v_norm_mlp_pre, v_norm_mlp_post, v_w_up, v_w_down):
    T = x.shape[1]
    c_arr = lax.axis_index("c").astype(jnp.int32).reshape(1)
    me_arr = (2 * lax.axis_index("x") + lax.axis_index("y")).astype(jnp.int32).reshape(1)
    names = ["w_in", "w_uq", "w_ukv", "w_out", "w_up", "w_down"]

    transposed = lambda a: jnp.swapaxes(a, 1, 2)
    mats = [transposed(w_in)[0], w_uq[0], w_ukv[0], w_out[0], w_up[0], w_down[0]]
    me8_arr = (4 * lax.axis_index("x") + 2 * lax.axis_index("y") + lax.axis_index("c")).astype(jnp.int32).reshape(1)
    col_major = lambda g: jnp.transpose(g, (1, 0, 2)).reshape(g.shape[1], N_CHIPS * g.shape[2])
    cast = lambda a: a.astype(MXU_DTYPE)
    to_shards = lambda g: jnp.transpose(g.reshape(g.shape[0], N_CHIPS, g.shape[1] // N_CHIPS), (1, 0, 2))
    halved = lambda g: g.reshape(N_CHIPS, 2, g.shape[1] // 2, g.shape[2])
    empty = lambda a, shape=None: lax.empty(a.shape if shape is None else shape, a.dtype)

    sem_in, buf_in, going = _copy_start(_ag_descs, [_cast_place_t(me_arr, mats[0], "cast_w_in")], None,
                                        "weight_allgather_start_in", 3)
    placed = [_cast_place(me_arr, w, "cast_" + n, going) for w, n in zip(mats[1:], names[1:])]
    sem_att, buf_att, going = _copy_start(_ag_descs, placed[:3], going, "weight_allgather_start_attn", 9)
    sem_mlp, buf_mlp, started = _copy_start(_ag_descs, placed[3:], going, "weight_allgather_start_mlp", 6)

    going_on = {}

    def in_weights(after):
        (win_g,) = _ag_forward(_copy_wait(_ag_descs, sem_in, buf_in, after, "weight_allgather_wait_in"), "in")
        landed = _copy_wait(_ag_descs, sem_att, buf_att, win_g, "weight_allgather_wait_attn")
        going_on["fw_attn"] = _copy_start(_fw_descs, landed, None, "weight_allgather_forward_start_attn", 9)
        return cast(win_g), going_on["fw_attn"][-1]

    def attn_weights(after):
        sems, bufs, _ = going_on["fw_attn"]
        wuq_g, wukv_g, wout_g = _copy_wait(_fw_descs, sems, bufs, after, "weight_allgather_forward_wait_attn")
        wuq_full = col_major(wuq_g).reshape(LORA, NH, HEAD + ROPE_B)
        w_uq_p = jnp.pad(wuq_full, ((0, 0), (0, 0), (0, QPAD - HEAD - ROPE_B))).reshape(LORA, NH * QPAD)
        w_ukv_p = col_major(wukv_g).reshape(LORA, NH, 2, HEAD).transpose(0, 2, 1, 3).reshape(LORA, 2 * A_W)
        return cast(w_uq_p), cast(w_ukv_p), cast(wout_g.reshape(2 * A_W, D_MODEL))

    def mlp_prefetch(after):
        landed = _copy_wait(_ag_descs, sem_mlp, buf_mlp, after, "weight_allgather_wait_mlp")
        going_on["fw"] = _copy_start(_fw_descs, landed, None, "weight_allgather_forward_start_mlp", 6)
        return going_on["fw"][-1]

    def mlp_weights(after):
        sems, bufs, _ = going_on["fw"]
        wup_g, wdown_g = _copy_wait(_fw_descs, sems, bufs, after, "weight_allgather_forward_wait_mlp")
        return cast(wup_g), cast(wdown_g.reshape(D_FF, D_MODEL))

    def exchange_start(g4s, tag):
        lands = [empty(g, (g.shape[0],) + g.shape[2:]) for g in g4s]
        return _copy_start(_EX_DESCS, g4s + lands, None, "grad_pair_exchange_start_" + tag, len(g4s))

    def exchange_finish(started_ex, after, ns, tag):
        sems, arrs, _ = started_ex
        arrs = _copy_wait(_EX_DESCS, sems, arrs, after, "grad_pair_exchange_wait_" + tag)
        n = len(ns)
        return [_pair_add(c_arr, g4, r, "pair_add_" + nm) for g4, r, nm in zip(arrs[:n], arrs[n:], ns)]

    def scatter_start(parts, after, tag):
        return _copy_start(_sc_descs, parts + [empty(p) for p in parts], after, "grad_scatter_start_" + tag,
                           3 * len(parts))

    def scatter_finish(started_sc, after, tag):
        sems, arrs, _ = started_sc
        arrs = _copy_wait(_sc_descs, sems, arrs, after, "grad_scatter_wait_" + tag)
        return arrs[:len(arrs) // 2], arrs[len(arrs) // 2:]

    def down_grad_ready(gw_down):
        going_on["x_down"] = exchange_start([halved(gw_down.reshape(N_CHIPS, D_MODEL, D_MODEL))], "down")
        return going_on["x_down"][-1]

    def up_grad_ready(gw_up):
        going_on["x_up"] = exchange_start([halved(gw_up)], "up")
        parts = exchange_finish(going_on["x_down"], going_on["x_up"][-1], names[5:], "down")
        going_on["s_down"] = scatter_start(parts, started, "down")
        return going_on["s_down"][-1][0:1, 0:1]

    def attn_grads_ready(gw_out, gw_uq_p, gw_ukv_p):
        gw_uq = to_shards(gw_uq_p.reshape(LORA, NH, QPAD)[:, :, :HEAD + ROPE_B].reshape(LORA, NH * (HEAD + ROPE_B)))
        gw_ukv = to_shards(gw_ukv_p.reshape(LORA, 2, NH, HEAD).transpose(0, 2, 1, 3).reshape(LORA, 2 * A_W))
        full4 = [halved(g) for g in (gw_uq, gw_ukv, gw_out.reshape(N_CHIPS, LORA, D_MODEL))]
        x_attn = exchange_start(full4, "attn")
        parts_up = exchange_finish(going_on["x_up"], x_attn[-1], names[4:5], "up")
        parts = exchange_finish(x_attn, parts_up[0], names[1:4], "attn") + parts_up
        going_on["s_rest"] = scatter_start(parts, going_on["s_down"][-1], "attn_up")
        return going_on["s_rest"][-1][0:1, 0:1]

    dx, gw_proj, small = _local_step(
        x[0], positions[0].astype(F32).reshape(T, 1), loss_target[0],
        norm_attn_pre + started[0:1, 0:1], norm_attn_post, q_latent_norm, kv_latent_norm, norm_mlp_pre, norm_mlp_post,
        in_weights, attn_weights, mlp_prefetch, mlp_weights, down_grad_ready, up_grad_ready, attn_grads_ready)

    ms = [transposed(m_w_in)[0], m_w_uq[0], m_w_ukv[0], m_w_out[0], m_w_up[0], m_w_down[0]]
    vs = [transposed(v_w_in)[0], v_w_uq[0], v_w_ukv[0], v_w_out[0], v_w_up[0], v_w_down[0]]
    sib_arr = 1 - c_arr

    def finish(parts, landed, lo, hi, tag):
        sl = slice(lo, hi)
        halves = [_sum4(me_arr, p, l, "chip_sum_" + n) for p, l, n in zip(parts, landed, names[sl])]
        n = len(halves)
        sems, arrs, _ = _copy_start(_SW_DESCS, halves + [empty(h) for h in halves], None,
                                    "grad_pair_swap_start_" + tag, n)
        own = [_adamw_half(c_arr, w, g, m, v, [], "adamw_own_" + nm)
               for w, g, m, v, nm in zip(mats[sl], arrs[:n], ms[sl], vs[sl], names[sl])]
        arrs = _copy_wait(_SW_DESCS, sems, arrs, own[-1][4], "grad_pair_swap_wait_" + tag)
        return [_adamw_half(sib_arr, w, g, m, v, prev[:4], "adamw_sib_" + nm)
                for w, g, m, v, prev, nm in zip(mats[sl], arrs[n:], ms[sl], vs[sl], own, names[sl])]

    sem_small, (gath,), small_going = _copy_start(
        _sm_descs, [_place_rows(me8_arr, small, N_DEV, "place_small")], None, "small_allgather_start", N_DEV - 1)
    sems, arrs, _ = _copy_start(_EXT_DESCS, [gw_proj, lax.empty((IN_COLS, D_MODEL // 2), WIRE_DTYPE)], None,
                                "grad_pair_exchange_start_in", 1)
    gw_proj, from_sib = _copy_wait(_EXT_DESCS, sems, arrs, small_going, "grad_pair_exchange_wait_in")
    part_in = _pair_add_t(c_arr, gw_proj, from_sib, "pair_add_w_in").reshape(N_CHIPS, IN_SHARD, D_MODEL // 2)
    s_in = scatter_start([part_in], None, "in")
    parts_rest, landed_rest = scatter_finish(going_on["s_rest"], s_in[-1], "attn_up")
    parts_down, landed_down = scatter_finish(going_on["s_down"], landed_rest[0], "down")
    upd_rest = finish(parts_rest + parts_down, landed_rest + landed_down, 1, 6, "rest")
    parts_in, landed_in = scatter_finish(s_in, upd_rest[-1][0], "in")
    upd = finish(parts_in, landed_in, 0, 1, "in") + upd_rest
    grads = [u[0] for u in upd]

    (gath,) = _copy_wait(_sm_descs, sem_small, [gath], grads[0], "small_allgather_wait")
    gains = [norm_attn_pre, norm_attn_post, q_latent_norm, kv_latent_norm, norm_mlp_pre, norm_mlp_post]
    gm = [m_norm_attn_pre, m_norm_attn_post, m_q_latent_norm, m_kv_latent_norm, m_norm_mlp_pre, m_norm_mlp_post]
    gv = [v_norm_attn_pre, v_norm_attn_post, v_q_latent_norm, v_kv_latent_norm, v_norm_mlp_pre, v_norm_mlp_post]
    cat = lambda xs: jnp.concatenate(xs, axis=1)
    g_s, d_s, m_s, v_s, loss_v = _small_update(gath, cat(gains), cat(gm), cat(gv))
    widths = [a.shape[1] for a in gains]
    offs = [sum(widths[:i]) for i in range(len(widths))]
    split = lambda a: [a[:, o:o + w] for o, w in zip(offs, widths)]
    g_gain, d_gain, m_gain, v_gain = split(g_s), split(d_s), split(m_s), split(v_s)

    def ordered(gain_list, mat_list):
        gl, ml = gain_list, [transposed(mat_list[0][None])] + [a[None] for a in mat_list[1:]]
        return [gl[0], gl[1], ml[0], gl[2], gl[3], ml[1], ml[2], ml[3], gl[4], gl[5], ml[4], ml[5]]

    loss = loss_v[0, 0]
    return (loss, dx[None],
            *ordered(g_gain, grads),
            *ordered(d_gain, [u[1] for u in upd]),
            *ordered(m_gain, [u[2] for u in upd]),
            *ordered(v_gain, [u[3] for u in upd]))
```

```python
import functools

import jax
import jax.numpy as jnp
from jax import lax
from jax.experimental import pallas as pl
from jax.experimental.pallas import tpu as pltpu

F32 = jnp.float32
BF16 = jnp.bfloat16
MXU_DTYPE = jnp.bfloat16
WIRE_DTYPE = jnp.bfloat16

D_MODEL = 2048
HEAD = 128
NH = 8
A_W = NH * HEAD
LORA = 512
ROPE_B = 64
QPAD = 256
MAIN_COLS = 3 * A_W + 2 * LORA
IN_COLS = MAIN_COLS + ROPE_B
PROJ_COLS = MAIN_COLS + HEAD
PROJ_TILE = PROJ_COLS // 3
IN_SHARD = 1040
IN_TR = 208
D_FF = 4 * D_MODEL
DIL = (1, 4, 16)
ROT_A = 32
ROPE_THETA = 500000.0
EPS = 1e-6
NEG = -1e30
N_CHIPS = 4
N_DEV = 8

ADAM_LR = 0.001
ADAM_B1 = 0.9
ADAM_B2 = 0.999
ADAM_EPS = 1e-08
ADAM_WD = 0.01
ADAM_STEP = 10

MESH = pl.DeviceIdType.MESH
ANY = pl.BlockSpec(memory_space=pl.ANY)


def _pcall(body, **kw):
    return pl.pallas_call(body, **kw)


_DIMS = {
    "nn": (((1,), (0,)), ((), ())),
    "nt": (((1,), (1,)), ((), ())),
    "tn": (((0,), (0,)), ((), ())),
}


def _mm_body(*refs, dims, nk, epi, n_extra, n_after, n_out):
    a_ref, b_ref = refs[0], refs[1]
    extra = refs[2:2 + n_extra]
    outs = refs[2 + n_extra + n_after:2 + n_extra + n_after + n_out]
    part = lax.dot_general(a_ref[...], b_ref[...], _DIMS[dims], preferred_element_type=F32)

    def finish(acc):
        res = epi(acc, *[r[...] for r in extra]) if epi is not None else (acc,)
        for o_ref, o in zip(outs, res):
            o_ref[...] = o.astype(o_ref.dtype)

    if nk == 1:
        finish(part)
        return
    acc_ref = refs[-1]
    k = pl.program_id(2)

    @pl.when(k == 0)
    def _():
        acc_ref[...] = part

    @pl.when(k > 0)
    def _():
        acc_ref[...] += part

    @pl.when(k == nk - 1)
    def _():
        finish(acc_ref[...])


def _matmul(a, b, *, dims, out_dtypes, tm, tn, tk, name, epi=None, extras=(), row_extras=(), b_outer=False,
            b_shards=0, out_shards=0, after=None):
    if b_shards:
        assert dims in ("nn", "nt") and b.shape[0] == b_shards
        b2 = (b.shape[1], b_shards * b.shape[2])
    else:
        b2 = b.shape
    if dims == "nn":
        (M, K), (K2, N) = a.shape, b2
    elif dims == "nt":
        (M, K), (N, K2) = a.shape, b2
    else:
        (K, M), (K2, N) = a.shape, b2
    assert K == K2, (a.shape, b.shape, dims)
    tm, tn, tk = min(tm, M), min(tn, N), min(tk, K)
    assert M % tm == 0 and N % tn == 0 and K % tk == 0, (name, M, N, K, tm, tn, tk)
    nk = K // tk

    def at(f):
        if b_outer:
            return lambda j, i, k: f(i, j, k)
        return f

    a_spec = {"nn": pl.BlockSpec((tm, tk), at(lambda i, j, k: (i, k))),
              "nt": pl.BlockSpec((tm, tk), at(lambda i, j, k: (i, k))),
              "tn": pl.BlockSpec((tk, tm), at(lambda i, j, k: (k, i)))}[dims]
    b_spec = {"nn": pl.BlockSpec((tk, tn), at(lambda i, j, k: (k, j))),
              "nt": pl.BlockSpec((tn, tk), at(lambda i, j, k: (j, k))),
              "tn": pl.BlockSpec((tk, tn), at(lambda i, j, k: (k, j)))}[dims]
    if b_shards:
        per = b.shape[2] // (tn if dims == "nn" else tk)
        assert per >= 1 and b.shape[2] % (tn if dims == "nn" else tk) == 0
        b_spec = {"nn": pl.BlockSpec((None, tk, tn), at(lambda i, j, k: (j // per, k, j % per))),
                  "nt": pl.BlockSpec((None, tn, tk), at(lambda i, j, k: (k // per, j, k % per)))}[dims]
    o_spec = pl.BlockSpec((tm, tn), at(lambda i, j, k: (i, j)))
    o_shape = (M, N)
    if out_shards:
        assert not extras and N % out_shards == 0 and (N // out_shards) % tn == 0
        o_per = (N // out_shards) // tn
        o_spec = pl.BlockSpec((None, tm, tn), at(lambda i, j, k: (j // o_per, i, j % o_per)))
        o_shape = (out_shards, M, N // out_shards)
    r_specs = [pl.BlockSpec((tm, r.shape[1]), at(lambda i, j, k: (i, 0))) for r in row_extras]
    after = [] if after is None else [after]
    body = functools.partial(_mm_body, dims=dims, nk=nk, epi=epi, n_extra=len(extras) + len(row_extras),
                             n_after=len(after), n_out=len(out_dtypes))
    res = _pcall(
        body, name=name,
        grid=(N // tn, M // tm, nk) if b_outer else (M // tm, N // tn, nk),
        in_specs=[a_spec, b_spec] + [o_spec] * len(extras) + r_specs + [ANY] * len(after),
        out_specs=[o_spec] * len(out_dtypes),
        out_shape=[jax.ShapeDtypeStruct(o_shape, dt) for dt in out_dtypes],
        scratch_shapes=[pltpu.VMEM((tm, tn), F32)] if nk > 1 else [],
        compiler_params=pltpu.CompilerParams(
            dimension_semantics=("parallel", "parallel", "arbitrary")),
    )(a, b, *extras, *row_extras, *after)
    return list(res)


def _rowwise(body, row_ins, vec_ins, row_outs, acc_outs, *, tr, name):
    T = row_ins[0].shape[0]
    tr = min(tr, T)
    assert T % tr == 0
    in_specs = [pl.BlockSpec((tr, a.shape[1]), lambda i: (i, 0)) for a in row_ins]
    in_specs += [pl.BlockSpec(a.shape, lambda i: (0, 0)) for a in vec_ins]
    out_specs = [pl.BlockSpec((tr, w), lambda i: (i, 0)) for (w, _) in row_outs]
    out_specs += [pl.BlockSpec(s, lambda i: (0, 0)) for s in acc_outs]
    out_shape = [jax.ShapeDtypeStruct((T, w), dt) for (w, dt) in row_outs]
    out_shape += [jax.ShapeDtypeStruct(s, F32) for s in acc_outs]
    sem = "arbitrary" if acc_outs else "parallel"
    return list(_pcall(
        body, name=name, grid=(T // tr,), in_specs=in_specs, out_specs=out_specs,
        out_shape=out_shape,
        compiler_params=pltpu.CompilerParams(dimension_semantics=(sem,)),
    )(*row_ins, *vec_ins))


def _rstd(x):
    return lax.rsqrt(jnp.mean(x * x, axis=-1, keepdims=True) + EPS)


def _rms_bwd(x, rstd, dyg):
    xh = x * rstd
    return rstd * (dyg - xh * jnp.mean(dyg * xh, axis=-1, keepdims=True)), xh


def _fold8(v):
    r, w = v.shape
    return jnp.sum(v.reshape(r // 8, 8, w), axis=0)


def _acc(ref, val):
    first = pl.program_id(0) == 0

    @pl.when(first)
    def _():
        ref[...] = val

    @pl.when(jnp.logical_not(first))
    def _():
        ref[...] += val


def _rope(x, c, sa, sb, half):
    return x * c + pltpu.roll(x, HEAD - half, 1) * sa + pltpu.roll(x, half, 1) * sb


def _rope_t(dy, c, sa, sb, half):
    return dy * c - pltpu.roll(dy, HEAD - half, 1) * sa - pltpu.roll(dy, half, 1) * sb


def _rope_tab_body(pos_ref, inv_ref, ca, saa, sab, cb, sba, sbb):
    pos = pos_ref[...]
    lane = lax.broadcasted_iota(jnp.int32, (pos.shape[0], HEAD), 1)
    ang_a = pos * inv_ref[0:1, :]
    ang_b = pos * inv_ref[1:2, :]
    c, s = jnp.cos(ang_a), jnp.sin(ang_a)
    ha = ROT_A // 2
    ca[...] = jnp.where(lane < ROT_A, c, 1.0)
    saa[...] = jnp.where(lane < ha, -s, 0.0)
    sab[...] = jnp.where((lane >= ha) & (lane < ROT_A), s, 0.0)
    c, s = jnp.cos(ang_b), jnp.sin(ang_b)
    hb = ROPE_B // 2
    cb[...] = jnp.where(lane < ROPE_B, c, 1.0)
    sba[...] = jnp.where(lane < hb, -s, 0.0)
    sbb[...] = jnp.where((lane >= hb) & (lane < ROPE_B), s, 0.0)


def _rms_fwd_body(x_ref, g_ref, h_ref):
    x = x_ref[...]
    h_ref[...] = ((x * _rstd(x)) * g_ref[...]).astype(h_ref.dtype)


def _postproj_body(p_ref, ca, saa, sab, cb, sba, sbb, gq_ref, gkv_ref,
                   q_ref, k_ref, v_ref, cqn_ref, ckvn_ref, krope_ref):
    c, sa, sb = ca[...], saa[...], sab[...]
    for h in range(NH):
        lo = h * HEAD
        q_ref[:, lo:lo + HEAD] = _rope(p_ref[:, lo:lo + HEAD], c, sa, sb, ROT_A // 2).astype(q_ref.dtype)
        k_ref[:, lo:lo + HEAD] = _rope(p_ref[:, A_W + lo:A_W + lo + HEAD], c, sa, sb, ROT_A // 2).astype(k_ref.dtype)
    v_ref[...] = p_ref[:, 2 * A_W:3 * A_W].astype(v_ref.dtype)
    cq = p_ref[:, 3 * A_W:3 * A_W + LORA]
    cqn_ref[...] = ((cq * _rstd(cq)) * gq_ref[...]).astype(cqn_ref.dtype)
    ckv = p_ref[:, 3 * A_W + LORA:MAIN_COLS]
    ckvn_ref[...] = ((ckv * _rstd(ckv)) * gkv_ref[...]).astype(ckvn_ref.dtype)
    krope_ref[...] = _rope(p_ref[:, MAIN_COLS:PROJ_COLS], cb[...], sba[...], sbb[...], ROPE_B // 2).astype(krope_ref.dtype)


def _mid_body(x_ref, o_ref, g2_ref, g3_ref, x1_ref, h2_ref):
    o = o_ref[...]
    x1 = x_ref[...] + (o * _rstd(o)) * g2_ref[...]
    x1_ref[...] = x1
    h2_ref[...] = ((x1 * _rstd(x1)) * g3_ref[...]).astype(h2_ref.dtype)


def _loss_body(x1_ref, d_ref, t_ref, g4_ref, dy_ref, dd_ref, loss_ref, dg4_ref):
    d = d_ref[...]
    rstd = _rstd(d)
    y = x1_ref[...] + (d * rstd) * g4_ref[...]
    e = y - t_ref[...]
    dy = e * (1.0 / D_MODEL)
    dy_ref[...] = dy
    dd, dh = _rms_bwd(d, rstd, dy * g4_ref[...])
    dd_ref[...] = dd.astype(dd_ref.dtype)
    _acc(dg4_ref, _fold8(dy * dh))
    e8 = _fold8(e * e)
    l = e8[:, 0:HEAD]
    for j in range(1, D_MODEL // HEAD):
        l = l + e8[:, j * HEAD:(j + 1) * HEAD]
    _acc(loss_ref, l)


def _bmid_body(dy_ref, dh2_ref, x1_ref, o_ref, g2_ref, g3_ref, dx1_ref, do_ref, dg3_ref, dg2_ref):
    x1 = x1_ref[...]
    dh2 = dh2_ref[...]
    dn, x1h = _rms_bwd(x1, _rstd(x1), dh2 * g3_ref[...])
    dx1 = dy_ref[...] + dn
    dx1_ref[...] = dx1
    _acc(dg3_ref, _fold8(dh2 * x1h))
    o = o_ref[...]
    do, oh = _rms_bwd(o, _rstd(o), dx1 * g2_ref[...])
    do_ref[...] = do.astype(do_ref.dtype)
    _acc(dg2_ref, _fold8(dx1 * oh))


def _dproj_body(dq_ref, dk_ref, dv_ref, dcq_ref, dckv_ref, p_ref, dkr_ref,
                ca, saa, sab, cb, sba, sbb, gq_ref, gkv_ref,
                dp_ref, dgq_ref, dgkv_ref):
    c, sa, sb = ca[...], saa[...], sab[...]
    for h in range(NH):
        lo = h * HEAD
        dp_ref[:, lo:lo + HEAD] = _rope_t(dq_ref[:, lo:lo + HEAD], c, sa, sb, ROT_A // 2).astype(dp_ref.dtype)
        dp_ref[:, A_W + lo:A_W + lo + HEAD] = _rope_t(dk_ref[:, lo:lo + HEAD], c, sa, sb, ROT_A // 2).astype(dp_ref.dtype)
    dp_ref[:, 2 * A_W:3 * A_W] = dv_ref[...].astype(dp_ref.dtype)
    cq = p_ref[:, 3 * A_W:3 * A_W + LORA]
    dcqn = dcq_ref[...]
    dcq, cqh = _rms_bwd(cq, _rstd(cq), dcqn * gq_ref[...])
    dp_ref[:, 3 * A_W:3 * A_W + LORA] = dcq.astype(dp_ref.dtype)
    _acc(dgq_ref, _fold8(dcqn * cqh))
    ckv = p_ref[:, 3 * A_W + LORA:MAIN_COLS]
    dckvn = dckv_ref[...]
    dckv, ckvh = _rms_bwd(ckv, _rstd(ckv), dckvn * gkv_ref[...])
    dp_ref[:, 3 * A_W + LORA:MAIN_COLS] = dckv.astype(dp_ref.dtype)
    _acc(dgkv_ref, _fold8(dckvn * ckvh))
    dkr = dkr_ref[:, 0:HEAD]
    for h in range(1, NH):
        dkr = dkr + dkr_ref[:, h * HEAD:(h + 1) * HEAD]
    dp_ref[:, MAIN_COLS:PROJ_COLS] = _rope_t(dkr, cb[...], sba[...], sbb[...], ROPE_B // 2).astype(dp_ref.dtype)


def _bin_body(dx1_ref, dh_ref, x_ref, g1_ref, dx_ref, dg1_ref):
    x = x_ref[...]
    dh = dh_ref[...]
    dn, xh = _rms_bwd(x, _rstd(x), dh * g1_ref[...])
    dx_ref[...] = dx1_ref[...] + dn
    _acc(dg1_ref, _fold8(dh * xh))


def _dot_nt(a, b):
    return lax.dot_general(a, b, _DIMS["nt"], preferred_element_type=F32)


def _dot_tn(a, b):
    return lax.dot_general(a, b, _DIMS["tn"], preferred_element_type=F32)


def _dot_nn(a, b):
    return jnp.dot(a, b, preferred_element_type=F32)


DIL_SCALE = HEAD ** -0.5
DIL_CHUNK = 256


def _dil_rows(t, d):
    r = t & (d - 1)
    n = t >> (d.bit_length() - 1)
    start = r + n * (HEAD * d)
    has_prev = n > 0
    pstart = jnp.where(has_prev, start - HEAD * d, start)
    if d == 1:
        return pl.ds(pl.multiple_of(start, HEAD), HEAD), pl.ds(pl.multiple_of(pstart, HEAD), HEAD), has_prev
    return pl.ds(start, HEAD, stride=d), pl.ds(pstart, HEAD, stride=d), has_prev


def _dil_band():
    row = lax.broadcasted_iota(jnp.int32, (HEAD, 2 * HEAD), 0)
    col = lax.broadcasted_iota(jnp.int32, (HEAD, 2 * HEAD), 1)
    return (col >= row) & (col <= row + HEAD), col >= HEAD


def _dil_fwd_body(q_ref, k_ref, v_ref, a_ref, lse_ref, o1, o2, o3, l1, l2, l3, *, nt, unroll):
    band, is_cur = _dil_band()
    for d, o_sc, l_sc in zip(DIL, (o1, o2, o3), (l1, l2, l3)):

        def tile(t, carry, d=d, o_sc=o_sc, l_sc=l_sc):
            rows, prows, has_prev = _dil_rows(t, d)
            q = q_ref[rows, :].astype(MXU_DTYPE)
            kk = jnp.concatenate([k_ref[prows, :], k_ref[rows, :]], axis=0).astype(MXU_DTYPE)
            vv = jnp.concatenate([v_ref[prows, :], v_ref[rows, :]], axis=0).astype(MXU_DTYPE)
            ok = band & (is_cur | has_prev)
            s = jnp.where(ok, _dot_nt(q, kk) * DIL_SCALE, NEG)
            m = jnp.max(s, axis=1, keepdims=True)
            p = jnp.exp(s - m)
            den = jnp.sum(p, axis=1, keepdims=True)
            o_sc[rows, :] = _dot_nn((p / den).astype(MXU_DTYPE), vv)
            l_sc[rows, :] = jnp.broadcast_to(m + jnp.log(den), (HEAD, HEAD))
            return carry

        lax.fori_loop(0, nt, tile, 0, unroll=unroll)

    def merge(i, carry):
        rs = pl.ds(pl.multiple_of(i * DIL_CHUNK, DIL_CHUNK), DIL_CHUNK)
        la, lb, lc = l1[rs, :], l2[rs, :], l3[rs, :]
        m = jnp.maximum(jnp.maximum(la, lb), lc)
        wa, wb, wc = jnp.exp(la - m), jnp.exp(lb - m), jnp.exp(lc - m)
        den = wa + wb + wc
        a = (wa / den) * o1[rs, :] + (wb / den) * o2[rs, :] + (wc / den) * o3[rs, :]
        a_ref[rs, :] = a.astype(a_ref.dtype)
        lse_ref[rs, :] = m + jnp.log(den)
        return carry

    lax.fori_loop(0, q_ref.shape[0] // DIL_CHUNK, merge, 0)


def _dil_fwd(q, k, v):
    T = q.shape[0]
    spec = pl.BlockSpec((T, HEAD), lambda h: (0, h))
    return _pcall(
        functools.partial(_dil_fwd_body, nt=T // HEAD, unroll=16), name="dil_fwd",
        grid=(NH,), in_specs=[spec] * 3, out_specs=[spec] * 2,
        out_shape=[jax.ShapeDtypeStruct((T, 2 * A_W), MXU_DTYPE), jax.ShapeDtypeStruct((T, A_W), F32)],
        scratch_shapes=[pltpu.VMEM((T, HEAD), F32)] * 6,
        compiler_params=pltpu.CompilerParams(dimension_semantics=("parallel",)),
    )(q, k, v)


def _dil_bwd_body(q_ref, k_ref, v_ref, do_ref, a_ref, lse_ref, dq_ref, dk_ref, dv_ref, dl_sc, *, nt, unroll):
    band, is_cur = _dil_band()

    def prep(i, carry):
        rs = pl.ds(pl.multiple_of(i * DIL_CHUNK, DIL_CHUNK), DIL_CHUNK)
        dl = jnp.sum(do_ref[rs, :] * a_ref[rs, :].astype(F32), axis=1, keepdims=True)
        dl_sc[rs, :] = jnp.broadcast_to(dl, (DIL_CHUNK, HEAD))
        zero = jnp.zeros((DIL_CHUNK, HEAD), F32)
        dq_ref[rs, :] = zero
        dk_ref[rs, :] = zero
        dv_ref[rs, :] = zero
        return carry

    lax.fori_loop(0, q_ref.shape[0] // DIL_CHUNK, prep, 0)

    for d in DIL:

        def tile(t, carry, d=d):
            rows, prows, has_prev = _dil_rows(t, d)
            q = q_ref[rows, :].astype(MXU_DTYPE)
            kk = jnp.concatenate([k_ref[prows, :], k_ref[rows, :]], axis=0).astype(MXU_DTYPE)
            vv = jnp.concatenate([v_ref[prows, :], v_ref[rows, :]], axis=0).astype(MXU_DTYPE)
            do = do_ref[rows, :].astype(MXU_DTYPE)
            lse = lse_ref[rows, :]
            dl = dl_sc[rows, :]
            ok = band & (is_cur | has_prev)
            s = _dot_nt(q, kk) * DIL_SCALE
            p = jnp.where(ok, jnp.exp(s - jnp.concatenate([lse, lse], axis=1)), 0.0)
            ds = (p * (_dot_nt(do, vv) - jnp.concatenate([dl, dl], axis=1))).astype(MXU_DTYPE)
            dq_ref[rows, :] += _dot_nn(ds, kk) * DIL_SCALE
            dkk = _dot_tn(ds, q) * DIL_SCALE
            dvv = _dot_tn(p.astype(MXU_DTYPE), do)
            dk_ref[rows, :] += dkk[HEAD:, :]
            dv_ref[rows, :] += dvv[HEAD:, :]
            dk_ref[prows, :] += dkk[:HEAD, :]
            dv_ref[prows, :] += dvv[:HEAD, :]
            return carry

        lax.fori_loop(0, nt, tile, 0, unroll=unroll)


def _dil_bwd(q, k, v, dmix, mixed, lse):
    T = q.shape[0]
    spec = pl.BlockSpec((T, HEAD), lambda h: (0, h))
    return _pcall(
        functools.partial(_dil_bwd_body, nt=T // HEAD, unroll=8), name="dil_bwd",
        grid=(NH,), in_specs=[spec] * 6, out_specs=[spec] * 3,
        out_shape=[jax.ShapeDtypeStruct((T, A_W), F32)] * 3,
        scratch_shapes=[pltpu.VMEM((T, HEAD), F32)],
        compiler_params=pltpu.CompilerParams(dimension_semantics=("parallel",)),
    )(q, k, v, dmix, mixed, lse)


MLA_SCALE = (HEAD + ROPE_B) ** -0.5
LOG2E = 1.4426950408889634
MLA_QSCALE = MLA_SCALE * LOG2E
MLA_T = 512
MLA_HP = 4


def _tri(t):
    row = lax.broadcasted_iota(jnp.int32, (t, t), 0)
    col = lax.broadcasted_iota(jnp.int32, (t, t), 1)
    return col <= row


def _lanes(x, n):
    return jnp.tile(x, (1, n // HEAD))


def _mla_fwd_body(q_ref, kn_ref, kr_ref, v_ref, mixed_ref, o_ref, lse_ref, m_sc, l_sc, acc_sc, *, t, hp):
    del mixed_ref
    qi = pl.program_id(1)
    m_sc[...] = jnp.full(m_sc.shape, NEG, F32)
    l_sc[...] = jnp.zeros(l_sc.shape, F32)
    acc_sc[...] = jnp.zeros(acc_sc.shape, F32)

    def step(j, masked):
        ks = pl.ds(pl.multiple_of(j * t, t), t)
        kr = kr_ref[ks, :]
        logits = []
        for hh in range(hp):
            kcat = jnp.concatenate([kn_ref[ks, hh * HEAD:(hh + 1) * HEAD], kr], axis=1)
            logits.append(_dot_nt(q_ref[:, hh * QPAD:(hh + 1) * QPAD], kcat))
        for hh in range(hp):
            s = logits[hh]
            if masked:
                s = jnp.where(_tri(t), s, NEG)
            m_prev = m_sc[hh]
            m_new = jnp.maximum(m_prev, jnp.max(s, axis=1, keepdims=True))
            alpha = jnp.exp2(m_prev - m_new)
            p = jnp.exp2(s - _lanes(m_new, t))
            l_sc[hh] = alpha * l_sc[hh] + jnp.sum(p, axis=1, keepdims=True)
            acc_sc[hh] = alpha * acc_sc[hh] + _dot_nn(p.astype(MXU_DTYPE), v_ref[ks, hh * HEAD:(hh + 1) * HEAD])
            m_sc[hh] = m_new

    def off_diag(j, carry):
        step(j, False)
        return carry

    lax.fori_loop(0, qi, off_diag, 0)
    step(qi, True)
    for hh in range(hp):
        l = l_sc[hh]
        o_ref[:, hh * HEAD:(hh + 1) * HEAD] = (acc_sc[hh] / l).astype(o_ref.dtype)
        lse_ref[:, hh * HEAD:(hh + 1) * HEAD] = m_sc[hh] + jnp.log2(l)


def _mla_fwd(qf, kv, kr, mixed):
    T = qf.shape[0]
    t, hp = min(MLA_T, T), MLA_HP
    ng = NH // hp
    return _pcall(
        functools.partial(_mla_fwd_body, t=t, hp=hp), name="mla_fwd",
        grid=(ng, T // t),
        in_specs=[pl.BlockSpec((t, hp * QPAD), lambda g, i: (i, g)),
                  pl.BlockSpec((T, hp * HEAD), lambda g, i: (0, g)),
                  pl.BlockSpec((T, HEAD), lambda g, i: (0, 0)),
                  pl.BlockSpec((T, hp * HEAD), lambda g, i: (0, ng + g)), ANY],
        out_specs=[pl.BlockSpec((t, hp * HEAD), lambda g, i: (i, ng + g)),
                   pl.BlockSpec((t, hp * HEAD), lambda g, i: (i, g))],
        out_shape=[jax.ShapeDtypeStruct(mixed.shape, mixed.dtype), jax.ShapeDtypeStruct((T, A_W), F32)],
        input_output_aliases={4: 0},
        scratch_shapes=[pltpu.VMEM((hp, t, HEAD), F32)] * 3,
        compiler_params=pltpu.CompilerParams(dimension_semantics=("parallel", "parallel")),
    )(qf, kv, kr, kv, mixed)


def _mla_bwd_body(q_ref, kn_ref, kr_ref, v_ref, do_ref, o_ref, lse_ref, cb, sba, sbb,
                  dq_ref, dkn_ref, dv_ref, dkr_ref, dq_sc, dl_sc, dk_sc, dv_sc, *, t):
    ki = pl.program_id(1)
    nq = q_ref.shape[0] // t

    @pl.when(ki == 0)
    def _():
        def prep(i, carry):
            rs = pl.ds(pl.multiple_of(i * t, t), t)
            dl = jnp.sum(do_ref[rs, :] * o_ref[rs, :].astype(F32), axis=1, keepdims=True)
            dl_sc[rs, :] = jnp.broadcast_to(dl, (t, HEAD))
            dq_sc[rs, :] = jnp.zeros((t, QPAD), F32)
            return carry
        lax.fori_loop(0, nq, prep, 0)

    kcat = jnp.concatenate([kn_ref[...], kr_ref[...]], axis=1)
    v = v_ref[...]
    dk_sc[...] = jnp.zeros(dk_sc.shape, F32)
    dv_sc[...] = jnp.zeros(dv_sc.shape, F32)

    def step(i, masked):
        qs = pl.ds(pl.multiple_of(i * t, t), t)
        q = q_ref[qs, :]
        do = do_ref[qs, :].astype(MXU_DTYPE)
        s = _dot_nt(q, kcat)
        dp = _dot_nt(do, v)
        p = jnp.exp2(s - _lanes(lse_ref[qs, :], t))
        if masked:
            p = jnp.where(_tri(t), p, 0.0)
        ds = (p * (dp - _lanes(dl_sc[qs, :], t))).astype(MXU_DTYPE)
        dv_sc[...] += _dot_tn(p.astype(MXU_DTYPE), do)
        dk_sc[...] += _dot_tn(ds, q)
        dq_sc[qs, :] += _dot_nn(ds, kcat) * MLA_SCALE

    step(ki, True)

    def off_diag(i, carry):
        step(i, False)
        return carry

    lax.fori_loop(ki + 1, nq, off_diag, 0)
    dk = dk_sc[...] * (1.0 / LOG2E)
    dkn_ref[...] = dk[:, 0:HEAD].astype(dkn_ref.dtype)
    dkr_ref[...] = dk[:, HEAD:QPAD]
    dv_ref[...] = dv_sc[...].astype(dv_ref.dtype)

    @pl.when(ki == nq - 1)
    def _():
        def emit(i, carry):
            rs = pl.ds(pl.multiple_of(i * t, t), t)
            dq_ref[rs, 0:HEAD] = dq_sc[rs, 0:HEAD].astype(dq_ref.dtype)
            dq_ref[rs, HEAD:QPAD] = _rope_t(dq_sc[rs, HEAD:QPAD], cb[rs, :], sba[rs, :], sbb[rs, :],
                                            ROPE_B // 2).astype(dq_ref.dtype)
            return carry
        lax.fori_loop(0, nq, emit, 0)


def _mla_bwd(qf, kv, kr, dmix, mixed, lse, tabs_b):
    T = qf.shape[0]
    t = min(MLA_T, T)
    head = lambda h, j: (0, h)
    b_half = lambda h, j: (0, NH + h)
    kblk = pl.BlockSpec((t, HEAD), lambda h, j: (j, h))
    return _pcall(
        functools.partial(_mla_bwd_body, t=t), name="mla_bwd",
        grid=(NH, T // t),
        in_specs=[pl.BlockSpec((T, QPAD), head), kblk,
                  pl.BlockSpec((t, HEAD), lambda h, j: (j, 0)),
                  pl.BlockSpec((t, HEAD), lambda h, j: (j, NH + h)),
                  pl.BlockSpec((T, HEAD), b_half), pl.BlockSpec((T, HEAD), b_half),
                  pl.BlockSpec((T, HEAD), head)] + [pl.BlockSpec((T, HEAD), lambda h, j: (0, 0))] * 3,
        out_specs=[pl.BlockSpec((T, QPAD), head), kblk, kblk, kblk],
        out_shape=[jax.ShapeDtypeStruct((T, NH * QPAD), MXU_DTYPE), jax.ShapeDtypeStruct((T, A_W), MXU_DTYPE),
                   jax.ShapeDtypeStruct((T, A_W), MXU_DTYPE), jax.ShapeDtypeStruct((T, A_W), F32)],
        scratch_shapes=[pltpu.VMEM((T, QPAD), F32), pltpu.VMEM((T, HEAD), F32), pltpu.VMEM((t, QPAD), F32),
                        pltpu.VMEM((t, HEAD), F32)],
        compiler_params=pltpu.CompilerParams(dimension_semantics=("parallel", "arbitrary")),
    )(qf, kv, kr, kv, dmix, mixed, lse, *tabs_b)


def _local_step(x, pos, target, g1, g2, gq, gkv, g3, g4,
                in_weights, attn_prefetch, attn_weights, mlp_prefetch, mlp_weights,
                down_grad_ready, up_grad_ready, attn_grads_ready):
    T = x.shape[0]
    TR = 256
    mm = functools.partial(_matmul, tm=2048, tn=1024, tk=2048, b_outer=True)
    mm_k = functools.partial(_matmul, tm=1024, tn=1024, tk=2048)
    mm_g = functools.partial(_matmul, tm=1024, tn=1024, tk=4096, b_outer=True)

    inv_a = ROPE_THETA ** (-jnp.arange(0, ROT_A, 2, dtype=F32) / ROT_A)
    inv_b = ROPE_THETA ** (-jnp.arange(0, ROPE_B, 2, dtype=F32) / ROPE_B)
    inv = jnp.stack([jnp.concatenate([inv_a, inv_a, jnp.zeros((HEAD - ROT_A,), F32)]),
                     jnp.concatenate([inv_b, inv_b, jnp.zeros((HEAD - ROPE_B,), F32)])])
    inv = jnp.concatenate([inv, jnp.zeros((6, HEAD), F32)], axis=0)
    tabs = _rowwise(_rope_tab_body, [pos], [inv], [(HEAD, F32)] * 6, [], tr=512, name="rope_tables")

    (h,) = _rowwise(_rms_fwd_body, [x], [g1], [(D_MODEL, MXU_DTYPE)], [], tr=TR, name="rms_in")
    w_proj = in_weights([h, tabs[0]])
    (proj,) = mm(h, w_proj, dims="nt", out_dtypes=[F32], tm=1024, tn=PROJ_TILE, name="proj_in")
    gq = gq + attn_prefetch(proj)
    q, k, v, cqn, ckvn, krope = _rowwise(
        _postproj_body, [proj] + tabs, [gq, gkv],
        [(A_W, F32)] * 3 + [(LORA, MXU_DTYPE)] * 2 + [(HEAD, MXU_DTYPE)], [], tr=TR, name="post_proj")
    mixed, lse_a = _dil_fwd(q, k, v)

    w_uq_p, w_ukv_p, w_out = attn_weights(cqn)

    def q_epi(acc, cb, sba, sbb):
        cols = []
        for hh in range(acc.shape[1] // QPAD):
            lo = hh * QPAD
            cols += [acc[:, lo:lo + HEAD], _rope(acc[:, lo + HEAD:lo + QPAD], cb, sba, sbb, ROPE_B // 2)]
        return (jnp.concatenate(cols, axis=1) * MLA_QSCALE,)
    (qf,) = mm(cqn, w_uq_p, dims="nn", out_dtypes=[MXU_DTYPE], name="q_up", epi=q_epi, row_extras=tuple(tabs[3:]))
    (kv,) = mm(ckvn, w_ukv_p, dims="nn", out_dtypes=[MXU_DTYPE], name="kv_up")
    mixed, lse_b = _mla_fwd(qf, kv, krope, mixed)
    (o,) = mm(mixed, w_out, dims="nn", out_dtypes=[F32], name="out_proj", after=mlp_prefetch(mixed))
    x1, h2 = _rowwise(_mid_body, [x, o], [g2, g3], [(D_MODEL, F32), (D_MODEL, MXU_DTYPE)], [], tr=TR, name="mid_norm")

    w_up, w_down = mlp_weights(h2)

    def up_epi(acc):
        r = jnp.maximum(acc, 0.0)
        return r * r, r
    u, r = mm(h2, w_up, dims="nn", out_dtypes=[MXU_DTYPE, MXU_DTYPE], name="mlp_up", epi=up_epi, b_shards=N_CHIPS)
    (dn,) = mm_k(u, w_down, dims="nn", out_dtypes=[F32], name="mlp_down")
    dy, dd, loss8, dg4 = _rowwise(_loss_body, [x1, dn, target], [g4], [(D_MODEL, F32), (D_MODEL, MXU_DTYPE)],
                                  [(8, HEAD), (8, D_MODEL)], tr=TR, name="loss_head")

    def dup_epi(acc, rr):
        return (acc * (2.0 * rr.astype(F32)),)
    (dup,) = mm(dd, w_down, dims="nt", out_dtypes=[MXU_DTYPE], name="d_up", epi=dup_epi, extras=(r,))
    (gw_down,) = mm_g(u, dd, dims="tn", out_dtypes=[WIRE_DTYPE], name="gw_down")
    (dh2,) = mm_k(dup, w_up, dims="nt", out_dtypes=[F32], name="d_h2", b_shards=N_CHIPS,
                  after=down_grad_ready(gw_down))
    (gw_up,) = mm_g(h2, dup, dims="tn", out_dtypes=[WIRE_DTYPE], name="gw_up", out_shards=N_CHIPS)
    g2 = g2 + up_grad_ready(gw_up)
    dx1, do, dg3, dg2 = _rowwise(_bmid_body, [dy, dh2, x1, o], [g2, g3], [(D_MODEL, F32), (D_MODEL, MXU_DTYPE)],
                                 [(8, D_MODEL), (8, D_MODEL)], tr=TR, name="bwd_mid")
    (dmix,) = mm(do, w_out, dims="nt", out_dtypes=[F32], name="d_mixed")
    (gw_out,) = mm_g(mixed, do, dims="tn", out_dtypes=[WIRE_DTYPE], name="gw_out")

    dq_pad, dkn, dvb, dkr = _mla_bwd(qf, kv, krope, dmix, mixed, lse_b, tabs[3:])
    (dcqn,) = mm(dq_pad, w_uq_p, dims="nt", out_dtypes=[F32], name="d_cq")
    (gw_uq_p,) = mm_g(cqn, dq_pad, dims="tn", out_dtypes=[WIRE_DTYPE], name="gw_uq")
    dkv = jnp.concatenate([dkn, dvb], axis=1)
    (dckvn,) = mm(dkv, w_ukv_p, dims="nt", out_dtypes=[F32], name="d_ckv")
    (gw_ukv_p,) = mm_g(ckvn, dkv, dims="tn", out_dtypes=[WIRE_DTYPE], name="gw_ukv")
    gq = gq + attn_grads_ready(gw_out, gw_uq_p, gw_ukv_p)

    dq_a, dk_a, dv_a = _dil_bwd(q, k, v, dmix, mixed, lse_a)
    dproj, dgq, dgkv = _rowwise(
        _dproj_body, [dq_a, dk_a, dv_a, dcqn, dckvn, proj, dkr] + tabs, [gq, gkv],
        [(PROJ_COLS, MXU_DTYPE)], [(8, LORA), (8, LORA)], tr=TR, name="d_proj")
    (dh,) = mm_k(dproj, w_proj, dims="nn", out_dtypes=[F32], tk=PROJ_TILE, name="d_h")
    (gw_proj,) = mm_g(dproj, h, dims="tn", out_dtypes=[WIRE_DTYPE], tm=PROJ_TILE, name="gw_in")
    dx, dg1 = _rowwise(_bin_body, [dx1, dh, x], [g1], [(D_MODEL, F32)], [(8, D_MODEL)], tr=TR, name="bwd_in")

    small = jnp.concatenate([dg1, dg2, dgq, dgkv, dg3, dg4, loss8], axis=1)
    return dx, gw_proj, small


def _place():
    x, y, c = lax.axis_index("x"), lax.axis_index("y"), lax.axis_index("c")
    chips = [(1 - x, y), (x, 1 - y), (1 - x, 1 - y)]
    return x, y, c, chips


def _cast_place_body(me_ref, w_ref, *rest):
    o_ref = rest[-1]
    o_ref[...] = w_ref[...].astype(o_ref.dtype)


def _cast_place(me_arr, w, name, after=None):
    rows, cols = w.shape
    tr = min(rows, 256)
    after = [] if after is None else [after]
    grid_spec = pltpu.PrefetchScalarGridSpec(
        num_scalar_prefetch=1, grid=(rows // tr,),
        in_specs=[pl.BlockSpec((tr, cols), lambda i, me: (i, 0))] + [ANY] * len(after),
        out_specs=pl.BlockSpec((None, tr, cols), lambda i, me: (me[0], i, 0)))
    return _pcall(
        _cast_place_body, name=name, grid_spec=grid_spec,
        out_shape=jax.ShapeDtypeStruct((N_CHIPS, rows, cols), WIRE_DTYPE),
        compiler_params=pltpu.CompilerParams(dimension_semantics=("parallel",)),
    )(me_arr, w, *after)


def _cast_place_t_body(me_ref, w_ref, o_ref, *, n):
    i = pl.program_id(0)

    @pl.when(i < n)
    def _():
        o_ref[...] = w_ref[...].astype(o_ref.dtype)

    @pl.when(i == n)
    def _():
        o_ref[...] = jnp.zeros_like(o_ref)


def _cast_place_t(me_arr, w_t, name):
    rows, cols = w_t.shape
    n = rows // IN_TR
    grid_spec = pltpu.PrefetchScalarGridSpec(
        num_scalar_prefetch=1, grid=(n + 1,),
        in_specs=[pl.BlockSpec((IN_TR, cols), lambda i, me: (jnp.minimum(i, n - 1), 0))],
        out_specs=pl.BlockSpec((IN_TR, cols), lambda i, me: (jnp.where(i < n, me[0] * n + i, N_CHIPS * n), 0)))
    return _pcall(
        functools.partial(_cast_place_t_body, n=n), name=name, grid_spec=grid_spec,
        out_shape=jax.ShapeDtypeStruct((PROJ_COLS, cols), WIRE_DTYPE),
        compiler_params=pltpu.CompilerParams(dimension_semantics=("arbitrary",)),
    )(me_arr, w_t)


HBM = pl.BlockSpec(memory_space=pltpu.HBM)
SEM = pl.BlockSpec(memory_space=pltpu.SEMAPHORE)
EFFECT = pltpu.SideEffectType.DATAFLOW_SIDE_EFFECTING


def _copy_start(make, arrays, after, name, n_sems):
    n_a = len(arrays)
    after = [] if after is None else [after]

    def body(*refs):
        for send, _ in make(refs[:n_a], refs[-n_a - 3], refs[-n_a - 2]):
            send.start()
        refs[-1][...] = jnp.zeros_like(refs[-1])

    res = _pcall(
        body, name=name,
        in_specs=[HBM] * n_a + [ANY] * len(after),
        out_specs=[SEM, SEM] + [HBM] * n_a + [pl.BlockSpec(memory_space=pltpu.VMEM)],
        out_shape=[pltpu.SemaphoreType.DMA((n_sems,)), pltpu.SemaphoreType.DMA((n_sems,))]
        + [pltpu.HBM(a.shape, a.dtype) for a in arrays] + [jax.ShapeDtypeStruct((8, HEAD), F32)],
        input_output_aliases={i: 2 + i for i in range(n_a)},
        compiler_params=pltpu.CompilerParams(has_side_effects=EFFECT),
    )(*[pltpu.with_memory_space_constraint(a, pltpu.HBM) for a in arrays], *after)
    return (res[0], res[1]), list(res[2:2 + n_a]), res[-1]


def _copy_wait(make, sems, arrays, after, name):
    n_a = len(arrays)
    after = list(after) if isinstance(after, (list, tuple)) else [after]

    def body(*refs):
        for send, recv in make(refs[:n_a], refs[n_a], refs[n_a + 1]):
            send.wait_send()
            recv.wait_recv()

    return list(_pcall(
        body, name=name,
        in_specs=[HBM] * n_a + [SEM, SEM] + [ANY] * len(after), out_specs=[HBM] * n_a,
        out_shape=[pltpu.HBM(a.shape, a.dtype) for a in arrays],
        input_output_aliases={i: i for i in range(n_a)},
        compiler_params=pltpu.CompilerParams(has_side_effects=EFFECT),
    )(*arrays, sems[0], sems[1], *after))


def _slot(buf, chip, half):
    if buf.ndim == 2:
        hc = buf.shape[1] // 2
        return buf.at[pl.ds(pl.multiple_of(chip * IN_SHARD, 16), IN_SHARD), pl.ds(pl.multiple_of(half * hc, HEAD), hc)]
    hr = buf.shape[1] // 2
    return buf.at[chip, pl.ds(pl.multiple_of(half * hr, 16), hr)]


def _ag_descs(bufs, send_sems, recv_sems):
    x, y, c, chips = _place()
    me = 2 * x + y
    out = []
    for w, buf in enumerate(bufs):
        for j, (px, py) in enumerate(chips):
            mk = lambda ref, w=w, j=j, px=px, py=py: pltpu.make_async_remote_copy(
                src_ref=ref, dst_ref=ref, send_sem=send_sems.at[w * 3 + j], recv_sem=recv_sems.at[w * 3 + j],
                device_id=(px, py, c), device_id_type=MESH)
            out.append((mk(_slot(buf, me, c)), mk(_slot(buf, 2 * px + py, c))))
    return out


def _fw_descs(bufs, send_sems, recv_sems):
    x, y, c, chips = _place()
    out = []
    for w, buf in enumerate(bufs):
        for j, (px, py) in enumerate(chips):
            def mk(which, w=w, j=j, buf=buf, px=px, py=py):
                ref = _slot(buf, 2 * px + py, which)
                return pltpu.make_async_remote_copy(
                    src_ref=ref, dst_ref=ref, send_sem=send_sems.at[w * 3 + j], recv_sem=recv_sems.at[w * 3 + j],
                    device_id=(x, y, 1 - c), device_id_type=MESH)
            out.append((mk(c), mk(1 - c)))
    return out


def _sc_descs(refs, send_sems, recv_sems):
    n_w = len(refs) // 2
    x, y, c, chips = _place()
    me = 2 * x + y
    out = []
    for w in range(n_w):
        for j, (px, py) in enumerate(chips):
            d = pltpu.make_async_remote_copy(
                src_ref=refs[w].at[2 * px + py], dst_ref=refs[n_w + w].at[me],
                send_sem=send_sems.at[w * 3 + j], recv_sem=recv_sems.at[w * 3 + j],
                device_id=(px, py, c), device_id_type=MESH)
            out.append((d, d))
    return out


def _pair_descs(src_of):
    def make(refs, send_sems, recv_sems):
        n_w = len(refs) // 2
        x, y, c, _ = _place()
        out = []
        for w in range(n_w):
            d = pltpu.make_async_remote_copy(
                src_ref=src_of(refs[w], c), dst_ref=refs[n_w + w],
                send_sem=send_sems.at[w], recv_sem=recv_sems.at[w],
                device_id=(x, y, 1 - c), device_id_type=MESH)
            out.append((d, d))
        return out
    return make


_EX_DESCS = _pair_descs(lambda g4, c: g4.at[:, 1 - c])
_SW_DESCS = _pair_descs(lambda half, c: half)
_EXT_DESCS = _pair_descs(lambda g, c: g.at[pl.ds(0, IN_COLS),
                                           pl.ds(pl.multiple_of((1 - c) * (D_MODEL // 2), HEAD), D_MODEL // 2)])


def _sm_descs(refs, send_sems, recv_sems):
    buf = refs[0]
    rows8 = buf.shape[0] // N_DEV
    x, y, c, _ = _place()
    flip = lambda v, d: 1 - v if d else v
    blk = lambda px, py, pc: buf.at[pl.ds(pl.multiple_of((4 * px + 2 * py + pc) * rows8, 8), rows8)]
    out = []
    for k in range(1, N_DEV):
        px, py, pc = flip(x, k & 4), flip(y, k & 2), flip(c, k & 1)
        mk = lambda ref, k=k, px=px, py=py, pc=pc: pltpu.make_async_remote_copy(
            src_ref=ref, dst_ref=ref, send_sem=send_sems.at[k - 1], recv_sem=recv_sems.at[k - 1],
            device_id=(px, py, pc), device_id_type=MESH)
        out.append((mk(blk(x, y, c)), mk(blk(px, py, pc))))
    return out


def _place_rows_body(i_ref, x_ref, o_ref):
    o_ref[...] = x_ref[...]


def _place_rows(i_arr, x, n_blocks, name):
    r, n = x.shape
    grid_spec = pltpu.PrefetchScalarGridSpec(
        num_scalar_prefetch=1, grid=(1,),
        in_specs=[pl.BlockSpec((r, n), lambda g, i: (0, 0))],
        out_specs=pl.BlockSpec((r, n), lambda g, i: (i[0], 0)))
    return _pcall(_place_rows_body, name=name, grid_spec=grid_spec,
                  out_shape=jax.ShapeDtypeStruct((n_blocks * r, n), x.dtype))(i_arr, x)


def _ag_forward_body(*refs, n_w):
    bufs = refs[n_w:2 * n_w]
    send_sems, recv_sems = refs[2 * n_w:]
    pairs = _fw_descs(bufs, send_sems, recv_sems)
    for fw, _ in pairs:
        fw.start()
    for fw, back in pairs:
        back.wait_recv()
        fw.wait_send()


def _ag_forward(bufs, tag):
    n_w = len(bufs)
    return list(_pcall(
        functools.partial(_ag_forward_body, n_w=n_w), name="weight_allgather_forward_" + tag,
        in_specs=[ANY] * n_w, out_specs=[ANY] * n_w,
        out_shape=[jax.ShapeDtypeStruct(b.shape, b.dtype) for b in bufs],
        input_output_aliases={w: w for w in range(n_w)},
        scratch_shapes=[pltpu.SemaphoreType.DMA((3 * n_w,))] * 2,
    )(*bufs))


def _pair_add_body(c_ref, mine_ref, theirs_ref, o_ref):
    o_ref[...] = (mine_ref[...].astype(F32) + theirs_ref[...].astype(F32)).astype(o_ref.dtype)


def _pair_add(c_arr, g4, recv, name):
    _, _, hr, cols = g4.shape
    tr = min(hr, 256)
    grid_spec = pltpu.PrefetchScalarGridSpec(
        num_scalar_prefetch=1, grid=(N_CHIPS, hr // tr),
        in_specs=[pl.BlockSpec((None, None, tr, cols), lambda s, i, c: (s, c[0], i, 0)),
                  pl.BlockSpec((None, tr, cols), lambda s, i, c: (s, i, 0))],
        out_specs=pl.BlockSpec((None, tr, cols), lambda s, i, c: (s, i, 0)))
    return _pcall(
        _pair_add_body, name=name, grid_spec=grid_spec,
        out_shape=jax.ShapeDtypeStruct(recv.shape, recv.dtype),
        compiler_params=pltpu.CompilerParams(dimension_semantics=("parallel", "parallel")),
    )(c_arr, g4, recv)


def _pair_add_t(c_arr, g, recv, name):
    rows, hc = recv.shape
    grid_spec = pltpu.PrefetchScalarGridSpec(
        num_scalar_prefetch=1, grid=(rows // IN_TR,),
        in_specs=[pl.BlockSpec((IN_TR, hc), lambda i, c: (i, c[0])), pl.BlockSpec((IN_TR, hc), lambda i, c: (i, 0))],
        out_specs=pl.BlockSpec((IN_TR, hc), lambda i, c: (i, 0)))
    return _pcall(
        _pair_add_body, name=name, grid_spec=grid_spec,
        out_shape=jax.ShapeDtypeStruct(recv.shape, recv.dtype),
        compiler_params=pltpu.CompilerParams(dimension_semantics=("parallel",)),
    )(c_arr, g, recv)


def _sum4_body(me_ref, p_ref, l0, l1, l2, l3, o_ref):
    me = me_ref[0]
    t = [jnp.where(me == j, p_ref[...], l[...]).astype(F32) for j, l in enumerate((l0, l1, l2, l3))]
    o_ref[...] = ((t[0] + t[1]) + t[2]) + t[3]


def _sum4(me_arr, part, landed, name):
    _, hr, cols = part.shape
    tr = IN_TR if hr == IN_SHARD else min(hr, 256)

    def slot(j):
        return lambda i, me: (jnp.where(me[0] == j, (j + 1) % N_CHIPS, j), i, 0)

    grid_spec = pltpu.PrefetchScalarGridSpec(
        num_scalar_prefetch=1, grid=(hr // tr,),
        in_specs=[pl.BlockSpec((None, tr, cols), lambda i, me: (me[0], i, 0))]
        + [pl.BlockSpec((None, tr, cols), slot(j)) for j in range(N_CHIPS)],
        out_specs=pl.BlockSpec((tr, cols), lambda i, me: (i, 0)))
    return _pcall(
        _sum4_body, name=name, grid_spec=grid_spec,
        out_shape=jax.ShapeDtypeStruct((hr, cols), F32),
        compiler_params=pltpu.CompilerParams(dimension_semantics=("parallel",)),
    )(me_arr, part, landed, landed, landed, landed)


def _adamw(w, g, m, v):
    m = ADAM_B1 * m + (1.0 - ADAM_B1) * g
    v = ADAM_B2 * v + (1.0 - ADAM_B2) * (g * g)
    m_hat = m / (1.0 - ADAM_B1 ** ADAM_STEP)
    v_hat = v / (1.0 - ADAM_B2 ** ADAM_STEP)
    delta = -ADAM_LR * (m_hat / (jnp.sqrt(v_hat) + ADAM_EPS) + ADAM_WD * w)
    return delta, m, v


def _adamw_half_body(h_ref, w_ref, g_in_ref, m_ref, v_ref, *rest):
    g_ref, d_ref, nm_ref, nv_ref, done_ref = rest[-5:]
    done_ref[...] = jnp.zeros_like(done_ref)
    g = g_in_ref[...]
    g_ref[...] = g
    d, m, v = _adamw(w_ref[...], g, m_ref[...], v_ref[...])
    d_ref[...] = d
    nm_ref[...] = m
    nv_ref[...] = v


def _adamw_half(h_arr, w, g_half, m, v, prev, name):
    rows, cols = w.shape
    if g_half.shape[0] == rows:
        tr, nh = IN_TR, rows // IN_TR
        at_half = pl.BlockSpec((tr, cols // 2), lambda i, h: (i, h[0]))
        g_spec = pl.BlockSpec((tr, cols // 2), lambda i, h: (i, 0))
    else:
        tr = min(rows // 2, 128)
        nh = (rows // 2) // tr
        at_half = pl.BlockSpec((tr, cols), lambda i, h: (h[0] * nh + i, 0))
        g_spec = pl.BlockSpec((tr, cols), lambda i, h: (i, 0))
    grid_spec = pltpu.PrefetchScalarGridSpec(
        num_scalar_prefetch=1, grid=(nh,),
        in_specs=[at_half, g_spec, at_half, at_half] + [ANY] * len(prev),
        out_specs=[at_half] * 4 + [pl.BlockSpec((8, HEAD), lambda i, h: (0, 0))])
    return list(_pcall(
        _adamw_half_body, name=name, grid_spec=grid_spec,
        out_shape=[jax.ShapeDtypeStruct(w.shape, F32)] * 4 + [jax.ShapeDtypeStruct((8, HEAD), F32)],
        input_output_aliases={5 + k: k for k in range(len(prev))},
        compiler_params=pltpu.CompilerParams(dimension_semantics=("arbitrary",)),
    )(h_arr, w, g_half, m, v, *prev))


def _small_update_body(gath_ref, w_ref, m_ref, v_ref, g_ref, d_ref, nm_ref, nv_ref, loss_ref, *, n_gain):
    tot = gath_ref[0:1, :]
    for i in range(1, gath_ref.shape[0]):
        tot = tot + gath_ref[i:i + 1, :]
    g = tot[:, 0:n_gain]
    g_ref[...] = g
    d, m, v = _adamw(w_ref[...], g, m_ref[...], v_ref[...])
    d_ref[...] = d
    nm_ref[...] = m
    nv_ref[...] = v
    loss_ref[...] = (0.5 / D_MODEL) * jnp.sum(tot[:, n_gain:n_gain + HEAD], axis=1, keepdims=True) * jnp.ones((1, HEAD), F32)


def _small_update(gath, w, m, v):
    n_gain = w.shape[1]
    vm = pl.BlockSpec(memory_space=pltpu.VMEM)
    return _pcall(
        functools.partial(_small_update_body, n_gain=n_gain), name="gain_update",
        in_specs=[vm] * 4, out_specs=[vm] * 5,
        out_shape=[jax.ShapeDtypeStruct((1, n_gain), F32)] * 4 + [jax.ShapeDtypeStruct((1, HEAD), F32)],
    )(gath, w, m, v)


def kernel(x, positions, norm_attn_pre, norm_attn_post, w_in, q_latent_norm, kv_latent_norm, w_uq, w_ukv, w_out, norm_mlp_pre, norm_mlp_post, w_up, w_down, loss_target, m_norm_attn_pre, m_norm_attn_post, m_w_in, m_q_latent_norm, m_kv_latent_norm, m_w_uq, m_w_ukv, m_w_out, m_norm_mlp_pre, m_norm_mlp_post, m_w_up, m_w_down, v_norm_attn_pre, v_norm_attn_post, v_w_in, v_q_latent_norm, v_kv_latent_norm, v_w_uq, v_w_ukv, v_w_out, v_norm_mlp_pre, v_norm_mlp_post, v_w_up, v_w_down):
    T = x.shape[1]
    c_arr = lax.axis_index("c").astype(jnp.int32).reshape(1)
    me_arr = (2 * lax.axis_index("x") + lax.axis_index("y")).astype(jnp.int32).reshape(1)
    names = ["w_in", "w_uq", "w_ukv", "w_out", "w_up", "w_down"]

    transposed = lambda a: jnp.swapaxes(a, 1, 2)
    mats = [transposed(w_in)[0], w_uq[0], w_ukv[0], w_out[0], w_up[0], w_down[0]]
    me8_arr = (4 * lax.axis_index("x") + 2 * lax.axis_index("y") + lax.axis_index("c")).astype(jnp.int32).reshape(1)
    col_major = lambda g: jnp.transpose(g, (1, 0, 2)).reshape(g.shape[1], N_CHIPS * g.shape[2])
    cast = lambda a: a.astype(MXU_DTYPE)
    to_shards = lambda g: jnp.transpose(g.reshape(g.shape[0], N_CHIPS, g.shape[1] // N_CHIPS), (1, 0, 2))
    halved = lambda g: g.reshape(N_CHIPS, 2, g.shape[1] // 2, g.shape[2])
    empty = lambda a, shape=None: lax.empty(a.shape if shape is None else shape, a.dtype)

    sem_in, buf_in, going = _copy_start(_ag_descs, [_cast_place_t(me_arr, mats[0], "cast_w_in")], None,
                                        "weight_allgather_start_in", 3)
    placed = [_cast_place(me_arr, w, "cast_" + n, going) for w, n in zip(mats[1:], names[1:])]
    sem_att, buf_att, going = _copy_start(_ag_descs, placed[:3], going, "weight_allgather_start_attn", 9)
    sem_mlp, buf_mlp, started = _copy_start(_ag_descs, placed[3:], going, "weight_allgather_start_mlp", 6)

    going_on = {}

    def in_weights(after):
        (win_g,) = _ag_forward(_copy_wait(_ag_descs, sem_in, buf_in, after, "weight_allgather_wait_in"), "in")
        return cast(win_g)

    def attn_prefetch(after):
        landed = _copy_wait(_ag_descs, sem_att, buf_att, after, "weight_allgather_wait_attn")
        going_on["fw_attn"] = _copy_start(_fw_descs, landed, None, "weight_allgather_forward_start_attn", 9)
        return going_on["fw_attn"][-1][0:1, 0:1]

    def attn_weights(after):
        sems, bufs, _ = going_on["fw_attn"]
        wuq_g, wukv_g, wout_g = _copy_wait(_fw_descs, sems, bufs, after, "weight_allgather_forward_wait_attn")
        wuq_full = col_major(wuq_g).reshape(LORA, NH, HEAD + ROPE_B)
        w_uq_p = jnp.pad(wuq_full, ((0, 0), (0, 0), (0, QPAD - HEAD - ROPE_B))).reshape(LORA, NH * QPAD)
        w_ukv_p = col_major(wukv_g).reshape(LORA, NH, 2, HEAD).transpose(0, 2, 1, 3).reshape(LORA, 2 * A_W)
        return cast(w_uq_p), cast(w_ukv_p), cast(wout_g.reshape(2 * A_W, D_MODEL))

    def mlp_prefetch(after):
        landed = _copy_wait(_ag_descs, sem_mlp, buf_mlp, after, "weight_allgather_wait_mlp")
        going_on["fw"] = _copy_start(_fw_descs, landed, None, "weight_allgather_forward_start_mlp", 6)
        return going_on["fw"][-1]

    def mlp_weights(after):
        sems, bufs, _ = going_on["fw"]
        wup_g, wdown_g = _copy_wait(_fw_descs, sems, bufs, after, "weight_allgather_forward_wait_mlp")
        return cast(wup_g), cast(wdown_g.reshape(D_FF, D_MODEL))

    def exchange_start(g4s, tag):
        lands = [empty(g, (g.shape[0],) + g.shape[2:]) for g in g4s]
        return _copy_start(_EX_DESCS, g4s + lands, None, "grad_pair_exchange_start_" + tag, len(g4s))

    def exchange_finish(started_ex, after, ns, tag):
        sems, arrs, _ = started_ex
        arrs = _copy_wait(_EX_DESCS, sems, arrs, after, "grad_pair_exchange_wait_" + tag)
        n = len(ns)
        return [_pair_add(c_arr, g4, r, "pair_add_" + nm) for g4, r, nm in zip(arrs[:n], arrs[n:], ns)]

    def scatter_start(parts, after, tag):
        return _copy_start(_sc_descs, parts + [empty(p) for p in parts], after, "grad_scatter_start_" + tag,
                           3 * len(parts))

    def scatter_finish(started_sc, after, tag):
        sems, arrs, _ = started_sc
        arrs = _copy_wait(_sc_descs, sems, arrs, after, "grad_scatter_wait_" + tag)
        return arrs[:len(arrs) // 2], arrs[len(arrs) // 2:]

    def down_grad_ready(gw_down):
        going_on["x_down"] = exchange_start([halved(gw_down.reshape(N_CHIPS, D_MODEL, D_MODEL))], "down")
        return going_on["x_down"][-1]

    def up_grad_ready(gw_up):
        going_on["x_up"] = exchange_start([halved(gw_up)], "up")
        parts = exchange_finish(going_on["x_down"], going_on["x_up"][-1], names[5:], "down")
        going_on["s_down"] = scatter_start(parts, started, "down")
        return going_on["s_down"][-1][0:1, 0:1]

    def attn_grads_ready(gw_out, gw_uq_p, gw_ukv_p):
        gw_uq = to_shards(gw_uq_p.reshape(LORA, NH, QPAD)[:, :, :HEAD + ROPE_B].reshape(LORA, NH * (HEAD + ROPE_B)))
        gw_ukv = to_shards(gw_ukv_p.reshape(LORA, 2, NH, HEAD).transpose(0, 2, 1, 3).reshape(LORA, 2 * A_W))
        full4 = [halved(g) for g in (gw_uq, gw_ukv, gw_out.reshape(N_CHIPS, LORA, D_MODEL))]
        x_attn = exchange_start(full4, "attn")
        parts_up = exchange_finish(going_on["x_up"], x_attn[-1], names[4:5], "up")
        parts = exchange_finish(x_attn, parts_up[0], names[1:4], "attn") + parts_up
        going_on["s_rest"] = scatter_start(parts, going_on["s_down"][-1], "attn_up")
        return going_on["s_rest"][-1][0:1, 0:1]

    dx, gw_proj, small = _local_step(
        x[0], positions[0].astype(F32).reshape(T, 1), loss_target[0],
        norm_attn_pre + started[0:1, 0:1], norm_attn_post, q_latent_norm, kv_latent_norm, norm_mlp_pre, norm_mlp_post,
        in_weights, attn_prefetch, attn_weights, mlp_prefetch, mlp_weights,
        down_grad_ready, up_grad_ready, attn_grads_ready)

    ms = [transposed(m_w_in)[0], m_w_uq[0], m_w_ukv[0], m_w_out[0], m_w_up[0], m_w_down[0]]
    vs = [transposed(v_w_in)[0], v_w_uq[0], v_w_ukv[0], v_w_out[0], v_w_up[0], v_w_down[0]]
    sib_arr = 1 - c_arr

    def finish(parts, landed, lo, hi, tag):
        sl = slice(lo, hi)
        halves = [_sum4(me_arr, p, l, "chip_sum_" + n) for p, l, n in zip(parts, landed, names[sl])]
        n = len(halves)
        sems, arrs, _ = _copy_start(_SW_DESCS, halves + [empty(h) for h in halves], None,
                                    "grad_pair_swap_start_" + tag, n)
        own = [_adamw_half(c_arr, w, g, m, v, [], "adamw_own_" + nm)
               for w, g, m, v, nm in zip(mats[sl], arrs[:n], ms[sl], vs[sl], names[sl])]
        arrs = _copy_wait(_SW_DESCS, sems, arrs, own[-1][4], "grad_pair_swap_wait_" + tag)
        return [_adamw_half(sib_arr, w, g, m, v, prev[:4], "adamw_sib_" + nm)
                for w, g, m, v, prev, nm in zip(mats[sl], arrs[n:], ms[sl], vs[sl], own, names[sl])]

    sem_small, (gath,), small_going = _copy_start(
        _sm_descs, [_place_rows(me8_arr, small, N_DEV, "place_small")], None, "small_allgather_start", N_DEV - 1)
    sems, arrs, _ = _copy_start(_EXT_DESCS, [gw_proj, lax.empty((IN_COLS, D_MODEL // 2), WIRE_DTYPE)], None,
                                "grad_pair_exchange_start_in", 1)
    gw_proj, from_sib = _copy_wait(_EXT_DESCS, sems, arrs, small_going, "grad_pair_exchange_wait_in")
    part_in = _pair_add_t(c_arr, gw_proj, from_sib, "pair_add_w_in").reshape(N_CHIPS, IN_SHARD, D_MODEL // 2)
    s_in = scatter_start([part_in], None, "in")
    parts_rest, landed_rest = scatter_finish(going_on["s_rest"], s_in[-1], "attn_up")
    parts_down, landed_down = scatter_finish(going_on["s_down"], landed_rest[0], "down")
    upd_rest = finish(parts_rest + parts_down, landed_rest + landed_down, 1, 6, "rest")
    parts_in, landed_in = scatter_finish(s_in, upd_rest[-1][0], "in")
    upd = finish(parts_in, landed_in, 0, 1, "in") + upd_rest
    grads = [u[0] for u in upd]

    (gath,) = _copy_wait(_sm_descs, sem_small, [gath], grads[0], "small_allgather_wait")
    gains = [norm_attn_pre, norm_attn_post, q_latent_norm, kv_latent_norm, norm_mlp_pre, norm_mlp_post]
    gm = [m_norm_attn_pre, m_norm_attn_post, m_q_latent_norm, m_kv_latent_norm, m_norm_mlp_pre, m_norm_mlp_post]
    gv = [v_norm_attn_pre, v_norm_attn_post, v_q_latent_norm, v_kv_latent_norm, v_norm_mlp_pre, v_norm_mlp_post]
    cat = lambda xs: jnp.concatenate(xs, axis=1)
    g_s, d_s, m_s, v_s, loss_v = _small_update(gath, cat(gains), cat(gm), cat(gv))
    widths = [a.shape[1] for a in gains]
    offs = [sum(widths[:i]) for i in range(len(widths))]
    split = lambda a: [a[:, o:o + w] for o, w in zip(offs, widths)]
    g_gain, d_gain, m_gain, v_gain = split(g_s), split(d_s), split(m_s), split(v_s)

    def ordered(gain_list, mat_list):
        gl, ml = gain_list, [transposed(mat_list[0][None])] + [a[None] for a in mat_list[1:]]
        return [gl[0], gl[1], ml[0], gl[2], gl[3], ml[1], ml[2], ml[3], gl[4], gl[5], ml[4], ml[5]]

    loss = loss_v[0, 0]
    return (loss, dx[None],
            *ordered(g_gain, grads),
            *ordered(d_gain, [u[1] for u in upd]),
            *ordered(m_gain, [u[2] for u in upd]),
            *ordered(v_gain, [u[3] for u in upd]))
```

```python
import functools

import jax
import jax.numpy as jnp
from jax import lax
from jax.experimental import pallas as pl
from jax.experimental.pallas import tpu as pltpu

F32 = jnp.float32
BF16 = jnp.bfloat16
MXU_DTYPE = jnp.bfloat16
WIRE_DTYPE = jnp.bfloat16

D_MODEL = 2048
HEAD = 128
NH = 8
A_W = NH * HEAD
LORA = 512
ROPE_B = 64
QPAD = 256
MAIN_COLS = 3 * A_W + 2 * LORA
IN_COLS = MAIN_COLS + ROPE_B
PROJ_COLS = MAIN_COLS + HEAD
PROJ_TILE = PROJ_COLS // 3
IN_SHARD = 1040
IN_TR = 208
D_FF = 4 * D_MODEL
DIL = (1, 4, 16)
ROT_A = 32
ROPE_THETA = 500000.0
EPS = 1e-6
NEG = -1e30
N_CHIPS = 4
N_DEV = 8

ADAM_LR = 0.001
ADAM_B1 = 0.9
ADAM_B2 = 0.999
ADAM_EPS = 1e-08
ADAM_WD = 0.01
ADAM_STEP = 10

MESH = pl.DeviceIdType.MESH
ANY = pl.BlockSpec(memory_space=pl.ANY)


def _pcall(body, **kw):
    return pl.pallas_call(body, **kw)


_DIMS = {
    "nn": (((1,), (0,)), ((), ())),
    "nt": (((1,), (1,)), ((), ())),
    "tn": (((0,), (0,)), ((), ())),
}


def _mm_body(*refs, dims, nk, epi, n_extra, n_after, n_out):
    a_ref, b_ref = refs[0], refs[1]
    extra = refs[2:2 + n_extra]
    outs = refs[2 + n_extra + n_after:2 + n_extra + n_after + n_out]
    part = lax.dot_general(a_ref[...], b_ref[...], _DIMS[dims], preferred_element_type=F32)

    def finish(acc):
        res = epi(acc, *[r[...] for r in extra]) if epi is not None else (acc,)
        for o_ref, o in zip(outs, res):
            o_ref[...] = o.astype(o_ref.dtype)

    if nk == 1:
        finish(part)
        return
    acc_ref = refs[-1]
    k = pl.program_id(2)

    @pl.when(k == 0)
    def _():
        acc_ref[...] = part

    @pl.when(k > 0)
    def _():
        acc_ref[...] += part

    @pl.when(k == nk - 1)
    def _():
        finish(acc_ref[...])


def _matmul(a, b, *, dims, out_dtypes, tm, tn, tk, name, epi=None, extras=(), row_extras=(), b_outer=False,
            b_shards=0, out_shards=0, after=None):
    if b_shards:
        assert dims in ("nn", "nt") and b.shape[0] == b_shards
        b2 = (b.shape[1], b_shards * b.shape[2])
    else:
        b2 = b.shape
    if dims == "nn":
        (M, K), (K2, N) = a.shape, b2
    elif dims == "nt":
        (M, K), (N, K2) = a.shape, b2
    else:
        (K, M), (K2, N) = a.shape, b2
    assert K == K2, (a.shape, b.shape, dims)
    tm, tn, tk = min(tm, M), min(tn, N), min(tk, K)
    assert M % tm == 0 and N % tn == 0 and K % tk == 0, (name, M, N, K, tm, tn, tk)
    nk = K // tk

    def at(f):
        if b_outer:
            return lambda j, i, k: f(i, j, k)
        return f

    a_spec = {"nn": pl.BlockSpec((tm, tk), at(lambda i, j, k: (i, k))),
              "nt": pl.BlockSpec((tm, tk), at(lambda i, j, k: (i, k))),
              "tn": pl.BlockSpec((tk, tm), at(lambda i, j, k: (k, i)))}[dims]
    b_spec = {"nn": pl.BlockSpec((tk, tn), at(lambda i, j, k: (k, j))),
              "nt": pl.BlockSpec((tn, tk), at(lambda i, j, k: (j, k))),
              "tn": pl.BlockSpec((tk, tn), at(lambda i, j, k: (k, j)))}[dims]
    if b_shards:
        per = b.shape[2] // (tn if dims == "nn" else tk)
        assert per >= 1 and b.shape[2] % (tn if dims == "nn" else tk) == 0
        b_spec = {"nn": pl.BlockSpec((None, tk, tn), at(lambda i, j, k: (j // per, k, j % per))),
                  "nt": pl.BlockSpec((None, tn, tk), at(lambda i, j, k: (k // per, j, k % per)))}[dims]
    o_spec = pl.BlockSpec((tm, tn), at(lambda i, j, k: (i, j)))
    o_shape = (M, N)
    if out_shards:
        assert not extras and N % out_shards == 0 and (N // out_shards) % tn == 0
        o_per = (N // out_shards) // tn
        o_spec = pl.BlockSpec((None, tm, tn), at(lambda i, j, k: (j // o_per, i, j % o_per)))
        o_shape = (out_shards, M, N // out_shards)
    r_specs = [pl.BlockSpec((tm, r.shape[1]), at(lambda i, j, k: (i, 0))) for r in row_extras]
    after = [] if after is None else [after]
    body = functools.partial(_mm_body, dims=dims, nk=nk, epi=epi, n_extra=len(extras) + len(row_extras),
                             n_after=len(after), n_out=len(out_dtypes))
    res = _pcall(
        body, name=name,
        grid=(N // tn, M // tm, nk) if b_outer else (M // tm, N // tn, nk),
        in_specs=[a_spec, b_spec] + [o_spec] * len(extras) + r_specs + [ANY] * len(after),
        out_specs=[o_spec] * len(out_dtypes),
        out_shape=[jax.ShapeDtypeStruct(o_shape, dt) for dt in out_dtypes],
        scratch_shapes=[pltpu.VMEM((tm, tn), F32)] if nk > 1 else [],
        compiler_params=pltpu.CompilerParams(
            dimension_semantics=("parallel", "parallel", "arbitrary")),
    )(a, b, *extras, *row_extras, *after)
    return list(res)


def _rowwise(body, row_ins, vec_ins, row_outs, acc_outs, *, tr, name):
    T = row_ins[0].shape[0]
    tr = min(tr, T)
    assert T % tr == 0
    in_specs = [pl.BlockSpec((tr, a.shape[1]), lambda i: (i, 0)) for a in row_ins]
    in_specs += [pl.BlockSpec(a.shape, lambda i: (0, 0)) for a in vec_ins]
    out_specs = [pl.BlockSpec((tr, w), lambda i: (i, 0)) for (w, _) in row_outs]
    out_specs += [pl.BlockSpec(s, lambda i: (0, 0)) for s in acc_outs]
    out_shape = [jax.ShapeDtypeStruct((T, w), dt) for (w, dt) in row_outs]
    out_shape += [jax.ShapeDtypeStruct(s, F32) for s in acc_outs]
    sem = "arbitrary" if acc_outs else "parallel"
    return list(_pcall(
        body, name=name, grid=(T // tr,), in_specs=in_specs, out_specs=out_specs,
        out_shape=out_shape,
        compiler_params=pltpu.CompilerParams(dimension_semantics=(sem,)),
    )(*row_ins, *vec_ins))


def _rstd(x):
    return lax.rsqrt(jnp.mean(x * x, axis=-1, keepdims=True) + EPS)


def _rms_bwd(x, rstd, dyg):
    xh = x * rstd
    return rstd * (dyg - xh * jnp.mean(dyg * xh, axis=-1, keepdims=True)), xh


def _fold8(v):
    r, w = v.shape
    return jnp.sum(v.reshape(r // 8, 8, w), axis=0)


def _acc(ref, val):
    first = pl.program_id(0) == 0

    @pl.when(first)
    def _():
        ref[...] = val

    @pl.when(jnp.logical_not(first))
    def _():
        ref[...] += val


def _rope(x, c, sa, sb, half):
    return x * c + pltpu.roll(x, HEAD - half, 1) * sa + pltpu.roll(x, half, 1) * sb


def _rope_t(dy, c, sa, sb, half):
    return dy * c - pltpu.roll(dy, HEAD - half, 1) * sa - pltpu.roll(dy, half, 1) * sb


def _rope_tab_body(pos_ref, inv_ref, ca, saa, sab, cb, sba, sbb):
    pos = pos_ref[...]
    lane = lax.broadcasted_iota(jnp.int32, (pos.shape[0], HEAD), 1)
    ang_a = pos * inv_ref[0:1, :]
    ang_b = pos * inv_ref[1:2, :]
    c, s = jnp.cos(ang_a), jnp.sin(ang_a)
    ha = ROT_A // 2
    ca[...] = jnp.where(lane < ROT_A, c, 1.0)
    saa[...] = jnp.where(lane < ha, -s, 0.0)
    sab[...] = jnp.where((lane >= ha) & (lane < ROT_A), s, 0.0)
    c, s = jnp.cos(ang_b), jnp.sin(ang_b)
    hb = ROPE_B // 2
    cb[...] = jnp.where(lane < ROPE_B, c, 1.0)
    sba[...] = jnp.where(lane < hb, -s, 0.0)
    sbb[...] = jnp.where((lane >= hb) & (lane < ROPE_B), s, 0.0)


def _rms_fwd_body(x_ref, g_ref, h_ref):
    x = x_ref[...]
    h_ref[...] = ((x * _rstd(x)) * g_ref[...]).astype(h_ref.dtype)


def _postproj_body(p_ref, ca, saa, sab, cb, sba, sbb, gq_ref, gkv_ref,
                   q_ref, k_ref, v_ref, cqn_ref, ckvn_ref, krope_ref):
    c, sa, sb = ca[...], saa[...], sab[...]
    for h in range(NH):
        lo = h * HEAD
        q_ref[:, lo:lo + HEAD] = _rope(p_ref[:, lo:lo + HEAD], c, sa, sb, ROT_A // 2).astype(q_ref.dtype)
        k_ref[:, lo:lo + HEAD] = _rope(p_ref[:, A_W + lo:A_W + lo + HEAD], c, sa, sb, ROT_A // 2).astype(k_ref.dtype)
    v_ref[...] = p_ref[:, 2 * A_W:3 * A_W].astype(v_ref.dtype)
    cq = p_ref[:, 3 * A_W:3 * A_W + LORA]
    cqn_ref[...] = ((cq * _rstd(cq)) * gq_ref[...]).astype(cqn_ref.dtype)
    ckv = p_ref[:, 3 * A_W + LORA:MAIN_COLS]
    ckvn_ref[...] = ((ckv * _rstd(ckv)) * gkv_ref[...]).astype(ckvn_ref.dtype)
    krope_ref[...] = _rope(p_ref[:, MAIN_COLS:PROJ_COLS], cb[...], sba[...], sbb[...], ROPE_B // 2).astype(krope_ref.dtype)


def _mid_body(x_ref, o_ref, g2_ref, g3_ref, x1_ref, h2_ref):
    o = o_ref[...]
    x1 = x_ref[...] + (o * _rstd(o)) * g2_ref[...]
    x1_ref[...] = x1
    h2_ref[...] = ((x1 * _rstd(x1)) * g3_ref[...]).astype(h2_ref.dtype)


def _loss_body(x1_ref, d_ref, t_ref, g4_ref, dy_ref, dd_ref, loss_ref, dg4_ref):
    d = d_ref[...]
    rstd = _rstd(d)
    y = x1_ref[...] + (d * rstd) * g4_ref[...]
    e = y - t_ref[...]
    dy = e * (1.0 / D_MODEL)
    dy_ref[...] = dy
    dd, dh = _rms_bwd(d, rstd, dy * g4_ref[...])
    dd_ref[...] = dd.astype(dd_ref.dtype)
    _acc(dg4_ref, _fold8(dy * dh))
    e8 = _fold8(e * e)
    l = e8[:, 0:HEAD]
    for j in range(1, D_MODEL // HEAD):
        l = l + e8[:, j * HEAD:(j + 1) * HEAD]
    _acc(loss_ref, l)


def _bmid_body(dy_ref, dh2_ref, x1_ref, o_ref, g2_ref, g3_ref, dx1_ref, do_ref, dg3_ref, dg2_ref):
    x1 = x1_ref[...]
    dh2 = dh2_ref[...]
    dn, x1h = _rms_bwd(x1, _rstd(x1), dh2 * g3_ref[...])
    dx1 = dy_ref[...] + dn
    dx1_ref[...] = dx1
    _acc(dg3_ref, _fold8(dh2 * x1h))
    o = o_ref[...]
    do, oh = _rms_bwd(o, _rstd(o), dx1 * g2_ref[...])
    do_ref[...] = do.astype(do_ref.dtype)
    _acc(dg2_ref, _fold8(dx1 * oh))


def _dproj_body(dq_ref, dk_ref, dv_ref, dcq_ref, dckv_ref, p_ref, dkr_ref,
                ca, saa, sab, cb, sba, sbb, gq_ref, gkv_ref,
                dp_ref, dgq_ref, dgkv_ref):
    c, sa, sb = ca[...], saa[...], sab[...]
    for h in range(NH):
        lo = h * HEAD
        dp_ref[:, lo:lo + HEAD] = _rope_t(dq_ref[:, lo:lo + HEAD], c, sa, sb, ROT_A // 2).astype(dp_ref.dtype)
        dp_ref[:, A_W + lo:A_W + lo + HEAD] = _rope_t(dk_ref[:, lo:lo + HEAD], c, sa, sb, ROT_A // 2).astype(dp_ref.dtype)
    dp_ref[:, 2 * A_W:3 * A_W] = dv_ref[...].astype(dp_ref.dtype)
    cq = p_ref[:, 3 * A_W:3 * A_W + LORA]
    dcqn = dcq_ref[...]
    dcq, cqh = _rms_bwd(cq, _rstd(cq), dcqn * gq_ref[...])
    dp_ref[:, 3 * A_W:3 * A_W + LORA] = dcq.astype(dp_ref.dtype)
    _acc(dgq_ref, _fold8(dcqn * cqh))
    ckv = p_ref[:, 3 * A_W + LORA:MAIN_COLS]
    dckvn = dckv_ref[...]
    dckv, ckvh = _rms_bwd(ckv, _rstd(ckv), dckvn * gkv_ref[...])
    dp_ref[:, 3 * A_W + LORA:MAIN_COLS] = dckv.astype(dp_ref.dtype)
    _acc(dgkv_ref, _fold8(dckvn * ckvh))
    dkr = dkr_ref[:, 0:HEAD]
    for h in range(1, NH):
        dkr = dkr + dkr_ref[:, h * HEAD:(h + 1) * HEAD]
    dp_ref[:, MAIN_COLS:PROJ_COLS] = _rope_t(dkr, cb[...], sba[...], sbb[...], ROPE_B // 2).astype(dp_ref.dtype)


def _bin_body(dx1_ref, dh_ref, x_ref, g1_ref, dx_ref, dg1_ref):
    x = x_ref[...]
    dh = dh_ref[...]
    dn, xh = _rms_bwd(x, _rstd(x), dh * g1_ref[...])
    dx_ref[...] = dx1_ref[...] + dn
    _acc(dg1_ref, _fold8(dh * xh))


def _dot_nt(a, b):
    return lax.dot_general(a, b, _DIMS["nt"], preferred_element_type=F32)


def _dot_tn(a, b):
    return lax.dot_general(a, b, _DIMS["tn"], preferred_element_type=F32)


def _dot_nn(a, b):
    return jnp.dot(a, b, preferred_element_type=F32)


DIL_SCALE = HEAD ** -0.5
DIL_CHUNK = 256


def _dil_rows(t, d):
    r = t & (d - 1)
    n = t >> (d.bit_length() - 1)
    start = r + n * (HEAD * d)
    has_prev = n > 0
    pstart = jnp.where(has_prev, start - HEAD * d, start)
    if d == 1:
        return pl.ds(pl.multiple_of(start, HEAD), HEAD), pl.ds(pl.multiple_of(pstart, HEAD), HEAD), has_prev
    return pl.ds(start, HEAD, stride=d), pl.ds(pstart, HEAD, stride=d), has_prev


def _dil_band():
    row = lax.broadcasted_iota(jnp.int32, (HEAD, 2 * HEAD), 0)
    col = lax.broadcasted_iota(jnp.int32, (HEAD, 2 * HEAD), 1)
    return (col >= row) & (col <= row + HEAD), col >= HEAD


def _dil_fwd_body(q_ref, k_ref, v_ref, a_ref, lse_ref, o1, o2, o3, l1, l2, l3, *, nt, unroll):
    band, is_cur = _dil_band()
    for d, o_sc, l_sc in zip(DIL, (o1, o2, o3), (l1, l2, l3)):

        def tile(t, carry, d=d, o_sc=o_sc, l_sc=l_sc):
            rows, prows, has_prev = _dil_rows(t, d)
            q = q_ref[rows, :].astype(MXU_DTYPE)
            kk = jnp.concatenate([k_ref[prows, :], k_ref[rows, :]], axis=0).astype(MXU_DTYPE)
            vv = jnp.concatenate([v_ref[prows, :], v_ref[rows, :]], axis=0).astype(MXU_DTYPE)
            ok = band & (is_cur | has_prev)
            s = jnp.where(ok, _dot_nt(q, kk) * DIL_SCALE, NEG)
            m = jnp.max(s, axis=1, keepdims=True)
            p = jnp.exp(s - m)
            den = jnp.sum(p, axis=1, keepdims=True)
            o_sc[rows, :] = _dot_nn((p / den).astype(MXU_DTYPE), vv)
            l_sc[rows, :] = jnp.broadcast_to(m + jnp.log(den), (HEAD, HEAD))
            return carry

        lax.fori_loop(0, nt, tile, 0, unroll=unroll)

    def merge(i, carry):
        rs = pl.ds(pl.multiple_of(i * DIL_CHUNK, DIL_CHUNK), DIL_CHUNK)
        la, lb, lc = l1[rs, :], l2[rs, :], l3[rs, :]
        m = jnp.maximum(jnp.maximum(la, lb), lc)
        wa, wb, wc = jnp.exp(la - m), jnp.exp(lb - m), jnp.exp(lc - m)
        den = wa + wb + wc
        a = (wa / den) * o1[rs, :] + (wb / den) * o2[rs, :] + (wc / den) * o3[rs, :]
        a_ref[rs, :] = a.astype(a_ref.dtype)
        lse_ref[rs, :] = m + jnp.log(den)
        return carry

    lax.fori_loop(0, q_ref.shape[0] // DIL_CHUNK, merge, 0)


def _dil_fwd(q, k, v):
    T = q.shape[0]
    spec = pl.BlockSpec((T, HEAD), lambda h: (0, h))
    return _pcall(
        functools.partial(_dil_fwd_body, nt=T // HEAD, unroll=16), name="dil_fwd",
        grid=(NH,), in_specs=[spec] * 3, out_specs=[spec] * 2,
        out_shape=[jax.ShapeDtypeStruct((T, 2 * A_W), MXU_DTYPE), jax.ShapeDtypeStruct((T, A_W), F32)],
        scratch_shapes=[pltpu.VMEM((T, HEAD), F32)] * 6,
        compiler_params=pltpu.CompilerParams(dimension_semantics=("parallel",)),
    )(q, k, v)


def _dil_bwd_body(q_ref, k_ref, v_ref, do_ref, a_ref, lse_ref, dq_ref, dk_ref, dv_ref, dl_sc, *, nt, unroll):
    band, is_cur = _dil_band()

    def prep(i, carry):
        rs = pl.ds(pl.multiple_of(i * DIL_CHUNK, DIL_CHUNK), DIL_CHUNK)
        dl = jnp.sum(do_ref[rs, :] * a_ref[rs, :].astype(F32), axis=1, keepdims=True)
        dl_sc[rs, :] = jnp.broadcast_to(dl, (DIL_CHUNK, HEAD))
        zero = jnp.zeros((DIL_CHUNK, HEAD), F32)
        dq_ref[rs, :] = zero
        dk_ref[rs, :] = zero
        dv_ref[rs, :] = zero
        return carry

    lax.fori_loop(0, q_ref.shape[0] // DIL_CHUNK, prep, 0)

    for d in DIL:

        def tile(t, carry, d=d):
            rows, prows, has_prev = _dil_rows(t, d)
            q = q_ref[rows, :].astype(MXU_DTYPE)
            kk = jnp.concatenate([k_ref[prows, :], k_ref[rows, :]], axis=0).astype(MXU_DTYPE)
            vv = jnp.concatenate([v_ref[prows, :], v_ref[rows, :]], axis=0).astype(MXU_DTYPE)
            do = do_ref[rows, :].astype(MXU_DTYPE)
            lse = lse_ref[rows, :]
            dl = dl_sc[rows, :]
            ok = band & (is_cur | has_prev)
            s = _dot_nt(q, kk) * DIL_SCALE
            p = jnp.where(ok, jnp.exp(s - jnp.concatenate([lse, lse], axis=1)), 0.0)
            ds = (p * (_dot_nt(do, vv) - jnp.concatenate([dl, dl], axis=1))).astype(MXU_DTYPE)
            dq_ref[rows, :] += _dot_nn(ds, kk) * DIL_SCALE
            dkk = _dot_tn(ds, q) * DIL_SCALE
            dvv = _dot_tn(p.astype(MXU_DTYPE), do)
            dk_ref[rows, :] += dkk[HEAD:, :]
            dv_ref[rows, :] += dvv[HEAD:, :]
            dk_ref[prows, :] += dkk[:HEAD, :]
            dv_ref[prows, :] += dvv[:HEAD, :]
            return carry

        lax.fori_loop(0, nt, tile, 0, unroll=unroll)


def _dil_bwd(q, k, v, dmix, mixed, lse):
    T = q.shape[0]
    spec = pl.BlockSpec((T, HEAD), lambda h: (0, h))
    return _pcall(
        functools.partial(_dil_bwd_body, nt=T // HEAD, unroll=8), name="dil_bwd",
        grid=(NH,), in_specs=[spec] * 6, out_specs=[spec] * 3,
        out_shape=[jax.ShapeDtypeStruct((T, A_W), F32)] * 3,
        scratch_shapes=[pltpu.VMEM((T, HEAD), F32)],
        compiler_params=pltpu.CompilerParams(dimension_semantics=("parallel",)),
    )(q, k, v, dmix, mixed, lse)


MLA_SCALE = (HEAD + ROPE_B) ** -0.5
LOG2E = 1.4426950408889634
MLA_QSCALE = MLA_SCALE * LOG2E
MLA_T = 512
MLA_HP = 4


def _tri(t):
    row = lax.broadcasted_iota(jnp.int32, (t, t), 0)
    col = lax.broadcasted_iota(jnp.int32, (t, t), 1)
    return col <= row


def _lanes(x, n):
    return jnp.tile(x, (1, n // HEAD))


def _mla_fwd_body(q_ref, kv_ref, kr_ref, mixed_ref, o_ref, lse_ref, m_sc, l_sc, acc_sc, *, t, hp):
    del mixed_ref
    qi = pl.program_id(1)
    m_sc[...] = jnp.full(m_sc.shape, NEG, F32)
    l_sc[...] = jnp.zeros(l_sc.shape, F32)
    acc_sc[...] = jnp.zeros(acc_sc.shape, F32)

    def step(j, masked):
        ks = pl.ds(pl.multiple_of(j * t, t), t)
        kr = kr_ref[ks, :]
        logits = []
        for hh in range(hp):
            kcat = jnp.concatenate([kv_ref[ks, 2 * hh * HEAD:(2 * hh + 1) * HEAD], kr], axis=1)
            logits.append(_dot_nt(q_ref[:, hh * QPAD:(hh + 1) * QPAD], kcat))
        for hh in range(hp):
            s = logits[hh]
            if masked:
                s = jnp.where(_tri(t), s, NEG)
            m_prev = m_sc[hh]
            m_new = jnp.maximum(m_prev, jnp.max(s, axis=1, keepdims=True))
            alpha = jnp.exp2(m_prev - m_new)
            p = jnp.exp2(s - _lanes(m_new, t))
            l_sc[hh] = alpha * l_sc[hh] + jnp.sum(p, axis=1, keepdims=True)
            acc_sc[hh] = alpha * acc_sc[hh] + _dot_nn(p.astype(MXU_DTYPE), kv_ref[ks, (2 * hh + 1) * HEAD:(2 * hh + 2) * HEAD])
            m_sc[hh] = m_new

    def off_diag(j, carry):
        step(j, False)
        return carry

    lax.fori_loop(0, qi, off_diag, 0)
    step(qi, True)
    for hh in range(hp):
        l = l_sc[hh]
        o_ref[:, hh * HEAD:(hh + 1) * HEAD] = (acc_sc[hh] / l).astype(o_ref.dtype)
        lse_ref[:, hh * HEAD:(hh + 1) * HEAD] = m_sc[hh] + jnp.log2(l)


def _mla_fwd(qf, kv, kr, mixed):
    T = qf.shape[0]
    t, hp = min(MLA_T, T), MLA_HP
    ng = NH // hp
    return _pcall(
        functools.partial(_mla_fwd_body, t=t, hp=hp), name="mla_fwd",
        grid=(ng, T // t),
        in_specs=[pl.BlockSpec((t, hp * QPAD), lambda g, i: (i, g)),
                  pl.BlockSpec((T, hp * 2 * HEAD), lambda g, i: (0, g)),
                  pl.BlockSpec((T, HEAD), lambda g, i: (0, 0)), ANY],
        out_specs=[pl.BlockSpec((t, hp * HEAD), lambda g, i: (i, ng + g)),
                   pl.BlockSpec((t, hp * HEAD), lambda g, i: (i, g))],
        out_shape=[jax.ShapeDtypeStruct(mixed.shape, mixed.dtype), jax.ShapeDtypeStruct((T, A_W), F32)],
        input_output_aliases={3: 0},
        scratch_shapes=[pltpu.VMEM((hp, t, HEAD), F32)] * 3,
        compiler_params=pltpu.CompilerParams(dimension_semantics=("parallel", "parallel")),
    )(qf, kv, kr, mixed)


def _mla_bwd_body(q_ref, kn_ref, kr_ref, v_ref, do_ref, o_ref, lse_ref, cb, sba, sbb,
                  dq_ref, dkv_ref, dkr_ref, dq_sc, dl_sc, dk_sc, dv_sc, *, t):
    ki = pl.program_id(1)
    nq = q_ref.shape[0] // t

    @pl.when(ki == 0)
    def _():
        def prep(i, carry):
            rs = pl.ds(pl.multiple_of(i * t, t), t)
            dl = jnp.sum(do_ref[rs, :] * o_ref[rs, :].astype(F32), axis=1, keepdims=True)
            dl_sc[rs, :] = jnp.broadcast_to(dl, (t, HEAD))
            dq_sc[rs, :] = jnp.zeros((t, QPAD), F32)
            return carry
        lax.fori_loop(0, nq, prep, 0)

    kcat = jnp.concatenate([kn_ref[...], kr_ref[...]], axis=1)
    v = v_ref[...]
    dk_sc[...] = jnp.zeros(dk_sc.shape, F32)
    dv_sc[...] = jnp.zeros(dv_sc.shape, F32)

    def step(i, masked):
        qs = pl.ds(pl.multiple_of(i * t, t), t)
        q = q_ref[qs, :]
        do = do_ref[qs, :].astype(MXU_DTYPE)
        s = _dot_nt(q, kcat)
        dp = _dot_nt(do, v)
        p = jnp.exp2(s - _lanes(lse_ref[qs, :], t))
        if masked:
            p = jnp.where(_tri(t), p, 0.0)
        ds = (p * (dp - _lanes(dl_sc[qs, :], t))).astype(MXU_DTYPE)
        dv_sc[...] += _dot_tn(p.astype(MXU_DTYPE), do)
        dk_sc[...] += _dot_tn(ds, q)
        dq_sc[qs, :] += _dot_nn(ds, kcat) * MLA_SCALE

    step(ki, True)

    def off_diag(i, carry):
        step(i, False)
        return carry

    lax.fori_loop(ki + 1, nq, off_diag, 0)
    dk = dk_sc[...] * (1.0 / LOG2E)
    dkv_ref[:, 0:HEAD] = dk[:, 0:HEAD].astype(dkv_ref.dtype)
    dkr_ref[...] = dk[:, HEAD:QPAD]
    dkv_ref[:, HEAD:] = dv_sc[...].astype(dkv_ref.dtype)

    @pl.when(ki == nq - 1)
    def _():
        def emit(i, carry):
            rs = pl.ds(pl.multiple_of(i * t, t), t)
            dq_ref[rs, 0:HEAD] = dq_sc[rs, 0:HEAD].astype(dq_ref.dtype)
            dq_ref[rs, HEAD:QPAD] = _rope_t(dq_sc[rs, HEAD:QPAD], cb[rs, :], sba[rs, :], sbb[rs, :],
                                            ROPE_B // 2).astype(dq_ref.dtype)
            return carry
        lax.fori_loop(0, nq, emit, 0)


def _mla_bwd(qf, kv, kr, dmix, mixed, lse, tabs_b):
    T = qf.shape[0]
    t = min(MLA_T, T)
    head = lambda h, j: (0, h)
    b_half = lambda h, j: (0, NH + h)
    kblk = pl.BlockSpec((t, HEAD), lambda h, j: (j, h))
    return _pcall(
        functools.partial(_mla_bwd_body, t=t), name="mla_bwd",
        grid=(NH, T // t),
        in_specs=[pl.BlockSpec((T, QPAD), head), pl.BlockSpec((t, HEAD), lambda h, j: (j, 2 * h)),
                  pl.BlockSpec((t, HEAD), lambda h, j: (j, 0)),
                  pl.BlockSpec((t, HEAD), lambda h, j: (j, 2 * h + 1)),
                  pl.BlockSpec((T, HEAD), b_half), pl.BlockSpec((T, HEAD), b_half),
                  pl.BlockSpec((T, HEAD), head)] + [pl.BlockSpec((T, HEAD), lambda h, j: (0, 0))] * 3,
        out_specs=[pl.BlockSpec((T, QPAD), head), pl.BlockSpec((t, 2 * HEAD), lambda h, j: (j, h)), kblk],
        out_shape=[jax.ShapeDtypeStruct((T, NH * QPAD), MXU_DTYPE), jax.ShapeDtypeStruct((T, 2 * A_W), MXU_DTYPE),
                   jax.ShapeDtypeStruct((T, A_W), F32)],
        scratch_shapes=[pltpu.VMEM((T, QPAD), F32), pltpu.VMEM((T, HEAD), F32), pltpu.VMEM((t, QPAD), F32),
                        pltpu.VMEM((t, HEAD), F32)],
        compiler_params=pltpu.CompilerParams(dimension_semantics=("parallel", "arbitrary")),
    )(qf, kv, kr, kv, dmix, mixed, lse, *tabs_b)


def _local_step(x, pos, target, g1, g2, gq, gkv, g3, g4,
                in_weights, attn_prefetch, attn_weights, mlp_prefetch, mlp_weights,
                down_grad_ready, up_grad_ready, attn_grads_ready):
    T = x.shape[0]
    TR = 256
    mm = functools.partial(_matmul, tm=2048, tn=1024, tk=2048, b_outer=True)
    mm_k = functools.partial(_matmul, tm=1024, tn=1024, tk=2048)
    mm_g = functools.partial(_matmul, tm=1024, tn=1024, tk=4096, b_outer=True)

    inv_a = ROPE_THETA ** (-jnp.arange(0, ROT_A, 2, dtype=F32) / ROT_A)
    inv_b = ROPE_THETA ** (-jnp.arange(0, ROPE_B, 2, dtype=F32) / ROPE_B)
    inv = jnp.stack([jnp.concatenate([inv_a, inv_a, jnp.zeros((HEAD - ROT_A,), F32)]),
                     jnp.concatenate([inv_b, inv_b, jnp.zeros((HEAD - ROPE_B,), F32)])])
    inv = jnp.concatenate([inv, jnp.zeros((6, HEAD), F32)], axis=0)
    tabs = _rowwise(_rope_tab_body, [pos], [inv], [(HEAD, F32)] * 6, [], tr=512, name="rope_tables")

    (h,) = _rowwise(_rms_fwd_body, [x], [g1], [(D_MODEL, MXU_DTYPE)], [], tr=TR, name="rms_in")
    w_proj = in_weights([h, tabs[0]])
    (proj,) = mm(h, w_proj, dims="nt", out_dtypes=[F32], tm=1024, tn=PROJ_TILE, name="proj_in")
    gq = gq + attn_prefetch(proj)
    q, k, v, cqn, ckvn, krope = _rowwise(
        _postproj_body, [proj] + tabs, [gq, gkv],
        [(A_W, F32)] * 3 + [(LORA, MXU_DTYPE)] * 2 + [(HEAD, MXU_DTYPE)], [], tr=TR, name="post_proj")
    mixed, lse_a = _dil_fwd(q, k, v)

    w_uq_p, w_ukv, w_out = attn_weights(cqn)

    def q_epi(acc, cb, sba, sbb):
        cols = []
        for hh in range(acc.shape[1] // QPAD):
            lo = hh * QPAD
            cols += [acc[:, lo:lo + HEAD], _rope(acc[:, lo + HEAD:lo + QPAD], cb, sba, sbb, ROPE_B // 2)]
        return (jnp.concatenate(cols, axis=1) * MLA_QSCALE,)
    (qf,) = mm(cqn, w_uq_p, dims="nn", out_dtypes=[MXU_DTYPE], name="q_up", epi=q_epi, row_extras=tuple(tabs[3:]))
    (kv,) = mm(ckvn, w_ukv, dims="nn", out_dtypes=[MXU_DTYPE], name="kv_up")
    mixed, lse_b = _mla_fwd(qf, kv, krope, mixed)
    (o,) = mm(mixed, w_out, dims="nn", out_dtypes=[F32], name="out_proj", after=mlp_prefetch(mixed))
    x1, h2 = _rowwise(_mid_body, [x, o], [g2, g3], [(D_MODEL, F32), (D_MODEL, MXU_DTYPE)], [], tr=TR, name="mid_norm")

    w_up, w_down = mlp_weights(h2)

    def up_epi(acc):
        r = jnp.maximum(acc, 0.0)
        return r * r, r
    u, r = mm(h2, w_up, dims="nn", out_dtypes=[MXU_DTYPE, MXU_DTYPE], name="mlp_up", epi=up_epi, b_shards=N_CHIPS)
    (dn,) = mm_k(u, w_down, dims="nn", out_dtypes=[F32], name="mlp_down")
    dy, dd, loss8, dg4 = _rowwise(_loss_body, [x1, dn, target], [g4], [(D_MODEL, F32), (D_MODEL, MXU_DTYPE)],
                                  [(8, HEAD), (8, D_MODEL)], tr=TR, name="loss_head")

    def dup_epi(acc, rr):
        return (acc * (2.0 * rr.astype(F32)),)
    (dup,) = mm(dd, w_down, dims="nt", out_dtypes=[MXU_DTYPE], name="d_up", epi=dup_epi, extras=(r,))
    (gw_down,) = mm_g(u, dd, dims="tn", out_dtypes=[WIRE_DTYPE], name="gw_down")
    (dh2,) = mm_k(dup, w_up, dims="nt", out_dtypes=[F32], name="d_h2", b_shards=N_CHIPS,
                  after=down_grad_ready(gw_down))
    (gw_up,) = mm_g(h2, dup, dims="tn", out_dtypes=[WIRE_DTYPE], name="gw_up", out_shards=N_CHIPS)
    g2 = g2 + up_grad_ready(gw_up)
    dx1, do, dg3, dg2 = _rowwise(_bmid_body, [dy, dh2, x1, o], [g2, g3], [(D_MODEL, F32), (D_MODEL, MXU_DTYPE)],
                                 [(8, D_MODEL), (8, D_MODEL)], tr=TR, name="bwd_mid")
    (dmix,) = mm(do, w_out, dims="nt", out_dtypes=[F32], name="d_mixed")
    (gw_out,) = mm_g(mixed, do, dims="tn", out_dtypes=[WIRE_DTYPE], name="gw_out")

    dq_pad, dkv, dkr = _mla_bwd(qf, kv, krope, dmix, mixed, lse_b, tabs[3:])
    (dcqn,) = mm(dq_pad, w_uq_p, dims="nt", out_dtypes=[F32], name="d_cq")
    (gw_uq_p,) = mm_g(cqn, dq_pad, dims="tn", out_dtypes=[WIRE_DTYPE], name="gw_uq")
    (dckvn,) = mm(dkv, w_ukv, dims="nt", out_dtypes=[F32], name="d_ckv")
    (gw_ukv,) = mm_g(ckvn, dkv, dims="tn", out_dtypes=[WIRE_DTYPE], name="gw_ukv")
    gq = gq + attn_grads_ready(gw_out, gw_uq_p, gw_ukv)

    dq_a, dk_a, dv_a = _dil_bwd(q, k, v, dmix, mixed, lse_a)
    dproj, dgq, dgkv = _rowwise(
        _dproj_body, [dq_a, dk_a, dv_a, dcqn, dckvn, proj, dkr] + tabs, [gq, gkv],
        [(PROJ_COLS, MXU_DTYPE)], [(8, LORA), (8, LORA)], tr=TR, name="d_proj")
    (dh,) = mm_k(dproj, w_proj, dims="nn", out_dtypes=[F32], tk=PROJ_TILE, name="d_h")
    (gw_proj,) = mm_g(dproj, h, dims="tn", out_dtypes=[WIRE_DTYPE], tm=PROJ_TILE, name="gw_in")
    dx, dg1 = _rowwise(_bin_body, [dx1, dh, x], [g1], [(D_MODEL, F32)], [(8, D_MODEL)], tr=TR, name="bwd_in")

    small = jnp.concatenate([dg1, dg2, dgq, dgkv, dg3, dg4, loss8], axis=1)
    return dx, gw_proj, small


def _place():
    x, y, c = lax.axis_index("x"), lax.axis_index("y"), lax.axis_index("c")
    chips = [(1 - x, y), (x, 1 - y), (1 - x, 1 - y)]
    return x, y, c, chips


def _cast_place_body(me_ref, w_ref, *rest):
    o_ref = rest[-1]
    o_ref[...] = w_ref[...].astype(o_ref.dtype)


def _cast_place(me_arr, w, name, after=None):
    rows, cols = w.shape
    tr = min(rows, 256)
    after = [] if after is None else [after]
    grid_spec = pltpu.PrefetchScalarGridSpec(
        num_scalar_prefetch=1, grid=(rows // tr,),
        in_specs=[pl.BlockSpec((tr, cols), lambda i, me: (i, 0))] + [ANY] * len(after),
        out_specs=pl.BlockSpec((None, tr, cols), lambda i, me: (me[0], i, 0)))
    return _pcall(
        _cast_place_body, name=name, grid_spec=grid_spec,
        out_shape=jax.ShapeDtypeStruct((N_CHIPS, rows, cols), WIRE_DTYPE),
        compiler_params=pltpu.CompilerParams(dimension_semantics=("parallel",)),
    )(me_arr, w, *after)


def _cast_place_t_body(me_ref, w_ref, o_ref, *, n):
    i = pl.program_id(0)

    @pl.when(i < n)
    def _():
        o_ref[...] = w_ref[...].astype(o_ref.dtype)

    @pl.when(i == n)
    def _():
        o_ref[...] = jnp.zeros_like(o_ref)


def _cast_place_t(me_arr, w_t, name):
    rows, cols = w_t.shape
    n = rows // IN_TR
    grid_spec = pltpu.PrefetchScalarGridSpec(
        num_scalar_prefetch=1, grid=(n + 1,),
        in_specs=[pl.BlockSpec((IN_TR, cols), lambda i, me: (jnp.minimum(i, n - 1), 0))],
        out_specs=pl.BlockSpec((IN_TR, cols), lambda i, me: (jnp.where(i < n, me[0] * n + i, N_CHIPS * n), 0)))
    return _pcall(
        functools.partial(_cast_place_t_body, n=n), name=name, grid_spec=grid_spec,
        out_shape=jax.ShapeDtypeStruct((PROJ_COLS, cols), WIRE_DTYPE),
        compiler_params=pltpu.CompilerParams(dimension_semantics=("arbitrary",)),
    )(me_arr, w_t)


HBM = pl.BlockSpec(memory_space=pltpu.HBM)
SEM = pl.BlockSpec(memory_space=pltpu.SEMAPHORE)
EFFECT = pltpu.SideEffectType.DATAFLOW_SIDE_EFFECTING


def _copy_start(make, arrays, after, name, n_sems):
    n_a = len(arrays)
    after = [] if after is None else [after]

    def body(*refs):
        for send, _ in make(refs[:n_a], refs[-n_a - 3], refs[-n_a - 2]):
            send.start()
        refs[-1][...] = jnp.zeros_like(refs[-1])

    res = _pcall(
        body, name=name,
        in_specs=[HBM] * n_a + [ANY] * len(after),
        out_specs=[SEM, SEM] + [HBM] * n_a + [pl.BlockSpec(memory_space=pltpu.VMEM)],
        out_shape=[pltpu.SemaphoreType.DMA((n_sems,)), pltpu.SemaphoreType.DMA((n_sems,))]
        + [pltpu.HBM(a.shape, a.dtype) for a in arrays] + [jax.ShapeDtypeStruct((8, HEAD), F32)],
        input_output_aliases={i: 2 + i for i in range(n_a)},
        compiler_params=pltpu.CompilerParams(has_side_effects=EFFECT),
    )(*[pltpu.with_memory_space_constraint(a, pltpu.HBM) for a in arrays], *after)
    return (res[0], res[1]), list(res[2:2 + n_a]), res[-1]


def _copy_wait(make, sems, arrays, after, name):
    n_a = len(arrays)
    after = list(after) if isinstance(after, (list, tuple)) else [after]

    def body(*refs):
        for send, recv in make(refs[:n_a], refs[n_a], refs[n_a + 1]):
            send.wait_send()
            recv.wait_recv()

    return list(_pcall(
        body, name=name,
        in_specs=[HBM] * n_a + [SEM, SEM] + [ANY] * len(after), out_specs=[HBM] * n_a,
        out_shape=[pltpu.HBM(a.shape, a.dtype) for a in arrays],
        input_output_aliases={i: i for i in range(n_a)},
        compiler_params=pltpu.CompilerParams(has_side_effects=EFFECT),
    )(*arrays, sems[0], sems[1], *after))


def _slot(buf, chip, half):
    if buf.ndim == 2:
        hc = buf.shape[1] // 2
        return buf.at[pl.ds(pl.multiple_of(chip * IN_SHARD, 16), IN_SHARD), pl.ds(pl.multiple_of(half * hc, HEAD), hc)]
    hr = buf.shape[1] // 2
    return buf.at[chip, pl.ds(pl.multiple_of(half * hr, 16), hr)]


def _ag_descs(bufs, send_sems, recv_sems):
    x, y, c, chips = _place()
    me = 2 * x + y
    out = []
    for w, buf in enumerate(bufs):
        for j, (px, py) in enumerate(chips):
            mk = lambda ref, w=w, j=j, px=px, py=py: pltpu.make_async_remote_copy(
                src_ref=ref, dst_ref=ref, send_sem=send_sems.at[w * 3 + j], recv_sem=recv_sems.at[w * 3 + j],
                device_id=(px, py, c), device_id_type=MESH)
            out.append((mk(_slot(buf, me, c)), mk(_slot(buf, 2 * px + py, c))))
    return out


def _fw_descs(bufs, send_sems, recv_sems):
    x, y, c, chips = _place()
    out = []
    for w, buf in enumerate(bufs):
        for j, (px, py) in enumerate(chips):
            def mk(which, w=w, j=j, buf=buf, px=px, py=py):
                ref = _slot(buf, 2 * px + py, which)
                return pltpu.make_async_remote_copy(
                    src_ref=ref, dst_ref=ref, send_sem=send_sems.at[w * 3 + j], recv_sem=recv_sems.at[w * 3 + j],
                    device_id=(x, y, 1 - c), device_id_type=MESH)
            out.append((mk(c), mk(1 - c)))
    return out


def _sc_descs(refs, send_sems, recv_sems):
    n_w = len(refs) // 2
    x, y, c, chips = _place()
    me = 2 * x + y
    out = []
    for w in range(n_w):
        for j, (px, py) in enumerate(chips):
            d = pltpu.make_async_remote_copy(
                src_ref=refs[w].at[2 * px + py], dst_ref=refs[n_w + w].at[me],
                send_sem=send_sems.at[w * 3 + j], recv_sem=recv_sems.at[w * 3 + j],
                device_id=(px, py, c), device_id_type=MESH)
            out.append((d, d))
    return out


def _pair_descs(src_of):
    def make(refs, send_sems, recv_sems):
        n_w = len(refs) // 2
        x, y, c, _ = _place()
        out = []
        for w in range(n_w):
            d = pltpu.make_async_remote_copy(
                src_ref=src_of(refs[w], c), dst_ref=refs[n_w + w],
                send_sem=send_sems.at[w], recv_sem=recv_sems.at[w],
                device_id=(x, y, 1 - c), device_id_type=MESH)
            out.append((d, d))
        return out
    return make


_EX_DESCS = _pair_descs(lambda g4, c: g4.at[:, 1 - c])
_SW_DESCS = _pair_descs(lambda half, c: half)
_EXT_DESCS = _pair_descs(lambda g, c: g.at[pl.ds(0, IN_COLS),
                                           pl.ds(pl.multiple_of((1 - c) * (D_MODEL // 2), HEAD), D_MODEL // 2)])


def _sm_descs(refs, send_sems, recv_sems):
    buf = refs[0]
    rows8 = buf.shape[0] // N_DEV
    x, y, c, _ = _place()
    flip = lambda v, d: 1 - v if d else v
    blk = lambda px, py, pc: buf.at[pl.ds(pl.multiple_of((4 * px + 2 * py + pc) * rows8, 8), rows8)]
    out = []
    for k in range(1, N_DEV):
        px, py, pc = flip(x, k & 4), flip(y, k & 2), flip(c, k & 1)
        mk = lambda ref, k=k, px=px, py=py, pc=pc: pltpu.make_async_remote_copy(
            src_ref=ref, dst_ref=ref, send_sem=send_sems.at[k - 1], recv_sem=recv_sems.at[k - 1],
            device_id=(px, py, pc), device_id_type=MESH)
        out.append((mk(blk(x, y, c)), mk(blk(px, py, pc))))
    return out


def _place_rows_body(i_ref, x_ref, o_ref):
    o_ref[...] = x_ref[...]


def _place_rows(i_arr, x, n_blocks, name):
    r, n = x.shape
    grid_spec = pltpu.PrefetchScalarGridSpec(
        num_scalar_prefetch=1, grid=(1,),
        in_specs=[pl.BlockSpec((r, n), lambda g, i: (0, 0))],
        out_specs=pl.BlockSpec((r, n), lambda g, i: (i[0], 0)))
    return _pcall(_place_rows_body, name=name, grid_spec=grid_spec,
                  out_shape=jax.ShapeDtypeStruct((n_blocks * r, n), x.dtype))(i_arr, x)


def _ag_forward_body(*refs, n_w):
    bufs = refs[n_w:2 * n_w]
    send_sems, recv_sems = refs[2 * n_w:]
    pairs = _fw_descs(bufs, send_sems, recv_sems)
    for fw, _ in pairs:
        fw.start()
    for fw, back in pairs:
        back.wait_recv()
        fw.wait_send()


def _ag_forward(bufs, tag):
    n_w = len(bufs)
    return list(_pcall(
        functools.partial(_ag_forward_body, n_w=n_w), name="weight_allgather_forward_" + tag,
        in_specs=[ANY] * n_w, out_specs=[ANY] * n_w,
        out_shape=[jax.ShapeDtypeStruct(b.shape, b.dtype) for b in bufs],
        input_output_aliases={w: w for w in range(n_w)},
        scratch_shapes=[pltpu.SemaphoreType.DMA((3 * n_w,))] * 2,
    )(*bufs))


def _pair_add_body(c_ref, mine_ref, theirs_ref, o_ref):
    o_ref[...] = (mine_ref[...].astype(F32) + theirs_ref[...].astype(F32)).astype(o_ref.dtype)


def _pair_add(c_arr, g4, recv, name):
    _, _, hr, cols = g4.shape
    tr = min(hr, 256)
    grid_spec = pltpu.PrefetchScalarGridSpec(
        num_scalar_prefetch=1, grid=(N_CHIPS, hr // tr),
        in_specs=[pl.BlockSpec((None, None, tr, cols), lambda s, i, c: (s, c[0], i, 0)),
                  pl.BlockSpec((None, tr, cols), lambda s, i, c: (s, i, 0))],
        out_specs=pl.BlockSpec((None, tr, cols), lambda s, i, c: (s, i, 0)))
    return _pcall(
        _pair_add_body, name=name, grid_spec=grid_spec,
        out_shape=jax.ShapeDtypeStruct(recv.shape, recv.dtype),
        compiler_params=pltpu.CompilerParams(dimension_semantics=("parallel", "parallel")),
    )(c_arr, g4, recv)


def _pair_add_t(c_arr, g, recv, name):
    rows, hc = recv.shape
    grid_spec = pltpu.PrefetchScalarGridSpec(
        num_scalar_prefetch=1, grid=(rows // IN_TR,),
        in_specs=[pl.BlockSpec((IN_TR, hc), lambda i, c: (i, c[0])), pl.BlockSpec((IN_TR, hc), lambda i, c: (i, 0))],
        out_specs=pl.BlockSpec((IN_TR, hc), lambda i, c: (i, 0)))
    return _pcall(
        _pair_add_body, name=name, grid_spec=grid_spec,
        out_shape=jax.ShapeDtypeStruct(recv.shape, recv.dtype),
        compiler_params=pltpu.CompilerParams(dimension_semantics=("parallel",)),
    )(c_arr, g, recv)


def _sum4_body(me_ref, p_ref, l0, l1, l2, l3, o_ref):
    me = me_ref[0]
    t = [jnp.where(me == j, p_ref[...], l[...]).astype(F32) for j, l in enumerate((l0, l1, l2, l3))]
    o_ref[...] = ((t[0] + t[1]) + t[2]) + t[3]


def _sum4(me_arr, part, landed, name):
    _, hr, cols = part.shape
    tr = IN_TR if hr == IN_SHARD else min(hr, 256)

    def slot(j):
        return lambda i, me: (jnp.where(me[0] == j, (j + 1) % N_CHIPS, j), i, 0)

    grid_spec = pltpu.PrefetchScalarGridSpec(
        num_scalar_prefetch=1, grid=(hr // tr,),
        in_specs=[pl.BlockSpec((None, tr, cols), lambda i, me: (me[0], i, 0))]
        + [pl.BlockSpec((None, tr, cols), slot(j)) for j in range(N_CHIPS)],
        out_specs=pl.BlockSpec((tr, cols), lambda i, me: (i, 0)))
    return _pcall(
        _sum4_body, name=name, grid_spec=grid_spec,
        out_shape=jax.ShapeDtypeStruct((hr, cols), F32),
        compiler_params=pltpu.CompilerParams(dimension_semantics=("parallel",)),
    )(me_arr, part, landed, landed, landed, landed)


def _adamw(w, g, m, v):
    m = ADAM_B1 * m + (1.0 - ADAM_B1) * g
    v = ADAM_B2 * v + (1.0 - ADAM_B2) * (g * g)
    m_hat = m / (1.0 - ADAM_B1 ** ADAM_STEP)
    v_hat = v / (1.0 - ADAM_B2 ** ADAM_STEP)
    delta = -ADAM_LR * (m_hat / (jnp.sqrt(v_hat) + ADAM_EPS) + ADAM_WD * w)
    return delta, m, v


def _adamw_half_body(h_ref, w_ref, g_in_ref, m_ref, v_ref, *rest):
    g_ref, d_ref, nm_ref, nv_ref, done_ref = rest[-5:]
    done_ref[...] = jnp.zeros_like(done_ref)
    g = g_in_ref[...]
    g_ref[...] = g
    d, m, v = _adamw(w_ref[...], g, m_ref[...], v_ref[...])
    d_ref[...] = d
    nm_ref[...] = m
    nv_ref[...] = v


def _adamw_half(h_arr, w, g_half, m, v, prev, name):
    rows, cols = w.shape
    if g_half.shape[0] == rows:
        tr, nh = IN_TR, rows // IN_TR
        at_half = pl.BlockSpec((tr, cols // 2), lambda i, h: (i, h[0]))
        g_spec = pl.BlockSpec((tr, cols // 2), lambda i, h: (i, 0))
    else:
        tr = min(rows // 2, 128)
        nh = (rows // 2) // tr
        at_half = pl.BlockSpec((tr, cols), lambda i, h: (h[0] * nh + i, 0))
        g_spec = pl.BlockSpec((tr, cols), lambda i, h: (i, 0))
    grid_spec = pltpu.PrefetchScalarGridSpec(
        num_scalar_prefetch=1, grid=(nh,),
        in_specs=[at_half, g_spec, at_half, at_half] + [ANY] * len(prev),
        out_specs=[at_half] * 4 + [pl.BlockSpec((8, HEAD), lambda i, h: (0, 0))])
    return list(_pcall(
        _adamw_half_body, name=name, grid_spec=grid_spec,
        out_shape=[jax.ShapeDtypeStruct(w.shape, F32)] * 4 + [jax.ShapeDtypeStruct((8, HEAD), F32)],
        input_output_aliases={5 + k: k for k in range(len(prev))},
        compiler_params=pltpu.CompilerParams(dimension_semantics=("arbitrary",)),
    )(h_arr, w, g_half, m, v, *prev))


def _small_update_body(gath_ref, w_ref, m_ref, v_ref, g_ref, d_ref, nm_ref, nv_ref, loss_ref, *, n_gain):
    tot = gath_ref[0:1, :]
    for i in range(1, gath_ref.shape[0]):
        tot = tot + gath_ref[i:i + 1, :]
    g = tot[:, 0:n_gain]
    g_ref[...] = g
    d, m, v = _adamw(w_ref[...], g, m_ref[...], v_ref[...])
    d_ref[...] = d
    nm_ref[...] = m
    nv_ref[...] = v
    loss_ref[...] = (0.5 / D_MODEL) * jnp.sum(tot[:, n_gain:n_gain + HEAD], axis=1, keepdims=True) * jnp.ones((1, HEAD), F32)


def _small_update(gath, w, m, v):
    n_gain = w.shape[1]
    vm = pl.BlockSpec(memory_space=pltpu.VMEM)
    return _pcall(
        functools.partial(_small_update_body, n_gain=n_gain), name="gain_update",
        in_specs=[vm] * 4, out_specs=[vm] * 5,
        out_shape=[jax.ShapeDtypeStruct((1, n_gain), F32)] * 4 + [jax.ShapeDtypeStruct((1, HEAD), F32)],
    )(gath, w, m, v)


def kernel(x, positions, norm_attn_pre, norm_attn_post, w_in, q_latent_norm, kv_latent_norm, w_uq, w_ukv, w_out, norm_mlp_pre, norm_mlp_post, w_up, w_down, loss_target, m_norm_attn_pre, m_norm_attn_post, m_w_in, m_q_latent_norm, m_kv_latent_norm, m_w_uq, m_w_ukv, m_w_out, m_norm_mlp_pre, m_norm_mlp_post, m_w_up, m_w_down, v_norm_attn_pre, v_norm_attn_post, v_w_in, v_q_latent_norm, v_kv_latent_norm, v_w_uq, v_w_ukv, v_w_out, v_norm_mlp_pre, v_norm_mlp_post, v_w_up, v_w_down):
    T = x.shape[1]
    c_arr = lax.axis_index("c").astype(jnp.int32).reshape(1)
    me_arr = (2 * lax.axis_index("x") + lax.axis_index("y")).astype(jnp.int32).reshape(1)
    names = ["w_in", "w_uq", "w_ukv", "w_out", "w_up", "w_down"]

    transposed = lambda a: jnp.swapaxes(a, 1, 2)
    mats = [transposed(w_in)[0], w_uq[0], w_ukv[0], w_out[0], w_up[0], w_down[0]]
    me8_arr = (4 * lax.axis_index("x") + 2 * lax.axis_index("y") + lax.axis_index("c")).astype(jnp.int32).reshape(1)
    col_major = lambda g: jnp.transpose(g, (1, 0, 2)).reshape(g.shape[1], N_CHIPS * g.shape[2])
    cast = lambda a: a.astype(MXU_DTYPE)
    to_shards = lambda g: jnp.transpose(g.reshape(g.shape[0], N_CHIPS, g.shape[1] // N_CHIPS), (1, 0, 2))
    halved = lambda g: g.reshape(N_CHIPS, 2, g.shape[1] // 2, g.shape[2])
    empty = lambda a, shape=None: lax.empty(a.shape if shape is None else shape, a.dtype)

    sem_in, buf_in, going = _copy_start(_ag_descs, [_cast_place_t(me_arr, mats[0], "cast_w_in")], None,
                                        "weight_allgather_start_in", 3)
    placed = [_cast_place(me_arr, w, "cast_" + n, going) for w, n in zip(mats[1:], names[1:])]
    sem_att, buf_att, going = _copy_start(_ag_descs, placed[:3], going, "weight_allgather_start_attn", 9)
    sem_mlp, buf_mlp, started = _copy_start(_ag_descs, placed[3:], going, "weight_allgather_start_mlp", 6)

    going_on = {}

    def in_weights(after):
        (win_g,) = _ag_forward(_copy_wait(_ag_descs, sem_in, buf_in, after, "weight_allgather_wait_in"), "in")
        return cast(win_g)

    def attn_prefetch(after):
        landed = _copy_wait(_ag_descs, sem_att, buf_att, after, "weight_allgather_wait_attn")
        going_on["fw_attn"] = _copy_start(_fw_descs, landed, None, "weight_allgather_forward_start_attn", 9)
        return going_on["fw_attn"][-1][0:1, 0:1]

    def attn_weights(after):
        sems, bufs, _ = going_on["fw_attn"]
        wuq_g, wukv_g, wout_g = _copy_wait(_fw_descs, sems, bufs, after, "weight_allgather_forward_wait_attn")
        wuq_full = col_major(wuq_g).reshape(LORA, NH, HEAD + ROPE_B)
        w_uq_p = jnp.pad(wuq_full, ((0, 0), (0, 0), (0, QPAD - HEAD - ROPE_B))).reshape(LORA, NH * QPAD)
        return cast(w_uq_p), cast(col_major(wukv_g)), cast(wout_g.reshape(2 * A_W, D_MODEL))

    def mlp_prefetch(after):
        landed = _copy_wait(_ag_descs, sem_mlp, buf_mlp, after, "weight_allgather_wait_mlp")
        going_on["fw"] = _copy_start(_fw_descs, landed, None, "weight_allgather_forward_start_mlp", 6)
        return going_on["fw"][-1]

    def mlp_weights(after):
        sems, bufs, _ = going_on["fw"]
        wup_g, wdown_g = _copy_wait(_fw_descs, sems, bufs, after, "weight_allgather_forward_wait_mlp")
        return cast(wup_g), cast(wdown_g.reshape(D_FF, D_MODEL))

    def exchange_start(g4s, tag):
        lands = [empty(g, (g.shape[0],) + g.shape[2:]) for g in g4s]
        return _copy_start(_EX_DESCS, g4s + lands, None, "grad_pair_exchange_start_" + tag, len(g4s))

    def exchange_finish(started_ex, after, ns, tag):
        sems, arrs, _ = started_ex
        arrs = _copy_wait(_EX_DESCS, sems, arrs, after, "grad_pair_exchange_wait_" + tag)
        n = len(ns)
        return [_pair_add(c_arr, g4, r, "pair_add_" + nm) for g4, r, nm in zip(arrs[:n], arrs[n:], ns)]

    def scatter_start(parts, after, tag):
        return _copy_start(_sc_descs, parts + [empty(p) for p in parts], after, "grad_scatter_start_" + tag,
                           3 * len(parts))

    def scatter_finish(started_sc, after, tag):
        sems, arrs, _ = started_sc
        arrs = _copy_wait(_sc_descs, sems, arrs, after, "grad_scatter_wait_" + tag)
        return arrs[:len(arrs) // 2], arrs[len(arrs) // 2:]

    def down_grad_ready(gw_down):
        going_on["x_down"] = exchange_start([halved(gw_down.reshape(N_CHIPS, D_MODEL, D_MODEL))], "down")
        return going_on["x_down"][-1]

    def up_grad_ready(gw_up):
        going_on["x_up"] = exchange_start([halved(gw_up)], "up")
        parts = exchange_finish(going_on["x_down"], going_on["x_up"][-1], names[5:], "down")
        going_on["s_down"] = scatter_start(parts, started, "down")
        return going_on["s_down"][-1][0:1, 0:1]

    def attn_grads_ready(gw_out, gw_uq_p, gw_ukv):
        gw_uq = to_shards(gw_uq_p.reshape(LORA, NH, QPAD)[:, :, :HEAD + ROPE_B].reshape(LORA, NH * (HEAD + ROPE_B)))
        full4 = [halved(g) for g in (gw_uq, to_shards(gw_ukv), gw_out.reshape(N_CHIPS, LORA, D_MODEL))]
        x_attn = exchange_start(full4, "attn")
        parts_up = exchange_finish(going_on["x_up"], x_attn[-1], names[4:5], "up")
        parts = exchange_finish(x_attn, parts_up[0], names[1:4], "attn") + parts_up
        going_on["s_rest"] = scatter_start(parts, going_on["s_down"][-1], "attn_up")
        return going_on["s_rest"][-1][0:1, 0:1]

    dx, gw_proj, small = _local_step(
        x[0], positions[0].astype(F32).reshape(T, 1), loss_target[0],
        norm_attn_pre + started[0:1, 0:1], norm_attn_post, q_latent_norm, kv_latent_norm, norm_mlp_pre, norm_mlp_post,
        in_weights, attn_prefetch, attn_weights, mlp_prefetch, mlp_weights,
        down_grad_ready, up_grad_ready, attn_grads_ready)

    ms = [transposed(m_w_in)[0], m_w_uq[0], m_w_ukv[0], m_w_out[0], m_w_up[0], m_w_down[0]]
    vs = [transposed(v_w_in)[0], v_w_uq[0], v_w_ukv[0], v_w_out[0], v_w_up[0], v_w_down[0]]
    sib_arr = 1 - c_arr

    def finish(parts, landed, lo, hi, tag):
        sl = slice(lo, hi)
        halves = [_sum4(me_arr, p, l, "chip_sum_" + n) for p, l, n in zip(parts, landed, names[sl])]
        n = len(halves)
        sems, arrs, _ = _copy_start(_SW_DESCS, halves + [empty(h) for h in halves], None,
                                    "grad_pair_swap_start_" + tag, n)
        own = [_adamw_half(c_arr, w, g, m, v, [], "adamw_own_" + nm)
               for w, g, m, v, nm in zip(mats[sl], arrs[:n], ms[sl], vs[sl], names[sl])]
        arrs = _copy_wait(_SW_DESCS, sems, arrs, own[-1][4], "grad_pair_swap_wait_" + tag)
        return [_adamw_half(sib_arr, w, g, m, v, prev[:4], "adamw_sib_" + nm)
                for w, g, m, v, prev, nm in zip(mats[sl], arrs[n:], ms[sl], vs[sl], own, names[sl])]

    sem_small, (gath,), small_going = _copy_start(
        _sm_descs, [_place_rows(me8_arr, small, N_DEV, "place_small")], None, "small_allgather_start", N_DEV - 1)
    sems, arrs, _ = _copy_start(_EXT_DESCS, [gw_proj, lax.empty((IN_COLS, D_MODEL // 2), WIRE_DTYPE)], None,
                                "grad_pair_exchange_start_in", 1)
    gw_proj, from_sib = _copy_wait(_EXT_DESCS, sems, arrs, small_going, "grad_pair_exchange_wait_in")
    part_in = _pair_add_t(c_arr, gw_proj, from_sib, "pair_add_w_in").reshape(N_CHIPS, IN_SHARD, D_MODEL // 2)
    s_in = scatter_start([part_in], None, "in")
    parts_rest, landed_rest = scatter_finish(going_on["s_rest"], s_in[-1], "attn_up")
    parts_down, landed_down = scatter_finish(going_on["s_down"], landed_rest[0], "down")
    upd_rest = finish(parts_rest + parts_down, landed_rest + landed_down, 1, 6, "rest")
    parts_in, landed_in = scatter_finish(s_in, upd_rest[-1][0], "in")
    upd = finish(parts_in, landed_in, 0, 1, "in") + upd_rest
    grads = [u[0] for u in upd]

    (gath,) = _copy_wait(_sm_descs, sem_small, [gath], grads[0], "small_allgather_wait")
    gains = [norm_attn_pre, norm_attn_post, q_latent_norm, kv_latent_norm, norm_mlp_pre, norm_mlp_post]
    gm = [m_norm_attn_pre, m_norm_attn_post, m_q_latent_norm, m_kv_latent_norm, m_norm_mlp_pre, m_norm_mlp_post]
    gv = [v_norm_attn_pre, v_norm_attn_post, v_q_latent_norm, v_kv_latent_norm, v_norm_mlp_pre, v_norm_mlp_post]
    cat = lambda xs: jnp.concatenate(xs, axis=1)
    g_s, d_s, m_s, v_s, loss_v = _small_update(gath, cat(gains), cat(gm), cat(gv))
    widths = [a.shape[1] for a in gains]
    offs = [sum(widths[:i]) for i in range(len(widths))]
    split = lambda a: [a[:, o:o + w] for o, w in zip(offs, widths)]
    g_gain, d_gain, m_gain, v_gain = split(g_s), split(d_s), split(m_s), split(v_s)

    def ordered(gain_list, mat_list):
        gl, ml = gain_list, [transposed(mat_list[0][None])] + [a[None] for a in mat_list[1:]]
        return [gl[0], gl[1], ml[0], gl[2], gl[3], ml[1], ml[2], ml[3], gl[4], gl[5], ml[4], ml[5]]

    loss = loss_v[0, 0]
    return (loss, dx[None],
            *ordered(g_gain, grads),
            *ordered(d_gain, [u[1] for u in upd]),
            *ordered(m_gain, [u[2] for u in upd]),
            *ordered(v_gain, [u[3] for u in upd]))
```

```python
import functools

import jax
import jax.numpy as jnp
from jax import lax
from jax.experimental import pallas as pl
from jax.experimental.pallas import tpu as pltpu

F32 = jnp.float32
BF16 = jnp.bfloat16
MXU_DTYPE = jnp.bfloat16
WIRE_DTYPE = jnp.bfloat16

D_MODEL = 2048
HEAD = 128
NH = 8
A_W = NH * HEAD
LORA = 512
ROPE_B = 64
QPAD = 256
MAIN_COLS = 3 * A_W + 2 * LORA
IN_COLS = MAIN_COLS + ROPE_B
PROJ_COLS = MAIN_COLS + HEAD
PROJ_TILE = PROJ_COLS // 3
IN_SHARD = 1040
IN_TR = 208
D_FF = 4 * D_MODEL
DIL = (1, 4, 16)
ROT_A = 32
ROPE_THETA = 500000.0
EPS = 1e-6
NEG = -1e30
N_CHIPS = 4
N_DEV = 8

ADAM_LR = 0.001
ADAM_B1 = 0.9
ADAM_B2 = 0.999
ADAM_EPS = 1e-08
ADAM_WD = 0.01
ADAM_STEP = 10

MESH = pl.DeviceIdType.MESH
ANY = pl.BlockSpec(memory_space=pl.ANY)


def _pcall(body, **kw):
    return pl.pallas_call(body, **kw)


_DIMS = {
    "nn": (((1,), (0,)), ((), ())),
    "nt": (((1,), (1,)), ((), ())),
    "tn": (((0,), (0,)), ((), ())),
}


def _mm_body(*refs, dims, nk, epi, n_extra, n_after, n_out):
    a_ref, b_ref = refs[0], refs[1]
    extra = refs[2:2 + n_extra]
    outs = refs[2 + n_extra + n_after:2 + n_extra + n_after + n_out]
    part = lax.dot_general(a_ref[...], b_ref[...], _DIMS[dims], preferred_element_type=F32)

    def finish(acc):
        res = epi(acc, *[r[...] for r in extra]) if epi is not None else (acc,)
        for o_ref, o in zip(outs, res):
            o_ref[...] = o.astype(o_ref.dtype)

    if nk == 1:
        finish(part)
        return
    acc_ref = refs[-1]
    k = pl.program_id(2)

    @pl.when(k == 0)
    def _():
        acc_ref[...] = part

    @pl.when(k > 0)
    def _():
        acc_ref[...] += part

    @pl.when(k == nk - 1)
    def _():
        finish(acc_ref[...])


def _matmul(a, b, *, dims, out_dtypes, tm, tn, tk, name, epi=None, extras=(), row_extras=(), b_outer=False,
            b_shards=0, out_shards=0, after=None):
    if b_shards:
        assert dims in ("nn", "nt") and b.shape[0] == b_shards
        b2 = (b.shape[1], b_shards * b.shape[2])
    else:
        b2 = b.shape
    if dims == "nn":
        (M, K), (K2, N) = a.shape, b2
    elif dims == "nt":
        (M, K), (N, K2) = a.shape, b2
    else:
        (K, M), (K2, N) = a.shape, b2
    assert K == K2, (a.shape, b.shape, dims)
    tm, tn, tk = min(tm, M), min(tn, N), min(tk, K)
    assert M % tm == 0 and N % tn == 0 and K % tk == 0, (name, M, N, K, tm, tn, tk)
    nk = K // tk

    def at(f):
        if b_outer:
            return lambda j, i, k: f(i, j, k)
        return f

    a_spec = {"nn": pl.BlockSpec((tm, tk), at(lambda i, j, k: (i, k))),
              "nt": pl.BlockSpec((tm, tk), at(lambda i, j, k: (i, k))),
              "tn": pl.BlockSpec((tk, tm), at(lambda i, j, k: (k, i)))}[dims]
    b_spec = {"nn": pl.BlockSpec((tk, tn), at(lambda i, j, k: (k, j))),
              "nt": pl.BlockSpec((tn, tk), at(lambda i, j, k: (j, k))),
              "tn": pl.BlockSpec((tk, tn), at(lambda i, j, k: (k, j)))}[dims]
    if b_shards:
        per = b.shape[2] // (tn if dims == "nn" else tk)
        assert per >= 1 and b.shape[2] % (tn if dims == "nn" else tk) == 0
        b_spec = {"nn": pl.BlockSpec((None, tk, tn), at(lambda i, j, k: (j // per, k, j % per))),
                  "nt": pl.BlockSpec((None, tn, tk), at(lambda i, j, k: (k // per, j, k % per)))}[dims]
    o_spec = pl.BlockSpec((tm, tn), at(lambda i, j, k: (i, j)))
    o_shape = (M, N)
    if out_shards:
        assert not extras and N % out_shards == 0 and (N // out_shards) % tn == 0
        o_per = (N // out_shards) // tn
        o_spec = pl.BlockSpec((None, tm, tn), at(lambda i, j, k: (j // o_per, i, j % o_per)))
        o_shape = (out_shards, M, N // out_shards)
    r_specs = [pl.BlockSpec((tm, r.shape[1]), at(lambda i, j, k: (i, 0))) for r in row_extras]
    after = [] if after is None else [after]
    body = functools.partial(_mm_body, dims=dims, nk=nk, epi=epi, n_extra=len(extras) + len(row_extras),
                             n_after=len(after), n_out=len(out_dtypes))
    res = _pcall(
        body, name=name,
        grid=(N // tn, M // tm, nk) if b_outer else (M // tm, N // tn, nk),
        in_specs=[a_spec, b_spec] + [o_spec] * len(extras) + r_specs + [ANY] * len(after),
        out_specs=[o_spec] * len(out_dtypes),
        out_shape=[jax.ShapeDtypeStruct(o_shape, dt) for dt in out_dtypes],
        scratch_shapes=[pltpu.VMEM((tm, tn), F32)] if nk > 1 else [],
        compiler_params=pltpu.CompilerParams(
            dimension_semantics=("parallel", "parallel", "arbitrary")),
    )(a, b, *extras, *row_extras, *after)
    return list(res)


def _rowwise(body, row_ins, vec_ins, row_outs, acc_outs, *, tr, name):
    T = row_ins[0].shape[0]
    tr = min(tr, T)
    assert T % tr == 0
    in_specs = [pl.BlockSpec((tr, a.shape[1]), lambda i: (i, 0)) for a in row_ins]
    in_specs += [pl.BlockSpec(a.shape, lambda i: (0, 0)) for a in vec_ins]
    out_specs = [pl.BlockSpec((tr, w), lambda i: (i, 0)) for (w, _) in row_outs]
    out_specs += [pl.BlockSpec(s, lambda i: (0, 0)) for s in acc_outs]
    out_shape = [jax.ShapeDtypeStruct((T, w), dt) for (w, dt) in row_outs]
    out_shape += [jax.ShapeDtypeStruct(s, F32) for s in acc_outs]
    sem = "arbitrary" if acc_outs else "parallel"
    return list(_pcall(
        body, name=name, grid=(T // tr,), in_specs=in_specs, out_specs=out_specs,
        out_shape=out_shape,
        compiler_params=pltpu.CompilerParams(dimension_semantics=(sem,)),
    )(*row_ins, *vec_ins))


def _rstd(x):
    return lax.rsqrt(jnp.mean(x * x, axis=-1, keepdims=True) + EPS)


def _rms_bwd(x, rstd, dyg):
    xh = x * rstd
    return rstd * (dyg - xh * jnp.mean(dyg * xh, axis=-1, keepdims=True)), xh


def _fold8(v):
    r, w = v.shape
    return jnp.sum(v.reshape(r // 8, 8, w), axis=0)


def _acc(ref, val):
    first = pl.program_id(0) == 0

    @pl.when(first)
    def _():
        ref[...] = val

    @pl.when(jnp.logical_not(first))
    def _():
        ref[...] += val


def _rope(x, c, sa, sb, half):
    return x * c + pltpu.roll(x, HEAD - half, 1) * sa + pltpu.roll(x, half, 1) * sb


def _rope_t(dy, c, sa, sb, half):
    return dy * c - pltpu.roll(dy, HEAD - half, 1) * sa - pltpu.roll(dy, half, 1) * sb


def _rope_tab_body(pos_ref, inv_ref, ca, saa, sab, cb, sba, sbb):
    pos = pos_ref[...]
    lane = lax.broadcasted_iota(jnp.int32, (pos.shape[0], HEAD), 1)
    ang_a = pos * inv_ref[0:1, :]
    ang_b = pos * inv_ref[1:2, :]
    c, s = jnp.cos(ang_a), jnp.sin(ang_a)
    ha = ROT_A // 2
    ca[...] = jnp.where(lane < ROT_A, c, 1.0)
    saa[...] = jnp.where(lane < ha, -s, 0.0)
    sab[...] = jnp.where((lane >= ha) & (lane < ROT_A), s, 0.0)
    c, s = jnp.cos(ang_b), jnp.sin(ang_b)
    hb = ROPE_B // 2
    cb[...] = jnp.where(lane < ROPE_B, c, 1.0)
    sba[...] = jnp.where(lane < hb, -s, 0.0)
    sbb[...] = jnp.where((lane >= hb) & (lane < ROPE_B), s, 0.0)


def _rms_fwd_body(x_ref, g_ref, h_ref):
    x = x_ref[...]
    h_ref[...] = ((x * _rstd(x)) * g_ref[...]).astype(h_ref.dtype)


def _postproj_body(p_ref, ca, saa, sab, cb, sba, sbb, gq_ref, gkv_ref,
                   q_ref, k_ref, v_ref, cqn_ref, ckvn_ref, krope_ref):
    c, sa, sb = ca[...], saa[...], sab[...]
    for h in range(NH):
        lo = h * HEAD
        q_ref[:, lo:lo + HEAD] = _rope(p_ref[:, lo:lo + HEAD], c, sa, sb, ROT_A // 2).astype(q_ref.dtype)
        k_ref[:, lo:lo + HEAD] = _rope(p_ref[:, A_W + lo:A_W + lo + HEAD], c, sa, sb, ROT_A // 2).astype(k_ref.dtype)
    v_ref[...] = p_ref[:, 2 * A_W:3 * A_W].astype(v_ref.dtype)
    cq = p_ref[:, 3 * A_W:3 * A_W + LORA]
    cqn_ref[...] = ((cq * _rstd(cq)) * gq_ref[...]).astype(cqn_ref.dtype)
    ckv = p_ref[:, 3 * A_W + LORA:MAIN_COLS]
    ckvn_ref[...] = ((ckv * _rstd(ckv)) * gkv_ref[...]).astype(ckvn_ref.dtype)
    krope_ref[...] = _rope(p_ref[:, MAIN_COLS:PROJ_COLS], cb[...], sba[...], sbb[...], ROPE_B // 2).astype(krope_ref.dtype)


def _mid_body(x_ref, o_ref, g2_ref, g3_ref, x1_ref, h2_ref):
    o = o_ref[...]
    x1 = x_ref[...] + (o * _rstd(o)) * g2_ref[...]
    x1_ref[...] = x1
    h2_ref[...] = ((x1 * _rstd(x1)) * g3_ref[...]).astype(h2_ref.dtype)


def _loss_body(x1_ref, d_ref, t_ref, g4_ref, dy_ref, dd_ref, loss_ref, dg4_ref):
    d = d_ref[...]
    rstd = _rstd(d)
    y = x1_ref[...] + (d * rstd) * g4_ref[...]
    e = y - t_ref[...]
    dy = e * (1.0 / D_MODEL)
    dy_ref[...] = dy
    dd, dh = _rms_bwd(d, rstd, dy * g4_ref[...])
    dd_ref[...] = dd.astype(dd_ref.dtype)
    _acc(dg4_ref, _fold8(dy * dh))
    e8 = _fold8(e * e)
    l = e8[:, 0:HEAD]
    for j in range(1, D_MODEL // HEAD):
        l = l + e8[:, j * HEAD:(j + 1) * HEAD]
    _acc(loss_ref, l)


def _bmid_body(dy_ref, dh2_ref, x1_ref, o_ref, g2_ref, g3_ref, dx1_ref, do_ref, dg3_ref, dg2_ref):
    x1 = x1_ref[...]
    dh2 = dh2_ref[...]
    dn, x1h = _rms_bwd(x1, _rstd(x1), dh2 * g3_ref[...])
    dx1 = dy_ref[...] + dn
    dx1_ref[...] = dx1
    _acc(dg3_ref, _fold8(dh2 * x1h))
    o = o_ref[...]
    do, oh = _rms_bwd(o, _rstd(o), dx1 * g2_ref[...])
    do_ref[...] = do.astype(do_ref.dtype)
    _acc(dg2_ref, _fold8(dx1 * oh))


def _dproj_body(dq_ref, dk_ref, dv_ref, dcq_ref, dckv_ref, p_ref, dkr_ref,
                ca, saa, sab, cb, sba, sbb, gq_ref, gkv_ref,
                dp_ref, dgq_ref, dgkv_ref):
    c, sa, sb = ca[...], saa[...], sab[...]
    for h in range(NH):
        lo = h * HEAD
        dp_ref[:, lo:lo + HEAD] = _rope_t(dq_ref[:, lo:lo + HEAD], c, sa, sb, ROT_A // 2).astype(dp_ref.dtype)
        dp_ref[:, A_W + lo:A_W + lo + HEAD] = _rope_t(dk_ref[:, lo:lo + HEAD], c, sa, sb, ROT_A // 2).astype(dp_ref.dtype)
    dp_ref[:, 2 * A_W:3 * A_W] = dv_ref[...].astype(dp_ref.dtype)
    cq = p_ref[:, 3 * A_W:3 * A_W + LORA]
    dcqn = dcq_ref[...]
    dcq, cqh = _rms_bwd(cq, _rstd(cq), dcqn * gq_ref[...])
    dp_ref[:, 3 * A_W:3 * A_W + LORA] = dcq.astype(dp_ref.dtype)
    _acc(dgq_ref, _fold8(dcqn * cqh))
    ckv = p_ref[:, 3 * A_W + LORA:MAIN_COLS]
    dckvn = dckv_ref[...]
    dckv, ckvh = _rms_bwd(ckv, _rstd(ckv), dckvn * gkv_ref[...])
    dp_ref[:, 3 * A_W + LORA:MAIN_COLS] = dckv.astype(dp_ref.dtype)
    _acc(dgkv_ref, _fold8(dckvn * ckvh))
    dkr = dkr_ref[:, 0:HEAD]
    for h in range(1, NH):
        dkr = dkr + dkr_ref[:, h * HEAD:(h + 1) * HEAD]
    dp_ref[:, MAIN_COLS:PROJ_COLS] = _rope_t(dkr, cb[...], sba[...], sbb[...], ROPE_B // 2).astype(dp_ref.dtype)


def _bin_body(dx1_ref, dh_ref, x_ref, g1_ref, dx_ref, dg1_ref):
    x = x_ref[...]
    dh = dh_ref[...]
    dn, xh = _rms_bwd(x, _rstd(x), dh * g1_ref[...])
    dx_ref[...] = dx1_ref[...] + dn
    _acc(dg1_ref, _fold8(dh * xh))


def _dot_nt(a, b):
    return lax.dot_general(a, b, _DIMS["nt"], preferred_element_type=F32)


def _dot_tn(a, b):
    return lax.dot_general(a, b, _DIMS["tn"], preferred_element_type=F32)


def _dot_nn(a, b):
    return jnp.dot(a, b, preferred_element_type=F32)


DIL_SCALE = HEAD ** -0.5
DIL_CHUNK = 256


def _dil_rows(t, d):
    r = t & (d - 1)
    n = t >> (d.bit_length() - 1)
    start = r + n * (HEAD * d)
    has_prev = n > 0
    pstart = jnp.where(has_prev, start - HEAD * d, start)
    if d == 1:
        return pl.ds(pl.multiple_of(start, HEAD), HEAD), pl.ds(pl.multiple_of(pstart, HEAD), HEAD), has_prev
    return pl.ds(start, HEAD, stride=d), pl.ds(pstart, HEAD, stride=d), has_prev


def _dil_band():
    row = lax.broadcasted_iota(jnp.int32, (HEAD, 2 * HEAD), 0)
    col = lax.broadcasted_iota(jnp.int32, (HEAD, 2 * HEAD), 1)
    return (col >= row) & (col <= row + HEAD), col >= HEAD


def _dil_fwd_body(q_ref, k_ref, v_ref, a_ref, lse_ref, o1, o2, o3, l1, l2, l3, *, nt, unroll):
    band, is_cur = _dil_band()
    for d, o_sc, l_sc in zip(DIL, (o1, o2, o3), (l1, l2, l3)):

        def tiles(g, carry, d=d, o_sc=o_sc, l_sc=l_sc):
            staged = []
            for u in range(unroll):
                rows, prows, has_prev = _dil_rows(g * unroll + u, d)
                q = q_ref[rows, :].astype(MXU_DTYPE)
                kk = jnp.concatenate([k_ref[prows, :], k_ref[rows, :]], axis=0).astype(MXU_DTYPE)
                staged.append((rows, prows, has_prev, _dot_nt(q, kk)))
            for rows, prows, has_prev, s in staged:
                vv = jnp.concatenate([v_ref[prows, :], v_ref[rows, :]], axis=0).astype(MXU_DTYPE)
                ok = band & (is_cur | has_prev)
                s = jnp.where(ok, s * DIL_SCALE, NEG)
                m = jnp.max(s, axis=1, keepdims=True)
                p = jnp.exp(s - m)
                den = jnp.sum(p, axis=1, keepdims=True)
                o_sc[rows, :] = _dot_nn((p / den).astype(MXU_DTYPE), vv)
                l_sc[rows, :] = jnp.broadcast_to(m + jnp.log(den), (HEAD, HEAD))
            return carry

        lax.fori_loop(0, nt // unroll, tiles, 0)

    def merge(i, carry):
        rs = pl.ds(pl.multiple_of(i * DIL_CHUNK, DIL_CHUNK), DIL_CHUNK)
        la, lb, lc = l1[rs, :], l2[rs, :], l3[rs, :]
        m = jnp.maximum(jnp.maximum(la, lb), lc)
        wa, wb, wc = jnp.exp(la - m), jnp.exp(lb - m), jnp.exp(lc - m)
        den = wa + wb + wc
        a = (wa / den) * o1[rs, :] + (wb / den) * o2[rs, :] + (wc / den) * o3[rs, :]
        a_ref[rs, :] = a.astype(a_ref.dtype)
        lse_ref[rs, :] = m + jnp.log(den)
        return carry

    lax.fori_loop(0, q_ref.shape[0] // DIL_CHUNK, merge, 0)


def _dil_fwd(q, k, v):
    T = q.shape[0]
    spec = pl.BlockSpec((T, HEAD), lambda h: (0, h))
    return _pcall(
        functools.partial(_dil_fwd_body, nt=T // HEAD, unroll=16), name="dil_fwd",
        grid=(NH,), in_specs=[spec] * 3, out_specs=[spec] * 2,
        out_shape=[jax.ShapeDtypeStruct((T, 2 * A_W), MXU_DTYPE), jax.ShapeDtypeStruct((T, A_W), F32)],
        scratch_shapes=[pltpu.VMEM((T, HEAD), F32)] * 6,
        compiler_params=pltpu.CompilerParams(dimension_semantics=("parallel",)),
    )(q, k, v)


def _dil_bwd_body(q_ref, k_ref, v_ref, do_ref, a_ref, lse_ref, dq_ref, dk_ref, dv_ref, dl_sc, *, nt, unroll):
    band, is_cur = _dil_band()

    def prep(i, carry):
        rs = pl.ds(pl.multiple_of(i * DIL_CHUNK, DIL_CHUNK), DIL_CHUNK)
        dl = jnp.sum(do_ref[rs, :] * a_ref[rs, :].astype(F32), axis=1, keepdims=True)
        dl_sc[rs, :] = jnp.broadcast_to(dl, (DIL_CHUNK, HEAD))
        zero = jnp.zeros((DIL_CHUNK, HEAD), F32)
        dq_ref[rs, :] = zero
        dk_ref[rs, :] = zero
        dv_ref[rs, :] = zero
        return carry

    lax.fori_loop(0, q_ref.shape[0] // DIL_CHUNK, prep, 0)

    for d in DIL:

        def tiles(g, carry, d=d):
            staged = []
            for u in range(unroll):
                rows, prows, has_prev = _dil_rows(g * unroll + u, d)
                q = q_ref[rows, :].astype(MXU_DTYPE)
                kk = jnp.concatenate([k_ref[prows, :], k_ref[rows, :]], axis=0).astype(MXU_DTYPE)
                vv = jnp.concatenate([v_ref[prows, :], v_ref[rows, :]], axis=0).astype(MXU_DTYPE)
                do = do_ref[rows, :].astype(MXU_DTYPE)
                staged.append((rows, prows, has_prev, q, kk, do, _dot_nt(q, kk), _dot_nt(do, vv)))
            for rows, prows, has_prev, q, kk, do, s, dp in staged:
                lse = lse_ref[rows, :]
                dl = dl_sc[rows, :]
                ok = band & (is_cur | has_prev)
                p = jnp.where(ok, jnp.exp(s * DIL_SCALE - jnp.concatenate([lse, lse], axis=1)), 0.0)
                ds = (p * (dp - jnp.concatenate([dl, dl], axis=1))).astype(MXU_DTYPE)
                dq_ref[rows, :] += _dot_nn(ds, kk) * DIL_SCALE
                dkk = _dot_tn(ds, q) * DIL_SCALE
                dvv = _dot_tn(p.astype(MXU_DTYPE), do)
                dk_ref[rows, :] += dkk[HEAD:, :]
                dv_ref[rows, :] += dvv[HEAD:, :]
                dk_ref[prows, :] += dkk[:HEAD, :]
                dv_ref[prows, :] += dvv[:HEAD, :]
            return carry

        lax.fori_loop(0, nt // unroll, tiles, 0)


def _dil_bwd(q, k, v, dmix, mixed, lse):
    T = q.shape[0]
    spec = pl.BlockSpec((T, HEAD), lambda h: (0, h))
    return _pcall(
        functools.partial(_dil_bwd_body, nt=T // HEAD, unroll=8), name="dil_bwd",
        grid=(NH,), in_specs=[spec] * 6, out_specs=[spec] * 3,
        out_shape=[jax.ShapeDtypeStruct((T, A_W), F32)] * 3,
        scratch_shapes=[pltpu.VMEM((T, HEAD), F32)],
        compiler_params=pltpu.CompilerParams(dimension_semantics=("parallel",)),
    )(q, k, v, dmix, mixed, lse)


MLA_SCALE = (HEAD + ROPE_B) ** -0.5
LOG2E = 1.4426950408889634
MLA_QSCALE = MLA_SCALE * LOG2E
MLA_T = 512
MLA_HP = 4


def _tri(t):
    row = lax.broadcasted_iota(jnp.int32, (t, t), 0)
    col = lax.broadcasted_iota(jnp.int32, (t, t), 1)
    return col <= row


def _lanes(x, n):
    return jnp.tile(x, (1, n // HEAD))


def _mla_fwd_body(q_ref, kv_ref, kr_ref, mixed_ref, o_ref, lse_ref, m_sc, l_sc, acc_sc, *, t, hp):
    del mixed_ref
    qi = pl.program_id(1)
    m_sc[...] = jnp.full(m_sc.shape, NEG, F32)
    l_sc[...] = jnp.zeros(l_sc.shape, F32)
    acc_sc[...] = jnp.zeros(acc_sc.shape, F32)

    def step(j, masked):
        ks = pl.ds(pl.multiple_of(j * t, t), t)
        kr = kr_ref[ks, :]
        logits = []
        for hh in range(hp):
            kcat = jnp.concatenate([kv_ref[ks, 2 * hh * HEAD:(2 * hh + 1) * HEAD], kr], axis=1)
            logits.append(_dot_nt(q_ref[:, hh * QPAD:(hh + 1) * QPAD], kcat))
        for hh in range(hp):
            s = logits[hh]
            if masked:
                s = jnp.where(_tri(t), s, NEG)
            m_prev = m_sc[hh]
            m_new = jnp.maximum(m_prev, jnp.max(s, axis=1, keepdims=True))
            alpha = jnp.exp2(m_prev - m_new)
            p = jnp.exp2(s - _lanes(m_new, t))
            l_sc[hh] = alpha * l_sc[hh] + jnp.sum(p, axis=1, keepdims=True)
            acc_sc[hh] = alpha * acc_sc[hh] + _dot_nn(p.astype(MXU_DTYPE), kv_ref[ks, (2 * hh + 1) * HEAD:(2 * hh + 2) * HEAD])
            m_sc[hh] = m_new

    def off_diag(j, carry):
        step(j, False)
        return carry

    lax.fori_loop(0, qi, off_diag, 0)
    step(qi, True)
    for hh in range(hp):
        l = l_sc[hh]
        o_ref[:, hh * HEAD:(hh + 1) * HEAD] = (acc_sc[hh] / l).astype(o_ref.dtype)
        lse_ref[:, hh * HEAD:(hh + 1) * HEAD] = m_sc[hh] + jnp.log2(l)


def _mla_fwd(qf, kv, kr, mixed):
    T = qf.shape[0]
    t, hp = min(MLA_T, T), MLA_HP
    ng = NH // hp
    return _pcall(
        functools.partial(_mla_fwd_body, t=t, hp=hp), name="mla_fwd",
        grid=(ng, T // t),
        in_specs=[pl.BlockSpec((t, hp * QPAD), lambda g, i: (i, g)),
                  pl.BlockSpec((T, hp * 2 * HEAD), lambda g, i: (0, g)),
                  pl.BlockSpec((T, HEAD), lambda g, i: (0, 0)), ANY],
        out_specs=[pl.BlockSpec((t, hp * HEAD), lambda g, i: (i, ng + g)),
                   pl.BlockSpec((t, hp * HEAD), lambda g, i: (i, g))],
        out_shape=[jax.ShapeDtypeStruct(mixed.shape, mixed.dtype), jax.ShapeDtypeStruct((T, A_W), F32)],
        input_output_aliases={3: 0},
        scratch_shapes=[pltpu.VMEM((hp, t, HEAD), F32)] * 3,
        compiler_params=pltpu.CompilerParams(dimension_semantics=("parallel", "parallel")),
    )(qf, kv, kr, mixed)


def _mla_bwd_body(q_ref, kn_ref, kr_ref, v_ref, do_ref, o_ref, lse_ref, cb, sba, sbb,
                  dq_ref, dkv_ref, dkr_ref, dq_sc, dl_sc, dk_sc, dv_sc, *, t):
    ki = pl.program_id(1)
    nq = q_ref.shape[0] // t

    @pl.when(ki == 0)
    def _():
        def prep(i, carry):
            rs = pl.ds(pl.multiple_of(i * t, t), t)
            dl = jnp.sum(do_ref[rs, :] * o_ref[rs, :].astype(F32), axis=1, keepdims=True)
            dl_sc[rs, :] = jnp.broadcast_to(dl, (t, HEAD))
            dq_sc[rs, :] = jnp.zeros((t, QPAD), F32)
            return carry
        lax.fori_loop(0, nq, prep, 0)

    kcat = jnp.concatenate([kn_ref[...], kr_ref[...]], axis=1)
    v = v_ref[...]
    dk_sc[...] = jnp.zeros(dk_sc.shape, F32)
    dv_sc[...] = jnp.zeros(dv_sc.shape, F32)

    def step(i, masked):
        qs = pl.ds(pl.multiple_of(i * t, t), t)
        q = q_ref[qs, :]
        do = do_ref[qs, :].astype(MXU_DTYPE)
        s = _dot_nt(q, kcat)
        dp = _dot_nt(do, v)
        p = jnp.exp2(s - _lanes(lse_ref[qs, :], t))
        if masked:
            p = jnp.where(_tri(t), p, 0.0)
        ds = (p * (dp - _lanes(dl_sc[qs, :], t))).astype(MXU_DTYPE)
        dv_sc[...] += _dot_tn(p.astype(MXU_DTYPE), do)
        dk_sc[...] += _dot_tn(ds, q)
        dq_sc[qs, :] += _dot_nn(ds, kcat) * MLA_SCALE

    step(ki, True)

    def off_diag(i, carry):
        step(i, False)
        return carry

    lax.fori_loop(ki + 1, nq, off_diag, 0)
    dk = dk_sc[...] * (1.0 / LOG2E)
    dkv_ref[:, 0:HEAD] = dk[:, 0:HEAD].astype(dkv_ref.dtype)
    dkr_ref[...] = dk[:, HEAD:QPAD]
    dkv_ref[:, HEAD:] = dv_sc[...].astype(dkv_ref.dtype)

    @pl.when(ki == nq - 1)
    def _():
        def emit(i, carry):
            rs = pl.ds(pl.multiple_of(i * t, t), t)
            dq_ref[rs, 0:HEAD] = dq_sc[rs, 0:HEAD].astype(dq_ref.dtype)
            dq_ref[rs, HEAD:QPAD] = _rope_t(dq_sc[rs, HEAD:QPAD], cb[rs, :], sba[rs, :], sbb[rs, :],
                                            ROPE_B // 2).astype(dq_ref.dtype)
            return carry
        lax.fori_loop(0, nq, emit, 0)


def _mla_bwd(qf, kv, kr, dmix, mixed, lse, tabs_b):
    T = qf.shape[0]
    t = min(MLA_T, T)
    head = lambda h, j: (0, h)
    b_half = lambda h, j: (0, NH + h)
    kblk = pl.BlockSpec((t, HEAD), lambda h, j: (j, h))
    return _pcall(
        functools.partial(_mla_bwd_body, t=t), name="mla_bwd",
        grid=(NH, T // t),
        in_specs=[pl.BlockSpec((T, QPAD), head), pl.BlockSpec((t, HEAD), lambda h, j: (j, 2 * h)),
                  pl.BlockSpec((t, HEAD), lambda h, j: (j, 0)),
                  pl.BlockSpec((t, HEAD), lambda h, j: (j, 2 * h + 1)),
                  pl.BlockSpec((T, HEAD), b_half), pl.BlockSpec((T, HEAD), b_half),
                  pl.BlockSpec((T, HEAD), head)] + [pl.BlockSpec((T, HEAD), lambda h, j: (0, 0))] * 3,
        out_specs=[pl.BlockSpec((T, QPAD), head), pl.BlockSpec((t, 2 * HEAD), lambda h, j: (j, h)), kblk],
        out_shape=[jax.ShapeDtypeStruct((T, NH * QPAD), MXU_DTYPE), jax.ShapeDtypeStruct((T, 2 * A_W), MXU_DTYPE),
                   jax.ShapeDtypeStruct((T, A_W), F32)],
        scratch_shapes=[pltpu.VMEM((T, QPAD), F32), pltpu.VMEM((T, HEAD), F32), pltpu.VMEM((t, QPAD), F32),
                        pltpu.VMEM((t, HEAD), F32)],
        compiler_params=pltpu.CompilerParams(dimension_semantics=("parallel", "arbitrary")),
    )(qf, kv, kr, kv, dmix, mixed, lse, *tabs_b)


def _local_step(x, pos, target, g1, g2, gq, gkv, g3, g4,
                in_weights, attn_prefetch, attn_weights, mlp_prefetch, mlp_weights,
                down_grad_ready, up_grad_ready, attn_grads_ready):
    T = x.shape[0]
    TR = 256
    mm = functools.partial(_matmul, tm=2048, tn=1024, tk=2048, b_outer=True)
    mm_k = functools.partial(_matmul, tm=1024, tn=1024, tk=2048)
    mm_g = functools.partial(_matmul, tm=1024, tn=1024, tk=4096, b_outer=True)

    inv_a = ROPE_THETA ** (-jnp.arange(0, ROT_A, 2, dtype=F32) / ROT_A)
    inv_b = ROPE_THETA ** (-jnp.arange(0, ROPE_B, 2, dtype=F32) / ROPE_B)
    inv = jnp.stack([jnp.concatenate([inv_a, inv_a, jnp.zeros((HEAD - ROT_A,), F32)]),
                     jnp.concatenate([inv_b, inv_b, jnp.zeros((HEAD - ROPE_B,), F32)])])
    inv = jnp.concatenate([inv, jnp.zeros((6, HEAD), F32)], axis=0)
    tabs = _rowwise(_rope_tab_body, [pos], [inv], [(HEAD, F32)] * 6, [], tr=512, name="rope_tables")

    (h,) = _rowwise(_rms_fwd_body, [x], [g1], [(D_MODEL, MXU_DTYPE)], [], tr=TR, name="rms_in")
    w_proj = in_weights([h, tabs[0]])
    (proj,) = mm(h, w_proj, dims="nt", out_dtypes=[F32], tm=1024, tn=PROJ_TILE, name="proj_in")
    gq = gq + attn_prefetch(proj)
    q, k, v, cqn, ckvn, krope = _rowwise(
        _postproj_body, [proj] + tabs, [gq, gkv],
        [(A_W, F32)] * 3 + [(LORA, MXU_DTYPE)] * 2 + [(HEAD, MXU_DTYPE)], [], tr=TR, name="post_proj")
    mixed, lse_a = _dil_fwd(q, k, v)

    w_uq_p, w_ukv, w_out = attn_weights(cqn)

    def q_epi(acc, cb, sba, sbb):
        cols = []
        for hh in range(acc.shape[1] // QPAD):
            lo = hh * QPAD
            cols += [acc[:, lo:lo + HEAD], _rope(acc[:, lo + HEAD:lo + QPAD], cb, sba, sbb, ROPE_B // 2)]
        return (jnp.concatenate(cols, axis=1) * MLA_QSCALE,)
    (qf,) = mm(cqn, w_uq_p, dims="nn", out_dtypes=[MXU_DTYPE], name="q_up", epi=q_epi, row_extras=tuple(tabs[3:]))
    (kv,) = mm(ckvn, w_ukv, dims="nn", out_dtypes=[MXU_DTYPE], name="kv_up")
    mixed, lse_b = _mla_fwd(qf, kv, krope, mixed)
    (o,) = mm(mixed, w_out, dims="nn", out_dtypes=[F32], name="out_proj", after=mlp_prefetch(mixed))
    x1, h2 = _rowwise(_mid_body, [x, o], [g2, g3], [(D_MODEL, F32), (D_MODEL, MXU_DTYPE)], [], tr=TR, name="mid_norm")

    w_up, w_down = mlp_weights(h2)

    def up_epi(acc):
        r = jnp.maximum(acc, 0.0)
        return r * r, r
    u, r = mm(h2, w_up, dims="nn", out_dtypes=[MXU_DTYPE, MXU_DTYPE], name="mlp_up", epi=up_epi, b_shards=N_CHIPS)
    (dn,) = mm_k(u, w_down, dims="nn", out_dtypes=[F32], name="mlp_down")
    dy, dd, loss8, dg4 = _rowwise(_loss_body, [x1, dn, target], [g4], [(D_MODEL, F32), (D_MODEL, MXU_DTYPE)],
                                  [(8, HEAD), (8, D_MODEL)], tr=TR, name="loss_head")

    def dup_epi(acc, rr):
        return (acc * (2.0 * rr.astype(F32)),)
    (dup,) = mm(dd, w_down, dims="nt", out_dtypes=[MXU_DTYPE], name="d_up", epi=dup_epi, extras=(r,))
    (gw_down,) = mm_g(u, dd, dims="tn", out_dtypes=[WIRE_DTYPE], name="gw_down")
    (dh2,) = mm_k(dup, w_up, dims="nt", out_dtypes=[F32], name="d_h2", b_shards=N_CHIPS,
                  after=down_grad_ready(gw_down))
    (gw_up,) = mm_g(h2, dup, dims="tn", out_dtypes=[WIRE_DTYPE], name="gw_up", out_shards=N_CHIPS)
    g2 = g2 + up_grad_ready(gw_up)
    dx1, do, dg3, dg2 = _rowwise(_bmid_body, [dy, dh2, x1, o], [g2, g3], [(D_MODEL, F32), (D_MODEL, MXU_DTYPE)],
                                 [(8, D_MODEL), (8, D_MODEL)], tr=TR, name="bwd_mid")
    (dmix,) = mm(do, w_out, dims="nt", out_dtypes=[F32], name="d_mixed")
    (gw_out,) = mm_g(mixed, do, dims="tn", out_dtypes=[WIRE_DTYPE], name="gw_out")

    dq_pad, dkv, dkr = _mla_bwd(qf, kv, krope, dmix, mixed, lse_b, tabs[3:])
    (dcqn,) = mm(dq_pad, w_uq_p, dims="nt", out_dtypes=[F32], name="d_cq")
    (gw_uq_p,) = mm_g(cqn, dq_pad, dims="tn", out_dtypes=[WIRE_DTYPE], name="gw_uq")
    (dckvn,) = mm(dkv, w_ukv, dims="nt", out_dtypes=[F32], name="d_ckv")
    (gw_ukv,) = mm_g(ckvn, dkv, dims="tn", out_dtypes=[WIRE_DTYPE], name="gw_ukv")
    gq = gq + attn_grads_ready(gw_out, gw_uq_p, gw_ukv)

    dq_a, dk_a, dv_a = _dil_bwd(q, k, v, dmix, mixed, lse_a)
    dproj, dgq, dgkv = _rowwise(
        _dproj_body, [dq_a, dk_a, dv_a, dcqn, dckvn, proj, dkr] + tabs, [gq, gkv],
        [(PROJ_COLS, MXU_DTYPE)], [(8, LORA), (8, LORA)], tr=TR, name="d_proj")
    (dh,) = mm_k(dproj, w_proj, dims="nn", out_dtypes=[F32], tk=PROJ_TILE, name="d_h")
    (gw_proj,) = mm_g(dproj, h, dims="tn", out_dtypes=[WIRE_DTYPE], tm=PROJ_TILE, name="gw_in")
    dx, dg1 = _rowwise(_bin_body, [dx1, dh, x], [g1], [(D_MODEL, F32)], [(8, D_MODEL)], tr=TR, name="bwd_in")

    small = jnp.concatenate([dg1, dg2, dgq, dgkv, dg3, dg4, loss8], axis=1)
    return dx, gw_proj, small


def _place():
    x, y, c = lax.axis_index("x"), lax.axis_index("y"), lax.axis_index("c")
    chips = [(1 - x, y), (x, 1 - y), (1 - x, 1 - y)]
    return x, y, c, chips


def _cast_place_body(me_ref, w_ref, *rest):
    o_ref = rest[-1]
    o_ref[...] = w_ref[...].astype(o_ref.dtype)


def _cast_place(me_arr, w, name, after=None):
    rows, cols = w.shape
    tr = min(rows, 256)
    after = [] if after is None else [after]
    grid_spec = pltpu.PrefetchScalarGridSpec(
        num_scalar_prefetch=1, grid=(rows // tr,),
        in_specs=[pl.BlockSpec((tr, cols), lambda i, me: (i, 0))] + [ANY] * len(after),
        out_specs=pl.BlockSpec((None, tr, cols), lambda i, me: (me[0], i, 0)))
    return _pcall(
        _cast_place_body, name=name, grid_spec=grid_spec,
        out_shape=jax.ShapeDtypeStruct((N_CHIPS, rows, cols), WIRE_DTYPE),
        compiler_params=pltpu.CompilerParams(dimension_semantics=("parallel",)),
    )(me_arr, w, *after)


def _cast_place_t_body(me_ref, w_ref, o_ref, *, n):
    i = pl.program_id(0)

    @pl.when(i < n)
    def _():
        o_ref[...] = w_ref[...].astype(o_ref.dtype)

    @pl.when(i == n)
    def _():
        o_ref[...] = jnp.zeros_like(o_ref)


def _cast_place_t(me_arr, w_t, name):
    rows, cols = w_t.shape
    n = rows // IN_TR
    grid_spec = pltpu.PrefetchScalarGridSpec(
        num_scalar_prefetch=1, grid=(n + 1,),
        in_specs=[pl.BlockSpec((IN_TR, cols), lambda i, me: (jnp.minimum(i, n - 1), 0))],
        out_specs=pl.BlockSpec((IN_TR, cols), lambda i, me: (jnp.where(i < n, me[0] * n + i, N_CHIPS * n), 0)))
    return _pcall(
        functools.partial(_cast_place_t_body, n=n), name=name, grid_spec=grid_spec,
        out_shape=jax.ShapeDtypeStruct((PROJ_COLS, cols), WIRE_DTYPE),
        compiler_params=pltpu.CompilerParams(dimension_semantics=("arbitrary",)),
    )(me_arr, w_t)


HBM = pl.BlockSpec(memory_space=pltpu.HBM)
SEM = pl.BlockSpec(memory_space=pltpu.SEMAPHORE)
EFFECT = pltpu.SideEffectType.DATAFLOW_SIDE_EFFECTING


def _copy_start(make, arrays, after, name, n_sems):
    n_a = len(arrays)
    after = [] if after is None else [after]

    def body(*refs):
        for send, _ in make(refs[:n_a], refs[-n_a - 3], refs[-n_a - 2]):
            send.start()
        refs[-1][...] = jnp.zeros_like(refs[-1])

    res = _pcall(
        body, name=name,
        in_specs=[HBM] * n_a + [ANY] * len(after),
        out_specs=[SEM, SEM] + [HBM] * n_a + [pl.BlockSpec(memory_space=pltpu.VMEM)],
        out_shape=[pltpu.SemaphoreType.DMA((n_sems,)), pltpu.SemaphoreType.DMA((n_sems,))]
        + [pltpu.HBM(a.shape, a.dtype) for a in arrays] + [jax.ShapeDtypeStruct((8, HEAD), F32)],
        input_output_aliases={i: 2 + i for i in range(n_a)},
        compiler_params=pltpu.CompilerParams(has_side_effects=EFFECT),
    )(*[pltpu.with_memory_space_constraint(a, pltpu.HBM) for a in arrays], *after)
    return (res[0], res[1]), list(res[2:2 + n_a]), res[-1]


def _copy_wait(make, sems, arrays, after, name):
    n_a = len(arrays)
    after = list(after) if isinstance(after, (list, tuple)) else [after]

    def body(*refs):
        for send, recv in make(refs[:n_a], refs[n_a], refs[n_a + 1]):
            send.wait_send()
            recv.wait_recv()

    return list(_pcall(
        body, name=name,
        in_specs=[HBM] * n_a + [SEM, SEM] + [ANY] * len(after), out_specs=[HBM] * n_a,
        out_shape=[pltpu.HBM(a.shape, a.dtype) for a in arrays],
        input_output_aliases={i: i for i in range(n_a)},
        compiler_params=pltpu.CompilerParams(has_side_effects=EFFECT),
    )(*arrays, sems[0], sems[1], *after))


def _slot(buf, chip, half):
    if buf.ndim == 2:
        hc = buf.shape[1] // 2
        return buf.at[pl.ds(pl.multiple_of(chip * IN_SHARD, 16), IN_SHARD), pl.ds(pl.multiple_of(half * hc, HEAD), hc)]
    hr = buf.shape[1] // 2
    return buf.at[chip, pl.ds(pl.multiple_of(half * hr, 16), hr)]


def _ag_descs(bufs, send_sems, recv_sems):
    x, y, c, chips = _place()
    me = 2 * x + y
    out = []
    for w, buf in enumerate(bufs):
        for j, (px, py) in enumerate(chips):
            mk = lambda ref, w=w, j=j, px=px, py=py: pltpu.make_async_remote_copy(
                src_ref=ref, dst_ref=ref, send_sem=send_sems.at[w * 3 + j], recv_sem=recv_sems.at[w * 3 + j],
                device_id=(px, py, c), device_id_type=MESH)
            out.append((mk(_slot(buf, me, c)), mk(_slot(buf, 2 * px + py, c))))
    return out


def _fw_descs(bufs, send_sems, recv_sems):
    x, y, c, chips = _place()
    out = []
    for w, buf in enumerate(bufs):
        for j, (px, py) in enumerate(chips):
            def mk(which, w=w, j=j, buf=buf, px=px, py=py):
                ref = _slot(buf, 2 * px + py, which)
                return pltpu.make_async_remote_copy(
                    src_ref=ref, dst_ref=ref, send_sem=send_sems.at[w * 3 + j], recv_sem=recv_sems.at[w * 3 + j],
                    device_id=(x, y, 1 - c), device_id_type=MESH)
            out.append((mk(c), mk(1 - c)))
    return out


def _sc_descs(refs, send_sems, recv_sems):
    n_w = len(refs) // 2
    x, y, c, chips = _place()
    me = 2 * x + y
    out = []
    for w in range(n_w):
        for j, (px, py) in enumerate(chips):
            d = pltpu.make_async_remote_copy(
                src_ref=refs[w].at[2 * px + py], dst_ref=refs[n_w + w].at[me],
                send_sem=send_sems.at[w * 3 + j], recv_sem=recv_sems.at[w * 3 + j],
                device_id=(px, py, c), device_id_type=MESH)
            out.append((d, d))
    return out


def _pair_descs(src_of):
    def make(refs, send_sems, recv_sems):
        n_w = len(refs) // 2
        x, y, c, _ = _place()
        out = []
        for w in range(n_w):
            d = pltpu.make_async_remote_copy(
                src_ref=src_of(refs[w], c), dst_ref=refs[n_w + w],
                send_sem=send_sems.at[w], recv_sem=recv_sems.at[w],
                device_id=(x, y, 1 - c), device_id_type=MESH)
            out.append((d, d))
        return out
    return make


_EX_DESCS = _pair_descs(lambda g4, c: g4.at[:, 1 - c])
_SW_DESCS = _pair_descs(lambda half, c: half)
_EXT_DESCS = _pair_descs(lambda g, c: g.at[pl.ds(0, IN_COLS),
                                           pl.ds(pl.multiple_of((1 - c) * (D_MODEL // 2), HEAD), D_MODEL // 2)])


def _sm_descs(refs, send_sems, recv_sems):
    buf = refs[0]
    rows8 = buf.shape[0] // N_DEV
    x, y, c, _ = _place()
    flip = lambda v, d: 1 - v if d else v
    blk = lambda px, py, pc: buf.at[pl.ds(pl.multiple_of((4 * px + 2 * py + pc) * rows8, 8), rows8)]
    out = []
    for k in range(1, N_DEV):
        px, py, pc = flip(x, k & 4), flip(y, k & 2), flip(c, k & 1)
        mk = lambda ref, k=k, px=px, py=py, pc=pc: pltpu.make_async_remote_copy(
            src_ref=ref, dst_ref=ref, send_sem=send_sems.at[k - 1], recv_sem=recv_sems.at[k - 1],
            device_id=(px, py, pc), device_id_type=MESH)
        out.append((mk(blk(x, y, c)), mk(blk(px, py, pc))))
    return out


def _place_rows_body(i_ref, x_ref, o_ref):
    o_ref[...] = x_ref[...]


def _place_rows(i_arr, x, n_blocks, name):
    r, n = x.shape
    grid_spec = pltpu.PrefetchScalarGridSpec(
        num_scalar_prefetch=1, grid=(1,),
        in_specs=[pl.BlockSpec((r, n), lambda g, i: (0, 0))],
        out_specs=pl.BlockSpec((r, n), lambda g, i: (i[0], 0)))
    return _pcall(_place_rows_body, name=name, grid_spec=grid_spec,
                  out_shape=jax.ShapeDtypeStruct((n_blocks * r, n), x.dtype))(i_arr, x)


def _ag_forward_body(*refs, n_w):
    bufs = refs[n_w:2 * n_w]
    send_sems, recv_sems = refs[2 * n_w:]
    pairs = _fw_descs(bufs, send_sems, recv_sems)
    for fw, _ in pairs:
        fw.start()
    for fw, back in pairs:
        back.wait_recv()
        fw.wait_send()


def _ag_forward(bufs, tag):
    n_w = len(bufs)
    return list(_pcall(
        functools.partial(_ag_forward_body, n_w=n_w), name="weight_allgather_forward_" + tag,
        in_specs=[ANY] * n_w, out_specs=[ANY] * n_w,
        out_shape=[jax.ShapeDtypeStruct(b.shape, b.dtype) for b in bufs],
        input_output_aliases={w: w for w in range(n_w)},
        scratch_shapes=[pltpu.SemaphoreType.DMA((3 * n_w,))] * 2,
    )(*bufs))


def _pair_add_body(c_ref, mine_ref, theirs_ref, o_ref):
    o_ref[...] = (mine_ref[...].astype(F32) + theirs_ref[...].astype(F32)).astype(o_ref.dtype)


def _pair_add(c_arr, g4, recv, name):
    _, _, hr, cols = g4.shape
    tr = min(hr, 256)
    grid_spec = pltpu.PrefetchScalarGridSpec(
        num_scalar_prefetch=1, grid=(N_CHIPS, hr // tr),
        in_specs=[pl.BlockSpec((None, None, tr, cols), lambda s, i, c: (s, c[0], i, 0)),
                  pl.BlockSpec((None, tr, cols), lambda s, i, c: (s, i, 0))],
        out_specs=pl.BlockSpec((None, tr, cols), lambda s, i, c: (s, i, 0)))
    return _pcall(
        _pair_add_body, name=name, grid_spec=grid_spec,
        out_shape=jax.ShapeDtypeStruct(recv.shape, recv.dtype),
        compiler_params=pltpu.CompilerParams(dimension_semantics=("parallel", "parallel")),
    )(c_arr, g4, recv)


def _pair_add_t(c_arr, g, recv, name):
    rows, hc = recv.shape
    grid_spec = pltpu.PrefetchScalarGridSpec(
        num_scalar_prefetch=1, grid=(rows // IN_TR,),
        in_specs=[pl.BlockSpec((IN_TR, hc), lambda i, c: (i, c[0])), pl.BlockSpec((IN_TR, hc), lambda i, c: (i, 0))],
        out_specs=pl.BlockSpec((IN_TR, hc), lambda i, c: (i, 0)))
    return _pcall(
        _pair_add_body, name=name, grid_spec=grid_spec,
        out_shape=jax.ShapeDtypeStruct(recv.shape, recv.dtype),
        compiler_params=pltpu.CompilerParams(dimension_semantics=("parallel",)),
    )(c_arr, g, recv)


def _sum4_body(me_ref, p_ref, l0, l1, l2, l3, o_ref):
    me = me_ref[0]
    t = [jnp.where(me == j, p_ref[...], l[...]).astype(F32) for j, l in enumerate((l0, l1, l2, l3))]
    o_ref[...] = ((t[0] + t[1]) + t[2]) + t[3]


def _sum4(me_arr, part, landed, name):
    _, hr, cols = part.shape
    tr = IN_TR if hr == IN_SHARD else min(hr, 256)

    def slot(j):
        return lambda i, me: (jnp.where(me[0] == j, (j + 1) % N_CHIPS, j), i, 0)

    grid_spec = pltpu.PrefetchScalarGridSpec(
        num_scalar_prefetch=1, grid=(hr // tr,),
        in_specs=[pl.BlockSpec((None, tr, cols), lambda i, me: (me[0], i, 0))]
        + [pl.BlockSpec((None, tr, cols), slot(j)) for j in range(N_CHIPS)],
        out_specs=pl.BlockSpec((tr, cols), lambda i, me: (i, 0)))
    return _pcall(
        _sum4_body, name=name, grid_spec=grid_spec,
        out_shape=jax.ShapeDtypeStruct((hr, cols), F32),
        compiler_params=pltpu.CompilerParams(dimension_semantics=("parallel",)),
    )(me_arr, part, landed, landed, landed, landed)


def _adamw(w, g, m, v):
    m = ADAM_B1 * m + (1.0 - ADAM_B1) * g
    v = ADAM_B2 * v + (1.0 - ADAM_B2) * (g * g)
    m_hat = m / (1.0 - ADAM_B1 ** ADAM_STEP)
    v_hat = v / (1.0 - ADAM_B2 ** ADAM_STEP)
    delta = -ADAM_LR * (m_hat / (jnp.sqrt(v_hat) + ADAM_EPS) + ADAM_WD * w)
    return delta, m, v


def _adamw_half_body(h_ref, w_ref, g_in_ref, m_ref, v_ref, *rest):
    g_ref, d_ref, nm_ref, nv_ref, done_ref = rest[-5:]
    done_ref[...] = jnp.zeros_like(done_ref)
    g = g_in_ref[...]
    g_ref[...] = g
    d, m, v = _adamw(w_ref[...], g, m_ref[...], v_ref[...])
    d_ref[...] = d
    nm_ref[...] = m
    nv_ref[...] = v


def _adamw_half(h_arr, w, g_half, m, v, prev, name):
    rows, cols = w.shape
    if g_half.shape[0] == rows:
        tr, nh = IN_TR, rows // IN_TR
        at_half = pl.BlockSpec((tr, cols // 2), lambda i, h: (i, h[0]))
        g_spec = pl.BlockSpec((tr, cols // 2), lambda i, h: (i, 0))
    else:
        tr = min(rows // 2, 128)
        nh = (rows // 2) // tr
        at_half = pl.BlockSpec((tr, cols), lambda i, h: (h[0] * nh + i, 0))
        g_spec = pl.BlockSpec((tr, cols), lambda i, h: (i, 0))
    grid_spec = pltpu.PrefetchScalarGridSpec(
        num_scalar_prefetch=1, grid=(nh,),
        in_specs=[at_half, g_spec, at_half, at_half] + [ANY] * len(prev),
        out_specs=[at_half] * 4 + [pl.BlockSpec((8, HEAD), lambda i, h: (0, 0))])
    return list(_pcall(
        _adamw_half_body, name=name, grid_spec=grid_spec,
        out_shape=[jax.ShapeDtypeStruct(w.shape, F32)] * 4 + [jax.ShapeDtypeStruct((8, HEAD), F32)],
        input_output_aliases={5 + k: k for k in range(len(prev))},
        compiler_params=pltpu.CompilerParams(dimension_semantics=("arbitrary",)),
    )(h_arr, w, g_half, m, v, *prev))


def _small_update_body(gath_ref, w_ref, m_ref, v_ref, g_ref, d_ref, nm_ref, nv_ref, loss_ref, *, n_gain):
    tot = gath_ref[0:1, :]
    for i in range(1, gath_ref.shape[0]):
        tot = tot + gath_ref[i:i + 1, :]
    g = tot[:, 0:n_gain]
    g_ref[...] = g
    d, m, v = _adamw(w_ref[...], g, m_ref[...], v_ref[...])
    d_ref[...] = d
    nm_ref[...] = m
    nv_ref[...] = v
    loss_ref[...] = (0.5 / D_MODEL) * jnp.sum(tot[:, n_gain:n_gain + HEAD], axis=1, keepdims=True) * jnp.ones((1, HEAD), F32)


def _small_update(gath, w, m, v):
    n_gain = w.shape[1]
    vm = pl.BlockSpec(memory_space=pltpu.VMEM)
    return _pcall(
        functools.partial(_small_update_body, n_gain=n_gain), name="gain_update",
        in_specs=[vm] * 4, out_specs=[vm] * 5,
        out_shape=[jax.ShapeDtypeStruct((1, n_gain), F32)] * 4 + [jax.ShapeDtypeStruct((1, HEAD), F32)],
    )(gath, w, m, v)


def kernel(x, positions, norm_attn_pre, norm_attn_post, w_in, q_latent_norm, kv_latent_norm, w_uq, w_ukv, w_out, norm_mlp_pre, norm_mlp_post, w_up, w_down, loss_target, m_norm_attn_pre, m_norm_attn_post, m_w_in, m_q_latent_norm, m_kv_latent_norm, m_w_uq, m_w_ukv, m_w_out, m_norm_mlp_pre, m_norm_mlp_post, m_w_up, m_w_down, v_norm_attn_pre, v_norm_attn_post, v_w_in, v_q_latent_norm, v_kv_latent_norm, v_w_uq, v_w_ukv, v_w_out, v_norm_mlp_pre, v_norm_mlp_post, v_w_up, v_w_down):
    T = x.shape[1]
    c_arr = lax.axis_index("c").astype(jnp.int32).reshape(1)
    me_arr = (2 * lax.axis_index("x") + lax.axis_index("y")).astype(jnp.int32).reshape(1)
    names = ["w_in", "w_uq", "w_ukv", "w_out", "w_up", "w_down"]

    transposed = lambda a: jnp.swapaxes(a, 1, 2)
    mats = [transposed(w_in)[0], w_uq[0], w_ukv[0], w_out[0], w_up[0], w_down[0]]
    me8_arr = (4 * lax.axis_index("x") + 2 * lax.axis_index("y") + lax.axis_index("c")).astype(jnp.int32).reshape(1)
    col_major = lambda g: jnp.transpose(g, (1, 0, 2)).reshape(g.shape[1], N_CHIPS * g.shape[2])
    cast = lambda a: a.astype(MXU_DTYPE)
    to_shards = lambda g: jnp.transpose(g.reshape(g.shape[0], N_CHIPS, g.shape[1] // N_CHIPS), (1, 0, 2))
    halved = lambda g: g.reshape(N_CHIPS, 2, g.shape[1] // 2, g.shape[2])
    empty = lambda a, shape=None: lax.empty(a.shape if shape is None else shape, a.dtype)

    sem_in, buf_in, going = _copy_start(_ag_descs, [_cast_place_t(me_arr, mats[0], "cast_w_in")], None,
                                        "weight_allgather_start_in", 3)
    placed = [_cast_place(me_arr, w, "cast_" + n, going) for w, n in zip(mats[1:], names[1:])]
    sem_att, buf_att, going = _copy_start(_ag_descs, placed[:3], going, "weight_allgather_start_attn", 9)
    sem_mlp, buf_mlp, started = _copy_start(_ag_descs, placed[3:], going, "weight_allgather_start_mlp", 6)

    going_on = {}

    def in_weights(after):
        (win_g,) = _ag_forward(_copy_wait(_ag_descs, sem_in, buf_in, after, "weight_allgather_wait_in"), "in")
        return cast(win_g)

    def attn_prefetch(after):
        landed = _copy_wait(_ag_descs, sem_att, buf_att, after, "weight_allgather_wait_attn")
        going_on["fw_attn"] = _copy_start(_fw_descs, landed, None, "weight_allgather_forward_start_attn", 9)
        return going_on["fw_attn"][-1][0:1, 0:1]

    def attn_weights(after):
        sems, bufs, _ = going_on["fw_attn"]
        wuq_g, wukv_g, wout_g = _copy_wait(_fw_descs, sems, bufs, after, "weight_allgather_forward_wait_attn")
        wuq_full = col_major(wuq_g).reshape(LORA, NH, HEAD + ROPE_B)
        w_uq_p = jnp.pad(wuq_full, ((0, 0), (0, 0), (0, QPAD - HEAD - ROPE_B))).reshape(LORA, NH * QPAD)
        return cast(w_uq_p), cast(col_major(wukv_g)), cast(wout_g.reshape(2 * A_W, D_MODEL))

    def mlp_prefetch(after):
        landed = _copy_wait(_ag_descs, sem_mlp, buf_mlp, after, "weight_allgather_wait_mlp")
        going_on["fw"] = _copy_start(_fw_descs, landed, None, "weight_allgather_forward_start_mlp", 6)
        return going_on["fw"][-1]

    def mlp_weights(after):
        sems, bufs, _ = going_on["fw"]
        wup_g, wdown_g = _copy_wait(_fw_descs, sems, bufs, after, "weight_allgather_forward_wait_mlp")
        return cast(wup_g), cast(wdown_g.reshape(D_FF, D_MODEL))

    def exchange_start(g4s, tag):
        lands = [empty(g, (g.shape[0],) + g.shape[2:]) for g in g4s]
        return _copy_start(_EX_DESCS, g4s + lands, None, "grad_pair_exchange_start_" + tag, len(g4s))

    def exchange_finish(started_ex, after, ns, tag):
        sems, arrs, _ = started_ex
        arrs = _copy_wait(_EX_DESCS, sems, arrs, after, "grad_pair_exchange_wait_" + tag)
        n = len(ns)
        return [_pair_add(c_arr, g4, r, "pair_add_" + nm) for g4, r, nm in zip(arrs[:n], arrs[n:], ns)]

    def scatter_start(parts, after, tag):
        return _copy_start(_sc_descs, parts + [empty(p) for p in parts], after, "grad_scatter_start_" + tag,
                           3 * len(parts))

    def scatter_finish(started_sc, after, tag):
        sems, arrs, _ = started_sc
        arrs = _copy_wait(_sc_descs, sems, arrs, after, "grad_scatter_wait_" + tag)
        return arrs[:len(arrs) // 2], arrs[len(arrs) // 2:]

    def down_grad_ready(gw_down):
        going_on["x_down"] = exchange_start([halved(gw_down.reshape(N_CHIPS, D_MODEL, D_MODEL))], "down")
        return going_on["x_down"][-1]

    def up_grad_ready(gw_up):
        going_on["x_up"] = exchange_start([halved(gw_up)], "up")
        parts = exchange_finish(going_on["x_down"], going_on["x_up"][-1], names[5:], "down")
        going_on["s_down"] = scatter_start(parts, started, "down")
        return going_on["s_down"][-1][0:1, 0:1]

    def attn_grads_ready(gw_out, gw_uq_p, gw_ukv):
        gw_uq = to_shards(gw_uq_p.reshape(LORA, NH, QPAD)[:, :, :HEAD + ROPE_B].reshape(LORA, NH * (HEAD + ROPE_B)))
        full4 = [halved(g) for g in (gw_uq, to_shards(gw_ukv), gw_out.reshape(N_CHIPS, LORA, D_MODEL))]
        x_attn = exchange_start(full4, "attn")
        parts_up = exchange_finish(going_on["x_up"], x_attn[-1], names[4:5], "up")
        parts = exchange_finish(x_attn, parts_up[0], names[1:4], "attn") + parts_up
        going_on["s_rest"] = scatter_start(parts, going_on["s_down"][-1], "attn_up")
        return going_on["s_rest"][-1][0:1, 0:1]

    dx, gw_proj, small = _local_step(
        x[0], positions[0].astype(F32).reshape(T, 1), loss_target[0],
        norm_attn_pre + started[0:1, 0:1], norm_attn_post, q_latent_norm, kv_latent_norm, norm_mlp_pre, norm_mlp_post,
        in_weights, attn_prefetch, attn_weights, mlp_prefetch, mlp_weights,
        down_grad_ready, up_grad_ready, attn_grads_ready)

    ms = [transposed(m_w_in)[0], m_w_uq[0], m_w_ukv[0], m_w_out[0], m_w_up[0], m_w_down[0]]
    vs = [transposed(v_w_in)[0], v_w_uq[0], v_w_ukv[0], v_w_out[0], v_w_up[0], v_w_down[0]]
    sib_arr = 1 - c_arr

    def finish(parts, landed, lo, hi, tag):
        sl = slice(lo, hi)
        halves = [_sum4(me_arr, p, l, "chip_sum_" + n) for p, l, n in zip(parts, landed, names[sl])]
        n = len(halves)
        sems, arrs, _ = _copy_start(_SW_DESCS, halves + [empty(h) for h in halves], None,
                                    "grad_pair_swap_start_" + tag, n)
        own = [_adamw_half(c_arr, w, g, m, v, [], "adamw_own_" + nm)
               for w, g, m, v, nm in zip(mats[sl], arrs[:n], ms[sl], vs[sl], names[sl])]
        arrs = _copy_wait(_SW_DESCS, sems, arrs, own[-1][4], "grad_pair_swap_wait_" + tag)
        return [_adamw_half(sib_arr, w, g, m, v, prev[:4], "adamw_sib_" + nm)
                for w, g, m, v, prev, nm in zip(mats[sl], arrs[n:], ms[sl], vs[sl], own, names[sl])]

    sem_small, (gath,), small_going = _copy_start(
        _sm_descs, [_place_rows(me8_arr, small, N_DEV, "place_small")], None, "small_allgather_start", N_DEV - 1)
    sems, arrs, _ = _copy_start(_EXT_DESCS, [gw_proj, lax.empty((IN_COLS, D_MODEL // 2), WIRE_DTYPE)], None,
                                "grad_pair_exchange_start_in", 1)
    gw_proj, from_sib = _copy_wait(_EXT_DESCS, sems, arrs, small_going, "grad_pair_exchange_wait_in")
    part_in = _pair_add_t(c_arr, gw_proj, from_sib, "pair_add_w_in").reshape(N_CHIPS, IN_SHARD, D_MODEL // 2)
    s_in = scatter_start([part_in], None, "in")
    parts_rest, landed_rest = scatter_finish(going_on["s_rest"], s_in[-1], "attn_up")
    parts_down, landed_down = scatter_finish(going_on["s_down"], landed_rest[0], "down")
    upd_rest = finish(parts_rest + parts_down, landed_rest + landed_down, 1, 6, "rest")
    parts_in, landed_in = scatter_finish(s_in, upd_rest[-1][0], "in")
    upd = finish(parts_in, landed_in, 0, 1, "in") + upd_rest
    grads = [u[0] for u in upd]

    (gath,) = _copy_wait(_sm_descs, sem_small, [gath], grads[0], "small_allgather_wait")
    gains = [norm_attn_pre, norm_attn_post, q_latent_norm, kv_latent_norm, norm_mlp_pre, norm_mlp_post]
    gm = [m_norm_attn_pre, m_norm_attn_post, m_q_latent_norm, m_kv_latent_norm, m_norm_mlp_pre, m_norm_mlp_post]
    gv = [v_norm_attn_pre, v_norm_attn_post, v_q_latent_norm, v_kv_latent_norm, v_norm_mlp_pre, v_norm_mlp_post]
    cat = lambda xs: jnp.concatenate(xs, axis=1)
    g_s, d_s, m_s, v_s, loss_v = _small_update(gath, cat(gains), cat(gm), cat(gv))
    widths = [a.shape[1] for a in gains]
    offs = [sum(widths[:i]) for i in range(len(widths))]
    split = lambda a: [a[:, o:o + w] for o, w in zip(offs, widths)]
    g_gain, d_gain, m_gain, v_gain = split(g_s), split(d_s), split(m_s), split(v_s)

    def ordered(gain_list, mat_list):
        gl, ml = gain_list, [transposed(mat_list[0][None])] + [a[None] for a in mat_list[1:]]
        return [gl[0], gl[1], ml[0], gl[2], gl[3], ml[1], ml[2], ml[3], gl[4], gl[5], ml[4], ml[5]]

    loss = loss_v[0, 0]
    return (loss, dx[None],
            *ordered(g_gain, grads),
            *ordered(d_gain, [u[1] for u in upd]),
            *ordered(m_gain, [u[2] for u in upd]),
            *ordered(v_gain, [u[3] for u in upd]))
```

```python
import functools

import jax
import jax.numpy as jnp
from jax import lax
from jax.experimental import pallas as pl
from jax.experimental.pallas import tpu as pltpu

F32 = jnp.float32
BF16 = jnp.bfloat16
MXU_DTYPE = jnp.bfloat16
WIRE_DTYPE = jnp.bfloat16

D_MODEL = 2048
HEAD = 128
NH = 8
A_W = NH * HEAD
LORA = 512
ROPE_B = 64
QPAD = 256
MAIN_COLS = 3 * A_W + 2 * LORA
IN_COLS = MAIN_COLS + ROPE_B
PROJ_COLS = MAIN_COLS + HEAD
PROJ_TILE = PROJ_COLS // 3
IN_SHARD = 1040
IN_TR = 208
D_FF = 4 * D_MODEL
DIL = (1, 4, 16)
ROT_A = 32
ROPE_THETA = 500000.0
EPS = 1e-6
NEG = -1e30
N_CHIPS = 4
N_DEV = 8

ADAM_LR = 0.001
ADAM_B1 = 0.9
ADAM_B2 = 0.999
ADAM_EPS = 1e-08
ADAM_WD = 0.01
ADAM_STEP = 10

MESH = pl.DeviceIdType.MESH
ANY = pl.BlockSpec(memory_space=pl.ANY)


def _pcall(body, **kw):
    return pl.pallas_call(body, **kw)


_DIMS = {
    "nn": (((1,), (0,)), ((), ())),
    "nt": (((1,), (1,)), ((), ())),
    "tn": (((0,), (0,)), ((), ())),
}


def _mm_body(*refs, dims, nk, epi, n_extra, n_after, n_out):
    a_ref, b_ref = refs[0], refs[1]
    extra = refs[2:2 + n_extra]
    outs = refs[2 + n_extra + n_after:2 + n_extra + n_after + n_out]
    part = lax.dot_general(a_ref[...], b_ref[...], _DIMS[dims], preferred_element_type=F32)

    def finish(acc):
        res = epi(acc, *[r[...] for r in extra]) if epi is not None else (acc,)
        for o_ref, o in zip(outs, res):
            o_ref[...] = o.astype(o_ref.dtype)

    if nk == 1:
        finish(part)
        return
    acc_ref = refs[-1]
    k = pl.program_id(2)

    @pl.when(k == 0)
    def _():
        acc_ref[...] = part

    @pl.when(k > 0)
    def _():
        acc_ref[...] += part

    @pl.when(k == nk - 1)
    def _():
        finish(acc_ref[...])


def _matmul(a, b, *, dims, out_dtypes, tm, tn, tk, name, epi=None, extras=(), row_extras=(), b_outer=False,
            b_shards=0, out_shards=0, after=None):
    if b_shards:
        assert dims in ("nn", "nt") and b.shape[0] == b_shards
        b2 = (b.shape[1], b_shards * b.shape[2])
    else:
        b2 = b.shape
    if dims == "nn":
        (M, K), (K2, N) = a.shape, b2
    elif dims == "nt":
        (M, K), (N, K2) = a.shape, b2
    else:
        (K, M), (K2, N) = a.shape, b2
    assert K == K2, (a.shape, b.shape, dims)
    tm, tn, tk = min(tm, M), min(tn, N), min(tk, K)
    assert M % tm == 0 and N % tn == 0 and K % tk == 0, (name, M, N, K, tm, tn, tk)
    nk = K // tk

    def at(f):
        if b_outer:
            return lambda j, i, k: f(i, j, k)
        return f

    a_spec = {"nn": pl.BlockSpec((tm, tk), at(lambda i, j, k: (i, k))),
              "nt": pl.BlockSpec((tm, tk), at(lambda i, j, k: (i, k))),
              "tn": pl.BlockSpec((tk, tm), at(lambda i, j, k: (k, i)))}[dims]
    b_spec = {"nn": pl.BlockSpec((tk, tn), at(lambda i, j, k: (k, j))),
              "nt": pl.BlockSpec((tn, tk), at(lambda i, j, k: (j, k))),
              "tn": pl.BlockSpec((tk, tn), at(lambda i, j, k: (k, j)))}[dims]
    if b_shards:
        per = b.shape[2] // (tn if dims == "nn" else tk)
        assert per >= 1 and b.shape[2] % (tn if dims == "nn" else tk) == 0
        b_spec = {"nn": pl.BlockSpec((None, tk, tn), at(lambda i, j, k: (j // per, k, j % per))),
                  "nt": pl.BlockSpec((None, tn, tk), at(lambda i, j, k: (k // per, j, k % per)))}[dims]
    o_spec = pl.BlockSpec((tm, tn), at(lambda i, j, k: (i, j)))
    o_shape = (M, N)
    if out_shards:
        assert not extras and N % out_shards == 0 and (N // out_shards) % tn == 0
        o_per = (N // out_shards) // tn
        o_spec = pl.BlockSpec((None, tm, tn), at(lambda i, j, k: (j // o_per, i, j % o_per)))
        o_shape = (out_shards, M, N // out_shards)
    r_specs = [pl.BlockSpec((tm, r.shape[1]), at(lambda i, j, k: (i, 0))) for r in row_extras]
    after = [] if after is None else [after]
    body = functools.partial(_mm_body, dims=dims, nk=nk, epi=epi, n_extra=len(extras) + len(row_extras),
                             n_after=len(after), n_out=len(out_dtypes))
    res = _pcall(
        body, name=name,
        grid=(N // tn, M // tm, nk) if b_outer else (M // tm, N // tn, nk),
        in_specs=[a_spec, b_spec] + [o_spec] * len(extras) + r_specs + [ANY] * len(after),
        out_specs=[o_spec] * len(out_dtypes),
        out_shape=[jax.ShapeDtypeStruct(o_shape, dt) for dt in out_dtypes],
        scratch_shapes=[pltpu.VMEM((tm, tn), F32)] if nk > 1 else [],
        compiler_params=pltpu.CompilerParams(
            dimension_semantics=("parallel", "parallel", "arbitrary")),
    )(a, b, *extras, *row_extras, *after)
    return list(res)


def _rowwise(body, row_ins, vec_ins, row_outs, acc_outs, *, tr, name):
    T = row_ins[0].shape[0]
    tr = min(tr, T)
    assert T % tr == 0
    in_specs = [pl.BlockSpec((tr, a.shape[1]), lambda i: (i, 0)) for a in row_ins]
    in_specs += [pl.BlockSpec(a.shape, lambda i: (0, 0)) for a in vec_ins]
    out_specs = [pl.BlockSpec((tr, w), lambda i: (i, 0)) for (w, _) in row_outs]
    out_specs += [pl.BlockSpec(s, lambda i: (0, 0)) for s in acc_outs]
    out_shape = [jax.ShapeDtypeStruct((T, w), dt) for (w, dt) in row_outs]
    out_shape += [jax.ShapeDtypeStruct(s, F32) for s in acc_outs]
    sem = "arbitrary" if acc_outs else "parallel"
    return list(_pcall(
        body, name=name, grid=(T // tr,), in_specs=in_specs, out_specs=out_specs,
        out_shape=out_shape,
        compiler_params=pltpu.CompilerParams(dimension_semantics=(sem,)),
    )(*row_ins, *vec_ins))


def _rstd(x):
    return lax.rsqrt(jnp.mean(x * x, axis=-1, keepdims=True) + EPS)


def _rms_bwd(x, rstd, dyg):
    xh = x * rstd
    return rstd * (dyg - xh * jnp.mean(dyg * xh, axis=-1, keepdims=True)), xh


def _fold8(v):
    r, w = v.shape
    return jnp.sum(v.reshape(r // 8, 8, w), axis=0)


def _acc(ref, val):
    first = pl.program_id(0) == 0

    @pl.when(first)
    def _():
        ref[...] = val

    @pl.when(jnp.logical_not(first))
    def _():
        ref[...] += val


def _rope(x, c, sa, sb, half):
    return x * c + pltpu.roll(x, HEAD - half, 1) * sa + pltpu.roll(x, half, 1) * sb


def _rope_t(dy, c, sa, sb, half):
    return dy * c - pltpu.roll(dy, HEAD - half, 1) * sa - pltpu.roll(dy, half, 1) * sb


def _rope_tab_body(pos_ref, inv_ref, ca, saa, sab, cb, sba, sbb):
    pos = pos_ref[...]
    lane = lax.broadcasted_iota(jnp.int32, (pos.shape[0], HEAD), 1)
    ang_a = pos * inv_ref[0:1, :]
    ang_b = pos * inv_ref[1:2, :]
    c, s = jnp.cos(ang_a), jnp.sin(ang_a)
    ha = ROT_A // 2
    ca[...] = jnp.where(lane < ROT_A, c, 1.0)
    saa[...] = jnp.where(lane < ha, -s, 0.0)
    sab[...] = jnp.where((lane >= ha) & (lane < ROT_A), s, 0.0)
    c, s = jnp.cos(ang_b), jnp.sin(ang_b)
    hb = ROPE_B // 2
    cb[...] = jnp.where(lane < ROPE_B, c, 1.0)
    sba[...] = jnp.where(lane < hb, -s, 0.0)
    sbb[...] = jnp.where((lane >= hb) & (lane < ROPE_B), s, 0.0)


def _rms_fwd_body(x_ref, g_ref, h_ref):
    x = x_ref[...]
    h_ref[...] = ((x * _rstd(x)) * g_ref[...]).astype(h_ref.dtype)


def _postproj_body(p_ref, ca, saa, sab, cb, sba, sbb, gq_ref, gkv_ref,
                   q_ref, k_ref, v_ref, cqn_ref, ckvn_ref, krope_ref):
    c, sa, sb = ca[...], saa[...], sab[...]
    for h in range(NH):
        lo = h * HEAD
        q_ref[:, lo:lo + HEAD] = _rope(p_ref[:, lo:lo + HEAD], c, sa, sb, ROT_A // 2).astype(q_ref.dtype)
        k_ref[:, lo:lo + HEAD] = _rope(p_ref[:, A_W + lo:A_W + lo + HEAD], c, sa, sb, ROT_A // 2).astype(k_ref.dtype)
    v_ref[...] = p_ref[:, 2 * A_W:3 * A_W].astype(v_ref.dtype)
    cq = p_ref[:, 3 * A_W:3 * A_W + LORA]
    cqn_ref[...] = ((cq * _rstd(cq)) * gq_ref[...]).astype(cqn_ref.dtype)
    ckv = p_ref[:, 3 * A_W + LORA:MAIN_COLS]
    ckvn_ref[...] = ((ckv * _rstd(ckv)) * gkv_ref[...]).astype(ckvn_ref.dtype)
    krope_ref[...] = _rope(p_ref[:, MAIN_COLS:PROJ_COLS], cb[...], sba[...], sbb[...], ROPE_B // 2).astype(krope_ref.dtype)


def _mid_body(x_ref, o_ref, g2_ref, g3_ref, x1_ref, h2_ref):
    o = o_ref[...]
    x1 = x_ref[...] + (o * _rstd(o)) * g2_ref[...]
    x1_ref[...] = x1
    h2_ref[...] = ((x1 * _rstd(x1)) * g3_ref[...]).astype(h2_ref.dtype)


def _loss_body(x1_ref, d_ref, t_ref, g4_ref, dy_ref, dd_ref, loss_ref, dg4_ref):
    d = d_ref[...]
    rstd = _rstd(d)
    y = x1_ref[...] + (d * rstd) * g4_ref[...]
    e = y - t_ref[...]
    dy = e * (1.0 / D_MODEL)
    dy_ref[...] = dy
    dd, dh = _rms_bwd(d, rstd, dy * g4_ref[...])
    dd_ref[...] = dd.astype(dd_ref.dtype)
    _acc(dg4_ref, _fold8(dy * dh))
    e8 = _fold8(e * e)
    l = e8[:, 0:HEAD]
    for j in range(1, D_MODEL // HEAD):
        l = l + e8[:, j * HEAD:(j + 1) * HEAD]
    _acc(loss_ref, l)


def _bmid_body(dy_ref, dh2_ref, x1_ref, o_ref, g2_ref, g3_ref, dx1_ref, do_ref, dg3_ref, dg2_ref):
    x1 = x1_ref[...]
    dh2 = dh2_ref[...]
    dn, x1h = _rms_bwd(x1, _rstd(x1), dh2 * g3_ref[...])
    dx1 = dy_ref[...] + dn
    dx1_ref[...] = dx1
    _acc(dg3_ref, _fold8(dh2 * x1h))
    o = o_ref[...]
    do, oh = _rms_bwd(o, _rstd(o), dx1 * g2_ref[...])
    do_ref[...] = do.astype(do_ref.dtype)
    _acc(dg2_ref, _fold8(dx1 * oh))


def _dproj_body(dq_ref, dk_ref, dv_ref, dcq_ref, dckv_ref, p_ref, dkr_ref,
                ca, saa, sab, cb, sba, sbb, gq_ref, gkv_ref,
                dp_ref, dgq_ref, dgkv_ref):
    c, sa, sb = ca[...], saa[...], sab[...]
    for h in range(NH):
        lo = h * HEAD
        dp_ref[:, lo:lo + HEAD] = _rope_t(dq_ref[:, lo:lo + HEAD], c, sa, sb, ROT_A // 2).astype(dp_ref.dtype)
        dp_ref[:, A_W + lo:A_W + lo + HEAD] = _rope_t(dk_ref[:, lo:lo + HEAD], c, sa, sb, ROT_A // 2).astype(dp_ref.dtype)
    dp_ref[:, 2 * A_W:3 * A_W] = dv_ref[...].astype(dp_ref.dtype)
    cq = p_ref[:, 3 * A_W:3 * A_W + LORA]
    dcqn = dcq_ref[...]
    dcq, cqh = _rms_bwd(cq, _rstd(cq), dcqn * gq_ref[...])
    dp_ref[:, 3 * A_W:3 * A_W + LORA] = dcq.astype(dp_ref.dtype)
    _acc(dgq_ref, _fold8(dcqn * cqh))
    ckv = p_ref[:, 3 * A_W + LORA:MAIN_COLS]
    dckvn = dckv_ref[...]
    dckv, ckvh = _rms_bwd(ckv, _rstd(ckv), dckvn * gkv_ref[...])
    dp_ref[:, 3 * A_W + LORA:MAIN_COLS] = dckv.astype(dp_ref.dtype)
    _acc(dgkv_ref, _fold8(dckvn * ckvh))
    dkr = dkr_ref[:, 0:HEAD]
    for h in range(1, NH):
        dkr = dkr + dkr_ref[:, h * HEAD:(h + 1) * HEAD]
    dp_ref[:, MAIN_COLS:PROJ_COLS] = _rope_t(dkr, cb[...], sba[...], sbb[...], ROPE_B // 2).astype(dp_ref.dtype)


def _bin_body(dx1_ref, dh_ref, x_ref, g1_ref, dx_ref, dg1_ref):
    x = x_ref[...]
    dh = dh_ref[...]
    dn, xh = _rms_bwd(x, _rstd(x), dh * g1_ref[...])
    dx_ref[...] = dx1_ref[...] + dn
    _acc(dg1_ref, _fold8(dh * xh))


def _dot_nt(a, b):
    return lax.dot_general(a, b, _DIMS["nt"], preferred_element_type=F32)


def _dot_tn(a, b):
    return lax.dot_general(a, b, _DIMS["tn"], preferred_element_type=F32)


def _dot_nn(a, b):
    return jnp.dot(a, b, preferred_element_type=F32)


DIL_SCALE = HEAD ** -0.5
DIL_CHUNK = 256


def _dil_rows(t, d):
    r = t & (d - 1)
    n = t >> (d.bit_length() - 1)
    start = r + n * (HEAD * d)
    has_prev = n > 0
    pstart = jnp.where(has_prev, start - HEAD * d, start)
    if d == 1:
        return pl.ds(pl.multiple_of(start, HEAD), HEAD), pl.ds(pl.multiple_of(pstart, HEAD), HEAD), has_prev
    return pl.ds(start, HEAD, stride=d), pl.ds(pstart, HEAD, stride=d), has_prev


def _dil_band():
    row = lax.broadcasted_iota(jnp.int32, (HEAD, 2 * HEAD), 0)
    col = lax.broadcasted_iota(jnp.int32, (HEAD, 2 * HEAD), 1)
    return (col >= row) & (col <= row + HEAD), col >= HEAD


def _dil_fwd_body(q_ref, k_ref, v_ref, a_ref, lse_ref, o1, o2, o3, l1, l2, l3, *, nt, unroll):
    band, is_cur = _dil_band()
    for d, o_sc, l_sc in zip(DIL, (o1, o2, o3), (l1, l2, l3)):

        def tiles(g, carry, d=d, o_sc=o_sc, l_sc=l_sc):
            staged = []
            for u in range(unroll):
                rows, prows, has_prev = _dil_rows(g * unroll + u, d)
                q = q_ref[rows, :].astype(MXU_DTYPE)
                kk = jnp.concatenate([k_ref[prows, :], k_ref[rows, :]], axis=0).astype(MXU_DTYPE)
                staged.append((rows, prows, has_prev, _dot_nt(q, kk)))
            for rows, prows, has_prev, s in staged:
                vv = jnp.concatenate([v_ref[prows, :], v_ref[rows, :]], axis=0).astype(MXU_DTYPE)
                ok = band & (is_cur | has_prev)
                s = jnp.where(ok, s * DIL_SCALE, NEG)
                m = jnp.max(s, axis=1, keepdims=True)
                p = jnp.exp(s - m)
                den = jnp.sum(p, axis=1, keepdims=True)
                o_sc[rows, :] = _dot_nn((p / den).astype(MXU_DTYPE), vv)
                l_sc[rows, :] = jnp.broadcast_to(m + jnp.log(den), (HEAD, HEAD))
            return carry

        lax.fori_loop(0, nt // unroll, tiles, 0)

    def merge(i, carry):
        rs = pl.ds(pl.multiple_of(i * DIL_CHUNK, DIL_CHUNK), DIL_CHUNK)
        la, lb, lc = l1[rs, :], l2[rs, :], l3[rs, :]
        m = jnp.maximum(jnp.maximum(la, lb), lc)
        wa, wb, wc = jnp.exp(la - m), jnp.exp(lb - m), jnp.exp(lc - m)
        den = wa + wb + wc
        a = (wa / den) * o1[rs, :] + (wb / den) * o2[rs, :] + (wc / den) * o3[rs, :]
        a_ref[rs, :] = a.astype(a_ref.dtype)
        lse_ref[rs, :] = m + jnp.log(den)
        return carry

    lax.fori_loop(0, q_ref.shape[0] // DIL_CHUNK, merge, 0)


def _dil_fwd(q, k, v):
    T = q.shape[0]
    spec = pl.BlockSpec((T, HEAD), lambda h: (0, h))
    return _pcall(
        functools.partial(_dil_fwd_body, nt=T // HEAD, unroll=16), name="dil_fwd",
        grid=(NH,), in_specs=[spec] * 3, out_specs=[spec] * 2,
        out_shape=[jax.ShapeDtypeStruct((T, 2 * A_W), MXU_DTYPE), jax.ShapeDtypeStruct((T, A_W), F32)],
        scratch_shapes=[pltpu.VMEM((T, HEAD), F32)] * 6,
        compiler_params=pltpu.CompilerParams(dimension_semantics=("parallel",)),
    )(q, k, v)


def _dil_bwd_body(q_ref, k_ref, v_ref, do_ref, a_ref, lse_ref, dq_ref, dk_ref, dv_ref, dl_sc, *, nt, unroll):
    band, is_cur = _dil_band()

    def prep(i, carry):
        rs = pl.ds(pl.multiple_of(i * DIL_CHUNK, DIL_CHUNK), DIL_CHUNK)
        dl = jnp.sum(do_ref[rs, :] * a_ref[rs, :].astype(F32), axis=1, keepdims=True)
        dl_sc[rs, :] = jnp.broadcast_to(dl, (DIL_CHUNK, HEAD))
        zero = jnp.zeros((DIL_CHUNK, HEAD), F32)
        dq_ref[rs, :] = zero
        dk_ref[rs, :] = zero
        dv_ref[rs, :] = zero
        return carry

    lax.fori_loop(0, q_ref.shape[0] // DIL_CHUNK, prep, 0)

    for d in DIL:

        def tiles(g, carry, d=d):
            staged = []
            for u in range(unroll):
                rows, prows, has_prev = _dil_rows(g * unroll + u, d)
                q = q_ref[rows, :].astype(MXU_DTYPE)
                kk = jnp.concatenate([k_ref[prows, :], k_ref[rows, :]], axis=0).astype(MXU_DTYPE)
                vv = jnp.concatenate([v_ref[prows, :], v_ref[rows, :]], axis=0).astype(MXU_DTYPE)
                do = do_ref[rows, :].astype(MXU_DTYPE)
                staged.append((rows, prows, has_prev, q, kk, do, _dot_nt(q, kk), _dot_nt(do, vv)))
            for rows, prows, has_prev, q, kk, do, s, dp in staged:
                lse = lse_ref[rows, :]
                dl = dl_sc[rows, :]
                ok = band & (is_cur | has_prev)
                p = jnp.where(ok, jnp.exp(s * DIL_SCALE - jnp.concatenate([lse, lse], axis=1)), 0.0)
                ds = (p * (dp - jnp.concatenate([dl, dl], axis=1))).astype(MXU_DTYPE)
                dq_ref[rows, :] += _dot_nn(ds, kk) * DIL_SCALE
                dkk = _dot_tn(ds, q) * DIL_SCALE
                dvv = _dot_tn(p.astype(MXU_DTYPE), do)
                dk_ref[rows, :] += dkk[HEAD:, :]
                dv_ref[rows, :] += dvv[HEAD:, :]
                dk_ref[prows, :] += dkk[:HEAD, :]
                dv_ref[prows, :] += dvv[:HEAD, :]
            return carry

        lax.fori_loop(0, nt // unroll, tiles, 0)


def _dil_bwd(q, k, v, dmix, mixed, lse):
    T = q.shape[0]
    spec = pl.BlockSpec((T, HEAD), lambda h: (0, h))
    return _pcall(
        functools.partial(_dil_bwd_body, nt=T // HEAD, unroll=8), name="dil_bwd",
        grid=(NH,), in_specs=[spec] * 6, out_specs=[spec] * 3,
        out_shape=[jax.ShapeDtypeStruct((T, A_W), F32)] * 3,
        scratch_shapes=[pltpu.VMEM((T, HEAD), F32)],
        compiler_params=pltpu.CompilerParams(dimension_semantics=("parallel",)),
    )(q, k, v, dmix, mixed, lse)


MLA_SCALE = (HEAD + ROPE_B) ** -0.5
LOG2E = 1.4426950408889634
MLA_QSCALE = MLA_SCALE * LOG2E
MLA_T = 512
MLA_HP = 4


def _tri(t):
    row = lax.broadcasted_iota(jnp.int32, (t, t), 0)
    col = lax.broadcasted_iota(jnp.int32, (t, t), 1)
    return col <= row


def _lanes(x, n):
    return jnp.tile(x, (1, n // HEAD))


def _mla_fwd_body(q_ref, kv_ref, kr_ref, mixed_ref, o_ref, lse_ref, m_sc, l_sc, acc_sc, *, t, hp):
    del mixed_ref
    qi = pl.program_id(1)
    m_sc[...] = jnp.full(m_sc.shape, NEG, F32)
    l_sc[...] = jnp.zeros(l_sc.shape, F32)
    acc_sc[...] = jnp.zeros(acc_sc.shape, F32)

    def step(j, masked):
        ks = pl.ds(pl.multiple_of(j * t, t), t)
        kr = kr_ref[ks, :]
        logits = []
        for hh in range(hp):
            kcat = jnp.concatenate([kv_ref[ks, 2 * hh * HEAD:(2 * hh + 1) * HEAD], kr], axis=1)
            logits.append(_dot_nt(q_ref[:, hh * QPAD:(hh + 1) * QPAD], kcat))
        for hh in range(hp):
            s = logits[hh]
            if masked:
                s = jnp.where(_tri(t), s, NEG)
            m_prev = m_sc[hh]
            m_new = jnp.maximum(m_prev, jnp.max(s, axis=1, keepdims=True))
            alpha = jnp.exp2(m_prev - m_new)
            p = jnp.exp2(s - _lanes(m_new, t))
            l_sc[hh] = alpha * l_sc[hh] + jnp.sum(p, axis=1, keepdims=True)
            acc_sc[hh] = alpha * acc_sc[hh] + _dot_nn(p.astype(MXU_DTYPE), kv_ref[ks, (2 * hh + 1) * HEAD:(2 * hh + 2) * HEAD])
            m_sc[hh] = m_new

    def off_diag(j, carry):
        step(j, False)
        return carry

    lax.fori_loop(0, qi, off_diag, 0)
    step(qi, True)
    for hh in range(hp):
        l = l_sc[hh]
        o_ref[:, hh * HEAD:(hh + 1) * HEAD] = (acc_sc[hh] / l).astype(o_ref.dtype)
        lse_ref[:, hh * HEAD:(hh + 1) * HEAD] = m_sc[hh] + jnp.log2(l)


def _mla_fwd(qf, kv, kr, mixed):
    T = qf.shape[0]
    t, hp = min(MLA_T, T), MLA_HP
    ng = NH // hp
    return _pcall(
        functools.partial(_mla_fwd_body, t=t, hp=hp), name="mla_fwd",
        grid=(ng, T // t),
        in_specs=[pl.BlockSpec((t, hp * QPAD), lambda g, i: (i, g)),
                  pl.BlockSpec((T, hp * 2 * HEAD), lambda g, i: (0, g)),
                  pl.BlockSpec((T, HEAD), lambda g, i: (0, 0)), ANY],
        out_specs=[pl.BlockSpec((t, hp * HEAD), lambda g, i: (i, ng + g)),
                   pl.BlockSpec((t, hp * HEAD), lambda g, i: (i, g))],
        out_shape=[jax.ShapeDtypeStruct(mixed.shape, mixed.dtype), jax.ShapeDtypeStruct((T, A_W), F32)],
        input_output_aliases={3: 0},
        scratch_shapes=[pltpu.VMEM((hp, t, HEAD), F32)] * 3,
        compiler_params=pltpu.CompilerParams(dimension_semantics=("parallel", "parallel")),
    )(qf, kv, kr, mixed)


def _mla_bwd_body(q_ref, kn_ref, kr_ref, v_ref, do_ref, o_ref, lse_ref, cb, sba, sbb,
                  dq_ref, dkv_ref, dkr_ref, dq_sc, dl_sc, dk_sc, dv_sc, *, t):
    ki = pl.program_id(1)
    nq = q_ref.shape[0] // t

    @pl.when(ki == 0)
    def _():
        def prep(i, carry):
            rs = pl.ds(pl.multiple_of(i * t, t), t)
            dl = jnp.sum(do_ref[rs, :] * o_ref[rs, :].astype(F32), axis=1, keepdims=True)
            dl_sc[rs, :] = jnp.broadcast_to(dl, (t, HEAD))
            dq_sc[rs, :] = jnp.zeros((t, QPAD), F32)
            return carry
        lax.fori_loop(0, nq, prep, 0)

    kcat = jnp.concatenate([kn_ref[...], kr_ref[...]], axis=1)
    v = v_ref[...]
    dk_sc[...] = jnp.zeros(dk_sc.shape, F32)
    dv_sc[...] = jnp.zeros(dv_sc.shape, F32)

    def steps(blocks):
        staged = []
        for i, masked in blocks:
            qs = pl.ds(pl.multiple_of(i * t, t), t)
            q = q_ref[qs, :]
            do = do_ref[qs, :].astype(MXU_DTYPE)
            staged.append((qs, q, do, _dot_nt(q, kcat), _dot_nt(do, v), masked))
        for qs, q, do, s, dp, masked in staged:
            p = jnp.exp2(s - _lanes(lse_ref[qs, :], t))
            if masked:
                p = jnp.where(_tri(t), p, 0.0)
            ds = (p * (dp - _lanes(dl_sc[qs, :], t))).astype(MXU_DTYPE)
            dv_sc[...] += _dot_tn(p.astype(MXU_DTYPE), do)
            dk_sc[...] += _dot_tn(ds, q)
            dq_sc[qs, :] += _dot_nn(ds, kcat) * MLA_SCALE

    n_blocks = nq - ki

    @pl.when(n_blocks == 1)
    def _():
        steps([(ki, True)])

    @pl.when(n_blocks > 1)
    def _():
        steps([(ki, True), (ki + 1, False)])

    def pair(j, carry):
        steps([(ki + 2 * j, False), (ki + 2 * j + 1, False)])
        return carry

    lax.fori_loop(1, n_blocks // 2, pair, 0)

    @pl.when((n_blocks > 1) & (n_blocks % 2 == 1))
    def _():
        steps([(nq - 1, False)])

    dk = dk_sc[...] * (1.0 / LOG2E)
    dkv_ref[:, 0:HEAD] = dk[:, 0:HEAD].astype(dkv_ref.dtype)
    dkr_ref[...] = dk[:, HEAD:QPAD]
    dkv_ref[:, HEAD:] = dv_sc[...].astype(dkv_ref.dtype)

    @pl.when(ki == nq - 1)
    def _():
        def emit(i, carry):
            rs = pl.ds(pl.multiple_of(i * t, t), t)
            dq_ref[rs, 0:HEAD] = dq_sc[rs, 0:HEAD].astype(dq_ref.dtype)
            dq_ref[rs, HEAD:QPAD] = _rope_t(dq_sc[rs, HEAD:QPAD], cb[rs, :], sba[rs, :], sbb[rs, :],
                                            ROPE_B // 2).astype(dq_ref.dtype)
            return carry
        lax.fori_loop(0, nq, emit, 0)


def _mla_bwd(qf, kv, kr, dmix, mixed, lse, tabs_b):
    T = qf.shape[0]
    t = min(MLA_T, T)
    head = lambda h, j: (0, h)
    b_half = lambda h, j: (0, NH + h)
    kblk = pl.BlockSpec((t, HEAD), lambda h, j: (j, h))
    return _pcall(
        functools.partial(_mla_bwd_body, t=t), name="mla_bwd",
        grid=(NH, T // t),
        in_specs=[pl.BlockSpec((T, QPAD), head), pl.BlockSpec((t, HEAD), lambda h, j: (j, 2 * h)),
                  pl.BlockSpec((t, HEAD), lambda h, j: (j, 0)),
                  pl.BlockSpec((t, HEAD), lambda h, j: (j, 2 * h + 1)),
                  pl.BlockSpec((T, HEAD), b_half), pl.BlockSpec((T, HEAD), b_half),
                  pl.BlockSpec((T, HEAD), head)] + [pl.BlockSpec((T, HEAD), lambda h, j: (0, 0))] * 3,
        out_specs=[pl.BlockSpec((T, QPAD), head), pl.BlockSpec((t, 2 * HEAD), lambda h, j: (j, h)), kblk],
        out_shape=[jax.ShapeDtypeStruct((T, NH * QPAD), MXU_DTYPE), jax.ShapeDtypeStruct((T, 2 * A_W), MXU_DTYPE),
                   jax.ShapeDtypeStruct((T, A_W), F32)],
        scratch_shapes=[pltpu.VMEM((T, QPAD), F32), pltpu.VMEM((T, HEAD), F32), pltpu.VMEM((t, QPAD), F32),
                        pltpu.VMEM((t, HEAD), F32)],
        compiler_params=pltpu.CompilerParams(dimension_semantics=("parallel", "arbitrary")),
    )(qf, kv, kr, kv, dmix, mixed, lse, *tabs_b)


def _local_step(x, pos, target, g1, g2, gq, gkv, g3, g4,
                in_weights, attn_prefetch, attn_weights, mlp_prefetch, mlp_weights,
                down_grad_ready, up_grad_ready, attn_grads_ready):
    T = x.shape[0]
    TR = 256
    mm = functools.partial(_matmul, tm=2048, tn=1024, tk=2048, b_outer=True)
    mm_k = functools.partial(_matmul, tm=1024, tn=1024, tk=2048)
    mm_g = functools.partial(_matmul, tm=1024, tn=1024, tk=4096, b_outer=True)

    inv_a = ROPE_THETA ** (-jnp.arange(0, ROT_A, 2, dtype=F32) / ROT_A)
    inv_b = ROPE_THETA ** (-jnp.arange(0, ROPE_B, 2, dtype=F32) / ROPE_B)
    inv = jnp.stack([jnp.concatenate([inv_a, inv_a, jnp.zeros((HEAD - ROT_A,), F32)]),
                     jnp.concatenate([inv_b, inv_b, jnp.zeros((HEAD - ROPE_B,), F32)])])
    inv = jnp.concatenate([inv, jnp.zeros((6, HEAD), F32)], axis=0)
    tabs = _rowwise(_rope_tab_body, [pos], [inv], [(HEAD, F32)] * 6, [], tr=512, name="rope_tables")

    (h,) = _rowwise(_rms_fwd_body, [x], [g1], [(D_MODEL, MXU_DTYPE)], [], tr=TR, name="rms_in")
    w_proj = in_weights([h, tabs[0]])
    (proj,) = mm(h, w_proj, dims="nt", out_dtypes=[F32], tm=1024, tn=PROJ_TILE, name="proj_in")
    gq = gq + attn_prefetch(proj)
    q, k, v, cqn, ckvn, krope = _rowwise(
        _postproj_body, [proj] + tabs, [gq, gkv],
        [(A_W, F32)] * 3 + [(LORA, MXU_DTYPE)] * 2 + [(HEAD, MXU_DTYPE)], [], tr=TR, name="post_proj")
    mixed, lse_a = _dil_fwd(q, k, v)

    w_uq_p, w_ukv, w_out = attn_weights(cqn)

    def q_epi(acc, cb, sba, sbb):
        cols = []
        for hh in range(acc.shape[1] // QPAD):
            lo = hh * QPAD
            cols += [acc[:, lo:lo + HEAD], _rope(acc[:, lo + HEAD:lo + QPAD], cb, sba, sbb, ROPE_B // 2)]
        return (jnp.concatenate(cols, axis=1) * MLA_QSCALE,)
    (qf,) = mm(cqn, w_uq_p, dims="nn", out_dtypes=[MXU_DTYPE], name="q_up", epi=q_epi, row_extras=tuple(tabs[3:]))
    (kv,) = mm(ckvn, w_ukv, dims="nn", out_dtypes=[MXU_DTYPE], name="kv_up")
    mixed, lse_b = _mla_fwd(qf, kv, krope, mixed)
    (o,) = mm(mixed, w_out, dims="nn", out_dtypes=[F32], name="out_proj", after=mlp_prefetch(mixed))
    x1, h2 = _rowwise(_mid_body, [x, o], [g2, g3], [(D_MODEL, F32), (D_MODEL, MXU_DTYPE)], [], tr=TR, name="mid_norm")

    w_up, w_down = mlp_weights(h2)

    def up_epi(acc):
        r = jnp.maximum(acc, 0.0)
        return r * r, r
    u, r = mm(h2, w_up, dims="nn", out_dtypes=[MXU_DTYPE, MXU_DTYPE], name="mlp_up", epi=up_epi, b_shards=N_CHIPS)
    (dn,) = mm_k(u, w_down, dims="nn", out_dtypes=[F32], name="mlp_down")
    dy, dd, loss8, dg4 = _rowwise(_loss_body, [x1, dn, target], [g4], [(D_MODEL, F32), (D_MODEL, MXU_DTYPE)],
                                  [(8, HEAD), (8, D_MODEL)], tr=TR, name="loss_head")

    def dup_epi(acc, rr):
        return (acc * (2.0 * rr.astype(F32)),)
    (dup,) = mm(dd, w_down, dims="nt", out_dtypes=[MXU_DTYPE], name="d_up", epi=dup_epi, extras=(r,))
    (gw_down,) = mm_g(u, dd, dims="tn", out_dtypes=[WIRE_DTYPE], name="gw_down")
    (dh2,) = mm_k(dup, w_up, dims="nt", out_dtypes=[F32], name="d_h2", b_shards=N_CHIPS,
                  after=down_grad_ready(gw_down))
    (gw_up,) = mm_g(h2, dup, dims="tn", out_dtypes=[WIRE_DTYPE], name="gw_up", out_shards=N_CHIPS)
    g2 = g2 + up_grad_ready(gw_up)
    dx1, do, dg3, dg2 = _rowwise(_bmid_body, [dy, dh2, x1, o], [g2, g3], [(D_MODEL, F32), (D_MODEL, MXU_DTYPE)],
                                 [(8, D_MODEL), (8, D_MODEL)], tr=TR, name="bwd_mid")
    (dmix,) = mm(do, w_out, dims="nt", out_dtypes=[F32], name="d_mixed")
    (gw_out,) = mm_g(mixed, do, dims="tn", out_dtypes=[WIRE_DTYPE], name="gw_out")

    dq_pad, dkv, dkr = _mla_bwd(qf, kv, krope, dmix, mixed, lse_b, tabs[3:])
    (dcqn,) = mm(dq_pad, w_uq_p, dims="nt", out_dtypes=[F32], name="d_cq")
    (gw_uq_p,) = mm_g(cqn, dq_pad, dims="tn", out_dtypes=[WIRE_DTYPE], name="gw_uq")
    (dckvn,) = mm(dkv, w_ukv, dims="nt", out_dtypes=[F32], name="d_ckv")
    (gw_ukv,) = mm_g(ckvn, dkv, dims="tn", out_dtypes=[WIRE_DTYPE], name="gw_ukv")
    gq = gq + attn_grads_ready(gw_out, gw_uq_p, gw_ukv)

    dq_a, dk_a, dv_a = _dil_bwd(q, k, v, dmix, mixed, lse_a)
    dproj, dgq, dgkv = _rowwise(
        _dproj_body, [dq_a, dk_a, dv_a, dcqn, dckvn, proj, dkr] + tabs, [gq, gkv],
        [(PROJ_COLS, MXU_DTYPE)], [(8, LORA), (8, LORA)], tr=TR, name="d_proj")
    (dh,) = mm_k(dproj, w_proj, dims="nn", out_dtypes=[F32], tk=PROJ_TILE, name="d_h")
    (gw_proj,) = mm_g(dproj, h, dims="tn", out_dtypes=[WIRE_DTYPE], tm=PROJ_TILE, name="gw_in")
    dx, dg1 = _rowwise(_bin_body, [dx1, dh, x], [g1], [(D_MODEL, F32)], [(8, D_MODEL)], tr=TR, name="bwd_in")

    small = jnp.concatenate([dg1, dg2, dgq, dgkv, dg3, dg4, loss8], axis=1)
    return dx, gw_proj, small


def _place():
    x, y, c = lax.axis_index("x"), lax.axis_index("y"), lax.axis_index("c")
    chips = [(1 - x, y), (x, 1 - y), (1 - x, 1 - y)]
    return x, y, c, chips


def _cast_place_body(me_ref, w_ref, *rest):
    o_ref = rest[-1]
    o_ref[...] = w_ref[...].astype(o_ref.dtype)


def _cast_place(me_arr, w, name, after=None):
    rows, cols = w.shape
    tr = min(rows, 256)
    after = [] if after is None else [after]
    grid_spec = pltpu.PrefetchScalarGridSpec(
        num_scalar_prefetch=1, grid=(rows // tr,),
        in_specs=[pl.BlockSpec((tr, cols), lambda i, me: (i, 0))] + [ANY] * len(after),
        out_specs=pl.BlockSpec((None, tr, cols), lambda i, me: (me[0], i, 0)))
    return _pcall(
        _cast_place_body, name=name, grid_spec=grid_spec,
        out_shape=jax.ShapeDtypeStruct((N_CHIPS, rows, cols), WIRE_DTYPE),
        compiler_params=pltpu.CompilerParams(dimension_semantics=("parallel",)),
    )(me_arr, w, *after)


def _cast_place_t_body(me_ref, w_ref, o_ref, *, n):
    i = pl.program_id(0)

    @pl.when(i < n)
    def _():
        o_ref[...] = w_ref[...].astype(o_ref.dtype)

    @pl.when(i == n)
    def _():
        o_ref[...] = jnp.zeros_like(o_ref)


def _cast_place_t(me_arr, w_t, name):
    rows, cols = w_t.shape
    n = rows // IN_TR
    grid_spec = pltpu.PrefetchScalarGridSpec(
        num_scalar_prefetch=1, grid=(n + 1,),
        in_specs=[pl.BlockSpec((IN_TR, cols), lambda i, me: (jnp.minimum(i, n - 1), 0))],
        out_specs=pl.BlockSpec((IN_TR, cols), lambda i, me: (jnp.where(i < n, me[0] * n + i, N_CHIPS * n), 0)))
    return _pcall(
        functools.partial(_cast_place_t_body, n=n), name=name, grid_spec=grid_spec,
        out_shape=jax.ShapeDtypeStruct((PROJ_COLS, cols), WIRE_DTYPE),
        compiler_params=pltpu.CompilerParams(dimension_semantics=("arbitrary",)),
    )(me_arr, w_t)


HBM = pl.BlockSpec(memory_space=pltpu.HBM)
SEM = pl.BlockSpec(memory_space=pltpu.SEMAPHORE)
EFFECT = pltpu.SideEffectType.DATAFLOW_SIDE_EFFECTING


def _copy_start(make, arrays, after, name, n_sems):
    n_a = len(arrays)
    after = [] if after is None else [after]

    def body(*refs):
        for send, _ in make(refs[:n_a], refs[-n_a - 3], refs[-n_a - 2]):
            send.start()
        refs[-1][...] = jnp.zeros_like(refs[-1])

    res = _pcall(
        body, name=name,
        in_specs=[HBM] * n_a + [ANY] * len(after),
        out_specs=[SEM, SEM] + [HBM] * n_a + [pl.BlockSpec(memory_space=pltpu.VMEM)],
        out_shape=[pltpu.SemaphoreType.DMA((n_sems,)), pltpu.SemaphoreType.DMA((n_sems,))]
        + [pltpu.HBM(a.shape, a.dtype) for a in arrays] + [jax.ShapeDtypeStruct((8, HEAD), F32)],
        input_output_aliases={i: 2 + i for i in range(n_a)},
        compiler_params=pltpu.CompilerParams(has_side_effects=EFFECT),
    )(*[pltpu.with_memory_space_constraint(a, pltpu.HBM) for a in arrays], *after)
    return (res[0], res[1]), list(res[2:2 + n_a]), res[-1]


def _copy_wait(make, sems, arrays, after, name):
    n_a = len(arrays)
    after = list(after) if isinstance(after, (list, tuple)) else [after]

    def body(*refs):
        for send, recv in make(refs[:n_a], refs[n_a], refs[n_a + 1]):
            send.wait_send()
            recv.wait_recv()

    return list(_pcall(
        body, name=name,
        in_specs=[HBM] * n_a + [SEM, SEM] + [ANY] * len(after), out_specs=[HBM] * n_a,
        out_shape=[pltpu.HBM(a.shape, a.dtype) for a in arrays],
        input_output_aliases={i: i for i in range(n_a)},
        compiler_params=pltpu.CompilerParams(has_side_effects=EFFECT),
    )(*arrays, sems[0], sems[1], *after))


def _slot(buf, chip, half):
    if buf.ndim == 2:
        hc = buf.shape[1] // 2
        return buf.at[pl.ds(pl.multiple_of(chip * IN_SHARD, 16), IN_SHARD), pl.ds(pl.multiple_of(half * hc, HEAD), hc)]
    hr = buf.shape[1] // 2
    return buf.at[chip, pl.ds(pl.multiple_of(half * hr, 16), hr)]


def _ag_descs(bufs, send_sems, recv_sems):
    x, y, c, chips = _place()
    me = 2 * x + y
    out = []
    for w, buf in enumerate(bufs):
        for j, (px, py) in enumerate(chips):
            mk = lambda ref, w=w, j=j, px=px, py=py: pltpu.make_async_remote_copy(
                src_ref=ref, dst_ref=ref, send_sem=send_sems.at[w * 3 + j], recv_sem=recv_sems.at[w * 3 + j],
                device_id=(px, py, c), device_id_type=MESH)
            out.append((mk(_slot(buf, me, c)), mk(_slot(buf, 2 * px + py, c))))
    return out


def _fw_descs(bufs, send_sems, recv_sems):
    x, y, c, chips = _place()
    out = []
    for w, buf in enumerate(bufs):
        for j, (px, py) in enumerate(chips):
            def mk(which, w=w, j=j, buf=buf, px=px, py=py):
                ref = _slot(buf, 2 * px + py, which)
                return pltpu.make_async_remote_copy(
                    src_ref=ref, dst_ref=ref, send_sem=send_sems.at[w * 3 + j], recv_sem=recv_sems.at[w * 3 + j],
                    device_id=(x, y, 1 - c), device_id_type=MESH)
            out.append((mk(c), mk(1 - c)))
    return out


def _sc_descs(refs, send_sems, recv_sems):
    n_w = len(refs) // 2
    x, y, c, chips = _place()
    me = 2 * x + y
    out = []
    for w in range(n_w):
        for j, (px, py) in enumerate(chips):
            d = pltpu.make_async_remote_copy(
                src_ref=refs[w].at[2 * px + py], dst_ref=refs[n_w + w].at[me],
                send_sem=send_sems.at[w * 3 + j], recv_sem=recv_sems.at[w * 3 + j],
                device_id=(px, py, c), device_id_type=MESH)
            out.append((d, d))
    return out


def _pair_descs(src_of):
    def make(refs, send_sems, recv_sems):
        n_w = len(refs) // 2
        x, y, c, _ = _place()
        out = []
        for w in range(n_w):
            d = pltpu.make_async_remote_copy(
                src_ref=src_of(refs[w], c), dst_ref=refs[n_w + w],
                send_sem=send_sems.at[w], recv_sem=recv_sems.at[w],
                device_id=(x, y, 1 - c), device_id_type=MESH)
            out.append((d, d))
        return out
    return make


_EX_DESCS = _pair_descs(lambda g4, c: g4.at[:, 1 - c])
_SW_DESCS = _pair_descs(lambda half, c: half)
_EXT_DESCS = _pair_descs(lambda g, c: g.at[pl.ds(0, IN_COLS),
                                           pl.ds(pl.multiple_of((1 - c) * (D_MODEL // 2), HEAD), D_MODEL // 2)])


def _sm_descs(refs, send_sems, recv_sems):
    buf = refs[0]
    rows8 = buf.shape[0] // N_DEV
    x, y, c, _ = _place()
    flip = lambda v, d: 1 - v if d else v
    blk = lambda px, py, pc: buf.at[pl.ds(pl.multiple_of((4 * px + 2 * py + pc) * rows8, 8), rows8)]
    out = []
    for k in range(1, N_DEV):
        px, py, pc = flip(x, k & 4), flip(y, k & 2), flip(c, k & 1)
        mk = lambda ref, k=k, px=px, py=py, pc=pc: pltpu.make_async_remote_copy(
            src_ref=ref, dst_ref=ref, send_sem=send_sems.at[k - 1], recv_sem=recv_sems.at[k - 1],
            device_id=(px, py, pc), device_id_type=MESH)
        out.append((mk(blk(x, y, c)), mk(blk(px, py, pc))))
    return out


def _place_rows_body(i_ref, x_ref, o_ref):
    o_ref[...] = x_ref[...]


def _place_rows(i_arr, x, n_blocks, name):
    r, n = x.shape
    grid_spec = pltpu.PrefetchScalarGridSpec(
        num_scalar_prefetch=1, grid=(1,),
        in_specs=[pl.BlockSpec((r, n), lambda g, i: (0, 0))],
        out_specs=pl.BlockSpec((r, n), lambda g, i: (i[0], 0)))
    return _pcall(_place_rows_body, name=name, grid_spec=grid_spec,
                  out_shape=jax.ShapeDtypeStruct((n_blocks * r, n), x.dtype))(i_arr, x)


def _ag_forward_body(*refs, n_w):
    bufs = refs[n_w:2 * n_w]
    send_sems, recv_sems = refs[2 * n_w:]
    pairs = _fw_descs(bufs, send_sems, recv_sems)
    for fw, _ in pairs:
        fw.start()
    for fw, back in pairs:
        back.wait_recv()
        fw.wait_send()


def _ag_forward(bufs, tag):
    n_w = len(bufs)
    return list(_pcall(
        functools.partial(_ag_forward_body, n_w=n_w), name="weight_allgather_forward_" + tag,
        in_specs=[ANY] * n_w, out_specs=[ANY] * n_w,
        out_shape=[jax.ShapeDtypeStruct(b.shape, b.dtype) for b in bufs],
        input_output_aliases={w: w for w in range(n_w)},
        scratch_shapes=[pltpu.SemaphoreType.DMA((3 * n_w,))] * 2,
    )(*bufs))


def _pair_add_body(c_ref, mine_ref, theirs_ref, o_ref):
    o_ref[...] = (mine_ref[...].astype(F32) + theirs_ref[...].astype(F32)).astype(o_ref.dtype)


def _pair_add(c_arr, g4, recv, name):
    _, _, hr, cols = g4.shape
    tr = min(hr, 256)
    grid_spec = pltpu.PrefetchScalarGridSpec(
        num_scalar_prefetch=1, grid=(N_CHIPS, hr // tr),
        in_specs=[pl.BlockSpec((None, None, tr, cols), lambda s, i, c: (s, c[0], i, 0)),
                  pl.BlockSpec((None, tr, cols), lambda s, i, c: (s, i, 0))],
        out_specs=pl.BlockSpec((None, tr, cols), lambda s, i, c: (s, i, 0)))
    return _pcall(
        _pair_add_body, name=name, grid_spec=grid_spec,
        out_shape=jax.ShapeDtypeStruct(recv.shape, recv.dtype),
        compiler_params=pltpu.CompilerParams(dimension_semantics=("parallel", "parallel")),
    )(c_arr, g4, recv)


def _pair_add_t(c_arr, g, recv, name):
    rows, hc = recv.shape
    grid_spec = pltpu.PrefetchScalarGridSpec(
        num_scalar_prefetch=1, grid=(rows // IN_TR,),
        in_specs=[pl.BlockSpec((IN_TR, hc), lambda i, c: (i, c[0])), pl.BlockSpec((IN_TR, hc), lambda i, c: (i, 0))],
        out_specs=pl.BlockSpec((IN_TR, hc), lambda i, c: (i, 0)))
    return _pcall(
        _pair_add_body, name=name, grid_spec=grid_spec,
        out_shape=jax.ShapeDtypeStruct(recv.shape, recv.dtype),
        compiler_params=pltpu.CompilerParams(dimension_semantics=("parallel",)),
    )(c_arr, g, recv)


def _sum4_body(me_ref, p_ref, l0, l1, l2, l3, o_ref):
    me = me_ref[0]
    t = [jnp.where(me == j, p_ref[...], l[...]).astype(F32) for j, l in enumerate((l0, l1, l2, l3))]
    o_ref[...] = ((t[0] + t[1]) + t[2]) + t[3]


def _sum4(me_arr, part, landed, name):
    _, hr, cols = part.shape
    tr = IN_TR if hr == IN_SHARD else min(hr, 256)

    def slot(j):
        return lambda i, me: (jnp.where(me[0] == j, (j + 1) % N_CHIPS, j), i, 0)

    grid_spec = pltpu.PrefetchScalarGridSpec(
        num_scalar_prefetch=1, grid=(hr // tr,),
        in_specs=[pl.BlockSpec((None, tr, cols), lambda i, me: (me[0], i, 0))]
        + [pl.BlockSpec((None, tr, cols), slot(j)) for j in range(N_CHIPS)],
        out_specs=pl.BlockSpec((tr, cols), lambda i, me: (i, 0)))
    return _pcall(
        _sum4_body, name=name, grid_spec=grid_spec,
        out_shape=jax.ShapeDtypeStruct((hr, cols), F32),
        compiler_params=pltpu.CompilerParams(dimension_semantics=("parallel",)),
    )(me_arr, part, landed, landed, landed, landed)


def _adamw(w, g, m, v):
    m = ADAM_B1 * m + (1.0 - ADAM_B1) * g
    v = ADAM_B2 * v + (1.0 - ADAM_B2) * (g * g)
    m_hat = m / (1.0 - ADAM_B1 ** ADAM_STEP)
    v_hat = v / (1.0 - ADAM_B2 ** ADAM_STEP)
    delta = -ADAM_LR * (m_hat / (jnp.sqrt(v_hat) + ADAM_EPS) + ADAM_WD * w)
    return delta, m, v


def _adamw_half_body(h_ref, w_ref, g_in_ref, m_ref, v_ref, *rest):
    g_ref, d_ref, nm_ref, nv_ref, done_ref = rest[-5:]
    done_ref[...] = jnp.zeros_like(done_ref)
    g = g_in_ref[...]
    g_ref[...] = g
    d, m, v = _adamw(w_ref[...], g, m_ref[...], v_ref[...])
    d_ref[...] = d
    nm_ref[...] = m
    nv_ref[...] = v


def _adamw_half(h_arr, w, g_half, m, v, prev, name):
    rows, cols = w.shape
    if g_half.shape[0] == rows:
        tr, nh = IN_TR, rows // IN_TR
        at_half = pl.BlockSpec((tr, cols // 2), lambda i, h: (i, h[0]))
        g_spec = pl.BlockSpec((tr, cols // 2), lambda i, h: (i, 0))
    else:
        tr = min(rows // 2, 128)
        nh = (rows // 2) // tr
        at_half = pl.BlockSpec((tr, cols), lambda i, h: (h[0] * nh + i, 0))
        g_spec = pl.BlockSpec((tr, cols), lambda i, h: (i, 0))
    grid_spec = pltpu.PrefetchScalarGridSpec(
        num_scalar_prefetch=1, grid=(nh,),
        in_specs=[at_half, g_spec, at_half, at_half] + [ANY] * len(prev),
        out_specs=[at_half] * 4 + [pl.BlockSpec((8, HEAD), lambda i, h: (0, 0))])
    return list(_pcall(
        _adamw_half_body, name=name, grid_spec=grid_spec,
        out_shape=[jax.ShapeDtypeStruct(w.shape, F32)] * 4 + [jax.ShapeDtypeStruct((8, HEAD), F32)],
        input_output_aliases={5 + k: k for k in range(len(prev))},
        compiler_params=pltpu.CompilerParams(dimension_semantics=("arbitrary",)),
    )(h_arr, w, g_half, m, v, *prev))


def _small_update_body(gath_ref, w_ref, m_ref, v_ref, g_ref, d_ref, nm_ref, nv_ref, loss_ref, *, n_gain):
    tot = gath_ref[0:1, :]
    for i in range(1, gath_ref.shape[0]):
        tot = tot + gath_ref[i:i + 1, :]
    g = tot[:, 0:n_gain]
    g_ref[...] = g
    d, m, v = _adamw(w_ref[...], g, m_ref[...], v_ref[...])
    d_ref[...] = d
    nm_ref[...] = m
    nv_ref[...] = v
    loss_ref[...] = (0.5 / D_MODEL) * jnp.sum(tot[:, n_gain:n_gain + HEAD], axis=1, keepdims=True) * jnp.ones((1, HEAD), F32)


def _small_update(gath, w, m, v):
    n_gain = w.shape[1]
    vm = pl.BlockSpec(memory_space=pltpu.VMEM)
    return _pcall(
        functools.partial(_small_update_body, n_gain=n_gain), name="gain_update",
        in_specs=[vm] * 4, out_specs=[vm] * 5,
        out_shape=[jax.ShapeDtypeStruct((1, n_gain), F32)] * 4 + [jax.ShapeDtypeStruct((1, HEAD), F32)],
    )(gath, w, m, v)


def kernel(x, positions, norm_attn_pre, norm_attn_post, w_in, q_latent_norm, kv_latent_norm, w_uq, w_ukv, w_out, norm_mlp_pre, norm_mlp_post, w_up, w_down, loss_target, m_norm_attn_pre, m_norm_attn_post, m_w_in, m_q_latent_norm, m_kv_latent_norm, m_w_uq, m_w_ukv, m_w_out, m_norm_mlp_pre, m_norm_mlp_post, m_w_up, m_w_down, v_norm_attn_pre, v_norm_attn_post, v_w_in, v_q_latent_norm, v_kv_latent_norm, v_w_uq, v_w_ukv, v_w_out, v_norm_mlp_pre, v_norm_mlp_post, v_w_up, v_w_down):
    T = x.shape[1]
    c_arr = lax.axis_index("c").astype(jnp.int32).reshape(1)
    me_arr = (2 * lax.axis_index("x") + lax.axis_index("y")).astype(jnp.int32).reshape(1)
    names = ["w_in", "w_uq", "w_ukv", "w_out", "w_up", "w_down"]

    transposed = lambda a: jnp.swapaxes(a, 1, 2)
    mats = [transposed(w_in)[0], w_uq[0], w_ukv[0], w_out[0], w_up[0], w_down[0]]
    me8_arr = (4 * lax.axis_index("x") + 2 * lax.axis_index("y") + lax.axis_index("c")).astype(jnp.int32).reshape(1)
    col_major = lambda g: jnp.transpose(g, (1, 0, 2)).reshape(g.shape[1], N_CHIPS * g.shape[2])
    cast = lambda a: a.astype(MXU_DTYPE)
    to_shards = lambda g: jnp.transpose(g.reshape(g.shape[0], N_CHIPS, g.shape[1] // N_CHIPS), (1, 0, 2))
    halved = lambda g: g.reshape(N_CHIPS, 2, g.shape[1] // 2, g.shape[2])
    empty = lambda a, shape=None: lax.empty(a.shape if shape is None else shape, a.dtype)

    sem_in, buf_in, going = _copy_start(_ag_descs, [_cast_place_t(me_arr, mats[0], "cast_w_in")], None,
                                        "weight_allgather_start_in", 3)
    placed = [_cast_place(me_arr, w, "cast_" + n, going) for w, n in zip(mats[1:], names[1:])]
    sem_att, buf_att, going = _copy_start(_ag_descs, placed[:3], going, "weight_allgather_start_attn", 9)
    sem_mlp, buf_mlp, started = _copy_start(_ag_descs, placed[3:], going, "weight_allgather_start_mlp", 6)

    going_on = {}

    def in_weights(after):
        (win_g,) = _ag_forward(_copy_wait(_ag_descs, sem_in, buf_in, after, "weight_allgather_wait_in"), "in")
        return cast(win_g)

    def attn_prefetch(after):
        landed = _copy_wait(_ag_descs, sem_att, buf_att, after, "weight_allgather_wait_attn")
        going_on["fw_attn"] = _copy_start(_fw_descs, landed, None, "weight_allgather_forward_start_attn", 9)
        return going_on["fw_attn"][-1][0:1, 0:1]

    def attn_weights(after):
        sems, bufs, _ = going_on["fw_attn"]
        wuq_g, wukv_g, wout_g = _copy_wait(_fw_descs, sems, bufs, after, "weight_allgather_forward_wait_attn")
        wuq_full = col_major(wuq_g).reshape(LORA, NH, HEAD + ROPE_B)
        w_uq_p = jnp.pad(wuq_full, ((0, 0), (0, 0), (0, QPAD - HEAD - ROPE_B))).reshape(LORA, NH * QPAD)
        return cast(w_uq_p), cast(col_major(wukv_g)), cast(wout_g.reshape(2 * A_W, D_MODEL))

    def mlp_prefetch(after):
        landed = _copy_wait(_ag_descs, sem_mlp, buf_mlp, after, "weight_allgather_wait_mlp")
        going_on["fw"] = _copy_start(_fw_descs, landed, None, "weight_allgather_forward_start_mlp", 6)
        return going_on["fw"][-1]

    def mlp_weights(after):
        sems, bufs, _ = going_on["fw"]
        wup_g, wdown_g = _copy_wait(_fw_descs, sems, bufs, after, "weight_allgather_forward_wait_mlp")
        return cast(wup_g), cast(wdown_g.reshape(D_FF, D_MODEL))

    def exchange_start(g4s, tag):
        lands = [empty(g, (g.shape[0],) + g.shape[2:]) for g in g4s]
        return _copy_start(_EX_DESCS, g4s + lands, None, "grad_pair_exchange_start_" + tag, len(g4s))

    def exchange_finish(started_ex, after, ns, tag):
        sems, arrs, _ = started_ex
        arrs = _copy_wait(_EX_DESCS, sems, arrs, after, "grad_pair_exchange_wait_" + tag)
        n = len(ns)
        return [_pair_add(c_arr, g4, r, "pair_add_" + nm) for g4, r, nm in zip(arrs[:n], arrs[n:], ns)]

    def scatter_start(parts, after, tag):
        return _copy_start(_sc_descs, parts + [empty(p) for p in parts], after, "grad_scatter_start_" + tag,
                           3 * len(parts))

    def scatter_finish(started_sc, after, tag):
        sems, arrs, _ = started_sc
        arrs = _copy_wait(_sc_descs, sems, arrs, after, "grad_scatter_wait_" + tag)
        return arrs[:len(arrs) // 2], arrs[len(arrs) // 2:]

    def down_grad_ready(gw_down):
        going_on["x_down"] = exchange_start([halved(gw_down.reshape(N_CHIPS, D_MODEL, D_MODEL))], "down")
        return going_on["x_down"][-1]

    def up_grad_ready(gw_up):
        going_on["x_up"] = exchange_start([halved(gw_up)], "up")
        parts = exchange_finish(going_on["x_down"], going_on["x_up"][-1], names[5:], "down")
        going_on["s_down"] = scatter_start(parts, started, "down")
        return going_on["s_down"][-1][0:1, 0:1]

    def attn_grads_ready(gw_out, gw_uq_p, gw_ukv):
        gw_uq = to_shards(gw_uq_p.reshape(LORA, NH, QPAD)[:, :, :HEAD + ROPE_B].reshape(LORA, NH * (HEAD + ROPE_B)))
        full4 = [halved(g) for g in (gw_uq, to_shards(gw_ukv), gw_out.reshape(N_CHIPS, LORA, D_MODEL))]
        x_attn = exchange_start(full4, "attn")
        parts_up = exchange_finish(going_on["x_up"], x_attn[-1], names[4:5], "up")
        parts = exchange_finish(x_attn, parts_up[0], names[1:4], "attn") + parts_up
        going_on["s_rest"] = scatter_start(parts, going_on["s_down"][-1], "attn_up")
        return going_on["s_rest"][-1][0:1, 0:1]

    dx, gw_proj, small = _local_step(
        x[0], positions[0].astype(F32).reshape(T, 1), loss_target[0],
        norm_attn_pre + started[0:1, 0:1], norm_attn_post, q_latent_norm, kv_latent_norm, norm_mlp_pre, norm_mlp_post,
        in_weights, attn_prefetch, attn_weights, mlp_prefetch, mlp_weights,
        down_grad_ready, up_grad_ready, attn_grads_ready)

    ms = [transposed(m_w_in)[0], m_w_uq[0], m_w_ukv[0], m_w_out[0], m_w_up[0], m_w_down[0]]
    vs = [transposed(v_w_in)[0], v_w_uq[0], v_w_ukv[0], v_w_out[0], v_w_up[0], v_w_down[0]]
    sib_arr = 1 - c_arr

    def finish(parts, landed, lo, hi, tag):
        sl = slice(lo, hi)
        halves = [_sum4(me_arr, p, l, "chip_sum_" + n) for p, l, n in zip(parts, landed, names[sl])]
        n = len(halves)
        sems, arrs, _ = _copy_start(_SW_DESCS, halves + [empty(h) for h in halves], None,
                                    "grad_pair_swap_start_" + tag, n)
        own = [_adamw_half(c_arr, w, g, m, v, [], "adamw_own_" + nm)
               for w, g, m, v, nm in zip(mats[sl], arrs[:n], ms[sl], vs[sl], names[sl])]
        arrs = _copy_wait(_SW_DESCS, sems, arrs, own[-1][4], "grad_pair_swap_wait_" + tag)
        return [_adamw_half(sib_arr, w, g, m, v, prev[:4], "adamw_sib_" + nm)
                for w, g, m, v, prev, nm in zip(mats[sl], arrs[n:], ms[sl], vs[sl], own, names[sl])]

    sem_small, (gath,), small_going = _copy_start(
        _sm_descs, [_place_rows(me8_arr, small, N_DEV, "place_small")], None, "small_allgather_start", N_DEV - 1)
    sems, arrs, _ = _copy_start(_EXT_DESCS, [gw_proj, lax.empty((IN_COLS, D_MODEL // 2), WIRE_DTYPE)], None,
                                "grad_pair_exchange_start_in", 1)
    gw_proj, from_sib = _copy_wait(_EXT_DESCS, sems, arrs, small_going, "grad_pair_exchange_wait_in")
    part_in = _pair_add_t(c_arr, gw_proj, from_sib, "pair_add_w_in").reshape(N_CHIPS, IN_SHARD, D_MODEL // 2)
    s_in = scatter_start([part_in], None, "in")
    parts_rest, landed_rest = scatter_finish(going_on["s_rest"], s_in[-1], "attn_up")
    parts_down, landed_down = scatter_finish(going_on["s_down"], landed_rest[0], "down")
    upd_rest = finish(parts_rest + parts_down, landed_rest + landed_down, 1, 6, "rest")
    parts_in, landed_in = scatter_finish(s_in, upd_rest[-1][0], "in")
    upd = finish(parts_in, landed_in, 0, 1, "in") + upd_rest
    grads = [u[0] for u in upd]

    (gath,) = _copy_wait(_sm_descs, sem_small, [gath], grads[0], "small_allgather_wait")
    gains = [norm_attn_pre, norm_attn_post, q_latent_norm, kv_latent_norm, norm_mlp_pre, norm_mlp_post]
    gm = [m_norm_attn_pre, m_norm_attn_post, m_q_latent_norm, m_kv_latent_norm, m_norm_mlp_pre, m_norm_mlp_post]
    gv = [v_norm_attn_pre, v_norm_attn_post, v_q_latent_norm, v_kv_latent_norm, v_norm_mlp_pre, v_norm_mlp_post]
    cat = lambda xs: jnp.concatenate(xs, axis=1)
    g_s, d_s, m_s, v_s, loss_v = _small_update(gath, cat(gains), cat(gm), cat(gv))
    widths = [a.shape[1] for a in gains]
    offs = [sum(widths[:i]) for i in range(len(widths))]
    split = lambda a: [a[:, o:o + w] for o, w in zip(offs, widths)]
    g_gain, d_gain, m_gain, v_gain = split(g_s), split(d_s), split(m_s), split(v_s)

    def ordered(gain_list, mat_list):
        gl, ml = gain_list, [transposed(mat_list[0][None])] + [a[None] for a in mat_list[1:]]
        return [gl[0], gl[1], ml[0], gl[2], gl[3], ml[1], ml[2], ml[3], gl[4], gl[5], ml[4], ml[5]]

    loss = loss_v[0, 0]
    return (loss, dx[None],
            *ordered(g_gain, grads),
            *ordered(d_gain, [u[1] for u in upd]),
            *ordered(m_gain, [u[2] for u in upd]),
            *ordered(v_gain, [u[3] for u in upd]))
```

```python
import functools

import jax
import jax.numpy as jnp
from jax import lax
from jax.experimental import pallas as pl
from jax.experimental.pallas import tpu as pltpu

F32 = jnp.float32
BF16 = jnp.bfloat16
MXU_DTYPE = jnp.bfloat16
WIRE_DTYPE = jnp.bfloat16

D_MODEL = 2048
HEAD = 128
NH = 8
A_W = NH * HEAD
LORA = 512
ROPE_B = 64
QPAD = 256
MAIN_COLS = 3 * A_W + 2 * LORA
IN_COLS = MAIN_COLS + ROPE_B
PROJ_COLS = MAIN_COLS + HEAD
PROJ_TILE = PROJ_COLS // 3
IN_SHARD = 1040
IN_TR = 208
D_FF = 4 * D_MODEL
DIL = (1, 4, 16)
ROT_A = 32
ROPE_THETA = 500000.0
EPS = 1e-6
NEG = -1e30
N_CHIPS = 4
N_DEV = 8

ADAM_LR = 0.001
ADAM_B1 = 0.9
ADAM_B2 = 0.999
ADAM_EPS = 1e-08
ADAM_WD = 0.01
ADAM_STEP = 10

MESH = pl.DeviceIdType.MESH
ANY = pl.BlockSpec(memory_space=pl.ANY)


def _pcall(body, **kw):
    return pl.pallas_call(body, **kw)


_DIMS = {
    "nn": (((1,), (0,)), ((), ())),
    "nt": (((1,), (1,)), ((), ())),
    "tn": (((0,), (0,)), ((), ())),
}


def _mm_body(*refs, dims, nk, epi, n_extra, n_after, n_out):
    a_ref, b_ref = refs[0], refs[1]
    extra = refs[2:2 + n_extra]
    outs = refs[2 + n_extra + n_after:2 + n_extra + n_after + n_out]
    part = lax.dot_general(a_ref[...], b_ref[...], _DIMS[dims], preferred_element_type=F32)

    def finish(acc):
        res = epi(acc, *[r[...] for r in extra]) if epi is not None else (acc,)
        for o_ref, o in zip(outs, res):
            o_ref[...] = o.astype(o_ref.dtype)

    if nk == 1:
        finish(part)
        return
    acc_ref = refs[-1]
    k = pl.program_id(2)

    @pl.when(k == 0)
    def _():
        acc_ref[...] = part

    @pl.when(k > 0)
    def _():
        acc_ref[...] += part

    @pl.when(k == nk - 1)
    def _():
        finish(acc_ref[...])


def _matmul(a, b, *, dims, out_dtypes, tm, tn, tk, name, epi=None, extras=(), row_extras=(), b_outer=False,
            b_shards=0, out_shards=0, after=None):
    if b_shards:
        assert dims in ("nn", "nt") and b.shape[0] == b_shards
        b2 = (b.shape[1], b_shards * b.shape[2])
    else:
        b2 = b.shape
    if dims == "nn":
        (M, K), (K2, N) = a.shape, b2
    elif dims == "nt":
        (M, K), (N, K2) = a.shape, b2
    else:
        (K, M), (K2, N) = a.shape, b2
    assert K == K2, (a.shape, b.shape, dims)
    tm, tn, tk = min(tm, M), min(tn, N), min(tk, K)
    assert M % tm == 0 and N % tn == 0 and K % tk == 0, (name, M, N, K, tm, tn, tk)
    nk = K // tk

    def at(f):
        if b_outer:
            return lambda j, i, k: f(i, j, k)
        return f

    a_spec = {"nn": pl.BlockSpec((tm, tk), at(lambda i, j, k: (i, k))),
              "nt": pl.BlockSpec((tm, tk), at(lambda i, j, k: (i, k))),
              "tn": pl.BlockSpec((tk, tm), at(lambda i, j, k: (k, i)))}[dims]
    b_spec = {"nn": pl.BlockSpec((tk, tn), at(lambda i, j, k: (k, j))),
              "nt": pl.BlockSpec((tn, tk), at(lambda i, j, k: (j, k))),
              "tn": pl.BlockSpec((tk, tn), at(lambda i, j, k: (k, j)))}[dims]
    if b_shards:
        per = b.shape[2] // (tn if dims == "nn" else tk)
        assert per >= 1 and b.shape[2] % (tn if dims == "nn" else tk) == 0
        b_spec = {"nn": pl.BlockSpec((None, tk, tn), at(lambda i, j, k: (j // per, k, j % per))),
                  "nt": pl.BlockSpec((None, tn, tk), at(lambda i, j, k: (k // per, j, k % per)))}[dims]
    o_spec = pl.BlockSpec((tm, tn), at(lambda i, j, k: (i, j)))
    o_shape = (M, N)
    if out_shards:
        assert not extras and N % out_shards == 0 and (N // out_shards) % tn == 0
        o_per = (N // out_shards) // tn
        o_spec = pl.BlockSpec((None, tm, tn), at(lambda i, j, k: (j // o_per, i, j % o_per)))
        o_shape = (out_shards, M, N // out_shards)
    r_specs = [pl.BlockSpec((tm, r.shape[1]), at(lambda i, j, k: (i, 0))) for r in row_extras]
    after = [] if after is None else [after]
    body = functools.partial(_mm_body, dims=dims, nk=nk, epi=epi, n_extra=len(extras) + len(row_extras),
                             n_after=len(after), n_out=len(out_dtypes))
    res = _pcall(
        body, name=name,
        grid=(N // tn, M // tm, nk) if b_outer else (M // tm, N // tn, nk),
        in_specs=[a_spec, b_spec] + [o_spec] * len(extras) + r_specs + [ANY] * len(after),
        out_specs=[o_spec] * len(out_dtypes),
        out_shape=[jax.ShapeDtypeStruct(o_shape, dt) for dt in out_dtypes],
        scratch_shapes=[pltpu.VMEM((tm, tn), F32)] if nk > 1 else [],
        compiler_params=pltpu.CompilerParams(
            dimension_semantics=("parallel", "parallel", "arbitrary")),
    )(a, b, *extras, *row_extras, *after)
    return list(res)


def _rowwise(body, row_ins, vec_ins, row_outs, acc_outs, *, tr, name):
    T = row_ins[0].shape[0]
    tr = min(tr, T)
    assert T % tr == 0
    in_specs = [pl.BlockSpec((tr, a.shape[1]), lambda i: (i, 0)) for a in row_ins]
    in_specs += [pl.BlockSpec(a.shape, lambda i: (0, 0)) for a in vec_ins]
    out_specs = [pl.BlockSpec((tr, w), lambda i: (i, 0)) for (w, _) in row_outs]
    out_specs += [pl.BlockSpec(s, lambda i: (0, 0)) for s in acc_outs]
    out_shape = [jax.ShapeDtypeStruct((T, w), dt) for (w, dt) in row_outs]
    out_shape += [jax.ShapeDtypeStruct(s, F32) for s in acc_outs]
    sem = "arbitrary" if acc_outs else "parallel"
    return list(_pcall(
        body, name=name, grid=(T // tr,), in_specs=in_specs, out_specs=out_specs,
        out_shape=out_shape,
        compiler_params=pltpu.CompilerParams(dimension_semantics=(sem,)),
    )(*row_ins, *vec_ins))


def _rstd(x):
    return lax.rsqrt(jnp.mean(x * x, axis=-1, keepdims=True) + EPS)


def _rms_bwd(x, rstd, dyg):
    xh = x * rstd
    return rstd * (dyg - xh * jnp.mean(dyg * xh, axis=-1, keepdims=True)), xh


def _fold8(v):
    r, w = v.shape
    return jnp.sum(v.reshape(r // 8, 8, w), axis=0)


def _acc(ref, val):
    first = pl.program_id(0) == 0

    @pl.when(first)
    def _():
        ref[...] = val

    @pl.when(jnp.logical_not(first))
    def _():
        ref[...] += val


def _rope(x, c, sa, sb, half):
    return x * c + pltpu.roll(x, HEAD - half, 1) * sa + pltpu.roll(x, half, 1) * sb


def _rope_t(dy, c, sa, sb, half):
    return dy * c - pltpu.roll(dy, HEAD - half, 1) * sa - pltpu.roll(dy, half, 1) * sb


def _rope_tab_body(pos_ref, inv_ref, ca, saa, sab, cb, sba, sbb):
    pos = pos_ref[...]
    lane = lax.broadcasted_iota(jnp.int32, (pos.shape[0], HEAD), 1)
    ang_a = pos * inv_ref[0:1, :]
    ang_b = pos * inv_ref[1:2, :]
    c, s = jnp.cos(ang_a), jnp.sin(ang_a)
    ha = ROT_A // 2
    ca[...] = jnp.where(lane < ROT_A, c, 1.0)
    saa[...] = jnp.where(lane < ha, -s, 0.0)
    sab[...] = jnp.where((lane >= ha) & (lane < ROT_A), s, 0.0)
    c, s = jnp.cos(ang_b), jnp.sin(ang_b)
    hb = ROPE_B // 2
    cb[...] = jnp.where(lane < ROPE_B, c, 1.0)
    sba[...] = jnp.where(lane < hb, -s, 0.0)
    sbb[...] = jnp.where((lane >= hb) & (lane < ROPE_B), s, 0.0)


def _rms_fwd_body(x_ref, g_ref, h_ref):
    x = x_ref[...]
    h_ref[...] = ((x * _rstd(x)) * g_ref[...]).astype(h_ref.dtype)


def _postproj_body(p_ref, ca, saa, sab, cb, sba, sbb, gq_ref, gkv_ref,
                   q_ref, k_ref, v_ref, cqn_ref, ckvn_ref, krope_ref):
    c, sa, sb = ca[...], saa[...], sab[...]
    for h in range(NH):
        lo = h * HEAD
        q_ref[:, lo:lo + HEAD] = _rope(p_ref[:, lo:lo + HEAD], c, sa, sb, ROT_A // 2).astype(q_ref.dtype)
        k_ref[:, lo:lo + HEAD] = _rope(p_ref[:, A_W + lo:A_W + lo + HEAD], c, sa, sb, ROT_A // 2).astype(k_ref.dtype)
    v_ref[...] = p_ref[:, 2 * A_W:3 * A_W].astype(v_ref.dtype)
    cq = p_ref[:, 3 * A_W:3 * A_W + LORA]
    cqn_ref[...] = ((cq * _rstd(cq)) * gq_ref[...]).astype(cqn_ref.dtype)
    ckv = p_ref[:, 3 * A_W + LORA:MAIN_COLS]
    ckvn_ref[...] = ((ckv * _rstd(ckv)) * gkv_ref[...]).astype(ckvn_ref.dtype)
    krope_ref[...] = _rope(p_ref[:, MAIN_COLS:PROJ_COLS], cb[...], sba[...], sbb[...], ROPE_B // 2).astype(krope_ref.dtype)


def _mid_body(x_ref, o_ref, g2_ref, g3_ref, x1_ref, h2_ref):
    o = o_ref[...]
    x1 = x_ref[...] + (o * _rstd(o)) * g2_ref[...]
    x1_ref[...] = x1
    h2_ref[...] = ((x1 * _rstd(x1)) * g3_ref[...]).astype(h2_ref.dtype)


def _loss_body(x1_ref, d_ref, t_ref, g4_ref, dy_ref, dd_ref, loss_ref, dg4_ref):
    d = d_ref[...]
    rstd = _rstd(d)
    y = x1_ref[...] + (d * rstd) * g4_ref[...]
    e = y - t_ref[...]
    dy = e * (1.0 / D_MODEL)
    dy_ref[...] = dy
    dd, dh = _rms_bwd(d, rstd, dy * g4_ref[...])
    dd_ref[...] = dd.astype(dd_ref.dtype)
    _acc(dg4_ref, _fold8(dy * dh))
    e8 = _fold8(e * e)
    l = e8[:, 0:HEAD]
    for j in range(1, D_MODEL // HEAD):
        l = l + e8[:, j * HEAD:(j + 1) * HEAD]
    _acc(loss_ref, l)


def _bmid_body(dy_ref, dh2_ref, x1_ref, o_ref, g2_ref, g3_ref, dx1_ref, do_ref, dg3_ref, dg2_ref):
    x1 = x1_ref[...]
    dh2 = dh2_ref[...]
    dn, x1h = _rms_bwd(x1, _rstd(x1), dh2 * g3_ref[...])
    dx1 = dy_ref[...] + dn
    dx1_ref[...] = dx1
    _acc(dg3_ref, _fold8(dh2 * x1h))
    o = o_ref[...]
    do, oh = _rms_bwd(o, _rstd(o), dx1 * g2_ref[...])
    do_ref[...] = do.astype(do_ref.dtype)
    _acc(dg2_ref, _fold8(dx1 * oh))


def _dproj_body(dq_ref, dk_ref, dv_ref, dcq_ref, dckv_ref, p_ref, dkr_ref,
                ca, saa, sab, cb, sba, sbb, gq_ref, gkv_ref,
                dp_ref, dgq_ref, dgkv_ref):
    c, sa, sb = ca[...], saa[...], sab[...]
    for h in range(NH):
        lo = h * HEAD
        dp_ref[:, lo:lo + HEAD] = _rope_t(dq_ref[:, lo:lo + HEAD], c, sa, sb, ROT_A // 2).astype(dp_ref.dtype)
        dp_ref[:, A_W + lo:A_W + lo + HEAD] = _rope_t(dk_ref[:, lo:lo + HEAD], c, sa, sb, ROT_A // 2).astype(dp_ref.dtype)
    dp_ref[:, 2 * A_W:3 * A_W] = dv_ref[...].astype(dp_ref.dtype)
    cq = p_ref[:, 3 * A_W:3 * A_W + LORA]
    dcqn = dcq_ref[...]
    dcq, cqh = _rms_bwd(cq, _rstd(cq), dcqn * gq_ref[...])
    dp_ref[:, 3 * A_W:3 * A_W + LORA] = dcq.astype(dp_ref.dtype)
    _acc(dgq_ref, _fold8(dcqn * cqh))
    ckv = p_ref[:, 3 * A_W + LORA:MAIN_COLS]
    dckvn = dckv_ref[...]
    dckv, ckvh = _rms_bwd(ckv, _rstd(ckv), dckvn * gkv_ref[...])
    dp_ref[:, 3 * A_W + LORA:MAIN_COLS] = dckv.astype(dp_ref.dtype)
    _acc(dgkv_ref, _fold8(dckvn * ckvh))
    dkr = dkr_ref[:, 0:HEAD]
    for h in range(1, NH):
        dkr = dkr + dkr_ref[:, h * HEAD:(h + 1) * HEAD]
    dp_ref[:, MAIN_COLS:PROJ_COLS] = _rope_t(dkr, cb[...], sba[...], sbb[...], ROPE_B // 2).astype(dp_ref.dtype)


def _bin_body(dx1_ref, dh_ref, x_ref, g1_ref, dx_ref, dg1_ref):
    x = x_ref[...]
    dh = dh_ref[...]
    dn, xh = _rms_bwd(x, _rstd(x), dh * g1_ref[...])
    dx_ref[...] = dx1_ref[...] + dn
    _acc(dg1_ref, _fold8(dh * xh))


def _dot_nt(a, b):
    return lax.dot_general(a, b, _DIMS["nt"], preferred_element_type=F32)


def _dot_tn(a, b):
    return lax.dot_general(a, b, _DIMS["tn"], preferred_element_type=F32)


def _dot_nn(a, b):
    return jnp.dot(a, b, preferred_element_type=F32)


DIL_SCALE = HEAD ** -0.5
DIL_CHUNK = 256


def _dil_rows(t, d, chain=0):
    if chain:
        r = t >> (chain.bit_length() - 1)
        n = t & (chain - 1)
    else:
        r = t & (d - 1)
        n = t >> (d.bit_length() - 1)
    start = r + n * (HEAD * d)
    has_prev = n > 0
    pstart = jnp.where(has_prev, start - HEAD * d, start)
    if d == 1:
        return pl.ds(pl.multiple_of(start, HEAD), HEAD), pl.ds(pl.multiple_of(pstart, HEAD), HEAD), has_prev
    return pl.ds(start, HEAD, stride=d), pl.ds(pstart, HEAD, stride=d), has_prev


def _dil_band():
    row = lax.broadcasted_iota(jnp.int32, (HEAD, 2 * HEAD), 0)
    col = lax.broadcasted_iota(jnp.int32, (HEAD, 2 * HEAD), 1)
    return (col >= row) & (col <= row + HEAD), col >= HEAD


def _dil_fwd_body(q_ref, k_ref, v_ref, a_ref, lse_ref, o1, o2, o3, l1, l2, l3, *, nt, unroll):
    band, is_cur = _dil_band()
    for d, o_sc, l_sc in zip(DIL, (o1, o2, o3), (l1, l2, l3)):

        def tiles(g, carry, d=d, o_sc=o_sc, l_sc=l_sc):
            staged = []
            for u in range(unroll):
                rows, prows, has_prev = _dil_rows(g * unroll + u, d)
                q = q_ref[rows, :].astype(MXU_DTYPE)
                kk = jnp.concatenate([k_ref[prows, :], k_ref[rows, :]], axis=0).astype(MXU_DTYPE)
                staged.append((rows, prows, has_prev, _dot_nt(q, kk)))
            for rows, prows, has_prev, s in staged:
                vv = jnp.concatenate([v_ref[prows, :], v_ref[rows, :]], axis=0).astype(MXU_DTYPE)
                ok = band & (is_cur | has_prev)
                s = jnp.where(ok, s * DIL_SCALE, NEG)
                m = jnp.max(s, axis=1, keepdims=True)
                p = jnp.exp(s - m)
                den = jnp.sum(p, axis=1, keepdims=True)
                o_sc[rows, :] = _dot_nn((p / den).astype(MXU_DTYPE), vv)
                l_sc[rows, :] = jnp.broadcast_to(m + jnp.log(den), (HEAD, HEAD))
            return carry

        lax.fori_loop(0, nt // unroll, tiles, 0)

    def merge(i, carry):
        rs = pl.ds(pl.multiple_of(i * DIL_CHUNK, DIL_CHUNK), DIL_CHUNK)
        la, lb, lc = l1[rs, :], l2[rs, :], l3[rs, :]
        m = jnp.maximum(jnp.maximum(la, lb), lc)
        wa, wb, wc = jnp.exp(la - m), jnp.exp(lb - m), jnp.exp(lc - m)
        den = wa + wb + wc
        a = (wa / den) * o1[rs, :] + (wb / den) * o2[rs, :] + (wc / den) * o3[rs, :]
        a_ref[rs, :] = a.astype(a_ref.dtype)
        lse_ref[rs, :] = m + jnp.log(den)
        return carry

    lax.fori_loop(0, q_ref.shape[0] // DIL_CHUNK, merge, 0)


def _dil_fwd(q, k, v):
    T = q.shape[0]
    spec = pl.BlockSpec((T, HEAD), lambda h: (0, h))
    return _pcall(
        functools.partial(_dil_fwd_body, nt=T // HEAD, unroll=16), name="dil_fwd",
        grid=(NH,), in_specs=[spec] * 3, out_specs=[spec] * 2,
        out_shape=[jax.ShapeDtypeStruct((T, 2 * A_W), MXU_DTYPE), jax.ShapeDtypeStruct((T, A_W), F32)],
        scratch_shapes=[pltpu.VMEM((T, HEAD), F32)] * 6,
        compiler_params=pltpu.CompilerParams(dimension_semantics=("parallel",)),
    )(q, k, v)


def _dil_bwd_body(q_ref, k_ref, v_ref, do_ref, a_ref, lse_ref, dq_ref, dk_ref, dv_ref, dl_sc, *, nt, unroll):
    band, is_cur = _dil_band()

    def prep(i, carry):
        rs = pl.ds(pl.multiple_of(i * DIL_CHUNK, DIL_CHUNK), DIL_CHUNK)
        dl = jnp.sum(do_ref[rs, :] * a_ref[rs, :].astype(F32), axis=1, keepdims=True)
        dl_sc[rs, :] = jnp.broadcast_to(dl, (DIL_CHUNK, HEAD))
        zero = jnp.zeros((DIL_CHUNK, HEAD), F32)
        dq_ref[rs, :] = zero
        dk_ref[rs, :] = zero
        dv_ref[rs, :] = zero
        return carry

    lax.fori_loop(0, q_ref.shape[0] // DIL_CHUNK, prep, 0)

    for d in DIL:
        chain = nt // d
        linked = min(chain, unroll)
        assert unroll % linked == 0

        def tiles(g, carry, d=d, chain=chain, linked=linked):
            staged = []
            for u in range(unroll):
                rows, prows, has_prev = _dil_rows(g * unroll + u, d, chain)
                q = q_ref[rows, :].astype(MXU_DTYPE)
                do = do_ref[rows, :].astype(MXU_DTYPE)
                kc = k_ref[rows, :].astype(MXU_DTYPE)
                vc = v_ref[rows, :].astype(MXU_DTYPE)
                if u % linked:
                    kp, vp = staged[-1][5], staged[-1][6]
                else:
                    kp = k_ref[prows, :].astype(MXU_DTYPE)
                    vp = v_ref[prows, :].astype(MXU_DTYPE)
                kk = jnp.concatenate([kp, kc], axis=0)
                vv = jnp.concatenate([vp, vc], axis=0)
                staged.append((rows, prows, has_prev, q, do, kc, vc, kk, _dot_nt(q, kk), _dot_nt(do, vv)))

            def add_own(own):
                rows, dk_own, dv_own = own
                dk_ref[rows, :] += dk_own
                dv_ref[rows, :] += dv_own

            own = None
            for u, (rows, prows, has_prev, q, do, _, _, kk, s, dp) in enumerate(staged):
                lse = lse_ref[rows, :]
                dl = dl_sc[rows, :]
                ok = band & (is_cur | has_prev)
                p = jnp.where(ok, jnp.exp(s * DIL_SCALE - jnp.concatenate([lse, lse], axis=1)), 0.0)
                ds = (p * (dp - jnp.concatenate([dl, dl], axis=1))).astype(MXU_DTYPE)
                dq_ref[rows, :] += _dot_nn(ds, kk) * DIL_SCALE
                dkk = _dot_tn(ds, q) * DIL_SCALE
                dvv = _dot_tn(p.astype(MXU_DTYPE), do)
                if u % linked:
                    add_own((own[0], own[1] + dkk[:HEAD, :], own[2] + dvv[:HEAD, :]))
                else:
                    if own is not None:
                        add_own(own)
                    dk_ref[prows, :] += dkk[:HEAD, :]
                    dv_ref[prows, :] += dvv[:HEAD, :]
                own = (rows, dkk[HEAD:, :], dvv[HEAD:, :])
            add_own(own)
            return carry

        lax.fori_loop(0, nt // unroll, tiles, 0)


def _dil_bwd(q, k, v, dmix, mixed, lse):
    T = q.shape[0]
    spec = pl.BlockSpec((T, HEAD), lambda h: (0, h))
    return _pcall(
        functools.partial(_dil_bwd_body, nt=T // HEAD, unroll=8), name="dil_bwd",
        grid=(NH,), in_specs=[spec] * 6, out_specs=[spec] * 3,
        out_shape=[jax.ShapeDtypeStruct((T, A_W), F32)] * 3,
        scratch_shapes=[pltpu.VMEM((T, HEAD), F32)],
        compiler_params=pltpu.CompilerParams(dimension_semantics=("parallel",)),
    )(q, k, v, dmix, mixed, lse)


MLA_SCALE = (HEAD + ROPE_B) ** -0.5
LOG2E = 1.4426950408889634
MLA_QSCALE = MLA_SCALE * LOG2E
MLA_T = 512
MLA_HP = 4


def _tri(t):
    row = lax.broadcasted_iota(jnp.int32, (t, t), 0)
    col = lax.broadcasted_iota(jnp.int32, (t, t), 1)
    return col <= row


def _lanes(x, n):
    return jnp.tile(x, (1, n // HEAD))


def _mla_fwd_body(q_ref, kv_ref, kr_ref, mixed_ref, o_ref, lse_ref, m_sc, l_sc, acc_sc, *, t, hp):
    del mixed_ref
    qi = pl.program_id(1)
    m_sc[...] = jnp.full(m_sc.shape, NEG, F32)
    l_sc[...] = jnp.zeros(l_sc.shape, F32)
    acc_sc[...] = jnp.zeros(acc_sc.shape, F32)

    def step(j, masked):
        ks = pl.ds(pl.multiple_of(j * t, t), t)
        kr = kr_ref[ks, :]
        logits = []
        for hh in range(hp):
            kcat = jnp.concatenate([kv_ref[ks, 2 * hh * HEAD:(2 * hh + 1) * HEAD], kr], axis=1)
            logits.append(_dot_nt(q_ref[:, hh * QPAD:(hh + 1) * QPAD], kcat))
        for hh in range(hp):
            s = logits[hh]
            if masked:
                s = jnp.where(_tri(t), s, NEG)
            m_prev = m_sc[hh]
            m_new = jnp.maximum(m_prev, jnp.max(s, axis=1, keepdims=True))
            alpha = jnp.exp2(m_prev - m_new)
            p = jnp.exp2(s - _lanes(m_new, t))
            l_sc[hh] = alpha * l_sc[hh] + jnp.sum(p, axis=1, keepdims=True)
            acc_sc[hh] = alpha * acc_sc[hh] + _dot_nn(p.astype(MXU_DTYPE), kv_ref[ks, (2 * hh + 1) * HEAD:(2 * hh + 2) * HEAD])
            m_sc[hh] = m_new

    def off_diag(j, carry):
        step(j, False)
        return carry

    lax.fori_loop(0, qi, off_diag, 0)
    step(qi, True)
    for hh in range(hp):
        l = l_sc[hh]
        o_ref[:, hh * HEAD:(hh + 1) * HEAD] = (acc_sc[hh] / l).astype(o_ref.dtype)
        lse_ref[:, hh * HEAD:(hh + 1) * HEAD] = m_sc[hh] + jnp.log2(l)


def _mla_fwd(qf, kv, kr, mixed):
    T = qf.shape[0]
    t, hp = min(MLA_T, T), MLA_HP
    ng = NH // hp
    return _pcall(
        functools.partial(_mla_fwd_body, t=t, hp=hp), name="mla_fwd",
        grid=(ng, T // t),
        in_specs=[pl.BlockSpec((t, hp * QPAD), lambda g, i: (i, g)),
                  pl.BlockSpec((T, hp * 2 * HEAD), lambda g, i: (0, g)),
                  pl.BlockSpec((T, HEAD), lambda g, i: (0, 0)), ANY],
        out_specs=[pl.BlockSpec((t, hp * HEAD), lambda g, i: (i, ng + g)),
                   pl.BlockSpec((t, hp * HEAD), lambda g, i: (i, g))],
        out_shape=[jax.ShapeDtypeStruct(mixed.shape, mixed.dtype), jax.ShapeDtypeStruct((T, A_W), F32)],
        input_output_aliases={3: 0},
        scratch_shapes=[pltpu.VMEM((hp, t, HEAD), F32)] * 3,
        compiler_params=pltpu.CompilerParams(dimension_semantics=("parallel", "parallel")),
    )(qf, kv, kr, mixed)


def _mla_bwd_body(q_ref, kn_ref, kr_ref, v_ref, do_ref, o_ref, lse_ref, cb, sba, sbb,
                  dq_ref, dkv_ref, dkr_ref, dq_sc, dl_sc, dk_sc, dv_sc, *, t):
    ki = pl.program_id(1)
    nq = q_ref.shape[0] // t

    @pl.when(ki == 0)
    def _():
        def prep(i, carry):
            rs = pl.ds(pl.multiple_of(i * t, t), t)
            dl = jnp.sum(do_ref[rs, :] * o_ref[rs, :].astype(F32), axis=1, keepdims=True)
            dl_sc[rs, :] = jnp.broadcast_to(dl, (t, HEAD))
            dq_sc[rs, :] = jnp.zeros((t, QPAD), F32)
            return carry
        lax.fori_loop(0, nq, prep, 0)

    kcat = jnp.concatenate([kn_ref[...], kr_ref[...]], axis=1)
    v = v_ref[...]
    dk_sc[...] = jnp.zeros(dk_sc.shape, F32)
    dv_sc[...] = jnp.zeros(dv_sc.shape, F32)

    def steps(blocks):
        staged = []
        for i, masked in blocks:
            qs = pl.ds(pl.multiple_of(i * t, t), t)
            q = q_ref[qs, :]
            do = do_ref[qs, :].astype(MXU_DTYPE)
            staged.append((qs, q, do, _dot_nt(q, kcat), _dot_nt(do, v), masked))
        for qs, q, do, s, dp, masked in staged:
            p = jnp.exp2(s - _lanes(lse_ref[qs, :], t))
            if masked:
                p = jnp.where(_tri(t), p, 0.0)
            ds = (p * (dp - _lanes(dl_sc[qs, :], t))).astype(MXU_DTYPE)
            dv_sc[...] += _dot_tn(p.astype(MXU_DTYPE), do)
            dk_sc[...] += _dot_tn(ds, q)
            dq_sc[qs, :] += _dot_nn(ds, kcat) * MLA_SCALE

    n_blocks = nq - ki

    @pl.when(n_blocks == 1)
    def _():
        steps([(ki, True)])

    @pl.when(n_blocks > 1)
    def _():
        steps([(ki, True), (ki + 1, False)])

    def pair(j, carry):
        steps([(ki + 2 * j, False), (ki + 2 * j + 1, False)])
        return carry

    lax.fori_loop(1, n_blocks // 2, pair, 0)

    @pl.when((n_blocks > 1) & (n_blocks % 2 == 1))
    def _():
        steps([(nq - 1, False)])

    dk = dk_sc[...] * (1.0 / LOG2E)
    dkv_ref[:, 0:HEAD] = dk[:, 0:HEAD].astype(dkv_ref.dtype)
    dkr_ref[...] = dk[:, HEAD:QPAD]
    dkv_ref[:, HEAD:] = dv_sc[...].astype(dkv_ref.dtype)

    @pl.when(ki == nq - 1)
    def _():
        def emit(i, carry):
            rs = pl.ds(pl.multiple_of(i * t, t), t)
            dq_ref[rs, 0:HEAD] = dq_sc[rs, 0:HEAD].astype(dq_ref.dtype)
            dq_ref[rs, HEAD:QPAD] = _rope_t(dq_sc[rs, HEAD:QPAD], cb[rs, :], sba[rs, :], sbb[rs, :],
                                            ROPE_B // 2).astype(dq_ref.dtype)
            return carry
        lax.fori_loop(0, nq, emit, 0)


def _mla_bwd(qf, kv, kr, dmix, mixed, lse, tabs_b):
    T = qf.shape[0]
    t = min(MLA_T, T)
    head = lambda h, j: (0, h)
    b_half = lambda h, j: (0, NH + h)
    kblk = pl.BlockSpec((t, HEAD), lambda h, j: (j, h))
    return _pcall(
        functools.partial(_mla_bwd_body, t=t), name="mla_bwd",
        grid=(NH, T // t),
        in_specs=[pl.BlockSpec((T, QPAD), head), pl.BlockSpec((t, HEAD), lambda h, j: (j, 2 * h)),
                  pl.BlockSpec((t, HEAD), lambda h, j: (j, 0)),
                  pl.BlockSpec((t, HEAD), lambda h, j: (j, 2 * h + 1)),
                  pl.BlockSpec((T, HEAD), b_half), pl.BlockSpec((T, HEAD), b_half),
                  pl.BlockSpec((T, HEAD), head)] + [pl.BlockSpec((T, HEAD), lambda h, j: (0, 0))] * 3,
        out_specs=[pl.BlockSpec((T, QPAD), head), pl.BlockSpec((t, 2 * HEAD), lambda h, j: (j, h)), kblk],
        out_shape=[jax.ShapeDtypeStruct((T, NH * QPAD), MXU_DTYPE), jax.ShapeDtypeStruct((T, 2 * A_W), MXU_DTYPE),
                   jax.ShapeDtypeStruct((T, A_W), F32)],
        scratch_shapes=[pltpu.VMEM((T, QPAD), F32), pltpu.VMEM((T, HEAD), F32), pltpu.VMEM((t, QPAD), F32),
                        pltpu.VMEM((t, HEAD), F32)],
        compiler_params=pltpu.CompilerParams(dimension_semantics=("parallel", "arbitrary")),
    )(qf, kv, kr, kv, dmix, mixed, lse, *tabs_b)


def _local_step(x, pos, target, g1, g2, gq, gkv, g3, g4,
                in_weights, attn_prefetch, attn_weights, mlp_prefetch, mlp_weights,
                down_grad_ready, up_grad_ready, attn_grads_ready):
    T = x.shape[0]
    TR = 256
    mm = functools.partial(_matmul, tm=2048, tn=1024, tk=2048, b_outer=True)
    mm_k = functools.partial(_matmul, tm=1024, tn=1024, tk=2048)
    mm_g = functools.partial(_matmul, tm=1024, tn=1024, tk=4096, b_outer=True)

    inv_a = ROPE_THETA ** (-jnp.arange(0, ROT_A, 2, dtype=F32) / ROT_A)
    inv_b = ROPE_THETA ** (-jnp.arange(0, ROPE_B, 2, dtype=F32) / ROPE_B)
    inv = jnp.stack([jnp.concatenate([inv_a, inv_a, jnp.zeros((HEAD - ROT_A,), F32)]),
                     jnp.concatenate([inv_b, inv_b, jnp.zeros((HEAD - ROPE_B,), F32)])])
    inv = jnp.concatenate([inv, jnp.zeros((6, HEAD), F32)], axis=0)
    tabs = _rowwise(_rope_tab_body, [pos], [inv], [(HEAD, F32)] * 6, [], tr=512, name="rope_tables")

    (h,) = _rowwise(_rms_fwd_body, [x], [g1], [(D_MODEL, MXU_DTYPE)], [], tr=TR, name="rms_in")
    w_proj = in_weights([h, tabs[0]])
    (proj,) = mm(h, w_proj, dims="nt", out_dtypes=[F32], tm=1024, tn=PROJ_TILE, name="proj_in")
    gq = gq + attn_prefetch(proj)
    q, k, v, cqn, ckvn, krope = _rowwise(
        _postproj_body, [proj] + tabs, [gq, gkv],
        [(A_W, F32)] * 3 + [(LORA, MXU_DTYPE)] * 2 + [(HEAD, MXU_DTYPE)], [], tr=TR, name="post_proj")
    mixed, lse_a = _dil_fwd(q, k, v)

    w_uq_p, w_ukv, w_out = attn_weights(cqn)

    def q_epi(acc, cb, sba, sbb):
        cols = []
        for hh in range(acc.shape[1] // QPAD):
            lo = hh * QPAD
            cols += [acc[:, lo:lo + HEAD], _rope(acc[:, lo + HEAD:lo + QPAD], cb, sba, sbb, ROPE_B // 2)]
        return (jnp.concatenate(cols, axis=1) * MLA_QSCALE,)
    (qf,) = mm(cqn, w_uq_p, dims="nn", out_dtypes=[MXU_DTYPE], name="q_up", epi=q_epi, row_extras=tuple(tabs[3:]))
    (kv,) = mm(ckvn, w_ukv, dims="nn", out_dtypes=[MXU_DTYPE], name="kv_up")
    mixed, lse_b = _mla_fwd(qf, kv, krope, mixed)
    (o,) = mm(mixed, w_out, dims="nn", out_dtypes=[F32], name="out_proj", after=mlp_prefetch(mixed))
    x1, h2 = _rowwise(_mid_body, [x, o], [g2, g3], [(D_MODEL, F32), (D_MODEL, MXU_DTYPE)], [], tr=TR, name="mid_norm")

    w_up, w_down = mlp_weights(h2)

    def up_epi(acc):
        r = jnp.maximum(acc, 0.0)
        return r * r, r
    u, r = mm(h2, w_up, dims="nn", out_dtypes=[MXU_DTYPE, MXU_DTYPE], name="mlp_up", epi=up_epi, b_shards=N_CHIPS)
    (dn,) = mm_k(u, w_down, dims="nn", out_dtypes=[F32], name="mlp_down")
    dy, dd, loss8, dg4 = _rowwise(_loss_body, [x1, dn, target], [g4], [(D_MODEL, F32), (D_MODEL, MXU_DTYPE)],
                                  [(8, HEAD), (8, D_MODEL)], tr=TR, name="loss_head")

    def dup_epi(acc, rr):
        return (acc * (2.0 * rr.astype(F32)),)
    (dup,) = mm(dd, w_down, dims="nt", out_dtypes=[MXU_DTYPE], name="d_up", epi=dup_epi, extras=(r,))
    (gw_down,) = mm_g(u, dd, dims="tn", out_dtypes=[WIRE_DTYPE], name="gw_down")
    (dh2,) = mm_k(dup, w_up, dims="nt", out_dtypes=[F32], name="d_h2", b_shards=N_CHIPS,
                  after=down_grad_ready(gw_down))
    (gw_up,) = mm_g(h2, dup, dims="tn", out_dtypes=[WIRE_DTYPE], name="gw_up", out_shards=N_CHIPS)
    g2 = g2 + up_grad_ready(gw_up)
    dx1, do, dg3, dg2 = _rowwise(_bmid_body, [dy, dh2, x1, o], [g2, g3], [(D_MODEL, F32), (D_MODEL, MXU_DTYPE)],
                                 [(8, D_MODEL), (8, D_MODEL)], tr=TR, name="bwd_mid")
    (dmix,) = mm(do, w_out, dims="nt", out_dtypes=[F32], name="d_mixed")
    (gw_out,) = mm_g(mixed, do, dims="tn", out_dtypes=[WIRE_DTYPE], name="gw_out")

    dq_pad, dkv, dkr = _mla_bwd(qf, kv, krope, dmix, mixed, lse_b, tabs[3:])
    (dcqn,) = mm(dq_pad, w_uq_p, dims="nt", out_dtypes=[F32], name="d_cq")
    (gw_uq_p,) = mm_g(cqn, dq_pad, dims="tn", out_dtypes=[WIRE_DTYPE], name="gw_uq")
    (dckvn,) = mm(dkv, w_ukv, dims="nt", out_dtypes=[F32], name="d_ckv")
    (gw_ukv,) = mm_g(ckvn, dkv, dims="tn", out_dtypes=[WIRE_DTYPE], name="gw_ukv")
    gq = gq + attn_grads_ready(gw_out, gw_uq_p, gw_ukv)

    dq_a, dk_a, dv_a = _dil_bwd(q, k, v, dmix, mixed, lse_a)
    dproj, dgq, dgkv = _rowwise(
        _dproj_body, [dq_a, dk_a, dv_a, dcqn, dckvn, proj, dkr] + tabs, [gq, gkv],
        [(PROJ_COLS, MXU_DTYPE)], [(8, LORA), (8, LORA)], tr=TR, name="d_proj")
    (dh,) = mm_k(dproj, w_proj, dims="nn", out_dtypes=[F32], tk=PROJ_TILE, name="d_h")
    (gw_proj,) = mm_g(dproj, h, dims="tn", out_dtypes=[WIRE_DTYPE], tm=PROJ_TILE, name="gw_in")
    dx, dg1 = _rowwise(_bin_body, [dx1, dh, x], [g1], [(D_MODEL, F32)], [(8, D_MODEL)], tr=TR, name="bwd_in")

    small = jnp.concatenate([dg1, dg2, dgq, dgkv, dg3, dg4, loss8], axis=1)
    return dx, gw_proj, small


def _place():
    x, y, c = lax.axis_index("x"), lax.axis_index("y"), lax.axis_index("c")
    chips = [(1 - x, y), (x, 1 - y), (1 - x, 1 - y)]
    return x, y, c, chips


def _cast_place_body(me_ref, w_ref, *rest):
    o_ref = rest[-1]
    o_ref[...] = w_ref[...].astype(o_ref.dtype)


def _cast_place(me_arr, w, name, after=None):
    rows, cols = w.shape
    tr = min(rows, 256)
    after = [] if after is None else [after]
    grid_spec = pltpu.PrefetchScalarGridSpec(
        num_scalar_prefetch=1, grid=(rows // tr,),
        in_specs=[pl.BlockSpec((tr, cols), lambda i, me: (i, 0))] + [ANY] * len(after),
        out_specs=pl.BlockSpec((None, tr, cols), lambda i, me: (me[0], i, 0)))
    return _pcall(
        _cast_place_body, name=name, grid_spec=grid_spec,
        out_shape=jax.ShapeDtypeStruct((N_CHIPS, rows, cols), WIRE_DTYPE),
        compiler_params=pltpu.CompilerParams(dimension_semantics=("parallel",)),
    )(me_arr, w, *after)


def _cast_place_t_body(me_ref, w_ref, o_ref, *, n):
    i = pl.program_id(0)

    @pl.when(i < n)
    def _():
        o_ref[...] = w_ref[...].astype(o_ref.dtype)

    @pl.when(i == n)
    def _():
        o_ref[...] = jnp.zeros_like(o_ref)


def _cast_place_t(me_arr, w_t, name):
    rows, cols = w_t.shape
    n = rows // IN_TR
    grid_spec = pltpu.PrefetchScalarGridSpec(
        num_scalar_prefetch=1, grid=(n + 1,),
        in_specs=[pl.BlockSpec((IN_TR, cols), lambda i, me: (jnp.minimum(i, n - 1), 0))],
        out_specs=pl.BlockSpec((IN_TR, cols), lambda i, me: (jnp.where(i < n, me[0] * n + i, N_CHIPS * n), 0)))
    return _pcall(
        functools.partial(_cast_place_t_body, n=n), name=name, grid_spec=grid_spec,
        out_shape=jax.ShapeDtypeStruct((PROJ_COLS, cols), WIRE_DTYPE),
        compiler_params=pltpu.CompilerParams(dimension_semantics=("arbitrary",)),
    )(me_arr, w_t)


HBM = pl.BlockSpec(memory_space=pltpu.HBM)
SEM = pl.BlockSpec(memory_space=pltpu.SEMAPHORE)
EFFECT = pltpu.SideEffectType.DATAFLOW_SIDE_EFFECTING


def _copy_start(make, arrays, after, name, n_sems):
    n_a = len(arrays)
    after = [] if after is None else [after]

    def body(*refs):
        for send, _ in make(refs[:n_a], refs[-n_a - 3], refs[-n_a - 2]):
            send.start()
        refs[-1][...] = jnp.zeros_like(refs[-1])

    res = _pcall(
        body, name=name,
        in_specs=[HBM] * n_a + [ANY] * len(after),
        out_specs=[SEM, SEM] + [HBM] * n_a + [pl.BlockSpec(memory_space=pltpu.VMEM)],
        out_shape=[pltpu.SemaphoreType.DMA((n_sems,)), pltpu.SemaphoreType.DMA((n_sems,))]
        + [pltpu.HBM(a.shape, a.dtype) for a in arrays] + [jax.ShapeDtypeStruct((8, HEAD), F32)],
        input_output_aliases={i: 2 + i for i in range(n_a)},
        compiler_params=pltpu.CompilerParams(has_side_effects=EFFECT),
    )(*[pltpu.with_memory_space_constraint(a, pltpu.HBM) for a in arrays], *after)
    return (res[0], res[1]), list(res[2:2 + n_a]), res[-1]


def _copy_wait(make, sems, arrays, after, name):
    n_a = len(arrays)
    after = list(after) if isinstance(after, (list, tuple)) else [after]

    def body(*refs):
        for send, recv in make(refs[:n_a], refs[n_a], refs[n_a + 1]):
            send.wait_send()
            recv.wait_recv()

    return list(_pcall(
        body, name=name,
        in_specs=[HBM] * n_a + [SEM, SEM] + [ANY] * len(after), out_specs=[HBM] * n_a,
        out_shape=[pltpu.HBM(a.shape, a.dtype) for a in arrays],
        input_output_aliases={i: i for i in range(n_a)},
        compiler_params=pltpu.CompilerParams(has_side_effects=EFFECT),
    )(*arrays, sems[0], sems[1], *after))


def _slot(buf, chip, half):
    if buf.ndim == 2:
        hc = buf.shape[1] // 2
        return buf.at[pl.ds(pl.multiple_of(chip * IN_SHARD, 16), IN_SHARD), pl.ds(pl.multiple_of(half * hc, HEAD), hc)]
    hr = buf.shape[1] // 2
    return buf.at[chip, pl.ds(pl.multiple_of(half * hr, 16), hr)]


def _ag_descs(bufs, send_sems, recv_sems):
    x, y, c, chips = _place()
    me = 2 * x + y
    out = []
    for w, buf in enumerate(bufs):
        for j, (px, py) in enumerate(chips):
            mk = lambda ref, w=w, j=j, px=px, py=py: pltpu.make_async_remote_copy(
                src_ref=ref, dst_ref=ref, send_sem=send_sems.at[w * 3 + j], recv_sem=recv_sems.at[w * 3 + j],
                device_id=(px, py, c), device_id_type=MESH)
            out.append((mk(_slot(buf, me, c)), mk(_slot(buf, 2 * px + py, c))))
    return out


def _fw_descs(bufs, send_sems, recv_sems):
    x, y, c, chips = _place()
    out = []
    for w, buf in enumerate(bufs):
        for j, (px, py) in enumerate(chips):
            def mk(which, w=w, j=j, buf=buf, px=px, py=py):
                ref = _slot(buf, 2 * px + py, which)
                return pltpu.make_async_remote_copy(
                    src_ref=ref, dst_ref=ref, send_sem=send_sems.at[w * 3 + j], recv_sem=recv_sems.at[w * 3 + j],
                    device_id=(x, y, 1 - c), device_id_type=MESH)
            out.append((mk(c), mk(1 - c)))
    return out


def _sc_descs(refs, send_sems, recv_sems):
    n_w = len(refs) // 2
    x, y, c, chips = _place()
    me = 2 * x + y
    out = []
    for w in range(n_w):
        for j, (px, py) in enumerate(chips):
            d = pltpu.make_async_remote_copy(
                src_ref=refs[w].at[2 * px + py], dst_ref=refs[n_w + w].at[me],
                send_sem=send_sems.at[w * 3 + j], recv_sem=recv_sems.at[w * 3 + j],
                device_id=(px, py, c), device_id_type=MESH)
            out.append((d, d))
    return out


def _pair_descs(src_of):
    def make(refs, send_sems, recv_sems):
        n_w = len(refs) // 2
        x, y, c, _ = _place()
        out = []
        for w in range(n_w):
            d = pltpu.make_async_remote_copy(
                src_ref=src_of(refs[w], c), dst_ref=refs[n_w + w],
                send_sem=send_sems.at[w], recv_sem=recv_sems.at[w],
                device_id=(x, y, 1 - c), device_id_type=MESH)
            out.append((d, d))
        return out
    return make


_EX_DESCS = _pair_descs(lambda g4, c: g4.at[:, 1 - c])
_SW_DESCS = _pair_descs(lambda half, c: half)
_EXT_DESCS = _pair_descs(lambda g, c: g.at[pl.ds(0, IN_COLS),
                                           pl.ds(pl.multiple_of((1 - c) * (D_MODEL // 2), HEAD), D_MODEL // 2)])


def _sm_descs(refs, send_sems, recv_sems):
    buf = refs[0]
    rows8 = buf.shape[0] // N_DEV
    x, y, c, _ = _place()
    flip = lambda v, d: 1 - v if d else v
    blk = lambda px, py, pc: buf.at[pl.ds(pl.multiple_of((4 * px + 2 * py + pc) * rows8, 8), rows8)]
    out = []
    for k in range(1, N_DEV):
        px, py, pc = flip(x, k & 4), flip(y, k & 2), flip(c, k & 1)
        mk = lambda ref, k=k, px=px, py=py, pc=pc: pltpu.make_async_remote_copy(
            src_ref=ref, dst_ref=ref, send_sem=send_sems.at[k - 1], recv_sem=recv_sems.at[k - 1],
            device_id=(px, py, pc), device_id_type=MESH)
        out.append((mk(blk(x, y, c)), mk(blk(px, py, pc))))
    return out


def _place_rows_body(i_ref, x_ref, o_ref):
    o_ref[...] = x_ref[...]


def _place_rows(i_arr, x, n_blocks, name):
    r, n = x.shape
    grid_spec = pltpu.PrefetchScalarGridSpec(
        num_scalar_prefetch=1, grid=(1,),
        in_specs=[pl.BlockSpec((r, n), lambda g, i: (0, 0))],
        out_specs=pl.BlockSpec((r, n), lambda g, i: (i[0], 0)))
    return _pcall(_place_rows_body, name=name, grid_spec=grid_spec,
                  out_shape=jax.ShapeDtypeStruct((n_blocks * r, n), x.dtype))(i_arr, x)


def _ag_forward_body(*refs, n_w):
    bufs = refs[n_w:2 * n_w]
    send_sems, recv_sems = refs[2 * n_w:]
    pairs = _fw_descs(bufs, send_sems, recv_sems)
    for fw, _ in pairs:
        fw.start()
    for fw, back in pairs:
        back.wait_recv()
        fw.wait_send()


def _ag_forward(bufs, tag):
    n_w = len(bufs)
    return list(_pcall(
        functools.partial(_ag_forward_body, n_w=n_w), name="weight_allgather_forward_" + tag,
        in_specs=[ANY] * n_w, out_specs=[ANY] * n_w,
        out_shape=[jax.ShapeDtypeStruct(b.shape, b.dtype) for b in bufs],
        input_output_aliases={w: w for w in range(n_w)},
        scratch_shapes=[pltpu.SemaphoreType.DMA((3 * n_w,))] * 2,
    )(*bufs))


def _pair_add_body(c_ref, mine_ref, theirs_ref, o_ref):
    o_ref[...] = (mine_ref[...].astype(F32) + theirs_ref[...].astype(F32)).astype(o_ref.dtype)


def _pair_add(c_arr, g4, recv, name):
    _, _, hr, cols = g4.shape
    tr = min(hr, 256)
    grid_spec = pltpu.PrefetchScalarGridSpec(
        num_scalar_prefetch=1, grid=(N_CHIPS, hr // tr),
        in_specs=[pl.BlockSpec((None, None, tr, cols), lambda s, i, c: (s, c[0], i, 0)),
                  pl.BlockSpec((None, tr, cols), lambda s, i, c: (s, i, 0))],
        out_specs=pl.BlockSpec((None, tr, cols), lambda s, i, c: (s, i, 0)))
    return _pcall(
        _pair_add_body, name=name, grid_spec=grid_spec,
        out_shape=jax.ShapeDtypeStruct(recv.shape, recv.dtype),
        compiler_params=pltpu.CompilerParams(dimension_semantics=("parallel", "parallel")),
    )(c_arr, g4, recv)


def _pair_add_t(c_arr, g, recv, name):
    rows, hc = recv.shape
    grid_spec = pltpu.PrefetchScalarGridSpec(
        num_scalar_prefetch=1, grid=(rows // IN_TR,),
        in_specs=[pl.BlockSpec((IN_TR, hc), lambda i, c: (i, c[0])), pl.BlockSpec((IN_TR, hc), lambda i, c: (i, 0))],
        out_specs=pl.BlockSpec((IN_TR, hc), lambda i, c: (i, 0)))
    return _pcall(
        _pair_add_body, name=name, grid_spec=grid_spec,
        out_shape=jax.ShapeDtypeStruct(recv.shape, recv.dtype),
        compiler_params=pltpu.CompilerParams(dimension_semantics=("parallel",)),
    )(c_arr, g, recv)


def _sum4_body(me_ref, p_ref, l0, l1, l2, l3, o_ref):
    me = me_ref[0]
    t = [jnp.where(me == j, p_ref[...], l[...]).astype(F32) for j, l in enumerate((l0, l1, l2, l3))]
    o_ref[...] = ((t[0] + t[1]) + t[2]) + t[3]


def _sum4(me_arr, part, landed, name):
    _, hr, cols = part.shape
    tr = IN_TR if hr == IN_SHARD else min(hr, 256)

    def slot(j):
        return lambda i, me: (jnp.where(me[0] == j, (j + 1) % N_CHIPS, j), i, 0)

    grid_spec = pltpu.PrefetchScalarGridSpec(
        num_scalar_prefetch=1, grid=(hr // tr,),
        in_specs=[pl.BlockSpec((None, tr, cols), lambda i, me: (me[0], i, 0))]
        + [pl.BlockSpec((None, tr, cols), slot(j)) for j in range(N_CHIPS)],
        out_specs=pl.BlockSpec((tr, cols), lambda i, me: (i, 0)))
    return _pcall(
        _sum4_body, name=name, grid_spec=grid_spec,
        out_shape=jax.ShapeDtypeStruct((hr, cols), F32),
        compiler_params=pltpu.CompilerParams(dimension_semantics=("parallel",)),
    )(me_arr, part, landed, landed, landed, landed)


def _adamw(w, g, m, v):
    m = ADAM_B1 * m + (1.0 - ADAM_B1) * g
    v = ADAM_B2 * v + (1.0 - ADAM_B2) * (g * g)
    m_hat = m / (1.0 - ADAM_B1 ** ADAM_STEP)
    v_hat = v / (1.0 - ADAM_B2 ** ADAM_STEP)
    delta = -ADAM_LR * (m_hat / (jnp.sqrt(v_hat) + ADAM_EPS) + ADAM_WD * w)
    return delta, m, v


def _adamw_half_body(h_ref, w_ref, g_in_ref, m_ref, v_ref, *rest):
    g_ref, d_ref, nm_ref, nv_ref, done_ref = rest[-5:]
    done_ref[...] = jnp.zeros_like(done_ref)
    g = g_in_ref[...]
    g_ref[...] = g
    d, m, v = _adamw(w_ref[...], g, m_ref[...], v_ref[...])
    d_ref[...] = d
    nm_ref[...] = m
    nv_ref[...] = v


def _adamw_half(h_arr, w, g_half, m, v, prev, name):
    rows, cols = w.shape
    if g_half.shape[0] == rows:
        tr, nh = IN_TR, rows // IN_TR
        at_half = pl.BlockSpec((tr, cols // 2), lambda i, h: (i, h[0]))
        g_spec = pl.BlockSpec((tr, cols // 2), lambda i, h: (i, 0))
    else:
        tr = min(rows // 2, 128)
        nh = (rows // 2) // tr
        at_half = pl.BlockSpec((tr, cols), lambda i, h: (h[0] * nh + i, 0))
        g_spec = pl.BlockSpec((tr, cols), lambda i, h: (i, 0))
    grid_spec = pltpu.PrefetchScalarGridSpec(
        num_scalar_prefetch=1, grid=(nh,),
        in_specs=[at_half, g_spec, at_half, at_half] + [ANY] * len(prev),
        out_specs=[at_half] * 4 + [pl.BlockSpec((8, HEAD), lambda i, h: (0, 0))])
    return list(_pcall(
        _adamw_half_body, name=name, grid_spec=grid_spec,
        out_shape=[jax.ShapeDtypeStruct(w.shape, F32)] * 4 + [jax.ShapeDtypeStruct((8, HEAD), F32)],
        input_output_aliases={5 + k: k for k in range(len(prev))},
        compiler_params=pltpu.CompilerParams(dimension_semantics=("arbitrary",)),
    )(h_arr, w, g_half, m, v, *prev))


def _small_update_body(gath_ref, w_ref, m_ref, v_ref, g_ref, d_ref, nm_ref, nv_ref, loss_ref, *, n_gain):
    tot = gath_ref[0:1, :]
    for i in range(1, gath_ref.shape[0]):
        tot = tot + gath_ref[i:i + 1, :]
    g = tot[:, 0:n_gain]
    g_ref[...] = g
    d, m, v = _adamw(w_ref[...], g, m_ref[...], v_ref[...])
    d_ref[...] = d
    nm_ref[...] = m
    nv_ref[...] = v
    loss_ref[...] = (0.5 / D_MODEL) * jnp.sum(tot[:, n_gain:n_gain + HEAD], axis=1, keepdims=True) * jnp.ones((1, HEAD), F32)


def _small_update(gath, w, m, v):
    n_gain = w.shape[1]
    vm = pl.BlockSpec(memory_space=pltpu.VMEM)
    return _pcall(
        functools.partial(_small_update_body, n_gain=n_gain), name="gain_update",
        in_specs=[vm] * 4, out_specs=[vm] * 5,
        out_shape=[jax.ShapeDtypeStruct((1, n_gain), F32)] * 4 + [jax.ShapeDtypeStruct((1, HEAD), F32)],
    )(gath, w, m, v)


def kernel(x, positions, norm_attn_pre, norm_attn_post, w_in, q_latent_norm, kv_latent_norm, w_uq, w_ukv, w_out, norm_mlp_pre, norm_mlp_post, w_up, w_down, loss_target, m_norm_attn_pre, m_norm_attn_post, m_w_in, m_q_latent_norm, m_kv_latent_norm, m_w_uq, m_w_ukv, m_w_out, m_norm_mlp_pre, m_norm_mlp_post, m_w_up, m_w_down, v_norm_attn_pre, v_norm_attn_post, v_w_in, v_q_latent_norm, v_kv_latent_norm, v_w_uq, v_w_ukv, v_w_out, v_norm_mlp_pre, v_norm_mlp_post, v_w_up, v_w_down):
    T = x.shape[1]
    c_arr = lax.axis_index("c").astype(jnp.int32).reshape(1)
    me_arr = (2 * lax.axis_index("x") + lax.axis_index("y")).astype(jnp.int32).reshape(1)
    names = ["w_in", "w_uq", "w_ukv", "w_out", "w_up", "w_down"]

    transposed = lambda a: jnp.swapaxes(a, 1, 2)
    mats = [transposed(w_in)[0], w_uq[0], w_ukv[0], w_out[0], w_up[0], w_down[0]]
    me8_arr = (4 * lax.axis_index("x") + 2 * lax.axis_index("y") + lax.axis_index("c")).astype(jnp.int32).reshape(1)
    col_major = lambda g: jnp.transpose(g, (1, 0, 2)).reshape(g.shape[1], N_CHIPS * g.shape[2])
    cast = lambda a: a.astype(MXU_DTYPE)
    to_shards = lambda g: jnp.transpose(g.reshape(g.shape[0], N_CHIPS, g.shape[1] // N_CHIPS), (1, 0, 2))
    halved = lambda g: g.reshape(N_CHIPS, 2, g.shape[1] // 2, g.shape[2])
    empty = lambda a, shape=None: lax.empty(a.shape if shape is None else shape, a.dtype)

    sem_in, buf_in, going = _copy_start(_ag_descs, [_cast_place_t(me_arr, mats[0], "cast_w_in")], None,
                                        "weight_allgather_start_in", 3)
    placed = [_cast_place(me_arr, w, "cast_" + n, going) for w, n in zip(mats[1:], names[1:])]
    sem_att, buf_att, going = _copy_start(_ag_descs, placed[:3], going, "weight_allgather_start_attn", 9)
    sem_mlp, buf_mlp, started = _copy_start(_ag_descs, placed[3:], going, "weight_allgather_start_mlp", 6)

    going_on = {}

    def in_weights(after):
        (win_g,) = _ag_forward(_copy_wait(_ag_descs, sem_in, buf_in, after, "weight_allgather_wait_in"), "in")
        return cast(win_g)

    def attn_prefetch(after):
        landed = _copy_wait(_ag_descs, sem_att, buf_att, after, "weight_allgather_wait_attn")
        going_on["fw_attn"] = _copy_start(_fw_descs, landed, None, "weight_allgather_forward_start_attn", 9)
        return going_on["fw_attn"][-1][0:1, 0:1]

    def attn_weights(after):
        sems, bufs, _ = going_on["fw_attn"]
        wuq_g, wukv_g, wout_g = _copy_wait(_fw_descs, sems, bufs, after, "weight_allgather_forward_wait_attn")
        wuq_full = col_major(wuq_g).reshape(LORA, NH, HEAD + ROPE_B)
        w_uq_p = jnp.pad(wuq_full, ((0, 0), (0, 0), (0, QPAD - HEAD - ROPE_B))).reshape(LORA, NH * QPAD)
        return cast(w_uq_p), cast(col_major(wukv_g)), cast(wout_g.reshape(2 * A_W, D_MODEL))

    def mlp_prefetch(after):
        landed = _copy_wait(_ag_descs, sem_mlp, buf_mlp, after, "weight_allgather_wait_mlp")
        going_on["fw"] = _copy_start(_fw_descs, landed, None, "weight_allgather_forward_start_mlp", 6)
        return going_on["fw"][-1]

    def mlp_weights(after):
        sems, bufs, _ = going_on["fw"]
        wup_g, wdown_g = _copy_wait(_fw_descs, sems, bufs, after, "weight_allgather_forward_wait_mlp")
        return cast(wup_g), cast(wdown_g.reshape(D_FF, D_MODEL))

    def exchange_start(g4s, tag):
        lands = [empty(g, (g.shape[0],) + g.shape[2:]) for g in g4s]
        return _copy_start(_EX_DESCS, g4s + lands, None, "grad_pair_exchange_start_" + tag, len(g4s))

    def exchange_finish(started_ex, after, ns, tag):
        sems, arrs, _ = started_ex
        arrs = _copy_wait(_EX_DESCS, sems, arrs, after, "grad_pair_exchange_wait_" + tag)
        n = len(ns)
        return [_pair_add(c_arr, g4, r, "pair_add_" + nm) for g4, r, nm in zip(arrs[:n], arrs[n:], ns)]

    def scatter_start(parts, after, tag):
        return _copy_start(_sc_descs, parts + [empty(p) for p in parts], after, "grad_scatter_start_" + tag,
                           3 * len(parts))

    def scatter_finish(started_sc, after, tag):
        sems, arrs, _ = started_sc
        arrs = _copy_wait(_sc_descs, sems, arrs, after, "grad_scatter_wait_" + tag)
        return arrs[:len(arrs) // 2], arrs[len(arrs) // 2:]

    def down_grad_ready(gw_down):
        going_on["x_down"] = exchange_start([halved(gw_down.reshape(N_CHIPS, D_MODEL, D_MODEL))], "down")
        return going_on["x_down"][-1]

    def up_grad_ready(gw_up):
        going_on["x_up"] = exchange_start([halved(gw_up)], "up")
        parts = exchange_finish(going_on["x_down"], going_on["x_up"][-1], names[5:], "down")
        going_on["s_down"] = scatter_start(parts, started, "down")
        return going_on["s_down"][-1][0:1, 0:1]

    def attn_grads_ready(gw_out, gw_uq_p, gw_ukv):
        gw_uq = to_shards(gw_uq_p.reshape(LORA, NH, QPAD)[:, :, :HEAD + ROPE_B].reshape(LORA, NH * (HEAD + ROPE_B)))
        full4 = [halved(g) for g in (gw_uq, to_shards(gw_ukv), gw_out.reshape(N_CHIPS, LORA, D_MODEL))]
        x_attn = exchange_start(full4, "attn")
        parts_up = exchange_finish(going_on["x_up"], x_attn[-1], names[4:5], "up")
        parts = exchange_finish(x_attn, parts_up[0], names[1:4], "attn") + parts_up
        going_on["s_rest"] = scatter_start(parts, going_on["s_down"][-1], "attn_up")
        return going_on["s_rest"][-1][0:1, 0:1]

    dx, gw_proj, small = _local_step(
        x[0], positions[0].astype(F32).reshape(T, 1), loss_target[0],
        norm_attn_pre + started[0:1, 0:1], norm_attn_post, q_latent_norm, kv_latent_norm, norm_mlp_pre, norm_mlp_post,
        in_weights, attn_prefetch, attn_weights, mlp_prefetch, mlp_weights,
        down_grad_ready, up_grad_ready, attn_grads_ready)

    ms = [transposed(m_w_in)[0], m_w_uq[0], m_w_ukv[0], m_w_out[0], m_w_up[0], m_w_down[0]]
    vs = [transposed(v_w_in)[0], v_w_uq[0], v_w_ukv[0], v_w_out[0], v_w_up[0], v_w_down[0]]
    sib_arr = 1 - c_arr

    def finish(parts, landed, lo, hi, tag):
        sl = slice(lo, hi)
        halves = [_sum4(me_arr, p, l, "chip_sum_" + n) for p, l, n in zip(parts, landed, names[sl])]
        n = len(halves)
        sems, arrs, _ = _copy_start(_SW_DESCS, halves + [empty(h) for h in halves], None,
                                    "grad_pair_swap_start_" + tag, n)
        own = [_adamw_half(c_arr, w, g, m, v, [], "adamw_own_" + nm)
               for w, g, m, v, nm in zip(mats[sl], arrs[:n], ms[sl], vs[sl], names[sl])]
        arrs = _copy_wait(_SW_DESCS, sems, arrs, own[-1][4], "grad_pair_swap_wait_" + tag)
        return [_adamw_half(sib_arr, w, g, m, v, prev[:4], "adamw_sib_" + nm)
                for w, g, m, v, prev, nm in zip(mats[sl], arrs[n:], ms[sl], vs[sl], own, names[sl])]

    sem_small, (gath,), small_going = _copy_start(
        _sm_descs, [_place_rows(me8_arr, small, N_DEV, "place_small")], None, "small_allgather_start", N_DEV - 1)
    sems, arrs, _ = _copy_start(_EXT_DESCS, [gw_proj, lax.empty((IN_COLS, D_MODEL // 2), WIRE_DTYPE)], None,
                                "grad_pair_exchange_start_in", 1)
    gw_proj, from_sib = _copy_wait(_EXT_DESCS, sems, arrs, small_going, "grad_pair_exchange_wait_in")
    part_in = _pair_add_t(c_arr, gw_proj, from_sib, "pair_add_w_in").reshape(N_CHIPS, IN_SHARD, D_MODEL // 2)
    s_in = scatter_start([part_in], None, "in")
    parts_rest, landed_rest = scatter_finish(going_on["s_rest"], s_in[-1], "attn_up")
    parts_down, landed_down = scatter_finish(going_on["s_down"], landed_rest[0], "down")
    upd_rest = finish(parts_rest + parts_down, landed_rest + landed_down, 1, 6, "rest")
    parts_in, landed_in = scatter_finish(s_in, upd_rest[-1][0], "in")
    upd = finish(parts_in, landed_in, 0, 1, "in") + upd_rest
    grads = [u[0] for u in upd]

    (gath,) = _copy_wait(_sm_descs, sem_small, [gath], grads[0], "small_allgather_wait")
    gains = [norm_attn_pre, norm_attn_post, q_latent_norm, kv_latent_norm, norm_mlp_pre, norm_mlp_post]
    gm = [m_norm_attn_pre, m_norm_attn_post, m_q_latent_norm, m_kv_latent_norm, m_norm_mlp_pre, m_norm_mlp_post]
    gv = [v_norm_attn_pre, v_norm_attn_post, v_q_latent_norm, v_kv_latent_norm, v_norm_mlp_pre, v_norm_mlp_post]
    cat = lambda xs: jnp.concatenate(xs, axis=1)
    g_s, d_s, m_s, v_s, loss_v = _small_update(gath, cat(gains), cat(gm), cat(gv))
    widths = [a.shape[1] for a in gains]
    offs = [sum(widths[:i]) for i in range(len(widths))]
    split = lambda a: [a[:, o:o + w] for o, w in zip(offs, widths)]
    g_gain, d_gain, m_gain, v_gain = split(g_s), split(d_s), split(m_s), split(v_s)

    def ordered(gain_list, mat_list):
        gl, ml = gain_list, [transposed(mat_list[0][None])] + [a[None] for a in mat_list[1:]]
        return [gl[0], gl[1], ml[0], gl[2], gl[3], ml[1], ml[2], ml[3], gl[4], gl[5], ml[4], ml[5]]

    loss = loss_v[0, 0]
    return (loss, dx[None],
            *ordered(g_gain, grads),
            *ordered(d_gain, [u[1] for u in upd]),
            *ordered(m_gain, [u[2] for u in upd]),
            *ordered(v_gain, [u[3] for u in upd]))
```

```python
import functools

import jax
import jax.numpy as jnp
from jax import lax
from jax.experimental import pallas as pl
from jax.experimental.pallas import tpu as pltpu

F32 = jnp.float32
BF16 = jnp.bfloat16
MXU_DTYPE = jnp.bfloat16
WIRE_DTYPE = jnp.bfloat16

D_MODEL = 2048
HEAD = 128
NH = 8
A_W = NH * HEAD
LORA = 512
ROPE_B = 64
QPAD = 256
MAIN_COLS = 3 * A_W + 2 * LORA
IN_COLS = MAIN_COLS + ROPE_B
PROJ_COLS = MAIN_COLS + HEAD
PROJ_TILE = PROJ_COLS // 3
IN_SHARD = 1040
IN_TR = 208
D_FF = 4 * D_MODEL
DIL = (1, 4, 16)
ROT_A = 32
ROPE_THETA = 500000.0
EPS = 1e-6
NEG = -1e30
N_CHIPS = 4
N_DEV = 8

ADAM_LR = 0.001
ADAM_B1 = 0.9
ADAM_B2 = 0.999
ADAM_EPS = 1e-08
ADAM_WD = 0.01
ADAM_STEP = 10

MESH = pl.DeviceIdType.MESH
ANY = pl.BlockSpec(memory_space=pl.ANY)


def _pcall(body, **kw):
    return pl.pallas_call(body, **kw)


_DIMS = {
    "nn": (((1,), (0,)), ((), ())),
    "nt": (((1,), (1,)), ((), ())),
    "tn": (((0,), (0,)), ((), ())),
}


def _mm_body(*refs, dims, nk, epi, n_extra, n_after, n_out):
    a_ref, b_ref = refs[0], refs[1]
    extra = refs[2:2 + n_extra]
    outs = refs[2 + n_extra + n_after:2 + n_extra + n_after + n_out]
    part = lax.dot_general(a_ref[...], b_ref[...], _DIMS[dims], preferred_element_type=F32)

    def finish(acc):
        res = epi(acc, *[r[...] for r in extra]) if epi is not None else (acc,)
        for o_ref, o in zip(outs, res):
            o_ref[...] = o.astype(o_ref.dtype)

    if nk == 1:
        finish(part)
        return
    acc_ref = refs[-1]
    k = pl.program_id(2)

    @pl.when(k == 0)
    def _():
        acc_ref[...] = part

    @pl.when(k > 0)
    def _():
        acc_ref[...] += part

    @pl.when(k == nk - 1)
    def _():
        finish(acc_ref[...])


def _matmul(a, b, *, dims, out_dtypes, tm, tn, tk, name, epi=None, extras=(), row_extras=(), b_outer=False,
            b_shards=0, out_shards=0, after=None):
    if b_shards:
        assert dims in ("nn", "nt") and b.shape[0] == b_shards
        b2 = (b.shape[1], b_shards * b.shape[2])
    else:
        b2 = b.shape
    if dims == "nn":
        (M, K), (K2, N) = a.shape, b2
    elif dims == "nt":
        (M, K), (N, K2) = a.shape, b2
    else:
        (K, M), (K2, N) = a.shape, b2
    assert K == K2, (a.shape, b.shape, dims)
    tm, tn, tk = min(tm, M), min(tn, N), min(tk, K)
    assert M % tm == 0 and N % tn == 0 and K % tk == 0, (name, M, N, K, tm, tn, tk)
    nk = K // tk

    def at(f):
        if b_outer:
            return lambda j, i, k: f(i, j, k)
        return f

    a_spec = {"nn": pl.BlockSpec((tm, tk), at(lambda i, j, k: (i, k))),
              "nt": pl.BlockSpec((tm, tk), at(lambda i, j, k: (i, k))),
              "tn": pl.BlockSpec((tk, tm), at(lambda i, j, k: (k, i)))}[dims]
    b_spec = {"nn": pl.BlockSpec((tk, tn), at(lambda i, j, k: (k, j))),
              "nt": pl.BlockSpec((tn, tk), at(lambda i, j, k: (j, k))),
              "tn": pl.BlockSpec((tk, tn), at(lambda i, j, k: (k, j)))}[dims]
    if b_shards:
        per = b.shape[2] // (tn if dims == "nn" else tk)
        assert per >= 1 and b.shape[2] % (tn if dims == "nn" else tk) == 0
        b_spec = {"nn": pl.BlockSpec((None, tk, tn), at(lambda i, j, k: (j // per, k, j % per))),
                  "nt": pl.BlockSpec((None, tn, tk), at(lambda i, j, k: (k // per, j, k % per)))}[dims]
    o_spec = pl.BlockSpec((tm, tn), at(lambda i, j, k: (i, j)))
    o_shape = (M, N)
    if out_shards:
        assert not extras and N % out_shards == 0 and (N // out_shards) % tn == 0
        o_per = (N // out_shards) // tn
        o_spec = pl.BlockSpec((None, tm, tn), at(lambda i, j, k: (j // o_per, i, j % o_per)))
        o_shape = (out_shards, M, N // out_shards)
    r_specs = [pl.BlockSpec((tm, r.shape[1]), at(lambda i, j, k: (i, 0))) for r in row_extras]
    after = [] if after is None else [after]
    body = functools.partial(_mm_body, dims=dims, nk=nk, epi=epi, n_extra=len(extras) + len(row_extras),
                             n_after=len(after), n_out=len(out_dtypes))
    res = _pcall(
        body, name=name,
        grid=(N // tn, M // tm, nk) if b_outer else (M // tm, N // tn, nk),
        in_specs=[a_spec, b_spec] + [o_spec] * len(extras) + r_specs + [ANY] * len(after),
        out_specs=[o_spec] * len(out_dtypes),
        out_shape=[jax.ShapeDtypeStruct(o_shape, dt) for dt in out_dtypes],
        scratch_shapes=[pltpu.VMEM((tm, tn), F32)] if nk > 1 else [],
        compiler_params=pltpu.CompilerParams(
            dimension_semantics=("parallel", "parallel", "arbitrary")),
    )(a, b, *extras, *row_extras, *after)
    return list(res)


def _rowwise(body, row_ins, vec_ins, row_outs, acc_outs, *, tr, name):
    T = row_ins[0].shape[0]
    tr = min(tr, T)
    assert T % tr == 0
    in_specs = [pl.BlockSpec((tr, a.shape[1]), lambda i: (i, 0)) for a in row_ins]
    in_specs += [pl.BlockSpec(a.shape, lambda i: (0, 0)) for a in vec_ins]
    out_specs = [pl.BlockSpec((tr, w), lambda i: (i, 0)) for (w, _) in row_outs]
    out_specs += [pl.BlockSpec(s, lambda i: (0, 0)) for s in acc_outs]
    out_shape = [jax.ShapeDtypeStruct((T, w), dt) for (w, dt) in row_outs]
    out_shape += [jax.ShapeDtypeStruct(s, F32) for s in acc_outs]
    sem = "arbitrary" if acc_outs else "parallel"
    return list(_pcall(
        body, name=name, grid=(T // tr,), in_specs=in_specs, out_specs=out_specs,
        out_shape=out_shape,
        compiler_params=pltpu.CompilerParams(dimension_semantics=(sem,)),
    )(*row_ins, *vec_ins))


def _rstd(x):
    return lax.rsqrt(jnp.mean(x * x, axis=-1, keepdims=True) + EPS)


def _rms_bwd(x, rstd, dyg):
    xh = x * rstd
    return rstd * (dyg - xh * jnp.mean(dyg * xh, axis=-1, keepdims=True)), xh


def _fold8(v):
    r, w = v.shape
    return jnp.sum(v.reshape(r // 8, 8, w), axis=0)


def _acc(ref, val):
    first = pl.program_id(0) == 0

    @pl.when(first)
    def _():
        ref[...] = val

    @pl.when(jnp.logical_not(first))
    def _():
        ref[...] += val


def _rope(x, c, sa, sb, half):
    return x * c + pltpu.roll(x, HEAD - half, 1) * sa + pltpu.roll(x, half, 1) * sb


def _rope_t(dy, c, sa, sb, half):
    return dy * c - pltpu.roll(dy, HEAD - half, 1) * sa - pltpu.roll(dy, half, 1) * sb


def _rope_tab_body(pos_ref, inv_ref, ca, saa, sab, cb, sba, sbb):
    pos = pos_ref[...]
    lane = lax.broadcasted_iota(jnp.int32, (pos.shape[0], HEAD), 1)
    ang_a = pos * inv_ref[0:1, :]
    ang_b = pos * inv_ref[1:2, :]
    c, s = jnp.cos(ang_a), jnp.sin(ang_a)
    ha = ROT_A // 2
    ca[...] = jnp.where(lane < ROT_A, c, 1.0)
    saa[...] = jnp.where(lane < ha, -s, 0.0)
    sab[...] = jnp.where((lane >= ha) & (lane < ROT_A), s, 0.0)
    c, s = jnp.cos(ang_b), jnp.sin(ang_b)
    hb = ROPE_B // 2
    cb[...] = jnp.where(lane < ROPE_B, c, 1.0)
    sba[...] = jnp.where(lane < hb, -s, 0.0)
    sbb[...] = jnp.where((lane >= hb) & (lane < ROPE_B), s, 0.0)


def _rms_fwd_body(x_ref, g_ref, h_ref):
    x = x_ref[...]
    h_ref[...] = ((x * _rstd(x)) * g_ref[...]).astype(h_ref.dtype)


def _postproj_body(p_ref, ca, saa, sab, cb, sba, sbb, gq_ref, gkv_ref,
                   q_ref, k_ref, v_ref, cqn_ref, ckvn_ref, krope_ref):
    c, sa, sb = ca[...], saa[...], sab[...]
    for h in range(NH):
        lo = h * HEAD
        q_ref[:, lo:lo + HEAD] = _rope(p_ref[:, lo:lo + HEAD], c, sa, sb, ROT_A // 2).astype(q_ref.dtype)
        k_ref[:, lo:lo + HEAD] = _rope(p_ref[:, A_W + lo:A_W + lo + HEAD], c, sa, sb, ROT_A // 2).astype(k_ref.dtype)
    v_ref[...] = p_ref[:, 2 * A_W:3 * A_W].astype(v_ref.dtype)
    cq = p_ref[:, 3 * A_W:3 * A_W + LORA]
    cqn_ref[...] = ((cq * _rstd(cq)) * gq_ref[...]).astype(cqn_ref.dtype)
    ckv = p_ref[:, 3 * A_W + LORA:MAIN_COLS]
    ckvn_ref[...] = ((ckv * _rstd(ckv)) * gkv_ref[...]).astype(ckvn_ref.dtype)
    krope_ref[...] = _rope(p_ref[:, MAIN_COLS:PROJ_COLS], cb[...], sba[...], sbb[...], ROPE_B // 2).astype(krope_ref.dtype)


def _mid_body(x_ref, o_ref, g2_ref, g3_ref, x1_ref, h2_ref):
    o = o_ref[...]
    x1 = x_ref[...] + (o * _rstd(o)) * g2_ref[...]
    x1_ref[...] = x1
    h2_ref[...] = ((x1 * _rstd(x1)) * g3_ref[...]).astype(h2_ref.dtype)


def _loss_body(x1_ref, d_ref, t_ref, g4_ref, dy_ref, dd_ref, loss_ref, dg4_ref):
    d = d_ref[...]
    rstd = _rstd(d)
    y = x1_ref[...] + (d * rstd) * g4_ref[...]
    e = y - t_ref[...]
    dy = e * (1.0 / D_MODEL)
    dy_ref[...] = dy
    dd, dh = _rms_bwd(d, rstd, dy * g4_ref[...])
    dd_ref[...] = dd.astype(dd_ref.dtype)
    _acc(dg4_ref, _fold8(dy * dh))
    e8 = _fold8(e * e)
    l = e8[:, 0:HEAD]
    for j in range(1, D_MODEL // HEAD):
        l = l + e8[:, j * HEAD:(j + 1) * HEAD]
    _acc(loss_ref, l)


def _bmid_body(dy_ref, dh2_ref, x1_ref, o_ref, g2_ref, g3_ref, dx1_ref, do_ref, dg3_ref, dg2_ref):
    x1 = x1_ref[...]
    dh2 = dh2_ref[...]
    dn, x1h = _rms_bwd(x1, _rstd(x1), dh2 * g3_ref[...])
    dx1 = dy_ref[...] + dn
    dx1_ref[...] = dx1
    _acc(dg3_ref, _fold8(dh2 * x1h))
    o = o_ref[...]
    do, oh = _rms_bwd(o, _rstd(o), dx1 * g2_ref[...])
    do_ref[...] = do.astype(do_ref.dtype)
    _acc(dg2_ref, _fold8(dx1 * oh))


def _dproj_body(dq_ref, dk_ref, dv_ref, dcq_ref, dckv_ref, p_ref, dkr_ref,
                ca, saa, sab, cb, sba, sbb, gq_ref, gkv_ref,
                dp_ref, dgq_ref, dgkv_ref):
    c, sa, sb = ca[...], saa[...], sab[...]
    for h in range(NH):
        lo = h * HEAD
        dp_ref[:, lo:lo + HEAD] = _rope_t(dq_ref[:, lo:lo + HEAD], c, sa, sb, ROT_A // 2).astype(dp_ref.dtype)
        dp_ref[:, A_W + lo:A_W + lo + HEAD] = _rope_t(dk_ref[:, lo:lo + HEAD], c, sa, sb, ROT_A // 2).astype(dp_ref.dtype)
    dp_ref[:, 2 * A_W:3 * A_W] = dv_ref[...].astype(dp_ref.dtype)
    cq = p_ref[:, 3 * A_W:3 * A_W + LORA]
    dcqn = dcq_ref[...]
    dcq, cqh = _rms_bwd(cq, _rstd(cq), dcqn * gq_ref[...])
    dp_ref[:, 3 * A_W:3 * A_W + LORA] = dcq.astype(dp_ref.dtype)
    _acc(dgq_ref, _fold8(dcqn * cqh))
    ckv = p_ref[:, 3 * A_W + LORA:MAIN_COLS]
    dckvn = dckv_ref[...]
    dckv, ckvh = _rms_bwd(ckv, _rstd(ckv), dckvn * gkv_ref[...])
    dp_ref[:, 3 * A_W + LORA:MAIN_COLS] = dckv.astype(dp_ref.dtype)
    _acc(dgkv_ref, _fold8(dckvn * ckvh))
    dkr = dkr_ref[:, 0:HEAD]
    for h in range(1, NH):
        dkr = dkr + dkr_ref[:, h * HEAD:(h + 1) * HEAD]
    dp_ref[:, MAIN_COLS:PROJ_COLS] = _rope_t(dkr, cb[...], sba[...], sbb[...], ROPE_B // 2).astype(dp_ref.dtype)


def _bin_body(dx1_ref, dh_ref, x_ref, g1_ref, dx_ref, dg1_ref):
    x = x_ref[...]
    dh = dh_ref[...]
    dn, xh = _rms_bwd(x, _rstd(x), dh * g1_ref[...])
    dx_ref[...] = dx1_ref[...] + dn
    _acc(dg1_ref, _fold8(dh * xh))


def _dot_nt(a, b):
    return lax.dot_general(a, b, _DIMS["nt"], preferred_element_type=F32)


def _dot_tn(a, b):
    return lax.dot_general(a, b, _DIMS["tn"], preferred_element_type=F32)


def _dot_nn(a, b):
    return jnp.dot(a, b, preferred_element_type=F32)


DIL_SCALE = HEAD ** -0.5
DIL_CHUNK = 256


def _dil_rows(t, d, chain=0):
    if chain:
        r = t >> (chain.bit_length() - 1)
        n = t & (chain - 1)
    else:
        r = t & (d - 1)
        n = t >> (d.bit_length() - 1)
    start = r + n * (HEAD * d)
    has_prev = n > 0
    pstart = jnp.where(has_prev, start - HEAD * d, start)
    if d == 1:
        return pl.ds(pl.multiple_of(start, HEAD), HEAD), pl.ds(pl.multiple_of(pstart, HEAD), HEAD), has_prev
    return pl.ds(start, HEAD, stride=d), pl.ds(pstart, HEAD, stride=d), has_prev


def _dil_band():
    row = lax.broadcasted_iota(jnp.int32, (HEAD, 2 * HEAD), 0)
    col = lax.broadcasted_iota(jnp.int32, (HEAD, 2 * HEAD), 1)
    return (col >= row) & (col <= row + HEAD), col >= HEAD


def _dil_fwd_body(q_ref, k_ref, v_ref, a_ref, lse_ref, o1, o2, o3, l1, l2, l3, *, nt, unroll):
    band, is_cur = _dil_band()
    for d, o_sc, l_sc in zip(DIL, (o1, o2, o3), (l1, l2, l3)):
        chain = nt // d
        linked = min(chain, unroll)
        assert unroll % linked == 0

        def tiles(g, carry, d=d, o_sc=o_sc, l_sc=l_sc, chain=chain, linked=linked):
            staged = []
            for u in range(unroll):
                rows, prows, has_prev = _dil_rows(g * unroll + u, d, chain)
                q = q_ref[rows, :].astype(MXU_DTYPE)
                kc = k_ref[rows, :].astype(MXU_DTYPE)
                vc = v_ref[rows, :].astype(MXU_DTYPE)
                if u % linked:
                    kp, vp = staged[-1][2], staged[-1][3]
                else:
                    kp = k_ref[prows, :].astype(MXU_DTYPE)
                    vp = v_ref[prows, :].astype(MXU_DTYPE)
                kk = jnp.concatenate([kp, kc], axis=0)
                staged.append((rows, has_prev, kc, vc, jnp.concatenate([vp, vc], axis=0), _dot_nt(q, kk)))
            for rows, has_prev, _, _, vv, s in staged:
                ok = band & (is_cur | has_prev)
                s = jnp.where(ok, s * DIL_SCALE, NEG)
                m = jnp.max(s, axis=1, keepdims=True)
                p = jnp.exp(s - m)
                den = jnp.sum(p, axis=1, keepdims=True)
                o_sc[rows, :] = _dot_nn((p / den).astype(MXU_DTYPE), vv)
                l_sc[rows, :] = jnp.broadcast_to(m + jnp.log(den), (HEAD, HEAD))
            return carry

        lax.fori_loop(0, nt // unroll, tiles, 0)

    def merge(i, carry):
        rs = pl.ds(pl.multiple_of(i * DIL_CHUNK, DIL_CHUNK), DIL_CHUNK)
        la, lb, lc = l1[rs, :], l2[rs, :], l3[rs, :]
        m = jnp.maximum(jnp.maximum(la, lb), lc)
        wa, wb, wc = jnp.exp(la - m), jnp.exp(lb - m), jnp.exp(lc - m)
        den = wa + wb + wc
        a = (wa / den) * o1[rs, :] + (wb / den) * o2[rs, :] + (wc / den) * o3[rs, :]
        a_ref[rs, :] = a.astype(a_ref.dtype)
        lse_ref[rs, :] = m + jnp.log(den)
        return carry

    lax.fori_loop(0, q_ref.shape[0] // DIL_CHUNK, merge, 0)


def _dil_fwd(q, k, v):
    T = q.shape[0]
    spec = pl.BlockSpec((T, HEAD), lambda h: (0, h))
    return _pcall(
        functools.partial(_dil_fwd_body, nt=T // HEAD, unroll=16), name="dil_fwd",
        grid=(NH,), in_specs=[spec] * 3, out_specs=[spec] * 2,
        out_shape=[jax.ShapeDtypeStruct((T, 2 * A_W), MXU_DTYPE), jax.ShapeDtypeStruct((T, A_W), F32)],
        scratch_shapes=[pltpu.VMEM((T, HEAD), F32)] * 6,
        compiler_params=pltpu.CompilerParams(dimension_semantics=("parallel",)),
    )(q, k, v)


def _dil_bwd_body(q_ref, k_ref, v_ref, do_ref, a_ref, lse_ref, dq_ref, dk_ref, dv_ref, dl_sc, *, nt, unroll):
    band, is_cur = _dil_band()

    def prep(i, carry):
        rs = pl.ds(pl.multiple_of(i * DIL_CHUNK, DIL_CHUNK), DIL_CHUNK)
        dl = jnp.sum(do_ref[rs, :] * a_ref[rs, :].astype(F32), axis=1, keepdims=True)
        dl_sc[rs, :] = jnp.broadcast_to(dl, (DIL_CHUNK, HEAD))
        zero = jnp.zeros((DIL_CHUNK, HEAD), F32)
        dq_ref[rs, :] = zero
        dk_ref[rs, :] = zero
        dv_ref[rs, :] = zero
        return carry

    lax.fori_loop(0, q_ref.shape[0] // DIL_CHUNK, prep, 0)

    for d in DIL:
        chain = nt // d
        linked = min(chain, unroll)
        assert unroll % linked == 0

        def tiles(g, carry, d=d, chain=chain, linked=linked):
            staged = []
            for u in range(unroll):
                rows, prows, has_prev = _dil_rows(g * unroll + u, d, chain)
                q = q_ref[rows, :].astype(MXU_DTYPE)
                do = do_ref[rows, :].astype(MXU_DTYPE)
                kc = k_ref[rows, :].astype(MXU_DTYPE)
                vc = v_ref[rows, :].astype(MXU_DTYPE)
                if u % linked:
                    kp, vp = staged[-1][5], staged[-1][6]
                else:
                    kp = k_ref[prows, :].astype(MXU_DTYPE)
                    vp = v_ref[prows, :].astype(MXU_DTYPE)
                kk = jnp.concatenate([kp, kc], axis=0)
                vv = jnp.concatenate([vp, vc], axis=0)
                staged.append((rows, prows, has_prev, q, do, kc, vc, kk, _dot_nt(q, kk), _dot_nt(do, vv)))

            def add_own(own):
                rows, dk_own, dv_own = own
                dk_ref[rows, :] += dk_own
                dv_ref[rows, :] += dv_own

            own = None
            for u, (rows, prows, has_prev, q, do, _, _, kk, s, dp) in enumerate(staged):
                lse = lse_ref[rows, :]
                dl = dl_sc[rows, :]
                ok = band & (is_cur | has_prev)
                p = jnp.where(ok, jnp.exp(s * DIL_SCALE - jnp.concatenate([lse, lse], axis=1)), 0.0)
                ds = (p * (dp - jnp.concatenate([dl, dl], axis=1))).astype(MXU_DTYPE)
                dq_ref[rows, :] += _dot_nn(ds, kk) * DIL_SCALE
                dkk = _dot_tn(ds, q) * DIL_SCALE
                dvv = _dot_tn(p.astype(MXU_DTYPE), do)
                if u % linked:
                    add_own((own[0], own[1] + dkk[:HEAD, :], own[2] + dvv[:HEAD, :]))
                else:
                    if own is not None:
                        add_own(own)
                    dk_ref[prows, :] += dkk[:HEAD, :]
                    dv_ref[prows, :] += dvv[:HEAD, :]
                own = (rows, dkk[HEAD:, :], dvv[HEAD:, :])
            add_own(own)
            return carry

        lax.fori_loop(0, nt // unroll, tiles, 0)


def _dil_bwd(q, k, v, dmix, mixed, lse):
    T = q.shape[0]
    spec = pl.BlockSpec((T, HEAD), lambda h: (0, h))
    return _pcall(
        functools.partial(_dil_bwd_body, nt=T // HEAD, unroll=8), name="dil_bwd",
        grid=(NH,), in_specs=[spec] * 6, out_specs=[spec] * 3,
        out_shape=[jax.ShapeDtypeStruct((T, A_W), F32)] * 3,
        scratch_shapes=[pltpu.VMEM((T, HEAD), F32)],
        compiler_params=pltpu.CompilerParams(dimension_semantics=("parallel",)),
    )(q, k, v, dmix, mixed, lse)


MLA_SCALE = (HEAD + ROPE_B) ** -0.5
LOG2E = 1.4426950408889634
MLA_QSCALE = MLA_SCALE * LOG2E
MLA_T = 512
MLA_HP = 4


def _tri(t):
    row = lax.broadcasted_iota(jnp.int32, (t, t), 0)
    col = lax.broadcasted_iota(jnp.int32, (t, t), 1)
    return col <= row


def _lanes(x, n):
    return jnp.tile(x, (1, n // HEAD))


def _mla_fwd_body(q_ref, kv_ref, kr_ref, mixed_ref, o_ref, lse_ref, m_sc, l_sc, acc_sc, *, t, hp):
    del mixed_ref
    qi = pl.program_id(1)
    m_sc[...] = jnp.full(m_sc.shape, NEG, F32)
    l_sc[...] = jnp.zeros(l_sc.shape, F32)
    acc_sc[...] = jnp.zeros(acc_sc.shape, F32)

    def step(j, masked):
        ks = pl.ds(pl.multiple_of(j * t, t), t)
        kr = kr_ref[ks, :]
        logits = []
        for hh in range(hp):
            kcat = jnp.concatenate([kv_ref[ks, 2 * hh * HEAD:(2 * hh + 1) * HEAD], kr], axis=1)
            logits.append(_dot_nt(q_ref[:, hh * QPAD:(hh + 1) * QPAD], kcat))
        for hh in range(hp):
            s = logits[hh]
            if masked:
                s = jnp.where(_tri(t), s, NEG)
            m_prev = m_sc[hh]
            m_new = jnp.maximum(m_prev, jnp.max(s, axis=1, keepdims=True))
            alpha = jnp.exp2(m_prev - m_new)
            p = jnp.exp2(s - _lanes(m_new, t))
            l_sc[hh] = alpha * l_sc[hh] + jnp.sum(p, axis=1, keepdims=True)
            acc_sc[hh] = alpha * acc_sc[hh] + _dot_nn(p.astype(MXU_DTYPE), kv_ref[ks, (2 * hh + 1) * HEAD:(2 * hh + 2) * HEAD])
            m_sc[hh] = m_new

    def off_diag(j, carry):
        step(j, False)
        return carry

    lax.fori_loop(0, qi, off_diag, 0)
    step(qi, True)
    for hh in range(hp):
        l = l_sc[hh]
        o_ref[:, hh * HEAD:(hh + 1) * HEAD] = (acc_sc[hh] / l).astype(o_ref.dtype)
        lse_ref[:, hh * HEAD:(hh + 1) * HEAD] = m_sc[hh] + jnp.log2(l)


def _mla_fwd(qf, kv, kr, mixed):
    T = qf.shape[0]
    t, hp = min(MLA_T, T), MLA_HP
    ng = NH // hp
    return _pcall(
        functools.partial(_mla_fwd_body, t=t, hp=hp), name="mla_fwd",
        grid=(ng, T // t),
        in_specs=[pl.BlockSpec((t, hp * QPAD), lambda g, i: (i, g)),
                  pl.BlockSpec((T, hp * 2 * HEAD), lambda g, i: (0, g)),
                  pl.BlockSpec((T, HEAD), lambda g, i: (0, 0)), ANY],
        out_specs=[pl.BlockSpec((t, hp * HEAD), lambda g, i: (i, ng + g)),
                   pl.BlockSpec((t, hp * HEAD), lambda g, i: (i, g))],
        out_shape=[jax.ShapeDtypeStruct(mixed.shape, mixed.dtype), jax.ShapeDtypeStruct((T, A_W), F32)],
        input_output_aliases={3: 0},
        scratch_shapes=[pltpu.VMEM((hp, t, HEAD), F32)] * 3,
        compiler_params=pltpu.CompilerParams(dimension_semantics=("parallel", "parallel")),
    )(qf, kv, kr, mixed)


def _mla_bwd_body(q_ref, kn_ref, kr_ref, v_ref, do_ref, o_ref, lse_ref, cb, sba, sbb,
                  dq_ref, dkv_ref, dkr_ref, dq_sc, dl_sc, dk_sc, dv_sc, *, t):
    ki = pl.program_id(1)
    nq = q_ref.shape[0] // t

    @pl.when(ki == 0)
    def _():
        def prep(i, carry):
            rs = pl.ds(pl.multiple_of(i * t, t), t)
            dl = jnp.sum(do_ref[rs, :] * o_ref[rs, :].astype(F32), axis=1, keepdims=True)
            dl_sc[rs, :] = jnp.broadcast_to(dl, (t, HEAD))
            dq_sc[rs, :] = jnp.zeros((t, QPAD), F32)
            return carry
        lax.fori_loop(0, nq, prep, 0)

    kcat = jnp.concatenate([kn_ref[...], kr_ref[...]], axis=1)
    v = v_ref[...]
    dk_sc[...] = jnp.zeros(dk_sc.shape, F32)
    dv_sc[...] = jnp.zeros(dv_sc.shape, F32)

    def steps(blocks):
        staged = []
        for i, masked in blocks:
            qs = pl.ds(pl.multiple_of(i * t, t), t)
            q = q_ref[qs, :]
            do = do_ref[qs, :].astype(MXU_DTYPE)
            staged.append((qs, q, do, _dot_nt(q, kcat), _dot_nt(do, v), masked))
        for qs, q, do, s, dp, masked in staged:
            p = jnp.exp2(s - _lanes(lse_ref[qs, :], t))
            if masked:
                p = jnp.where(_tri(t), p, 0.0)
            ds = (p * (dp - _lanes(dl_sc[qs, :], t))).astype(MXU_DTYPE)
            dv_sc[...] += _dot_tn(p.astype(MXU_DTYPE), do)
            dk_sc[...] += _dot_tn(ds, q)
            dq_sc[qs, :] += _dot_nn(ds, kcat) * MLA_SCALE

    n_blocks = nq - ki

    @pl.when(n_blocks == 1)
    def _():
        steps([(ki, True)])

    @pl.when(n_blocks > 1)
    def _():
        steps([(ki, True), (ki + 1, False)])

    def pair(j, carry):
        steps([(ki + 2 * j, False), (ki + 2 * j + 1, False)])
        return carry

    lax.fori_loop(1, n_blocks // 2, pair, 0)

    @pl.when((n_blocks > 1) & (n_blocks % 2 == 1))
    def _():
        steps([(nq - 1, False)])

    dk = dk_sc[...] * (1.0 / LOG2E)
    dkv_ref[:, 0:HEAD] = dk[:, 0:HEAD].astype(dkv_ref.dtype)
    dkr_ref[...] = dk[:, HEAD:QPAD]
    dkv_ref[:, HEAD:] = dv_sc[...].astype(dkv_ref.dtype)

    @pl.when(ki == nq - 1)
    def _():
        def emit(i, carry):
            rs = pl.ds(pl.multiple_of(i * t, t), t)
            dq_ref[rs, 0:HEAD] = dq_sc[rs, 0:HEAD].astype(dq_ref.dtype)
            dq_ref[rs, HEAD:QPAD] = _rope_t(dq_sc[rs, HEAD:QPAD], cb[rs, :], sba[rs, :], sbb[rs, :],
                                            ROPE_B // 2).astype(dq_ref.dtype)
            return carry
        lax.fori_loop(0, nq, emit, 0)


def _mla_bwd(qf, kv, kr, dmix, mixed, lse, tabs_b):
    T = qf.shape[0]
    t = min(MLA_T, T)
    head = lambda h, j: (0, h)
    b_half = lambda h, j: (0, NH + h)
    kblk = pl.BlockSpec((t, HEAD), lambda h, j: (j, h))
    return _pcall(
        functools.partial(_mla_bwd_body, t=t), name="mla_bwd",
        grid=(NH, T // t),
        in_specs=[pl.BlockSpec((T, QPAD), head), pl.BlockSpec((t, HEAD), lambda h, j: (j, 2 * h)),
                  pl.BlockSpec((t, HEAD), lambda h, j: (j, 0)),
                  pl.BlockSpec((t, HEAD), lambda h, j: (j, 2 * h + 1)),
                  pl.BlockSpec((T, HEAD), b_half), pl.BlockSpec((T, HEAD), b_half),
                  pl.BlockSpec((T, HEAD), head)] + [pl.BlockSpec((T, HEAD), lambda h, j: (0, 0))] * 3,
        out_specs=[pl.BlockSpec((T, QPAD), head), pl.BlockSpec((t, 2 * HEAD), lambda h, j: (j, h)), kblk],
        out_shape=[jax.ShapeDtypeStruct((T, NH * QPAD), MXU_DTYPE), jax.ShapeDtypeStruct((T, 2 * A_W), MXU_DTYPE),
                   jax.ShapeDtypeStruct((T, A_W), F32)],
        scratch_shapes=[pltpu.VMEM((T, QPAD), F32), pltpu.VMEM((T, HEAD), F32), pltpu.VMEM((t, QPAD), F32),
                        pltpu.VMEM((t, HEAD), F32)],
        compiler_params=pltpu.CompilerParams(dimension_semantics=("parallel", "arbitrary")),
    )(qf, kv, kr, kv, dmix, mixed, lse, *tabs_b)


def _local_step(x, pos, target, g1, g2, gq, gkv, g3, g4,
                in_weights, attn_prefetch, attn_weights, mlp_prefetch, mlp_weights,
                down_grad_ready, up_grad_ready, attn_grads_ready):
    T = x.shape[0]
    TR = 256
    mm = functools.partial(_matmul, tm=2048, tn=1024, tk=2048, b_outer=True)
    mm_k = functools.partial(_matmul, tm=1024, tn=1024, tk=2048)
    mm_g = functools.partial(_matmul, tm=1024, tn=1024, tk=4096, b_outer=True)

    inv_a = ROPE_THETA ** (-jnp.arange(0, ROT_A, 2, dtype=F32) / ROT_A)
    inv_b = ROPE_THETA ** (-jnp.arange(0, ROPE_B, 2, dtype=F32) / ROPE_B)
    inv = jnp.stack([jnp.concatenate([inv_a, inv_a, jnp.zeros((HEAD - ROT_A,), F32)]),
                     jnp.concatenate([inv_b, inv_b, jnp.zeros((HEAD - ROPE_B,), F32)])])
    inv = jnp.concatenate([inv, jnp.zeros((6, HEAD), F32)], axis=0)
    tabs = _rowwise(_rope_tab_body, [pos], [inv], [(HEAD, F32)] * 6, [], tr=512, name="rope_tables")

    (h,) = _rowwise(_rms_fwd_body, [x], [g1], [(D_MODEL, MXU_DTYPE)], [], tr=TR, name="rms_in")
    w_proj = in_weights([h, tabs[0]])
    (proj,) = mm(h, w_proj, dims="nt", out_dtypes=[F32], tm=1024, tn=PROJ_TILE, name="proj_in")
    gq = gq + attn_prefetch(proj)
    q, k, v, cqn, ckvn, krope = _rowwise(
        _postproj_body, [proj] + tabs, [gq, gkv],
        [(A_W, F32)] * 3 + [(LORA, MXU_DTYPE)] * 2 + [(HEAD, MXU_DTYPE)], [], tr=TR, name="post_proj")
    mixed, lse_a = _dil_fwd(q, k, v)

    w_uq_p, w_ukv, w_out = attn_weights(cqn)

    def q_epi(acc, cb, sba, sbb):
        cols = []
        for hh in range(acc.shape[1] // QPAD):
            lo = hh * QPAD
            cols += [acc[:, lo:lo + HEAD], _rope(acc[:, lo + HEAD:lo + QPAD], cb, sba, sbb, ROPE_B // 2)]
        return (jnp.concatenate(cols, axis=1) * MLA_QSCALE,)
    (qf,) = mm(cqn, w_uq_p, dims="nn", out_dtypes=[MXU_DTYPE], name="q_up", epi=q_epi, row_extras=tuple(tabs[3:]))
    (kv,) = mm(ckvn, w_ukv, dims="nn", out_dtypes=[MXU_DTYPE], name="kv_up")
    mixed, lse_b = _mla_fwd(qf, kv, krope, mixed)
    (o,) = mm(mixed, w_out, dims="nn", out_dtypes=[F32], name="out_proj", after=mlp_prefetch(mixed))
    x1, h2 = _rowwise(_mid_body, [x, o], [g2, g3], [(D_MODEL, F32), (D_MODEL, MXU_DTYPE)], [], tr=TR, name="mid_norm")

    w_up, w_down = mlp_weights(h2)

    def up_epi(acc):
        r = jnp.maximum(acc, 0.0)
        return r * r, r
    u, r = mm(h2, w_up, dims="nn", out_dtypes=[MXU_DTYPE, MXU_DTYPE], name="mlp_up", epi=up_epi, b_shards=N_CHIPS)
    (dn,) = mm_k(u, w_down, dims="nn", out_dtypes=[F32], name="mlp_down")
    dy, dd, loss8, dg4 = _rowwise(_loss_body, [x1, dn, target], [g4], [(D_MODEL, F32), (D_MODEL, MXU_DTYPE)],
                                  [(8, HEAD), (8, D_MODEL)], tr=TR, name="loss_head")

    def dup_epi(acc, rr):
        return (acc * (2.0 * rr.astype(F32)),)
    (dup,) = mm(dd, w_down, dims="nt", out_dtypes=[MXU_DTYPE], name="d_up", epi=dup_epi, extras=(r,))
    (gw_down,) = mm_g(u, dd, dims="tn", out_dtypes=[WIRE_DTYPE], name="gw_down")
    (dh2,) = mm_k(dup, w_up, dims="nt", out_dtypes=[F32], name="d_h2", b_shards=N_CHIPS,
                  after=down_grad_ready(gw_down))
    (gw_up,) = mm_g(h2, dup, dims="tn", out_dtypes=[WIRE_DTYPE], name="gw_up", out_shards=N_CHIPS)
    g2 = g2 + up_grad_ready(gw_up)
    dx1, do, dg3, dg2 = _rowwise(_bmid_body, [dy, dh2, x1, o], [g2, g3], [(D_MODEL, F32), (D_MODEL, MXU_DTYPE)],
                                 [(8, D_MODEL), (8, D_MODEL)], tr=TR, name="bwd_mid")
    (dmix,) = mm(do, w_out, dims="nt", out_dtypes=[F32], name="d_mixed")
    (gw_out,) = mm_g(mixed, do, dims="tn", out_dtypes=[WIRE_DTYPE], name="gw_out")

    dq_pad, dkv, dkr = _mla_bwd(qf, kv, krope, dmix, mixed, lse_b, tabs[3:])
    (dcqn,) = mm(dq_pad, w_uq_p, dims="nt", out_dtypes=[F32], name="d_cq")
    (gw_uq_p,) = mm_g(cqn, dq_pad, dims="tn", out_dtypes=[WIRE_DTYPE], name="gw_uq")
    (dckvn,) = mm(dkv, w_ukv, dims="nt", out_dtypes=[F32], name="d_ckv")
    (gw_ukv,) = mm_g(ckvn, dkv, dims="tn", out_dtypes=[WIRE_DTYPE], name="gw_ukv")
    gq = gq + attn_grads_ready(gw_out, gw_uq_p, gw_ukv)

    dq_a, dk_a, dv_a = _dil_bwd(q, k, v, dmix, mixed, lse_a)
    dproj, dgq, dgkv = _rowwise(
        _dproj_body, [dq_a, dk_a, dv_a, dcqn, dckvn, proj, dkr] + tabs, [gq, gkv],
        [(PROJ_COLS, MXU_DTYPE)], [(8, LORA), (8, LORA)], tr=TR, name="d_proj")
    (dh,) = mm_k(dproj, w_proj, dims="nn", out_dtypes=[F32], tk=PROJ_TILE, name="d_h")
    (gw_proj,) = mm_g(dproj, h, dims="tn", out_dtypes=[WIRE_DTYPE], tm=PROJ_TILE, name="gw_in")
    dx, dg1 = _rowwise(_bin_body, [dx1, dh, x], [g1], [(D_MODEL, F32)], [(8, D_MODEL)], tr=TR, name="bwd_in")

    small = jnp.concatenate([dg1, dg2, dgq, dgkv, dg3, dg4, loss8], axis=1)
    return dx, gw_proj, small


def _place():
    x, y, c = lax.axis_index("x"), lax.axis_index("y"), lax.axis_index("c")
    chips = [(1 - x, y), (x, 1 - y), (1 - x, 1 - y)]
    return x, y, c, chips


def _cast_place_body(me_ref, w_ref, *rest):
    o_ref = rest[-1]
    o_ref[...] = w_ref[...].astype(o_ref.dtype)


def _cast_place(me_arr, w, name, after=None):
    rows, cols = w.shape
    tr = min(rows, 256)
    after = [] if after is None else [after]
    grid_spec = pltpu.PrefetchScalarGridSpec(
        num_scalar_prefetch=1, grid=(rows // tr,),
        in_specs=[pl.BlockSpec((tr, cols), lambda i, me: (i, 0))] + [ANY] * len(after),
        out_specs=pl.BlockSpec((None, tr, cols), lambda i, me: (me[0], i, 0)))
    return _pcall(
        _cast_place_body, name=name, grid_spec=grid_spec,
        out_shape=jax.ShapeDtypeStruct((N_CHIPS, rows, cols), WIRE_DTYPE),
        compiler_params=pltpu.CompilerParams(dimension_semantics=("parallel",)),
    )(me_arr, w, *after)


def _cast_place_t_body(me_ref, w_ref, o_ref, *, n):
    i = pl.program_id(0)

    @pl.when(i < n)
    def _():
        o_ref[...] = w_ref[...].astype(o_ref.dtype)

    @pl.when(i == n)
    def _():
        o_ref[...] = jnp.zeros_like(o_ref)


def _cast_place_t(me_arr, w_t, name):
    rows, cols = w_t.shape
    n = rows // IN_TR
    grid_spec = pltpu.PrefetchScalarGridSpec(
        num_scalar_prefetch=1, grid=(n + 1,),
        in_specs=[pl.BlockSpec((IN_TR, cols), lambda i, me: (jnp.minimum(i, n - 1), 0))],
        out_specs=pl.BlockSpec((IN_TR, cols), lambda i, me: (jnp.where(i < n, me[0] * n + i, N_CHIPS * n), 0)))
    return _pcall(
        functools.partial(_cast_place_t_body, n=n), name=name, grid_spec=grid_spec,
        out_shape=jax.ShapeDtypeStruct((PROJ_COLS, cols), WIRE_DTYPE),
        compiler_params=pltpu.CompilerParams(dimension_semantics=("arbitrary",)),
    )(me_arr, w_t)


HBM = pl.BlockSpec(memory_space=pltpu.HBM)
SEM = pl.BlockSpec(memory_space=pltpu.SEMAPHORE)
EFFECT = pltpu.SideEffectType.DATAFLOW_SIDE_EFFECTING


def _copy_start(make, arrays, after, name, n_sems):
    n_a = len(arrays)
    after = [] if after is None else [after]

    def body(*refs):
        for send, _ in make(refs[:n_a], refs[-n_a - 3], refs[-n_a - 2]):
            send.start()
        refs[-1][...] = jnp.zeros_like(refs[-1])

    res = _pcall(
        body, name=name,
        in_specs=[HBM] * n_a + [ANY] * len(after),
        out_specs=[SEM, SEM] + [HBM] * n_a + [pl.BlockSpec(memory_space=pltpu.VMEM)],
        out_shape=[pltpu.SemaphoreType.DMA((n_sems,)), pltpu.SemaphoreType.DMA((n_sems,))]
        + [pltpu.HBM(a.shape, a.dtype) for a in arrays] + [jax.ShapeDtypeStruct((8, HEAD), F32)],
        input_output_aliases={i: 2 + i for i in range(n_a)},
        compiler_params=pltpu.CompilerParams(has_side_effects=EFFECT),
    )(*[pltpu.with_memory_space_constraint(a, pltpu.HBM) for a in arrays], *after)
    return (res[0], res[1]), list(res[2:2 + n_a]), res[-1]


def _copy_wait(make, sems, arrays, after, name):
    n_a = len(arrays)
    after = list(after) if isinstance(after, (list, tuple)) else [after]

    def body(*refs):
        for send, recv in make(refs[:n_a], refs[n_a], refs[n_a + 1]):
            send.wait_send()
            recv.wait_recv()

    return list(_pcall(
        body, name=name,
        in_specs=[HBM] * n_a + [SEM, SEM] + [ANY] * len(after), out_specs=[HBM] * n_a,
        out_shape=[pltpu.HBM(a.shape, a.dtype) for a in arrays],
        input_output_aliases={i: i for i in range(n_a)},
        compiler_params=pltpu.CompilerParams(has_side_effects=EFFECT),
    )(*arrays, sems[0], sems[1], *after))


def _slot(buf, chip, half):
    if buf.ndim == 2:
        hc = buf.shape[1] // 2
        return buf.at[pl.ds(pl.multiple_of(chip * IN_SHARD, 16), IN_SHARD), pl.ds(pl.multiple_of(half * hc, HEAD), hc)]
    hr = buf.shape[1] // 2
    return buf.at[chip, pl.ds(pl.multiple_of(half * hr, 16), hr)]


def _ag_descs(bufs, send_sems, recv_sems):
    x, y, c, chips = _place()
    me = 2 * x + y
    out = []
    for w, buf in enumerate(bufs):
        for j, (px, py) in enumerate(chips):
            mk = lambda ref, w=w, j=j, px=px, py=py: pltpu.make_async_remote_copy(
                src_ref=ref, dst_ref=ref, send_sem=send_sems.at[w * 3 + j], recv_sem=recv_sems.at[w * 3 + j],
                device_id=(px, py, c), device_id_type=MESH)
            out.append((mk(_slot(buf, me, c)), mk(_slot(buf, 2 * px + py, c))))
    return out


def _fw_descs(bufs, send_sems, recv_sems):
    x, y, c, chips = _place()
    out = []
    for w, buf in enumerate(bufs):
        for j, (px, py) in enumerate(chips):
            def mk(which, w=w, j=j, buf=buf, px=px, py=py):
                ref = _slot(buf, 2 * px + py, which)
                return pltpu.make_async_remote_copy(
                    src_ref=ref, dst_ref=ref, send_sem=send_sems.at[w * 3 + j], recv_sem=recv_sems.at[w * 3 + j],
                    device_id=(x, y, 1 - c), device_id_type=MESH)
            out.append((mk(c), mk(1 - c)))
    return out


def _sc_descs(refs, send_sems, recv_sems):
    n_w = len(refs) // 2
    x, y, c, chips = _place()
    me = 2 * x + y
    out = []
    for w in range(n_w):
        for j, (px, py) in enumerate(chips):
            d = pltpu.make_async_remote_copy(
                src_ref=refs[w].at[2 * px + py], dst_ref=refs[n_w + w].at[me],
                send_sem=send_sems.at[w * 3 + j], recv_sem=recv_sems.at[w * 3 + j],
                device_id=(px, py, c), device_id_type=MESH)
            out.append((d, d))
    return out


def _pair_descs(src_of):
    def make(refs, send_sems, recv_sems):
        n_w = len(refs) // 2
        x, y, c, _ = _place()
        out = []
        for w in range(n_w):
            d = pltpu.make_async_remote_copy(
                src_ref=src_of(refs[w], c), dst_ref=refs[n_w + w],
                send_sem=send_sems.at[w], recv_sem=recv_sems.at[w],
                device_id=(x, y, 1 - c), device_id_type=MESH)
            out.append((d, d))
        return out
    return make


_EX_DESCS = _pair_descs(lambda g4, c: g4.at[:, 1 - c])
_SW_DESCS = _pair_descs(lambda half, c: half)
_EXT_DESCS = _pair_descs(lambda g, c: g.at[pl.ds(0, IN_COLS),
                                           pl.ds(pl.multiple_of((1 - c) * (D_MODEL // 2), HEAD), D_MODEL // 2)])


def _sm_descs(refs, send_sems, recv_sems):
    buf = refs[0]
    rows8 = buf.shape[0] // N_DEV
    x, y, c, _ = _place()
    flip = lambda v, d: 1 - v if d else v
    blk = lambda px, py, pc: buf.at[pl.ds(pl.multiple_of((4 * px + 2 * py + pc) * rows8, 8), rows8)]
    out = []
    for k in range(1, N_DEV):
        px, py, pc = flip(x, k & 4), flip(y, k & 2), flip(c, k & 1)
        mk = lambda ref, k=k, px=px, py=py, pc=pc: pltpu.make_async_remote_copy(
            src_ref=ref, dst_ref=ref, send_sem=send_sems.at[k - 1], recv_sem=recv_sems.at[k - 1],
            device_id=(px, py, pc), device_id_type=MESH)
        out.append((mk(blk(x, y, c)), mk(blk(px, py, pc))))
    return out


def _place_rows_body(i_ref, x_ref, o_ref):
    o_ref[...] = x_ref[...]


def _place_rows(i_arr, x, n_blocks, name):
    r, n = x.shape
    grid_spec = pltpu.PrefetchScalarGridSpec(
        num_scalar_prefetch=1, grid=(1,),
        in_specs=[pl.BlockSpec((r, n), lambda g, i: (0, 0))],
        out_specs=pl.BlockSpec((r, n), lambda g, i: (i[0], 0)))
    return _pcall(_place_rows_body, name=name, grid_spec=grid_spec,
                  out_shape=jax.ShapeDtypeStruct((n_blocks * r, n), x.dtype))(i_arr, x)


def _ag_forward_body(*refs, n_w):
    bufs = refs[n_w:2 * n_w]
    send_sems, recv_sems = refs[2 * n_w:]
    pairs = _fw_descs(bufs, send_sems, recv_sems)
    for fw, _ in pairs:
        fw.start()
    for fw, back in pairs:
        back.wait_recv()
        fw.wait_send()


def _ag_forward(bufs, tag):
    n_w = len(bufs)
    return list(_pcall(
        functools.partial(_ag_forward_body, n_w=n_w), name="weight_allgather_forward_" + tag,
        in_specs=[ANY] * n_w, out_specs=[ANY] * n_w,
        out_shape=[jax.ShapeDtypeStruct(b.shape, b.dtype) for b in bufs],
        input_output_aliases={w: w for w in range(n_w)},
        scratch_shapes=[pltpu.SemaphoreType.DMA((3 * n_w,))] * 2,
    )(*bufs))


def _pair_add_body(c_ref, mine_ref, theirs_ref, o_ref):
    o_ref[...] = (mine_ref[...].astype(F32) + theirs_ref[...].astype(F32)).astype(o_ref.dtype)


def _pair_add(c_arr, g4, recv, name):
    _, _, hr, cols = g4.shape
    tr = min(hr, 256)
    grid_spec = pltpu.PrefetchScalarGridSpec(
        num_scalar_prefetch=1, grid=(N_CHIPS, hr // tr),
        in_specs=[pl.BlockSpec((None, None, tr, cols), lambda s, i, c: (s, c[0], i, 0)),
                  pl.BlockSpec((None, tr, cols), lambda s, i, c: (s, i, 0))],
        out_specs=pl.BlockSpec((None, tr, cols), lambda s, i, c: (s, i, 0)))
    return _pcall(
        _pair_add_body, name=name, grid_spec=grid_spec,
        out_shape=jax.ShapeDtypeStruct(recv.shape, recv.dtype),
        compiler_params=pltpu.CompilerParams(dimension_semantics=("parallel", "parallel")),
    )(c_arr, g4, recv)


def _pair_add_t(c_arr, g, recv, name):
    rows, hc = recv.shape
    grid_spec = pltpu.PrefetchScalarGridSpec(
        num_scalar_prefetch=1, grid=(rows // IN_TR,),
        in_specs=[pl.BlockSpec((IN_TR, hc), lambda i, c: (i, c[0])), pl.BlockSpec((IN_TR, hc), lambda i, c: (i, 0))],
        out_specs=pl.BlockSpec((IN_TR, hc), lambda i, c: (i, 0)))
    return _pcall(
        _pair_add_body, name=name, grid_spec=grid_spec,
        out_shape=jax.ShapeDtypeStruct(recv.shape, recv.dtype),
        compiler_params=pltpu.CompilerParams(dimension_semantics=("parallel",)),
    )(c_arr, g, recv)


def _sum4_body(me_ref, p_ref, l0, l1, l2, l3, o_ref):
    me = me_ref[0]
    t = [jnp.where(me == j, p_ref[...], l[...]).astype(F32) for j, l in enumerate((l0, l1, l2, l3))]
    o_ref[...] = ((t[0] + t[1]) + t[2]) + t[3]


def _sum4(me_arr, part, landed, name):
    _, hr, cols = part.shape
    tr = IN_TR if hr == IN_SHARD else min(hr, 256)

    def slot(j):
        return lambda i, me: (jnp.where(me[0] == j, (j + 1) % N_CHIPS, j), i, 0)

    grid_spec = pltpu.PrefetchScalarGridSpec(
        num_scalar_prefetch=1, grid=(hr // tr,),
        in_specs=[pl.BlockSpec((None, tr, cols), lambda i, me: (me[0], i, 0))]
        + [pl.BlockSpec((None, tr, cols), slot(j)) for j in range(N_CHIPS)],
        out_specs=pl.BlockSpec((tr, cols), lambda i, me: (i, 0)))
    return _pcall(
        _sum4_body, name=name, grid_spec=grid_spec,
        out_shape=jax.ShapeDtypeStruct((hr, cols), F32),
        compiler_params=pltpu.CompilerParams(dimension_semantics=("parallel",)),
    )(me_arr, part, landed, landed, landed, landed)


def _adamw(w, g, m, v):
    m = ADAM_B1 * m + (1.0 - ADAM_B1) * g
    v = ADAM_B2 * v + (1.0 - ADAM_B2) * (g * g)
    m_hat = m / (1.0 - ADAM_B1 ** ADAM_STEP)
    v_hat = v / (1.0 - ADAM_B2 ** ADAM_STEP)
    delta = -ADAM_LR * (m_hat / (jnp.sqrt(v_hat) + ADAM_EPS) + ADAM_WD * w)
    return delta, m, v


def _adamw_half_body(h_ref, w_ref, g_in_ref, m_ref, v_ref, *rest):
    g_ref, d_ref, nm_ref, nv_ref, done_ref = rest[-5:]
    done_ref[...] = jnp.zeros_like(done_ref)
    g = g_in_ref[...]
    g_ref[...] = g
    d, m, v = _adamw(w_ref[...], g, m_ref[...], v_ref[...])
    d_ref[...] = d
    nm_ref[...] = m
    nv_ref[...] = v


def _adamw_half(h_arr, w, g_half, m, v, prev, name):
    rows, cols = w.shape
    if g_half.shape[0] == rows:
        tr, nh = IN_TR, rows // IN_TR
        at_half = pl.BlockSpec((tr, cols // 2), lambda i, h: (i, h[0]))
        g_spec = pl.BlockSpec((tr, cols // 2), lambda i, h: (i, 0))
    else:
        tr = min(rows // 2, 128)
        nh = (rows // 2) // tr
        at_half = pl.BlockSpec((tr, cols), lambda i, h: (h[0] * nh + i, 0))
        g_spec = pl.BlockSpec((tr, cols), lambda i, h: (i, 0))
    grid_spec = pltpu.PrefetchScalarGridSpec(
        num_scalar_prefetch=1, grid=(nh,),
        in_specs=[at_half, g_spec, at_half, at_half] + [ANY] * len(prev),
        out_specs=[at_half] * 4 + [pl.BlockSpec((8, HEAD), lambda i, h: (0, 0))])
    return list(_pcall(
        _adamw_half_body, name=name, grid_spec=grid_spec,
        out_shape=[jax.ShapeDtypeStruct(w.shape, F32)] * 4 + [jax.ShapeDtypeStruct((8, HEAD), F32)],
        input_output_aliases={5 + k: k for k in range(len(prev))},
        compiler_params=pltpu.CompilerParams(dimension_semantics=("arbitrary",)),
    )(h_arr, w, g_half, m, v, *prev))


def _small_update_body(gath_ref, w_ref, m_ref, v_ref, g_ref, d_ref, nm_ref, nv_ref, loss_ref, *, n_gain):
    tot = gath_ref[0:1, :]
    for i in range(1, gath_ref.shape[0]):
        tot = tot + gath_ref[i:i + 1, :]
    g = tot[:, 0:n_gain]
    g_ref[...] = g
    d, m, v = _adamw(w_ref[...], g, m_ref[...], v_ref[...])
    d_ref[...] = d
    nm_ref[...] = m
    nv_ref[...] = v
    loss_ref[...] = (0.5 / D_MODEL) * jnp.sum(tot[:, n_gain:n_gain + HEAD], axis=1, keepdims=True) * jnp.ones((1, HEAD), F32)


def _small_update(gath, w, m, v):
    n_gain = w.shape[1]
    vm = pl.BlockSpec(memory_space=pltpu.VMEM)
    return _pcall(
        functools.partial(_small_update_body, n_gain=n_gain), name="gain_update",
        in_specs=[vm] * 4, out_specs=[vm] * 5,
        out_shape=[jax.ShapeDtypeStruct((1, n_gain), F32)] * 4 + [jax.ShapeDtypeStruct((1, HEAD), F32)],
    )(gath, w, m, v)


def kernel(x, positions, norm_attn_pre, norm_attn_post, w_in, q_latent_norm, kv_latent_norm, w_uq, w_ukv, w_out, norm_mlp_pre, norm_mlp_post, w_up, w_down, loss_target, m_norm_attn_pre, m_norm_attn_post, m_w_in, m_q_latent_norm, m_kv_latent_norm, m_w_uq, m_w_ukv, m_w_out, m_norm_mlp_pre, m_norm_mlp_post, m_w_up, m_w_down, v_norm_attn_pre, v_norm_attn_post, v_w_in, v_q_latent_norm, v_kv_latent_norm, v_w_uq, v_w_ukv, v_w_out, v_norm_mlp_pre, v_norm_mlp_post, v_w_up, v_w_down):
    T = x.shape[1]
    c_arr = lax.axis_index("c").astype(jnp.int32).reshape(1)
    me_arr = (2 * lax.axis_index("x") + lax.axis_index("y")).astype(jnp.int32).reshape(1)
    names = ["w_in", "w_uq", "w_ukv", "w_out", "w_up", "w_down"]

    transposed = lambda a: jnp.swapaxes(a, 1, 2)
    mats = [transposed(w_in)[0], w_uq[0], w_ukv[0], w_out[0], w_up[0], w_down[0]]
    me8_arr = (4 * lax.axis_index("x") + 2 * lax.axis_index("y") + lax.axis_index("c")).astype(jnp.int32).reshape(1)
    col_major = lambda g: jnp.transpose(g, (1, 0, 2)).reshape(g.shape[1], N_CHIPS * g.shape[2])
    cast = lambda a: a.astype(MXU_DTYPE)
    to_shards = lambda g: jnp.transpose(g.reshape(g.shape[0], N_CHIPS, g.shape[1] // N_CHIPS), (1, 0, 2))
    halved = lambda g: g.reshape(N_CHIPS, 2, g.shape[1] // 2, g.shape[2])
    empty = lambda a, shape=None: lax.empty(a.shape if shape is None else shape, a.dtype)

    sem_in, buf_in, going = _copy_start(_ag_descs, [_cast_place_t(me_arr, mats[0], "cast_w_in")], None,
                                        "weight_allgather_start_in", 3)
    placed = [_cast_place(me_arr, w, "cast_" + n, going) for w, n in zip(mats[1:], names[1:])]
    sem_att, buf_att, going = _copy_start(_ag_descs, placed[:3], going, "weight_allgather_start_attn", 9)
    sem_mlp, buf_mlp, started = _copy_start(_ag_descs, placed[3:], going, "weight_allgather_start_mlp", 6)

    going_on = {}

    def in_weights(after):
        (win_g,) = _ag_forward(_copy_wait(_ag_descs, sem_in, buf_in, after, "weight_allgather_wait_in"), "in")
        return cast(win_g)

    def attn_prefetch(after):
        landed = _copy_wait(_ag_descs, sem_att, buf_att, after, "weight_allgather_wait_attn")
        going_on["fw_attn"] = _copy_start(_fw_descs, landed, None, "weight_allgather_forward_start_attn", 9)
        return going_on["fw_attn"][-1][0:1, 0:1]

    def attn_weights(after):
        sems, bufs, _ = going_on["fw_attn"]
        wuq_g, wukv_g, wout_g = _copy_wait(_fw_descs, sems, bufs, after, "weight_allgather_forward_wait_attn")
        wuq_full = col_major(wuq_g).reshape(LORA, NH, HEAD + ROPE_B)
        w_uq_p = jnp.pad(wuq_full, ((0, 0), (0, 0), (0, QPAD - HEAD - ROPE_B))).reshape(LORA, NH * QPAD)
        return cast(w_uq_p), cast(col_major(wukv_g)), cast(wout_g.reshape(2 * A_W, D_MODEL))

    def mlp_prefetch(after):
        landed = _copy_wait(_ag_descs, sem_mlp, buf_mlp, after, "weight_allgather_wait_mlp")
        going_on["fw"] = _copy_start(_fw_descs, landed, None, "weight_allgather_forward_start_mlp", 6)
        return going_on["fw"][-1]

    def mlp_weights(after):
        sems, bufs, _ = going_on["fw"]
        wup_g, wdown_g = _copy_wait(_fw_descs, sems, bufs, after, "weight_allgather_forward_wait_mlp")
        return cast(wup_g), cast(wdown_g.reshape(D_FF, D_MODEL))

    def exchange_start(g4s, tag):
        lands = [empty(g, (g.shape[0],) + g.shape[2:]) for g in g4s]
        return _copy_start(_EX_DESCS, g4s + lands, None, "grad_pair_exchange_start_" + tag, len(g4s))

    def exchange_finish(started_ex, after, ns, tag):
        sems, arrs, _ = started_ex
        arrs = _copy_wait(_EX_DESCS, sems, arrs, after, "grad_pair_exchange_wait_" + tag)
        n = len(ns)
        return [_pair_add(c_arr, g4, r, "pair_add_" + nm) for g4, r, nm in zip(arrs[:n], arrs[n:], ns)]

    def scatter_start(parts, after, tag):
        return _copy_start(_sc_descs, parts + [empty(p) for p in parts], after, "grad_scatter_start_" + tag,
                           3 * len(parts))

    def scatter_finish(started_sc, after, tag):
        sems, arrs, _ = started_sc
        arrs = _copy_wait(_sc_descs, sems, arrs, after, "grad_scatter_wait_" + tag)
        return arrs[:len(arrs) // 2], arrs[len(arrs) // 2:]

    def down_grad_ready(gw_down):
        going_on["x_down"] = exchange_start([halved(gw_down.reshape(N_CHIPS, D_MODEL, D_MODEL))], "down")
        return going_on["x_down"][-1]

    def up_grad_ready(gw_up):
        going_on["x_up"] = exchange_start([halved(gw_up)], "up")
        parts = exchange_finish(going_on["x_down"], going_on["x_up"][-1], names[5:], "down")
        going_on["s_down"] = scatter_start(parts, started, "down")
        return going_on["s_down"][-1][0:1, 0:1]

    def attn_grads_ready(gw_out, gw_uq_p, gw_ukv):
        gw_uq = to_shards(gw_uq_p.reshape(LORA, NH, QPAD)[:, :, :HEAD + ROPE_B].reshape(LORA, NH * (HEAD + ROPE_B)))
        full4 = [halved(g) for g in (gw_uq, to_shards(gw_ukv), gw_out.reshape(N_CHIPS, LORA, D_MODEL))]
        x_attn = exchange_start(full4, "attn")
        parts_up = exchange_finish(going_on["x_up"], x_attn[-1], names[4:5], "up")
        parts = exchange_finish(x_attn, parts_up[0], names[1:4], "attn") + parts_up
        going_on["s_rest"] = scatter_start(parts, going_on["s_down"][-1], "attn_up")
        return going_on["s_rest"][-1][0:1, 0:1]

    dx, gw_proj, small = _local_step(
        x[0], positions[0].astype(F32).reshape(T, 1), loss_target[0],
        norm_attn_pre + started[0:1, 0:1], norm_attn_post, q_latent_norm, kv_latent_norm, norm_mlp_pre, norm_mlp_post,
        in_weights, attn_prefetch, attn_weights, mlp_prefetch, mlp_weights,
        down_grad_ready, up_grad_ready, attn_grads_ready)

    ms = [transposed(m_w_in)[0], m_w_uq[0], m_w_ukv[0], m_w_out[0], m_w_up[0], m_w_down[0]]
    vs = [transposed(v_w_in)[0], v_w_uq[0], v_w_ukv[0], v_w_out[0], v_w_up[0], v_w_down[0]]
    sib_arr = 1 - c_arr

    def finish(parts, landed, lo, hi, tag):
        sl = slice(lo, hi)
        halves = [_sum4(me_arr, p, l, "chip_sum_" + n) for p, l, n in zip(parts, landed, names[sl])]
        n = len(halves)
        sems, arrs, _ = _copy_start(_SW_DESCS, halves + [empty(h) for h in halves], None,
                                    "grad_pair_swap_start_" + tag, n)
        own = [_adamw_half(c_arr, w, g, m, v, [], "adamw_own_" + nm)
               for w, g, m, v, nm in zip(mats[sl], arrs[:n], ms[sl], vs[sl], names[sl])]
        arrs = _copy_wait(_SW_DESCS, sems, arrs, own[-1][4], "grad_pair_swap_wait_" + tag)
        return [_adamw_half(sib_arr, w, g, m, v, prev[:4], "adamw_sib_" + nm)
                for w, g, m, v, prev, nm in zip(mats[sl], arrs[n:], ms[sl], vs[sl], own, names[sl])]

    sem_small, (gath,), small_going = _copy_start(
        _sm_descs, [_place_rows(me8_arr, small, N_DEV, "place_small")], None, "small_allgather_start", N_DEV - 1)
    sems, arrs, _ = _copy_start(_EXT_DESCS, [gw_proj, lax.empty((IN_COLS, D_MODEL // 2), WIRE_DTYPE)], None,
                                "grad_pair_exchange_start_in", 1)
    gw_proj, from_sib = _copy_wait(_EXT_DESCS, sems, arrs, small_going, "grad_pair_exchange_wait_in")
    part_in = _pair_add_t(c_arr, gw_proj, from_sib, "pair_add_w_in").reshape(N_CHIPS, IN_SHARD, D_MODEL // 2)
    s_in = scatter_start([part_in], None, "in")
    parts_rest, landed_rest = scatter_finish(going_on["s_rest"], s_in[-1], "attn_up")
    parts_down, landed_down = scatter_finish(going_on["s_down"], landed_rest[0], "down")
    upd_rest = finish(parts_rest + parts_down, landed_rest + landed_down, 1, 6, "rest")
    parts_in, landed_in = scatter_finish(s_in, upd_rest[-1][0], "in")
    upd = finish(parts_in, landed_in, 0, 1, "in") + upd_rest
    grads = [u[0] for u in upd]

    (gath,) = _copy_wait(_sm_descs, sem_small, [gath], grads[0], "small_allgather_wait")
    gains = [norm_attn_pre, norm_attn_post, q_latent_norm, kv_latent_norm, norm_mlp_pre, norm_mlp_post]
    gm = [m_norm_attn_pre, m_norm_attn_post, m_q_latent_norm, m_kv_latent_norm, m_norm_mlp_pre, m_norm_mlp_post]
    gv = [v_norm_attn_pre, v_norm_attn_post, v_q_latent_norm, v_kv_latent_norm, v_norm_mlp_pre, v_norm_mlp_post]
    cat = lambda xs: jnp.concatenate(xs, axis=1)
    g_s, d_s, m_s, v_s, loss_v = _small_update(gath, cat(gains), cat(gm), cat(gv))
    widths = [a.shape[1] for a in gains]
    offs = [sum(widths[:i]) for i in range(len(widths))]
    split = lambda a: [a[:, o:o + w] for o, w in zip(offs, widths)]
    g_gain, d_gain, m_gain, v_gain = split(g_s), split(d_s), split(m_s), split(v_s)

    def ordered(gain_list, mat_list):
        gl, ml = gain_list, [transposed(mat_list[0][None])] + [a[None] for a in mat_list[1:]]
        return [gl[0], gl[1], ml[0], gl[2], gl[3], ml[1], ml[2], ml[3], gl[4], gl[5], ml[4], ml[5]]

    loss = loss_v[0, 0]
    return (loss, dx[None],
            *ordered(g_gain, grads),
            *ordered(d_gain, [u[1] for u in upd]),
            *ordered(m_gain, [u[2] for u in upd]),
            *ordered(v_gain, [u[3] for u in upd]))
```

```python
import functools

import jax
import jax.numpy as jnp
from jax import lax
from jax.experimental import pallas as pl
from jax.experimental.pallas import tpu as pltpu

F32 = jnp.float32
BF16 = jnp.bfloat16
MXU_DTYPE = jnp.bfloat16
WIRE_DTYPE = jnp.bfloat16

D_MODEL = 2048
HEAD = 128
NH = 8
A_W = NH * HEAD
LORA = 512
ROPE_B = 64
QPAD = 256
MAIN_COLS = 3 * A_W + 2 * LORA
IN_COLS = MAIN_COLS + ROPE_B
PROJ_COLS = MAIN_COLS + HEAD
PROJ_TILE = PROJ_COLS // 3
IN_SHARD = 1040
IN_TR = 208
D_FF = 4 * D_MODEL
DIL = (1, 4, 16)
ROT_A = 32
ROPE_THETA = 500000.0
EPS = 1e-6
NEG = -1e30
N_CHIPS = 4
N_DEV = 8

ADAM_LR = 0.001
ADAM_B1 = 0.9
ADAM_B2 = 0.999
ADAM_EPS = 1e-08
ADAM_WD = 0.01
ADAM_STEP = 10

MESH = pl.DeviceIdType.MESH
ANY = pl.BlockSpec(memory_space=pl.ANY)


def _pcall(body, **kw):
    return pl.pallas_call(body, **kw)


_DIMS = {
    "nn": (((1,), (0,)), ((), ())),
    "nt": (((1,), (1,)), ((), ())),
    "tn": (((0,), (0,)), ((), ())),
}


def _mm_body(*refs, dims, nk, epi, n_extra, n_after, n_out):
    a_ref, b_ref = refs[0], refs[1]
    extra = refs[2:2 + n_extra]
    outs = refs[2 + n_extra + n_after:2 + n_extra + n_after + n_out]
    part = lax.dot_general(a_ref[...], b_ref[...], _DIMS[dims], preferred_element_type=F32)

    def finish(acc):
        res = epi(acc, *[r[...] for r in extra]) if epi is not None else (acc,)
        for o_ref, o in zip(outs, res):
            o_ref[...] = o.astype(o_ref.dtype)

    if nk == 1:
        finish(part)
        return
    acc_ref = refs[-1]
    k = pl.program_id(2)

    @pl.when(k == 0)
    def _():
        acc_ref[...] = part

    @pl.when(k > 0)
    def _():
        acc_ref[...] += part

    @pl.when(k == nk - 1)
    def _():
        finish(acc_ref[...])


def _matmul(a, b, *, dims, out_dtypes, tm, tn, tk, name, epi=None, extras=(), row_extras=(), b_outer=False,
            b_shards=0, out_shards=0, after=None):
    if b_shards:
        assert dims in ("nn", "nt") and b.shape[0] == b_shards
        b2 = (b.shape[1], b_shards * b.shape[2])
    else:
        b2 = b.shape
    if dims == "nn":
        (M, K), (K2, N) = a.shape, b2
    elif dims == "nt":
        (M, K), (N, K2) = a.shape, b2
    else:
        (K, M), (K2, N) = a.shape, b2
    assert K == K2, (a.shape, b.shape, dims)
    tm, tn, tk = min(tm, M), min(tn, N), min(tk, K)
    assert M % tm == 0 and N % tn == 0 and K % tk == 0, (name, M, N, K, tm, tn, tk)
    nk = K // tk

    def at(f):
        if b_outer:
            return lambda j, i, k: f(i, j, k)
        return f

    a_spec = {"nn": pl.BlockSpec((tm, tk), at(lambda i, j, k: (i, k))),
              "nt": pl.BlockSpec((tm, tk), at(lambda i, j, k: (i, k))),
              "tn": pl.BlockSpec((tk, tm), at(lambda i, j, k: (k, i)))}[dims]
    b_spec = {"nn": pl.BlockSpec((tk, tn), at(lambda i, j, k: (k, j))),
              "nt": pl.BlockSpec((tn, tk), at(lambda i, j, k: (j, k))),
              "tn": pl.BlockSpec((tk, tn), at(lambda i, j, k: (k, j)))}[dims]
    if b_shards:
        per = b.shape[2] // (tn if dims == "nn" else tk)
        assert per >= 1 and b.shape[2] % (tn if dims == "nn" else tk) == 0
        b_spec = {"nn": pl.BlockSpec((None, tk, tn), at(lambda i, j, k: (j // per, k, j % per))),
                  "nt": pl.BlockSpec((None, tn, tk), at(lambda i, j, k: (k // per, j, k % per)))}[dims]
    o_spec = pl.BlockSpec((tm, tn), at(lambda i, j, k: (i, j)))
    o_shape = (M, N)
    if out_shards:
        assert not extras and N % out_shards == 0 and (N // out_shards) % tn == 0
        o_per = (N // out_shards) // tn
        o_spec = pl.BlockSpec((None, tm, tn), at(lambda i, j, k: (j // o_per, i, j % o_per)))
        o_shape = (out_shards, M, N // out_shards)
    r_specs = [pl.BlockSpec((tm, r.shape[1]), at(lambda i, j, k: (i, 0))) for r in row_extras]
    after = [] if after is None else [after]
    body = functools.partial(_mm_body, dims=dims, nk=nk, epi=epi, n_extra=len(extras) + len(row_extras),
                             n_after=len(after), n_out=len(out_dtypes))
    res = _pcall(
        body, name=name,
        grid=(N // tn, M // tm, nk) if b_outer else (M // tm, N // tn, nk),
        in_specs=[a_spec, b_spec] + [o_spec] * len(extras) + r_specs + [ANY] * len(after),
        out_specs=[o_spec] * len(out_dtypes),
        out_shape=[jax.ShapeDtypeStruct(o_shape, dt) for dt in out_dtypes],
        scratch_shapes=[pltpu.VMEM((tm, tn), F32)] if nk > 1 else [],
        compiler_params=pltpu.CompilerParams(
            dimension_semantics=("parallel", "parallel", "arbitrary")),
    )(a, b, *extras, *row_extras, *after)
    return list(res)


def _rowwise(body, row_ins, vec_ins, row_outs, acc_outs, *, tr, name):
    T = row_ins[0].shape[0]
    tr = min(tr, T)
    assert T % tr == 0
    in_specs = [pl.BlockSpec((tr, a.shape[1]), lambda i: (i, 0)) for a in row_ins]
    in_specs += [pl.BlockSpec(a.shape, lambda i: (0, 0)) for a in vec_ins]
    out_specs = [pl.BlockSpec((tr, w), lambda i: (i, 0)) for (w, _) in row_outs]
    out_specs += [pl.BlockSpec(s, lambda i: (0, 0)) for s in acc_outs]
    out_shape = [jax.ShapeDtypeStruct((T, w), dt) for (w, dt) in row_outs]
    out_shape += [jax.ShapeDtypeStruct(s, F32) for s in acc_outs]
    sem = "arbitrary" if acc_outs else "parallel"
    return list(_pcall(
        body, name=name, grid=(T // tr,), in_specs=in_specs, out_specs=out_specs,
        out_shape=out_shape,
        compiler_params=pltpu.CompilerParams(dimension_semantics=(sem,)),
    )(*row_ins, *vec_ins))


def _rstd(x):
    return lax.rsqrt(jnp.mean(x * x, axis=-1, keepdims=True) + EPS)


def _rms_bwd(x, rstd, dyg):
    xh = x * rstd
    return rstd * (dyg - xh * jnp.mean(dyg * xh, axis=-1, keepdims=True)), xh


def _fold8(v):
    r, w = v.shape
    return jnp.sum(v.reshape(r // 8, 8, w), axis=0)


def _acc(ref, val):
    first = pl.program_id(0) == 0

    @pl.when(first)
    def _():
        ref[...] = val

    @pl.when(jnp.logical_not(first))
    def _():
        ref[...] += val


def _rope(x, c, sa, sb, half):
    return x * c + pltpu.roll(x, HEAD - half, 1) * sa + pltpu.roll(x, half, 1) * sb


def _rope_t(dy, c, sa, sb, half):
    return dy * c - pltpu.roll(dy, HEAD - half, 1) * sa - pltpu.roll(dy, half, 1) * sb


def _rope_tab_body(pos_ref, inv_ref, ca, saa, sab, cb, sba, sbb):
    pos = pos_ref[...]
    lane = lax.broadcasted_iota(jnp.int32, (pos.shape[0], HEAD), 1)
    ang_a = pos * inv_ref[0:1, :]
    ang_b = pos * inv_ref[1:2, :]
    c, s = jnp.cos(ang_a), jnp.sin(ang_a)
    ha = ROT_A // 2
    ca[...] = jnp.where(lane < ROT_A, c, 1.0)
    saa[...] = jnp.where(lane < ha, -s, 0.0)
    sab[...] = jnp.where((lane >= ha) & (lane < ROT_A), s, 0.0)
    c, s = jnp.cos(ang_b), jnp.sin(ang_b)
    hb = ROPE_B // 2
    cb[...] = jnp.where(lane < ROPE_B, c, 1.0)
    sba[...] = jnp.where(lane < hb, -s, 0.0)
    sbb[...] = jnp.where((lane >= hb) & (lane < ROPE_B), s, 0.0)


def _rms_fwd_body(x_ref, g_ref, h_ref):
    x = x_ref[...]
    h_ref[...] = ((x * _rstd(x)) * g_ref[...]).astype(h_ref.dtype)


def _postproj_body(p_ref, ca, saa, sab, cb, sba, sbb, gq_ref, gkv_ref,
                   q_ref, k_ref, v_ref, cqn_ref, ckvn_ref, krope_ref):
    c, sa, sb = ca[...], saa[...], sab[...]
    for h in range(NH):
        lo = h * HEAD
        q_ref[:, lo:lo + HEAD] = _rope(p_ref[:, lo:lo + HEAD], c, sa, sb, ROT_A // 2).astype(q_ref.dtype)
        k_ref[:, lo:lo + HEAD] = _rope(p_ref[:, A_W + lo:A_W + lo + HEAD], c, sa, sb, ROT_A // 2).astype(k_ref.dtype)
    v_ref[...] = p_ref[:, 2 * A_W:3 * A_W].astype(v_ref.dtype)
    cq = p_ref[:, 3 * A_W:3 * A_W + LORA]
    cqn_ref[...] = ((cq * _rstd(cq)) * gq_ref[...]).astype(cqn_ref.dtype)
    ckv = p_ref[:, 3 * A_W + LORA:MAIN_COLS]
    ckvn_ref[...] = ((ckv * _rstd(ckv)) * gkv_ref[...]).astype(ckvn_ref.dtype)
    krope_ref[...] = _rope(p_ref[:, MAIN_COLS:PROJ_COLS], cb[...], sba[...], sbb[...], ROPE_B // 2).astype(krope_ref.dtype)


def _mid_body(x_ref, o_ref, g2_ref, g3_ref, x1_ref, h2_ref):
    o = o_ref[...]
    x1 = x_ref[...] + (o * _rstd(o)) * g2_ref[...]
    x1_ref[...] = x1
    h2_ref[...] = ((x1 * _rstd(x1)) * g3_ref[...]).astype(h2_ref.dtype)


def _loss_body(x1_ref, d_ref, t_ref, g4_ref, dy_ref, dd_ref, loss_ref, dg4_ref):
    d = d_ref[...]
    rstd = _rstd(d)
    y = x1_ref[...] + (d * rstd) * g4_ref[...]
    e = y - t_ref[...]
    dy = e * (1.0 / D_MODEL)
    dy_ref[...] = dy
    dd, dh = _rms_bwd(d, rstd, dy * g4_ref[...])
    dd_ref[...] = dd.astype(dd_ref.dtype)
    _acc(dg4_ref, _fold8(dy * dh))
    e8 = _fold8(e * e)
    l = e8[:, 0:HEAD]
    for j in range(1, D_MODEL // HEAD):
        l = l + e8[:, j * HEAD:(j + 1) * HEAD]
    _acc(loss_ref, l)


def _bmid_body(dy_ref, dh2_ref, x1_ref, o_ref, g2_ref, g3_ref, dx1_ref, do_ref, dg3_ref, dg2_ref):
    x1 = x1_ref[...]
    dh2 = dh2_ref[...]
    dn, x1h = _rms_bwd(x1, _rstd(x1), dh2 * g3_ref[...])
    dx1 = dy_ref[...] + dn
    dx1_ref[...] = dx1
    _acc(dg3_ref, _fold8(dh2 * x1h))
    o = o_ref[...]
    do, oh = _rms_bwd(o, _rstd(o), dx1 * g2_ref[...])
    do_ref[...] = do.astype(do_ref.dtype)
    _acc(dg2_ref, _fold8(dx1 * oh))


def _dproj_body(dq_ref, dk_ref, dv_ref, dcq_ref, dckv_ref, p_ref, dkr_ref,
                ca, saa, sab, cb, sba, sbb, gq_ref, gkv_ref,
                dp_ref, dgq_ref, dgkv_ref):
    c, sa, sb = ca[...], saa[...], sab[...]
    for h in range(NH):
        lo = h * HEAD
        dp_ref[:, lo:lo + HEAD] = _rope_t(dq_ref[:, lo:lo + HEAD], c, sa, sb, ROT_A // 2).astype(dp_ref.dtype)
        dp_ref[:, A_W + lo:A_W + lo + HEAD] = _rope_t(dk_ref[:, lo:lo + HEAD], c, sa, sb, ROT_A // 2).astype(dp_ref.dtype)
    dp_ref[:, 2 * A_W:3 * A_W] = dv_ref[...].astype(dp_ref.dtype)
    cq = p_ref[:, 3 * A_W:3 * A_W + LORA]
    dcqn = dcq_ref[...]
    dcq, cqh = _rms_bwd(cq, _rstd(cq), dcqn * gq_ref[...])
    dp_ref[:, 3 * A_W:3 * A_W + LORA] = dcq.astype(dp_ref.dtype)
    _acc(dgq_ref, _fold8(dcqn * cqh))
    ckv = p_ref[:, 3 * A_W + LORA:MAIN_COLS]
    dckvn = dckv_ref[...]
    dckv, ckvh = _rms_bwd(ckv, _rstd(ckv), dckvn * gkv_ref[...])
    dp_ref[:, 3 * A_W + LORA:MAIN_COLS] = dckv.astype(dp_ref.dtype)
    _acc(dgkv_ref, _fold8(dckvn * ckvh))
    dkr = dkr_ref[:, 0:HEAD]
    for h in range(1, NH):
        dkr = dkr + dkr_ref[:, h * HEAD:(h + 1) * HEAD]
    dp_ref[:, MAIN_COLS:PROJ_COLS] = _rope_t(dkr, cb[...], sba[...], sbb[...], ROPE_B // 2).astype(dp_ref.dtype)


def _bin_body(dx1_ref, dh_ref, x_ref, g1_ref, dx_ref, dg1_ref):
    x = x_ref[...]
    dh = dh_ref[...]
    dn, xh = _rms_bwd(x, _rstd(x), dh * g1_ref[...])
    dx_ref[...] = dx1_ref[...] + dn
    _acc(dg1_ref, _fold8(dh * xh))


def _dot_nt(a, b):
    return lax.dot_general(a, b, _DIMS["nt"], preferred_element_type=F32)


def _dot_tn(a, b):
    return lax.dot_general(a, b, _DIMS["tn"], preferred_element_type=F32)


def _dot_nn(a, b):
    return jnp.dot(a, b, preferred_element_type=F32)


DIL_SCALE = HEAD ** -0.5
DIL_CHUNK = 256


def _dil_rows(t, d, chain):
    r = t >> (chain.bit_length() - 1)
    n = t & (chain - 1)
    start = r + n * (HEAD * d)
    has_prev = n > 0
    pstart = jnp.where(has_prev, start - HEAD * d, start)
    if d == 1:
        return pl.ds(pl.multiple_of(start, HEAD), HEAD), pl.ds(pl.multiple_of(pstart, HEAD), HEAD), has_prev
    return pl.ds(start, HEAD, stride=d), pl.ds(pstart, HEAD, stride=d), has_prev


def _dil_band():
    row = lax.broadcasted_iota(jnp.int32, (HEAD, 2 * HEAD), 0)
    col = lax.broadcasted_iota(jnp.int32, (HEAD, 2 * HEAD), 1)
    return (col >= row) & (col <= row + HEAD), col >= HEAD


def _dil_fwd_body(q_ref, k_ref, v_ref, a_ref, lse_ref, o1, o2, o3, l1, l2, l3, *, nt, unroll):
    band, is_cur = _dil_band()
    for d, o_sc, l_sc in zip(DIL, (o1, o2, o3), (l1, l2, l3)):
        chain = nt // d
        linked = min(chain, unroll)
        assert unroll % linked == 0

        def tiles(g, carry, d=d, o_sc=o_sc, l_sc=l_sc, chain=chain, linked=linked):
            staged = []
            for u in range(unroll):
                rows, prows, has_prev = _dil_rows(g * unroll + u, d, chain)
                q = q_ref[rows, :].astype(MXU_DTYPE)
                kc = k_ref[rows, :].astype(MXU_DTYPE)
                vc = v_ref[rows, :].astype(MXU_DTYPE)
                if u % linked:
                    kp, vp = staged[-1][2], staged[-1][3]
                else:
                    kp = k_ref[prows, :].astype(MXU_DTYPE)
                    vp = v_ref[prows, :].astype(MXU_DTYPE)
                kk = jnp.concatenate([kp, kc], axis=0)
                staged.append((rows, has_prev, kc, vc, jnp.concatenate([vp, vc], axis=0), _dot_nt(q, kk)))
            for rows, has_prev, _, _, vv, s in staged:
                ok = band & (is_cur | has_prev)
                s = jnp.where(ok, s * DIL_SCALE, NEG)
                m = jnp.max(s, axis=1, keepdims=True)
                p = jnp.exp(s - m)
                den = jnp.sum(p, axis=1, keepdims=True)
                o_sc[rows, :] = _dot_nn((p / den).astype(MXU_DTYPE), vv)
                l_sc[rows, :] = jnp.broadcast_to(m + jnp.log(den), (HEAD, HEAD))
            return carry

        lax.fori_loop(0, nt // unroll, tiles, 0)

    def merge(i, carry):
        rs = pl.ds(pl.multiple_of(i * DIL_CHUNK, DIL_CHUNK), DIL_CHUNK)
        la, lb, lc = l1[rs, :], l2[rs, :], l3[rs, :]
        m = jnp.maximum(jnp.maximum(la, lb), lc)
        wa, wb, wc = jnp.exp(la - m), jnp.exp(lb - m), jnp.exp(lc - m)
        den = wa + wb + wc
        a = (wa / den) * o1[rs, :] + (wb / den) * o2[rs, :] + (wc / den) * o3[rs, :]
        a_ref[rs, :] = a.astype(a_ref.dtype)
        lse_ref[rs, :] = m + jnp.log(den)
        return carry

    lax.fori_loop(0, q_ref.shape[0] // DIL_CHUNK, merge, 0)


def _dil_fwd(q, k, v):
    T = q.shape[0]
    spec = pl.BlockSpec((T, HEAD), lambda h: (0, h))
    return _pcall(
        functools.partial(_dil_fwd_body, nt=T // HEAD, unroll=16), name="dil_fwd",
        grid=(NH,), in_specs=[spec] * 3, out_specs=[spec] * 2,
        out_shape=[jax.ShapeDtypeStruct((T, 2 * A_W), MXU_DTYPE), jax.ShapeDtypeStruct((T, A_W), F32)],
        scratch_shapes=[pltpu.VMEM((T, HEAD), F32)] * 6,
        compiler_params=pltpu.CompilerParams(dimension_semantics=("parallel",)),
    )(q, k, v)


def _dil_bwd_body(q_ref, k_ref, v_ref, do_ref, a_ref, lse_ref, dq_ref, dk_ref, dv_ref, dl_sc, *, nt, unroll):
    band, is_cur = _dil_band()

    def prep(i, carry):
        rs = pl.ds(pl.multiple_of(i * DIL_CHUNK, DIL_CHUNK), DIL_CHUNK)
        dl = jnp.sum(do_ref[rs, :] * a_ref[rs, :].astype(F32), axis=1, keepdims=True)
        dl_sc[rs, :] = jnp.broadcast_to(dl, (DIL_CHUNK, HEAD))
        zero = jnp.zeros((DIL_CHUNK, HEAD), F32)
        dq_ref[rs, :] = zero
        dk_ref[rs, :] = zero
        dv_ref[rs, :] = zero
        return carry

    lax.fori_loop(0, q_ref.shape[0] // DIL_CHUNK, prep, 0)

    for d in DIL:
        chain = nt // d
        linked = min(chain, unroll)
        assert unroll % linked == 0

        def tiles(g, carry, d=d, chain=chain, linked=linked):
            staged = []
            for u in range(unroll):
                rows, prows, has_prev = _dil_rows(g * unroll + u, d, chain)
                q = q_ref[rows, :].astype(MXU_DTYPE)
                do = do_ref[rows, :].astype(MXU_DTYPE)
                kc = k_ref[rows, :].astype(MXU_DTYPE)
                vc = v_ref[rows, :].astype(MXU_DTYPE)
                if u % linked:
                    kp, vp = staged[-1][5], staged[-1][6]
                else:
                    kp = k_ref[prows, :].astype(MXU_DTYPE)
                    vp = v_ref[prows, :].astype(MXU_DTYPE)
                kk = jnp.concatenate([kp, kc], axis=0)
                vv = jnp.concatenate([vp, vc], axis=0)
                staged.append((rows, prows, has_prev, q, do, kc, vc, kk, _dot_nt(q, kk), _dot_nt(do, vv)))

            def add_own(own):
                rows, dk_own, dv_own = own
                dk_ref[rows, :] += dk_own
                dv_ref[rows, :] += dv_own

            own = None
            for u, (rows, prows, has_prev, q, do, _, _, kk, s, dp) in enumerate(staged):
                lse = lse_ref[rows, :]
                dl = dl_sc[rows, :]
                ok = band & (is_cur | has_prev)
                p = jnp.where(ok, jnp.exp(s * DIL_SCALE - jnp.concatenate([lse, lse], axis=1)), 0.0)
                ds = (p * (dp - jnp.concatenate([dl, dl], axis=1))).astype(MXU_DTYPE)
                dq_ref[rows, :] += _dot_nn(ds, kk) * DIL_SCALE
                dkk = _dot_tn(ds, q) * DIL_SCALE
                dvv = _dot_tn(p.astype(MXU_DTYPE), do)
                if u % linked:
                    add_own((own[0], own[1] + dkk[:HEAD, :], own[2] + dvv[:HEAD, :]))
                else:
                    if own is not None:
                        add_own(own)
                    dk_ref[prows, :] += dkk[:HEAD, :]
                    dv_ref[prows, :] += dvv[:HEAD, :]
                own = (rows, dkk[HEAD:, :], dvv[HEAD:, :])
            add_own(own)
            return carry

        lax.fori_loop(0, nt // unroll, tiles, 0)


def _dil_bwd(q, k, v, dmix, mixed, lse):
    T = q.shape[0]
    spec = pl.BlockSpec((T, HEAD), lambda h: (0, h))
    return _pcall(
        functools.partial(_dil_bwd_body, nt=T // HEAD, unroll=8), name="dil_bwd",
        grid=(NH,), in_specs=[spec] * 6, out_specs=[spec] * 3,
        out_shape=[jax.ShapeDtypeStruct((T, A_W), F32)] * 3,
        scratch_shapes=[pltpu.VMEM((T, HEAD), F32)],
        compiler_params=pltpu.CompilerParams(dimension_semantics=("parallel",)),
    )(q, k, v, dmix, mixed, lse)


MLA_SCALE = (HEAD + ROPE_B) ** -0.5
LOG2E = 1.4426950408889634
MLA_QSCALE = MLA_SCALE * LOG2E
MLA_T = 512
MLA_HP = 4


def _tri(t):
    row = lax.broadcasted_iota(jnp.int32, (t, t), 0)
    col = lax.broadcasted_iota(jnp.int32, (t, t), 1)
    return col <= row


def _lanes(x, n):
    return jnp.tile(x, (1, n // HEAD))


def _mla_fwd_body(q_ref, kv_ref, kr_ref, mixed_ref, o_ref, lse_ref, m_sc, l_sc, acc_sc, *, t, hp):
    del mixed_ref
    qi = pl.program_id(1)
    m_sc[...] = jnp.full(m_sc.shape, NEG, F32)
    l_sc[...] = jnp.zeros(l_sc.shape, F32)
    acc_sc[...] = jnp.zeros(acc_sc.shape, F32)

    def step(j, masked):
        ks = pl.ds(pl.multiple_of(j * t, t), t)
        kr = kr_ref[ks, :]
        logits = []
        for hh in range(hp):
            kcat = jnp.concatenate([kv_ref[ks, 2 * hh * HEAD:(2 * hh + 1) * HEAD], kr], axis=1)
            logits.append(_dot_nt(q_ref[:, hh * QPAD:(hh + 1) * QPAD], kcat))
        for hh in range(hp):
            s = logits[hh]
            if masked:
                s = jnp.where(_tri(t), s, NEG)
            m_prev = m_sc[hh]
            m_new = jnp.maximum(m_prev, jnp.max(s, axis=1, keepdims=True))
            alpha = jnp.exp2(m_prev - m_new)
            p = jnp.exp2(s - _lanes(m_new, t))
            l_sc[hh] = alpha * l_sc[hh] + jnp.sum(p, axis=1, keepdims=True)
            acc_sc[hh] = alpha * acc_sc[hh] + _dot_nn(p.astype(MXU_DTYPE), kv_ref[ks, (2 * hh + 1) * HEAD:(2 * hh + 2) * HEAD])
            m_sc[hh] = m_new

    def off_diag(j, carry):
        step(j, False)
        return carry

    lax.fori_loop(0, qi, off_diag, 0)
    step(qi, True)
    for hh in range(hp):
        l = l_sc[hh]
        o_ref[:, hh * HEAD:(hh + 1) * HEAD] = (acc_sc[hh] / l).astype(o_ref.dtype)
        lse_ref[:, hh * HEAD:(hh + 1) * HEAD] = m_sc[hh] + jnp.log2(l)


def _mla_fwd(qf, kv, kr, mixed):
    T = qf.shape[0]
    t, hp = min(MLA_T, T), MLA_HP
    ng = NH // hp
    return _pcall(
        functools.partial(_mla_fwd_body, t=t, hp=hp), name="mla_fwd",
        grid=(ng, T // t),
        in_specs=[pl.BlockSpec((t, hp * QPAD), lambda g, i: (i, g)),
                  pl.BlockSpec((T, hp * 2 * HEAD), lambda g, i: (0, g)),
                  pl.BlockSpec((T, HEAD), lambda g, i: (0, 0)), ANY],
        out_specs=[pl.BlockSpec((t, hp * HEAD), lambda g, i: (i, ng + g)),
                   pl.BlockSpec((t, hp * HEAD), lambda g, i: (i, g))],
        out_shape=[jax.ShapeDtypeStruct(mixed.shape, mixed.dtype), jax.ShapeDtypeStruct((T, A_W), F32)],
        input_output_aliases={3: 0},
        scratch_shapes=[pltpu.VMEM((hp, t, HEAD), F32)] * 3,
        compiler_params=pltpu.CompilerParams(dimension_semantics=("parallel", "parallel")),
    )(qf, kv, kr, mixed)


def _mla_bwd_body(q_ref, kn_ref, kr_ref, v_ref, do_ref, o_ref, lse_ref, cb, sba, sbb,
                  dq_ref, dkv_ref, dkr_ref, dq_sc, dl_sc, dk_sc, dv_sc, *, t):
    ki = pl.program_id(1)
    nq = q_ref.shape[0] // t

    @pl.when(ki == 0)
    def _():
        def prep(i, carry):
            rs = pl.ds(pl.multiple_of(i * t, t), t)
            dl = jnp.sum(do_ref[rs, :] * o_ref[rs, :].astype(F32), axis=1, keepdims=True)
            dl_sc[rs, :] = jnp.broadcast_to(dl, (t, HEAD))
            dq_sc[rs, :] = jnp.zeros((t, QPAD), F32)
            return carry
        lax.fori_loop(0, nq, prep, 0)

    kcat = jnp.concatenate([kn_ref[...], kr_ref[...]], axis=1)
    v = v_ref[...]
    dk_sc[...] = jnp.zeros(dk_sc.shape, F32)
    dv_sc[...] = jnp.zeros(dv_sc.shape, F32)

    def steps(blocks):
        staged = []
        for i, masked in blocks:
            qs = pl.ds(pl.multiple_of(i * t, t), t)
            q = q_ref[qs, :]
            do = do_ref[qs, :].astype(MXU_DTYPE)
            staged.append((qs, q, do, _dot_nt(q, kcat), _dot_nt(do, v), masked))
        for qs, q, do, s, dp, masked in staged:
            p = jnp.exp2(s - _lanes(lse_ref[qs, :], t))
            if masked:
                p = jnp.where(_tri(t), p, 0.0)
            ds = (p * (dp - _lanes(dl_sc[qs, :], t))).astype(MXU_DTYPE)
            dv_sc[...] += _dot_tn(p.astype(MXU_DTYPE), do)
            dk_sc[...] += _dot_tn(ds, q)
            dq_sc[qs, :] += _dot_nn(ds, kcat) * MLA_SCALE

    n_blocks = nq - ki

    @pl.when(n_blocks == 1)
    def _():
        steps([(ki, True)])

    @pl.when(n_blocks > 1)
    def _():
        steps([(ki, True), (ki + 1, False)])

    def pair(j, carry):
        steps([(ki + 2 * j, False), (ki + 2 * j + 1, False)])
        return carry

    lax.fori_loop(1, n_blocks // 2, pair, 0)

    @pl.when((n_blocks > 1) & (n_blocks % 2 == 1))
    def _():
        steps([(nq - 1, False)])

    dk = dk_sc[...] * (1.0 / LOG2E)
    dkv_ref[:, 0:HEAD] = dk[:, 0:HEAD].astype(dkv_ref.dtype)
    dkr_ref[...] = dk[:, HEAD:QPAD]
    dkv_ref[:, HEAD:] = dv_sc[...].astype(dkv_ref.dtype)

    @pl.when(ki == nq - 1)
    def _():
        def emit(i, carry):
            rs = pl.ds(pl.multiple_of(i * t, t), t)
            dq_ref[rs, 0:HEAD] = dq_sc[rs, 0:HEAD].astype(dq_ref.dtype)
            dq_ref[rs, HEAD:QPAD] = _rope_t(dq_sc[rs, HEAD:QPAD], cb[rs, :], sba[rs, :], sbb[rs, :],
                                            ROPE_B // 2).astype(dq_ref.dtype)
            return carry
        lax.fori_loop(0, nq, emit, 0)


def _mla_bwd(qf, kv, kr, dmix, mixed, lse, tabs_b):
    T = qf.shape[0]
    t = min(MLA_T, T)
    head = lambda h, j: (0, h)
    b_half = lambda h, j: (0, NH + h)
    kblk = pl.BlockSpec((t, HEAD), lambda h, j: (j, h))
    return _pcall(
        functools.partial(_mla_bwd_body, t=t), name="mla_bwd",
        grid=(NH, T // t),
        in_specs=[pl.BlockSpec((T, QPAD), head), pl.BlockSpec((t, HEAD), lambda h, j: (j, 2 * h)),
                  pl.BlockSpec((t, HEAD), lambda h, j: (j, 0)),
                  pl.BlockSpec((t, HEAD), lambda h, j: (j, 2 * h + 1)),
                  pl.BlockSpec((T, HEAD), b_half), pl.BlockSpec((T, HEAD), b_half),
                  pl.BlockSpec((T, HEAD), head)] + [pl.BlockSpec((T, HEAD), lambda h, j: (0, 0))] * 3,
        out_specs=[pl.BlockSpec((T, QPAD), head), pl.BlockSpec((t, 2 * HEAD), lambda h, j: (j, h)), kblk],
        out_shape=[jax.ShapeDtypeStruct((T, NH * QPAD), MXU_DTYPE), jax.ShapeDtypeStruct((T, 2 * A_W), MXU_DTYPE),
                   jax.ShapeDtypeStruct((T, A_W), F32)],
        scratch_shapes=[pltpu.VMEM((T, QPAD), F32), pltpu.VMEM((T, HEAD), F32), pltpu.VMEM((t, QPAD), F32),
                        pltpu.VMEM((t, HEAD), F32)],
        compiler_params=pltpu.CompilerParams(dimension_semantics=("parallel", "arbitrary")),
    )(qf, kv, kr, kv, dmix, mixed, lse, *tabs_b)


def _local_step(x, pos, target, g1, g2, gq, gkv, g3, g4,
                in_weights, attn_prefetch, attn_weights, mlp_prefetch, mlp_weights,
                down_grad_ready, up_grad_ready, attn_grads_ready):
    T = x.shape[0]
    TR = 256
    mm = functools.partial(_matmul, tm=2048, tn=1024, tk=2048, b_outer=True)
    mm_k = functools.partial(_matmul, tm=1024, tn=1024, tk=2048)
    mm_g = functools.partial(_matmul, tm=1024, tn=1024, tk=4096, b_outer=True)

    inv_a = ROPE_THETA ** (-jnp.arange(0, ROT_A, 2, dtype=F32) / ROT_A)
    inv_b = ROPE_THETA ** (-jnp.arange(0, ROPE_B, 2, dtype=F32) / ROPE_B)
    inv = jnp.stack([jnp.concatenate([inv_a, inv_a, jnp.zeros((HEAD - ROT_A,), F32)]),
                     jnp.concatenate([inv_b, inv_b, jnp.zeros((HEAD - ROPE_B,), F32)])])
    inv = jnp.concatenate([inv, jnp.zeros((6, HEAD), F32)], axis=0)
    tabs = _rowwise(_rope_tab_body, [pos], [inv], [(HEAD, F32)] * 6, [], tr=512, name="rope_tables")

    (h,) = _rowwise(_rms_fwd_body, [x], [g1], [(D_MODEL, MXU_DTYPE)], [], tr=TR, name="rms_in")
    w_proj = in_weights([h, tabs[0]])
    (proj,) = mm(h, w_proj, dims="nt", out_dtypes=[F32], tm=1024, tn=PROJ_TILE, name="proj_in")
    gq = gq + attn_prefetch(proj)
    q, k, v, cqn, ckvn, krope = _rowwise(
        _postproj_body, [proj] + tabs, [gq, gkv],
        [(A_W, F32)] * 3 + [(LORA, MXU_DTYPE)] * 2 + [(HEAD, MXU_DTYPE)], [], tr=TR, name="post_proj")
    mixed, lse_a = _dil_fwd(q, k, v)

    w_uq_p, w_ukv, w_out = attn_weights(cqn)

    def q_epi(acc, cb, sba, sbb):
        cols = []
        for hh in range(acc.shape[1] // QPAD):
            lo = hh * QPAD
            cols += [acc[:, lo:lo + HEAD], _rope(acc[:, lo + HEAD:lo + QPAD], cb, sba, sbb, ROPE_B // 2)]
        return (jnp.concatenate(cols, axis=1) * MLA_QSCALE,)
    (qf,) = mm(cqn, w_uq_p, dims="nn", out_dtypes=[MXU_DTYPE], name="q_up", epi=q_epi, row_extras=tuple(tabs[3:]))
    (kv,) = mm(ckvn, w_ukv, dims="nn", out_dtypes=[MXU_DTYPE], name="kv_up")
    mixed, lse_b = _mla_fwd(qf, kv, krope, mixed)
    (o,) = mm(mixed, w_out, dims="nn", out_dtypes=[F32], name="out_proj", after=mlp_prefetch(mixed))
    x1, h2 = _rowwise(_mid_body, [x, o], [g2, g3], [(D_MODEL, F32), (D_MODEL, MXU_DTYPE)], [], tr=TR, name="mid_norm")

    w_up, w_down = mlp_weights(h2)

    def up_epi(acc):
        r = jnp.maximum(acc, 0.0)
        return r * r, r
    u, r = mm(h2, w_up, dims="nn", out_dtypes=[MXU_DTYPE, MXU_DTYPE], name="mlp_up", epi=up_epi, b_shards=N_CHIPS)
    (dn,) = mm_k(u, w_down, dims="nn", out_dtypes=[F32], name="mlp_down")
    dy, dd, loss8, dg4 = _rowwise(_loss_body, [x1, dn, target], [g4], [(D_MODEL, F32), (D_MODEL, MXU_DTYPE)],
                                  [(8, HEAD), (8, D_MODEL)], tr=TR, name="loss_head")

    def dup_epi(acc, rr):
        return (acc * (2.0 * rr.astype(F32)),)
    (dup,) = mm(dd, w_down, dims="nt", out_dtypes=[MXU_DTYPE], name="d_up", epi=dup_epi, extras=(r,))
    (gw_down,) = mm_g(u, dd, dims="tn", out_dtypes=[WIRE_DTYPE], name="gw_down")
    (dh2,) = mm_k(dup, w_up, dims="nt", out_dtypes=[F32], name="d_h2", b_shards=N_CHIPS,
                  after=down_grad_ready(gw_down))
    (gw_up,) = mm_g(h2, dup, dims="tn", out_dtypes=[WIRE_DTYPE], name="gw_up", out_shards=N_CHIPS)
    g2 = g2 + up_grad_ready(gw_up)
    dx1, do, dg3, dg2 = _rowwise(_bmid_body, [dy, dh2, x1, o], [g2, g3], [(D_MODEL, F32), (D_MODEL, MXU_DTYPE)],
                                 [(8, D_MODEL), (8, D_MODEL)], tr=TR, name="bwd_mid")
    (dmix,) = mm(do, w_out, dims="nt", out_dtypes=[F32], name="d_mixed")
    (gw_out,) = mm_g(mixed, do, dims="tn", out_dtypes=[WIRE_DTYPE], name="gw_out")

    dq_pad, dkv, dkr = _mla_bwd(qf, kv, krope, dmix, mixed, lse_b, tabs[3:])
    (dcqn,) = mm(dq_pad, w_uq_p, dims="nt", out_dtypes=[F32], name="d_cq")
    (gw_uq_p,) = mm_g(cqn, dq_pad, dims="tn", out_dtypes=[WIRE_DTYPE], name="gw_uq")
    (dckvn,) = mm(dkv, w_ukv, dims="nt", out_dtypes=[F32], name="d_ckv")
    (gw_ukv,) = mm_g(ckvn, dkv, dims="tn", out_dtypes=[WIRE_DTYPE], name="gw_ukv")
    gq = gq + attn_grads_ready(gw_out, gw_uq_p, gw_ukv)

    dq_a, dk_a, dv_a = _dil_bwd(q, k, v, dmix, mixed, lse_a)
    dproj, dgq, dgkv = _rowwise(
        _dproj_body, [dq_a, dk_a, dv_a, dcqn, dckvn, proj, dkr] + tabs, [gq, gkv],
        [(PROJ_COLS, MXU_DTYPE)], [(8, LORA), (8, LORA)], tr=TR, name="d_proj")
    (dh,) = mm_k(dproj, w_proj, dims="nn", out_dtypes=[F32], tk=PROJ_TILE, name="d_h")
    (gw_proj,) = mm_g(dproj, h, dims="tn", out_dtypes=[WIRE_DTYPE], tm=PROJ_TILE, name="gw_in")
    dx, dg1 = _rowwise(_bin_body, [dx1, dh, x], [g1], [(D_MODEL, F32)], [(8, D_MODEL)], tr=TR, name="bwd_in")

    small = jnp.concatenate([dg1, dg2, dgq, dgkv, dg3, dg4, loss8], axis=1)
    return dx, gw_proj, small


def _place():
    x, y, c = lax.axis_index("x"), lax.axis_index("y"), lax.axis_index("c")
    chips = [(1 - x, y), (x, 1 - y), (1 - x, 1 - y)]
    return x, y, c, chips


def _cast_place_body(me_ref, w_ref, *rest):
    o_ref = rest[-1]
    o_ref[...] = w_ref[...].astype(o_ref.dtype)


def _cast_place(me_arr, w, name, after=None):
    rows, cols = w.shape
    tr = min(rows, 256)
    after = [] if after is None else [after]
    grid_spec = pltpu.PrefetchScalarGridSpec(
        num_scalar_prefetch=1, grid=(rows // tr,),
        in_specs=[pl.BlockSpec((tr, cols), lambda i, me: (i, 0))] + [ANY] * len(after),
        out_specs=pl.BlockSpec((None, tr, cols), lambda i, me: (me[0], i, 0)))
    return _pcall(
        _cast_place_body, name=name, grid_spec=grid_spec,
        out_shape=jax.ShapeDtypeStruct((N_CHIPS, rows, cols), WIRE_DTYPE),
        compiler_params=pltpu.CompilerParams(dimension_semantics=("parallel",)),
    )(me_arr, w, *after)


def _cast_place_t_body(me_ref, w_ref, o_ref, *, n):
    i = pl.program_id(0)

    @pl.when(i < n)
    def _():
        o_ref[...] = w_ref[...].astype(o_ref.dtype)

    @pl.when(i == n)
    def _():
        o_ref[...] = jnp.zeros_like(o_ref)


def _cast_place_t(me_arr, w_t, name):
    rows, cols = w_t.shape
    n = rows // IN_TR
    grid_spec = pltpu.PrefetchScalarGridSpec(
        num_scalar_prefetch=1, grid=(n + 1,),
        in_specs=[pl.BlockSpec((IN_TR, cols), lambda i, me: (jnp.minimum(i, n - 1), 0))],
        out_specs=pl.BlockSpec((IN_TR, cols), lambda i, me: (jnp.where(i < n, me[0] * n + i, N_CHIPS * n), 0)))
    return _pcall(
        functools.partial(_cast_place_t_body, n=n), name=name, grid_spec=grid_spec,
        out_shape=jax.ShapeDtypeStruct((PROJ_COLS, cols), WIRE_DTYPE),
        compiler_params=pltpu.CompilerParams(dimension_semantics=("arbitrary",)),
    )(me_arr, w_t)


HBM = pl.BlockSpec(memory_space=pltpu.HBM)
SEM = pl.BlockSpec(memory_space=pltpu.SEMAPHORE)
EFFECT = pltpu.SideEffectType.DATAFLOW_SIDE_EFFECTING


def _copy_start(make, arrays, after, name, n_sems):
    n_a = len(arrays)
    after = [] if after is None else [after]

    def body(*refs):
        for send, _ in make(refs[:n_a], refs[-n_a - 3], refs[-n_a - 2]):
            send.start()
        refs[-1][...] = jnp.zeros_like(refs[-1])

    res = _pcall(
        body, name=name,
        in_specs=[HBM] * n_a + [ANY] * len(after),
        out_specs=[SEM, SEM] + [HBM] * n_a + [pl.BlockSpec(memory_space=pltpu.VMEM)],
        out_shape=[pltpu.SemaphoreType.DMA((n_sems,)), pltpu.SemaphoreType.DMA((n_sems,))]
        + [pltpu.HBM(a.shape, a.dtype) for a in arrays] + [jax.ShapeDtypeStruct((8, HEAD), F32)],
        input_output_aliases={i: 2 + i for i in range(n_a)},
        compiler_params=pltpu.CompilerParams(has_side_effects=EFFECT),
    )(*[pltpu.with_memory_space_constraint(a, pltpu.HBM) for a in arrays], *after)
    return (res[0], res[1]), list(res[2:2 + n_a]), res[-1]


def _copy_wait(make, sems, arrays, after, name):
    n_a = len(arrays)
    after = list(after) if isinstance(after, (list, tuple)) else [after]

    def body(*refs):
        for send, recv in make(refs[:n_a], refs[n_a], refs[n_a + 1]):
            send.wait_send()
            recv.wait_recv()

    return list(_pcall(
        body, name=name,
        in_specs=[HBM] * n_a + [SEM, SEM] + [ANY] * len(after), out_specs=[HBM] * n_a,
        out_shape=[pltpu.HBM(a.shape, a.dtype) for a in arrays],
        input_output_aliases={i: i for i in range(n_a)},
        compiler_params=pltpu.CompilerParams(has_side_effects=EFFECT),
    )(*arrays, sems[0], sems[1], *after))


def _slot(buf, chip, half):
    if buf.ndim == 2:
        hc = buf.shape[1] // 2
        return buf.at[pl.ds(pl.multiple_of(chip * IN_SHARD, 16), IN_SHARD), pl.ds(pl.multiple_of(half * hc, HEAD), hc)]
    hr = buf.shape[1] // 2
    return buf.at[chip, pl.ds(pl.multiple_of(half * hr, 16), hr)]


def _ag_descs(bufs, send_sems, recv_sems):
    x, y, c, chips = _place()
    me = 2 * x + y
    out = []
    for w, buf in enumerate(bufs):
        for j, (px, py) in enumerate(chips):
            mk = lambda ref, w=w, j=j, px=px, py=py: pltpu.make_async_remote_copy(
                src_ref=ref, dst_ref=ref, send_sem=send_sems.at[w * 3 + j], recv_sem=recv_sems.at[w * 3 + j],
                device_id=(px, py, c), device_id_type=MESH)
            out.append((mk(_slot(buf, me, c)), mk(_slot(buf, 2 * px + py, c))))
    return out


def _fw_descs(bufs, send_sems, recv_sems):
    x, y, c, chips = _place()
    out = []
    for w, buf in enumerate(bufs):
        for j, (px, py) in enumerate(chips):
            def mk(which, w=w, j=j, buf=buf, px=px, py=py):
                ref = _slot(buf, 2 * px + py, which)
                return pltpu.make_async_remote_copy(
                    src_ref=ref, dst_ref=ref, send_sem=send_sems.at[w * 3 + j], recv_sem=recv_sems.at[w * 3 + j],
                    device_id=(x, y, 1 - c), device_id_type=MESH)
            out.append((mk(c), mk(1 - c)))
    return out


def _sc_descs(refs, send_sems, recv_sems):
    n_w = len(refs) // 2
    x, y, c, chips = _place()
    me = 2 * x + y
    out = []
    for w in range(n_w):
        for j, (px, py) in enumerate(chips):
            d = pltpu.make_async_remote_copy(
                src_ref=refs[w].at[2 * px + py], dst_ref=refs[n_w + w].at[me],
                send_sem=send_sems.at[w * 3 + j], recv_sem=recv_sems.at[w * 3 + j],
                device_id=(px, py, c), device_id_type=MESH)
            out.append((d, d))
    return out


def _pair_descs(src_of):
    def make(refs, send_sems, recv_sems):
        n_w = len(refs) // 2
        x, y, c, _ = _place()
        out = []
        for w in range(n_w):
            d = pltpu.make_async_remote_copy(
                src_ref=src_of(refs[w], c), dst_ref=refs[n_w + w],
                send_sem=send_sems.at[w], recv_sem=recv_sems.at[w],
                device_id=(x, y, 1 - c), device_id_type=MESH)
            out.append((d, d))
        return out
    return make


_EX_DESCS = _pair_descs(lambda g4, c: g4.at[:, 1 - c])
_SW_DESCS = _pair_descs(lambda half, c: half)
_EXT_DESCS = _pair_descs(lambda g, c: g.at[pl.ds(0, IN_COLS),
                                           pl.ds(pl.multiple_of((1 - c) * (D_MODEL // 2), HEAD), D_MODEL // 2)])


def _sm_descs(refs, send_sems, recv_sems):
    buf = refs[0]
    rows8 = buf.shape[0] // N_DEV
    x, y, c, _ = _place()
    flip = lambda v, d: 1 - v if d else v
    blk = lambda px, py, pc: buf.at[pl.ds(pl.multiple_of((4 * px + 2 * py + pc) * rows8, 8), rows8)]
    out = []
    for k in range(1, N_DEV):
        px, py, pc = flip(x, k & 4), flip(y, k & 2), flip(c, k & 1)
        mk = lambda ref, k=k, px=px, py=py, pc=pc: pltpu.make_async_remote_copy(
            src_ref=ref, dst_ref=ref, send_sem=send_sems.at[k - 1], recv_sem=recv_sems.at[k - 1],
            device_id=(px, py, pc), device_id_type=MESH)
        out.append((mk(blk(x, y, c)), mk(blk(px, py, pc))))
    return out


def _place_rows_body(i_ref, x_ref, o_ref):
    o_ref[...] = x_ref[...]


def _place_rows(i_arr, x, n_blocks, name):
    r, n = x.shape
    grid_spec = pltpu.PrefetchScalarGridSpec(
        num_scalar_prefetch=1, grid=(1,),
        in_specs=[pl.BlockSpec((r, n), lambda g, i: (0, 0))],
        out_specs=pl.BlockSpec((r, n), lambda g, i: (i[0], 0)))
    return _pcall(_place_rows_body, name=name, grid_spec=grid_spec,
                  out_shape=jax.ShapeDtypeStruct((n_blocks * r, n), x.dtype))(i_arr, x)


def _ag_forward_body(*refs, n_w):
    bufs = refs[n_w:2 * n_w]
    send_sems, recv_sems = refs[2 * n_w:]
    pairs = _fw_descs(bufs, send_sems, recv_sems)
    for fw, _ in pairs:
        fw.start()
    for fw, back in pairs:
        back.wait_recv()
        fw.wait_send()


def _ag_forward(bufs, tag):
    n_w = len(bufs)
    return list(_pcall(
        functools.partial(_ag_forward_body, n_w=n_w), name="weight_allgather_forward_" + tag,
        in_specs=[ANY] * n_w, out_specs=[ANY] * n_w,
        out_shape=[jax.ShapeDtypeStruct(b.shape, b.dtype) for b in bufs],
        input_output_aliases={w: w for w in range(n_w)},
        scratch_shapes=[pltpu.SemaphoreType.DMA((3 * n_w,))] * 2,
    )(*bufs))


def _pair_add_body(c_ref, mine_ref, theirs_ref, o_ref):
    o_ref[...] = (mine_ref[...].astype(F32) + theirs_ref[...].astype(F32)).astype(o_ref.dtype)


def _pair_add(c_arr, g4, recv, name):
    _, _, hr, cols = g4.shape
    tr = min(hr, 256)
    grid_spec = pltpu.PrefetchScalarGridSpec(
        num_scalar_prefetch=1, grid=(N_CHIPS, hr // tr),
        in_specs=[pl.BlockSpec((None, None, tr, cols), lambda s, i, c: (s, c[0], i, 0)),
                  pl.BlockSpec((None, tr, cols), lambda s, i, c: (s, i, 0))],
        out_specs=pl.BlockSpec((None, tr, cols), lambda s, i, c: (s, i, 0)))
    return _pcall(
        _pair_add_body, name=name, grid_spec=grid_spec,
        out_shape=jax.ShapeDtypeStruct(recv.shape, recv.dtype),
        compiler_params=pltpu.CompilerParams(dimension_semantics=("parallel", "parallel")),
    )(c_arr, g4, recv)


def _pair_add_t(c_arr, g, recv, name):
    rows, hc = recv.shape
    grid_spec = pltpu.PrefetchScalarGridSpec(
        num_scalar_prefetch=1, grid=(rows // IN_TR,),
        in_specs=[pl.BlockSpec((IN_TR, hc), lambda i, c: (i, c[0])), pl.BlockSpec((IN_TR, hc), lambda i, c: (i, 0))],
        out_specs=pl.BlockSpec((IN_TR, hc), lambda i, c: (i, 0)))
    return _pcall(
        _pair_add_body, name=name, grid_spec=grid_spec,
        out_shape=jax.ShapeDtypeStruct(recv.shape, recv.dtype),
        compiler_params=pltpu.CompilerParams(dimension_semantics=("parallel",)),
    )(c_arr, g, recv)


def _sum4_body(me_ref, p_ref, l0, l1, l2, l3, o_ref):
    me = me_ref[0]
    t = [jnp.where(me == j, p_ref[...], l[...]).astype(F32) for j, l in enumerate((l0, l1, l2, l3))]
    o_ref[...] = ((t[0] + t[1]) + t[2]) + t[3]


def _sum4(me_arr, part, landed, name):
    _, hr, cols = part.shape
    tr = IN_TR if hr == IN_SHARD else min(hr, 256)

    def slot(j):
        return lambda i, me: (jnp.where(me[0] == j, (j + 1) % N_CHIPS, j), i, 0)

    grid_spec = pltpu.PrefetchScalarGridSpec(
        num_scalar_prefetch=1, grid=(hr // tr,),
        in_specs=[pl.BlockSpec((None, tr, cols), lambda i, me: (me[0], i, 0))]
        + [pl.BlockSpec((None, tr, cols), slot(j)) for j in range(N_CHIPS)],
        out_specs=pl.BlockSpec((tr, cols), lambda i, me: (i, 0)))
    return _pcall(
        _sum4_body, name=name, grid_spec=grid_spec,
        out_shape=jax.ShapeDtypeStruct((hr, cols), F32),
        compiler_params=pltpu.CompilerParams(dimension_semantics=("parallel",)),
    )(me_arr, part, landed, landed, landed, landed)


def _adamw(w, g, m, v):
    m = ADAM_B1 * m + (1.0 - ADAM_B1) * g
    v = ADAM_B2 * v + (1.0 - ADAM_B2) * (g * g)
    m_hat = m / (1.0 - ADAM_B1 ** ADAM_STEP)
    v_hat = v / (1.0 - ADAM_B2 ** ADAM_STEP)
    delta = -ADAM_LR * (m_hat / (jnp.sqrt(v_hat) + ADAM_EPS) + ADAM_WD * w)
    return delta, m, v


def _adamw_half_body(h_ref, w_ref, g_in_ref, m_ref, v_ref, *rest):
    g_ref, d_ref, nm_ref, nv_ref, done_ref = rest[-5:]
    done_ref[...] = jnp.zeros_like(done_ref)
    g = g_in_ref[...]
    g_ref[...] = g
    d, m, v = _adamw(w_ref[...], g, m_ref[...], v_ref[...])
    d_ref[...] = d
    nm_ref[...] = m
    nv_ref[...] = v


def _adamw_half(h_arr, w, g_half, m, v, prev, name):
    rows, cols = w.shape
    if g_half.shape[0] == rows:
        tr, nh = IN_TR, rows // IN_TR
        at_half = pl.BlockSpec((tr, cols // 2), lambda i, h: (i, h[0]))
        g_spec = pl.BlockSpec((tr, cols // 2), lambda i, h: (i, 0))
    else:
        tr = min(rows // 2, 128)
        nh = (rows // 2) // tr
        at_half = pl.BlockSpec((tr, cols), lambda i, h: (h[0] * nh + i, 0))
        g_spec = pl.BlockSpec((tr, cols), lambda i, h: (i, 0))
    grid_spec = pltpu.PrefetchScalarGridSpec(
        num_scalar_prefetch=1, grid=(nh,),
        in_specs=[at_half, g_spec, at_half, at_half] + [ANY] * len(prev),
        out_specs=[at_half] * 4 + [pl.BlockSpec((8, HEAD), lambda i, h: (0, 0))])
    return list(_pcall(
        _adamw_half_body, name=name, grid_spec=grid_spec,
        out_shape=[jax.ShapeDtypeStruct(w.shape, F32)] * 4 + [jax.ShapeDtypeStruct((8, HEAD), F32)],
        input_output_aliases={5 + k: k for k in range(len(prev))},
        compiler_params=pltpu.CompilerParams(dimension_semantics=("arbitrary",)),
    )(h_arr, w, g_half, m, v, *prev))


def _small_update_body(gath_ref, w_ref, m_ref, v_ref, g_ref, d_ref, nm_ref, nv_ref, loss_ref, *, n_gain):
    tot = gath_ref[0:1, :]
    for i in range(1, gath_ref.shape[0]):
        tot = tot + gath_ref[i:i + 1, :]
    g = tot[:, 0:n_gain]
    g_ref[...] = g
    d, m, v = _adamw(w_ref[...], g, m_ref[...], v_ref[...])
    d_ref[...] = d
    nm_ref[...] = m
    nv_ref[...] = v
    loss_ref[...] = (0.5 / D_MODEL) * jnp.sum(tot[:, n_gain:n_gain + HEAD], axis=1, keepdims=True) * jnp.ones((1, HEAD), F32)


def _small_update(gath, w, m, v):
    n_gain = w.shape[1]
    vm = pl.BlockSpec(memory_space=pltpu.VMEM)
    return _pcall(
        functools.partial(_small_update_body, n_gain=n_gain), name="gain_update",
        in_specs=[vm] * 4, out_specs=[vm] * 5,
        out_shape=[jax.ShapeDtypeStruct((1, n_gain), F32)] * 4 + [jax.ShapeDtypeStruct((1, HEAD), F32)],
    )(gath, w, m, v)


def kernel(x, positions, norm_attn_pre, norm_attn_post, w_in, q_latent_norm, kv_latent_norm, w_uq, w_ukv, w_out, norm_mlp_pre, norm_mlp_post, w_up, w_down, loss_target, m_norm_attn_pre, m_norm_attn_post, m_w_in, m_q_latent_norm, m_kv_latent_norm, m_w_uq, m_w_ukv, m_w_out, m_norm_mlp_pre, m_norm_mlp_post, m_w_up, m_w_down, v_norm_attn_pre, v_norm_attn_post, v_w_in, v_q_latent_norm, v_kv_latent_norm, v_w_uq, v_w_ukv, v_w_out, v_norm_mlp_pre, v_norm_mlp_post, v_w_up, v_w_down):
    T = x.shape[1]
    c_arr = lax.axis_index("c").astype(jnp.int32).reshape(1)
    me_arr = (2 * lax.axis_index("x") + lax.axis_index("y")).astype(jnp.int32).reshape(1)
    names = ["w_in", "w_uq", "w_ukv", "w_out", "w_up", "w_down"]

    transposed = lambda a: jnp.swapaxes(a, 1, 2)
    mats = [transposed(w_in)[0], w_uq[0], w_ukv[0], w_out[0], w_up[0], w_down[0]]
    me8_arr = (4 * lax.axis_index("x") + 2 * lax.axis_index("y") + lax.axis_index("c")).astype(jnp.int32).reshape(1)
    col_major = lambda g: jnp.transpose(g, (1, 0, 2)).reshape(g.shape[1], N_CHIPS * g.shape[2])
    cast = lambda a: a.astype(MXU_DTYPE)
    to_shards = lambda g: jnp.transpose(g.reshape(g.shape[0], N_CHIPS, g.shape[1] // N_CHIPS), (1, 0, 2))
    halved = lambda g: g.reshape(N_CHIPS, 2, g.shape[1] // 2, g.shape[2])
    empty = lambda a, shape=None: lax.empty(a.shape if shape is None else shape, a.dtype)

    sem_in, buf_in, going = _copy_start(_ag_descs, [_cast_place_t(me_arr, mats[0], "cast_w_in")], None,
                                        "weight_allgather_start_in", 3)
    placed = [_cast_place(me_arr, w, "cast_" + n, going) for w, n in zip(mats[1:], names[1:])]
    sem_att, buf_att, going = _copy_start(_ag_descs, placed[:3], going, "weight_allgather_start_attn", 9)
    sem_mlp, buf_mlp, started = _copy_start(_ag_descs, placed[3:], going, "weight_allgather_start_mlp", 6)

    going_on = {}

    def in_weights(after):
        (win_g,) = _ag_forward(_copy_wait(_ag_descs, sem_in, buf_in, after, "weight_allgather_wait_in"), "in")
        return cast(win_g)

    def attn_prefetch(after):
        landed = _copy_wait(_ag_descs, sem_att, buf_att, after, "weight_allgather_wait_attn")
        going_on["fw_attn"] = _copy_start(_fw_descs, landed, None, "weight_allgather_forward_start_attn", 9)
        return going_on["fw_attn"][-1][0:1, 0:1]

    def attn_weights(after):
        sems, bufs, _ = going_on["fw_attn"]
        wuq_g, wukv_g, wout_g = _copy_wait(_fw_descs, sems, bufs, after, "weight_allgather_forward_wait_attn")
        wuq_full = col_major(wuq_g).reshape(LORA, NH, HEAD + ROPE_B)
        w_uq_p = jnp.pad(wuq_full, ((0, 0), (0, 0), (0, QPAD - HEAD - ROPE_B))).reshape(LORA, NH * QPAD)
        return cast(w_uq_p), cast(col_major(wukv_g)), cast(wout_g.reshape(2 * A_W, D_MODEL))

    def mlp_prefetch(after):
        landed = _copy_wait(_ag_descs, sem_mlp, buf_mlp, after, "weight_allgather_wait_mlp")
        going_on["fw"] = _copy_start(_fw_descs, landed, None, "weight_allgather_forward_start_mlp", 6)
        return going_on["fw"][-1]

    def mlp_weights(after):
        sems, bufs, _ = going_on["fw"]
        wup_g, wdown_g = _copy_wait(_fw_descs, sems, bufs, after, "weight_allgather_forward_wait_mlp")
        return cast(wup_g), cast(wdown_g.reshape(D_FF, D_MODEL))

    def exchange_start(g4s, tag):
        lands = [empty(g, (g.shape[0],) + g.shape[2:]) for g in g4s]
        return _copy_start(_EX_DESCS, g4s + lands, None, "grad_pair_exchange_start_" + tag, len(g4s))

    def exchange_finish(started_ex, after, ns, tag):
        sems, arrs, _ = started_ex
        arrs = _copy_wait(_EX_DESCS, sems, arrs, after, "grad_pair_exchange_wait_" + tag)
        n = len(ns)
        return [_pair_add(c_arr, g4, r, "pair_add_" + nm) for g4, r, nm in zip(arrs[:n], arrs[n:], ns)]

    def scatter_start(parts, after, tag):
        return _copy_start(_sc_descs, parts + [empty(p) for p in parts], after, "grad_scatter_start_" + tag,
                           3 * len(parts))

    def scatter_finish(started_sc, after, tag):
        sems, arrs, _ = started_sc
        arrs = _copy_wait(_sc_descs, sems, arrs, after, "grad_scatter_wait_" + tag)
        return arrs[:len(arrs) // 2], arrs[len(arrs) // 2:]

    def down_grad_ready(gw_down):
        going_on["x_down"] = exchange_start([halved(gw_down.reshape(N_CHIPS, D_MODEL, D_MODEL))], "down")
        return going_on["x_down"][-1]

    def up_grad_ready(gw_up):
        going_on["x_up"] = exchange_start([halved(gw_up)], "up")
        parts = exchange_finish(going_on["x_down"], going_on["x_up"][-1], names[5:], "down")
        going_on["s_down"] = scatter_start(parts, started, "down")
        return going_on["s_down"][-1][0:1, 0:1]

    def attn_grads_ready(gw_out, gw_uq_p, gw_ukv):
        gw_uq = to_shards(gw_uq_p.reshape(LORA, NH, QPAD)[:, :, :HEAD + ROPE_B].reshape(LORA, NH * (HEAD + ROPE_B)))
        full4 = [halved(g) for g in (gw_uq, to_shards(gw_ukv), gw_out.reshape(N_CHIPS, LORA, D_MODEL))]
        x_attn = exchange_start(full4, "attn")
        parts_up = exchange_finish(going_on["x_up"], x_attn[-1], names[4:5], "up")
        parts = exchange_finish(x_attn, parts_up[0], names[1:4], "attn") + parts_up
        going_on["s_rest"] = scatter_start(parts, going_on["s_down"][-1], "attn_up")
        return going_on["s_rest"][-1][0:1, 0:1]

    dx, gw_proj, small = _local_step(
        x[0], positions[0].astype(F32).reshape(T, 1), loss_target[0],
        norm_attn_pre + started[0:1, 0:1], norm_attn_post, q_latent_norm, kv_latent_norm, norm_mlp_pre, norm_mlp_post,
        in_weights, attn_prefetch, attn_weights, mlp_prefetch, mlp_weights,
        down_grad_ready, up_grad_ready, attn_grads_ready)

    ms = [transposed(m_w_in)[0], m_w_uq[0], m_w_ukv[0], m_w_out[0], m_w_up[0], m_w_down[0]]
    vs = [transposed(v_w_in)[0], v_w_uq[0], v_w_ukv[0], v_w_out[0], v_w_up[0], v_w_down[0]]
    sib_arr = 1 - c_arr

    def chip_sums(parts, landed, lo, hi):
        return [_sum4(me_arr, p, l, "chip_sum_" + n) for p, l, n in zip(parts, landed, names[lo:hi])]

    def finish(halves, lo, hi, tag, after=None):
        sl = slice(lo, hi)
        n = len(halves)
        sems, arrs, _ = _copy_start(_SW_DESCS, halves + [empty(h) for h in halves], after,
                                    "grad_pair_swap_start_" + tag, n)
        own = [_adamw_half(c_arr, w, g, m, v, [], "adamw_own_" + nm)
               for w, g, m, v, nm in zip(mats[sl], arrs[:n], ms[sl], vs[sl], names[sl])]
        arrs = _copy_wait(_SW_DESCS, sems, arrs, own[-1][4], "grad_pair_swap_wait_" + tag)
        return [_adamw_half(sib_arr, w, g, m, v, prev[:4], "adamw_sib_" + nm)
                for w, g, m, v, prev, nm in zip(mats[sl], arrs[n:], ms[sl], vs[sl], own, names[sl])]

    sem_small, (gath,), small_going = _copy_start(
        _sm_descs, [_place_rows(me8_arr, small, N_DEV, "place_small")], None, "small_allgather_start", N_DEV - 1)
    sems, arrs, x_in_going = _copy_start(_EXT_DESCS, [gw_proj, lax.empty((IN_COLS, D_MODEL // 2), WIRE_DTYPE)], None,
                                         "grad_pair_exchange_start_in", 1)
    parts_rest, landed_rest = scatter_finish(going_on["s_rest"], x_in_going, "attn_up")
    parts_down, landed_down = scatter_finish(going_on["s_down"], landed_rest[0], "down")
    halves_rest = chip_sums(parts_rest + parts_down, landed_rest + landed_down, 1, 6)
    gw_proj, from_sib = _copy_wait(_EXT_DESCS, sems, arrs, [small_going] + halves_rest, "grad_pair_exchange_wait_in")
    part_in = _pair_add_t(c_arr, gw_proj, from_sib, "pair_add_w_in").reshape(N_CHIPS, IN_SHARD, D_MODEL // 2)
    s_in = scatter_start([part_in], None, "in")
    upd_rest = finish(halves_rest, 1, 6, "rest", after=s_in[-1])
    parts_in, landed_in = scatter_finish(s_in, upd_rest[-1][0], "in")
    upd = finish(chip_sums(parts_in, landed_in, 0, 1), 0, 1, "in") + upd_rest
    grads = [u[0] for u in upd]

    (gath,) = _copy_wait(_sm_descs, sem_small, [gath], grads[0], "small_allgather_wait")
    gains = [norm_attn_pre, norm_attn_post, q_latent_norm, kv_latent_norm, norm_mlp_pre, norm_mlp_post]
    gm = [m_norm_attn_pre, m_norm_attn_post, m_q_latent_norm, m_kv_latent_norm, m_norm_mlp_pre, m_norm_mlp_post]
    gv = [v_norm_attn_pre, v_norm_attn_post, v_q_latent_norm, v_kv_latent_norm, v_norm_mlp_pre, v_norm_mlp_post]
    cat = lambda xs: jnp.concatenate(xs, axis=1)
    g_s, d_s, m_s, v_s, loss_v = _small_update(gath, cat(gains), cat(gm), cat(gv))
    widths = [a.shape[1] for a in gains]
    offs = [sum(widths[:i]) for i in range(len(widths))]
    split = lambda a: [a[:, o:o + w] for o, w in zip(offs, widths)]
    g_gain, d_gain, m_gain, v_gain = split(g_s), split(d_s), split(m_s), split(v_s)

    def ordered(gain_list, mat_list):
        gl, ml = gain_list, [transposed(mat_list[0][None])] + [a[None] for a in mat_list[1:]]
        return [gl[0], gl[1], ml[0], gl[2], gl[3], ml[1], ml[2], ml[3], gl[4], gl[5], ml[4], ml[5]]

    loss = loss_v[0, 0]
    return (loss, dx[None],
            *ordered(g_gain, grads),
            *ordered(d_gain, [u[1] for u in upd]),
            *ordered(m_gain, [u[2] for u in upd]),
            *ordered(v_gain, [u[3] for u in upd]))
```

```python
import functools

import jax
import jax.numpy as jnp
from jax import lax
from jax.experimental import pallas as pl
from jax.experimental.pallas import tpu as pltpu

F32 = jnp.float32
BF16 = jnp.bfloat16
MXU_DTYPE = jnp.bfloat16
WIRE_DTYPE = jnp.bfloat16

D_MODEL = 2048
HEAD = 128
NH = 8
A_W = NH * HEAD
LORA = 512
ROPE_B = 64
QPAD = 256
MAIN_COLS = 3 * A_W + 2 * LORA
IN_COLS = MAIN_COLS + ROPE_B
PROJ_COLS = MAIN_COLS + HEAD
PROJ_TILE = PROJ_COLS // 3
IN_SHARD = 1040
IN_TR = 208
D_FF = 4 * D_MODEL
DIL = (1, 4, 16)
ROT_A = 32
ROPE_THETA = 500000.0
EPS = 1e-6
NEG = -1e30
N_CHIPS = 4
N_DEV = 8

ADAM_LR = 0.001
ADAM_B1 = 0.9
ADAM_B2 = 0.999
ADAM_EPS = 1e-08
ADAM_WD = 0.01
ADAM_STEP = 10

MESH = pl.DeviceIdType.MESH
ANY = pl.BlockSpec(memory_space=pl.ANY)


def _pcall(body, **kw):
    return pl.pallas_call(body, **kw)


_DIMS = {
    "nn": (((1,), (0,)), ((), ())),
    "nt": (((1,), (1,)), ((), ())),
    "tn": (((0,), (0,)), ((), ())),
}


def _mm_body(*refs, dims, nk, epi, n_extra, n_after, n_out):
    a_ref, b_ref = refs[0], refs[1]
    extra = refs[2:2 + n_extra]
    outs = refs[2 + n_extra + n_after:2 + n_extra + n_after + n_out]
    part = lax.dot_general(a_ref[...], b_ref[...], _DIMS[dims], preferred_element_type=F32)

    def finish(acc):
        res = epi(acc, *[r[...] for r in extra]) if epi is not None else (acc,)
        for o_ref, o in zip(outs, res):
            o_ref[...] = o.astype(o_ref.dtype)

    if nk == 1:
        finish(part)
        return
    acc_ref = refs[-1]
    k = pl.program_id(2)

    @pl.when(k == 0)
    def _():
        acc_ref[...] = part

    @pl.when(k > 0)
    def _():
        acc_ref[...] += part

    @pl.when(k == nk - 1)
    def _():
        finish(acc_ref[...])


def _matmul(a, b, *, dims, out_dtypes, tm, tn, tk, name, epi=None, extras=(), row_extras=(), b_outer=False,
            b_shards=0, out_shards=0, after=None):
    if b_shards:
        assert dims in ("nn", "nt") and b.shape[0] == b_shards
        b2 = (b.shape[1], b_shards * b.shape[2])
    else:
        b2 = b.shape
    if dims == "nn":
        (M, K), (K2, N) = a.shape, b2
    elif dims == "nt":
        (M, K), (N, K2) = a.shape, b2
    else:
        (K, M), (K2, N) = a.shape, b2
    assert K == K2, (a.shape, b.shape, dims)
    tm, tn, tk = min(tm, M), min(tn, N), min(tk, K)
    assert M % tm == 0 and N % tn == 0 and K % tk == 0, (name, M, N, K, tm, tn, tk)
    nk = K // tk

    def at(f):
        if b_outer:
            return lambda j, i, k: f(i, j, k)
        return f

    a_spec = {"nn": pl.BlockSpec((tm, tk), at(lambda i, j, k: (i, k))),
              "nt": pl.BlockSpec((tm, tk), at(lambda i, j, k: (i, k))),
              "tn": pl.BlockSpec((tk, tm), at(lambda i, j, k: (k, i)))}[dims]
    b_spec = {"nn": pl.BlockSpec((tk, tn), at(lambda i, j, k: (k, j))),
              "nt": pl.BlockSpec((tn, tk), at(lambda i, j, k: (j, k))),
              "tn": pl.BlockSpec((tk, tn), at(lambda i, j, k: (k, j)))}[dims]
    if b_shards:
        per = b.shape[2] // (tn if dims == "nn" else tk)
        assert per >= 1 and b.shape[2] % (tn if dims == "nn" else tk) == 0
        b_spec = {"nn": pl.BlockSpec((None, tk, tn), at(lambda i, j, k: (j // per, k, j % per))),
                  "nt": pl.BlockSpec((None, tn, tk), at(lambda i, j, k: (k // per, j, k % per)))}[dims]
    o_spec = pl.BlockSpec((tm, tn), at(lambda i, j, k: (i, j)))
    o_shape = (M, N)
    if out_shards:
        assert not extras and N % out_shards == 0 and (N // out_shards) % tn == 0
        o_per = (N // out_shards) // tn
        o_spec = pl.BlockSpec((None, tm, tn), at(lambda i, j, k: (j // o_per, i, j % o_per)))
        o_shape = (out_shards, M, N // out_shards)
    r_specs = [pl.BlockSpec((tm, r.shape[1]), at(lambda i, j, k: (i, 0))) for r in row_extras]
    after = [] if after is None else [after]
    body = functools.partial(_mm_body, dims=dims, nk=nk, epi=epi, n_extra=len(extras) + len(row_extras),
                             n_after=len(after), n_out=len(out_dtypes))
    res = _pcall(
        body, name=name,
        grid=(N // tn, M // tm, nk) if b_outer else (M // tm, N // tn, nk),
        in_specs=[a_spec, b_spec] + [o_spec] * len(extras) + r_specs + [ANY] * len(after),
        out_specs=[o_spec] * len(out_dtypes),
        out_shape=[jax.ShapeDtypeStruct(o_shape, dt) for dt in out_dtypes],
        scratch_shapes=[pltpu.VMEM((tm, tn), F32)] if nk > 1 else [],
        compiler_params=pltpu.CompilerParams(
            dimension_semantics=("parallel", "parallel", "arbitrary")),
    )(a, b, *extras, *row_extras, *after)
    return list(res)


def _rowwise(body, row_ins, vec_ins, row_outs, acc_outs, *, tr, name):
    T = row_ins[0].shape[0]
    tr = min(tr, T)
    assert T % tr == 0
    in_specs = [pl.BlockSpec((tr, a.shape[1]), lambda i: (i, 0)) for a in row_ins]
    in_specs += [pl.BlockSpec(a.shape, lambda i: (0, 0)) for a in vec_ins]
    out_specs = [pl.BlockSpec((tr, w), lambda i: (i, 0)) for (w, _) in row_outs]
    out_specs += [pl.BlockSpec(s, lambda i: (0, 0)) for s in acc_outs]
    out_shape = [jax.ShapeDtypeStruct((T, w), dt) for (w, dt) in row_outs]
    out_shape += [jax.ShapeDtypeStruct(s, F32) for s in acc_outs]
    sem = "arbitrary" if acc_outs else "parallel"
    return list(_pcall(
        body, name=name, grid=(T // tr,), in_specs=in_specs, out_specs=out_specs,
        out_shape=out_shape,
        compiler_params=pltpu.CompilerParams(dimension_semantics=(sem,)),
    )(*row_ins, *vec_ins))


def _rstd(x):
    return lax.rsqrt(jnp.mean(x * x, axis=-1, keepdims=True) + EPS)


def _rms_bwd(x, rstd, dyg):
    xh = x * rstd
    return rstd * (dyg - xh * jnp.mean(dyg * xh, axis=-1, keepdims=True)), xh


def _fold8(v):
    r, w = v.shape
    return jnp.sum(v.reshape(r // 8, 8, w), axis=0)


def _acc(ref, val):
    first = pl.program_id(0) == 0

    @pl.when(first)
    def _():
        ref[...] = val

    @pl.when(jnp.logical_not(first))
    def _():
        ref[...] += val


def _rope(x, c, sa, sb, half):
    return x * c + pltpu.roll(x, HEAD - half, 1) * sa + pltpu.roll(x, half, 1) * sb


def _rope_t(dy, c, sa, sb, half):
    return dy * c - pltpu.roll(dy, HEAD - half, 1) * sa - pltpu.roll(dy, half, 1) * sb


def _rope_tab_body(pos_ref, inv_ref, ca, saa, sab, cb, sba, sbb):
    pos = pos_ref[...]
    lane = lax.broadcasted_iota(jnp.int32, (pos.shape[0], HEAD), 1)
    ang_a = pos * inv_ref[0:1, :]
    ang_b = pos * inv_ref[1:2, :]
    c, s = jnp.cos(ang_a), jnp.sin(ang_a)
    ha = ROT_A // 2
    ca[...] = jnp.where(lane < ROT_A, c, 1.0)
    saa[...] = jnp.where(lane < ha, -s, 0.0)
    sab[...] = jnp.where((lane >= ha) & (lane < ROT_A), s, 0.0)
    c, s = jnp.cos(ang_b), jnp.sin(ang_b)
    hb = ROPE_B // 2
    cb[...] = jnp.where(lane < ROPE_B, c, 1.0)
    sba[...] = jnp.where(lane < hb, -s, 0.0)
    sbb[...] = jnp.where((lane >= hb) & (lane < ROPE_B), s, 0.0)


def _rms_fwd_body(x_ref, g_ref, h_ref):
    x = x_ref[...]
    h_ref[...] = ((x * _rstd(x)) * g_ref[...]).astype(h_ref.dtype)


def _postproj_body(p_ref, ca, saa, sab, cb, sba, sbb, gq_ref, gkv_ref,
                   q_ref, k_ref, v_ref, cqn_ref, ckvn_ref, krope_ref):
    c, sa, sb = ca[...], saa[...], sab[...]
    for h in range(NH):
        lo = h * HEAD
        q_ref[:, lo:lo + HEAD] = _rope(p_ref[:, lo:lo + HEAD], c, sa, sb, ROT_A // 2).astype(q_ref.dtype)
        k_ref[:, lo:lo + HEAD] = _rope(p_ref[:, A_W + lo:A_W + lo + HEAD], c, sa, sb, ROT_A // 2).astype(k_ref.dtype)
    v_ref[...] = p_ref[:, 2 * A_W:3 * A_W].astype(v_ref.dtype)
    cq = p_ref[:, 3 * A_W:3 * A_W + LORA]
    cqn_ref[...] = ((cq * _rstd(cq)) * gq_ref[...]).astype(cqn_ref.dtype)
    ckv = p_ref[:, 3 * A_W + LORA:MAIN_COLS]
    ckvn_ref[...] = ((ckv * _rstd(ckv)) * gkv_ref[...]).astype(ckvn_ref.dtype)
    krope_ref[...] = _rope(p_ref[:, MAIN_COLS:PROJ_COLS], cb[...], sba[...], sbb[...], ROPE_B // 2).astype(krope_ref.dtype)


def _mid_body(x_ref, o_ref, g2_ref, g3_ref, x1_ref, h2_ref):
    o = o_ref[...]
    x1 = x_ref[...] + (o * _rstd(o)) * g2_ref[...]
    x1_ref[...] = x1
    h2_ref[...] = ((x1 * _rstd(x1)) * g3_ref[...]).astype(h2_ref.dtype)


def _loss_body(x1_ref, d_ref, t_ref, g4_ref, dy_ref, dd_ref, loss_ref, dg4_ref):
    d = d_ref[...]
    rstd = _rstd(d)
    y = x1_ref[...] + (d * rstd) * g4_ref[...]
    e = y - t_ref[...]
    dy = e * (1.0 / D_MODEL)
    dy_ref[...] = dy
    dd, dh = _rms_bwd(d, rstd, dy * g4_ref[...])
    dd_ref[...] = dd.astype(dd_ref.dtype)
    _acc(dg4_ref, _fold8(dy * dh))
    e8 = _fold8(e * e)
    l = e8[:, 0:HEAD]
    for j in range(1, D_MODEL // HEAD):
        l = l + e8[:, j * HEAD:(j + 1) * HEAD]
    _acc(loss_ref, l)


def _bmid_body(dy_ref, dh2_ref, x1_ref, o_ref, g2_ref, g3_ref, dx1_ref, do_ref, dg3_ref, dg2_ref):
    x1 = x1_ref[...]
    dh2 = dh2_ref[...]
    dn, x1h = _rms_bwd(x1, _rstd(x1), dh2 * g3_ref[...])
    dx1 = dy_ref[...] + dn
    dx1_ref[...] = dx1
    _acc(dg3_ref, _fold8(dh2 * x1h))
    o = o_ref[...]
    do, oh = _rms_bwd(o, _rstd(o), dx1 * g2_ref[...])
    do_ref[...] = do.astype(do_ref.dtype)
    _acc(dg2_ref, _fold8(dx1 * oh))


def _dproj_body(dq_ref, dk_ref, dv_ref, dcq_ref, dckv_ref, p_ref, dkr_ref,
                ca, saa, sab, cb, sba, sbb, gq_ref, gkv_ref,
                dp_ref, dgq_ref, dgkv_ref):
    c, sa, sb = ca[...], saa[...], sab[...]
    for h in range(NH):
        lo = h * HEAD
        dp_ref[:, lo:lo + HEAD] = _rope_t(dq_ref[:, lo:lo + HEAD], c, sa, sb, ROT_A // 2).astype(dp_ref.dtype)
        dp_ref[:, A_W + lo:A_W + lo + HEAD] = _rope_t(dk_ref[:, lo:lo + HEAD], c, sa, sb, ROT_A // 2).astype(dp_ref.dtype)
    dp_ref[:, 2 * A_W:3 * A_W] = dv_ref[...].astype(dp_ref.dtype)
    cq = p_ref[:, 3 * A_W:3 * A_W + LORA]
    dcqn = dcq_ref[...]
    dcq, cqh = _rms_bwd(cq, _rstd(cq), dcqn * gq_ref[...])
    dp_ref[:, 3 * A_W:3 * A_W + LORA] = dcq.astype(dp_ref.dtype)
    _acc(dgq_ref, _fold8(dcqn * cqh))
    ckv = p_ref[:, 3 * A_W + LORA:MAIN_COLS]
    dckvn = dckv_ref[...]
    dckv, ckvh = _rms_bwd(ckv, _rstd(ckv), dckvn * gkv_ref[...])
    dp_ref[:, 3 * A_W + LORA:MAIN_COLS] = dckv.astype(dp_ref.dtype)
    _acc(dgkv_ref, _fold8(dckvn * ckvh))
    dkr = dkr_ref[:, 0:HEAD]
    for h in range(1, NH):
        dkr = dkr + dkr_ref[:, h * HEAD:(h + 1) * HEAD]
    dp_ref[:, MAIN_COLS:PROJ_COLS] = _rope_t(dkr, cb[...], sba[...], sbb[...], ROPE_B // 2).astype(dp_ref.dtype)


def _bin_body(dx1_ref, dh_ref, x_ref, g1_ref, dx_ref, dg1_ref):
    x = x_ref[...]
    dh = dh_ref[...]
    dn, xh = _rms_bwd(x, _rstd(x), dh * g1_ref[...])
    dx_ref[...] = dx1_ref[...] + dn
    _acc(dg1_ref, _fold8(dh * xh))


def _dot_nt(a, b):
    return lax.dot_general(a, b, _DIMS["nt"], preferred_element_type=F32)


def _dot_tn(a, b):
    return lax.dot_general(a, b, _DIMS["tn"], preferred_element_type=F32)


def _dot_nn(a, b):
    return jnp.dot(a, b, preferred_element_type=F32)


DIL_SCALE = HEAD ** -0.5
DIL_CHUNK = 256


def _dil_rows(t, d, chain=0):
    if chain:
        r = t >> (chain.bit_length() - 1)
        n = t & (chain - 1)
    else:
        r = t & (d - 1)
        n = t >> (d.bit_length() - 1)
    start = r + n * (HEAD * d)
    has_prev = n > 0
    pstart = jnp.where(has_prev, start - HEAD * d, start)
    if d == 1:
        return pl.ds(pl.multiple_of(start, HEAD), HEAD), pl.ds(pl.multiple_of(pstart, HEAD), HEAD), has_prev
    return pl.ds(start, HEAD, stride=d), pl.ds(pstart, HEAD, stride=d), has_prev


def _dil_band():
    row = lax.broadcasted_iota(jnp.int32, (HEAD, 2 * HEAD), 0)
    col = lax.broadcasted_iota(jnp.int32, (HEAD, 2 * HEAD), 1)
    return (col >= row) & (col <= row + HEAD), col >= HEAD


def _dil_fwd_body(q_ref, k_ref, v_ref, a_ref, lse_ref, o1, o2, o3, l1, l2, l3, *, nt, unroll):
    band, is_cur = _dil_band()
    for d, o_sc, l_sc in zip(DIL, (o1, o2, o3), (l1, l2, l3)):
        chain = nt // d
        linked = min(chain, unroll)
        assert unroll % linked == 0

        def tiles(g, carry, d=d, o_sc=o_sc, l_sc=l_sc, chain=chain, linked=linked):
            staged = []
            for u in range(unroll):
                rows, prows, has_prev = _dil_rows(g * unroll + u, d, chain)
                q = q_ref[rows, :].astype(MXU_DTYPE)
                kc = k_ref[rows, :].astype(MXU_DTYPE)
                vc = v_ref[rows, :].astype(MXU_DTYPE)
                if u % linked:
                    kp, vp = staged[-1][2], staged[-1][3]
                else:
                    kp = k_ref[prows, :].astype(MXU_DTYPE)
                    vp = v_ref[prows, :].astype(MXU_DTYPE)
                kk = jnp.concatenate([kp, kc], axis=0)
                staged.append((rows, has_prev, kc, vc, jnp.concatenate([vp, vc], axis=0), _dot_nt(q, kk)))
            for rows, has_prev, _, _, vv, s in staged:
                ok = band & (is_cur | has_prev)
                s = jnp.where(ok, s * DIL_SCALE, NEG)
                m = jnp.max(s, axis=1, keepdims=True)
                p = jnp.exp(s - m)
                den = jnp.sum(p, axis=1, keepdims=True)
                o_sc[rows, :] = _dot_nn((p / den).astype(MXU_DTYPE), vv)
                l_sc[rows, :] = jnp.broadcast_to(m + jnp.log(den), (HEAD, HEAD))
            return carry

        lax.fori_loop(0, nt // unroll, tiles, 0)

    def merge(i, carry):
        rs = pl.ds(pl.multiple_of(i * DIL_CHUNK, DIL_CHUNK), DIL_CHUNK)
        la, lb, lc = l1[rs, :], l2[rs, :], l3[rs, :]
        m = jnp.maximum(jnp.maximum(la, lb), lc)
        wa, wb, wc = jnp.exp(la - m), jnp.exp(lb - m), jnp.exp(lc - m)
        den = wa + wb + wc
        a = (wa / den) * o1[rs, :] + (wb / den) * o2[rs, :] + (wc / den) * o3[rs, :]
        a_ref[rs, :] = a.astype(a_ref.dtype)
        lse_ref[rs, :] = m + jnp.log(den)
        return carry

    lax.fori_loop(0, q_ref.shape[0] // DIL_CHUNK, merge, 0)


def _dil_fwd(q, k, v):
    T = q.shape[0]
    spec = pl.BlockSpec((T, HEAD), lambda h: (0, h))
    return _pcall(
        functools.partial(_dil_fwd_body, nt=T // HEAD, unroll=16), name="dil_fwd",
        grid=(NH,), in_specs=[spec] * 3, out_specs=[spec] * 2,
        out_shape=[jax.ShapeDtypeStruct((T, 2 * A_W), MXU_DTYPE), jax.ShapeDtypeStruct((T, A_W), F32)],
        scratch_shapes=[pltpu.VMEM((T, HEAD), F32)] * 6,
        compiler_params=pltpu.CompilerParams(dimension_semantics=("parallel",)),
    )(q, k, v)


def _dil_bwd_body(q_ref, k_ref, v_ref, do_ref, a_ref, lse_ref, dq_ref, dk_ref, dv_ref, dl_sc, *, nt, unroll):
    band, is_cur = _dil_band()

    def prep(i, carry):
        rs = pl.ds(pl.multiple_of(i * DIL_CHUNK, DIL_CHUNK), DIL_CHUNK)
        dl = jnp.sum(do_ref[rs, :] * a_ref[rs, :].astype(F32), axis=1, keepdims=True)
        dl_sc[rs, :] = jnp.broadcast_to(dl, (DIL_CHUNK, HEAD))
        zero = jnp.zeros((DIL_CHUNK, HEAD), F32)
        dq_ref[rs, :] = zero
        dk_ref[rs, :] = zero
        dv_ref[rs, :] = zero
        return carry

    lax.fori_loop(0, q_ref.shape[0] // DIL_CHUNK, prep, 0)

    for d in DIL:
        chain = nt // d
        linked = min(chain, unroll)
        assert unroll % linked == 0

        def tiles(g, carry, d=d, chain=chain, linked=linked):
            staged = []
            for u in range(unroll):
                rows, prows, has_prev = _dil_rows(g * unroll + u, d, chain)
                q = q_ref[rows, :].astype(MXU_DTYPE)
                do = do_ref[rows, :].astype(MXU_DTYPE)
                kc = k_ref[rows, :].astype(MXU_DTYPE)
                vc = v_ref[rows, :].astype(MXU_DTYPE)
                if u % linked:
                    kp, vp = staged[-1][5], staged[-1][6]
                else:
                    kp = k_ref[prows, :].astype(MXU_DTYPE)
                    vp = v_ref[prows, :].astype(MXU_DTYPE)
                kk = jnp.concatenate([kp, kc], axis=0)
                vv = jnp.concatenate([vp, vc], axis=0)
                staged.append((rows, prows, has_prev, q, do, kc, vc, kk, _dot_nt(q, kk), _dot_nt(do, vv)))

            def add_own(own):
                rows, dk_own, dv_own = own
                dk_ref[rows, :] += dk_own
                dv_ref[rows, :] += dv_own

            own = None
            for u, (rows, prows, has_prev, q, do, _, _, kk, s, dp) in enumerate(staged):
                lse = lse_ref[rows, :]
                dl = dl_sc[rows, :]
                ok = band & (is_cur | has_prev)
                p = jnp.where(ok, jnp.exp(s * DIL_SCALE - jnp.concatenate([lse, lse], axis=1)), 0.0)
                ds = (p * (dp - jnp.concatenate([dl, dl], axis=1))).astype(MXU_DTYPE)
                dq_ref[rows, :] += _dot_nn(ds, kk) * DIL_SCALE
                dkk = _dot_tn(ds, q) * DIL_SCALE
                dvv = _dot_tn(p.astype(MXU_DTYPE), do)
                if u % linked:
                    add_own((own[0], own[1] + dkk[:HEAD, :], own[2] + dvv[:HEAD, :]))
                else:
                    if own is not None:
                        add_own(own)
                    dk_ref[prows, :] += dkk[:HEAD, :]
                    dv_ref[prows, :] += dvv[:HEAD, :]
                own = (rows, dkk[HEAD:, :], dvv[HEAD:, :])
            add_own(own)
            return carry

        lax.fori_loop(0, nt // unroll, tiles, 0)


def _dil_bwd(q, k, v, dmix, mixed, lse):
    T = q.shape[0]
    spec = pl.BlockSpec((T, HEAD), lambda h: (0, h))
    return _pcall(
        functools.partial(_dil_bwd_body, nt=T // HEAD, unroll=8), name="dil_bwd",
        grid=(NH,), in_specs=[spec] * 6, out_specs=[spec] * 3,
        out_shape=[jax.ShapeDtypeStruct((T, A_W), F32)] * 3,
        scratch_shapes=[pltpu.VMEM((T, HEAD), F32)],
        compiler_params=pltpu.CompilerParams(dimension_semantics=("parallel",)),
    )(q, k, v, dmix, mixed, lse)


MLA_SCALE = (HEAD + ROPE_B) ** -0.5
LOG2E = 1.4426950408889634
MLA_QSCALE = MLA_SCALE * LOG2E
MLA_T = 512
MLA_HP = 4


def _tri(t):
    row = lax.broadcasted_iota(jnp.int32, (t, t), 0)
    col = lax.broadcasted_iota(jnp.int32, (t, t), 1)
    return col <= row


def _lanes(x, n):
    return jnp.tile(x, (1, n // HEAD))


def _mla_fwd_body(q_ref, kv_ref, kr_ref, mixed_ref, o_ref, lse_ref, m_sc, l_sc, acc_sc, *, t, hp):
    del mixed_ref
    qi = pl.program_id(1)
    m_sc[...] = jnp.full(m_sc.shape, NEG, F32)
    l_sc[...] = jnp.zeros(l_sc.shape, F32)
    acc_sc[...] = jnp.zeros(acc_sc.shape, F32)

    def step(j, masked):
        ks = pl.ds(pl.multiple_of(j * t, t), t)
        kr = kr_ref[ks, :]
        logits = []
        for hh in range(hp):
            kcat = jnp.concatenate([kv_ref[ks, 2 * hh * HEAD:(2 * hh + 1) * HEAD], kr], axis=1)
            logits.append(_dot_nt(q_ref[:, hh * QPAD:(hh + 1) * QPAD], kcat))
        for hh in range(hp):
            s = logits[hh]
            if masked:
                s = jnp.where(_tri(t), s, NEG)
            m_prev = m_sc[hh]
            m_new = jnp.maximum(m_prev, jnp.max(s, axis=1, keepdims=True))
            alpha = jnp.exp2(m_prev - m_new)
            p = jnp.exp2(s - _lanes(m_new, t))
            l_sc[hh] = alpha * l_sc[hh] + jnp.sum(p, axis=1, keepdims=True)
            acc_sc[hh] = alpha * acc_sc[hh] + _dot_nn(p.astype(MXU_DTYPE), kv_ref[ks, (2 * hh + 1) * HEAD:(2 * hh + 2) * HEAD])
            m_sc[hh] = m_new

    def off_diag(j, carry):
        step(j, False)
        return carry

    lax.fori_loop(0, qi, off_diag, 0)
    step(qi, True)
    for hh in range(hp):
        l = l_sc[hh]
        o_ref[:, hh * HEAD:(hh + 1) * HEAD] = (acc_sc[hh] / l).astype(o_ref.dtype)
        lse_ref[:, hh * HEAD:(hh + 1) * HEAD] = m_sc[hh] + jnp.log2(l)


def _mla_fwd(qf, kv, kr, mixed):
    T = qf.shape[0]
    t, hp = min(MLA_T, T), MLA_HP
    ng = NH // hp
    return _pcall(
        functools.partial(_mla_fwd_body, t=t, hp=hp), name="mla_fwd",
        grid=(ng, T // t),
        in_specs=[pl.BlockSpec((t, hp * QPAD), lambda g, i: (i, g)),
                  pl.BlockSpec((T, hp * 2 * HEAD), lambda g, i: (0, g)),
                  pl.BlockSpec((T, HEAD), lambda g, i: (0, 0)), ANY],
        out_specs=[pl.BlockSpec((t, hp * HEAD), lambda g, i: (i, ng + g)),
                   pl.BlockSpec((t, hp * HEAD), lambda g, i: (i, g))],
        out_shape=[jax.ShapeDtypeStruct(mixed.shape, mixed.dtype), jax.ShapeDtypeStruct((T, A_W), F32)],
        input_output_aliases={3: 0},
        scratch_shapes=[pltpu.VMEM((hp, t, HEAD), F32)] * 3,
        compiler_params=pltpu.CompilerParams(dimension_semantics=("parallel", "parallel")),
    )(qf, kv, kr, mixed)


def _mla_bwd_body(q_ref, kn_ref, kr_ref, v_ref, do_ref, o_ref, lse_ref, cb, sba, sbb,
                  dq_ref, dkv_ref, dkr_ref, dq_sc, dl_sc, dk_sc, dv_sc, *, t):
    ki = pl.program_id(1)
    nq = q_ref.shape[0] // t

    @pl.when(ki == 0)
    def _():
        def prep(i, carry):
            rs = pl.ds(pl.multiple_of(i * t, t), t)
            dl = jnp.sum(do_ref[rs, :] * o_ref[rs, :].astype(F32), axis=1, keepdims=True)
            dl_sc[rs, :] = jnp.broadcast_to(dl, (t, HEAD))
            dq_sc[rs, :] = jnp.zeros((t, QPAD), F32)
            return carry
        lax.fori_loop(0, nq, prep, 0)

    kcat = jnp.concatenate([kn_ref[...], kr_ref[...]], axis=1)
    v = v_ref[...]
    dk_sc[...] = jnp.zeros(dk_sc.shape, F32)
    dv_sc[...] = jnp.zeros(dv_sc.shape, F32)

    def steps(blocks):
        staged = []
        for i, masked in blocks:
            qs = pl.ds(pl.multiple_of(i * t, t), t)
            q = q_ref[qs, :]
            do = do_ref[qs, :].astype(MXU_DTYPE)
            staged.append((qs, q, do, _dot_nt(q, kcat), _dot_nt(do, v), masked))
        for qs, q, do, s, dp, masked in staged:
            p = jnp.exp2(s - _lanes(lse_ref[qs, :], t))
            if masked:
                p = jnp.where(_tri(t), p, 0.0)
            ds = (p * (dp - _lanes(dl_sc[qs, :], t))).astype(MXU_DTYPE)
            dv_sc[...] += _dot_tn(p.astype(MXU_DTYPE), do)
            dk_sc[...] += _dot_tn(ds, q)
            dq_sc[qs, :] += _dot_nn(ds, kcat) * MLA_SCALE

    n_blocks = nq - ki

    @pl.when(n_blocks == 1)
    def _():
        steps([(ki, True)])

    @pl.when(n_blocks > 1)
    def _():
        steps([(ki, True), (ki + 1, False)])

    def pair(j, carry):
        steps([(ki + 2 * j, False), (ki + 2 * j + 1, False)])
        return carry

    lax.fori_loop(1, n_blocks // 2, pair, 0)

    @pl.when((n_blocks > 1) & (n_blocks % 2 == 1))
    def _():
        steps([(nq - 1, False)])

    dk = dk_sc[...] * (1.0 / LOG2E)
    dkv_ref[:, 0:HEAD] = dk[:, 0:HEAD].astype(dkv_ref.dtype)
    dkr_ref[...] = dk[:, HEAD:QPAD]
    dkv_ref[:, HEAD:] = dv_sc[...].astype(dkv_ref.dtype)

    @pl.when(ki == nq - 1)
    def _():
        def emit(i, carry):
            rs = pl.ds(pl.multiple_of(i * t, t), t)
            dq_ref[rs, 0:HEAD] = dq_sc[rs, 0:HEAD].astype(dq_ref.dtype)
            dq_ref[rs, HEAD:QPAD] = _rope_t(dq_sc[rs, HEAD:QPAD], cb[rs, :], sba[rs, :], sbb[rs, :],
                                            ROPE_B // 2).astype(dq_ref.dtype)
            return carry
        lax.fori_loop(0, nq, emit, 0)


def _mla_bwd(qf, kv, kr, dmix, mixed, lse, tabs_b):
    T = qf.shape[0]
    t = min(MLA_T, T)
    head = lambda h, j: (0, h)
    b_half = lambda h, j: (0, NH + h)
    kblk = pl.BlockSpec((t, HEAD), lambda h, j: (j, h))
    return _pcall(
        functools.partial(_mla_bwd_body, t=t), name="mla_bwd",
        grid=(NH, T // t),
        in_specs=[pl.BlockSpec((T, QPAD), head), pl.BlockSpec((t, HEAD), lambda h, j: (j, 2 * h)),
                  pl.BlockSpec((t, HEAD), lambda h, j: (j, 0)),
                  pl.BlockSpec((t, HEAD), lambda h, j: (j, 2 * h + 1)),
                  pl.BlockSpec((T, HEAD), b_half), pl.BlockSpec((T, HEAD), b_half),
                  pl.BlockSpec((T, HEAD), head)] + [pl.BlockSpec((T, HEAD), lambda h, j: (0, 0))] * 3,
        out_specs=[pl.BlockSpec((T, QPAD), head), pl.BlockSpec((t, 2 * HEAD), lambda h, j: (j, h)), kblk],
        out_shape=[jax.ShapeDtypeStruct((T, NH * QPAD), MXU_DTYPE), jax.ShapeDtypeStruct((T, 2 * A_W), MXU_DTYPE),
                   jax.ShapeDtypeStruct((T, A_W), F32)],
        scratch_shapes=[pltpu.VMEM((T, QPAD), F32), pltpu.VMEM((T, HEAD), F32), pltpu.VMEM((t, QPAD), F32),
                        pltpu.VMEM((t, HEAD), F32)],
        compiler_params=pltpu.CompilerParams(dimension_semantics=("parallel", "arbitrary")),
    )(qf, kv, kr, kv, dmix, mixed, lse, *tabs_b)


def _local_step(x, pos, target, g1, g2, gq, gkv, g3, g4,
                in_weights, attn_prefetch, attn_weights, mlp_prefetch, mlp_weights,
                down_grad_ready, up_grad_ready, attn_grads_ready):
    T = x.shape[0]
    TR = 256
    mm = functools.partial(_matmul, tm=2048, tn=1024, tk=2048, b_outer=True)
    mm_k = functools.partial(_matmul, tm=1024, tn=1024, tk=2048)
    mm_g = functools.partial(_matmul, tm=1024, tn=1024, tk=4096, b_outer=True)

    inv_a = ROPE_THETA ** (-jnp.arange(0, ROT_A, 2, dtype=F32) / ROT_A)
    inv_b = ROPE_THETA ** (-jnp.arange(0, ROPE_B, 2, dtype=F32) / ROPE_B)
    inv = jnp.stack([jnp.concatenate([inv_a, inv_a, jnp.zeros((HEAD - ROT_A,), F32)]),
                     jnp.concatenate([inv_b, inv_b, jnp.zeros((HEAD - ROPE_B,), F32)])])
    inv = jnp.concatenate([inv, jnp.zeros((6, HEAD), F32)], axis=0)
    tabs = _rowwise(_rope_tab_body, [pos], [inv], [(HEAD, F32)] * 6, [], tr=512, name="rope_tables")

    (h,) = _rowwise(_rms_fwd_body, [x], [g1], [(D_MODEL, MXU_DTYPE)], [], tr=TR, name="rms_in")
    w_proj = in_weights([h, tabs[0]])
    (proj,) = mm(h, w_proj, dims="nt", out_dtypes=[F32], tm=1024, tn=PROJ_TILE, name="proj_in")
    gq = gq + attn_prefetch(proj)
    q, k, v, cqn, ckvn, krope = _rowwise(
        _postproj_body, [proj] + tabs, [gq, gkv],
        [(A_W, F32)] * 3 + [(LORA, MXU_DTYPE)] * 2 + [(HEAD, MXU_DTYPE)], [], tr=TR, name="post_proj")
    mixed, lse_a = _dil_fwd(q, k, v)

    w_uq_p, w_ukv, w_out = attn_weights(cqn)

    def q_epi(acc, cb, sba, sbb):
        cols = []
        for hh in range(acc.shape[1] // QPAD):
            lo = hh * QPAD
            cols += [acc[:, lo:lo + HEAD], _rope(acc[:, lo + HEAD:lo + QPAD], cb, sba, sbb, ROPE_B // 2)]
        return (jnp.concatenate(cols, axis=1) * MLA_QSCALE,)
    (qf,) = mm(cqn, w_uq_p, dims="nn", out_dtypes=[MXU_DTYPE], name="q_up", epi=q_epi, row_extras=tuple(tabs[3:]))
    (kv,) = mm(ckvn, w_ukv, dims="nn", out_dtypes=[MXU_DTYPE], name="kv_up")
    mixed, lse_b = _mla_fwd(qf, kv, krope, mixed)
    (o,) = mm(mixed, w_out, dims="nn", out_dtypes=[F32], name="out_proj", after=mlp_prefetch(mixed))
    x1, h2 = _rowwise(_mid_body, [x, o], [g2, g3], [(D_MODEL, F32), (D_MODEL, MXU_DTYPE)], [], tr=TR, name="mid_norm")

    w_up, w_down = mlp_weights(h2)

    def up_epi(acc):
        r = jnp.maximum(acc, 0.0)
        return r * r, r
    u, r = mm(h2, w_up, dims="nn", out_dtypes=[MXU_DTYPE, MXU_DTYPE], name="mlp_up", epi=up_epi, b_shards=N_CHIPS)
    (dn,) = mm_k(u, w_down, dims="nn", out_dtypes=[F32], name="mlp_down")
    dy, dd, loss8, dg4 = _rowwise(_loss_body, [x1, dn, target], [g4], [(D_MODEL, F32), (D_MODEL, MXU_DTYPE)],
                                  [(8, HEAD), (8, D_MODEL)], tr=TR, name="loss_head")

    def dup_epi(acc, rr):
        return (acc * (2.0 * rr.astype(F32)),)
    (dup,) = mm(dd, w_down, dims="nt", out_dtypes=[MXU_DTYPE], name="d_up", epi=dup_epi, extras=(r,))
    (gw_down,) = mm_g(u, dd, dims="tn", out_dtypes=[WIRE_DTYPE], name="gw_down")
    (dh2,) = mm_k(dup, w_up, dims="nt", out_dtypes=[F32], name="d_h2", b_shards=N_CHIPS,
                  after=down_grad_ready(gw_down))
    (gw_up,) = mm_g(h2, dup, dims="tn", out_dtypes=[WIRE_DTYPE], name="gw_up", out_shards=N_CHIPS)
    g2 = g2 + up_grad_ready(gw_up)
    dx1, do, dg3, dg2 = _rowwise(_bmid_body, [dy, dh2, x1, o], [g2, g3], [(D_MODEL, F32), (D_MODEL, MXU_DTYPE)],
                                 [(8, D_MODEL), (8, D_MODEL)], tr=TR, name="bwd_mid")
    (dmix,) = mm(do, w_out, dims="nt", out_dtypes=[F32], name="d_mixed")
    (gw_out,) = mm_g(mixed, do, dims="tn", out_dtypes=[WIRE_DTYPE], name="gw_out")

    dq_pad, dkv, dkr = _mla_bwd(qf, kv, krope, dmix, mixed, lse_b, tabs[3:])
    (dcqn,) = mm(dq_pad, w_uq_p, dims="nt", out_dtypes=[F32], name="d_cq")
    (gw_uq_p,) = mm_g(cqn, dq_pad, dims="tn", out_dtypes=[WIRE_DTYPE], name="gw_uq")
    (dckvn,) = mm(dkv, w_ukv, dims="nt", out_dtypes=[F32], name="d_ckv")
    (gw_ukv,) = mm_g(ckvn, dkv, dims="tn", out_dtypes=[WIRE_DTYPE], name="gw_ukv")
    gq = gq + attn_grads_ready(gw_out, gw_uq_p, gw_ukv)

    dq_a, dk_a, dv_a = _dil_bwd(q, k, v, dmix, mixed, lse_a)
    dproj, dgq, dgkv = _rowwise(
        _dproj_body, [dq_a, dk_a, dv_a, dcqn, dckvn, proj, dkr] + tabs, [gq, gkv],
        [(PROJ_COLS, MXU_DTYPE)], [(8, LORA), (8, LORA)], tr=TR, name="d_proj")
    (dh,) = mm_k(dproj, w_proj, dims="nn", out_dtypes=[F32], tk=PROJ_TILE, name="d_h")
    (gw_proj,) = mm_g(dproj, h, dims="tn", out_dtypes=[WIRE_DTYPE], tm=PROJ_TILE, name="gw_in")
    dx, dg1 = _rowwise(_bin_body, [dx1, dh, x], [g1], [(D_MODEL, F32)], [(8, D_MODEL)], tr=TR, name="bwd_in")

    small = jnp.concatenate([dg1, dg2, dgq, dgkv, dg3, dg4, loss8], axis=1)
    return dx, gw_proj, small


def _place():
    x, y, c = lax.axis_index("x"), lax.axis_index("y"), lax.axis_index("c")
    chips = [(1 - x, y), (x, 1 - y), (1 - x, 1 - y)]
    return x, y, c, chips


def _cast_place_body(me_ref, w_ref, *rest):
    o_ref = rest[-1]
    o_ref[...] = w_ref[...].astype(o_ref.dtype)


def _cast_place(me_arr, w, name, after=None):
    rows, cols = w.shape
    tr = min(rows, 256)
    after = [] if after is None else [after]
    grid_spec = pltpu.PrefetchScalarGridSpec(
        num_scalar_prefetch=1, grid=(rows // tr,),
        in_specs=[pl.BlockSpec((tr, cols), lambda i, me: (i, 0))] + [ANY] * len(after),
        out_specs=pl.BlockSpec((None, tr, cols), lambda i, me: (me[0], i, 0)))
    return _pcall(
        _cast_place_body, name=name, grid_spec=grid_spec,
        out_shape=jax.ShapeDtypeStruct((N_CHIPS, rows, cols), WIRE_DTYPE),
        compiler_params=pltpu.CompilerParams(dimension_semantics=("parallel",)),
    )(me_arr, w, *after)


def _cast_place_t_body(me_ref, w_ref, o_ref, *, n):
    i = pl.program_id(0)

    @pl.when(i < n)
    def _():
        o_ref[...] = w_ref[...].astype(o_ref.dtype)

    @pl.when(i == n)
    def _():
        o_ref[...] = jnp.zeros_like(o_ref)


def _cast_place_t(me_arr, w_t, name):
    rows, cols = w_t.shape
    n = rows // IN_TR
    grid_spec = pltpu.PrefetchScalarGridSpec(
        num_scalar_prefetch=1, grid=(n + 1,),
        in_specs=[pl.BlockSpec((IN_TR, cols), lambda i, me: (jnp.minimum(i, n - 1), 0))],
        out_specs=pl.BlockSpec((IN_TR, cols), lambda i, me: (jnp.where(i < n, me[0] * n + i, N_CHIPS * n), 0)))
    return _pcall(
        functools.partial(_cast_place_t_body, n=n), name=name, grid_spec=grid_spec,
        out_shape=jax.ShapeDtypeStruct((PROJ_COLS, cols), WIRE_DTYPE),
        compiler_params=pltpu.CompilerParams(dimension_semantics=("arbitrary",)),
    )(me_arr, w_t)


HBM = pl.BlockSpec(memory_space=pltpu.HBM)
SEM = pl.BlockSpec(memory_space=pltpu.SEMAPHORE)
EFFECT = pltpu.SideEffectType.DATAFLOW_SIDE_EFFECTING


def _copy_start(make, arrays, after, name, n_sems):
    n_a = len(arrays)
    after = [] if after is None else [after]

    def body(*refs):
        for send, _ in make(refs[:n_a], refs[-n_a - 3], refs[-n_a - 2]):
            send.start()
        refs[-1][...] = jnp.zeros_like(refs[-1])

    res = _pcall(
        body, name=name,
        in_specs=[HBM] * n_a + [ANY] * len(after),
        out_specs=[SEM, SEM] + [HBM] * n_a + [pl.BlockSpec(memory_space=pltpu.VMEM)],
        out_shape=[pltpu.SemaphoreType.DMA((n_sems,)), pltpu.SemaphoreType.DMA((n_sems,))]
        + [pltpu.HBM(a.shape, a.dtype) for a in arrays] + [jax.ShapeDtypeStruct((8, HEAD), F32)],
        input_output_aliases={i: 2 + i for i in range(n_a)},
        compiler_params=pltpu.CompilerParams(has_side_effects=EFFECT),
    )(*[pltpu.with_memory_space_constraint(a, pltpu.HBM) for a in arrays], *after)
    return (res[0], res[1]), list(res[2:2 + n_a]), res[-1]


def _copy_wait(make, sems, arrays, after, name):
    n_a = len(arrays)
    after = list(after) if isinstance(after, (list, tuple)) else [after]

    def body(*refs):
        for send, recv in make(refs[:n_a], refs[n_a], refs[n_a + 1]):
            send.wait_send()
            recv.wait_recv()

    return list(_pcall(
        body, name=name,
        in_specs=[HBM] * n_a + [SEM, SEM] + [ANY] * len(after), out_specs=[HBM] * n_a,
        out_shape=[pltpu.HBM(a.shape, a.dtype) for a in arrays],
        input_output_aliases={i: i for i in range(n_a)},
        compiler_params=pltpu.CompilerParams(has_side_effects=EFFECT),
    )(*arrays, sems[0], sems[1], *after))


def _slot(buf, chip, half):
    if buf.ndim == 2:
        hc = buf.shape[1] // 2
        return buf.at[pl.ds(pl.multiple_of(chip * IN_SHARD, 16), IN_SHARD), pl.ds(pl.multiple_of(half * hc, HEAD), hc)]
    hr = buf.shape[1] // 2
    return buf.at[chip, pl.ds(pl.multiple_of(half * hr, 16), hr)]


def _ag_descs(bufs, send_sems, recv_sems):
    x, y, c, chips = _place()
    me = 2 * x + y
    out = []
    for w, buf in enumerate(bufs):
        for j, (px, py) in enumerate(chips):
            mk = lambda ref, w=w, j=j, px=px, py=py: pltpu.make_async_remote_copy(
                src_ref=ref, dst_ref=ref, send_sem=send_sems.at[w * 3 + j], recv_sem=recv_sems.at[w * 3 + j],
                device_id=(px, py, c), device_id_type=MESH)
            out.append((mk(_slot(buf, me, c)), mk(_slot(buf, 2 * px + py, c))))
    return out


def _fw_descs(bufs, send_sems, recv_sems):
    x, y, c, chips = _place()
    out = []
    for w, buf in enumerate(bufs):
        for j, (px, py) in enumerate(chips):
            def mk(which, w=w, j=j, buf=buf, px=px, py=py):
                ref = _slot(buf, 2 * px + py, which)
                return pltpu.make_async_remote_copy(
                    src_ref=ref, dst_ref=ref, send_sem=send_sems.at[w * 3 + j], recv_sem=recv_sems.at[w * 3 + j],
                    device_id=(x, y, 1 - c), device_id_type=MESH)
            out.append((mk(c), mk(1 - c)))
    return out


def _sc_descs(refs, send_sems, recv_sems):
    n_w = len(refs) // 2
    x, y, c, chips = _place()
    me = 2 * x + y
    out = []
    for w in range(n_w):
        for j, (px, py) in enumerate(chips):
            d = pltpu.make_async_remote_copy(
                src_ref=refs[w].at[2 * px + py], dst_ref=refs[n_w + w].at[me],
                send_sem=send_sems.at[w * 3 + j], recv_sem=recv_sems.at[w * 3 + j],
                device_id=(px, py, c), device_id_type=MESH)
            out.append((d, d))
    return out


def _pair_descs(src_of):
    def make(refs, send_sems, recv_sems):
        n_w = len(refs) // 2
        x, y, c, _ = _place()
        out = []
        for w in range(n_w):
            d = pltpu.make_async_remote_copy(
                src_ref=src_of(refs[w], c), dst_ref=refs[n_w + w],
                send_sem=send_sems.at[w], recv_sem=recv_sems.at[w],
                device_id=(x, y, 1 - c), device_id_type=MESH)
            out.append((d, d))
        return out
    return make


_EX_DESCS = _pair_descs(lambda g4, c: g4.at[:, 1 - c])
_SW_DESCS = _pair_descs(lambda half, c: half)
_EXT_DESCS = _pair_descs(lambda g, c: g.at[pl.ds(0, IN_COLS),
                                           pl.ds(pl.multiple_of((1 - c) * (D_MODEL // 2), HEAD), D_MODEL // 2)])


def _sm_descs(refs, send_sems, recv_sems):
    buf = refs[0]
    rows8 = buf.shape[0] // N_DEV
    x, y, c, _ = _place()
    flip = lambda v, d: 1 - v if d else v
    blk = lambda px, py, pc: buf.at[pl.ds(pl.multiple_of((4 * px + 2 * py + pc) * rows8, 8), rows8)]
    out = []
    for k in range(1, N_DEV):
        px, py, pc = flip(x, k & 4), flip(y, k & 2), flip(c, k & 1)
        mk = lambda ref, k=k, px=px, py=py, pc=pc: pltpu.make_async_remote_copy(
            src_ref=ref, dst_ref=ref, send_sem=send_sems.at[k - 1], recv_sem=recv_sems.at[k - 1],
            device_id=(px, py, pc), device_id_type=MESH)
        out.append((mk(blk(x, y, c)), mk(blk(px, py, pc))))
    return out


def _place_rows_body(i_ref, x_ref, o_ref):
    o_ref[...] = x_ref[...]


def _place_rows(i_arr, x, n_blocks, name):
    r, n = x.shape
    grid_spec = pltpu.PrefetchScalarGridSpec(
        num_scalar_prefetch=1, grid=(1,),
        in_specs=[pl.BlockSpec((r, n), lambda g, i: (0, 0))],
        out_specs=pl.BlockSpec((r, n), lambda g, i: (i[0], 0)))
    return _pcall(_place_rows_body, name=name, grid_spec=grid_spec,
                  out_shape=jax.ShapeDtypeStruct((n_blocks * r, n), x.dtype))(i_arr, x)


def _ag_forward_body(*refs, n_w):
    bufs = refs[n_w:2 * n_w]
    send_sems, recv_sems = refs[2 * n_w:]
    pairs = _fw_descs(bufs, send_sems, recv_sems)
    for fw, _ in pairs:
        fw.start()
    for fw, back in pairs:
        back.wait_recv()
        fw.wait_send()


def _ag_forward(bufs, tag):
    n_w = len(bufs)
    return list(_pcall(
        functools.partial(_ag_forward_body, n_w=n_w), name="weight_allgather_forward_" + tag,
        in_specs=[ANY] * n_w, out_specs=[ANY] * n_w,
        out_shape=[jax.ShapeDtypeStruct(b.shape, b.dtype) for b in bufs],
        input_output_aliases={w: w for w in range(n_w)},
        scratch_shapes=[pltpu.SemaphoreType.DMA((3 * n_w,))] * 2,
    )(*bufs))


def _pair_add_body(c_ref, mine_ref, theirs_ref, o_ref):
    o_ref[...] = (mine_ref[...].astype(F32) + theirs_ref[...].astype(F32)).astype(o_ref.dtype)


def _pair_add(c_arr, g4, recv, name):
    _, _, hr, cols = g4.shape
    tr = min(hr, 512)
    grid_spec = pltpu.PrefetchScalarGridSpec(
        num_scalar_prefetch=1, grid=(N_CHIPS, hr // tr),
        in_specs=[pl.BlockSpec((None, None, tr, cols), lambda s, i, c: (s, c[0], i, 0)),
                  pl.BlockSpec((None, tr, cols), lambda s, i, c: (s, i, 0))],
        out_specs=pl.BlockSpec((None, tr, cols), lambda s, i, c: (s, i, 0)))
    return _pcall(
        _pair_add_body, name=name, grid_spec=grid_spec,
        out_shape=jax.ShapeDtypeStruct(recv.shape, recv.dtype),
        compiler_params=pltpu.CompilerParams(dimension_semantics=("parallel", "parallel")),
    )(c_arr, g4, recv)


def _pair_add_t(c_arr, g, recv, name):
    rows, hc = recv.shape
    grid_spec = pltpu.PrefetchScalarGridSpec(
        num_scalar_prefetch=1, grid=(rows // IN_TR,),
        in_specs=[pl.BlockSpec((IN_TR, hc), lambda i, c: (i, c[0])), pl.BlockSpec((IN_TR, hc), lambda i, c: (i, 0))],
        out_specs=pl.BlockSpec((IN_TR, hc), lambda i, c: (i, 0)))
    return _pcall(
        _pair_add_body, name=name, grid_spec=grid_spec,
        out_shape=jax.ShapeDtypeStruct(recv.shape, recv.dtype),
        compiler_params=pltpu.CompilerParams(dimension_semantics=("parallel",)),
    )(c_arr, g, recv)


def _sum4_body(me_ref, p_ref, l0, l1, l2, l3, o_ref):
    me = me_ref[0]
    t = [jnp.where(me == j, p_ref[...], l[...]).astype(F32) for j, l in enumerate((l0, l1, l2, l3))]
    o_ref[...] = ((t[0] + t[1]) + t[2]) + t[3]


def _sum4(me_arr, part, landed, name):
    _, hr, cols = part.shape
    tr = IN_TR if hr == IN_SHARD else min(hr, 256)

    def slot(j):
        return lambda i, me: (jnp.where(me[0] == j, (j + 1) % N_CHIPS, j), i, 0)

    grid_spec = pltpu.PrefetchScalarGridSpec(
        num_scalar_prefetch=1, grid=(hr // tr,),
        in_specs=[pl.BlockSpec((None, tr, cols), lambda i, me: (me[0], i, 0))]
        + [pl.BlockSpec((None, tr, cols), slot(j)) for j in range(N_CHIPS)],
        out_specs=pl.BlockSpec((tr, cols), lambda i, me: (i, 0)))
    return _pcall(
        _sum4_body, name=name, grid_spec=grid_spec,
        out_shape=jax.ShapeDtypeStruct((hr, cols), F32),
        compiler_params=pltpu.CompilerParams(dimension_semantics=("parallel",)),
    )(me_arr, part, landed, landed, landed, landed)


def _adamw(w, g, m, v):
    m = ADAM_B1 * m + (1.0 - ADAM_B1) * g
    v = ADAM_B2 * v + (1.0 - ADAM_B2) * (g * g)
    m_hat = m / (1.0 - ADAM_B1 ** ADAM_STEP)
    v_hat = v / (1.0 - ADAM_B2 ** ADAM_STEP)
    delta = -ADAM_LR * (m_hat / (jnp.sqrt(v_hat) + ADAM_EPS) + ADAM_WD * w)
    return delta, m, v


def _adamw_half_body(h_ref, w_ref, g_in_ref, m_ref, v_ref, *rest):
    g_ref, d_ref, nm_ref, nv_ref, done_ref = rest[-5:]
    done_ref[...] = jnp.zeros_like(done_ref)
    g = g_in_ref[...]
    g_ref[...] = g
    d, m, v = _adamw(w_ref[...], g, m_ref[...], v_ref[...])
    d_ref[...] = d
    nm_ref[...] = m
    nv_ref[...] = v


def _adamw_half(h_arr, w, g_half, m, v, prev, name):
    rows, cols = w.shape
    if g_half.shape[0] == rows:
        tr, nh = IN_TR, rows // IN_TR
        at_half = pl.BlockSpec((tr, cols // 2), lambda i, h: (i, h[0]))
        g_spec = pl.BlockSpec((tr, cols // 2), lambda i, h: (i, 0))
    else:
        tr = min(rows // 2, 128)
        nh = (rows // 2) // tr
        at_half = pl.BlockSpec((tr, cols), lambda i, h: (h[0] * nh + i, 0))
        g_spec = pl.BlockSpec((tr, cols), lambda i, h: (i, 0))
    grid_spec = pltpu.PrefetchScalarGridSpec(
        num_scalar_prefetch=1, grid=(nh,),
        in_specs=[at_half, g_spec, at_half, at_half] + [ANY] * len(prev),
        out_specs=[at_half] * 4 + [pl.BlockSpec((8, HEAD), lambda i, h: (0, 0))])
    return list(_pcall(
        _adamw_half_body, name=name, grid_spec=grid_spec,
        out_shape=[jax.ShapeDtypeStruct(w.shape, F32)] * 4 + [jax.ShapeDtypeStruct((8, HEAD), F32)],
        input_output_aliases={5 + k: k for k in range(len(prev))},
        compiler_params=pltpu.CompilerParams(dimension_semantics=("arbitrary",)),
    )(h_arr, w, g_half, m, v, *prev))


def _small_update_body(gath_ref, w_ref, m_ref, v_ref, g_ref, d_ref, nm_ref, nv_ref, loss_ref, *, n_gain):
    tot = gath_ref[0:1, :]
    for i in range(1, gath_ref.shape[0]):
        tot = tot + gath_ref[i:i + 1, :]
    g = tot[:, 0:n_gain]
    g_ref[...] = g
    d, m, v = _adamw(w_ref[...], g, m_ref[...], v_ref[...])
    d_ref[...] = d
    nm_ref[...] = m
    nv_ref[...] = v
    loss_ref[...] = (0.5 / D_MODEL) * jnp.sum(tot[:, n_gain:n_gain + HEAD], axis=1, keepdims=True) * jnp.ones((1, HEAD), F32)


def _small_update(gath, w, m, v):
    n_gain = w.shape[1]
    vm = pl.BlockSpec(memory_space=pltpu.VMEM)
    return _pcall(
        functools.partial(_small_update_body, n_gain=n_gain), name="gain_update",
        in_specs=[vm] * 4, out_specs=[vm] * 5,
        out_shape=[jax.ShapeDtypeStruct((1, n_gain), F32)] * 4 + [jax.ShapeDtypeStruct((1, HEAD), F32)],
    )(gath, w, m, v)


def kernel(x, positions, norm_attn_pre, norm_attn_post, w_in, q_latent_norm, kv_latent_norm, w_uq, w_ukv, w_out, norm_mlp_pre, norm_mlp_post, w_up, w_down, loss_target, m_norm_attn_pre, m_norm_attn_post, m_w_in, m_q_latent_norm, m_kv_latent_norm, m_w_uq, m_w_ukv, m_w_out, m_norm_mlp_pre, m_norm_mlp_post, m_w_up, m_w_down, v_norm_attn_pre, v_norm_attn_post, v_w_in, v_q_latent_norm, v_kv_latent_norm, v_w_uq, v_w_ukv, v_w_out, v_norm_mlp_pre, v_norm_mlp_post, v_w_up, v_w_down):
    T = x.shape[1]
    c_arr = lax.axis_index("c").astype(jnp.int32).reshape(1)
    me_arr = (2 * lax.axis_index("x") + lax.axis_index("y")).astype(jnp.int32).reshape(1)
    names = ["w_in", "w_uq", "w_ukv", "w_out", "w_up", "w_down"]

    transposed = lambda a: jnp.swapaxes(a, 1, 2)
    mats = [transposed(w_in)[0], w_uq[0], w_ukv[0], w_out[0], w_up[0], w_down[0]]
    me8_arr = (4 * lax.axis_index("x") + 2 * lax.axis_index("y") + lax.axis_index("c")).astype(jnp.int32).reshape(1)
    col_major = lambda g: jnp.transpose(g, (1, 0, 2)).reshape(g.shape[1], N_CHIPS * g.shape[2])
    cast = lambda a: a.astype(MXU_DTYPE)
    to_shards = lambda g: jnp.transpose(g.reshape(g.shape[0], N_CHIPS, g.shape[1] // N_CHIPS), (1, 0, 2))
    halved = lambda g: g.reshape(N_CHIPS, 2, g.shape[1] // 2, g.shape[2])
    empty = lambda a, shape=None: lax.empty(a.shape if shape is None else shape, a.dtype)

    sem_in, buf_in, going = _copy_start(_ag_descs, [_cast_place_t(me_arr, mats[0], "cast_w_in")], None,
                                        "weight_allgather_start_in", 3)
    placed = [_cast_place(me_arr, w, "cast_" + n, going) for w, n in zip(mats[1:], names[1:])]
    sem_att, buf_att, going = _copy_start(_ag_descs, placed[:3], going, "weight_allgather_start_attn", 9)
    sem_mlp, buf_mlp, started = _copy_start(_ag_descs, placed[3:], going, "weight_allgather_start_mlp", 6)

    going_on = {}

    def in_weights(after):
        (win_g,) = _ag_forward(_copy_wait(_ag_descs, sem_in, buf_in, after, "weight_allgather_wait_in"), "in")
        return cast(win_g)

    def attn_prefetch(after):
        landed = _copy_wait(_ag_descs, sem_att, buf_att, after, "weight_allgather_wait_attn")
        going_on["fw_attn"] = _copy_start(_fw_descs, landed, None, "weight_allgather_forward_start_attn", 9)
        return going_on["fw_attn"][-1][0:1, 0:1]

    def attn_weights(after):
        sems, bufs, _ = going_on["fw_attn"]
        wuq_g, wukv_g, wout_g = _copy_wait(_fw_descs, sems, bufs, after, "weight_allgather_forward_wait_attn")
        wuq_full = col_major(wuq_g).reshape(LORA, NH, HEAD + ROPE_B)
        w_uq_p = jnp.pad(wuq_full, ((0, 0), (0, 0), (0, QPAD - HEAD - ROPE_B))).reshape(LORA, NH * QPAD)
        return cast(w_uq_p), cast(col_major(wukv_g)), cast(wout_g.reshape(2 * A_W, D_MODEL))

    def mlp_prefetch(after):
        landed = _copy_wait(_ag_descs, sem_mlp, buf_mlp, after, "weight_allgather_wait_mlp")
        going_on["fw"] = _copy_start(_fw_descs, landed, None, "weight_allgather_forward_start_mlp", 6)
        return going_on["fw"][-1]

    def mlp_weights(after):
        sems, bufs, _ = going_on["fw"]
        wup_g, wdown_g = _copy_wait(_fw_descs, sems, bufs, after, "weight_allgather_forward_wait_mlp")
        return cast(wup_g), cast(wdown_g.reshape(D_FF, D_MODEL))

    def exchange_start(g4s, tag):
        lands = [empty(g, (g.shape[0],) + g.shape[2:]) for g in g4s]
        return _copy_start(_EX_DESCS, g4s + lands, None, "grad_pair_exchange_start_" + tag, len(g4s))

    def exchange_finish(started_ex, after, ns, tag):
        sems, arrs, _ = started_ex
        arrs = _copy_wait(_EX_DESCS, sems, arrs, after, "grad_pair_exchange_wait_" + tag)
        n = len(ns)
        return [_pair_add(c_arr, g4, r, "pair_add_" + nm) for g4, r, nm in zip(arrs[:n], arrs[n:], ns)]

    def scatter_start(parts, after, tag):
        return _copy_start(_sc_descs, parts + [empty(p) for p in parts], after, "grad_scatter_start_" + tag,
                           3 * len(parts))

    def scatter_finish(started_sc, after, tag):
        sems, arrs, _ = started_sc
        arrs = _copy_wait(_sc_descs, sems, arrs, after, "grad_scatter_wait_" + tag)
        return arrs[:len(arrs) // 2], arrs[len(arrs) // 2:]

    def down_grad_ready(gw_down):
        going_on["x_down"] = exchange_start([halved(gw_down.reshape(N_CHIPS, D_MODEL, D_MODEL))], "down")
        return going_on["x_down"][-1]

    def up_grad_ready(gw_up):
        going_on["x_up"] = exchange_start([halved(gw_up)], "up")
        parts = exchange_finish(going_on["x_down"], going_on["x_up"][-1], names[5:], "down")
        going_on["s_down"] = scatter_start(parts, started, "down")
        return going_on["s_down"][-1][0:1, 0:1]

    def attn_grads_ready(gw_out, gw_uq_p, gw_ukv):
        gw_uq = to_shards(gw_uq_p.reshape(LORA, NH, QPAD)[:, :, :HEAD + ROPE_B].reshape(LORA, NH * (HEAD + ROPE_B)))
        full4 = [halved(g) for g in (gw_uq, to_shards(gw_ukv), gw_out.reshape(N_CHIPS, LORA, D_MODEL))]
        x_attn = exchange_start(full4, "attn")
        parts_up = exchange_finish(going_on["x_up"], x_attn[-1], names[4:5], "up")
        parts = exchange_finish(x_attn, parts_up[0], names[1:4], "attn") + parts_up
        going_on["s_rest"] = scatter_start(parts, going_on["s_down"][-1], "attn_up")
        return going_on["s_rest"][-1][0:1, 0:1]

    dx, gw_proj, small = _local_step(
        x[0], positions[0].astype(F32).reshape(T, 1), loss_target[0],
        norm_attn_pre + started[0:1, 0:1], norm_attn_post, q_latent_norm, kv_latent_norm, norm_mlp_pre, norm_mlp_post,
        in_weights, attn_prefetch, attn_weights, mlp_prefetch, mlp_weights,
        down_grad_ready, up_grad_ready, attn_grads_ready)

    ms = [transposed(m_w_in)[0], m_w_uq[0], m_w_ukv[0], m_w_out[0], m_w_up[0], m_w_down[0]]
    vs = [transposed(v_w_in)[0], v_w_uq[0], v_w_ukv[0], v_w_out[0], v_w_up[0], v_w_down[0]]
    sib_arr = 1 - c_arr

    def finish(parts, landed, lo, hi, tag):
        sl = slice(lo, hi)
        halves = [_sum4(me_arr, p, l, "chip_sum_" + n) for p, l, n in zip(parts, landed, names[sl])]
        n = len(halves)
        sems, arrs, _ = _copy_start(_SW_DESCS, halves + [empty(h) for h in halves], None,
                                    "grad_pair_swap_start_" + tag, n)
        own = [_adamw_half(c_arr, w, g, m, v, [], "adamw_own_" + nm)
               for w, g, m, v, nm in zip(mats[sl], arrs[:n], ms[sl], vs[sl], names[sl])]
        arrs = _copy_wait(_SW_DESCS, sems, arrs, own[-1][4], "grad_pair_swap_wait_" + tag)
        return [_adamw_half(sib_arr, w, g, m, v, prev[:4], "adamw_sib_" + nm)
                for w, g, m, v, prev, nm in zip(mats[sl], arrs[n:], ms[sl], vs[sl], own, names[sl])]

    sem_small, (gath,), small_going = _copy_start(
        _sm_descs, [_place_rows(me8_arr, small, N_DEV, "place_small")], None, "small_allgather_start", N_DEV - 1)
    sems, arrs, _ = _copy_start(_EXT_DESCS, [gw_proj, lax.empty((IN_COLS, D_MODEL // 2), WIRE_DTYPE)], None,
                                "grad_pair_exchange_start_in", 1)
    gw_proj, from_sib = _copy_wait(_EXT_DESCS, sems, arrs, small_going, "grad_pair_exchange_wait_in")
    part_in = _pair_add_t(c_arr, gw_proj, from_sib, "pair_add_w_in").reshape(N_CHIPS, IN_SHARD, D_MODEL // 2)
    s_in = scatter_start([part_in], None, "in")
    parts_rest, landed_rest = scatter_finish(going_on["s_rest"], s_in[-1], "attn_up")
    parts_down, landed_down = scatter_finish(going_on["s_down"], landed_rest[0], "down")
    upd_rest = finish(parts_rest + parts_down, landed_rest + landed_down, 1, 6, "rest")
    parts_in, landed_in = scatter_finish(s_in, upd_rest[-1][0], "in")
    upd = finish(parts_in, landed_in, 0, 1, "in") + upd_rest
    grads = [u[0] for u in upd]

    (gath,) = _copy_wait(_sm_descs, sem_small, [gath], grads[0], "small_allgather_wait")
    gains = [norm_attn_pre, norm_attn_post, q_latent_norm, kv_latent_norm, norm_mlp_pre, norm_mlp_post]
    gm = [m_norm_attn_pre, m_norm_attn_post, m_q_latent_norm, m_kv_latent_norm, m_norm_mlp_pre, m_norm_mlp_post]
    gv = [v_norm_attn_pre, v_norm_attn_post, v_q_latent_norm, v_kv_latent_norm, v_norm_mlp_pre, v_norm_mlp_post]
    cat = lambda xs: jnp.concatenate(xs, axis=1)
    g_s, d_s, m_s, v_s, loss_v = _small_update(gath, cat(gains), cat(gm), cat(gv))
    widths = [a.shape[1] for a in gains]
    offs = [sum(widths[:i]) for i in range(len(widths))]
    split = lambda a: [a[:, o:o + w] for o, w in zip(offs, widths)]
    g_gain, d_gain, m_gain, v_gain = split(g_s), split(d_s), split(m_s), split(v_s)

    def ordered(gain_list, mat_list):
        gl, ml = gain_list, [transposed(mat_list[0][None])] + [a[None] for a in mat_list[1:]]
        return [gl[0], gl[1], ml[0], gl[2], gl[3], ml[1], ml[2], ml[3], gl[4], gl[5], ml[4], ml[5]]

    loss = loss_v[0, 0]
    return (loss, dx[None],
            *ordered(g_gain, grads),
            *ordered(d_gain, [u[1] for u in upd]),
            *ordered(m_gain, [u[2] for u in upd]),
            *ordered(v_gain, [u[3] for u in upd]))
```
